```python
import jax, jax.numpy as jnp
from jax import lax
import numpy as np

D_MODEL = 2048
BATCH = 8
SEQ = 2048
DEPTH = 1

CONV_CH = 1024
CONV_K = 31
N_HEADS = 8
QK_NOPE = 128
QK_ROPE = 64
V_HEAD = 128
QK_HEAD = QK_NOPE + QK_ROPE
Q_LORA = 768
KV_LORA = 512
ATTN_CH = N_HEADS * V_HEAD
MIX_WIDTH = CONV_CH + ATTN_CH
IN_COLS = 2 * CONV_CH + Q_LORA + KV_LORA + QK_ROPE
ROPE_THETA = 10000.0
Q_BLOCK = 128
D_FF = ((8 * D_MODEL // 3 + 255) // 256) * 256
EPS = 1e-6

kernel_name = "hymba_conformer_mla_sandwich_layer"


def rmsnorm(x, g):
    xf = x.astype(jnp.float32)
    y = xf * lax.rsqrt(jnp.mean(xf * xf, axis=-1, keepdims=True) + EPS)
    return (y * g.astype(jnp.float32)).astype(x.dtype)


def layernorm(x, g, b):
    xf = x.astype(jnp.float32)
    mu = jnp.mean(xf, axis=-1, keepdims=True)
    var = jnp.mean(jnp.square(xf - mu), axis=-1, keepdims=True)
    y = (xf - mu) * lax.rsqrt(var + EPS)
    return (y * g.astype(jnp.float32) + b.astype(jnp.float32)).astype(x.dtype)


def rope_tables(positions, dtype):
    inv_freq = ROPE_THETA ** (-jnp.arange(0, QK_ROPE, 2, dtype=jnp.float32) / QK_ROPE)
    ang = positions.astype(jnp.float32)[..., None] * inv_freq
    return jnp.cos(ang).astype(dtype), jnp.sin(ang).astype(dtype)


def apply_rope(x, cos, sin):
    x1, x2 = jnp.split(x, 2, axis=-1)
    return jnp.concatenate([x1 * cos - x2 * sin, x2 * cos + x1 * sin], axis=-1)


def causal_depthwise_conv(u, w, b):
    y = lax.conv_general_dilated(
        u, w[:, None, :], window_strides=(1,), padding=[(CONV_K - 1, 0)],
        dimension_numbers=("NWC", "WIO", "NWC"), feature_group_count=u.shape[-1])
    return y + b


def causal_attention(q, k, v):
    B, S, H, Dq = q.shape
    nblk = S // Q_BLOCK
    qb = q.reshape(B, nblk, Q_BLOCK, H, Dq).transpose(1, 0, 2, 3, 4)
    kpos = jnp.arange(S)
    scale = Dq ** -0.5
    neg = jnp.finfo(jnp.float32).min

    def one_block(args):
        i, qi = args
        s = jnp.einsum('bqhd,bkhd->bhqk', qi, k).astype(jnp.float32) * scale
        qpos = i * Q_BLOCK + jnp.arange(Q_BLOCK)
        s = jnp.where(kpos[None, :] <= qpos[:, None], s, neg)
        p = jax.nn.softmax(s, axis=-1).astype(v.dtype)
        return jnp.einsum('bhqk,bkhd->bqhd', p, v)

    out = lax.map(one_block, (jnp.arange(nblk), qb))
    return out.transpose(1, 0, 2, 3, 4).reshape(B, S, H * v.shape[-1])


def _fwd_setup_inputs(seed: int = 0) -> dict:
    key = jax.random.key(seed)
    ks = jax.random.split(key, 24)
    f = jnp.float32
    L = DEPTH

    def w(k, shape, fan_in):
        return jax.random.normal(k, shape, f) * (fan_in ** -0.5)

    def gain(k, n):
        return jnp.ones((L, n), f) + 0.05 * jax.random.normal(k, (L, n), f)

    x = jax.random.normal(ks[0], (BATCH, SEQ, D_MODEL), f)
    offset = jax.random.randint(ks[1], (BATCH, 1), 0, 1024, dtype=jnp.int32)
    positions = offset + jnp.arange(SEQ, dtype=jnp.int32)[None, :]
    return {
        "x": x,
        "positions": positions,
        "pre_mix_norm": gain(ks[2], D_MODEL),
        "w_in": w(ks[3], (L, D_MODEL, IN_COLS), D_MODEL),
        "q_norm": gain(ks[4], Q_LORA),
        "w_uq": w(ks[5], (L, Q_LORA, N_HEADS * QK_HEAD), Q_LORA),
        "kv_norm": gain(ks[6], KV_LORA),
        "w_ukv": w(ks[7], (L, KV_LORA, N_HEADS * (QK_NOPE + V_HEAD)), KV_LORA),
        "conv_w": w(ks[8], (L, CONV_K, CONV_CH), CONV_K),
        "conv_b": 0.02 * jax.random.normal(ks[9], (L, CONV_CH), f),
        "conv_ln_g": gain(ks[10], CONV_CH),
        "conv_ln_b": 0.02 * jax.random.normal(ks[11], (L, CONV_CH), f),
        "conv_out_norm": gain(ks[12], CONV_CH),
        "attn_out_norm": gain(ks[13], ATTN_CH),
        "w_out": w(ks[14], (L, MIX_WIDTH, D_MODEL), MIX_WIDTH),
        "post_mix_norm": gain(ks[15], D_MODEL),
        "pre_ffn_norm": gain(ks[16], D_MODEL),
        "w_gate": w(ks[17], (L, D_MODEL, D_FF), D_MODEL),
        "w_up": w(ks[18], (L, D_MODEL, D_FF), D_MODEL),
        "w_down": w(ks[19], (L, D_FF, D_MODEL), D_FF),
        "post_ffn_norm": gain(ks[20], D_MODEL),
    }


def _fwd_reference(x, positions, pre_mix_norm, w_in, q_norm, w_uq, kv_norm, w_ukv,
              conv_w, conv_b, conv_ln_g, conv_ln_b, conv_out_norm, attn_out_norm,
              w_out, post_mix_norm, pre_ffn_norm, w_gate, w_up, w_down, post_ffn_norm):
    B, S, _ = x.shape
    cos, sin = rope_tables(positions, x.dtype)
    c1 = 2 * CONV_CH
    c2 = c1 + Q_LORA
    c3 = c2 + KV_LORA
    for l in range(DEPTH):
        h = rmsnorm(x, pre_mix_norm[l])
        z = h @ w_in[l]
        conv_in, q_lat, kv_lat, k_rope = z[..., :c1], z[..., c1:c2], z[..., c2:c3], z[..., c3:]

        a, g = jnp.split(conv_in, 2, axis=-1)
        u = a * jax.nn.sigmoid(g)
        u = causal_depthwise_conv(u, conv_w[l], conv_b[l])
        u = jax.nn.silu(layernorm(u, conv_ln_g[l], conv_ln_b[l]))

        q = (rmsnorm(q_lat, q_norm[l]) @ w_uq[l]).reshape(B, S, N_HEADS, QK_HEAD)
        q_nope, q_pe = q[..., :QK_NOPE], q[..., QK_NOPE:]
        q_pe = apply_rope(q_pe, cos[:, :, None, :], sin[:, :, None, :])
        kv = (rmsnorm(kv_lat, kv_norm[l]) @ w_ukv[l]).reshape(B, S, N_HEADS, QK_NOPE + V_HEAD)
        k_nope, v = kv[..., :QK_NOPE], kv[..., QK_NOPE:]
        k_pe = apply_rope(k_rope, cos, sin)
        k_pe = jnp.broadcast_to(k_pe[:, :, None, :], (B, S, N_HEADS, QK_ROPE))
        q_full = jnp.concatenate([q_nope, q_pe], axis=-1)
        k_full = jnp.concatenate([k_nope, k_pe], axis=-1)
        attn = causal_attention(q_full, k_full, v)

        mix = jnp.concatenate([rmsnorm(u, conv_out_norm[l]),
                               rmsnorm(attn, attn_out_norm[l])], axis=-1) @ w_out[l]
        x = x + rmsnorm(mix, post_mix_norm[l])

        hf = rmsnorm(x, pre_ffn_norm[l])
        ff = (jax.nn.silu(hf @ w_gate[l]) * (hf @ w_up[l])) @ w_down[l]
        x = x + rmsnorm(ff, post_ffn_norm[l])
    return x


import jax as _jax
import jax.numpy as _jnp

TWIN_FORMAT = 'train_step'
FWD_PARAMS = ['x', 'positions', 'pre_mix_norm', 'w_in', 'q_norm', 'w_uq', 'kv_norm', 'w_ukv', 'conv_w', 'conv_b', 'conv_ln_g', 'conv_ln_b', 'conv_out_norm', 'attn_out_norm', 'w_out', 'post_mix_norm', 'pre_ffn_norm', 'w_gate', 'w_up', 'w_down', 'post_ffn_norm']
TWIN_WEIGHTS = ['pre_mix_norm', 'w_in', 'q_norm', 'w_uq', 'kv_norm', 'w_ukv', 'conv_w', 'conv_b', 'conv_ln_g', 'conv_ln_b', 'conv_out_norm', 'attn_out_norm', 'w_out', 'post_mix_norm', 'pre_ffn_norm', 'w_gate', 'w_up', 'w_down', 'post_ffn_norm']
TWIN_DIFF_INPUT = 'x'
TWIN_INPUTS = ['x', 'positions', 'pre_mix_norm', 'w_in', 'q_norm', 'w_uq', 'kv_norm', 'w_ukv', 'conv_w', 'conv_b', 'conv_ln_g', 'conv_ln_b', 'conv_out_norm', 'attn_out_norm', 'w_out', 'post_mix_norm', 'pre_ffn_norm', 'w_gate', 'w_up', 'w_down', 'post_ffn_norm', 'loss_target', 'm_pre_mix_norm', 'm_w_in', 'm_q_norm', 'm_w_uq', 'm_kv_norm', 'm_w_ukv', 'm_conv_w', 'm_conv_b', 'm_conv_ln_g', 'm_conv_ln_b', 'm_conv_out_norm', 'm_attn_out_norm', 'm_w_out', 'm_post_mix_norm', 'm_pre_ffn_norm', 'm_w_gate', 'm_w_up', 'm_w_down', 'm_post_ffn_norm', 'v_pre_mix_norm', 'v_w_in', 'v_q_norm', 'v_w_uq', 'v_kv_norm', 'v_w_ukv', 'v_conv_w', 'v_conv_b', 'v_conv_ln_g', 'v_conv_ln_b', 'v_conv_out_norm', 'v_attn_out_norm', 'v_w_out', 'v_post_mix_norm', 'v_pre_ffn_norm', 'v_w_gate', 'v_w_up', 'v_w_down', 'v_post_ffn_norm']
TWIN_OUTPUTS = ['loss', 'grad_x', 'grad_pre_mix_norm', 'grad_w_in', 'grad_q_norm', 'grad_w_uq', 'grad_kv_norm', 'grad_w_ukv', 'grad_conv_w', 'grad_conv_b', 'grad_conv_ln_g', 'grad_conv_ln_b', 'grad_conv_out_norm', 'grad_attn_out_norm', 'grad_w_out', 'grad_post_mix_norm', 'grad_pre_ffn_norm', 'grad_w_gate', 'grad_w_up', 'grad_w_down', 'grad_post_ffn_norm', 'delta_pre_mix_norm', 'delta_w_in', 'delta_q_norm', 'delta_w_uq', 'delta_kv_norm', 'delta_w_ukv', 'delta_conv_w', 'delta_conv_b', 'delta_conv_ln_g', 'delta_conv_ln_b', 'delta_conv_out_norm', 'delta_attn_out_norm', 'delta_w_out', 'delta_post_mix_norm', 'delta_pre_ffn_norm', 'delta_w_gate', 'delta_w_up', 'delta_w_down', 'delta_post_ffn_norm', 'new_m_pre_mix_norm', 'new_m_w_in', 'new_m_q_norm', 'new_m_w_uq', 'new_m_kv_norm', 'new_m_w_ukv', 'new_m_conv_w', 'new_m_conv_b', 'new_m_conv_ln_g', 'new_m_conv_ln_b', 'new_m_conv_out_norm', 'new_m_attn_out_norm', 'new_m_w_out', 'new_m_post_mix_norm', 'new_m_pre_ffn_norm', 'new_m_w_gate', 'new_m_w_up', 'new_m_w_down', 'new_m_post_ffn_norm', 'new_v_pre_mix_norm', 'new_v_w_in', 'new_v_q_norm', 'new_v_w_uq', 'new_v_kv_norm', 'new_v_w_ukv', 'new_v_conv_w', 'new_v_conv_b', 'new_v_conv_ln_g', 'new_v_conv_ln_b', 'new_v_conv_out_norm', 'new_v_attn_out_norm', 'new_v_w_out', 'new_v_post_mix_norm', 'new_v_pre_ffn_norm', 'new_v_w_gate', 'new_v_w_up', 'new_v_w_down', 'new_v_post_ffn_norm']
TWIN_LEAF_KINDS = {'loss': 'loss', 'grad_x': 'grad_x', 'grad_pre_mix_norm': 'grad_w', 'grad_w_in': 'grad_w', 'grad_q_norm': 'grad_w', 'grad_w_uq': 'grad_w', 'grad_kv_norm': 'grad_w', 'grad_w_ukv': 'grad_w', 'grad_conv_w': 'grad_w', 'grad_conv_b': 'grad_w', 'grad_conv_ln_g': 'grad_w', 'grad_conv_ln_b': 'grad_w', 'grad_conv_out_norm': 'grad_w', 'grad_attn_out_norm': 'grad_w', 'grad_w_out': 'grad_w', 'grad_post_mix_norm': 'grad_w', 'grad_pre_ffn_norm': 'grad_w', 'grad_w_gate': 'grad_w', 'grad_w_up': 'grad_w', 'grad_w_down': 'grad_w', 'grad_post_ffn_norm': 'grad_w', 'delta_pre_mix_norm': 'delta_w', 'delta_w_in': 'delta_w', 'delta_q_norm': 'delta_w', 'delta_w_uq': 'delta_w', 'delta_kv_norm': 'delta_w', 'delta_w_ukv': 'delta_w', 'delta_conv_w': 'delta_w', 'delta_conv_b': 'delta_w', 'delta_conv_ln_g': 'delta_w', 'delta_conv_ln_b': 'delta_w', 'delta_conv_out_norm': 'delta_w', 'delta_attn_out_norm': 'delta_w', 'delta_w_out': 'delta_w', 'delta_post_mix_norm': 'delta_w', 'delta_pre_ffn_norm': 'delta_w', 'delta_w_gate': 'delta_w', 'delta_w_up': 'delta_w', 'delta_w_down': 'delta_w', 'delta_post_ffn_norm': 'delta_w', 'new_m_pre_mix_norm': 'new_m', 'new_m_w_in': 'new_m', 'new_m_q_norm': 'new_m', 'new_m_w_uq': 'new_m', 'new_m_kv_norm': 'new_m', 'new_m_w_ukv': 'new_m', 'new_m_conv_w': 'new_m', 'new_m_conv_b': 'new_m', 'new_m_conv_ln_g': 'new_m', 'new_m_conv_ln_b': 'new_m', 'new_m_conv_out_norm': 'new_m', 'new_m_attn_out_norm': 'new_m', 'new_m_w_out': 'new_m', 'new_m_post_mix_norm': 'new_m', 'new_m_pre_ffn_norm': 'new_m', 'new_m_w_gate': 'new_m', 'new_m_w_up': 'new_m', 'new_m_w_down': 'new_m', 'new_m_post_ffn_norm': 'new_m', 'new_v_pre_mix_norm': 'new_v', 'new_v_w_in': 'new_v', 'new_v_q_norm': 'new_v', 'new_v_w_uq': 'new_v', 'new_v_kv_norm': 'new_v', 'new_v_w_ukv': 'new_v', 'new_v_conv_w': 'new_v', 'new_v_conv_b': 'new_v', 'new_v_conv_ln_g': 'new_v', 'new_v_conv_ln_b': 'new_v', 'new_v_conv_out_norm': 'new_v', 'new_v_attn_out_norm': 'new_v', 'new_v_w_out': 'new_v', 'new_v_post_mix_norm': 'new_v', 'new_v_pre_ffn_norm': 'new_v', 'new_v_w_gate': 'new_v', 'new_v_w_up': 'new_v', 'new_v_w_down': 'new_v', 'new_v_post_ffn_norm': 'new_v'}


def _forward(args):
    return _fwd_reference(*[args[k] for k in FWD_PARAMS])


def _output_shape():
    out = _jax.eval_shape(lambda: _forward(_fwd_setup_inputs(0)))
    return out.shape, out.dtype

N_MICROBATCH = 1
ADAM_LR = 0.001
ADAM_B1 = 0.9
ADAM_B2 = 0.999
ADAM_EPS = 1e-08
ADAM_WD = 0.01
ADAM_STEP = 10
PER_EXAMPLE_BATCH_AXIS = {'x': 0, 'positions': 0, 'loss_target': 0}
SHARED_INPUTS = []
_WEIGHT_DTYPES = {'pre_mix_norm': _jnp.float32, 'w_in': _jnp.float32, 'q_norm': _jnp.float32, 'w_uq': _jnp.float32, 'kv_norm': _jnp.float32, 'w_ukv': _jnp.float32, 'conv_w': _jnp.float32, 'conv_b': _jnp.float32, 'conv_ln_g': _jnp.float32, 'conv_ln_b': _jnp.float32, 'conv_out_norm': _jnp.float32, 'attn_out_norm': _jnp.float32, 'w_out': _jnp.float32, 'post_mix_norm': _jnp.float32, 'pre_ffn_norm': _jnp.float32, 'w_gate': _jnp.float32, 'w_up': _jnp.float32, 'w_down': _jnp.float32, 'post_ffn_norm': _jnp.float32}
MOMENT_SCALE = {'pre_mix_norm': 2.814750e-01, 'w_in': 2.114406e-01, 'q_norm': 2.148032e-01, 'w_uq': 1.447088e-01, 'kv_norm': 5.464205e-01, 'w_ukv': 2.263217e-01, 'conv_w': 1.773381e-01, 'conv_b': 1.228283e+00, 'conv_ln_g': 4.750349e-01, 'conv_ln_b': 6.935987e-01, 'conv_out_norm': 2.898948e-01, 'attn_out_norm': 3.055614e-01, 'w_out': 2.973472e-01, 'post_mix_norm': 8.040874e+00, 'pre_ffn_norm': 2.759355e-01, 'w_gate': 8.937290e-02, 'w_up': 1.259652e-01, 'w_down': 2.087112e-01, 'post_ffn_norm': 8.049944e+00}


def _to_microbatches(a, axis):
    t = _jnp.moveaxis(a, axis, 0)
    t = t.reshape((N_MICROBATCH, t.shape[0] // N_MICROBATCH) + t.shape[1:])
    return _jnp.moveaxis(t, 1, axis + 1)


def setup_inputs(seed: int = 0) -> dict:
    inp = _fwd_setup_inputs(seed)
    key = _jax.random.fold_in(_jax.random.key(seed), 7919)
    shape, _ = _output_shape()
    out = dict(inp)
    out["loss_target"] = _jax.random.normal(_jax.random.fold_in(key, 0), shape, _jnp.float32)
    for i, name in enumerate(TWIN_WEIGHTS):
        w = inp[name].astype(_jnp.float32)
        if MOMENT_SCALE is None:
            s = _jnp.sqrt(_jnp.mean(_jnp.square(w)) + 1e-30)
        else:
            s = MOMENT_SCALE[name]
        km, kv = _jax.random.split(_jax.random.fold_in(key, i + 1))
        out[name] = w
        out["m_" + name] = s * _jax.random.normal(km, w.shape, _jnp.float32)
        out["v_" + name] = (s * s) * _jax.random.uniform(kv, w.shape, _jnp.float32, 0.5, 1.5)
    if N_MICROBATCH > 1:
        for name, axis in PER_EXAMPLE_BATCH_AXIS.items():
            out[name] = _to_microbatches(out[name], axis)
    return {'x': out['x'], 'positions': out['positions'], 'pre_mix_norm': out['pre_mix_norm'], 'w_in': out['w_in'], 'q_norm': out['q_norm'], 'w_uq': out['w_uq'], 'kv_norm': out['kv_norm'], 'w_ukv': out['w_ukv'], 'conv_w': out['conv_w'], 'conv_b': out['conv_b'], 'conv_ln_g': out['conv_ln_g'], 'conv_ln_b': out['conv_ln_b'], 'conv_out_norm': out['conv_out_norm'], 'attn_out_norm': out['attn_out_norm'], 'w_out': out['w_out'], 'post_mix_norm': out['post_mix_norm'], 'pre_ffn_norm': out['pre_ffn_norm'], 'w_gate': out['w_gate'], 'w_up': out['w_up'], 'w_down': out['w_down'], 'post_ffn_norm': out['post_ffn_norm'], 'loss_target': out['loss_target'], 'm_pre_mix_norm': out['m_pre_mix_norm'], 'm_w_in': out['m_w_in'], 'm_q_norm': out['m_q_norm'], 'm_w_uq': out['m_w_uq'], 'm_kv_norm': out['m_kv_norm'], 'm_w_ukv': out['m_w_ukv'], 'm_conv_w': out['m_conv_w'], 'm_conv_b': out['m_conv_b'], 'm_conv_ln_g': out['m_conv_ln_g'], 'm_conv_ln_b': out['m_conv_ln_b'], 'm_conv_out_norm': out['m_conv_out_norm'], 'm_attn_out_norm': out['m_attn_out_norm'], 'm_w_out': out['m_w_out'], 'm_post_mix_norm': out['m_post_mix_norm'], 'm_pre_ffn_norm': out['m_pre_ffn_norm'], 'm_w_gate': out['m_w_gate'], 'm_w_up': out['m_w_up'], 'm_w_down': out['m_w_down'], 'm_post_ffn_norm': out['m_post_ffn_norm'], 'v_pre_mix_norm': out['v_pre_mix_norm'], 'v_w_in': out['v_w_in'], 'v_q_norm': out['v_q_norm'], 'v_w_uq': out['v_w_uq'], 'v_kv_norm': out['v_kv_norm'], 'v_w_ukv': out['v_w_ukv'], 'v_conv_w': out['v_conv_w'], 'v_conv_b': out['v_conv_b'], 'v_conv_ln_g': out['v_conv_ln_g'], 'v_conv_ln_b': out['v_conv_ln_b'], 'v_conv_out_norm': out['v_conv_out_norm'], 'v_attn_out_norm': out['v_attn_out_norm'], 'v_w_out': out['v_w_out'], 'v_post_mix_norm': out['v_post_mix_norm'], 'v_pre_ffn_norm': out['v_pre_ffn_norm'], 'v_w_gate': out['v_w_gate'], 'v_w_up': out['v_w_up'], 'v_w_down': out['v_w_down'], 'v_post_ffn_norm': out['v_post_ffn_norm']}


def _loss(weights, diff, rest, loss_target):
    with _jax.named_scope("forward"):
        args = {**rest, TWIN_DIFF_INPUT: diff, **{k: w.astype(_WEIGHT_DTYPES[k]) for k, w in weights.items()}}
        y = _forward(args)
    with _jax.named_scope("loss_head"):
        err = _jnp.square(y.astype(_jnp.float32) - loss_target)
        return 0.5 * _jnp.sum(_jnp.mean(err, axis=-1)) if err.ndim else 0.5 * err


def _adamw(w, g, m, v):
    m = ADAM_B1 * m + (1.0 - ADAM_B1) * g
    v = ADAM_B2 * v + (1.0 - ADAM_B2) * _jnp.square(g)
    m_hat = m / (1.0 - ADAM_B1 ** ADAM_STEP)
    v_hat = v / (1.0 - ADAM_B2 ** ADAM_STEP)
    delta = -ADAM_LR * (m_hat / (_jnp.sqrt(v_hat) + ADAM_EPS) + ADAM_WD * w)
    return delta, m, v


def reference(x, positions, pre_mix_norm, w_in, q_norm, w_uq, kv_norm, w_ukv, conv_w, conv_b, conv_ln_g, conv_ln_b, conv_out_norm, attn_out_norm, w_out, post_mix_norm, pre_ffn_norm, w_gate, w_up, w_down, post_ffn_norm, loss_target, m_pre_mix_norm, m_w_in, m_q_norm, m_w_uq, m_kv_norm, m_w_ukv, m_conv_w, m_conv_b, m_conv_ln_g, m_conv_ln_b, m_conv_out_norm, m_attn_out_norm, m_w_out, m_post_mix_norm, m_pre_ffn_norm, m_w_gate, m_w_up, m_w_down, m_post_ffn_norm, v_pre_mix_norm, v_w_in, v_q_norm, v_w_uq, v_kv_norm, v_w_ukv, v_conv_w, v_conv_b, v_conv_ln_g, v_conv_ln_b, v_conv_out_norm, v_attn_out_norm, v_w_out, v_post_mix_norm, v_pre_ffn_norm, v_w_gate, v_w_up, v_w_down, v_post_ffn_norm):
    given = dict(x=x, positions=positions, pre_mix_norm=pre_mix_norm, w_in=w_in, q_norm=q_norm, w_uq=w_uq, kv_norm=kv_norm, w_ukv=w_ukv, conv_w=conv_w, conv_b=conv_b, conv_ln_g=conv_ln_g, conv_ln_b=conv_ln_b, conv_out_norm=conv_out_norm, attn_out_norm=attn_out_norm, w_out=w_out, post_mix_norm=post_mix_norm, pre_ffn_norm=pre_ffn_norm, w_gate=w_gate, w_up=w_up, w_down=w_down, post_ffn_norm=post_ffn_norm, loss_target=loss_target, m_pre_mix_norm=m_pre_mix_norm, m_w_in=m_w_in, m_q_norm=m_q_norm, m_w_uq=m_w_uq, m_kv_norm=m_kv_norm, m_w_ukv=m_w_ukv, m_conv_w=m_conv_w, m_conv_b=m_conv_b, m_conv_ln_g=m_conv_ln_g, m_conv_ln_b=m_conv_ln_b, m_conv_out_norm=m_conv_out_norm, m_attn_out_norm=m_attn_out_norm, m_w_out=m_w_out, m_post_mix_norm=m_post_mix_norm, m_pre_ffn_norm=m_pre_ffn_norm, m_w_gate=m_w_gate, m_w_up=m_w_up, m_w_down=m_w_down, m_post_ffn_norm=m_post_ffn_norm, v_pre_mix_norm=v_pre_mix_norm, v_w_in=v_w_in, v_q_norm=v_q_norm, v_w_uq=v_w_uq, v_kv_norm=v_kv_norm, v_w_ukv=v_w_ukv, v_conv_w=v_conv_w, v_conv_b=v_conv_b, v_conv_ln_g=v_conv_ln_g, v_conv_ln_b=v_conv_ln_b, v_conv_out_norm=v_conv_out_norm, v_attn_out_norm=v_attn_out_norm, v_w_out=v_w_out, v_post_mix_norm=v_post_mix_norm, v_pre_ffn_norm=v_pre_ffn_norm, v_w_gate=v_w_gate, v_w_up=v_w_up, v_w_down=v_w_down, v_post_ffn_norm=v_post_ffn_norm)
    weights = {n: given[n] for n in TWIN_WEIGHTS}
    shared = {n: given[n] for n in SHARED_INPUTS}
    per_example = {n: given[n] for n in ['x', 'positions']}
    grad_fn = _jax.value_and_grad(_loss, argnums=(0, 1))

    def one_microbatch(ex, loss_target):
        ex = dict(ex)
        diff = ex.pop(TWIN_DIFF_INPUT)
        return grad_fn(weights, diff, {**shared, **ex}, loss_target)

    if N_MICROBATCH == 1:
        loss, (grad_w, grad_x) = one_microbatch(per_example, given["loss_target"])
    else:
        def body(carry, xs):
            loss_sum, grad_sum = carry
            l_k, (gw_k, gx_k) = one_microbatch(xs[0], xs[1])
            with _jax.named_scope("update"):
                return (loss_sum + l_k, _jax.tree.map(_jnp.add, grad_sum, gw_k)), gx_k

        init = (_jnp.zeros((), _jnp.float32), _jax.tree.map(_jnp.zeros_like, weights))
        (loss, grad_w), grad_x = _jax.lax.scan(body, init, (per_example, given["loss_target"]))
    with _jax.named_scope("update"):
        delta_w, new_m, new_v = {}, {}, {}
        for n in TWIN_WEIGHTS:
            delta_w[n], new_m[n], new_v[n] = _adamw(weights[n], grad_w[n], given["m_" + n], given["v_" + n])
    return (loss, grad_x, *[grad_w[n] for n in TWIN_WEIGHTS], *[delta_w[n] for n in TWIN_WEIGHTS],
            *[new_m[n] for n in TWIN_WEIGHTS], *[new_v[n] for n in TWIN_WEIGHTS])
```

```python
import functools

import jax
import jax.numpy as jnp
from jax import lax
from jax.experimental import pallas as pl
from jax.experimental.pallas import tpu as pltpu

F32 = jnp.float32
BF16 = jnp.bfloat16

D_MODEL = 2048
CONV_CH = 1024
CONV_K = 31
CONV_K_PAD = 32
N_HEADS = 8
QK_NOPE = 128
QK_ROPE = 64
V_HEAD = 128
QK_HEAD = QK_NOPE + QK_ROPE
Q_LORA = 768
KV_LORA = 512
ATTN_CH = N_HEADS * V_HEAD
Z2_COLS = Q_LORA + KV_LORA + 128
D_FF = 5632
ROPE_THETA = 10000.0
EPS = 1e-6
LANES = 128
N_CHIPS = 4
N_DEV = 8

ADAM_LR = 0.001
ADAM_B1 = 0.9
ADAM_B2 = 0.999
ADAM_EPS = 1e-08
ADAM_WD = 0.01
ADAM_STEP = 10

VMEM_LIMIT = 56 * 1024 * 1024
ROW_TILE = 256
MESH = pl.DeviceIdType.MESH


def _params(sem=None):
    return pltpu.CompilerParams(dimension_semantics=sem, vmem_limit_bytes=VMEM_LIMIT)


def _first_divisor(n, cands):
    for c in cands:
        if n % c == 0:
            return c
    return n


def _matmul(pairs, mode, out_dtype, name):
    a0, b0 = pairs[0]
    if mode == "nn":
        m, n = a0.shape[0], b0.shape[1]
        ks = [a.shape[1] for a, _ in pairs]
    elif mode == "nt":
        m, n = a0.shape[0], b0.shape[0]
        ks = [a.shape[1] for a, _ in pairs]
    else:
        m, n = a0.shape[1], b0.shape[1]
        ks = [a.shape[0] for a, _ in pairs]
    tm = _first_divisor(m, (1024, 768, 512, 256))
    tn = n if n <= 1536 else _first_divisor(n, (1024, 512, 256, 128))
    tks = [512 if k % 512 == 0 else k for k in ks]
    nks = [k // tk for k, tk in zip(ks, tks)]
    offs = [sum(nks[:p]) for p in range(len(pairs))]
    nk = sum(nks)
    n_pairs = len(pairs)

    def kk(k, p):
        return jnp.clip(k - offs[p], 0, nks[p] - 1)

    in_specs = []
    for p in range(n_pairs):
        tk = tks[p]
        if mode == "nn":
            in_specs.append(pl.BlockSpec((tm, tk), lambda i, j, k, p=p: (i, kk(k, p))))
            in_specs.append(pl.BlockSpec((tk, tn), lambda i, j, k, p=p: (kk(k, p), j)))
        elif mode == "nt":
            in_specs.append(pl.BlockSpec((tm, tk), lambda i, j, k, p=p: (i, kk(k, p))))
            in_specs.append(pl.BlockSpec((tn, tk), lambda i, j, k, p=p: (j, kk(k, p))))
        else:
            in_specs.append(pl.BlockSpec((tk, tm), lambda i, j, k, p=p: (kk(k, p), i)))
            in_specs.append(pl.BlockSpec((tk, tn), lambda i, j, k, p=p: (kk(k, p), j)))
    dims = {"nn": (((1,), (0,)), ((), ())), "nt": (((1,), (1,)), ((), ())), "tn": (((0,), (0,)), ((), ()))}[mode]

    def body(*refs):
        o_ref, acc = refs[2 * n_pairs], refs[2 * n_pairs + 1]
        k = pl.program_id(2)

        @pl.when(k == 0)
        def _():
            acc[...] = jnp.zeros_like(acc)

        for p in range(n_pairs):
            a_ref, b_ref = refs[2 * p], refs[2 * p + 1]

            def step(a_ref=a_ref, b_ref=b_ref):
                acc[...] += lax.dot_general(a_ref[...], b_ref[...], dims, preferred_element_type=F32)

            if n_pairs == 1:
                step()
            else:
                pl.when((k >= offs[p]) & (k < offs[p] + nks[p]))(step)

        @pl.when(k == nk - 1)
        def _():
            o_ref[...] = acc[...].astype(o_ref.dtype)

    flat = [t for pr in pairs for t in pr]
    return pl.pallas_call(
        body,
        name=name,
        grid=(m // tm, n // tn, nk),
        in_specs=in_specs,
        out_specs=pl.BlockSpec((tm, tn), lambda i, j, k: (i, j)),
        out_shape=jax.ShapeDtypeStruct((m, n), out_dtype),
        scratch_shapes=[pltpu.VMEM((tm, tn), F32)],
        compiler_params=_params(("parallel", "parallel", "arbitrary")),
    )(*flat)


def _rowwise(fn, row_ins, vec_ins, row_outs, acc_outs, name):
    t = row_ins[0].shape[0]
    tm = ROW_TILE
    n_in = len(row_ins) + len(vec_ins)
    n_row = len(row_outs)

    def body(*refs):
        ins = [r[...] for r in refs[:n_in]]
        outs = refs[n_in:]
        vals = fn(*ins)
        for r, v in zip(outs[:n_row], vals[:n_row]):
            r[...] = v.astype(r.dtype)
        if acc_outs:
            @pl.when(pl.program_id(0) == 0)
            def _():
                for r in outs[n_row:]:
                    r[...] = jnp.zeros_like(r)

            for r, v in zip(outs[n_row:], vals[n_row:]):
                r[...] += v

    in_specs = [pl.BlockSpec((tm, a.shape[1]), lambda i: (i, 0)) for a in row_ins]
    in_specs += [pl.BlockSpec(a.shape, lambda i: (0, 0)) for a in vec_ins]
    out_specs = [pl.BlockSpec((tm, c), lambda i: (i, 0)) for c, _ in row_outs]
    out_specs += [pl.BlockSpec((1, c), lambda i: (0, 0)) for c in acc_outs]
    out_shape = [jax.ShapeDtypeStruct((t, c), dt) for c, dt in row_outs]
    out_shape += [jax.ShapeDtypeStruct((1, c), F32) for c in acc_outs]
    return pl.pallas_call(
        body,
        name=name,
        grid=(t // tm,),
        in_specs=in_specs,
        out_specs=out_specs,
        out_shape=out_shape,
        compiler_params=_params(("arbitrary",)),
    )(*row_ins, *vec_ins)


def _mean(v):
    return jnp.mean(v, axis=-1, keepdims=True)


def _colsum(v):
    return jnp.sum(v, axis=0, keepdims=True)


def _rms_fwd(v, g):
    r = lax.rsqrt(_mean(v * v) + EPS)
    vhat = v * r
    return vhat * g, vhat, r


def _rms_bwd(dn, vhat, r, g):
    dng = dn * g
    return r * (dng - vhat * _mean(dng * vhat)), _colsum(dn * vhat)


def _swap_rope_halves(v):
    n = v.shape[-1]
    lane = lax.broadcasted_iota(jnp.int32, v.shape, v.ndim - 1)
    return jnp.where(lane % QK_ROPE < QK_ROPE // 2, pltpu.roll(v, n - QK_ROPE // 2, v.ndim - 1),
                     pltpu.roll(v, QK_ROPE // 2, v.ndim - 1))


def _rope(v, cos2, sin2):
    return v * cos2 + _swap_rope_halves(v) * sin2


def _rope_transposed(dv, cos2, sin2):
    return dv * cos2 + _swap_rope_halves(dv * sin2)


def _sigmoid(v):
    return 1.0 / (1.0 + jnp.exp(-v))


CONV_ROWS = 256


def _conv_fwd(ag, conv_w, conv_b):
    t = ag.shape[0]
    cb = LANES

    def body(ag_ref, w_ref, b_ref, o_ref, scr):
        a = ag_ref[:, :cb].astype(F32)
        g = ag_ref[:, cb:].astype(F32)
        scr[pl.ds(0, CONV_K_PAD), :] = jnp.zeros((CONV_K_PAD, cb), F32)
        scr[pl.ds(CONV_K_PAD, t), :] = a * _sigmoid(g)
        for r0 in range(0, t, CONV_ROWS):
            acc = jnp.zeros((CONV_ROWS, cb), F32) + b_ref[...]
            for k in range(CONV_K):
                acc = acc + w_ref[k:k + 1, :] * scr[pl.ds(r0 + CONV_K_PAD - (CONV_K - 1) + k, CONV_ROWS), :]
            o_ref[pl.ds(r0, CONV_ROWS), :] = acc

    return pl.pallas_call(
        body,
        name="conv_fwd",
        grid=(CONV_CH // cb,),
        in_specs=[pl.BlockSpec((t, 2 * cb), lambda j: (0, j)),
                  pl.BlockSpec((CONV_K_PAD, cb), lambda j: (0, j)),
                  pl.BlockSpec((1, cb), lambda j: (0, j))],
        out_specs=pl.BlockSpec((t, cb), lambda j: (0, j)),
        out_shape=jax.ShapeDtypeStruct((t, CONV_CH), F32),
        scratch_shapes=[pltpu.VMEM((t + CONV_K_PAD, cb), F32)],
        compiler_params=_params(("parallel",)),
    )(ag, conv_w, conv_b)


def _conv_bwd(d_u1, ag, conv_w):
    t = ag.shape[0]
    cb = LANES

    def body(du_ref, ag_ref, w_ref, dag_ref, dw_ref, db_ref, su, sd):
        a = ag_ref[:, :cb].astype(F32)
        g = ag_ref[:, cb:].astype(F32)
        sg = _sigmoid(g)
        su[pl.ds(0, CONV_K_PAD), :] = jnp.zeros((CONV_K_PAD, cb), F32)
        su[pl.ds(CONV_K_PAD, t), :] = a * sg
        sd[pl.ds(0, t), :] = du_ref[...]
        sd[pl.ds(t, CONV_K_PAD), :] = jnp.zeros((CONV_K_PAD, cb), F32)
        db_ref[...] = _colsum(du_ref[...])
        dw_ref[...] = jnp.zeros_like(dw_ref)
        for r0 in range(0, t, CONV_ROWS):
            du = sd[pl.ds(r0, CONV_ROWS), :]
            acc = jnp.zeros((CONV_ROWS, cb), F32)
            for k in range(CONV_K):
                acc = acc + w_ref[k:k + 1, :] * sd[pl.ds(r0 + (CONV_K - 1) - k, CONV_ROWS), :]
                dw_ref[k:k + 1, :] += _colsum(du * su[pl.ds(r0 + CONV_K_PAD - (CONV_K - 1) + k, CONV_ROWS), :])
            sgc = sg[r0:r0 + CONV_ROWS]
            ac = a[r0:r0 + CONV_ROWS]
            dag_ref[pl.ds(r0, CONV_ROWS), :cb] = (acc * sgc).astype(dag_ref.dtype)
            dag_ref[pl.ds(r0, CONV_ROWS), cb:] = (acc * ac * sgc * (1.0 - sgc)).astype(dag_ref.dtype)

    return pl.pallas_call(
        body,
        name="conv_bwd",
        grid=(CONV_CH // cb,),
        in_specs=[pl.BlockSpec((t, cb), lambda j: (0, j)),
                  pl.BlockSpec((t, 2 * cb), lambda j: (0, j)),
                  pl.BlockSpec((CONV_K_PAD, cb), lambda j: (0, j))],
        out_specs=[pl.BlockSpec((t, 2 * cb), lambda j: (0, j)),
                   pl.BlockSpec((CONV_K_PAD, cb), lambda j: (0, j)),
                   pl.BlockSpec((1, cb), lambda j: (0, j))],
        out_shape=[jax.ShapeDtypeStruct((t, 2 * CONV_CH), BF16),
                   jax.ShapeDtypeStruct((CONV_K_PAD, CONV_CH), F32),
                   jax.ShapeDtypeStruct((1, CONV_CH), F32)],
        scratch_shapes=[pltpu.VMEM((t + CONV_K_PAD, cb), F32), pltpu.VMEM((t + CONV_K_PAD, cb), F32)],
        compiler_params=_params(("parallel",)),
    )(d_u1, ag, conv_w)


ATT_TQ = 256
NEG = float(jnp.finfo(jnp.float32).min)
SCALE = QK_HEAD ** -0.5
NT_DIMS = (((1,), (1,)), ((), ()))
TN_DIMS = (((0,), (0,)), ((), ()))


def _att_probs(qf, kf, row0):
    s = lax.dot_general(qf, kf, NT_DIMS, preferred_element_type=F32) * SCALE
    tq, t = s.shape
    qpos = row0 + lax.broadcasted_iota(jnp.int32, (tq, t), 0)
    kpos = lax.broadcasted_iota(jnp.int32, (tq, t), 1)
    s = jnp.where(kpos <= qpos, s, NEG)
    p = jnp.exp(s - jnp.max(s, axis=-1, keepdims=True))
    return p * (1.0 / jnp.sum(p, axis=-1, keepdims=True))


def _half_mask(shape, which):
    lane = lax.broadcasted_iota(jnp.int32, shape, len(shape) - 1)
    return (lane // QK_ROPE == which).astype(F32)


def _attention_fwd(q, kv, kpe2, cos2, sin2):
    t = q.shape[0]
    tq = ATT_TQ

    def body(qn_ref, qp_ref, c_ref, s_ref, kv_ref, kpe_ref, o_ref):
        row0 = pl.program_id(1) * tq
        roped = _rope(qp_ref[...].astype(F32), c_ref[...], s_ref[...])
        for e in range(2):
            qf = jnp.concatenate([qn_ref[:, e * QK_NOPE:(e + 1) * QK_NOPE],
                                  (roped * _half_mask(roped.shape, e)).astype(BF16)], axis=1)
            kf = jnp.concatenate([kv_ref[:, e * 256:e * 256 + QK_NOPE], kpe_ref[...]], axis=1)
            p = _att_probs(qf, kf, row0)
            v = kv_ref[:, e * 256 + QK_NOPE:(e + 1) * 256]
            o_ref[:, e * V_HEAD:(e + 1) * V_HEAD] = jnp.dot(p.astype(BF16), v, preferred_element_type=F32).astype(o_ref.dtype)

    return pl.pallas_call(
        body,
        name="attention_fwd",
        grid=(N_HEADS // 2, t // tq),
        in_specs=[pl.BlockSpec((tq, 2 * QK_NOPE), lambda h, i: (i, h)),
                  pl.BlockSpec((tq, LANES), lambda h, i: (i, N_HEADS + h)),
                  pl.BlockSpec((tq, LANES), lambda h, i: (i, 0)),
                  pl.BlockSpec((tq, LANES), lambda h, i: (i, 0)),
                  pl.BlockSpec((t, 512), lambda h, i: (0, h)),
                  pl.BlockSpec((t, LANES), lambda h, i: (0, 0))],
        out_specs=pl.BlockSpec((tq, 2 * V_HEAD), lambda h, i: (i, h)),
        out_shape=jax.ShapeDtypeStruct((t, ATTN_CH), BF16),
        compiler_params=_params(("parallel", "parallel")),
    )(q, q, cos2, sin2, kv, kpe2)


def _attention_bwd(q, kv, kpe2, cos2, sin2, d_attn):
    t = q.shape[0]
    tq = ATT_TQ
    n_q = t // tq

    def body(qn_ref, qp_ref, c_ref, s_ref, kv_ref, kpe_ref, do_ref, dqn_ref, dqp_ref, dkv_ref, dkpe_ref, dkv_acc):
        h, i = pl.program_id(0), pl.program_id(1)
        row0 = i * tq

        @pl.when(i == 0)
        def _():
            dkv_acc[...] = jnp.zeros_like(dkv_acc)

        @pl.when((i == 0) & (h == 0))
        def _():
            dkpe_ref[...] = jnp.zeros_like(dkpe_ref)

        roped = _rope(qp_ref[...].astype(F32), c_ref[...], s_ref[...])
        d_roped = jnp.zeros((tq, LANES), F32)
        for e in range(2):
            mask = _half_mask(roped.shape, e)
            qf = jnp.concatenate([qn_ref[:, e * QK_NOPE:(e + 1) * QK_NOPE], (roped * mask).astype(BF16)], axis=1)
            kf = jnp.concatenate([kv_ref[:, e * 256:e * 256 + QK_NOPE], kpe_ref[...]], axis=1)
            v = kv_ref[:, e * 256 + QK_NOPE:(e + 1) * 256]
            do = do_ref[:, e * V_HEAD:(e + 1) * V_HEAD]
            p = _att_probs(qf, kf, row0)
            dp = lax.dot_general(do, v, NT_DIMS, preferred_element_type=F32)
            ds = (p * (dp - jnp.sum(p * dp, axis=-1, keepdims=True)) * SCALE).astype(BF16)
            dqf = jnp.dot(ds, kf, preferred_element_type=F32)
            dkf = lax.dot_general(ds, qf, TN_DIMS, preferred_element_type=F32)
            dv = lax.dot_general(p.astype(BF16), do, TN_DIMS, preferred_element_type=F32)
            dqn_ref[:, e * QK_NOPE:(e + 1) * QK_NOPE] = dqf[:, :QK_NOPE].astype(dqn_ref.dtype)
            d_roped = d_roped + dqf[:, QK_NOPE:] * mask
            dkv_acc[:, e * 256:e * 256 + QK_NOPE] += dkf[:, :QK_NOPE]
            dkv_acc[:, e * 256 + QK_NOPE:(e + 1) * 256] += dv
            dkpe_ref[...] += dkf[:, QK_NOPE:]
        dqp_ref[...] = _rope_transposed(d_roped, c_ref[...], s_ref[...]).astype(dqp_ref.dtype)

        @pl.when(i == n_q - 1)
        def _():
            dkv_ref[...] = dkv_acc[...].astype(dkv_ref.dtype)

    return pl.pallas_call(
        body,
        name="attention_bwd",
        grid=(N_HEADS // 2, n_q),
        in_specs=[pl.BlockSpec((tq, 2 * QK_NOPE), lambda h, i: (i, h)),
                  pl.BlockSpec((tq, LANES), lambda h, i: (i, N_HEADS + h)),
                  pl.BlockSpec((tq, LANES), lambda h, i: (i, 0)),
                  pl.BlockSpec((tq, LANES), lambda h, i: (i, 0)),
                  pl.BlockSpec((t, 512), lambda h, i: (0, h)),
                  pl.BlockSpec((t, LANES), lambda h, i: (0, 0)),
                  pl.BlockSpec((tq, 2 * V_HEAD), lambda h, i: (i, h))],
        out_specs=[pl.BlockSpec((tq, 2 * QK_NOPE), lambda h, i: (i, h)),
                   pl.BlockSpec((tq, LANES), lambda h, i: (i, h)),
                   pl.BlockSpec((t, 512), lambda h, i: (0, h)),
                   pl.BlockSpec((t, LANES), lambda h, i: (0, 0))],
        out_shape=[jax.ShapeDtypeStruct((t, N_HEADS * QK_NOPE), BF16),
                   jax.ShapeDtypeStruct((t, N_HEADS * QK_ROPE), BF16),
                   jax.ShapeDtypeStruct((t, N_HEADS * 256), BF16),
                   jax.ShapeDtypeStruct((t, LANES), F32)],
        scratch_shapes=[pltpu.VMEM((t, 512), F32)],
        compiler_params=_params(("arbitrary", "arbitrary")),
    )(q, q, cos2, sin2, kv, kpe2, d_attn)


FF_TILE = 512


def _swiglu_fwd(gu):
    t = gu.shape[0]
    nj = D_FF // FF_TILE

    def body(g_ref, u_ref, o_ref):
        g = g_ref[...].astype(F32)
        o_ref[...] = (g * _sigmoid(g) * u_ref[...].astype(F32)).astype(o_ref.dtype)

    return pl.pallas_call(
        body,
        name="swiglu_fwd",
        grid=(t // ROW_TILE, nj),
        in_specs=[pl.BlockSpec((ROW_TILE, FF_TILE), lambda i, j: (i, j)),
                  pl.BlockSpec((ROW_TILE, FF_TILE), lambda i, j: (i, nj + j))],
        out_specs=pl.BlockSpec((ROW_TILE, FF_TILE), lambda i, j: (i, j)),
        out_shape=jax.ShapeDtypeStruct((t, D_FF), BF16),
        compiler_params=_params(("parallel", "parallel")),
    )(gu, gu)


def _swiglu_bwd(gu, d_act):
    t = gu.shape[0]
    nj = D_FF // FF_TILE

    def body(g_ref, u_ref, d_ref, o_ref):
        g = g_ref[...].astype(F32)
        u = u_ref[...].astype(F32)
        d = d_ref[...].astype(F32)
        sg = _sigmoid(g)
        is_gate = pl.program_id(1) < nj
        d_gate = d * u * sg * (1.0 + g * (1.0 - sg))
        d_up = d * g * sg
        o_ref[...] = jnp.where(is_gate, d_gate, d_up).astype(o_ref.dtype)

    return pl.pallas_call(
        body,
        name="swiglu_bwd",
        grid=(t // ROW_TILE, 2 * nj),
        in_specs=[pl.BlockSpec((ROW_TILE, FF_TILE), lambda i, j: (i, j % nj)),
                  pl.BlockSpec((ROW_TILE, FF_TILE), lambda i, j: (i, nj + j % nj)),
                  pl.BlockSpec((ROW_TILE, FF_TILE), lambda i, j: (i, j % nj))],
        out_specs=pl.BlockSpec((ROW_TILE, FF_TILE), lambda i, j: (i, j)),
        out_shape=jax.ShapeDtypeStruct((t, 2 * D_FF), BF16),
        compiler_params=_params(("parallel", "parallel")),
    )(gu, gu, d_act)


def _local_step(x, target, cos2, sin2, vec, w):
    d = D_MODEL

    (h,) = _rowwise(lambda xv, g: (_rms_fwd(xv, g)[0],), [x], [vec["pre_mix_norm"]], [(d, BF16)], [], "pre_mix_norm_fwd")
    ag = _matmul([(h, w["w_ag"])], "nn", BF16, "in_proj_ag")
    z2 = _matmul([(h, w["w_z2"])], "nn", BF16, "in_proj_z2")
    u1 = _conv_fwd(ag, w["conv_w"], vec["conv_b"])

    def latents_fwd(z, c2, s2, qg, kvg):
        z = z.astype(F32)
        qn = _rms_fwd(z[:, :Q_LORA], qg)[0]
        kvn = _rms_fwd(z[:, Q_LORA:Q_LORA + KV_LORA], kvg)[0]
        kr = z[:, Q_LORA + KV_LORA:]
        kr2 = kr + pltpu.roll(kr, QK_ROPE, 1)
        return qn, kvn, _rope(kr2, c2, s2)

    qn, kvn, kpe2 = _rowwise(latents_fwd, [z2, cos2, sin2], [vec["q_norm"], vec["kv_norm"]],
                             [(Q_LORA, BF16), (KV_LORA, BF16), (LANES, BF16)], [], "latents_fwd")
    q = _matmul([(qn, w["w_uq"])], "nn", BF16, "q_up")
    kv = _matmul([(kvn, w["w_ukv"])], "nn", BF16, "kv_up")
    attn = _attention_fwd(q, kv, kpe2, cos2, sin2)

    def conv_post(u, lg, lb):
        mu = _mean(u)
        uc = u - mu
        rstd = lax.rsqrt(_mean(uc * uc) + EPS)
        uhat = uc * rstd
        u2 = uhat * lg + lb
        sg = _sigmoid(u2)
        return uhat, rstd, u2, sg, u2 * sg

    def mix_in_fwd(u, at, lg, lb, cg, ag_):
        u3 = conv_post(u, lg, lb)[4]
        cn = _rms_fwd(u3, cg)[0]
        an = _rms_fwd(at.astype(F32), ag_)[0]
        return (jnp.concatenate([cn, an], axis=1),)

    (cat,) = _rowwise(mix_in_fwd, [u1, attn], [vec["conv_ln_g"], vec["conv_ln_b"], vec["conv_out_norm"], vec["attn_out_norm"]],
                      [(2 * CONV_CH, BF16)], [], "mix_in_fwd")
    mix = _matmul([(cat, w["w_out"])], "nn", F32, "out_proj")

    def residual1(xv, mv, gpm, gpf):
        x1 = xv + _rms_fwd(mv, gpm)[0]
        return x1, _rms_fwd(x1, gpf)[0]

    x1, hf = _rowwise(residual1, [x, mix], [vec["post_mix_norm"], vec["pre_ffn_norm"]], [(d, F32), (d, BF16)], [], "residual1_fwd")
    gu = _matmul([(hf, w["w_gu"])], "nn", BF16, "ffn_gate_up")
    act = _swiglu_fwd(gu)
    ff = _matmul([(act, w["w_down"])], "nn", F32, "ffn_down")

    def loss_head(x1v, ffv, tg, g):
        n, fhat, r = _rms_fwd(ffv, g)
        err = x1v + n - tg
        loss = 0.5 * jnp.sum(_mean(err * err), axis=0, keepdims=True)
        dy = err * (1.0 / d)
        d_ff, dg = _rms_bwd(dy, fhat, r, g)
        return dy, d_ff, dg, jnp.broadcast_to(loss, (1, LANES))

    dy, d_ff, g_post_ffn, loss = _rowwise(loss_head, [x1, ff, target], [vec["post_ffn_norm"]],
                                          [(d, F32), (d, BF16)], [d, LANES], "loss_head")
    d_act = _matmul([(d_ff, w["w_down"])], "nt", BF16, "ffn_down_dx")
    dw_down = _matmul([(act, d_ff)], "tn", BF16, "ffn_down_dw")
    d_gu = _swiglu_bwd(gu, d_act)
    d_hf = _matmul([(d_gu, w["w_gu"])], "nt", F32, "ffn_gate_up_dx")
    dw_gu = _matmul([(hf, d_gu)], "tn", BF16, "ffn_gate_up_dw")

    def residual1_bwd(dyv, dhf, x1v, mv, gpf, gpm):
        _, x1hat, r1 = _rms_fwd(x1v, gpf)
        dn, dgpf = _rms_bwd(dhf, x1hat, r1, gpf)
        d_x1 = dyv + dn
        _, mhat, rm = _rms_fwd(mv, gpm)
        d_mix, dgpm = _rms_bwd(d_x1, mhat, rm, gpm)
        return d_x1, d_mix, dgpf, dgpm

    d_x1, d_mix, g_pre_ffn, g_post_mix = _rowwise(residual1_bwd, [dy, d_hf, x1, mix], [vec["pre_ffn_norm"], vec["post_mix_norm"]],
                                                  [(d, F32), (d, BF16)], [d, d], "residual1_bwd")
    d_cat = _matmul([(d_mix, w["w_out"])], "nt", BF16, "out_proj_dx")
    dw_out = _matmul([(cat, d_mix)], "tn", BF16, "out_proj_dw")

    def mix_in_bwd(dc, u, at, lg, lb, cg, ag_):
        dc = dc.astype(F32)
        uhat, rstd, u2, sg, u3 = conv_post(u, lg, lb)
        _, u3hat, rc = _rms_fwd(u3, cg)
        d_u3, dcg = _rms_bwd(dc[:, :CONV_CH], u3hat, rc, cg)
        d_u2 = d_u3 * sg * (1.0 + u2 * (1.0 - sg))
        dgl = d_u2 * lg
        d_u1 = rstd * (dgl - _mean(dgl) - uhat * _mean(dgl * uhat))
        _, ahat, ra = _rms_fwd(at.astype(F32), ag_)
        d_at, dag = _rms_bwd(dc[:, CONV_CH:], ahat, ra, ag_)
        return d_u1, d_at, dcg, _colsum(d_u2 * uhat), _colsum(d_u2), dag

    d_u1, d_attn, g_conv_out, g_ln_g, g_ln_b, g_attn_out = _rowwise(
        mix_in_bwd, [d_cat, u1, attn], [vec["conv_ln_g"], vec["conv_ln_b"], vec["conv_out_norm"], vec["attn_out_norm"]],
        [(CONV_CH, F32), (ATTN_CH, BF16)], [CONV_CH] * 4, "mix_in_bwd")
    d_ag, d_conv_w, g_conv_b = _conv_bwd(d_u1, ag, w["conv_w"])
    d_qn_, d_qp_, d_kv, d_kpe2 = _attention_bwd(q, kv, kpe2, cos2, sin2, d_attn)
    d_q = jnp.concatenate([d_qn_, d_qp_], axis=1)
    d_qn = _matmul([(d_q, w["w_uq"])], "nt", BF16, "q_up_dx")
    dw_uq = _matmul([(qn, d_q)], "tn", BF16, "q_up_dw")
    d_kvn = _matmul([(d_kv, w["w_ukv"])], "nt", BF16, "kv_up_dx")
    dw_ukv = _matmul([(kvn, d_kv)], "tn", BF16, "kv_up_dw")

    def latents_bwd(z, dq, dk, dkp, c2, s2, qg, kvg):
        z = z.astype(F32)
        _, qhat, rq = _rms_fwd(z[:, :Q_LORA], qg)
        d_ql, dqg = _rms_bwd(dq.astype(F32), qhat, rq, qg)
        _, khat, rk = _rms_fwd(z[:, Q_LORA:Q_LORA + KV_LORA], kvg)
        d_kl, dkg = _rms_bwd(dk.astype(F32), khat, rk, kvg)
        both = dkp + pltpu.roll(dkp, QK_ROPE, 1)
        d_kr = _rope_transposed(both, c2, s2) * _half_mask(both.shape, 0)
        return jnp.concatenate([d_ql, d_kl, d_kr], axis=1), dqg, dkg

    d_z2, g_q_norm, g_kv_norm = _rowwise(latents_bwd, [z2, d_qn, d_kvn, d_kpe2, cos2, sin2], [vec["q_norm"], vec["kv_norm"]],
                                         [(Z2_COLS, BF16)], [Q_LORA, KV_LORA], "latents_bwd")
    d_h = _matmul([(d_ag, w["w_ag"]), (d_z2, w["w_z2"])], "nt", F32, "in_proj_dx")
    dw_ag = _matmul([(h, d_ag)], "tn", BF16, "in_proj_ag_dw")
    dw_z2 = _matmul([(h, d_z2)], "tn", BF16, "in_proj_z2_dw")

    def pre_mix_bwd(dx1, dh, xv, g):
        _, xhat, r = _rms_fwd(xv, g)
        dn, dg = _rms_bwd(dh, xhat, r, g)
        return dx1 + dn, dg

    grad_x, g_pre_mix = _rowwise(pre_mix_bwd, [d_x1, d_h, x], [vec["pre_mix_norm"]], [(d, F32)], [d], "pre_mix_norm_bwd")

    dw = dict(w_ag=dw_ag, w_z2=dw_z2, w_uq=dw_uq, w_ukv=dw_ukv, conv_w=d_conv_w, w_out=dw_out, w_gu=dw_gu, w_down=dw_down)
    dvec = dict(pre_mix_norm=g_pre_mix, q_norm=g_q_norm, kv_norm=g_kv_norm, conv_b=g_conv_b, conv_ln_g=g_ln_g,
                conv_ln_b=g_ln_b, conv_out_norm=g_conv_out, attn_out_norm=g_attn_out, post_mix_norm=g_post_mix,
                pre_ffn_norm=g_pre_ffn, post_ffn_norm=g_post_ffn)
    return loss, grad_x, dw, dvec


ANY = pl.BlockSpec(memory_space=pl.ANY)


def _place():
    x, y, c = lax.axis_index("x"), lax.axis_index("y"), lax.axis_index("c")
    chips = [(1 - x, y), (x, 1 - y), (1 - x, 1 - y)]
    return x, y, c, chips


def _all_gather_shards(shards, whole):
    n = len(shards)
    n_split = sum(1 for f in whole if not f)

    def body(*refs):
        ins, outs = refs[:n], refs[n:2 * n]
        send, recv, fsend, frecv, lsem = refs[2 * n:]
        x, y, c, chips = _place()
        p = 2 * x + y
        local = [pltpu.make_async_copy(ins[k], outs[k].at[p], lsem.at[k]) for k in range(n)]
        for cp in local:
            cp.start()

        def rows(k, half):
            r = shards[k].shape[0]
            return pl.ds(0, r) if whole[k] else pl.ds(half * (r // 2), r // 2)

        def ici(k, j, src_part, dst_part, half):
            px, py = chips[j]
            return pltpu.make_async_remote_copy(
                src_ref=ins[k].at[rows(k, half), :] if src_part is None else outs[k].at[src_part, rows(k, half), :],
                dst_ref=outs[k].at[dst_part, rows(k, half), :],
                send_sem=send.at[3 * k + j], recv_sem=recv.at[3 * k + j],
                device_id=(px, py, c), device_id_type=MESH)

        def d2d(s, k, j, half):
            px, py = chips[j]
            q = 2 * px + py
            return pltpu.make_async_remote_copy(
                src_ref=outs[k].at[q, rows(k, half), :], dst_ref=outs[k].at[q, rows(k, half), :],
                send_sem=fsend.at[3 * s + j], recv_sem=frecv.at[3 * s + j],
                device_id=(x, y, 1 - c), device_id_type=MESH)

        sent = [ici(k, j, None, p, c) for k in range(n) for j in range(3)]
        for cp in sent:
            cp.start()
        handed = []
        s = 0
        for k in range(n):
            for j in range(3):
                px, py = chips[j]
                ici(k, j, None, 2 * px + py, c).wait_recv()
                if not whole[k]:
                    cp = d2d(s, k, j, c)
                    cp.start()
                    handed.append(cp)
            if not whole[k]:
                s += 1
        s = 0
        for k in range(n):
            if whole[k]:
                continue
            for j in range(3):
                d2d(s, k, j, 1 - c).wait_recv()
            s += 1
        for cp in sent + handed:
            cp.wait_send()
        for cp in local:
            cp.wait()

    return pl.pallas_call(
        body,
        name="all_gather_weights",
        in_specs=[ANY] * n,
        out_specs=[ANY] * n,
        out_shape=[jax.ShapeDtypeStruct((N_CHIPS,) + a.shape, a.dtype) for a in shards],
        scratch_shapes=[pltpu.SemaphoreType.DMA((3 * n,)), pltpu.SemaphoreType.DMA((3 * n,)),
                        pltpu.SemaphoreType.DMA((3 * n_split,)), pltpu.SemaphoreType.DMA((3 * n_split,)),
                        pltpu.SemaphoreType.DMA((n,))],
        compiler_params=pltpu.CompilerParams(has_side_effects=True),
    )(*shards)


def _pair_exchange(parts):
    n = len(parts)

    def body(*refs):
        ins, outs = refs[:n], refs[n:2 * n]
        send, recv = refs[2 * n:]
        x, y, c, _ = _place()
        copies = []
        for k in range(n):
            rh = parts[k].shape[1] // 2
            cp = pltpu.make_async_remote_copy(
                src_ref=ins[k].at[:, pl.ds((1 - c) * rh, rh), :], dst_ref=outs[k],
                send_sem=send.at[k], recv_sem=recv.at[k], device_id=(x, y, 1 - c), device_id_type=MESH)
            cp.start()
            copies.append(cp)
        for cp in copies:
            cp.wait()

    return pl.pallas_call(
        body,
        name="grad_pair_exchange",
        in_specs=[ANY] * n,
        out_specs=[ANY] * n,
        out_shape=[jax.ShapeDtypeStruct((N_CHIPS, a.shape[1] // 2, a.shape[2]), a.dtype) for a in parts],
        scratch_shapes=[pltpu.SemaphoreType.DMA((n,)), pltpu.SemaphoreType.DMA((n,))],
        compiler_params=pltpu.CompilerParams(has_side_effects=True),
    )(*parts)


def _pair_sum(core, part, landed, name):
    _, r, cdim = part.shape
    rh = r // 2
    tr = _first_divisor(rh, (256, 128, 64, 32, 16))
    nb = rh // tr

    def body(c_ref, a_ref, b_ref, o_ref):
        o_ref[...] = (a_ref[...].astype(F32) + b_ref[...].astype(F32)).astype(o_ref.dtype)

    return pl.pallas_call(
        body,
        name=name,
        grid_spec=pltpu.PrefetchScalarGridSpec(
            num_scalar_prefetch=1,
            grid=(N_CHIPS, nb),
            in_specs=[pl.BlockSpec((None, tr, cdim), lambda q, i, c_ref: (q, c_ref[0] * nb + i, 0)),
                      pl.BlockSpec((None, tr, cdim), lambda q, i, c_ref: (q, i, 0))],
            out_specs=pl.BlockSpec((None, tr, cdim), lambda q, i, c_ref: (q, i, 0))),
        out_shape=jax.ShapeDtypeStruct((N_CHIPS, rh, cdim), BF16),
        compiler_params=_params(("parallel", "parallel")),
    )(core, part, landed)


def _chip_exchange(sums):
    n = len(sums)

    def body(*refs):
        ins, outs = refs[:n], refs[n:2 * n]
        send, recv, lsem = refs[2 * n:]
        x, y, c, chips = _place()
        p = 2 * x + y
        local = [pltpu.make_async_copy(ins[k].at[p], outs[k].at[p], lsem.at[k]) for k in range(n)]
        for cp in local:
            cp.start()

        def ici(k, j, slot):
            px, py = chips[j]
            return pltpu.make_async_remote_copy(
                src_ref=ins[k].at[2 * px + py], dst_ref=outs[k].at[slot],
                send_sem=send.at[3 * k + j], recv_sem=recv.at[3 * k + j],
                device_id=(px, py, c), device_id_type=MESH)

        sent = [ici(k, j, p) for k in range(n) for j in range(3)]
        for cp in sent:
            cp.start()
        for k in range(n):
            for j in range(3):
                px, py = chips[j]
                ici(k, j, 2 * px + py).wait_recv()
        for cp in sent:
            cp.wait_send()
        for cp in local:
            cp.wait()

    return pl.pallas_call(
        body,
        name="grad_chip_exchange",
        in_specs=[ANY] * n,
        out_specs=[ANY] * n,
        out_shape=[jax.ShapeDtypeStruct(a.shape, a.dtype) for a in sums],
        scratch_shapes=[pltpu.SemaphoreType.DMA((3 * n,)), pltpu.SemaphoreType.DMA((3 * n,)),
                        pltpu.SemaphoreType.DMA((n,))],
        compiler_params=pltpu.CompilerParams(has_side_effects=True),
    )(*sums)


def _chip_sum(slots, name):
    _, rh, cdim = slots.shape
    tr = _first_divisor(rh, (256, 128, 64, 32, 16))

    def body(a_ref, o_ref):
        acc = a_ref[0].astype(F32)
        for q in range(1, N_CHIPS):
            acc = acc + a_ref[q].astype(F32)
        o_ref[...] = acc

    return pl.pallas_call(
        body,
        name=name,
        grid=(rh // tr,),
        in_specs=[pl.BlockSpec((N_CHIPS, tr, cdim), lambda i: (0, i, 0))],
        out_specs=pl.BlockSpec((tr, cdim), lambda i: (i, 0)),
        out_shape=jax.ShapeDtypeStruct((rh, cdim), F32),
        compiler_params=_params(("parallel",)),
    )(slots)


def _half_exchange(halves):
    n = len(halves)

    def body(*refs):
        ins, outs = refs[:n], refs[n:2 * n]
        send, recv, lsem = refs[2 * n:]
        x, y, c, _ = _place()
        copies, local = [], []
        for k in range(n):
            rh = halves[k].shape[0]
            mine = outs[k].at[pl.ds(c * rh, rh), :]
            lc = pltpu.make_async_copy(ins[k], mine, lsem.at[k])
            lc.start()
            local.append(lc)
            cp = pltpu.make_async_remote_copy(src_ref=ins[k], dst_ref=mine, send_sem=send.at[k], recv_sem=recv.at[k],
                                              device_id=(x, y, 1 - c), device_id_type=MESH)
            cp.start()
            copies.append(cp)
        for k in range(n):
            rh = halves[k].shape[0]
            copies[k].wait_send()
            pltpu.make_async_remote_copy(src_ref=ins[k], dst_ref=outs[k].at[pl.ds((1 - c) * rh, rh), :],
                                         send_sem=send.at[k], recv_sem=recv.at[k],
                                         device_id=(x, y, 1 - c), device_id_type=MESH).wait_recv()
            local[k].wait()

    return pl.pallas_call(
        body,
        name="grad_half_exchange",
        in_specs=[ANY] * n,
        out_specs=[ANY] * n,
        out_shape=[jax.ShapeDtypeStruct((2 * a.shape[0], a.shape[1]), a.dtype) for a in halves],
        scratch_shapes=[pltpu.SemaphoreType.DMA((n,)), pltpu.SemaphoreType.DMA((n,)), pltpu.SemaphoreType.DMA((n,))],
        compiler_params=pltpu.CompilerParams(has_side_effects=True),
    )(*halves)


SMALL_ROWS = 32


def _all_reduce_small(pack):
    def body(in_ref, out_ref, gath, send, recv):
        x, y, c, _ = _place()
        me = 4 * x + 2 * y + c
        gath[me] = in_ref[...]
        copies = []
        for k in range(1, N_DEV):
            dx, dy, dc = (k >> 2) & 1, (k >> 1) & 1, k & 1
            peer = (x ^ dx, y ^ dy, c ^ dc)
            cp = pltpu.make_async_remote_copy(src_ref=in_ref, dst_ref=gath.at[me], send_sem=send.at[k], recv_sem=recv.at[k],
                                              device_id=peer, device_id_type=MESH)
            cp.start()
            copies.append((cp, 4 * peer[0] + 2 * peer[1] + peer[2]))
        for k, (cp, peer_id) in enumerate(copies, start=1):
            cp.wait_send()
            pltpu.make_async_remote_copy(src_ref=in_ref, dst_ref=gath.at[peer_id], send_sem=send.at[k], recv_sem=recv.at[k],
                                         device_id=(x, y, c), device_id_type=MESH).wait_recv()
        acc = gath[0]
        for dev in range(1, N_DEV):
            acc = acc + gath[dev]
        out_ref[...] = acc

    return pl.pallas_call(
        body,
        name="all_reduce_small",
        in_specs=[pl.BlockSpec(memory_space=pltpu.VMEM)],
        out_specs=pl.BlockSpec(memory_space=pltpu.VMEM),
        out_shape=jax.ShapeDtypeStruct(pack.shape, F32),
        scratch_shapes=[pltpu.VMEM((N_DEV,) + pack.shape, F32), pltpu.SemaphoreType.DMA((N_DEV,)), pltpu.SemaphoreType.DMA((N_DEV,))],
        compiler_params=pltpu.CompilerParams(has_side_effects=True, vmem_limit_bytes=VMEM_LIMIT),
    )(pack)


def _adamw(g, w, m, v, name):
    r, cdim = g.shape
    tr = r if r * cdim * 4 <= (1 << 20) else _first_divisor(r, (128, 64, 32, 16, 8))
    bc1 = 1.0 - ADAM_B1 ** ADAM_STEP
    bc2 = 1.0 - ADAM_B2 ** ADAM_STEP

    def body(g_ref, w_ref, m_ref, v_ref, go_ref, d_ref, mo_ref, vo_ref):
        gv = g_ref[...]
        mn = ADAM_B1 * m_ref[...] + (1.0 - ADAM_B1) * gv
        vn = ADAM_B2 * v_ref[...] + (1.0 - ADAM_B2) * (gv * gv)
        go_ref[...] = gv
        mo_ref[...] = mn
        vo_ref[...] = vn
        d_ref[...] = -ADAM_LR * ((mn / bc1) / (jnp.sqrt(vn / bc2) + ADAM_EPS) + ADAM_WD * w_ref[...])

    spec = pl.BlockSpec((tr, cdim), lambda i: (i, 0))
    return pl.pallas_call(
        body,
        name=name,
        grid=(r // tr,),
        in_specs=[spec] * 4,
        out_specs=[spec] * 4,
        out_shape=[jax.ShapeDtypeStruct((r, cdim), F32)] * 4,
        compiler_params=_params(("parallel",)),
    )(g, w, m, v)


VEC_NAMES = ["pre_mix_norm", "q_norm", "kv_norm", "conv_b", "conv_ln_g", "conv_ln_b", "conv_out_norm",
             "attn_out_norm", "post_mix_norm", "pre_ffn_norm", "post_ffn_norm"]
LOSS_ROW = len(VEC_NAMES)
CONV_W_ROW = 16


def _cols_to_full(parts):
    _, r, cdim = parts.shape
    return parts.transpose(1, 0, 2).reshape(r, N_CHIPS * cdim)


def _full_to_cols(full):
    r, n = full.shape
    return full.reshape(r, N_CHIPS, n // N_CHIPS).transpose(1, 0, 2)


def _assemble_weights(g):
    w_in = _cols_to_full(g["w_in"])
    d = w_in.shape[0]
    w_a = w_in[:, :CONV_CH].reshape(d, CONV_CH // LANES, 1, LANES)
    w_g = w_in[:, CONV_CH:2 * CONV_CH].reshape(d, CONV_CH // LANES, 1, LANES)
    w_ag = jnp.concatenate([w_a, w_g], axis=2).reshape(d, 2 * CONV_CH)
    w_z2 = jnp.concatenate([w_in[:, 2 * CONV_CH:], jnp.zeros((d, LANES - QK_ROPE), w_in.dtype)], axis=1)
    uq = _cols_to_full(g["w_uq"]).reshape(Q_LORA, N_HEADS, QK_HEAD)
    w_uq = jnp.concatenate([uq[:, :, :QK_NOPE].reshape(Q_LORA, N_HEADS * QK_NOPE),
                            uq[:, :, QK_NOPE:].reshape(Q_LORA, N_HEADS * QK_ROPE)], axis=1)
    w_gu = jnp.concatenate([_cols_to_full(g["w_gate"]), _cols_to_full(g["w_up"])], axis=1)
    return dict(w_ag=w_ag, w_z2=w_z2, w_uq=w_uq, w_ukv=_cols_to_full(g["w_ukv"]), conv_w=_cols_to_full(g["conv_w"]),
                w_out=g["w_out"].reshape(-1, g["w_out"].shape[2]), w_gu=w_gu,
                w_down=g["w_down"].reshape(-1, g["w_down"].shape[2]))


def _grads_to_parts(dw):
    d = dw["w_ag"].shape[0]
    ag = dw["w_ag"].reshape(d, CONV_CH // LANES, 2, LANES)
    d_in = jnp.concatenate([ag[:, :, 0, :].reshape(d, CONV_CH), ag[:, :, 1, :].reshape(d, CONV_CH),
                            dw["w_z2"][:, :Q_LORA + KV_LORA + QK_ROPE]], axis=1)
    uq = dw["w_uq"]
    d_uq = jnp.concatenate([uq[:, :N_HEADS * QK_NOPE].reshape(Q_LORA, N_HEADS, QK_NOPE),
                            uq[:, N_HEADS * QK_NOPE:].reshape(Q_LORA, N_HEADS, QK_ROPE)], axis=2).reshape(Q_LORA, N_HEADS * QK_HEAD)
    return dict(w_in=_full_to_cols(d_in), w_uq=_full_to_cols(d_uq), w_ukv=_full_to_cols(dw["w_ukv"]),
                w_out=dw["w_out"].reshape(N_CHIPS, -1, dw["w_out"].shape[1]),
                w_gate=_full_to_cols(dw["w_gu"][:, :D_FF]), w_up=_full_to_cols(dw["w_gu"][:, D_FF:]),
                w_down=dw["w_down"].reshape(N_CHIPS, -1, dw["w_down"].shape[1]))


BIG = ["w_in", "w_uq", "w_ukv", "w_out", "w_gate", "w_up", "w_down"]


def _pad_lanes(v, n):
    return jnp.pad(v, ((0, 0), (0, n - v.shape[1])))


def kernel(x, positions, pre_mix_norm, w_in, q_norm, w_uq, kv_norm, w_ukv, conv_w, conv_b, conv_ln_g, conv_ln_b, conv_out_norm, attn_out_norm, w_out, post_mix_norm, pre_ffn_norm, w_gate, w_up, w_down, post_ffn_norm, loss_target, m_pre_mix_norm, m_w_in, m_q_norm, m_w_uq, m_kv_norm, m_w_ukv, m_conv_w, m_conv_b, m_conv_ln_g, m_conv_ln_b, m_conv_out_norm, m_attn_out_norm, m_w_out, m_post_mix_norm, m_pre_ffn_norm, m_w_gate, m_w_up, m_w_down, m_post_ffn_norm, v_pre_mix_norm, v_w_in, v_q_norm, v_w_uq, v_kv_norm, v_w_ukv, v_conv_w, v_conv_b, v_conv_ln_g, v_conv_ln_b, v_conv_out_norm, v_attn_out_norm, v_w_out, v_post_mix_norm, v_pre_ffn_norm, v_w_gate, v_w_up, v_w_down, v_post_ffn_norm):
    given = dict(locals())
    names = ["pre_mix_norm", "w_in", "q_norm", "w_uq", "kv_norm", "w_ukv", "conv_w", "conv_b", "conv_ln_g", "conv_ln_b",
             "conv_out_norm", "attn_out_norm", "w_out", "post_mix_norm", "pre_ffn_norm", "w_gate", "w_up", "w_down", "post_ffn_norm"]
    def as_2d(a):
        return a if a.ndim == 2 else a[0]

    weights = {n: as_2d(given[n]) for n in names}
    mom = {n: as_2d(given["m_" + n]) for n in names}
    var = {n: as_2d(given["v_" + n]) for n in names}
    d = D_MODEL

    inv_freq = ROPE_THETA ** (-jnp.arange(0, QK_ROPE, 2, dtype=F32) / QK_ROPE)
    ang = positions[0].astype(F32)[:, None] * inv_freq
    cos, sin = jnp.cos(ang), jnp.sin(ang)
    cos2 = jnp.concatenate([cos, cos, cos, cos], axis=1)
    sin2 = jnp.concatenate([-sin, sin, -sin, sin], axis=1)

    shards = [weights[n].astype(BF16) for n in BIG] + [jnp.pad(weights["conv_w"], ((0, CONV_K_PAD - CONV_K), (0, 0)))]
    gathered = _all_gather_shards(shards, [False] * len(BIG) + [True])
    full = _assemble_weights(dict(zip(BIG + ["conv_w"], gathered)))
    vec = {n: weights[n] for n in VEC_NAMES}

    loss, grad_x, dw, dvec = _local_step(x[0], loss_target[0], cos2, sin2, vec, full)

    rows = [_pad_lanes(dvec[n], d) for n in VEC_NAMES] + [_pad_lanes(loss, d)]
    rows.append(jnp.zeros((CONV_W_ROW - len(rows), d), F32))
    rows.append(dw["conv_w"].reshape(SMALL_ROWS - CONV_W_ROW, d))
    small = _all_reduce_small(jnp.concatenate(rows, axis=0))
    chip = 2 * lax.axis_index("x") + lax.axis_index("y")
    g_conv_w_full = small[CONV_W_ROW:].reshape(CONV_K_PAD, CONV_CH)
    g_small = {n: small[i:i + 1, :weights[n].shape[1]] for i, n in enumerate(VEC_NAMES)}
    g_small["conv_w"] = lax.dynamic_slice(g_conv_w_full, (0, chip * (CONV_CH // N_CHIPS)), (CONV_K_PAD, CONV_CH // N_CHIPS))[:CONV_K]
    loss_out = small[LOSS_ROW, 0]

    parts = _grads_to_parts(dw)
    plist = [parts[n] for n in BIG]
    landed = _pair_exchange(plist)
    core = lax.axis_index("c").astype(jnp.int32).reshape(1)
    sums = [_pair_sum(core, a, b, "grad_pair_sum_" + n) for n, a, b in zip(BIG, plist, landed)]
    slots = _chip_exchange(sums)
    halves = [_chip_sum(s, "grad_chip_sum_" + n) for n, s in zip(BIG, slots)]
    g_big = dict(zip(BIG, _half_exchange(halves)))

    res = {}
    for n in names:
        g = g_big[n] if n in g_big else g_small[n]
        res[n] = _adamw(g, weights[n], mom[n], var[n], "adamw_" + n)
    outs = [loss_out, grad_x[None]]
    for i in range(4):
        outs += [res[n][i].reshape(given[n].shape) for n in names]
    return tuple(outs)
```

```python
import functools

import jax
import jax.numpy as jnp
from jax import lax
from jax.experimental import pallas as pl
from jax.experimental.pallas import tpu as pltpu

F32 = jnp.float32
BF16 = jnp.bfloat16

D_MODEL = 2048
CONV_CH = 1024
CONV_K = 31
CONV_K_PAD = 32
N_HEADS = 8
QK_NOPE = 128
QK_ROPE = 64
V_HEAD = 128
QK_HEAD = QK_NOPE + QK_ROPE
Q_LORA = 768
KV_LORA = 512
ATTN_CH = N_HEADS * V_HEAD
Z2_COLS = Q_LORA + KV_LORA + 128
D_FF = 5632
ROPE_THETA = 10000.0
EPS = 1e-6
LANES = 128
N_CHIPS = 4
N_DEV = 8

ADAM_LR = 0.001
ADAM_B1 = 0.9
ADAM_B2 = 0.999
ADAM_EPS = 1e-08
ADAM_WD = 0.01
ADAM_STEP = 10

VMEM_LIMIT = 56 * 1024 * 1024
ROW_TILE = 256
MESH = pl.DeviceIdType.MESH


def _params(sem=None):
    return pltpu.CompilerParams(dimension_semantics=sem, vmem_limit_bytes=VMEM_LIMIT)


def _first_divisor(n, cands):
    for c in cands:
        if n % c == 0:
            return c
    return n


def _matmul(pairs, mode, out_dtype, name, b_parts=None, out_parts=False, tiles=(None, None, None)):
    a0, b0 = pairs[0]
    part_c = b0.shape[2] if b_parts else None
    if mode == "nn":
        m, n = a0.shape[0], (N_CHIPS * part_c if b_parts else b0.shape[1])
        ks = [a.shape[1] for a, _ in pairs]
    elif mode == "nt":
        m, n = a0.shape[0], b0.shape[-2]
        ks = [a.shape[1] for a, _ in pairs]
    else:
        m, n = a0.shape[1], b0.shape[1]
        ks = [a.shape[0] for a, _ in pairs]
    tm = tiles[0] or _first_divisor(m, (1024, 768, 512, 256))
    tn = tiles[1] or (n if n <= 1536 else _first_divisor(n, (1024, 512, 256, 128)))
    tks = [tiles[2] or (512 if k % 512 == 0 else k) for k in ks]
    nks = [k // tk for k, tk in zip(ks, tks)]
    offs = [sum(nks[:p]) for p in range(len(pairs))]
    nk = sum(nks)
    n_pairs = len(pairs)
    assert not (b_parts or out_parts) or n_pairs == 1

    def kk(k, p):
        return jnp.clip(k - offs[p], 0, nks[p] - 1)

    in_specs = []
    for p in range(n_pairs):
        tk = tks[p]
        if mode == "nn":
            in_specs.append(pl.BlockSpec((tm, tk), lambda i, j, k, p=p: (i, kk(k, p))))
            if b_parts == "n":
                per = part_c // tn
                in_specs.append(pl.BlockSpec((None, tk, tn), lambda i, j, k: (j // per, k, j % per)))
            else:
                in_specs.append(pl.BlockSpec((tk, tn), lambda i, j, k, p=p: (kk(k, p), j)))
        elif mode == "nt":
            in_specs.append(pl.BlockSpec((tm, tk), lambda i, j, k, p=p: (i, kk(k, p))))
            if b_parts == "k":
                per = part_c // tk
                in_specs.append(pl.BlockSpec((None, tn, tk), lambda i, j, k: (k // per, j, k % per)))
            else:
                in_specs.append(pl.BlockSpec((tn, tk), lambda i, j, k, p=p: (j, kk(k, p))))
        else:
            in_specs.append(pl.BlockSpec((tk, tm), lambda i, j, k, p=p: (kk(k, p), i)))
            in_specs.append(pl.BlockSpec((tk, tn), lambda i, j, k, p=p: (kk(k, p), j)))
    if out_parts:
        out_per = (n // N_CHIPS) // tn
        out_spec = pl.BlockSpec((None, tm, tn), lambda i, j, k: (j // out_per, i, j % out_per))
        out_shape = jax.ShapeDtypeStruct((N_CHIPS, m, n // N_CHIPS), out_dtype)
    else:
        out_spec = pl.BlockSpec((tm, tn), lambda i, j, k: (i, j))
        out_shape = jax.ShapeDtypeStruct((m, n), out_dtype)
    dims = {"nn": (((1,), (0,)), ((), ())), "nt": (((1,), (1,)), ((), ())), "tn": (((0,), (0,)), ((), ()))}[mode]

    def body(*refs):
        o_ref, acc = refs[2 * n_pairs], refs[2 * n_pairs + 1]
        k = pl.program_id(2)

        @pl.when(k == 0)
        def _():
            acc[...] = jnp.zeros_like(acc)

        for p in range(n_pairs):
            a_ref, b_ref = refs[2 * p], refs[2 * p + 1]

            def step(a_ref=a_ref, b_ref=b_ref):
                acc[...] += lax.dot_general(a_ref[...], b_ref[...], dims, preferred_element_type=F32)

            if n_pairs == 1:
                step()
            else:
                pl.when((k >= offs[p]) & (k < offs[p] + nks[p]))(step)

        @pl.when(k == nk - 1)
        def _():
            o_ref[...] = acc[...].astype(o_ref.dtype)

    flat = [t for pr in pairs for t in pr]
    return pl.pallas_call(
        body,
        name=name,
        grid=(m // tm, n // tn, nk),
        in_specs=in_specs,
        out_specs=out_spec,
        out_shape=out_shape,
        scratch_shapes=[pltpu.VMEM((tm, tn), F32)],
        compiler_params=_params(("parallel", "parallel", "arbitrary")),
    )(*flat)


F4 = D_FF // N_CHIPS
FFN_TM = 1024
FFN_TK = 512


def _ffn_up(hf, w_gu):
    t, d = hf.shape
    nk = d // FFN_TK

    def body(a_ref, b_ref, gu_ref, act_ref, acc):
        k = pl.program_id(2)

        @pl.when(k == 0)
        def _():
            acc[...] = jnp.zeros_like(acc)

        acc[...] += jnp.dot(a_ref[...], b_ref[...], preferred_element_type=F32)

        @pl.when(k == nk - 1)
        def _():
            g = acc[:, :F4]
            gu_ref[...] = acc[...].astype(gu_ref.dtype)
            act_ref[...] = (g * _sigmoid(g) * acc[:, F4:]).astype(act_ref.dtype)

    return pl.pallas_call(
        body,
        name="ffn_up",
        grid=(t // FFN_TM, N_CHIPS, nk),
        in_specs=[pl.BlockSpec((FFN_TM, FFN_TK), lambda i, q, k: (i, k)),
                  pl.BlockSpec((None, FFN_TK, 2 * F4), lambda i, q, k: (q, k, 0))],
        out_specs=[pl.BlockSpec((FFN_TM, 2 * F4), lambda i, q, k: (i, q)),
                   pl.BlockSpec((FFN_TM, F4), lambda i, q, k: (i, q))],
        out_shape=[jax.ShapeDtypeStruct((t, 2 * D_FF), BF16), jax.ShapeDtypeStruct((t, D_FF), BF16)],
        scratch_shapes=[pltpu.VMEM((FFN_TM, 2 * F4), F32)],
        compiler_params=_params(("parallel", "parallel", "arbitrary")),
    )(hf, w_gu)


def _ffn_down_dx(d_ff, w_down, gu):
    t, d = d_ff.shape
    nk = d // FFN_TK

    def body(a_ref, b_ref, gu_ref, o_ref, acc):
        k = pl.program_id(2)

        @pl.when(k == 0)
        def _():
            acc[...] = jnp.zeros_like(acc)

        acc[...] += lax.dot_general(a_ref[...], b_ref[...], NT_DIMS, preferred_element_type=F32)

        @pl.when(k == nk - 1)
        def _():
            g = gu_ref[:, :F4].astype(F32)
            u = gu_ref[:, F4:].astype(F32)
            sg = _sigmoid(g)
            d_act = acc[...]
            o_ref[:, :F4] = (d_act * u * sg * (1.0 + g * (1.0 - sg))).astype(o_ref.dtype)
            o_ref[:, F4:] = (d_act * g * sg).astype(o_ref.dtype)

    return pl.pallas_call(
        body,
        name="ffn_down_dx",
        grid=(t // FFN_TM, N_CHIPS, nk),
        in_specs=[pl.BlockSpec((FFN_TM, FFN_TK), lambda i, q, k: (i, k)),
                  pl.BlockSpec((F4, FFN_TK), lambda i, q, k: (q, k)),
                  pl.BlockSpec((FFN_TM, 2 * F4), lambda i, q, k: (i, q))],
        out_specs=pl.BlockSpec((FFN_TM, 2 * F4), lambda i, q, k: (i, q)),
        out_shape=jax.ShapeDtypeStruct((t, 2 * D_FF), BF16),
        scratch_shapes=[pltpu.VMEM((FFN_TM, F4), F32)],
        compiler_params=_params(("parallel", "parallel", "arbitrary")),
    )(d_ff, w_down, gu)


def _rowwise(fn, row_ins, vec_ins, row_outs, acc_outs, name):
    t = row_ins[0].shape[0]
    tm = ROW_TILE
    n_in = len(row_ins) + len(vec_ins)
    n_row = len(row_outs)

    def body(*refs):
        ins = [r[...] for r in refs[:n_in]]
        outs = refs[n_in:]
        vals = fn(*ins)
        for r, v in zip(outs[:n_row], vals[:n_row]):
            r[...] = v.astype(r.dtype)
        if acc_outs:
            @pl.when(pl.program_id(0) == 0)
            def _():
                for r in outs[n_row:]:
                    r[...] = jnp.zeros_like(r)

            for r, v in zip(outs[n_row:], vals[n_row:]):
                r[...] += v

    in_specs = [pl.BlockSpec((tm, a.shape[1]), lambda i: (i, 0)) for a in row_ins]
    in_specs += [pl.BlockSpec(a.shape, lambda i: (0, 0)) for a in vec_ins]
    out_specs = [pl.BlockSpec((tm, c), lambda i: (i, 0)) for c, _ in row_outs]
    out_specs += [pl.BlockSpec((1, c), lambda i: (0, 0)) for c in acc_outs]
    out_shape = [jax.ShapeDtypeStruct((t, c), dt) for c, dt in row_outs]
    out_shape += [jax.ShapeDtypeStruct((1, c), F32) for c in acc_outs]
    return pl.pallas_call(
        body,
        name=name,
        grid=(t // tm,),
        in_specs=in_specs,
        out_specs=out_specs,
        out_shape=out_shape,
        compiler_params=_params(("arbitrary",)),
    )(*row_ins, *vec_ins)


def _mean(v):
    return jnp.mean(v, axis=-1, keepdims=True)


def _colsum(v):
    return jnp.sum(v, axis=0, keepdims=True)


def _rms_fwd(v, g):
    r = lax.rsqrt(_mean(v * v) + EPS)
    vhat = v * r
    return vhat * g, vhat, r


def _rms_bwd(dn, vhat, r, g):
    dng = dn * g
    return r * (dng - vhat * _mean(dng * vhat)), _colsum(dn * vhat)


def _swap_rope_halves(v):
    n = v.shape[-1]
    lane = lax.broadcasted_iota(jnp.int32, v.shape, v.ndim - 1)
    return jnp.where(lane % QK_ROPE < QK_ROPE // 2, pltpu.roll(v, n - QK_ROPE // 2, v.ndim - 1),
                     pltpu.roll(v, QK_ROPE // 2, v.ndim - 1))


def _rope(v, cos2, sin2):
    return v * cos2 + _swap_rope_halves(v) * sin2


def _rope_transposed(dv, cos2, sin2):
    return dv * cos2 + _swap_rope_halves(dv * sin2)


def _sigmoid(v):
    return 1.0 / (1.0 + jnp.exp(-v))


CONV_ROWS = 256


def _conv_fwd(ag, conv_w, conv_b):
    t = ag.shape[0]
    cb = LANES

    def body(ag_ref, w_ref, b_ref, o_ref, scr):
        a = ag_ref[:, :cb].astype(F32)
        g = ag_ref[:, cb:].astype(F32)
        scr[pl.ds(0, CONV_K_PAD), :] = jnp.zeros((CONV_K_PAD, cb), F32)
        scr[pl.ds(CONV_K_PAD, t), :] = a * _sigmoid(g)
        for r0 in range(0, t, CONV_ROWS):
            acc = jnp.zeros((CONV_ROWS, cb), F32) + b_ref[...]
            for k in range(CONV_K):
                acc = acc + w_ref[k:k + 1, :] * scr[pl.ds(r0 + CONV_K_PAD - (CONV_K - 1) + k, CONV_ROWS), :]
            o_ref[pl.ds(r0, CONV_ROWS), :] = acc

    return pl.pallas_call(
        body,
        name="conv_fwd",
        grid=(CONV_CH // cb,),
        in_specs=[pl.BlockSpec((t, 2 * cb), lambda j: (0, j)),
                  pl.BlockSpec((CONV_K_PAD, cb), lambda j: (0, j)),
                  pl.BlockSpec((1, cb), lambda j: (0, j))],
        out_specs=pl.BlockSpec((t, cb), lambda j: (0, j)),
        out_shape=jax.ShapeDtypeStruct((t, CONV_CH), F32),
        scratch_shapes=[pltpu.VMEM((t + CONV_K_PAD, cb), F32)],
        compiler_params=_params(("parallel",)),
    )(ag, conv_w, conv_b)


def _conv_bwd(d_u1, ag, conv_w):
    t = ag.shape[0]
    cb = LANES

    def body(du_ref, ag_ref, w_ref, dag_ref, dw_ref, db_ref, su, sd):
        a = ag_ref[:, :cb].astype(F32)
        g = ag_ref[:, cb:].astype(F32)
        sg = _sigmoid(g)
        su[pl.ds(0, CONV_K_PAD), :] = jnp.zeros((CONV_K_PAD, cb), F32)
        su[pl.ds(CONV_K_PAD, t), :] = a * sg
        sd[pl.ds(0, t), :] = du_ref[...]
        sd[pl.ds(t, CONV_K_PAD), :] = jnp.zeros((CONV_K_PAD, cb), F32)
        db_ref[...] = _colsum(du_ref[...])
        dw_ref[...] = jnp.zeros_like(dw_ref)
        for r0 in range(0, t, CONV_ROWS):
            du = sd[pl.ds(r0, CONV_ROWS), :]
            acc = jnp.zeros((CONV_ROWS, cb), F32)
            for k in range(CONV_K):
                acc = acc + w_ref[k:k + 1, :] * sd[pl.ds(r0 + (CONV_K - 1) - k, CONV_ROWS), :]
                dw_ref[k:k + 1, :] += _colsum(du * su[pl.ds(r0 + CONV_K_PAD - (CONV_K - 1) + k, CONV_ROWS), :])
            sgc = sg[r0:r0 + CONV_ROWS]
            ac = a[r0:r0 + CONV_ROWS]
            dag_ref[pl.ds(r0, CONV_ROWS), :cb] = (acc * sgc).astype(dag_ref.dtype)
            dag_ref[pl.ds(r0, CONV_ROWS), cb:] = (acc * ac * sgc * (1.0 - sgc)).astype(dag_ref.dtype)

    return pl.pallas_call(
        body,
        name="conv_bwd",
        grid=(CONV_CH // cb,),
        in_specs=[pl.BlockSpec((t, cb), lambda j: (0, j)),
                  pl.BlockSpec((t, 2 * cb), lambda j: (0, j)),
                  pl.BlockSpec((CONV_K_PAD, cb), lambda j: (0, j))],
        out_specs=[pl.BlockSpec((t, 2 * cb), lambda j: (0, j)),
                   pl.BlockSpec((CONV_K_PAD, cb), lambda j: (0, j)),
                   pl.BlockSpec((1, cb), lambda j: (0, j))],
        out_shape=[jax.ShapeDtypeStruct((t, 2 * CONV_CH), BF16),
                   jax.ShapeDtypeStruct((CONV_K_PAD, CONV_CH), F32),
                   jax.ShapeDtypeStruct((1, CONV_CH), F32)],
        scratch_shapes=[pltpu.VMEM((t + CONV_K_PAD, cb), F32), pltpu.VMEM((t + CONV_K_PAD, cb), F32)],
        compiler_params=_params(("parallel",)),
    )(d_u1, ag, conv_w)


ATT_TQ = 256
NEG = float(jnp.finfo(jnp.float32).min)
SCALE = QK_HEAD ** -0.5
NT_DIMS = (((1,), (1,)), ((), ()))
TN_DIMS = (((0,), (0,)), ((), ()))


def _att_probs(qf, kf, row0):
    s = lax.dot_general(qf, kf, NT_DIMS, preferred_element_type=F32) * SCALE
    tq, t = s.shape
    qpos = row0 + lax.broadcasted_iota(jnp.int32, (tq, t), 0)
    kpos = lax.broadcasted_iota(jnp.int32, (tq, t), 1)
    s = jnp.where(kpos <= qpos, s, NEG)
    p = jnp.exp(s - jnp.max(s, axis=-1, keepdims=True))
    return p * (1.0 / jnp.sum(p, axis=-1, keepdims=True))


def _half_mask(shape, which):
    lane = lax.broadcasted_iota(jnp.int32, shape, len(shape) - 1)
    return (lane // QK_ROPE == which).astype(F32)


def _attention_fwd(q, kv, kpe2, cos2, sin2):
    t = q.shape[0]
    tq = ATT_TQ

    def body(qn_ref, qp_ref, c_ref, s_ref, kv_ref, kpe_ref, o_ref):
        row0 = pl.program_id(1) * tq
        roped = _rope(qp_ref[...].astype(F32), c_ref[...], s_ref[...])
        for e in range(2):
            qf = jnp.concatenate([qn_ref[:, e * QK_NOPE:(e + 1) * QK_NOPE],
                                  (roped * _half_mask(roped.shape, e)).astype(BF16)], axis=1)
            kf = jnp.concatenate([kv_ref[:, e * 256:e * 256 + QK_NOPE], kpe_ref[...]], axis=1)
            p = _att_probs(qf, kf, row0)
            v = kv_ref[:, e * 256 + QK_NOPE:(e + 1) * 256]
            o_ref[:, e * V_HEAD:(e + 1) * V_HEAD] = jnp.dot(p.astype(BF16), v, preferred_element_type=F32).astype(o_ref.dtype)

    return pl.pallas_call(
        body,
        name="attention_fwd",
        grid=(N_HEADS // 2, t // tq),
        in_specs=[pl.BlockSpec((tq, 2 * QK_NOPE), lambda h, i: (i, h)),
                  pl.BlockSpec((tq, LANES), lambda h, i: (i, N_HEADS + h)),
                  pl.BlockSpec((tq, LANES), lambda h, i: (i, 0)),
                  pl.BlockSpec((tq, LANES), lambda h, i: (i, 0)),
                  pl.BlockSpec((t, 512), lambda h, i: (0, h)),
                  pl.BlockSpec((t, LANES), lambda h, i: (0, 0))],
        out_specs=pl.BlockSpec((tq, 2 * V_HEAD), lambda h, i: (i, h)),
        out_shape=jax.ShapeDtypeStruct((t, ATTN_CH), BF16),
        compiler_params=_params(("parallel", "parallel")),
    )(q, q, cos2, sin2, kv, kpe2)


def _attention_bwd(q, kv, kpe2, cos2, sin2, d_attn):
    t = q.shape[0]
    tq = ATT_TQ
    n_q = t // tq

    def body(qn_ref, qp_ref, c_ref, s_ref, kv_ref, kpe_ref, do_ref, dqn_ref, dqp_ref, dkv_ref, dkpe_ref, dkv_acc):
        h, i = pl.program_id(0), pl.program_id(1)
        row0 = i * tq

        @pl.when(i == 0)
        def _():
            dkv_acc[...] = jnp.zeros_like(dkv_acc)

        @pl.when((i == 0) & (h == 0))
        def _():
            dkpe_ref[...] = jnp.zeros_like(dkpe_ref)

        roped = _rope(qp_ref[...].astype(F32), c_ref[...], s_ref[...])
        d_roped = jnp.zeros((tq, LANES), F32)
        for e in range(2):
            mask = _half_mask(roped.shape, e)
            qf = jnp.concatenate([qn_ref[:, e * QK_NOPE:(e + 1) * QK_NOPE], (roped * mask).astype(BF16)], axis=1)
            kf = jnp.concatenate([kv_ref[:, e * 256:e * 256 + QK_NOPE], kpe_ref[...]], axis=1)
            v = kv_ref[:, e * 256 + QK_NOPE:(e + 1) * 256]
            do = do_ref[:, e * V_HEAD:(e + 1) * V_HEAD]
            p = _att_probs(qf, kf, row0)
            dp = lax.dot_general(do, v, NT_DIMS, preferred_element_type=F32)
            ds = (p * (dp - jnp.sum(p * dp, axis=-1, keepdims=True)) * SCALE).astype(BF16)
            dqf = jnp.dot(ds, kf, preferred_element_type=F32)
            dkf = lax.dot_general(ds, qf, TN_DIMS, preferred_element_type=F32)
            dv = lax.dot_general(p.astype(BF16), do, TN_DIMS, preferred_element_type=F32)
            dqn_ref[:, e * QK_NOPE:(e + 1) * QK_NOPE] = dqf[:, :QK_NOPE].astype(dqn_ref.dtype)
            d_roped = d_roped + dqf[:, QK_NOPE:] * mask
            dkv_acc[:, e * 256:e * 256 + QK_NOPE] += dkf[:, :QK_NOPE]
            dkv_acc[:, e * 256 + QK_NOPE:(e + 1) * 256] += dv
            dkpe_ref[...] += dkf[:, QK_NOPE:]
        dqp_ref[...] = _rope_transposed(d_roped, c_ref[...], s_ref[...]).astype(dqp_ref.dtype)

        @pl.when(i == n_q - 1)
        def _():
            dkv_ref[...] = dkv_acc[...].astype(dkv_ref.dtype)

    return pl.pallas_call(
        body,
        name="attention_bwd",
        grid=(N_HEADS // 2, n_q),
        in_specs=[pl.BlockSpec((tq, 2 * QK_NOPE), lambda h, i: (i, h)),
                  pl.BlockSpec((tq, LANES), lambda h, i: (i, N_HEADS + h)),
                  pl.BlockSpec((tq, LANES), lambda h, i: (i, 0)),
                  pl.BlockSpec((tq, LANES), lambda h, i: (i, 0)),
                  pl.BlockSpec((t, 512), lambda h, i: (0, h)),
                  pl.BlockSpec((t, LANES), lambda h, i: (0, 0)),
                  pl.BlockSpec((tq, 2 * V_HEAD), lambda h, i: (i, h))],
        out_specs=[pl.BlockSpec((tq, 2 * QK_NOPE), lambda h, i: (i, h)),
                   pl.BlockSpec((tq, LANES), lambda h, i: (i, h)),
                   pl.BlockSpec((t, 512), lambda h, i: (0, h)),
                   pl.BlockSpec((t, LANES), lambda h, i: (0, 0))],
        out_shape=[jax.ShapeDtypeStruct((t, N_HEADS * QK_NOPE), BF16),
                   jax.ShapeDtypeStruct((t, N_HEADS * QK_ROPE), BF16),
                   jax.ShapeDtypeStruct((t, N_HEADS * 256), BF16),
                   jax.ShapeDtypeStruct((t, LANES), F32)],
        scratch_shapes=[pltpu.VMEM((t, 512), F32)],
        compiler_params=_params(("arbitrary", "arbitrary")),
    )(q, q, cos2, sin2, kv, kpe2, d_attn)


def _local_step(x, target, cos2, sin2, vec, w):
    d = D_MODEL

    (h,) = _rowwise(lambda xv, g: (_rms_fwd(xv, g)[0],), [x], [vec["pre_mix_norm"]], [(d, BF16)], [], "pre_mix_norm_fwd")
    ag = _matmul([(h, w["w_ag"])], "nn", BF16, "in_proj_ag")
    z2 = _matmul([(h, w["w_z2"])], "nn", BF16, "in_proj_z2")
    u1 = _conv_fwd(ag, w["conv_w"], vec["conv_b"])

    def latents_fwd(z, c2, s2, qg, kvg):
        z = z.astype(F32)
        qn = _rms_fwd(z[:, :Q_LORA], qg)[0]
        kvn = _rms_fwd(z[:, Q_LORA:Q_LORA + KV_LORA], kvg)[0]
        kr = z[:, Q_LORA + KV_LORA:]
        kr2 = kr + pltpu.roll(kr, QK_ROPE, 1)
        return qn, kvn, _rope(kr2, c2, s2)

    qn, kvn, kpe2 = _rowwise(latents_fwd, [z2, cos2, sin2], [vec["q_norm"], vec["kv_norm"]],
                             [(Q_LORA, BF16), (KV_LORA, BF16), (LANES, BF16)], [], "latents_fwd")
    q = _matmul([(qn, w["w_uq"])], "nn", BF16, "q_up")
    kv = _matmul([(kvn, w["w_ukv"])], "nn", BF16, "kv_up")
    attn = _attention_fwd(q, kv, kpe2, cos2, sin2)

    def conv_post(u, lg, lb):
        mu = _mean(u)
        uc = u - mu
        rstd = lax.rsqrt(_mean(uc * uc) + EPS)
        uhat = uc * rstd
        u2 = uhat * lg + lb
        sg = _sigmoid(u2)
        return uhat, rstd, u2, sg, u2 * sg

    def mix_in_fwd(u, at, lg, lb, cg, ag_):
        u3 = conv_post(u, lg, lb)[4]
        cn = _rms_fwd(u3, cg)[0]
        an = _rms_fwd(at.astype(F32), ag_)[0]
        return (jnp.concatenate([cn, an], axis=1),)

    (cat,) = _rowwise(mix_in_fwd, [u1, attn], [vec["conv_ln_g"], vec["conv_ln_b"], vec["conv_out_norm"], vec["attn_out_norm"]],
                      [(2 * CONV_CH, BF16)], [], "mix_in_fwd")
    mix = _matmul([(cat, w["w_out"])], "nn", F32, "out_proj")

    def residual1(xv, mv, gpm, gpf):
        x1 = xv + _rms_fwd(mv, gpm)[0]
        return x1, _rms_fwd(x1, gpf)[0]

    x1, hf = _rowwise(residual1, [x, mix], [vec["post_mix_norm"], vec["pre_ffn_norm"]], [(d, F32), (d, BF16)], [], "residual1_fwd")
    gu, act = _ffn_up(hf, w["w_gu"])
    ff = _matmul([(act, w["w_down"])], "nn", F32, "ffn_down")

    def loss_head(x1v, ffv, tg, g):
        n, fhat, r = _rms_fwd(ffv, g)
        err = x1v + n - tg
        loss = 0.5 * jnp.sum(_mean(err * err), axis=0, keepdims=True)
        dy = err * (1.0 / d)
        d_ff, dg = _rms_bwd(dy, fhat, r, g)
        return dy, d_ff, dg, jnp.broadcast_to(loss, (1, LANES))

    dy, d_ff, g_post_ffn, loss = _rowwise(loss_head, [x1, ff, target], [vec["post_ffn_norm"]],
                                          [(d, F32), (d, BF16)], [d, LANES], "loss_head")
    d_gu = _ffn_down_dx(d_ff, w["w_down"], gu)
    dw_down = _matmul([(act, d_ff)], "tn", BF16, "ffn_down_dw", tiles=(F4, None, None))
    d_hf = _matmul([(d_gu, w["w_gu"])], "nt", F32, "ffn_gate_up_dx", b_parts="k", tiles=(None, None, F4))
    dw_gu = _matmul([(hf, d_gu)], "tn", BF16, "ffn_gate_up_dw", out_parts=True, tiles=(None, F4, None))

    def residual1_bwd(dyv, dhf, x1v, mv, gpf, gpm):
        _, x1hat, r1 = _rms_fwd(x1v, gpf)
        dn, dgpf = _rms_bwd(dhf, x1hat, r1, gpf)
        d_x1 = dyv + dn
        _, mhat, rm = _rms_fwd(mv, gpm)
        d_mix, dgpm = _rms_bwd(d_x1, mhat, rm, gpm)
        return d_x1, d_mix, dgpf, dgpm

    d_x1, d_mix, g_pre_ffn, g_post_mix = _rowwise(residual1_bwd, [dy, d_hf, x1, mix], [vec["pre_ffn_norm"], vec["post_mix_norm"]],
                                                  [(d, F32), (d, BF16)], [d, d], "residual1_bwd")
    d_cat = _matmul([(d_mix, w["w_out"])], "nt", BF16, "out_proj_dx")
    dw_out = _matmul([(cat, d_mix)], "tn", BF16, "out_proj_dw")

    def mix_in_bwd(dc, u, at, lg, lb, cg, ag_):
        dc = dc.astype(F32)
        uhat, rstd, u2, sg, u3 = conv_post(u, lg, lb)
        _, u3hat, rc = _rms_fwd(u3, cg)
        d_u3, dcg = _rms_bwd(dc[:, :CONV_CH], u3hat, rc, cg)
        d_u2 = d_u3 * sg * (1.0 + u2 * (1.0 - sg))
        dgl = d_u2 * lg
        d_u1 = rstd * (dgl - _mean(dgl) - uhat * _mean(dgl * uhat))
        _, ahat, ra = _rms_fwd(at.astype(F32), ag_)
        d_at, dag = _rms_bwd(dc[:, CONV_CH:], ahat, ra, ag_)
        return d_u1, d_at, dcg, _colsum(d_u2 * uhat), _colsum(d_u2), dag

    d_u1, d_attn, g_conv_out, g_ln_g, g_ln_b, g_attn_out = _rowwise(
        mix_in_bwd, [d_cat, u1, attn], [vec["conv_ln_g"], vec["conv_ln_b"], vec["conv_out_norm"], vec["attn_out_norm"]],
        [(CONV_CH, F32), (ATTN_CH, BF16)], [CONV_CH] * 4, "mix_in_bwd")
    d_ag, d_conv_w, g_conv_b = _conv_bwd(d_u1, ag, w["conv_w"])
    d_qn_, d_qp_, d_kv, d_kpe2 = _attention_bwd(q, kv, kpe2, cos2, sin2, d_attn)
    d_q = jnp.concatenate([d_qn_, d_qp_], axis=1)
    d_qn = _matmul([(d_q, w["w_uq"])], "nt", BF16, "q_up_dx")
    dw_uq = _matmul([(qn, d_q)], "tn", BF16, "q_up_dw")
    d_kvn = _matmul([(d_kv, w["w_ukv"])], "nt", BF16, "kv_up_dx")
    dw_ukv = _matmul([(kvn, d_kv)], "tn", BF16, "kv_up_dw")

    def latents_bwd(z, dq, dk, dkp, c2, s2, qg, kvg):
        z = z.astype(F32)
        _, qhat, rq = _rms_fwd(z[:, :Q_LORA], qg)
        d_ql, dqg = _rms_bwd(dq.astype(F32), qhat, rq, qg)
        _, khat, rk = _rms_fwd(z[:, Q_LORA:Q_LORA + KV_LORA], kvg)
        d_kl, dkg = _rms_bwd(dk.astype(F32), khat, rk, kvg)
        both = dkp + pltpu.roll(dkp, QK_ROPE, 1)
        d_kr = _rope_transposed(both, c2, s2) * _half_mask(both.shape, 0)
        return jnp.concatenate([d_ql, d_kl, d_kr], axis=1), dqg, dkg

    d_z2, g_q_norm, g_kv_norm = _rowwise(latents_bwd, [z2, d_qn, d_kvn, d_kpe2, cos2, sin2], [vec["q_norm"], vec["kv_norm"]],
                                         [(Z2_COLS, BF16)], [Q_LORA, KV_LORA], "latents_bwd")
    d_h = _matmul([(d_ag, w["w_ag"]), (d_z2, w["w_z2"])], "nt", F32, "in_proj_dx")
    dw_ag = _matmul([(h, d_ag)], "tn", BF16, "in_proj_ag_dw")
    dw_z2 = _matmul([(h, d_z2)], "tn", BF16, "in_proj_z2_dw")

    def pre_mix_bwd(dx1, dh, xv, g):
        _, xhat, r = _rms_fwd(xv, g)
        dn, dg = _rms_bwd(dh, xhat, r, g)
        return dx1 + dn, dg

    grad_x, g_pre_mix = _rowwise(pre_mix_bwd, [d_x1, d_h, x], [vec["pre_mix_norm"]], [(d, F32)], [d], "pre_mix_norm_bwd")

    dw = dict(w_ag=dw_ag, w_z2=dw_z2, w_uq=dw_uq, w_ukv=dw_ukv, conv_w=d_conv_w, w_out=dw_out, w_gu=dw_gu, w_down=dw_down)
    dvec = dict(pre_mix_norm=g_pre_mix, q_norm=g_q_norm, kv_norm=g_kv_norm, conv_b=g_conv_b, conv_ln_g=g_ln_g,
                conv_ln_b=g_ln_b, conv_out_norm=g_conv_out, attn_out_norm=g_attn_out, post_mix_norm=g_post_mix,
                pre_ffn_norm=g_pre_ffn, post_ffn_norm=g_post_ffn)
    return loss, grad_x, dw, dvec


ANY = pl.BlockSpec(memory_space=pl.ANY)


def _place():
    x, y, c = lax.axis_index("x"), lax.axis_index("y"), lax.axis_index("c")
    chips = [(1 - x, y), (x, 1 - y), (1 - x, 1 - y)]
    return x, y, c, chips


def _to_parts(chip, pieces, dtype, name):
    r = pieces[0].shape[0]
    widths = [a.shape[1] for a in pieces]
    tr = r if r <= 512 else _first_divisor(r, (512, 256, 128))

    def body(p_ref, *refs):
        o_ref = refs[len(pieces)]
        off = 0
        for a_ref, wdt in zip(refs, widths):
            o_ref[:, off:off + wdt] = a_ref[...].astype(o_ref.dtype)
            off += wdt

    return pl.pallas_call(
        body,
        name=name,
        grid_spec=pltpu.PrefetchScalarGridSpec(
            num_scalar_prefetch=1,
            grid=(r // tr,),
            in_specs=[pl.BlockSpec((tr, wdt), lambda i, p_ref: (i, 0)) for wdt in widths],
            out_specs=pl.BlockSpec((None, tr, sum(widths)), lambda i, p_ref: (p_ref[0], i, 0))),
        out_shape=jax.ShapeDtypeStruct((N_CHIPS, r, sum(widths)), dtype),
        compiler_params=_params(("parallel",)),
    )(chip, *pieces)


def _all_gather_parts(bufs, whole):
    n = len(bufs)
    n_split = sum(1 for f in whole if not f)

    def body(*refs):
        outs = refs[n:2 * n]
        send, recv, fsend, frecv = refs[2 * n:]
        x, y, c, chips = _place()
        p = 2 * x + y

        def rows(k, half):
            r = bufs[k].shape[1]
            return pl.ds(0, r) if whole[k] else pl.ds(half * (r // 2), r // 2)

        def ici(k, j, part):
            px, py = chips[j]
            blk = outs[k].at[part, rows(k, c), :]
            return pltpu.make_async_remote_copy(src_ref=blk, dst_ref=blk, send_sem=send.at[3 * k + j], recv_sem=recv.at[3 * k + j],
                                                device_id=(px, py, c), device_id_type=MESH)

        def d2d(s, k, j, half):
            px, py = chips[j]
            blk = outs[k].at[2 * px + py, rows(k, half), :]
            return pltpu.make_async_remote_copy(src_ref=blk, dst_ref=blk, send_sem=fsend.at[3 * s + j], recv_sem=frecv.at[3 * s + j],
                                                device_id=(x, y, 1 - c), device_id_type=MESH)

        sent = [ici(k, j, p) for k in range(n) for j in range(3)]
        for cp in sent:
            cp.start()
        handed = []
        s = 0
        for k in range(n):
            for j in range(3):
                px, py = chips[j]
                ici(k, j, 2 * px + py).wait_recv()
                if not whole[k]:
                    cp = d2d(s, k, j, c)
                    cp.start()
                    handed.append(cp)
            if not whole[k]:
                s += 1
        s = 0
        for k in range(n):
            if whole[k]:
                continue
            for j in range(3):
                d2d(s, k, j, 1 - c).wait_recv()
            s += 1
        for cp in sent + handed:
            cp.wait_send()

    return pl.pallas_call(
        body,
        name="all_gather_weights",
        in_specs=[ANY] * n,
        out_specs=[ANY] * n,
        out_shape=[jax.ShapeDtypeStruct(a.shape, a.dtype) for a in bufs],
        input_output_aliases={k: k for k in range(n)},
        scratch_shapes=[pltpu.SemaphoreType.DMA((3 * n,)), pltpu.SemaphoreType.DMA((3 * n,)),
                        pltpu.SemaphoreType.DMA((3 * n_split,)), pltpu.SemaphoreType.DMA((3 * n_split,))],
        compiler_params=pltpu.CompilerParams(has_side_effects=True),
    )(*bufs)


def _pair_exchange(parts):
    n = len(parts)

    def body(*refs):
        ins, outs = refs[:n], refs[n:2 * n]
        send, recv = refs[2 * n:]
        x, y, c, _ = _place()
        copies = []
        for k in range(n):
            rh = parts[k].shape[1] // 2
            cp = pltpu.make_async_remote_copy(
                src_ref=ins[k].at[:, pl.ds((1 - c) * rh, rh), :], dst_ref=outs[k],
                send_sem=send.at[k], recv_sem=recv.at[k], device_id=(x, y, 1 - c), device_id_type=MESH)
            cp.start()
            copies.append(cp)
        for cp in copies:
            cp.wait()

    return pl.pallas_call(
        body,
        name="grad_pair_exchange",
        in_specs=[ANY] * n,
        out_specs=[ANY] * n,
        out_shape=[jax.ShapeDtypeStruct((N_CHIPS, a.shape[1] // 2, a.shape[2]), a.dtype) for a in parts],
        scratch_shapes=[pltpu.SemaphoreType.DMA((n,)), pltpu.SemaphoreType.DMA((n,))],
        compiler_params=pltpu.CompilerParams(has_side_effects=True),
    )(*parts)


def _pair_sum(core, part, landed, name):
    _, r, cdim = part.shape
    rh = r // 2
    tr = _first_divisor(rh, (256, 128, 64, 32, 16))
    nb = rh // tr

    def body(c_ref, a_ref, b_ref, o_ref):
        o_ref[...] = (a_ref[...].astype(F32) + b_ref[...].astype(F32)).astype(o_ref.dtype)

    return pl.pallas_call(
        body,
        name=name,
        grid_spec=pltpu.PrefetchScalarGridSpec(
            num_scalar_prefetch=1,
            grid=(N_CHIPS, nb),
            in_specs=[pl.BlockSpec((None, tr, cdim), lambda q, i, c_ref: (q, c_ref[0] * nb + i, 0)),
                      pl.BlockSpec((None, tr, cdim), lambda q, i, c_ref: (q, i, 0))],
            out_specs=pl.BlockSpec((None, tr, cdim), lambda q, i, c_ref: (q, i, 0))),
        out_shape=jax.ShapeDtypeStruct((N_CHIPS, rh, cdim), BF16),
        compiler_params=_params(("parallel", "parallel")),
    )(core, part, landed)


def _chip_exchange(sums):
    n = len(sums)

    def body(*refs):
        ins, outs = refs[:n], refs[n:2 * n]
        send, recv = refs[2 * n:]
        x, y, c, chips = _place()
        p = 2 * x + y

        def ici(k, j, slot):
            px, py = chips[j]
            return pltpu.make_async_remote_copy(
                src_ref=ins[k].at[2 * px + py], dst_ref=outs[k].at[slot],
                send_sem=send.at[3 * k + j], recv_sem=recv.at[3 * k + j],
                device_id=(px, py, c), device_id_type=MESH)

        sent = [ici(k, j, p) for k in range(n) for j in range(3)]
        for cp in sent:
            cp.start()
        for k in range(n):
            for j in range(3):
                px, py = chips[j]
                ici(k, j, 2 * px + py).wait_recv()
        for cp in sent:
            cp.wait_send()

    return pl.pallas_call(
        body,
        name="grad_chip_exchange",
        in_specs=[ANY] * n,
        out_specs=[ANY] * n,
        out_shape=[jax.ShapeDtypeStruct(a.shape, a.dtype) for a in sums],
        scratch_shapes=[pltpu.SemaphoreType.DMA((3 * n,)), pltpu.SemaphoreType.DMA((3 * n,))],
        compiler_params=pltpu.CompilerParams(has_side_effects=True),
    )(*sums)


def _chip_sum(place, own, slots, name):
    _, rh, cdim = slots.shape
    tr = _first_divisor(rh, (256, 128, 64, 32, 16))
    nb = rh // tr

    def body(place_ref, own_ref, s1_ref, s2_ref, s3_ref, o_ref):
        acc = own_ref[...].astype(F32)
        for s_ref in (s1_ref, s2_ref, s3_ref):
            acc = acc + s_ref[...].astype(F32)
        o_ref[...] = acc

    def other(j):
        return lambda i, place_ref: ((place_ref[0] + j) % N_CHIPS, i, 0)

    return pl.pallas_call(
        body,
        name=name,
        grid_spec=pltpu.PrefetchScalarGridSpec(
            num_scalar_prefetch=1,
            grid=(nb,),
            in_specs=[pl.BlockSpec((None, tr, cdim), other(0))] + [pl.BlockSpec((None, tr, cdim), other(j)) for j in (1, 2, 3)],
            out_specs=pl.BlockSpec((tr, cdim), lambda i, place_ref: (place_ref[1] * nb + i, 0))),
        out_shape=jax.ShapeDtypeStruct((2 * rh, cdim), F32),
        compiler_params=_params(("parallel",)),
    )(place, own, slots, slots, slots)


def _half_exchange(bufs):
    n = len(bufs)

    def body(*refs):
        outs = refs[n:2 * n]
        send, recv = refs[2 * n:]
        x, y, c, _ = _place()
        copies = []
        for k in range(n):
            rh = bufs[k].shape[0] // 2
            mine = outs[k].at[pl.ds(c * rh, rh), :]
            cp = pltpu.make_async_remote_copy(src_ref=mine, dst_ref=mine, send_sem=send.at[k], recv_sem=recv.at[k],
                                              device_id=(x, y, 1 - c), device_id_type=MESH)
            cp.start()
            copies.append(cp)
        for k in range(n):
            rh = bufs[k].shape[0] // 2
            theirs = outs[k].at[pl.ds((1 - c) * rh, rh), :]
            copies[k].wait_send()
            pltpu.make_async_remote_copy(src_ref=theirs, dst_ref=theirs, send_sem=send.at[k], recv_sem=recv.at[k],
                                         device_id=(x, y, 1 - c), device_id_type=MESH).wait_recv()

    return pl.pallas_call(
        body,
        name="grad_half_exchange",
        in_specs=[ANY] * n,
        out_specs=[ANY] * n,
        out_shape=[jax.ShapeDtypeStruct(a.shape, a.dtype) for a in bufs],
        input_output_aliases={k: k for k in range(n)},
        scratch_shapes=[pltpu.SemaphoreType.DMA((n,)), pltpu.SemaphoreType.DMA((n,))],
        compiler_params=pltpu.CompilerParams(has_side_effects=True),
    )(*bufs)


SMALL_ROWS = 32


def _all_reduce_small(pack):
    def body(in_ref, out_ref, gath, send, recv):
        x, y, c, _ = _place()
        me = 4 * x + 2 * y + c
        gath[me] = in_ref[...]
        copies = []
        for k in range(1, N_DEV):
            dx, dy, dc = (k >> 2) & 1, (k >> 1) & 1, k & 1
            peer = (x ^ dx, y ^ dy, c ^ dc)
            cp = pltpu.make_async_remote_copy(src_ref=in_ref, dst_ref=gath.at[me], send_sem=send.at[k], recv_sem=recv.at[k],
                                              device_id=peer, device_id_type=MESH)
            cp.start()
            copies.append((cp, 4 * peer[0] + 2 * peer[1] + peer[2]))
        for k, (cp, peer_id) in enumerate(copies, start=1):
            cp.wait_send()
            pltpu.make_async_remote_copy(src_ref=in_ref, dst_ref=gath.at[peer_id], send_sem=send.at[k], recv_sem=recv.at[k],
                                         device_id=(x, y, c), device_id_type=MESH).wait_recv()
        acc = gath[0]
        for dev in range(1, N_DEV):
            acc = acc + gath[dev]
        out_ref[...] = acc

    return pl.pallas_call(
        body,
        name="all_reduce_small",
        in_specs=[pl.BlockSpec(memory_space=pltpu.VMEM)],
        out_specs=pl.BlockSpec(memory_space=pltpu.VMEM),
        out_shape=jax.ShapeDtypeStruct(pack.shape, F32),
        scratch_shapes=[pltpu.VMEM((N_DEV,) + pack.shape, F32), pltpu.SemaphoreType.DMA((N_DEV,)), pltpu.SemaphoreType.DMA((N_DEV,))],
        compiler_params=pltpu.CompilerParams(has_side_effects=True, vmem_limit_bytes=VMEM_LIMIT),
    )(pack)


def _adamw(g, w, m, v, name, g_block=0):
    r, cdim = w.shape
    tr = r if r * cdim * 4 <= (1 << 20) else _first_divisor(r, (128, 64, 32, 16, 8))
    bc1 = 1.0 - ADAM_B1 ** ADAM_STEP
    bc2 = 1.0 - ADAM_B2 ** ADAM_STEP

    def body(g_ref, w_ref, m_ref, v_ref, go_ref, d_ref, mo_ref, vo_ref):
        gv = g_ref[...]
        mn = ADAM_B1 * m_ref[...] + (1.0 - ADAM_B1) * gv
        vn = ADAM_B2 * v_ref[...] + (1.0 - ADAM_B2) * (gv * gv)
        go_ref[...] = gv
        mo_ref[...] = mn
        vo_ref[...] = vn
        d_ref[...] = -ADAM_LR * ((mn / bc1) / (jnp.sqrt(vn / bc2) + ADAM_EPS) + ADAM_WD * w_ref[...])

    spec = pl.BlockSpec((tr, cdim), lambda i: (i, 0))
    return pl.pallas_call(
        body,
        name=name,
        grid=(r // tr,),
        in_specs=[pl.BlockSpec((tr, cdim), lambda i: (i, g_block))] + [spec] * 3,
        out_specs=[spec] * 4,
        out_shape=[jax.ShapeDtypeStruct((r, cdim), F32)] * 4,
        compiler_params=_params(("parallel",)),
    )(g, w, m, v)


VEC_NAMES = ["pre_mix_norm", "q_norm", "kv_norm", "conv_b", "conv_ln_g", "conv_ln_b", "conv_out_norm",
             "attn_out_norm", "post_mix_norm", "pre_ffn_norm", "post_ffn_norm"]
LOSS_ROW = len(VEC_NAMES)
CONV_W_ROW = 16


def _cols_to_full(parts):
    _, r, cdim = parts.shape
    return parts.transpose(1, 0, 2).reshape(r, N_CHIPS * cdim)


def _full_to_cols(full):
    r, n = full.shape
    return full.reshape(r, N_CHIPS, n // N_CHIPS).transpose(1, 0, 2)


def _assemble_weights(g):
    w_in = _cols_to_full(g["w_in"])
    d = w_in.shape[0]
    w_a = w_in[:, :CONV_CH].reshape(d, CONV_CH // LANES, 1, LANES)
    w_g = w_in[:, CONV_CH:2 * CONV_CH].reshape(d, CONV_CH // LANES, 1, LANES)
    w_ag = jnp.concatenate([w_a, w_g], axis=2).reshape(d, 2 * CONV_CH)
    w_z2 = jnp.concatenate([w_in[:, 2 * CONV_CH:], jnp.zeros((d, LANES - QK_ROPE), w_in.dtype)], axis=1)
    uq = _cols_to_full(g["w_uq"]).reshape(Q_LORA, N_HEADS, QK_HEAD)
    w_uq = jnp.concatenate([uq[:, :, :QK_NOPE].reshape(Q_LORA, N_HEADS * QK_NOPE),
                            uq[:, :, QK_NOPE:].reshape(Q_LORA, N_HEADS * QK_ROPE)], axis=1)
    return dict(w_ag=w_ag, w_z2=w_z2, w_uq=w_uq, w_ukv=_cols_to_full(g["w_ukv"]), conv_w=_cols_to_full(g["conv_w"]),
                w_out=g["w_out"].reshape(-1, g["w_out"].shape[2]), w_gu=g["w_gu"],
                w_down=g["w_down"].reshape(-1, g["w_down"].shape[2]))


def _grads_to_parts(dw):
    d = dw["w_ag"].shape[0]
    ag = dw["w_ag"].reshape(d, CONV_CH // LANES, 2, LANES)
    d_in = jnp.concatenate([ag[:, :, 0, :].reshape(d, CONV_CH), ag[:, :, 1, :].reshape(d, CONV_CH),
                            dw["w_z2"][:, :Q_LORA + KV_LORA + QK_ROPE]], axis=1)
    uq = dw["w_uq"]
    d_uq = jnp.concatenate([uq[:, :N_HEADS * QK_NOPE].reshape(Q_LORA, N_HEADS, QK_NOPE),
                            uq[:, N_HEADS * QK_NOPE:].reshape(Q_LORA, N_HEADS, QK_ROPE)], axis=2).reshape(Q_LORA, N_HEADS * QK_HEAD)
    return dict(w_in=_full_to_cols(d_in), w_uq=_full_to_cols(d_uq), w_ukv=_full_to_cols(dw["w_ukv"]),
                w_out=dw["w_out"].reshape(N_CHIPS, -1, dw["w_out"].shape[1]), w_gu=dw["w_gu"],
                w_down=dw["w_down"].reshape(N_CHIPS, -1, dw["w_down"].shape[1]))


BIG = ["w_in", "w_uq", "w_ukv", "w_out", "w_gu", "w_down"]


def _pad_lanes(v, n):
    return jnp.pad(v, ((0, 0), (0, n - v.shape[1])))


def kernel(x, positions, pre_mix_norm, w_in, q_norm, w_uq, kv_norm, w_ukv, conv_w, conv_b, conv_ln_g, conv_ln_b, conv_out_norm, attn_out_norm, w_out, post_mix_norm, pre_ffn_norm, w_gate, w_up, w_down, post_ffn_norm, loss_target, m_pre_mix_norm, m_w_in, m_q_norm, m_w_uq, m_kv_norm, m_w_ukv, m_conv_w, m_conv_b, m_conv_ln_g, m_conv_ln_b, m_conv_out_norm, m_attn_out_norm, m_w_out, m_post_mix_norm, m_pre_ffn_norm, m_w_gate, m_w_up, m_w_down, m_post_ffn_norm, v_pre_mix_norm, v_w_in, v_q_norm, v_w_uq, v_kv_norm, v_w_ukv, v_conv_w, v_conv_b, v_conv_ln_g, v_conv_ln_b, v_conv_out_norm, v_attn_out_norm, v_w_out, v_post_mix_norm, v_pre_ffn_norm, v_w_gate, v_w_up, v_w_down, v_post_ffn_norm):
    given = dict(locals())
    names = ["pre_mix_norm", "w_in", "q_norm", "w_uq", "kv_norm", "w_ukv", "conv_w", "conv_b", "conv_ln_g", "conv_ln_b",
             "conv_out_norm", "attn_out_norm", "w_out", "post_mix_norm", "pre_ffn_norm", "w_gate", "w_up", "w_down", "post_ffn_norm"]
    def as_2d(a):
        return a if a.ndim == 2 else a[0]

    weights = {n: as_2d(given[n]) for n in names}
    mom = {n: as_2d(given["m_" + n]) for n in names}
    var = {n: as_2d(given["v_" + n]) for n in names}
    d = D_MODEL

    inv_freq = ROPE_THETA ** (-jnp.arange(0, QK_ROPE, 2, dtype=F32) / QK_ROPE)
    ang = positions[0].astype(F32)[:, None] * inv_freq
    cos, sin = jnp.cos(ang), jnp.sin(ang)
    cos2 = jnp.concatenate([cos, cos, cos, cos], axis=1)
    sin2 = jnp.concatenate([-sin, sin, -sin, sin], axis=1)

    chip = 2 * lax.axis_index("x") + lax.axis_index("y")
    core = lax.axis_index("c")
    chip1 = chip.astype(jnp.int32).reshape(1)
    pieces = {n: [weights[n]] for n in BIG if n != "w_gu"}
    pieces["w_gu"] = [weights["w_gate"], weights["w_up"]]
    bufs = [_to_parts(chip1, pieces[n], BF16, "to_parts_" + n) for n in BIG]
    bufs.append(_to_parts(chip1, [jnp.pad(weights["conv_w"], ((0, CONV_K_PAD - CONV_K), (0, 0)))], F32, "to_parts_conv_w"))
    gathered = _all_gather_parts(bufs, [False] * len(BIG) + [True])
    full = _assemble_weights(dict(zip(BIG + ["conv_w"], gathered)))
    vec = {n: weights[n] for n in VEC_NAMES}

    loss, grad_x, dw, dvec = _local_step(x[0], loss_target[0], cos2, sin2, vec, full)

    rows = [_pad_lanes(dvec[n], d) for n in VEC_NAMES] + [_pad_lanes(loss, d)]
    rows.append(jnp.zeros((CONV_W_ROW - len(rows), d), F32))
    rows.append(dw["conv_w"].reshape(SMALL_ROWS - CONV_W_ROW, d))
    small = _all_reduce_small(jnp.concatenate(rows, axis=0))
    g_conv_w_full = small[CONV_W_ROW:].reshape(CONV_K_PAD, CONV_CH)
    g_small = {n: small[i:i + 1, :weights[n].shape[1]] for i, n in enumerate(VEC_NAMES)}
    g_small["conv_w"] = lax.dynamic_slice(g_conv_w_full, (0, chip * (CONV_CH // N_CHIPS)), (CONV_K_PAD, CONV_CH // N_CHIPS))[:CONV_K]
    loss_out = small[LOSS_ROW, 0]

    parts = _grads_to_parts(dw)
    plist = [parts[n] for n in BIG]
    landed = _pair_exchange(plist)
    core1 = core.astype(jnp.int32).reshape(1)
    place = jnp.stack([chip, core]).astype(jnp.int32)
    sums = [_pair_sum(core1, a, b, "grad_pair_sum_" + n) for n, a, b in zip(BIG, plist, landed)]
    slots = _chip_exchange(sums)
    halves = [_chip_sum(place, s, sl, "grad_chip_sum_" + n) for n, s, sl in zip(BIG, sums, slots)]
    g_big = dict(zip(BIG, _half_exchange(halves)))

    res = {}
    for n in names:
        if n in ("w_gate", "w_up"):
            g, blk = g_big["w_gu"], int(n == "w_up")
        else:
            g, blk = (g_big[n] if n in g_big else g_small[n]), 0
        res[n] = _adamw(g, weights[n], mom[n], var[n], "adamw_" + n, g_block=blk)
    outs = [loss_out, grad_x[None]]
    for i in range(4):
        outs += [res[n][i].reshape(given[n].shape) for n in names]
    return tuple(outs)
```

```python
import functools

import jax
import jax.numpy as jnp
from jax import lax
from jax.experimental import pallas as pl
from jax.experimental.pallas import tpu as pltpu

F32 = jnp.float32
BF16 = jnp.bfloat16

D_MODEL = 2048
CONV_CH = 1024
CONV_K = 31
CONV_K_PAD = 32
N_HEADS = 8
QK_NOPE = 128
QK_ROPE = 64
V_HEAD = 128
QK_HEAD = QK_NOPE + QK_ROPE
Q_LORA = 768
KV_LORA = 512
ATTN_CH = N_HEADS * V_HEAD
Z2_COLS = Q_LORA + KV_LORA + 128
D_FF = 5632
ROPE_THETA = 10000.0
EPS = 1e-6
LANES = 128
N_CHIPS = 4
N_DEV = 8

ADAM_LR = 0.001
ADAM_B1 = 0.9
ADAM_B2 = 0.999
ADAM_EPS = 1e-08
ADAM_WD = 0.01
ADAM_STEP = 10

VMEM_LIMIT = 56 * 1024 * 1024
ROW_TILE = 256
MESH = pl.DeviceIdType.MESH


def _params(sem=None):
    return pltpu.CompilerParams(dimension_semantics=sem, vmem_limit_bytes=VMEM_LIMIT)


def _first_divisor(n, cands):
    for c in cands:
        if n % c == 0:
            return c
    return n


def _matmul(pairs, mode, out_dtype, name, b_parts=None, out_parts=False, tiles=(None, None, None), after=None):
    a0, b0 = pairs[0]
    part_c = b0.shape[2] if b_parts else None
    if mode == "nn":
        m, n = a0.shape[0], (N_CHIPS * part_c if b_parts else b0.shape[1])
        ks = [a.shape[1] for a, _ in pairs]
    elif mode == "nt":
        m, n = a0.shape[0], b0.shape[-2]
        ks = [a.shape[1] for a, _ in pairs]
    else:
        m, n = a0.shape[1], b0.shape[1]
        ks = [a.shape[0] for a, _ in pairs]
    tm = tiles[0] or _first_divisor(m, (1024, 768, 512, 256))
    tn = tiles[1] or (n if n <= 1536 else _first_divisor(n, (1024, 512, 256, 128)))
    tks = [tiles[2] or (512 if k % 512 == 0 else k) for k in ks]
    nks = [k // tk for k, tk in zip(ks, tks)]
    offs = [sum(nks[:p]) for p in range(len(pairs))]
    nk = sum(nks)
    n_pairs = len(pairs)
    assert not (b_parts or out_parts) or n_pairs == 1

    def kk(k, p):
        return jnp.clip(k - offs[p], 0, nks[p] - 1)

    in_specs = []
    for p in range(n_pairs):
        tk = tks[p]
        if mode == "nn":
            in_specs.append(pl.BlockSpec((tm, tk), lambda i, j, k, p=p: (i, kk(k, p))))
            if b_parts == "n":
                per = part_c // tn
                in_specs.append(pl.BlockSpec((None, tk, tn), lambda i, j, k: (j // per, k, j % per)))
            else:
                in_specs.append(pl.BlockSpec((tk, tn), lambda i, j, k, p=p: (kk(k, p), j)))
        elif mode == "nt":
            in_specs.append(pl.BlockSpec((tm, tk), lambda i, j, k, p=p: (i, kk(k, p))))
            if b_parts == "k":
                per = part_c // tk
                in_specs.append(pl.BlockSpec((None, tn, tk), lambda i, j, k: (k // per, j, k % per)))
            else:
                in_specs.append(pl.BlockSpec((tn, tk), lambda i, j, k, p=p: (j, kk(k, p))))
        else:
            in_specs.append(pl.BlockSpec((tk, tm), lambda i, j, k, p=p: (kk(k, p), i)))
            in_specs.append(pl.BlockSpec((tk, tn), lambda i, j, k, p=p: (kk(k, p), j)))
    if out_parts:
        out_per = (n // N_CHIPS) // tn
        out_spec = pl.BlockSpec((None, tm, tn), lambda i, j, k: (j // out_per, i, j % out_per))
        out_shape = jax.ShapeDtypeStruct((N_CHIPS, m, n // N_CHIPS), out_dtype)
    else:
        out_spec = pl.BlockSpec((tm, tn), lambda i, j, k: (i, j))
        out_shape = jax.ShapeDtypeStruct((m, n), out_dtype)
    dims = {"nn": (((1,), (0,)), ((), ())), "nt": (((1,), (1,)), ((), ())), "tn": (((0,), (0,)), ((), ()))}[mode]

    n_after = 0 if after is None else 1

    def body(*refs):
        o_ref, acc = refs[2 * n_pairs + n_after], refs[2 * n_pairs + n_after + 1]
        k = pl.program_id(2)

        @pl.when(k == 0)
        def _():
            acc[...] = jnp.zeros_like(acc)

        for p in range(n_pairs):
            a_ref, b_ref = refs[2 * p], refs[2 * p + 1]

            def step(a_ref=a_ref, b_ref=b_ref):
                acc[...] += lax.dot_general(a_ref[...], b_ref[...], dims, preferred_element_type=F32)

            if n_pairs == 1:
                step()
            else:
                pl.when((k >= offs[p]) & (k < offs[p] + nks[p]))(step)

        @pl.when(k == nk - 1)
        def _():
            o_ref[...] = acc[...].astype(o_ref.dtype)

    flat = [t for pr in pairs for t in pr] + ([] if after is None else [after])
    return pl.pallas_call(
        body,
        name=name,
        grid=(m // tm, n // tn, nk),
        in_specs=in_specs + [pl.BlockSpec(memory_space=pl.ANY)] * n_after,
        out_specs=out_spec,
        out_shape=out_shape,
        scratch_shapes=[pltpu.VMEM((tm, tn), F32)],
        compiler_params=_params(("parallel", "parallel", "arbitrary")),
    )(*flat)


F4 = D_FF // N_CHIPS
FFN_TM = 1024
FFN_TK = 512


def _ffn_up(hf, w_gu):
    t, d = hf.shape
    nk = d // FFN_TK

    def body(a_ref, b_ref, gu_ref, act_ref, acc):
        k = pl.program_id(2)

        @pl.when(k == 0)
        def _():
            acc[...] = jnp.zeros_like(acc)

        acc[...] += jnp.dot(a_ref[...], b_ref[...], preferred_element_type=F32)

        @pl.when(k == nk - 1)
        def _():
            g = acc[:, :F4]
            gu_ref[...] = acc[...].astype(gu_ref.dtype)
            act_ref[...] = (g * _sigmoid(g) * acc[:, F4:]).astype(act_ref.dtype)

    return pl.pallas_call(
        body,
        name="ffn_up",
        grid=(t // FFN_TM, N_CHIPS, nk),
        in_specs=[pl.BlockSpec((FFN_TM, FFN_TK), lambda i, q, k: (i, k)),
                  pl.BlockSpec((None, FFN_TK, 2 * F4), lambda i, q, k: (q, k, 0))],
        out_specs=[pl.BlockSpec((FFN_TM, 2 * F4), lambda i, q, k: (i, q)),
                   pl.BlockSpec((FFN_TM, F4), lambda i, q, k: (i, q))],
        out_shape=[jax.ShapeDtypeStruct((t, 2 * D_FF), BF16), jax.ShapeDtypeStruct((t, D_FF), BF16)],
        scratch_shapes=[pltpu.VMEM((FFN_TM, 2 * F4), F32)],
        compiler_params=_params(("parallel", "parallel", "arbitrary")),
    )(hf, w_gu)


def _ffn_down_dx(d_ff, w_down, gu):
    t, d = d_ff.shape
    nk = d // FFN_TK

    def body(a_ref, b_ref, gu_ref, o_ref, acc):
        k = pl.program_id(2)

        @pl.when(k == 0)
        def _():
            acc[...] = jnp.zeros_like(acc)

        acc[...] += lax.dot_general(a_ref[...], b_ref[...], NT_DIMS, preferred_element_type=F32)

        @pl.when(k == nk - 1)
        def _():
            g = gu_ref[:, :F4].astype(F32)
            u = gu_ref[:, F4:].astype(F32)
            sg = _sigmoid(g)
            d_act = acc[...]
            o_ref[:, :F4] = (d_act * u * sg * (1.0 + g * (1.0 - sg))).astype(o_ref.dtype)
            o_ref[:, F4:] = (d_act * g * sg).astype(o_ref.dtype)

    return pl.pallas_call(
        body,
        name="ffn_down_dx",
        grid=(t // FFN_TM, N_CHIPS, nk),
        in_specs=[pl.BlockSpec((FFN_TM, FFN_TK), lambda i, q, k: (i, k)),
                  pl.BlockSpec((F4, FFN_TK), lambda i, q, k: (q, k)),
                  pl.BlockSpec((FFN_TM, 2 * F4), lambda i, q, k: (i, q))],
        out_specs=pl.BlockSpec((FFN_TM, 2 * F4), lambda i, q, k: (i, q)),
        out_shape=jax.ShapeDtypeStruct((t, 2 * D_FF), BF16),
        scratch_shapes=[pltpu.VMEM((FFN_TM, F4), F32)],
        compiler_params=_params(("parallel", "parallel", "arbitrary")),
    )(d_ff, w_down, gu)


def _rowwise(fn, row_ins, vec_ins, row_outs, acc_outs, name):
    t = row_ins[0].shape[0]
    tm = ROW_TILE
    n_in = len(row_ins) + len(vec_ins)
    n_row = len(row_outs)

    def body(*refs):
        ins = [r[...] for r in refs[:n_in]]
        outs = refs[n_in:]
        vals = fn(*ins)
        for r, v in zip(outs[:n_row], vals[:n_row]):
            r[...] = v.astype(r.dtype)
        if acc_outs:
            @pl.when(pl.program_id(0) == 0)
            def _():
                for r in outs[n_row:]:
                    r[...] = jnp.zeros_like(r)

            for r, v in zip(outs[n_row:], vals[n_row:]):
                r[...] += v

    in_specs = [pl.BlockSpec((tm, a.shape[1]), lambda i: (i, 0)) for a in row_ins]
    in_specs += [pl.BlockSpec(a.shape, lambda i: (0, 0)) for a in vec_ins]
    out_specs = [pl.BlockSpec((tm, c), lambda i: (i, 0)) for c, _ in row_outs]
    out_specs += [pl.BlockSpec((1, c), lambda i: (0, 0)) for c in acc_outs]
    out_shape = [jax.ShapeDtypeStruct((t, c), dt) for c, dt in row_outs]
    out_shape += [jax.ShapeDtypeStruct((1, c), F32) for c in acc_outs]
    return pl.pallas_call(
        body,
        name=name,
        grid=(t // tm,),
        in_specs=in_specs,
        out_specs=out_specs,
        out_shape=out_shape,
        compiler_params=_params(("arbitrary",)),
    )(*row_ins, *vec_ins)


def _mean(v):
    return jnp.mean(v, axis=-1, keepdims=True)


def _colsum(v):
    return jnp.sum(v, axis=0, keepdims=True)


def _rms_fwd(v, g):
    r = lax.rsqrt(_mean(v * v) + EPS)
    vhat = v * r
    return vhat * g, vhat, r


def _rms_bwd(dn, vhat, r, g):
    dng = dn * g
    return r * (dng - vhat * _mean(dng * vhat)), _colsum(dn * vhat)


def _swap_rope_halves(v):
    n = v.shape[-1]
    lane = lax.broadcasted_iota(jnp.int32, v.shape, v.ndim - 1)
    return jnp.where(lane % QK_ROPE < QK_ROPE // 2, pltpu.roll(v, n - QK_ROPE // 2, v.ndim - 1),
                     pltpu.roll(v, QK_ROPE // 2, v.ndim - 1))


def _rope(v, cos2, sin2):
    return v * cos2 + _swap_rope_halves(v) * sin2


def _rope_transposed(dv, cos2, sin2):
    return dv * cos2 + _swap_rope_halves(dv * sin2)


def _sigmoid(v):
    return 1.0 / (1.0 + jnp.exp(-v))


CONV_ROWS = 256


def _conv_fwd(ag, conv_w, conv_b):
    t = ag.shape[0]
    cb = LANES

    def body(ag_ref, w_ref, b_ref, o_ref, scr):
        a = ag_ref[:, :cb].astype(F32)
        g = ag_ref[:, cb:].astype(F32)
        scr[pl.ds(0, CONV_K_PAD), :] = jnp.zeros((CONV_K_PAD, cb), F32)
        scr[pl.ds(CONV_K_PAD, t), :] = a * _sigmoid(g)
        for r0 in range(0, t, CONV_ROWS):
            acc = jnp.zeros((CONV_ROWS, cb), F32) + b_ref[...]
            for k in range(CONV_K):
                acc = acc + w_ref[k:k + 1, :] * scr[pl.ds(r0 + CONV_K_PAD - (CONV_K - 1) + k, CONV_ROWS), :]
            o_ref[pl.ds(r0, CONV_ROWS), :] = acc

    return pl.pallas_call(
        body,
        name="conv_fwd",
        grid=(CONV_CH // cb,),
        in_specs=[pl.BlockSpec((t, 2 * cb), lambda j: (0, j)),
                  pl.BlockSpec((CONV_K_PAD, cb), lambda j: (0, j)),
                  pl.BlockSpec((1, cb), lambda j: (0, j))],
        out_specs=pl.BlockSpec((t, cb), lambda j: (0, j)),
        out_shape=jax.ShapeDtypeStruct((t, CONV_CH), F32),
        scratch_shapes=[pltpu.VMEM((t + CONV_K_PAD, cb), F32)],
        compiler_params=_params(("parallel",)),
    )(ag, conv_w, conv_b)


def _conv_bwd(d_u1, ag, conv_w):
    t = ag.shape[0]
    cb = LANES

    def body(du_ref, ag_ref, w_ref, dag_ref, dw_ref, db_ref, su, sd):
        a = ag_ref[:, :cb].astype(F32)
        g = ag_ref[:, cb:].astype(F32)
        sg = _sigmoid(g)
        su[pl.ds(0, CONV_K_PAD), :] = jnp.zeros((CONV_K_PAD, cb), F32)
        su[pl.ds(CONV_K_PAD, t), :] = a * sg
        sd[pl.ds(0, t), :] = du_ref[...]
        sd[pl.ds(t, CONV_K_PAD), :] = jnp.zeros((CONV_K_PAD, cb), F32)
        db_ref[...] = _colsum(du_ref[...])
        dw_ref[...] = jnp.zeros_like(dw_ref)
        for r0 in range(0, t, CONV_ROWS):
            du = sd[pl.ds(r0, CONV_ROWS), :]
            acc = jnp.zeros((CONV_ROWS, cb), F32)
            for k in range(CONV_K):
                acc = acc + w_ref[k:k + 1, :] * sd[pl.ds(r0 + (CONV_K - 1) - k, CONV_ROWS), :]
                dw_ref[k:k + 1, :] += _colsum(du * su[pl.ds(r0 + CONV_K_PAD - (CONV_K - 1) + k, CONV_ROWS), :])
            sgc = sg[r0:r0 + CONV_ROWS]
            ac = a[r0:r0 + CONV_ROWS]
            dag_ref[pl.ds(r0, CONV_ROWS), :cb] = (acc * sgc).astype(dag_ref.dtype)
            dag_ref[pl.ds(r0, CONV_ROWS), cb:] = (acc * ac * sgc * (1.0 - sgc)).astype(dag_ref.dtype)

    return pl.pallas_call(
        body,
        name="conv_bwd",
        grid=(CONV_CH // cb,),
        in_specs=[pl.BlockSpec((t, cb), lambda j: (0, j)),
                  pl.BlockSpec((t, 2 * cb), lambda j: (0, j)),
                  pl.BlockSpec((CONV_K_PAD, cb), lambda j: (0, j))],
        out_specs=[pl.BlockSpec((t, 2 * cb), lambda j: (0, j)),
                   pl.BlockSpec((CONV_K_PAD, cb), lambda j: (0, j)),
                   pl.BlockSpec((1, cb), lambda j: (0, j))],
        out_shape=[jax.ShapeDtypeStruct((t, 2 * CONV_CH), BF16),
                   jax.ShapeDtypeStruct((CONV_K_PAD, CONV_CH), F32),
                   jax.ShapeDtypeStruct((1, CONV_CH), F32)],
        scratch_shapes=[pltpu.VMEM((t + CONV_K_PAD, cb), F32), pltpu.VMEM((t + CONV_K_PAD, cb), F32)],
        compiler_params=_params(("parallel",)),
    )(d_u1, ag, conv_w)


ATT_TQ = 256
NEG = float(jnp.finfo(jnp.float32).min)
SCALE = QK_HEAD ** -0.5
NT_DIMS = (((1,), (1,)), ((), ()))
TN_DIMS = (((0,), (0,)), ((), ()))


def _att_probs(qf, kf, row0):
    s = lax.dot_general(qf, kf, NT_DIMS, preferred_element_type=F32) * SCALE
    tq, t = s.shape
    qpos = row0 + lax.broadcasted_iota(jnp.int32, (tq, t), 0)
    kpos = lax.broadcasted_iota(jnp.int32, (tq, t), 1)
    s = jnp.where(kpos <= qpos, s, NEG)
    p = jnp.exp(s - jnp.max(s, axis=-1, keepdims=True))
    return p * (1.0 / jnp.sum(p, axis=-1, keepdims=True))


def _half_mask(shape, which):
    lane = lax.broadcasted_iota(jnp.int32, shape, len(shape) - 1)
    return (lane // QK_ROPE == which).astype(F32)


def _attention_fwd(q, kv, kpe2, cos2, sin2):
    t = q.shape[0]
    tq = ATT_TQ

    def body(qn_ref, qp_ref, c_ref, s_ref, kv_ref, kpe_ref, o_ref):
        row0 = pl.program_id(1) * tq
        roped = _rope(qp_ref[...].astype(F32), c_ref[...], s_ref[...])
        for e in range(2):
            qf = jnp.concatenate([qn_ref[:, e * QK_NOPE:(e + 1) * QK_NOPE],
                                  (roped * _half_mask(roped.shape, e)).astype(BF16)], axis=1)
            kf = jnp.concatenate([kv_ref[:, e * 256:e * 256 + QK_NOPE], kpe_ref[...]], axis=1)
            p = _att_probs(qf, kf, row0)
            v = kv_ref[:, e * 256 + QK_NOPE:(e + 1) * 256]
            o_ref[:, e * V_HEAD:(e + 1) * V_HEAD] = jnp.dot(p.astype(BF16), v, preferred_element_type=F32).astype(o_ref.dtype)

    return pl.pallas_call(
        body,
        name="attention_fwd",
        grid=(N_HEADS // 2, t // tq),
        in_specs=[pl.BlockSpec((tq, 2 * QK_NOPE), lambda h, i: (i, h)),
                  pl.BlockSpec((tq, LANES), lambda h, i: (i, N_HEADS + h)),
                  pl.BlockSpec((tq, LANES), lambda h, i: (i, 0)),
                  pl.BlockSpec((tq, LANES), lambda h, i: (i, 0)),
                  pl.BlockSpec((t, 512), lambda h, i: (0, h)),
                  pl.BlockSpec((t, LANES), lambda h, i: (0, 0))],
        out_specs=pl.BlockSpec((tq, 2 * V_HEAD), lambda h, i: (i, h)),
        out_shape=jax.ShapeDtypeStruct((t, ATTN_CH), BF16),
        compiler_params=_params(("parallel", "parallel")),
    )(q, q, cos2, sin2, kv, kpe2)


def _attention_bwd(q, kv, kpe2, cos2, sin2, d_attn):
    t = q.shape[0]
    tq = ATT_TQ
    n_q = t // tq

    def body(qn_ref, qp_ref, c_ref, s_ref, kv_ref, kpe_ref, do_ref, dqn_ref, dqp_ref, dkv_ref, dkpe_ref, dkv_acc):
        h, i = pl.program_id(0), pl.program_id(1)
        row0 = i * tq

        @pl.when(i == 0)
        def _():
            dkv_acc[...] = jnp.zeros_like(dkv_acc)

        @pl.when((i == 0) & (h == 0))
        def _():
            dkpe_ref[...] = jnp.zeros_like(dkpe_ref)

        roped = _rope(qp_ref[...].astype(F32), c_ref[...], s_ref[...])
        d_roped = jnp.zeros((tq, LANES), F32)
        for e in range(2):
            mask = _half_mask(roped.shape, e)
            qf = jnp.concatenate([qn_ref[:, e * QK_NOPE:(e + 1) * QK_NOPE], (roped * mask).astype(BF16)], axis=1)
            kf = jnp.concatenate([kv_ref[:, e * 256:e * 256 + QK_NOPE], kpe_ref[...]], axis=1)
            v = kv_ref[:, e * 256 + QK_NOPE:(e + 1) * 256]
            do = do_ref[:, e * V_HEAD:(e + 1) * V_HEAD]
            p = _att_probs(qf, kf, row0)
            dp = lax.dot_general(do, v, NT_DIMS, preferred_element_type=F32)
            ds = (p * (dp - jnp.sum(p * dp, axis=-1, keepdims=True)) * SCALE).astype(BF16)
            dqf = jnp.dot(ds, kf, preferred_element_type=F32)
            dkf = lax.dot_general(ds, qf, TN_DIMS, preferred_element_type=F32)
            dv = lax.dot_general(p.astype(BF16), do, TN_DIMS, preferred_element_type=F32)
            dqn_ref[:, e * QK_NOPE:(e + 1) * QK_NOPE] = dqf[:, :QK_NOPE].astype(dqn_ref.dtype)
            d_roped = d_roped + dqf[:, QK_NOPE:] * mask
            dkv_acc[:, e * 256:e * 256 + QK_NOPE] += dkf[:, :QK_NOPE]
            dkv_acc[:, e * 256 + QK_NOPE:(e + 1) * 256] += dv
            dkpe_ref[...] += dkf[:, QK_NOPE:]
        dqp_ref[...] = _rope_transposed(d_roped, c_ref[...], s_ref[...]).astype(dqp_ref.dtype)

        @pl.when(i == n_q - 1)
        def _():
            dkv_ref[...] = dkv_acc[...].astype(dkv_ref.dtype)

    return pl.pallas_call(
        body,
        name="attention_bwd",
        grid=(N_HEADS // 2, n_q),
        in_specs=[pl.BlockSpec((tq, 2 * QK_NOPE), lambda h, i: (i, h)),
                  pl.BlockSpec((tq, LANES), lambda h, i: (i, N_HEADS + h)),
                  pl.BlockSpec((tq, LANES), lambda h, i: (i, 0)),
                  pl.BlockSpec((tq, LANES), lambda h, i: (i, 0)),
                  pl.BlockSpec((t, 512), lambda h, i: (0, h)),
                  pl.BlockSpec((t, LANES), lambda h, i: (0, 0)),
                  pl.BlockSpec((tq, 2 * V_HEAD), lambda h, i: (i, h))],
        out_specs=[pl.BlockSpec((tq, 2 * QK_NOPE), lambda h, i: (i, h)),
                   pl.BlockSpec((tq, LANES), lambda h, i: (i, h)),
                   pl.BlockSpec((t, 512), lambda h, i: (0, h)),
                   pl.BlockSpec((t, LANES), lambda h, i: (0, 0))],
        out_shape=[jax.ShapeDtypeStruct((t, N_HEADS * QK_NOPE), BF16),
                   jax.ShapeDtypeStruct((t, N_HEADS * QK_ROPE), BF16),
                   jax.ShapeDtypeStruct((t, N_HEADS * 256), BF16),
                   jax.ShapeDtypeStruct((t, LANES), F32)],
        scratch_shapes=[pltpu.VMEM((t, 512), F32)],
        compiler_params=_params(("arbitrary", "arbitrary")),
    )(q, q, cos2, sin2, kv, kpe2, d_attn)


def _local_step(x, target, cos2, sin2, vec, w, ffn):
    d = D_MODEL

    (h,) = _rowwise(lambda xv, g: (_rms_fwd(xv, g)[0],), [x], [vec["pre_mix_norm"]], [(d, BF16)], [], "pre_mix_norm_fwd")
    ag = _matmul([(h, w["w_ag"])], "nn", BF16, "in_proj_ag")
    z2 = _matmul([(h, w["w_z2"])], "nn", BF16, "in_proj_z2")
    u1 = _conv_fwd(ag, w["conv_w"], vec["conv_b"])

    def latents_fwd(z, c2, s2, qg, kvg):
        z = z.astype(F32)
        qn = _rms_fwd(z[:, :Q_LORA], qg)[0]
        kvn = _rms_fwd(z[:, Q_LORA:Q_LORA + KV_LORA], kvg)[0]
        kr = z[:, Q_LORA + KV_LORA:]
        kr2 = kr + pltpu.roll(kr, QK_ROPE, 1)
        return qn, kvn, _rope(kr2, c2, s2)

    qn, kvn, kpe2 = _rowwise(latents_fwd, [z2, cos2, sin2], [vec["q_norm"], vec["kv_norm"]],
                             [(Q_LORA, BF16), (KV_LORA, BF16), (LANES, BF16)], [], "latents_fwd")
    q = _matmul([(qn, w["w_uq"])], "nn", BF16, "q_up")
    kv = _matmul([(kvn, w["w_ukv"])], "nn", BF16, "kv_up")
    attn = _attention_fwd(q, kv, kpe2, cos2, sin2)

    def conv_post(u, lg, lb):
        mu = _mean(u)
        uc = u - mu
        rstd = lax.rsqrt(_mean(uc * uc) + EPS)
        uhat = uc * rstd
        u2 = uhat * lg + lb
        sg = _sigmoid(u2)
        return uhat, rstd, u2, sg, u2 * sg

    def mix_in_fwd(u, at, lg, lb, cg, ag_):
        u3 = conv_post(u, lg, lb)[4]
        cn = _rms_fwd(u3, cg)[0]
        an = _rms_fwd(at.astype(F32), ag_)[0]
        return (jnp.concatenate([cn, an], axis=1),)

    (cat,) = _rowwise(mix_in_fwd, [u1, attn], [vec["conv_ln_g"], vec["conv_ln_b"], vec["conv_out_norm"], vec["attn_out_norm"]],
                      [(2 * CONV_CH, BF16)], [], "mix_in_fwd")
    mix = _matmul([(cat, w["w_out"])], "nn", F32, "out_proj")

    def residual1(xv, mv, gpm, gpf):
        x1 = xv + _rms_fwd(mv, gpm)[0]
        return x1, _rms_fwd(x1, gpf)[0]

    x1, hf = _rowwise(residual1, [x, mix], [vec["post_mix_norm"], vec["pre_ffn_norm"]], [(d, F32), (d, BF16)], [], "residual1_fwd")
    w_gu = ffn["w_gu"](hf)
    gu, act = _ffn_up(hf, w_gu)
    w_down = ffn["w_down"](act)
    ff = _matmul([(act, w_down)], "nn", F32, "ffn_down")

    def loss_head(x1v, ffv, tg, g):
        n, fhat, r = _rms_fwd(ffv, g)
        err = x1v + n - tg
        loss = 0.5 * jnp.sum(_mean(err * err), axis=0, keepdims=True)
        dy = err * (1.0 / d)
        d_ff, dg = _rms_bwd(dy, fhat, r, g)
        return dy, d_ff, dg, jnp.broadcast_to(loss, (1, LANES))

    dy, d_ff, g_post_ffn, loss = _rowwise(loss_head, [x1, ff, target], [vec["post_ffn_norm"]],
                                          [(d, F32), (d, BF16)], [d, LANES], "loss_head")
    d_gu = _ffn_down_dx(d_ff, w_down, gu)
    dw_down = _matmul([(act, d_ff)], "tn", BF16, "ffn_down_dw", tiles=(F4, None, None))
    dw_gu = _matmul([(hf, d_gu)], "tn", BF16, "ffn_gate_up_dw", out_parts=True, tiles=(None, F4, None))
    started = ffn["grads"](dw_gu, dw_down)
    d_hf = _matmul([(d_gu, w_gu)], "nt", F32, "ffn_gate_up_dx", b_parts="k", tiles=(None, None, F4), after=started)

    def residual1_bwd(dyv, dhf, x1v, mv, gpf, gpm):
        _, x1hat, r1 = _rms_fwd(x1v, gpf)
        dn, dgpf = _rms_bwd(dhf, x1hat, r1, gpf)
        d_x1 = dyv + dn
        _, mhat, rm = _rms_fwd(mv, gpm)
        d_mix, dgpm = _rms_bwd(d_x1, mhat, rm, gpm)
        return d_x1, d_mix, dgpf, dgpm

    d_x1, d_mix, g_pre_ffn, g_post_mix = _rowwise(residual1_bwd, [dy, d_hf, x1, mix], [vec["pre_ffn_norm"], vec["post_mix_norm"]],
                                                  [(d, F32), (d, BF16)], [d, d], "residual1_bwd")
    d_cat = _matmul([(d_mix, w["w_out"])], "nt", BF16, "out_proj_dx")
    dw_out = _matmul([(cat, d_mix)], "tn", BF16, "out_proj_dw")

    def mix_in_bwd(dc, u, at, lg, lb, cg, ag_):
        dc = dc.astype(F32)
        uhat, rstd, u2, sg, u3 = conv_post(u, lg, lb)
        _, u3hat, rc = _rms_fwd(u3, cg)
        d_u3, dcg = _rms_bwd(dc[:, :CONV_CH], u3hat, rc, cg)
        d_u2 = d_u3 * sg * (1.0 + u2 * (1.0 - sg))
        dgl = d_u2 * lg
        d_u1 = rstd * (dgl - _mean(dgl) - uhat * _mean(dgl * uhat))
        _, ahat, ra = _rms_fwd(at.astype(F32), ag_)
        d_at, dag = _rms_bwd(dc[:, CONV_CH:], ahat, ra, ag_)
        return d_u1, d_at, dcg, _colsum(d_u2 * uhat), _colsum(d_u2), dag

    d_u1, d_attn, g_conv_out, g_ln_g, g_ln_b, g_attn_out = _rowwise(
        mix_in_bwd, [d_cat, u1, attn], [vec["conv_ln_g"], vec["conv_ln_b"], vec["conv_out_norm"], vec["attn_out_norm"]],
        [(CONV_CH, F32), (ATTN_CH, BF16)], [CONV_CH] * 4, "mix_in_bwd")
    d_ag, d_conv_w, g_conv_b = _conv_bwd(d_u1, ag, w["conv_w"])
    d_qn_, d_qp_, d_kv, d_kpe2 = _attention_bwd(q, kv, kpe2, cos2, sin2, d_attn)
    d_q = jnp.concatenate([d_qn_, d_qp_], axis=1)
    d_qn = _matmul([(d_q, w["w_uq"])], "nt", BF16, "q_up_dx")
    dw_uq = _matmul([(qn, d_q)], "tn", BF16, "q_up_dw")
    d_kvn = _matmul([(d_kv, w["w_ukv"])], "nt", BF16, "kv_up_dx")
    dw_ukv = _matmul([(kvn, d_kv)], "tn", BF16, "kv_up_dw")

    def latents_bwd(z, dq, dk, dkp, c2, s2, qg, kvg):
        z = z.astype(F32)
        _, qhat, rq = _rms_fwd(z[:, :Q_LORA], qg)
        d_ql, dqg = _rms_bwd(dq.astype(F32), qhat, rq, qg)
        _, khat, rk = _rms_fwd(z[:, Q_LORA:Q_LORA + KV_LORA], kvg)
        d_kl, dkg = _rms_bwd(dk.astype(F32), khat, rk, kvg)
        both = dkp + pltpu.roll(dkp, QK_ROPE, 1)
        d_kr = _rope_transposed(both, c2, s2) * _half_mask(both.shape, 0)
        return jnp.concatenate([d_ql, d_kl, d_kr], axis=1), dqg, dkg

    d_z2, g_q_norm, g_kv_norm = _rowwise(latents_bwd, [z2, d_qn, d_kvn, d_kpe2, cos2, sin2], [vec["q_norm"], vec["kv_norm"]],
                                         [(Z2_COLS, BF16)], [Q_LORA, KV_LORA], "latents_bwd")
    d_h = _matmul([(d_ag, w["w_ag"]), (d_z2, w["w_z2"])], "nt", F32, "in_proj_dx")
    dw_ag = _matmul([(h, d_ag)], "tn", BF16, "in_proj_ag_dw")
    dw_z2 = _matmul([(h, d_z2)], "tn", BF16, "in_proj_z2_dw")

    def pre_mix_bwd(dx1, dh, xv, g):
        _, xhat, r = _rms_fwd(xv, g)
        dn, dg = _rms_bwd(dh, xhat, r, g)
        return dx1 + dn, dg

    grad_x, g_pre_mix = _rowwise(pre_mix_bwd, [d_x1, d_h, x], [vec["pre_mix_norm"]], [(d, F32)], [d], "pre_mix_norm_bwd")

    dw = dict(w_ag=dw_ag, w_z2=dw_z2, w_uq=dw_uq, w_ukv=dw_ukv, conv_w=d_conv_w, w_out=dw_out, w_gu=dw_gu, w_down=dw_down)
    dvec = dict(pre_mix_norm=g_pre_mix, q_norm=g_q_norm, kv_norm=g_kv_norm, conv_b=g_conv_b, conv_ln_g=g_ln_g,
                conv_ln_b=g_ln_b, conv_out_norm=g_conv_out, attn_out_norm=g_attn_out, post_mix_norm=g_post_mix,
                pre_ffn_norm=g_pre_ffn, post_ffn_norm=g_post_ffn)
    return loss, grad_x, dw, dvec


ANY = pl.BlockSpec(memory_space=pl.ANY)


def _place():
    x, y, c = lax.axis_index("x"), lax.axis_index("y"), lax.axis_index("c")
    chips = [(1 - x, y), (x, 1 - y), (1 - x, 1 - y)]
    return x, y, c, chips


def _to_parts(chip, pieces, dtype, name):
    r = pieces[0].shape[0]
    widths = [a.shape[1] for a in pieces]
    tr = r if r <= 512 else _first_divisor(r, (512, 256, 128))

    def body(p_ref, *refs):
        o_ref = refs[len(pieces)]
        off = 0
        for a_ref, wdt in zip(refs, widths):
            o_ref[:, off:off + wdt] = a_ref[...].astype(o_ref.dtype)
            off += wdt

    return pl.pallas_call(
        body,
        name=name,
        grid_spec=pltpu.PrefetchScalarGridSpec(
            num_scalar_prefetch=1,
            grid=(r // tr,),
            in_specs=[pl.BlockSpec((tr, wdt), lambda i, p_ref: (i, 0)) for wdt in widths],
            out_specs=pl.BlockSpec((None, tr, sum(widths)), lambda i, p_ref: (p_ref[0], i, 0))),
        out_shape=jax.ShapeDtypeStruct((N_CHIPS, r, sum(widths)), dtype),
        compiler_params=_params(("parallel",)),
    )(chip, *pieces)


HBM = pl.BlockSpec(memory_space=pltpu.HBM)
SEM = pl.BlockSpec(memory_space=pltpu.SEMAPHORE)
EFFECT = pltpu.SideEffectType.DATAFLOW_SIDE_EFFECTING


def _in_hbm(a):
    return pltpu.with_memory_space_constraint(a, pltpu.HBM)


def _gather_rows(buf, whole, half):
    r = buf.shape[1]
    return pl.ds(0, r) if whole else pl.ds(half * (r // 2), r // 2)


def _gather_start(bufs, whole, groups):
    n = len(bufs)
    n_g = len(groups)

    def body(*refs):
        sems = refs[n:n + 2 * n_g]
        outs = refs[n + 2 * n_g:2 * n + 2 * n_g]
        x, y, c, chips = _place()
        p = 2 * x + y
        for gi, group in enumerate(groups):
            for ki, k in enumerate(group):
                blk = outs[k].at[p, _gather_rows(bufs[k], whole[k], c), :]
                for j, (px, py) in enumerate(chips):
                    pltpu.make_async_remote_copy(src_ref=blk, dst_ref=blk, send_sem=sems[2 * gi].at[3 * ki + j],
                                                 recv_sem=sems[2 * gi + 1].at[3 * ki + j],
                                                 device_id=(px, py, c), device_id_type=MESH).start()

    sem_shapes = []
    for group in groups:
        sem_shapes += [pltpu.SemaphoreType.DMA((3 * len(group),))] * 2
    res = pl.pallas_call(
        body,
        name="gather_start",
        in_specs=[HBM] * n,
        out_specs=[SEM] * (2 * n_g) + [HBM] * n,
        out_shape=sem_shapes + [pltpu.HBM(a.shape, a.dtype) for a in bufs],
        input_output_aliases={k: 2 * n_g + k for k in range(n)},
        compiler_params=pltpu.CompilerParams(has_side_effects=EFFECT),
    )(*[_in_hbm(a) for a in bufs])
    sems = [(res[2 * gi], res[2 * gi + 1]) for gi in range(n_g)]
    return sems, list(res[2 * n_g:2 * n_g + n])


def _gather_wait(bufs, whole, send, recv, after, name):
    n = len(bufs)

    def body(*refs):
        ins = refs[:n]
        send_ref, recv_ref = refs[n], refs[n + 1]
        x, y, c, chips = _place()
        p = 2 * x + y
        for ki in range(n):
            rows = _gather_rows(bufs[ki], whole[ki], c)
            for j, (px, py) in enumerate(chips):
                cp = pltpu.make_async_remote_copy(src_ref=ins[ki].at[p, rows, :], dst_ref=ins[ki].at[2 * px + py, rows, :],
                                                  send_sem=send_ref.at[3 * ki + j], recv_sem=recv_ref.at[3 * ki + j],
                                                  device_id=(px, py, c), device_id_type=MESH)
                cp.wait_send()
                cp.wait_recv()

    res = pl.pallas_call(
        body,
        name=name,
        in_specs=[HBM] * n + [SEM, SEM, ANY],
        out_specs=[HBM] * n,
        out_shape=[pltpu.HBM(a.shape, a.dtype) for a in bufs],
        input_output_aliases={k: k for k in range(n)},
        compiler_params=pltpu.CompilerParams(has_side_effects=EFFECT),
    )(*bufs, send, recv, after)
    return list(res)


def _gather_hand_on(bufs, name):
    n = len(bufs)

    def body(*refs):
        outs = refs[n:2 * n]
        send, recv = refs[2 * n:]
        x, y, c, chips = _place()

        def d2d(k, j, half):
            px, py = chips[j]
            blk = outs[k].at[2 * px + py, _gather_rows(bufs[k], False, half), :]
            return pltpu.make_async_remote_copy(src_ref=blk, dst_ref=blk, send_sem=send.at[3 * k + j], recv_sem=recv.at[3 * k + j],
                                                device_id=(x, y, 1 - c), device_id_type=MESH)

        sent = [d2d(k, j, c) for k in range(n) for j in range(3)]
        for cp in sent:
            cp.start()
        for k in range(n):
            for j in range(3):
                d2d(k, j, 1 - c).wait_recv()
        for cp in sent:
            cp.wait_send()

    return pl.pallas_call(
        body,
        name=name,
        in_specs=[ANY] * n,
        out_specs=[ANY] * n,
        out_shape=[jax.ShapeDtypeStruct(a.shape, a.dtype) for a in bufs],
        input_output_aliases={k: k for k in range(n)},
        scratch_shapes=[pltpu.SemaphoreType.DMA((3 * n,)), pltpu.SemaphoreType.DMA((3 * n,))],
        compiler_params=pltpu.CompilerParams(has_side_effects=True),
    )(*bufs)


def _chip_exchange_start(sums):
    n = len(sums)

    def body(*refs):
        send, recv = refs[2 * n], refs[2 * n + 1]
        src = refs[2 * n + 2:3 * n + 2]
        dst = refs[3 * n + 2:4 * n + 2]
        x, y, c, chips = _place()
        p = 2 * x + y
        for k in range(n):
            for j, (px, py) in enumerate(chips):
                pltpu.make_async_remote_copy(src_ref=src[k].at[2 * px + py], dst_ref=dst[k].at[p],
                                             send_sem=send.at[3 * k + j], recv_sem=recv.at[3 * k + j],
                                             device_id=(px, py, c), device_id_type=MESH).start()

    res = pl.pallas_call(
        body,
        name="grad_chip_exchange_start",
        in_specs=[HBM] * (2 * n),
        out_specs=[SEM, SEM] + [HBM] * (2 * n),
        out_shape=[pltpu.SemaphoreType.DMA((3 * n,))] * 2 + [pltpu.HBM(a.shape, a.dtype) for a in sums] * 2,
        input_output_aliases={k: 2 + k for k in range(2 * n)},
        compiler_params=pltpu.CompilerParams(has_side_effects=EFFECT),
    )(*[_in_hbm(a) for a in sums], *[_in_hbm(lax.empty(a.shape, a.dtype)) for a in sums])
    return res[0], res[1], list(res[2:2 + n]), list(res[2 + n:2 + 2 * n])


def _chip_exchange_wait(sums, slots, send, recv, after):
    n = len(sums)

    def body(*refs):
        src, dst = refs[:n], refs[n:2 * n]
        send_ref, recv_ref = refs[2 * n], refs[2 * n + 1]
        x, y, c, chips = _place()
        for k in range(n):
            for j, (px, py) in enumerate(chips):
                cp = pltpu.make_async_remote_copy(src_ref=src[k].at[2 * px + py], dst_ref=dst[k].at[2 * px + py],
                                                  send_sem=send_ref.at[3 * k + j], recv_sem=recv_ref.at[3 * k + j],
                                                  device_id=(px, py, c), device_id_type=MESH)
                cp.wait_send()
                cp.wait_recv()

    res = pl.pallas_call(
        body,
        name="grad_chip_exchange_wait",
        in_specs=[HBM] * (2 * n) + [SEM, SEM, ANY],
        out_specs=[HBM] * (2 * n),
        out_shape=[pltpu.HBM(a.shape, a.dtype) for a in sums] * 2,
        input_output_aliases={k: k for k in range(2 * n)},
        compiler_params=pltpu.CompilerParams(has_side_effects=EFFECT),
    )(*sums, *slots, send, recv, after)
    return list(res[:n]), list(res[n:])


def _pair_exchange(parts, name):
    n = len(parts)

    def body(*refs):
        ins, outs = refs[:n], refs[n:2 * n]
        send, recv = refs[2 * n:]
        x, y, c, _ = _place()
        copies = []
        for k in range(n):
            rh = parts[k].shape[1] // 2
            cp = pltpu.make_async_remote_copy(
                src_ref=ins[k].at[:, pl.ds((1 - c) * rh, rh), :], dst_ref=outs[k],
                send_sem=send.at[k], recv_sem=recv.at[k], device_id=(x, y, 1 - c), device_id_type=MESH)
            cp.start()
            copies.append(cp)
        for cp in copies:
            cp.wait()

    return pl.pallas_call(
        body,
        name=name,
        in_specs=[ANY] * n,
        out_specs=[ANY] * n,
        out_shape=[jax.ShapeDtypeStruct((N_CHIPS, a.shape[1] // 2, a.shape[2]), a.dtype) for a in parts],
        scratch_shapes=[pltpu.SemaphoreType.DMA((n,)), pltpu.SemaphoreType.DMA((n,))],
        compiler_params=pltpu.CompilerParams(has_side_effects=True),
    )(*parts)


def _pair_sum(core, part, landed, name):
    _, r, cdim = part.shape
    rh = r // 2
    tr = _first_divisor(rh, (256, 128, 64, 32, 16))
    nb = rh // tr

    def body(c_ref, a_ref, b_ref, o_ref):
        o_ref[...] = (a_ref[...].astype(F32) + b_ref[...].astype(F32)).astype(o_ref.dtype)

    return pl.pallas_call(
        body,
        name=name,
        grid_spec=pltpu.PrefetchScalarGridSpec(
            num_scalar_prefetch=1,
            grid=(N_CHIPS, nb),
            in_specs=[pl.BlockSpec((None, tr, cdim), lambda q, i, c_ref: (q, c_ref[0] * nb + i, 0)),
                      pl.BlockSpec((None, tr, cdim), lambda q, i, c_ref: (q, i, 0))],
            out_specs=pl.BlockSpec((None, tr, cdim), lambda q, i, c_ref: (q, i, 0))),
        out_shape=jax.ShapeDtypeStruct((N_CHIPS, rh, cdim), BF16),
        compiler_params=_params(("parallel", "parallel")),
    )(core, part, landed)


def _chip_exchange(sums):
    n = len(sums)

    def body(*refs):
        ins, outs = refs[:n], refs[n:2 * n]
        send, recv = refs[2 * n:]
        x, y, c, chips = _place()
        p = 2 * x + y

        def ici(k, j, slot):
            px, py = chips[j]
            return pltpu.make_async_remote_copy(
                src_ref=ins[k].at[2 * px + py], dst_ref=outs[k].at[slot],
                send_sem=send.at[3 * k + j], recv_sem=recv.at[3 * k + j],
                device_id=(px, py, c), device_id_type=MESH)

        sent = [ici(k, j, p) for k in range(n) for j in range(3)]
        for cp in sent:
            cp.start()
        for k in range(n):
            for j in range(3):
                px, py = chips[j]
                ici(k, j, 2 * px + py).wait_recv()
        for cp in sent:
            cp.wait_send()

    return pl.pallas_call(
        body,
        name="grad_chip_exchange",
        in_specs=[ANY] * n,
        out_specs=[ANY] * n,
        out_shape=[jax.ShapeDtypeStruct(a.shape, a.dtype) for a in sums],
        scratch_shapes=[pltpu.SemaphoreType.DMA((3 * n,)), pltpu.SemaphoreType.DMA((3 * n,))],
        compiler_params=pltpu.CompilerParams(has_side_effects=True),
    )(*sums)


def _chip_sum(place, own, slots, name):
    _, rh, cdim = slots.shape
    tr = _first_divisor(rh, (256, 128, 64, 32, 16))
    nb = rh // tr

    def body(place_ref, own_ref, s1_ref, s2_ref, s3_ref, o_ref):
        acc = own_ref[...].astype(F32)
        for s_ref in (s1_ref, s2_ref, s3_ref):
            acc = acc + s_ref[...].astype(F32)
        o_ref[...] = acc

    def other(j):
        return lambda i, place_ref: ((place_ref[0] + j) % N_CHIPS, i, 0)

    return pl.pallas_call(
        body,
        name=name,
        grid_spec=pltpu.PrefetchScalarGridSpec(
            num_scalar_prefetch=1,
            grid=(nb,),
            in_specs=[pl.BlockSpec((None, tr, cdim), other(0))] + [pl.BlockSpec((None, tr, cdim), other(j)) for j in (1, 2, 3)],
            out_specs=pl.BlockSpec((tr, cdim), lambda i, place_ref: (place_ref[1] * nb + i, 0))),
        out_shape=jax.ShapeDtypeStruct((2 * rh, cdim), F32),
        compiler_params=_params(("parallel",)),
    )(place, own, slots, slots, slots)


def _half_exchange(bufs):
    n = len(bufs)

    def body(*refs):
        outs = refs[n:2 * n]
        send, recv = refs[2 * n:]
        x, y, c, _ = _place()
        copies = []
        for k in range(n):
            rh = bufs[k].shape[0] // 2
            mine = outs[k].at[pl.ds(c * rh, rh), :]
            cp = pltpu.make_async_remote_copy(src_ref=mine, dst_ref=mine, send_sem=send.at[k], recv_sem=recv.at[k],
                                              device_id=(x, y, 1 - c), device_id_type=MESH)
            cp.start()
            copies.append(cp)
        for k in range(n):
            rh = bufs[k].shape[0] // 2
            theirs = outs[k].at[pl.ds((1 - c) * rh, rh), :]
            copies[k].wait_send()
            pltpu.make_async_remote_copy(src_ref=theirs, dst_ref=theirs, send_sem=send.at[k], recv_sem=recv.at[k],
                                         device_id=(x, y, 1 - c), device_id_type=MESH).wait_recv()

    return pl.pallas_call(
        body,
        name="grad_half_exchange",
        in_specs=[ANY] * n,
        out_specs=[ANY] * n,
        out_shape=[jax.ShapeDtypeStruct(a.shape, a.dtype) for a in bufs],
        input_output_aliases={k: k for k in range(n)},
        scratch_shapes=[pltpu.SemaphoreType.DMA((n,)), pltpu.SemaphoreType.DMA((n,))],
        compiler_params=pltpu.CompilerParams(has_side_effects=True),
    )(*bufs)


SMALL_ROWS = 32


def _all_reduce_small(pack):
    def body(in_ref, out_ref, gath, send, recv):
        x, y, c, _ = _place()
        me = 4 * x + 2 * y + c
        gath[me] = in_ref[...]
        copies = []
        for k in range(1, N_DEV):
            dx, dy, dc = (k >> 2) & 1, (k >> 1) & 1, k & 1
            peer = (x ^ dx, y ^ dy, c ^ dc)
            cp = pltpu.make_async_remote_copy(src_ref=in_ref, dst_ref=gath.at[me], send_sem=send.at[k], recv_sem=recv.at[k],
                                              device_id=peer, device_id_type=MESH)
            cp.start()
            copies.append((cp, 4 * peer[0] + 2 * peer[1] + peer[2]))
        for k, (cp, peer_id) in enumerate(copies, start=1):
            cp.wait_send()
            pltpu.make_async_remote_copy(src_ref=in_ref, dst_ref=gath.at[peer_id], send_sem=send.at[k], recv_sem=recv.at[k],
                                         device_id=(x, y, c), device_id_type=MESH).wait_recv()
        acc = gath[0]
        for dev in range(1, N_DEV):
            acc = acc + gath[dev]
        out_ref[...] = acc

    return pl.pallas_call(
        body,
        name="all_reduce_small",
        in_specs=[pl.BlockSpec(memory_space=pltpu.VMEM)],
        out_specs=pl.BlockSpec(memory_space=pltpu.VMEM),
        out_shape=jax.ShapeDtypeStruct(pack.shape, F32),
        scratch_shapes=[pltpu.VMEM((N_DEV,) + pack.shape, F32), pltpu.SemaphoreType.DMA((N_DEV,)), pltpu.SemaphoreType.DMA((N_DEV,))],
        compiler_params=pltpu.CompilerParams(has_side_effects=True, vmem_limit_bytes=VMEM_LIMIT),
    )(pack)


def _adamw(g, w, m, v, name, g_block=0):
    r, cdim = w.shape
    tr = r if r * cdim * 4 <= (1 << 20) else _first_divisor(r, (128, 64, 32, 16, 8))
    bc1 = 1.0 - ADAM_B1 ** ADAM_STEP
    bc2 = 1.0 - ADAM_B2 ** ADAM_STEP

    def body(g_ref, w_ref, m_ref, v_ref, go_ref, d_ref, mo_ref, vo_ref):
        gv = g_ref[...]
        mn = ADAM_B1 * m_ref[...] + (1.0 - ADAM_B1) * gv
        vn = ADAM_B2 * v_ref[...] + (1.0 - ADAM_B2) * (gv * gv)
        go_ref[...] = gv
        mo_ref[...] = mn
        vo_ref[...] = vn
        d_ref[...] = -ADAM_LR * ((mn / bc1) / (jnp.sqrt(vn / bc2) + ADAM_EPS) + ADAM_WD * w_ref[...])

    spec = pl.BlockSpec((tr, cdim), lambda i: (i, 0))
    return pl.pallas_call(
        body,
        name=name,
        grid=(r // tr,),
        in_specs=[pl.BlockSpec((tr, cdim), lambda i: (i, g_block))] + [spec] * 3,
        out_specs=[spec] * 4,
        out_shape=[jax.ShapeDtypeStruct((r, cdim), F32)] * 4,
        compiler_params=_params(("parallel",)),
    )(g, w, m, v)


VEC_NAMES = ["pre_mix_norm", "q_norm", "kv_norm", "conv_b", "conv_ln_g", "conv_ln_b", "conv_out_norm",
             "attn_out_norm", "post_mix_norm", "pre_ffn_norm", "post_ffn_norm"]
LOSS_ROW = len(VEC_NAMES)
CONV_W_ROW = 16


def _cols_to_full(parts):
    _, r, cdim = parts.shape
    return parts.transpose(1, 0, 2).reshape(r, N_CHIPS * cdim)


def _full_to_cols(full):
    r, n = full.shape
    return full.reshape(r, N_CHIPS, n // N_CHIPS).transpose(1, 0, 2)


def _assemble_weights(g):
    w_in = _cols_to_full(g["w_in"])
    d = w_in.shape[0]
    w_a = w_in[:, :CONV_CH].reshape(d, CONV_CH // LANES, 1, LANES)
    w_g = w_in[:, CONV_CH:2 * CONV_CH].reshape(d, CONV_CH // LANES, 1, LANES)
    w_ag = jnp.concatenate([w_a, w_g], axis=2).reshape(d, 2 * CONV_CH)
    w_z2 = jnp.concatenate([w_in[:, 2 * CONV_CH:], jnp.zeros((d, LANES - QK_ROPE), w_in.dtype)], axis=1)
    uq = _cols_to_full(g["w_uq"]).reshape(Q_LORA, N_HEADS, QK_HEAD)
    w_uq = jnp.concatenate([uq[:, :, :QK_NOPE].reshape(Q_LORA, N_HEADS * QK_NOPE),
                            uq[:, :, QK_NOPE:].reshape(Q_LORA, N_HEADS * QK_ROPE)], axis=1)
    return dict(w_ag=w_ag, w_z2=w_z2, w_uq=w_uq, w_ukv=_cols_to_full(g["w_ukv"]), conv_w=_cols_to_full(g["conv_w"]),
                w_out=g["w_out"].reshape(-1, g["w_out"].shape[2]))


def _grads_to_parts(dw):
    d = dw["w_ag"].shape[0]
    ag = dw["w_ag"].reshape(d, CONV_CH // LANES, 2, LANES)
    d_in = jnp.concatenate([ag[:, :, 0, :].reshape(d, CONV_CH), ag[:, :, 1, :].reshape(d, CONV_CH),
                            dw["w_z2"][:, :Q_LORA + KV_LORA + QK_ROPE]], axis=1)
    uq = dw["w_uq"]
    d_uq = jnp.concatenate([uq[:, :N_HEADS * QK_NOPE].reshape(Q_LORA, N_HEADS, QK_NOPE),
                            uq[:, N_HEADS * QK_NOPE:].reshape(Q_LORA, N_HEADS, QK_ROPE)], axis=2).reshape(Q_LORA, N_HEADS * QK_HEAD)
    return dict(w_in=_full_to_cols(d_in), w_uq=_full_to_cols(d_uq), w_ukv=_full_to_cols(dw["w_ukv"]),
                w_out=dw["w_out"].reshape(N_CHIPS, -1, dw["w_out"].shape[1]))


MIXER = ["w_in", "w_uq", "w_ukv", "w_out"]
FFN = ["w_gu", "w_down"]
BIG = MIXER + FFN


def _pad_lanes(v, n):
    return jnp.pad(v, ((0, 0), (0, n - v.shape[1])))


def kernel(x, positions, pre_mix_norm, w_in, q_norm, w_uq, kv_norm, w_ukv, conv_w, conv_b, conv_ln_g, conv_ln_b, conv_out_norm, attn_out_norm, w_out, post_mix_norm, pre_ffn_norm, w_gate, w_up, w_down, post_ffn_norm, loss_target, m_pre_mix_norm, m_w_in, m_q_norm, m_w_uq, m_kv_norm, m_w_ukv, m_conv_w, m_conv_b, m_conv_ln_g, m_conv_ln_b, m_conv_out_norm, m_attn_out_norm, m_w_out, m_post_mix_norm, m_pre_ffn_norm, m_w_gate, m_w_up, m_w_down, m_post_ffn_norm, v_pre_mix_norm, v_w_in, v_q_norm, v_w_uq, v_kv_norm, v_w_ukv, v_conv_w, v_conv_b, v_conv_ln_g, v_conv_ln_b, v_conv_out_norm, v_attn_out_norm, v_w_out, v_post_mix_norm, v_pre_ffn_norm, v_w_gate, v_w_up, v_w_down, v_post_ffn_norm):
    given = dict(locals())
    names = ["pre_mix_norm", "w_in", "q_norm", "w_uq", "kv_norm", "w_ukv", "conv_w", "conv_b", "conv_ln_g", "conv_ln_b",
             "conv_out_norm", "attn_out_norm", "w_out", "post_mix_norm", "pre_ffn_norm", "w_gate", "w_up", "w_down", "post_ffn_norm"]
    def as_2d(a):
        return a if a.ndim == 2 else a[0]

    weights = {n: as_2d(given[n]) for n in names}
    mom = {n: as_2d(given["m_" + n]) for n in names}
    var = {n: as_2d(given["v_" + n]) for n in names}
    d = D_MODEL

    inv_freq = ROPE_THETA ** (-jnp.arange(0, QK_ROPE, 2, dtype=F32) / QK_ROPE)
    ang = positions[0].astype(F32)[:, None] * inv_freq
    cos, sin = jnp.cos(ang), jnp.sin(ang)
    cos2 = jnp.concatenate([cos, cos, cos, cos], axis=1)
    sin2 = jnp.concatenate([-sin, sin, -sin, sin], axis=1)

    chip = 2 * lax.axis_index("x") + lax.axis_index("y")
    core = lax.axis_index("c")
    chip1 = chip.astype(jnp.int32).reshape(1)
    pieces = {n: [weights[n]] for n in BIG if n != "w_gu"}
    pieces["w_gu"] = [weights["w_gate"], weights["w_up"]]
    core1 = core.astype(jnp.int32).reshape(1)
    place = jnp.stack([chip, core]).astype(jnp.int32)
    bufs = [_to_parts(chip1, pieces[n], BF16, "to_parts_" + n) for n in MIXER]
    bufs.append(_to_parts(chip1, [jnp.pad(weights["conv_w"], ((0, CONV_K_PAD - CONV_K), (0, 0)))], F32, "to_parts_conv_w"))
    bufs += [_to_parts(chip1, pieces[n], BF16, "to_parts_" + n) for n in FFN]
    n_mix = len(MIXER)
    whole = [False] * n_mix + [True, False, False]
    sems, thru = _gather_start(bufs, whole, [list(range(n_mix + 1)), [n_mix + 1], [n_mix + 2]])
    mixer = _gather_wait(thru[:n_mix + 1], whole[:n_mix + 1], *sems[0], cos2, "gather_wait_mixer")
    mixer = list(_gather_hand_on(mixer[:n_mix], "gather_hand_on_mixer")) + [mixer[n_mix]]
    full = _assemble_weights(dict(zip(MIXER + ["conv_w"], mixer)))
    vec = {n: weights[n] for n in VEC_NAMES}
    rs = {}

    def get_w_gu(after):
        got = _gather_wait([thru[n_mix + 1]], [False], *sems[1], after, "gather_wait_w_gu")
        return _gather_hand_on(got, "gather_hand_on_w_gu")[0]

    def get_w_down(after):
        got = _gather_wait([thru[n_mix + 2]], [False], *sems[2], after, "gather_wait_w_down")
        got = _gather_hand_on(got, "gather_hand_on_w_down")[0]
        return got.reshape(-1, got.shape[2])

    def ffn_grads(dw_gu, dw_down):
        plist = [dw_gu, dw_down.reshape(N_CHIPS, -1, dw_down.shape[1])]
        landed = _pair_exchange(plist, "grad_pair_exchange_ffn")
        sums = [_pair_sum(core1, a, b, "grad_pair_sum_" + n) for n, a, b in zip(FFN, plist, landed)]
        rs["send"], rs["recv"], rs["sums"], rs["slots"] = _chip_exchange_start(sums)
        return rs["sums"][0]

    loss, grad_x, dw, dvec = _local_step(x[0], loss_target[0], cos2, sin2, vec, full,
                                         dict(w_gu=get_w_gu, w_down=get_w_down, grads=ffn_grads))

    rows = [_pad_lanes(dvec[n], d) for n in VEC_NAMES] + [_pad_lanes(loss, d)]
    rows.append(jnp.zeros((CONV_W_ROW - len(rows), d), F32))
    rows.append(dw["conv_w"].reshape(SMALL_ROWS - CONV_W_ROW, d))
    small = _all_reduce_small(jnp.concatenate(rows, axis=0))
    g_conv_w_full = small[CONV_W_ROW:].reshape(CONV_K_PAD, CONV_CH)
    g_small = {n: small[i:i + 1, :weights[n].shape[1]] for i, n in enumerate(VEC_NAMES)}
    g_small["conv_w"] = lax.dynamic_slice(g_conv_w_full, (0, chip * (CONV_CH // N_CHIPS)), (CONV_K_PAD, CONV_CH // N_CHIPS))[:CONV_K]
    loss_out = small[LOSS_ROW, 0]

    parts = _grads_to_parts(dw)
    plist = [parts[n] for n in MIXER]
    landed = _pair_exchange(plist, "grad_pair_exchange_mixer")
    sums = [_pair_sum(core1, a, b, "grad_pair_sum_" + n) for n, a, b in zip(MIXER, plist, landed)]
    slots = list(_chip_exchange(sums))
    ffn_sums, ffn_slots = _chip_exchange_wait(rs["sums"], rs["slots"], rs["send"], rs["recv"], slots[0])
    halves = [_chip_sum(place, s, sl, "grad_chip_sum_" + n) for n, s, sl in zip(BIG, sums + ffn_sums, slots + ffn_slots)]
    g_big = dict(zip(BIG, _half_exchange(halves)))

    res = {}
    for n in names:
        if n in ("w_gate", "w_up"):
            g, blk = g_big["w_gu"], int(n == "w_up")
        else:
            g, blk = (g_big[n] if n in g_big else g_small[n]), 0
        res[n] = _adamw(g, weights[n], mom[n], var[n], "adamw_" + n, g_block=blk)
    outs = [loss_out, grad_x[None]]
    for i in range(4):
        outs += [res[n][i].reshape(given[n].shape) for n in names]
    return tuple(outs)
```

```python
import functools

import jax
import jax.numpy as jnp
from jax import lax
from jax.experimental import pallas as pl
from jax.experimental.pallas import tpu as pltpu

F32 = jnp.float32
BF16 = jnp.bfloat16

D_MODEL = 2048
CONV_CH = 1024
CONV_K = 31
CONV_K_PAD = 32
N_HEADS = 8
QK_NOPE = 128
QK_ROPE = 64
V_HEAD = 128
QK_HEAD = QK_NOPE + QK_ROPE
Q_LORA = 768
KV_LORA = 512
ATTN_CH = N_HEADS * V_HEAD
Z2_COLS = Q_LORA + KV_LORA + 128
D_FF = 5632
ROPE_THETA = 10000.0
EPS = 1e-6
LANES = 128
N_CHIPS = 4
N_DEV = 8

ADAM_LR = 0.001
ADAM_B1 = 0.9
ADAM_B2 = 0.999
ADAM_EPS = 1e-08
ADAM_WD = 0.01
ADAM_STEP = 10

VMEM_LIMIT = 56 * 1024 * 1024
ROW_TILE = 256
MESH = pl.DeviceIdType.MESH


def _params(sem=None):
    return pltpu.CompilerParams(dimension_semantics=sem, vmem_limit_bytes=VMEM_LIMIT)


def _first_divisor(n, cands):
    for c in cands:
        if n % c == 0:
            return c
    return n


def _matmul(pairs, mode, out_dtype, name, b_parts=None, out_parts=False, tiles=(None, None, None), after=None):
    a0, b0 = pairs[0]
    part_c = b0.shape[2] if b_parts else None
    if mode == "nn":
        m, n = a0.shape[0], (N_CHIPS * part_c if b_parts else b0.shape[1])
        ks = [a.shape[1] for a, _ in pairs]
    elif mode == "nt":
        m, n = a0.shape[0], b0.shape[-2]
        ks = [a.shape[1] for a, _ in pairs]
    else:
        m, n = a0.shape[1], b0.shape[1]
        ks = [a.shape[0] for a, _ in pairs]
    tm = tiles[0] or _first_divisor(m, (1024, 768, 512, 256))
    tn = tiles[1] or (n if n <= 1536 else _first_divisor(n, (1024, 512, 256, 128)))
    tks = [tiles[2] or (512 if k % 512 == 0 else k) for k in ks]
    nks = [k // tk for k, tk in zip(ks, tks)]
    offs = [sum(nks[:p]) for p in range(len(pairs))]
    nk = sum(nks)
    n_pairs = len(pairs)
    assert not (b_parts or out_parts) or n_pairs == 1

    def kk(k, p):
        return jnp.clip(k - offs[p], 0, nks[p] - 1)

    in_specs = []
    for p in range(n_pairs):
        tk = tks[p]
        if mode == "nn":
            in_specs.append(pl.BlockSpec((tm, tk), lambda i, j, k, p=p: (i, kk(k, p))))
            if b_parts == "n":
                per = part_c // tn
                in_specs.append(pl.BlockSpec((None, tk, tn), lambda i, j, k: (j // per, k, j % per)))
            else:
                in_specs.append(pl.BlockSpec((tk, tn), lambda i, j, k, p=p: (kk(k, p), j)))
        elif mode == "nt":
            in_specs.append(pl.BlockSpec((tm, tk), lambda i, j, k, p=p: (i, kk(k, p))))
            if b_parts == "k":
                per = part_c // tk
                in_specs.append(pl.BlockSpec((None, tn, tk), lambda i, j, k: (k // per, j, k % per)))
            else:
                in_specs.append(pl.BlockSpec((tn, tk), lambda i, j, k, p=p: (j, kk(k, p))))
        else:
            in_specs.append(pl.BlockSpec((tk, tm), lambda i, j, k, p=p: (kk(k, p), i)))
            in_specs.append(pl.BlockSpec((tk, tn), lambda i, j, k, p=p: (kk(k, p), j)))
    if out_parts:
        out_per = (n // N_CHIPS) // tn
        out_spec = pl.BlockSpec((None, tm, tn), lambda i, j, k: (j // out_per, i, j % out_per))
        out_shape = jax.ShapeDtypeStruct((N_CHIPS, m, n // N_CHIPS), out_dtype)
    else:
        out_spec = pl.BlockSpec((tm, tn), lambda i, j, k: (i, j))
        out_shape = jax.ShapeDtypeStruct((m, n), out_dtype)
    dims = {"nn": (((1,), (0,)), ((), ())), "nt": (((1,), (1,)), ((), ())), "tn": (((0,), (0,)), ((), ()))}[mode]

    n_after = 0 if after is None else 1

    def body(*refs):
        o_ref, acc = refs[2 * n_pairs + n_after], refs[2 * n_pairs + n_after + 1]
        k = pl.program_id(2)

        @pl.when(k == 0)
        def _():
            acc[...] = jnp.zeros_like(acc)

        for p in range(n_pairs):
            a_ref, b_ref = refs[2 * p], refs[2 * p + 1]

            def step(a_ref=a_ref, b_ref=b_ref):
                acc[...] += lax.dot_general(a_ref[...], b_ref[...], dims, preferred_element_type=F32)

            if n_pairs == 1:
                step()
            else:
                pl.when((k >= offs[p]) & (k < offs[p] + nks[p]))(step)

        @pl.when(k == nk - 1)
        def _():
            o_ref[...] = acc[...].astype(o_ref.dtype)

    flat = [t for pr in pairs for t in pr] + ([] if after is None else [after])
    return pl.pallas_call(
        body,
        name=name,
        grid=(m // tm, n // tn, nk),
        in_specs=in_specs + [pl.BlockSpec(memory_space=pl.ANY)] * n_after,
        out_specs=out_spec,
        out_shape=out_shape,
        scratch_shapes=[pltpu.VMEM((tm, tn), F32)],
        compiler_params=_params(("parallel", "parallel", "arbitrary")),
    )(*flat)


F4 = D_FF // N_CHIPS
FFN_TM = 1024
FFN_TK = 512


def _ffn_up(hf, w_gu):
    t, d = hf.shape
    nk = d // FFN_TK

    def body(a_ref, b_ref, gu_ref, act_ref, acc):
        k = pl.program_id(2)

        @pl.when(k == 0)
        def _():
            acc[...] = jnp.zeros_like(acc)

        acc[...] += jnp.dot(a_ref[...], b_ref[...], preferred_element_type=F32)

        @pl.when(k == nk - 1)
        def _():
            g = acc[:, :F4]
            gu_ref[...] = acc[...].astype(gu_ref.dtype)
            act_ref[...] = (g * _sigmoid(g) * acc[:, F4:]).astype(act_ref.dtype)

    return pl.pallas_call(
        body,
        name="ffn_up",
        grid=(t // FFN_TM, N_CHIPS, nk),
        in_specs=[pl.BlockSpec((FFN_TM, FFN_TK), lambda i, q, k: (i, k)),
                  pl.BlockSpec((None, FFN_TK, 2 * F4), lambda i, q, k: (q, k, 0))],
        out_specs=[pl.BlockSpec((FFN_TM, 2 * F4), lambda i, q, k: (i, q)),
                   pl.BlockSpec((FFN_TM, F4), lambda i, q, k: (i, q))],
        out_shape=[jax.ShapeDtypeStruct((t, 2 * D_FF), BF16), jax.ShapeDtypeStruct((t, D_FF), BF16)],
        scratch_shapes=[pltpu.VMEM((FFN_TM, 2 * F4), F32)],
        compiler_params=_params(("parallel", "parallel", "arbitrary")),
    )(hf, w_gu)


def _ffn_down_dx(d_ff, w_down, gu):
    t, d = d_ff.shape
    nk = d // FFN_TK

    def body(a_ref, b_ref, gu_ref, o_ref, acc):
        k = pl.program_id(2)

        @pl.when(k == 0)
        def _():
            acc[...] = jnp.zeros_like(acc)

        acc[...] += lax.dot_general(a_ref[...], b_ref[...], NT_DIMS, preferred_element_type=F32)

        @pl.when(k == nk - 1)
        def _():
            g = gu_ref[:, :F4].astype(F32)
            u = gu_ref[:, F4:].astype(F32)
            sg = _sigmoid(g)
            d_act = acc[...]
            o_ref[:, :F4] = (d_act * u * sg * (1.0 + g * (1.0 - sg))).astype(o_ref.dtype)
            o_ref[:, F4:] = (d_act * g * sg).astype(o_ref.dtype)

    return pl.pallas_call(
        body,
        name="ffn_down_dx",
        grid=(t // FFN_TM, N_CHIPS, nk),
        in_specs=[pl.BlockSpec((FFN_TM, FFN_TK), lambda i, q, k: (i, k)),
                  pl.BlockSpec((F4, FFN_TK), lambda i, q, k: (q, k)),
                  pl.BlockSpec((FFN_TM, 2 * F4), lambda i, q, k: (i, q))],
        out_specs=pl.BlockSpec((FFN_TM, 2 * F4), lambda i, q, k: (i, q)),
        out_shape=jax.ShapeDtypeStruct((t, 2 * D_FF), BF16),
        scratch_shapes=[pltpu.VMEM((FFN_TM, F4), F32)],
        compiler_params=_params(("parallel", "parallel", "arbitrary")),
    )(d_ff, w_down, gu)


def _rowwise(fn, row_ins, vec_ins, row_outs, acc_outs, name):
    t = row_ins[0].shape[0]
    tm = ROW_TILE
    n_in = len(row_ins) + len(vec_ins)
    n_row = len(row_outs)

    def body(*refs):
        ins = [r[...] for r in refs[:n_in]]
        outs = refs[n_in:]
        vals = fn(*ins)
        for r, v in zip(outs[:n_row], vals[:n_row]):
            r[...] = v.astype(r.dtype)
        if acc_outs:
            @pl.when(pl.program_id(0) == 0)
            def _():
                for r in outs[n_row:]:
                    r[...] = jnp.zeros_like(r)

            for r, v in zip(outs[n_row:], vals[n_row:]):
                r[...] += v

    in_specs = [pl.BlockSpec((tm, a.shape[1]), lambda i: (i, 0)) for a in row_ins]
    in_specs += [pl.BlockSpec(a.shape, lambda i: (0, 0)) for a in vec_ins]
    out_specs = [pl.BlockSpec((tm, c), lambda i: (i, 0)) for c, _ in row_outs]
    out_specs += [pl.BlockSpec((1, c), lambda i: (0, 0)) for c in acc_outs]
    out_shape = [jax.ShapeDtypeStruct((t, c), dt) for c, dt in row_outs]
    out_shape += [jax.ShapeDtypeStruct((1, c), F32) for c in acc_outs]
    return pl.pallas_call(
        body,
        name=name,
        grid=(t // tm,),
        in_specs=in_specs,
        out_specs=out_specs,
        out_shape=out_shape,
        compiler_params=_params(("arbitrary",)),
    )(*row_ins, *vec_ins)


def _mean(v):
    return jnp.mean(v, axis=-1, keepdims=True)


def _colsum(v):
    return jnp.sum(v, axis=0, keepdims=True)


def _rms_fwd(v, g):
    r = lax.rsqrt(_mean(v * v) + EPS)
    vhat = v * r
    return vhat * g, vhat, r


def _rms_bwd(dn, vhat, r, g):
    dng = dn * g
    return r * (dng - vhat * _mean(dng * vhat)), _colsum(dn * vhat)


def _swap_rope_halves(v):
    n = v.shape[-1]
    lane = lax.broadcasted_iota(jnp.int32, v.shape, v.ndim - 1)
    return jnp.where(lane % QK_ROPE < QK_ROPE // 2, pltpu.roll(v, n - QK_ROPE // 2, v.ndim - 1),
                     pltpu.roll(v, QK_ROPE // 2, v.ndim - 1))


def _rope(v, cos2, sin2):
    return v * cos2 + _swap_rope_halves(v) * sin2


def _rope_transposed(dv, cos2, sin2):
    return dv * cos2 + _swap_rope_halves(dv * sin2)


def _sigmoid(v):
    return 1.0 / (1.0 + jnp.exp(-v))


CONV_ROWS = 256


def _conv_fwd(ag, conv_w, conv_b):
    t = ag.shape[0]
    cb = LANES

    def body(ag_ref, w_ref, b_ref, o_ref, scr):
        a = ag_ref[:, :cb].astype(F32)
        g = ag_ref[:, cb:].astype(F32)
        scr[pl.ds(0, CONV_K_PAD), :] = jnp.zeros((CONV_K_PAD, cb), F32)
        scr[pl.ds(CONV_K_PAD, t), :] = a * _sigmoid(g)
        for r0 in range(0, t, CONV_ROWS):
            acc = jnp.zeros((CONV_ROWS, cb), F32) + b_ref[...]
            for k in range(CONV_K):
                acc = acc + w_ref[k:k + 1, :] * scr[pl.ds(r0 + CONV_K_PAD - (CONV_K - 1) + k, CONV_ROWS), :]
            o_ref[pl.ds(r0, CONV_ROWS), :] = acc

    return pl.pallas_call(
        body,
        name="conv_fwd",
        grid=(CONV_CH // cb,),
        in_specs=[pl.BlockSpec((t, 2 * cb), lambda j: (0, j)),
                  pl.BlockSpec((CONV_K_PAD, cb), lambda j: (0, j)),
                  pl.BlockSpec((1, cb), lambda j: (0, j))],
        out_specs=pl.BlockSpec((t, cb), lambda j: (0, j)),
        out_shape=jax.ShapeDtypeStruct((t, CONV_CH), F32),
        scratch_shapes=[pltpu.VMEM((t + CONV_K_PAD, cb), F32)],
        compiler_params=_params(("parallel",)),
    )(ag, conv_w, conv_b)


def _conv_bwd(d_u1, ag, conv_w):
    t = ag.shape[0]
    cb = LANES

    def body(du_ref, ag_ref, w_ref, dag_ref, dw_ref, db_ref, su, sd):
        a = ag_ref[:, :cb].astype(F32)
        g = ag_ref[:, cb:].astype(F32)
        sg = _sigmoid(g)
        su[pl.ds(0, CONV_K_PAD), :] = jnp.zeros((CONV_K_PAD, cb), F32)
        su[pl.ds(CONV_K_PAD, t), :] = a * sg
        sd[pl.ds(0, t), :] = du_ref[...]
        sd[pl.ds(t, CONV_K_PAD), :] = jnp.zeros((CONV_K_PAD, cb), F32)
        db_ref[...] = _colsum(du_ref[...])
        dw_ref[...] = jnp.zeros_like(dw_ref)
        for r0 in range(0, t, CONV_ROWS):
            du = sd[pl.ds(r0, CONV_ROWS), :]
            acc = jnp.zeros((CONV_ROWS, cb), F32)
            for k in range(CONV_K):
                acc = acc + w_ref[k:k + 1, :] * sd[pl.ds(r0 + (CONV_K - 1) - k, CONV_ROWS), :]
                dw_ref[k:k + 1, :] += _colsum(du * su[pl.ds(r0 + CONV_K_PAD - (CONV_K - 1) + k, CONV_ROWS), :])
            sgc = sg[r0:r0 + CONV_ROWS]
            ac = a[r0:r0 + CONV_ROWS]
            dag_ref[pl.ds(r0, CONV_ROWS), :cb] = (acc * sgc).astype(dag_ref.dtype)
            dag_ref[pl.ds(r0, CONV_ROWS), cb:] = (acc * ac * sgc * (1.0 - sgc)).astype(dag_ref.dtype)

    return pl.pallas_call(
        body,
        name="conv_bwd",
        grid=(CONV_CH // cb,),
        in_specs=[pl.BlockSpec((t, cb), lambda j: (0, j)),
                  pl.BlockSpec((t, 2 * cb), lambda j: (0, j)),
                  pl.BlockSpec((CONV_K_PAD, cb), lambda j: (0, j))],
        out_specs=[pl.BlockSpec((t, 2 * cb), lambda j: (0, j)),
                   pl.BlockSpec((CONV_K_PAD, cb), lambda j: (0, j)),
                   pl.BlockSpec((1, cb), lambda j: (0, j))],
        out_shape=[jax.ShapeDtypeStruct((t, 2 * CONV_CH), BF16),
                   jax.ShapeDtypeStruct((CONV_K_PAD, CONV_CH), F32),
                   jax.ShapeDtypeStruct((1, CONV_CH), F32)],
        scratch_shapes=[pltpu.VMEM((t + CONV_K_PAD, cb), F32), pltpu.VMEM((t + CONV_K_PAD, cb), F32)],
        compiler_params=_params(("parallel",)),
    )(d_u1, ag, conv_w)


ATT_TQ = 256
NEG = float(jnp.finfo(jnp.float32).min)
SCALE = QK_HEAD ** -0.5
NT_DIMS = (((1,), (1,)), ((), ()))
TN_DIMS = (((0,), (0,)), ((), ()))


def _att_probs(qf, kf, row0):
    s = lax.dot_general(qf, kf, NT_DIMS, preferred_element_type=F32) * SCALE
    tq, t = s.shape
    qpos = row0 + lax.broadcasted_iota(jnp.int32, (tq, t), 0)
    kpos = lax.broadcasted_iota(jnp.int32, (tq, t), 1)
    s = jnp.where(kpos <= qpos, s, NEG)
    p = jnp.exp(s - jnp.max(s, axis=-1, keepdims=True))
    return p * (1.0 / jnp.sum(p, axis=-1, keepdims=True))


def _half_mask(shape, which):
    lane = lax.broadcasted_iota(jnp.int32, shape, len(shape) - 1)
    return (lane // QK_ROPE == which).astype(F32)


def _attention_fwd(q, kv, kpe2, cos2, sin2):
    t = q.shape[0]
    tq = ATT_TQ

    def body(qn_ref, qp_ref, c_ref, s_ref, kv_ref, kpe_ref, o_ref):
        row0 = pl.program_id(1) * tq
        roped = _rope(qp_ref[...].astype(F32), c_ref[...], s_ref[...])
        for e in range(2):
            qf = jnp.concatenate([qn_ref[:, e * QK_NOPE:(e + 1) * QK_NOPE],
                                  (roped * _half_mask(roped.shape, e)).astype(BF16)], axis=1)
            kf = jnp.concatenate([kv_ref[:, e * 256:e * 256 + QK_NOPE], kpe_ref[...]], axis=1)
            p = _att_probs(qf, kf, row0)
            v = kv_ref[:, e * 256 + QK_NOPE:(e + 1) * 256]
            o_ref[:, e * V_HEAD:(e + 1) * V_HEAD] = jnp.dot(p.astype(BF16), v, preferred_element_type=F32).astype(o_ref.dtype)

    return pl.pallas_call(
        body,
        name="attention_fwd",
        grid=(N_HEADS // 2, t // tq),
        in_specs=[pl.BlockSpec((tq, 2 * QK_NOPE), lambda h, i: (i, h)),
                  pl.BlockSpec((tq, LANES), lambda h, i: (i, N_HEADS + h)),
                  pl.BlockSpec((tq, LANES), lambda h, i: (i, 0)),
                  pl.BlockSpec((tq, LANES), lambda h, i: (i, 0)),
                  pl.BlockSpec((t, 512), lambda h, i: (0, h)),
                  pl.BlockSpec((t, LANES), lambda h, i: (0, 0))],
        out_specs=pl.BlockSpec((tq, 2 * V_HEAD), lambda h, i: (i, h)),
        out_shape=jax.ShapeDtypeStruct((t, ATTN_CH), BF16),
        compiler_params=_params(("parallel", "parallel")),
    )(q, q, cos2, sin2, kv, kpe2)


def _attention_bwd(q, kv, kpe2, cos2, sin2, d_attn):
    t = q.shape[0]
    tq = ATT_TQ
    n_q = t // tq

    def body(qn_ref, qp_ref, c_ref, s_ref, kv_ref, kpe_ref, do_ref, dqn_ref, dqp_ref, dkv_ref, dkpe_ref, dkv_acc):
        h, i = pl.program_id(0), pl.program_id(1)
        row0 = i * tq

        @pl.when(i == 0)
        def _():
            dkv_acc[...] = jnp.zeros_like(dkv_acc)

        @pl.when((i == 0) & (h == 0))
        def _():
            dkpe_ref[...] = jnp.zeros_like(dkpe_ref)

        roped = _rope(qp_ref[...].astype(F32), c_ref[...], s_ref[...])
        d_roped = jnp.zeros((tq, LANES), F32)
        for e in range(2):
            mask = _half_mask(roped.shape, e)
            qf = jnp.concatenate([qn_ref[:, e * QK_NOPE:(e + 1) * QK_NOPE], (roped * mask).astype(BF16)], axis=1)
            kf = jnp.concatenate([kv_ref[:, e * 256:e * 256 + QK_NOPE], kpe_ref[...]], axis=1)
            v = kv_ref[:, e * 256 + QK_NOPE:(e + 1) * 256]
            do = do_ref[:, e * V_HEAD:(e + 1) * V_HEAD]
            p = _att_probs(qf, kf, row0)
            dp = lax.dot_general(do, v, NT_DIMS, preferred_element_type=F32)
            ds = (p * (dp - jnp.sum(p * dp, axis=-1, keepdims=True)) * SCALE).astype(BF16)
            dqf = jnp.dot(ds, kf, preferred_element_type=F32)
            dkf = lax.dot_general(ds, qf, TN_DIMS, preferred_element_type=F32)
            dv = lax.dot_general(p.astype(BF16), do, TN_DIMS, preferred_element_type=F32)
            dqn_ref[:, e * QK_NOPE:(e + 1) * QK_NOPE] = dqf[:, :QK_NOPE].astype(dqn_ref.dtype)
            d_roped = d_roped + dqf[:, QK_NOPE:] * mask
            dkv_acc[:, e * 256:e * 256 + QK_NOPE] += dkf[:, :QK_NOPE]
            dkv_acc[:, e * 256 + QK_NOPE:(e + 1) * 256] += dv
            dkpe_ref[...] += dkf[:, QK_NOPE:]
        dqp_ref[...] = _rope_transposed(d_roped, c_ref[...], s_ref[...]).astype(dqp_ref.dtype)

        @pl.when(i == n_q - 1)
        def _():
            dkv_ref[...] = dkv_acc[...].astype(dkv_ref.dtype)

    return pl.pallas_call(
        body,
        name="attention_bwd",
        grid=(N_HEADS // 2, n_q),
        in_specs=[pl.BlockSpec((tq, 2 * QK_NOPE), lambda h, i: (i, h)),
                  pl.BlockSpec((tq, LANES), lambda h, i: (i, N_HEADS + h)),
                  pl.BlockSpec((tq, LANES), lambda h, i: (i, 0)),
                  pl.BlockSpec((tq, LANES), lambda h, i: (i, 0)),
                  pl.BlockSpec((t, 512), lambda h, i: (0, h)),
                  pl.BlockSpec((t, LANES), lambda h, i: (0, 0)),
                  pl.BlockSpec((tq, 2 * V_HEAD), lambda h, i: (i, h))],
        out_specs=[pl.BlockSpec((tq, 2 * QK_NOPE), lambda h, i: (i, h)),
                   pl.BlockSpec((tq, LANES), lambda h, i: (i, h)),
                   pl.BlockSpec((t, 512), lambda h, i: (0, h)),
                   pl.BlockSpec((t, LANES), lambda h, i: (0, 0))],
        out_shape=[jax.ShapeDtypeStruct((t, N_HEADS * QK_NOPE), BF16),
                   jax.ShapeDtypeStruct((t, N_HEADS * QK_ROPE), BF16),
                   jax.ShapeDtypeStruct((t, N_HEADS * 256), BF16),
                   jax.ShapeDtypeStruct((t, LANES), F32)],
        scratch_shapes=[pltpu.VMEM((t, 512), F32)],
        compiler_params=_params(("arbitrary", "arbitrary")),
    )(q, q, cos2, sin2, kv, kpe2, d_attn)


def _local_step(x, target, cos2, sin2, vec, w, ffn):
    d = D_MODEL

    (h,) = _rowwise(lambda xv, g: (_rms_fwd(xv, g)[0],), [x], [vec["pre_mix_norm"]], [(d, BF16)], [], "pre_mix_norm_fwd")
    ag = _matmul([(h, w["w_ag"])], "nn", BF16, "in_proj_ag")
    z2 = _matmul([(h, w["w_z2"])], "nn", BF16, "in_proj_z2")
    w = {**w, **ffn["mixer_rest"](z2)}
    u1 = _conv_fwd(ag, w["conv_w"], vec["conv_b"])

    def latents_fwd(z, c2, s2, qg, kvg):
        z = z.astype(F32)
        qn = _rms_fwd(z[:, :Q_LORA], qg)[0]
        kvn = _rms_fwd(z[:, Q_LORA:Q_LORA + KV_LORA], kvg)[0]
        kr = z[:, Q_LORA + KV_LORA:]
        kr2 = kr + pltpu.roll(kr, QK_ROPE, 1)
        return qn, kvn, _rope(kr2, c2, s2)

    qn, kvn, kpe2 = _rowwise(latents_fwd, [z2, cos2, sin2], [vec["q_norm"], vec["kv_norm"]],
                             [(Q_LORA, BF16), (KV_LORA, BF16), (LANES, BF16)], [], "latents_fwd")
    q = _matmul([(qn, w["w_uq"])], "nn", BF16, "q_up")
    kv = _matmul([(kvn, w["w_ukv"])], "nn", BF16, "kv_up")
    attn = _attention_fwd(q, kv, kpe2, cos2, sin2)

    def conv_post(u, lg, lb):
        mu = _mean(u)
        uc = u - mu
        rstd = lax.rsqrt(_mean(uc * uc) + EPS)
        uhat = uc * rstd
        u2 = uhat * lg + lb
        sg = _sigmoid(u2)
        return uhat, rstd, u2, sg, u2 * sg

    def mix_in_fwd(u, at, lg, lb, cg, ag_):
        u3 = conv_post(u, lg, lb)[4]
        cn = _rms_fwd(u3, cg)[0]
        an = _rms_fwd(at.astype(F32), ag_)[0]
        return (jnp.concatenate([cn, an], axis=1),)

    (cat,) = _rowwise(mix_in_fwd, [u1, attn], [vec["conv_ln_g"], vec["conv_ln_b"], vec["conv_out_norm"], vec["attn_out_norm"]],
                      [(2 * CONV_CH, BF16)], [], "mix_in_fwd")
    mix = _matmul([(cat, w["w_out"])], "nn", F32, "out_proj")

    def residual1(xv, mv, gpm, gpf):
        x1 = xv + _rms_fwd(mv, gpm)[0]
        return x1, _rms_fwd(x1, gpf)[0]

    x1, hf = _rowwise(residual1, [x, mix], [vec["post_mix_norm"], vec["pre_ffn_norm"]], [(d, F32), (d, BF16)], [], "residual1_fwd")
    w_gu = ffn["w_gu"](hf)
    gu, act = _ffn_up(hf, w_gu)
    w_down = ffn["w_down"](act)
    ff = _matmul([(act, w_down)], "nn", F32, "ffn_down")

    def loss_head(x1v, ffv, tg, g):
        n, fhat, r = _rms_fwd(ffv, g)
        err = x1v + n - tg
        loss = 0.5 * jnp.sum(_mean(err * err), axis=0, keepdims=True)
        dy = err * (1.0 / d)
        d_ff, dg = _rms_bwd(dy, fhat, r, g)
        return dy, d_ff, dg, jnp.broadcast_to(loss, (1, LANES))

    dy, d_ff, g_post_ffn, loss = _rowwise(loss_head, [x1, ff, target], [vec["post_ffn_norm"]],
                                          [(d, F32), (d, BF16)], [d, LANES], "loss_head")
    d_gu = _ffn_down_dx(d_ff, w_down, gu)
    dw_down = _matmul([(act, d_ff)], "tn", BF16, "ffn_down_dw", tiles=(F4, None, None))
    dw_gu = _matmul([(hf, d_gu)], "tn", BF16, "ffn_gate_up_dw", out_parts=True, tiles=(None, F4, None))
    started = ffn["grads"](dw_gu, dw_down)
    d_hf = _matmul([(d_gu, w_gu)], "nt", F32, "ffn_gate_up_dx", b_parts="k", tiles=(None, None, F4), after=started)

    def residual1_bwd(dyv, dhf, x1v, mv, gpf, gpm):
        _, x1hat, r1 = _rms_fwd(x1v, gpf)
        dn, dgpf = _rms_bwd(dhf, x1hat, r1, gpf)
        d_x1 = dyv + dn
        _, mhat, rm = _rms_fwd(mv, gpm)
        d_mix, dgpm = _rms_bwd(d_x1, mhat, rm, gpm)
        return d_x1, d_mix, dgpf, dgpm

    d_x1, d_mix, g_pre_ffn, g_post_mix = _rowwise(residual1_bwd, [dy, d_hf, x1, mix], [vec["pre_ffn_norm"], vec["post_mix_norm"]],
                                                  [(d, F32), (d, BF16)], [d, d], "residual1_bwd")
    d_cat = _matmul([(d_mix, w["w_out"])], "nt", BF16, "out_proj_dx")
    dw_out = _matmul([(cat, d_mix)], "tn", BF16, "out_proj_dw")

    def mix_in_bwd(dc, u, at, lg, lb, cg, ag_):
        dc = dc.astype(F32)
        uhat, rstd, u2, sg, u3 = conv_post(u, lg, lb)
        _, u3hat, rc = _rms_fwd(u3, cg)
        d_u3, dcg = _rms_bwd(dc[:, :CONV_CH], u3hat, rc, cg)
        d_u2 = d_u3 * sg * (1.0 + u2 * (1.0 - sg))
        dgl = d_u2 * lg
        d_u1 = rstd * (dgl - _mean(dgl) - uhat * _mean(dgl * uhat))
        _, ahat, ra = _rms_fwd(at.astype(F32), ag_)
        d_at, dag = _rms_bwd(dc[:, CONV_CH:], ahat, ra, ag_)
        return d_u1, d_at, dcg, _colsum(d_u2 * uhat), _colsum(d_u2), dag

    d_u1, d_attn, g_conv_out, g_ln_g, g_ln_b, g_attn_out = _rowwise(
        mix_in_bwd, [d_cat, u1, attn], [vec["conv_ln_g"], vec["conv_ln_b"], vec["conv_out_norm"], vec["attn_out_norm"]],
        [(CONV_CH, F32), (ATTN_CH, BF16)], [CONV_CH] * 4, "mix_in_bwd")
    d_ag, d_conv_w, g_conv_b = _conv_bwd(d_u1, ag, w["conv_w"])
    d_qn_, d_qp_, d_kv, d_kpe2 = _attention_bwd(q, kv, kpe2, cos2, sin2, d_attn)
    d_q = jnp.concatenate([d_qn_, d_qp_], axis=1)
    d_qn = _matmul([(d_q, w["w_uq"])], "nt", BF16, "q_up_dx")
    dw_uq = _matmul([(qn, d_q)], "tn", BF16, "q_up_dw")
    d_kvn = _matmul([(d_kv, w["w_ukv"])], "nt", BF16, "kv_up_dx")
    dw_ukv = _matmul([(kvn, d_kv)], "tn", BF16, "kv_up_dw")

    def latents_bwd(z, dq, dk, dkp, c2, s2, qg, kvg):
        z = z.astype(F32)
        _, qhat, rq = _rms_fwd(z[:, :Q_LORA], qg)
        d_ql, dqg = _rms_bwd(dq.astype(F32), qhat, rq, qg)
        _, khat, rk = _rms_fwd(z[:, Q_LORA:Q_LORA + KV_LORA], kvg)
        d_kl, dkg = _rms_bwd(dk.astype(F32), khat, rk, kvg)
        both = dkp + pltpu.roll(dkp, QK_ROPE, 1)
        d_kr = _rope_transposed(both, c2, s2) * _half_mask(both.shape, 0)
        return jnp.concatenate([d_ql, d_kl, d_kr], axis=1), dqg, dkg

    d_z2, g_q_norm, g_kv_norm = _rowwise(latents_bwd, [z2, d_qn, d_kvn, d_kpe2, cos2, sin2], [vec["q_norm"], vec["kv_norm"]],
                                         [(Z2_COLS, BF16)], [Q_LORA, KV_LORA], "latents_bwd")
    d_h = _matmul([(d_ag, w["w_ag"]), (d_z2, w["w_z2"])], "nt", F32, "in_proj_dx")
    dw_ag = _matmul([(h, d_ag)], "tn", BF16, "in_proj_ag_dw")
    dw_z2 = _matmul([(h, d_z2)], "tn", BF16, "in_proj_z2_dw")

    def pre_mix_bwd(dx1, dh, xv, g):
        _, xhat, r = _rms_fwd(xv, g)
        dn, dg = _rms_bwd(dh, xhat, r, g)
        return dx1 + dn, dg

    grad_x, g_pre_mix = _rowwise(pre_mix_bwd, [d_x1, d_h, x], [vec["pre_mix_norm"]], [(d, F32)], [d], "pre_mix_norm_bwd")

    dw = dict(w_ag=dw_ag, w_z2=dw_z2, w_uq=dw_uq, w_ukv=dw_ukv, conv_w=d_conv_w, w_out=dw_out, w_gu=dw_gu, w_down=dw_down)
    dvec = dict(pre_mix_norm=g_pre_mix, q_norm=g_q_norm, kv_norm=g_kv_norm, conv_b=g_conv_b, conv_ln_g=g_ln_g,
                conv_ln_b=g_ln_b, conv_out_norm=g_conv_out, attn_out_norm=g_attn_out, post_mix_norm=g_post_mix,
                pre_ffn_norm=g_pre_ffn, post_ffn_norm=g_post_ffn)
    return loss, grad_x, dw, dvec


ANY = pl.BlockSpec(memory_space=pl.ANY)


def _place():
    x, y, c = lax.axis_index("x"), lax.axis_index("y"), lax.axis_index("c")
    chips = [(1 - x, y), (x, 1 - y), (1 - x, 1 - y)]
    return x, y, c, chips


def _to_parts(chip, pieces, dtype, name):
    r = pieces[0].shape[0]
    widths = [a.shape[1] for a in pieces]
    tr = r if r <= 512 else _first_divisor(r, (512, 256, 128))

    def body(p_ref, *refs):
        o_ref = refs[len(pieces)]
        off = 0
        for a_ref, wdt in zip(refs, widths):
            o_ref[:, off:off + wdt] = a_ref[...].astype(o_ref.dtype)
            off += wdt

    return pl.pallas_call(
        body,
        name=name,
        grid_spec=pltpu.PrefetchScalarGridSpec(
            num_scalar_prefetch=1,
            grid=(r // tr,),
            in_specs=[pl.BlockSpec((tr, wdt), lambda i, p_ref: (i, 0)) for wdt in widths],
            out_specs=pl.BlockSpec((None, tr, sum(widths)), lambda i, p_ref: (p_ref[0], i, 0))),
        out_shape=jax.ShapeDtypeStruct((N_CHIPS, r, sum(widths)), dtype),
        compiler_params=_params(("parallel",)),
    )(chip, *pieces)


HBM = pl.BlockSpec(memory_space=pltpu.HBM)
SEM = pl.BlockSpec(memory_space=pltpu.SEMAPHORE)
EFFECT = pltpu.SideEffectType.DATAFLOW_SIDE_EFFECTING


def _in_hbm(a):
    return pltpu.with_memory_space_constraint(a, pltpu.HBM)


def _gather_rows(buf, whole, half):
    r = buf.shape[1]
    return pl.ds(0, r) if whole else pl.ds(half * (r // 2), r // 2)


def _gather_start(bufs, whole, groups, name):
    n = len(bufs)
    n_g = len(groups)

    def body(*refs):
        sems = refs[n:n + 2 * n_g]
        outs = refs[n + 2 * n_g:2 * n + 2 * n_g]
        x, y, c, chips = _place()
        p = 2 * x + y
        for gi, group in enumerate(groups):
            for ki, k in enumerate(group):
                blk = outs[k].at[p, _gather_rows(bufs[k], whole[k], c), :]
                for j, (px, py) in enumerate(chips):
                    pltpu.make_async_remote_copy(src_ref=blk, dst_ref=blk, send_sem=sems[2 * gi].at[3 * ki + j],
                                                 recv_sem=sems[2 * gi + 1].at[3 * ki + j],
                                                 device_id=(px, py, c), device_id_type=MESH).start()

    sem_shapes = []
    for group in groups:
        sem_shapes += [pltpu.SemaphoreType.DMA((3 * len(group),))] * 2
    res = pl.pallas_call(
        body,
        name=name,
        in_specs=[HBM] * n,
        out_specs=[SEM] * (2 * n_g) + [HBM] * n,
        out_shape=sem_shapes + [pltpu.HBM(a.shape, a.dtype) for a in bufs],
        input_output_aliases={k: 2 * n_g + k for k in range(n)},
        compiler_params=pltpu.CompilerParams(has_side_effects=EFFECT),
    )(*[_in_hbm(a) for a in bufs])
    sems = [(res[2 * gi], res[2 * gi + 1]) for gi in range(n_g)]
    return sems, list(res[2 * n_g:2 * n_g + n])


def _gather_wait(bufs, whole, send, recv, after, name):
    n = len(bufs)

    def body(*refs):
        ins = refs[:n]
        send_ref, recv_ref = refs[n], refs[n + 1]
        x, y, c, chips = _place()
        p = 2 * x + y
        for ki in range(n):
            rows = _gather_rows(bufs[ki], whole[ki], c)
            for j, (px, py) in enumerate(chips):
                cp = pltpu.make_async_remote_copy(src_ref=ins[ki].at[p, rows, :], dst_ref=ins[ki].at[2 * px + py, rows, :],
                                                  send_sem=send_ref.at[3 * ki + j], recv_sem=recv_ref.at[3 * ki + j],
                                                  device_id=(px, py, c), device_id_type=MESH)
                cp.wait_send()
                cp.wait_recv()

    res = pl.pallas_call(
        body,
        name=name,
        in_specs=[HBM] * n + [SEM, SEM, ANY],
        out_specs=[HBM] * n,
        out_shape=[pltpu.HBM(a.shape, a.dtype) for a in bufs],
        input_output_aliases={k: k for k in range(n)},
        compiler_params=pltpu.CompilerParams(has_side_effects=EFFECT),
    )(*bufs, send, recv, after)
    return list(res)


def _gather_hand_on(bufs, name):
    n = len(bufs)

    def body(*refs):
        outs = refs[n:2 * n]
        send, recv = refs[2 * n:]
        x, y, c, chips = _place()

        def d2d(k, j, half):
            px, py = chips[j]
            blk = outs[k].at[2 * px + py, _gather_rows(bufs[k], False, half), :]
            return pltpu.make_async_remote_copy(src_ref=blk, dst_ref=blk, send_sem=send.at[3 * k + j], recv_sem=recv.at[3 * k + j],
                                                device_id=(x, y, 1 - c), device_id_type=MESH)

        sent = [d2d(k, j, c) for k in range(n) for j in range(3)]
        for cp in sent:
            cp.start()
        for k in range(n):
            for j in range(3):
                d2d(k, j, 1 - c).wait_recv()
        for cp in sent:
            cp.wait_send()

    return pl.pallas_call(
        body,
        name=name,
        in_specs=[ANY] * n,
        out_specs=[ANY] * n,
        out_shape=[jax.ShapeDtypeStruct(a.shape, a.dtype) for a in bufs],
        input_output_aliases={k: k for k in range(n)},
        scratch_shapes=[pltpu.SemaphoreType.DMA((3 * n,)), pltpu.SemaphoreType.DMA((3 * n,))],
        compiler_params=pltpu.CompilerParams(has_side_effects=True),
    )(*bufs)


def _chip_exchange_start(sums, name):
    n = len(sums)

    def body(*refs):
        send, recv = refs[2 * n], refs[2 * n + 1]
        src = refs[2 * n + 2:3 * n + 2]
        dst = refs[3 * n + 2:4 * n + 2]
        x, y, c, chips = _place()
        p = 2 * x + y
        for k in range(n):
            for j, (px, py) in enumerate(chips):
                pltpu.make_async_remote_copy(src_ref=src[k].at[2 * px + py], dst_ref=dst[k].at[p],
                                             send_sem=send.at[3 * k + j], recv_sem=recv.at[3 * k + j],
                                             device_id=(px, py, c), device_id_type=MESH).start()

    res = pl.pallas_call(
        body,
        name=name,
        in_specs=[HBM] * (2 * n),
        out_specs=[SEM, SEM] + [HBM] * (2 * n),
        out_shape=[pltpu.SemaphoreType.DMA((3 * n,))] * 2 + [pltpu.HBM(a.shape, a.dtype) for a in sums] * 2,
        input_output_aliases={k: 2 + k for k in range(2 * n)},
        compiler_params=pltpu.CompilerParams(has_side_effects=EFFECT),
    )(*[_in_hbm(a) for a in sums], *[_in_hbm(lax.empty(a.shape, a.dtype)) for a in sums])
    return res[0], res[1], list(res[2:2 + n]), list(res[2 + n:2 + 2 * n])


def _chip_exchange_wait(sums, slots, send, recv, after, name):
    n = len(sums)

    def body(*refs):
        src, dst = refs[:n], refs[n:2 * n]
        send_ref, recv_ref = refs[2 * n], refs[2 * n + 1]
        x, y, c, chips = _place()
        for k in range(n):
            for j, (px, py) in enumerate(chips):
                cp = pltpu.make_async_remote_copy(src_ref=src[k].at[2 * px + py], dst_ref=dst[k].at[2 * px + py],
                                                  send_sem=send_ref.at[3 * k + j], recv_sem=recv_ref.at[3 * k + j],
                                                  device_id=(px, py, c), device_id_type=MESH)
                cp.wait_send()
                cp.wait_recv()

    res = pl.pallas_call(
        body,
        name=name,
        in_specs=[HBM] * (2 * n) + [SEM, SEM, ANY],
        out_specs=[HBM] * (2 * n),
        out_shape=[pltpu.HBM(a.shape, a.dtype) for a in sums] * 2,
        input_output_aliases={k: k for k in range(2 * n)},
        compiler_params=pltpu.CompilerParams(has_side_effects=EFFECT),
    )(*sums, *slots, send, recv, after)
    return list(res[:n]), list(res[n:])


def _pair_exchange(parts, name):
    n = len(parts)

    def body(*refs):
        ins, outs = refs[:n], refs[n:2 * n]
        send, recv = refs[2 * n:]
        x, y, c, _ = _place()
        copies = []
        for k in range(n):
            rh = parts[k].shape[1] // 2
            cp = pltpu.make_async_remote_copy(
                src_ref=ins[k].at[:, pl.ds((1 - c) * rh, rh), :], dst_ref=outs[k],
                send_sem=send.at[k], recv_sem=recv.at[k], device_id=(x, y, 1 - c), device_id_type=MESH)
            cp.start()
            copies.append(cp)
        for cp in copies:
            cp.wait()

    return pl.pallas_call(
        body,
        name=name,
        in_specs=[ANY] * n,
        out_specs=[ANY] * n,
        out_shape=[jax.ShapeDtypeStruct((N_CHIPS, a.shape[1] // 2, a.shape[2]), a.dtype) for a in parts],
        scratch_shapes=[pltpu.SemaphoreType.DMA((n,)), pltpu.SemaphoreType.DMA((n,))],
        compiler_params=pltpu.CompilerParams(has_side_effects=True),
    )(*parts)


def _pair_sum(core, part, landed, name):
    _, r, cdim = part.shape
    rh = r // 2
    tr = _first_divisor(rh, (256, 128, 64, 32, 16))
    nb = rh // tr

    def body(c_ref, a_ref, b_ref, o_ref):
        o_ref[...] = (a_ref[...].astype(F32) + b_ref[...].astype(F32)).astype(o_ref.dtype)

    return pl.pallas_call(
        body,
        name=name,
        grid_spec=pltpu.PrefetchScalarGridSpec(
            num_scalar_prefetch=1,
            grid=(N_CHIPS, nb),
            in_specs=[pl.BlockSpec((None, tr, cdim), lambda q, i, c_ref: (q, c_ref[0] * nb + i, 0)),
                      pl.BlockSpec((None, tr, cdim), lambda q, i, c_ref: (q, i, 0))],
            out_specs=pl.BlockSpec((None, tr, cdim), lambda q, i, c_ref: (q, i, 0))),
        out_shape=jax.ShapeDtypeStruct((N_CHIPS, rh, cdim), BF16),
        compiler_params=_params(("parallel", "parallel")),
    )(core, part, landed)


def _chip_sum(place, own, slots, name):
    _, rh, cdim = slots.shape
    tr = _first_divisor(rh, (256, 128, 64, 32, 16))
    nb = rh // tr

    def body(place_ref, own_ref, s1_ref, s2_ref, s3_ref, o_ref):
        acc = own_ref[...].astype(F32)
        for s_ref in (s1_ref, s2_ref, s3_ref):
            acc = acc + s_ref[...].astype(F32)
        o_ref[...] = acc

    def other(j):
        return lambda i, place_ref: ((place_ref[0] + j) % N_CHIPS, i, 0)

    return pl.pallas_call(
        body,
        name=name,
        grid_spec=pltpu.PrefetchScalarGridSpec(
            num_scalar_prefetch=1,
            grid=(nb,),
            in_specs=[pl.BlockSpec((None, tr, cdim), other(0))] + [pl.BlockSpec((None, tr, cdim), other(j)) for j in (1, 2, 3)],
            out_specs=pl.BlockSpec((tr, cdim), lambda i, place_ref: (place_ref[1] * nb + i, 0))),
        out_shape=jax.ShapeDtypeStruct((2 * rh, cdim), F32),
        compiler_params=_params(("parallel",)),
    )(place, own, slots, slots, slots)


def _half_exchange(bufs, name):
    n = len(bufs)

    def body(*refs):
        outs = refs[n:2 * n]
        send, recv = refs[2 * n:]
        x, y, c, _ = _place()
        copies = []
        for k in range(n):
            rh = bufs[k].shape[0] // 2
            mine = outs[k].at[pl.ds(c * rh, rh), :]
            cp = pltpu.make_async_remote_copy(src_ref=mine, dst_ref=mine, send_sem=send.at[k], recv_sem=recv.at[k],
                                              device_id=(x, y, 1 - c), device_id_type=MESH)
            cp.start()
            copies.append(cp)
        for k in range(n):
            rh = bufs[k].shape[0] // 2
            theirs = outs[k].at[pl.ds((1 - c) * rh, rh), :]
            copies[k].wait_send()
            pltpu.make_async_remote_copy(src_ref=theirs, dst_ref=theirs, send_sem=send.at[k], recv_sem=recv.at[k],
                                         device_id=(x, y, 1 - c), device_id_type=MESH).wait_recv()

    return pl.pallas_call(
        body,
        name=name,
        in_specs=[ANY] * n,
        out_specs=[ANY] * n,
        out_shape=[jax.ShapeDtypeStruct(a.shape, a.dtype) for a in bufs],
        input_output_aliases={k: k for k in range(n)},
        scratch_shapes=[pltpu.SemaphoreType.DMA((n,)), pltpu.SemaphoreType.DMA((n,))],
        compiler_params=pltpu.CompilerParams(has_side_effects=True),
    )(*bufs)


SMALL_ROWS = 32


def _all_reduce_small(pack):
    def body(in_ref, out_ref, gath, send, recv):
        x, y, c, _ = _place()
        me = 4 * x + 2 * y + c
        gath[me] = in_ref[...]
        copies = []
        for k in range(1, N_DEV):
            dx, dy, dc = (k >> 2) & 1, (k >> 1) & 1, k & 1
            peer = (x ^ dx, y ^ dy, c ^ dc)
            cp = pltpu.make_async_remote_copy(src_ref=in_ref, dst_ref=gath.at[me], send_sem=send.at[k], recv_sem=recv.at[k],
                                              device_id=peer, device_id_type=MESH)
            cp.start()
            copies.append((cp, 4 * peer[0] + 2 * peer[1] + peer[2]))
        for k, (cp, peer_id) in enumerate(copies, start=1):
            cp.wait_send()
            pltpu.make_async_remote_copy(src_ref=in_ref, dst_ref=gath.at[peer_id], send_sem=send.at[k], recv_sem=recv.at[k],
                                         device_id=(x, y, c), device_id_type=MESH).wait_recv()
        acc = gath[0]
        for dev in range(1, N_DEV):
            acc = acc + gath[dev]
        out_ref[...] = acc

    return pl.pallas_call(
        body,
        name="all_reduce_small",
        in_specs=[pl.BlockSpec(memory_space=pltpu.VMEM)],
        out_specs=pl.BlockSpec(memory_space=pltpu.VMEM),
        out_shape=jax.ShapeDtypeStruct(pack.shape, F32),
        scratch_shapes=[pltpu.VMEM((N_DEV,) + pack.shape, F32), pltpu.SemaphoreType.DMA((N_DEV,)), pltpu.SemaphoreType.DMA((N_DEV,))],
        compiler_params=pltpu.CompilerParams(has_side_effects=True, vmem_limit_bytes=VMEM_LIMIT),
    )(pack)


def _adamw_update(g_ref, w_ref, m_ref, v_ref, go_ref, d_ref, mo_ref, vo_ref):
    bc1 = 1.0 - ADAM_B1 ** ADAM_STEP
    bc2 = 1.0 - ADAM_B2 ** ADAM_STEP
    gv = g_ref[...]
    mn = ADAM_B1 * m_ref[...] + (1.0 - ADAM_B1) * gv
    vn = ADAM_B2 * v_ref[...] + (1.0 - ADAM_B2) * (gv * gv)
    go_ref[...] = gv
    mo_ref[...] = mn
    vo_ref[...] = vn
    d_ref[...] = -ADAM_LR * ((mn / bc1) / (jnp.sqrt(vn / bc2) + ADAM_EPS) + ADAM_WD * w_ref[...])


def _adamw_small(gs, ws, ms, vs):
    n = len(gs)

    def body(*refs):
        ins, outs = refs[:4 * n], refs[4 * n:]
        for k in range(n):
            _adamw_update(*[ins[i * n + k] for i in range(4)], *outs[4 * k:4 * k + 4])

    vmem = pl.BlockSpec(memory_space=pltpu.VMEM)
    res = pl.pallas_call(
        body,
        name="adamw_small",
        in_specs=[vmem] * (4 * n),
        out_specs=[vmem] * (4 * n),
        out_shape=[jax.ShapeDtypeStruct(w.shape, F32) for w in ws for _ in range(4)],
        compiler_params=pltpu.CompilerParams(vmem_limit_bytes=VMEM_LIMIT),
    )(*gs, *ws, *ms, *vs)
    return [tuple(res[4 * k:4 * k + 4]) for k in range(n)]


def _adamw(g, w, m, v, name, g_block=0):
    r, cdim = w.shape
    tr = r if r * cdim * 4 <= (1 << 20) else _first_divisor(r, (128, 64, 32, 16, 8))

    def body(*refs):
        _adamw_update(*refs)

    spec = pl.BlockSpec((tr, cdim), lambda i: (i, 0))
    return pl.pallas_call(
        body,
        name=name,
        grid=(r // tr,),
        in_specs=[pl.BlockSpec((tr, cdim), lambda i: (i, g_block))] + [spec] * 3,
        out_specs=[spec] * 4,
        out_shape=[jax.ShapeDtypeStruct((r, cdim), F32)] * 4,
        compiler_params=_params(("parallel",)),
    )(g, w, m, v)


VEC_NAMES = ["pre_mix_norm", "q_norm", "kv_norm", "conv_b", "conv_ln_g", "conv_ln_b", "conv_out_norm",
             "attn_out_norm", "post_mix_norm", "pre_ffn_norm", "post_ffn_norm"]
LOSS_ROW = len(VEC_NAMES)
CONV_W_ROW = 16


def _cols_to_full(parts):
    _, r, cdim = parts.shape
    return parts.transpose(1, 0, 2).reshape(r, N_CHIPS * cdim)


def _full_to_cols(full):
    r, n = full.shape
    return full.reshape(r, N_CHIPS, n // N_CHIPS).transpose(1, 0, 2)


def _assemble_w_in(parts):
    w_in = _cols_to_full(parts)
    d = w_in.shape[0]
    w_a = w_in[:, :CONV_CH].reshape(d, CONV_CH // LANES, 1, LANES)
    w_g = w_in[:, CONV_CH:2 * CONV_CH].reshape(d, CONV_CH // LANES, 1, LANES)
    w_ag = jnp.concatenate([w_a, w_g], axis=2).reshape(d, 2 * CONV_CH)
    w_z2 = jnp.concatenate([w_in[:, 2 * CONV_CH:], jnp.zeros((d, LANES - QK_ROPE), w_in.dtype)], axis=1)
    return dict(w_ag=w_ag, w_z2=w_z2)


def _assemble_mixer_rest(g):
    uq = _cols_to_full(g["w_uq"]).reshape(Q_LORA, N_HEADS, QK_HEAD)
    w_uq = jnp.concatenate([uq[:, :, :QK_NOPE].reshape(Q_LORA, N_HEADS * QK_NOPE),
                            uq[:, :, QK_NOPE:].reshape(Q_LORA, N_HEADS * QK_ROPE)], axis=1)
    return dict(w_uq=w_uq, w_ukv=_cols_to_full(g["w_ukv"]), conv_w=_cols_to_full(g["conv_w"]),
                w_out=g["w_out"].reshape(-1, g["w_out"].shape[2]))


def _grads_to_parts(dw):
    d = dw["w_ag"].shape[0]
    ag = dw["w_ag"].reshape(d, CONV_CH // LANES, 2, LANES)
    d_in = jnp.concatenate([ag[:, :, 0, :].reshape(d, CONV_CH), ag[:, :, 1, :].reshape(d, CONV_CH),
                            dw["w_z2"][:, :Q_LORA + KV_LORA + QK_ROPE]], axis=1)
    uq = dw["w_uq"]
    d_uq = jnp.concatenate([uq[:, :N_HEADS * QK_NOPE].reshape(Q_LORA, N_HEADS, QK_NOPE),
                            uq[:, N_HEADS * QK_NOPE:].reshape(Q_LORA, N_HEADS, QK_ROPE)], axis=2).reshape(Q_LORA, N_HEADS * QK_HEAD)
    return dict(w_in=_full_to_cols(d_in), w_uq=_full_to_cols(d_uq), w_ukv=_full_to_cols(dw["w_ukv"]),
                w_out=dw["w_out"].reshape(N_CHIPS, -1, dw["w_out"].shape[1]))


MIXER = ["w_in", "w_uq", "w_ukv", "w_out"]
FFN = ["w_gu", "w_down"]
BIG = MIXER + FFN


def _pad_lanes(v, n):
    return jnp.pad(v, ((0, 0), (0, n - v.shape[1])))


def kernel(x, positions, pre_mix_norm, w_in, q_norm, w_uq, kv_norm, w_ukv, conv_w, conv_b, conv_ln_g, conv_ln_b, conv_out_norm, attn_out_norm, w_out, post_mix_norm, pre_ffn_norm, w_gate, w_up, w_down, post_ffn_norm, loss_target, m_pre_mix_norm, m_w_in, m_q_norm, m_w_uq, m_kv_norm, m_w_ukv, m_conv_w, m_conv_b, m_conv_ln_g, m_conv_ln_b, m_conv_out_norm, m_attn_out_norm, m_w_out, m_post_mix_norm, m_pre_ffn_norm, m_w_gate, m_w_up, m_w_down, m_post_ffn_norm, v_pre_mix_norm, v_w_in, v_q_norm, v_w_uq, v_kv_norm, v_w_ukv, v_conv_w, v_conv_b, v_conv_ln_g, v_conv_ln_b, v_conv_out_norm, v_attn_out_norm, v_w_out, v_post_mix_norm, v_pre_ffn_norm, v_w_gate, v_w_up, v_w_down, v_post_ffn_norm):
    given = dict(locals())
    names = ["pre_mix_norm", "w_in", "q_norm", "w_uq", "kv_norm", "w_ukv", "conv_w", "conv_b", "conv_ln_g", "conv_ln_b",
             "conv_out_norm", "attn_out_norm", "w_out", "post_mix_norm", "pre_ffn_norm", "w_gate", "w_up", "w_down", "post_ffn_norm"]
    def as_2d(a):
        return a if a.ndim == 2 else a[0]

    weights = {n: as_2d(given[n]) for n in names}
    mom = {n: as_2d(given["m_" + n]) for n in names}
    var = {n: as_2d(given["v_" + n]) for n in names}
    d = D_MODEL

    inv_freq = ROPE_THETA ** (-jnp.arange(0, QK_ROPE, 2, dtype=F32) / QK_ROPE)
    ang = positions[0].astype(F32)[:, None] * inv_freq
    cos, sin = jnp.cos(ang), jnp.sin(ang)
    cos2 = jnp.concatenate([cos, cos, cos, cos], axis=1)
    sin2 = jnp.concatenate([-sin, sin, -sin, sin], axis=1)

    chip = 2 * lax.axis_index("x") + lax.axis_index("y")
    core = lax.axis_index("c")
    chip1 = chip.astype(jnp.int32).reshape(1)
    pieces = {n: [weights[n]] for n in BIG if n != "w_gu"}
    pieces["w_gu"] = [weights["w_gate"], weights["w_up"]]
    core1 = core.astype(jnp.int32).reshape(1)
    place = jnp.stack([chip, core]).astype(jnp.int32)
    rest = ["w_uq", "w_ukv", "w_out"]
    mix_bufs = [_to_parts(chip1, pieces[n], BF16, "to_parts_" + n) for n in ["w_in"] + rest]
    mix_bufs.append(_to_parts(chip1, [jnp.pad(weights["conv_w"], ((0, CONV_K_PAD - CONV_K), (0, 0)))], F32, "to_parts_conv_w"))
    mix_whole = [False] * 4 + [True]
    mix_sems, mix_thru = _gather_start(mix_bufs, mix_whole, [[0], [1, 2, 3, 4]], "gather_start_mixer")
    ffn_bufs = [_to_parts(chip1, pieces[n], BF16, "to_parts_" + n) for n in FFN]
    ffn_sems, ffn_thru = _gather_start(ffn_bufs, [False, False], [[0], [1]], "gather_start_ffn")
    got = _gather_wait(mix_thru[:1], [False], *mix_sems[0], cos2, "gather_wait_w_in")
    full = _assemble_w_in(_gather_hand_on(got, "gather_hand_on_w_in")[0])
    vec = {n: weights[n] for n in VEC_NAMES}
    rs = {}

    def get_mixer_rest(after):
        got = _gather_wait(mix_thru[1:], mix_whole[1:], *mix_sems[1], after, "gather_wait_mixer_rest")
        got = list(_gather_hand_on(got[:3], "gather_hand_on_mixer_rest")) + [got[3]]
        return _assemble_mixer_rest(dict(zip(rest + ["conv_w"], got)))

    def get_w_gu(after):
        got = _gather_wait(ffn_thru[:1], [False], *ffn_sems[0], after, "gather_wait_w_gu")
        return _gather_hand_on(got, "gather_hand_on_w_gu")[0]

    def get_w_down(after):
        got = _gather_wait(ffn_thru[1:], [False], *ffn_sems[1], after, "gather_wait_w_down")
        got = _gather_hand_on(got, "gather_hand_on_w_down")[0]
        return got.reshape(-1, got.shape[2])

    def reduce_start(group, plist):
        landed = _pair_exchange(plist, "grad_pair_exchange_" + group)
        sums = [_pair_sum(core1, a, b, "grad_pair_sum_%s_%d" % (group, k)) for k, (a, b) in enumerate(zip(plist, landed))]
        rs[group] = _chip_exchange_start(sums, "grad_chip_exchange_start_" + group)
        return rs[group][2][0]

    def reduce_finish(group, after):
        send, recv, sums, slots = rs[group]
        sums, slots = _chip_exchange_wait(sums, slots, send, recv, after, "grad_chip_exchange_wait_" + group)
        halves = [_chip_sum(place, s, sl, "grad_chip_sum_%s_%d" % (group, k)) for k, (s, sl) in enumerate(zip(sums, slots))]
        return list(_half_exchange(halves, "grad_half_exchange_" + group))

    def ffn_grads(dw_gu, dw_down):
        return reduce_start("ffn", [dw_gu, dw_down.reshape(N_CHIPS, -1, dw_down.shape[1])])

    loss, grad_x, dw, dvec = _local_step(x[0], loss_target[0], cos2, sin2, vec, full,
                                         dict(mixer_rest=get_mixer_rest, w_gu=get_w_gu, w_down=get_w_down, grads=ffn_grads))

    rows = [_pad_lanes(dvec[n], d) for n in VEC_NAMES] + [_pad_lanes(loss, d)]
    rows.append(jnp.zeros((CONV_W_ROW - len(rows), d), F32))
    rows.append(dw["conv_w"].reshape(SMALL_ROWS - CONV_W_ROW, d))
    small = _all_reduce_small(jnp.concatenate(rows, axis=0))
    g_conv_w_full = small[CONV_W_ROW:].reshape(CONV_K_PAD, CONV_CH)
    g_small = {n: small[i:i + 1, :weights[n].shape[1]] for i, n in enumerate(VEC_NAMES)}
    g_small["conv_w"] = lax.dynamic_slice(g_conv_w_full, (0, chip * (CONV_CH // N_CHIPS)), (CONV_K_PAD, CONV_CH // N_CHIPS))[:CONV_K]
    loss_out = small[LOSS_ROW, 0]

    parts = _grads_to_parts(dw)
    started = reduce_start("mixer", [parts[n] for n in MIXER])
    g_gu, g_down = reduce_finish("ffn", started)
    res = {}
    for n, g, blk in (("w_gate", g_gu, 0), ("w_up", g_gu, 1), ("w_down", g_down, 0)):
        res[n] = _adamw(g, weights[n], mom[n], var[n], "adamw_" + n, g_block=blk)
    small_names = VEC_NAMES + ["conv_w"]
    res.update(zip(small_names, _adamw_small([g_small[n] for n in small_names], [weights[n] for n in small_names],
                                             [mom[n] for n in small_names], [var[n] for n in small_names])))
    for n, g in zip(MIXER, reduce_finish("mixer", res["w_down"][1])):
        res[n] = _adamw(g, weights[n], mom[n], var[n], "adamw_" + n)
    outs = [loss_out, grad_x[None]]
    for i in range(4):
        outs += [res[n][i].reshape(given[n].shape) for n in names]
    return tuple(outs)
```

```python
import functools

import jax
import jax.numpy as jnp
from jax import lax
from jax.experimental import pallas as pl
from jax.experimental.pallas import tpu as pltpu

F32 = jnp.float32
BF16 = jnp.bfloat16

D_MODEL = 2048
CONV_CH = 1024
CONV_K = 31
CONV_K_PAD = 32
N_HEADS = 8
QK_NOPE = 128
QK_ROPE = 64
V_HEAD = 128
QK_HEAD = QK_NOPE + QK_ROPE
Q_LORA = 768
KV_LORA = 512
ATTN_CH = N_HEADS * V_HEAD
Z2_COLS = Q_LORA + KV_LORA + 128
D_FF = 5632
ROPE_THETA = 10000.0
EPS = 1e-6
LANES = 128
N_CHIPS = 4
N_DEV = 8

ADAM_LR = 0.001
ADAM_B1 = 0.9
ADAM_B2 = 0.999
ADAM_EPS = 1e-08
ADAM_WD = 0.01
ADAM_STEP = 10

VMEM_LIMIT = 56 * 1024 * 1024
ROW_TILE = 256
MESH = pl.DeviceIdType.MESH


def _params(sem=None):
    return pltpu.CompilerParams(dimension_semantics=sem, vmem_limit_bytes=VMEM_LIMIT)


def _first_divisor(n, cands):
    for c in cands:
        if n % c == 0:
            return c
    return n


def _matmul(pairs, mode, out_dtype, name, b_parts=None, out_parts=False, tiles=(None, None, None), after=None):
    a0, b0 = pairs[0]
    part_c = b0.shape[2] if b_parts else None
    if mode == "nn":
        m, n = a0.shape[0], (N_CHIPS * part_c if b_parts else b0.shape[1])
        ks = [a.shape[1] for a, _ in pairs]
    elif mode == "nt":
        m, n = a0.shape[0], b0.shape[-2]
        ks = [a.shape[1] for a, _ in pairs]
    else:
        m, n = a0.shape[1], b0.shape[1]
        ks = [a.shape[0] for a, _ in pairs]
    tm = tiles[0] or _first_divisor(m, (1024, 768, 512, 256))
    tn = tiles[1] or (n if n <= 1536 else _first_divisor(n, (1024, 512, 256, 128)))
    tks = [tiles[2] or (512 if k % 512 == 0 else k) for k in ks]
    nks = [k // tk for k, tk in zip(ks, tks)]
    offs = [sum(nks[:p]) for p in range(len(pairs))]
    nk = sum(nks)
    n_pairs = len(pairs)
    assert not (b_parts or out_parts) or n_pairs == 1

    def kk(k, p):
        return jnp.clip(k - offs[p], 0, nks[p] - 1)

    in_specs = []
    for p in range(n_pairs):
        tk = tks[p]
        if mode == "nn":
            in_specs.append(pl.BlockSpec((tm, tk), lambda i, j, k, p=p: (i, kk(k, p))))
            if b_parts == "n":
                per = part_c // tn
                in_specs.append(pl.BlockSpec((None, tk, tn), lambda i, j, k: (j // per, k, j % per)))
            else:
                in_specs.append(pl.BlockSpec((tk, tn), lambda i, j, k, p=p: (kk(k, p), j)))
        elif mode == "nt":
            in_specs.append(pl.BlockSpec((tm, tk), lambda i, j, k, p=p: (i, kk(k, p))))
            if b_parts == "k":
                per = part_c // tk
                in_specs.append(pl.BlockSpec((None, tn, tk), lambda i, j, k: (k // per, j, k % per)))
            else:
                in_specs.append(pl.BlockSpec((tn, tk), lambda i, j, k, p=p: (j, kk(k, p))))
        else:
            in_specs.append(pl.BlockSpec((tk, tm), lambda i, j, k, p=p: (kk(k, p), i)))
            in_specs.append(pl.BlockSpec((tk, tn), lambda i, j, k, p=p: (kk(k, p), j)))
    if out_parts:
        out_per = (n // N_CHIPS) // tn
        out_spec = pl.BlockSpec((None, tm, tn), lambda i, j, k: (j // out_per, i, j % out_per))
        out_shape = jax.ShapeDtypeStruct((N_CHIPS, m, n // N_CHIPS), out_dtype)
    else:
        out_spec = pl.BlockSpec((tm, tn), lambda i, j, k: (i, j))
        out_shape = jax.ShapeDtypeStruct((m, n), out_dtype)
    dims = {"nn": (((1,), (0,)), ((), ())), "nt": (((1,), (1,)), ((), ())), "tn": (((0,), (0,)), ((), ()))}[mode]

    n_after = 0 if after is None else 1

    def body(*refs):
        o_ref, acc = refs[2 * n_pairs + n_after], refs[2 * n_pairs + n_after + 1]
        k = pl.program_id(2)

        @pl.when(k == 0)
        def _():
            acc[...] = jnp.zeros_like(acc)

        for p in range(n_pairs):
            a_ref, b_ref = refs[2 * p], refs[2 * p + 1]

            def step(a_ref=a_ref, b_ref=b_ref):
                acc[...] += lax.dot_general(a_ref[...], b_ref[...], dims, preferred_element_type=F32)

            if n_pairs == 1:
                step()
            else:
                pl.when((k >= offs[p]) & (k < offs[p] + nks[p]))(step)

        @pl.when(k == nk - 1)
        def _():
            o_ref[...] = acc[...].astype(o_ref.dtype)

    flat = [t for pr in pairs for t in pr] + ([] if after is None else [after])
    return pl.pallas_call(
        body,
        name=name,
        grid=(m // tm, n // tn, nk),
        in_specs=in_specs + [pl.BlockSpec(memory_space=pl.ANY)] * n_after,
        out_specs=out_spec,
        out_shape=out_shape,
        scratch_shapes=[pltpu.VMEM((tm, tn), F32)],
        compiler_params=_params(("parallel", "parallel", "arbitrary")),
    )(*flat)


F4 = D_FF // N_CHIPS
FFN_TM = 1024
FFN_TK = 512


def _ffn_up(hf, w_gu):
    t, d = hf.shape
    nk = d // FFN_TK

    def body(a_ref, b_ref, gu_ref, act_ref, acc):
        k = pl.program_id(2)

        @pl.when(k == 0)
        def _():
            acc[...] = jnp.zeros_like(acc)

        acc[...] += jnp.dot(a_ref[...], b_ref[...], preferred_element_type=F32)

        @pl.when(k == nk - 1)
        def _():
            g = acc[:, :F4]
            gu_ref[...] = acc[...].astype(gu_ref.dtype)
            act_ref[...] = (g * _sigmoid(g) * acc[:, F4:]).astype(act_ref.dtype)

    return pl.pallas_call(
        body,
        name="ffn_up",
        grid=(t // FFN_TM, N_CHIPS, nk),
        in_specs=[pl.BlockSpec((FFN_TM, FFN_TK), lambda i, q, k: (i, k)),
                  pl.BlockSpec((None, FFN_TK, 2 * F4), lambda i, q, k: (q, k, 0))],
        out_specs=[pl.BlockSpec((FFN_TM, 2 * F4), lambda i, q, k: (i, q)),
                   pl.BlockSpec((FFN_TM, F4), lambda i, q, k: (i, q))],
        out_shape=[jax.ShapeDtypeStruct((t, 2 * D_FF), BF16), jax.ShapeDtypeStruct((t, D_FF), BF16)],
        scratch_shapes=[pltpu.VMEM((FFN_TM, 2 * F4), F32)],
        compiler_params=_params(("parallel", "parallel", "arbitrary")),
    )(hf, w_gu)


def _ffn_down_dx(d_ff, w_down, gu):
    t, d = d_ff.shape
    nk = d // FFN_TK

    def body(a_ref, b_ref, gu_ref, o_ref, acc):
        k = pl.program_id(2)

        @pl.when(k == 0)
        def _():
            acc[...] = jnp.zeros_like(acc)

        acc[...] += lax.dot_general(a_ref[...], b_ref[...], NT_DIMS, preferred_element_type=F32)

        @pl.when(k == nk - 1)
        def _():
            g = gu_ref[:, :F4].astype(F32)
            u = gu_ref[:, F4:].astype(F32)
            sg = _sigmoid(g)
            d_act = acc[...]
            o_ref[:, :F4] = (d_act * u * sg * (1.0 + g * (1.0 - sg))).astype(o_ref.dtype)
            o_ref[:, F4:] = (d_act * g * sg).astype(o_ref.dtype)

    return pl.pallas_call(
        body,
        name="ffn_down_dx",
        grid=(t // FFN_TM, N_CHIPS, nk),
        in_specs=[pl.BlockSpec((FFN_TM, FFN_TK), lambda i, q, k: (i, k)),
                  pl.BlockSpec((F4, FFN_TK), lambda i, q, k: (q, k)),
                  pl.BlockSpec((FFN_TM, 2 * F4), lambda i, q, k: (i, q))],
        out_specs=pl.BlockSpec((FFN_TM, 2 * F4), lambda i, q, k: (i, q)),
        out_shape=jax.ShapeDtypeStruct((t, 2 * D_FF), BF16),
        scratch_shapes=[pltpu.VMEM((FFN_TM, F4), F32)],
        compiler_params=_params(("parallel", "parallel", "arbitrary")),
    )(d_ff, w_down, gu)


def _rowwise(fn, row_ins, vec_ins, row_outs, acc_outs, name, after=None):
    t = row_ins[0].shape[0]
    tm = ROW_TILE
    n_in = len(row_ins) + len(vec_ins)
    n_row = len(row_outs)
    extra = [] if after is None else [after]

    def body(*refs):
        ins = [r[...] for r in refs[:n_in]]
        outs = refs[n_in + len(extra):]
        vals = fn(*ins)
        for r, v in zip(outs[:n_row], vals[:n_row]):
            r[...] = v.astype(r.dtype)
        if acc_outs:
            @pl.when(pl.program_id(0) == 0)
            def _():
                for r in outs[n_row:]:
                    r[...] = jnp.zeros_like(r)

            for r, v in zip(outs[n_row:], vals[n_row:]):
                r[...] += v

    in_specs = [pl.BlockSpec((tm, a.shape[1]), lambda i: (i, 0)) for a in row_ins]
    in_specs += [pl.BlockSpec(a.shape, lambda i: (0, 0)) for a in vec_ins]
    out_specs = [pl.BlockSpec((tm, c), lambda i: (i, 0)) for c, _ in row_outs]
    out_specs += [pl.BlockSpec((1, c), lambda i: (0, 0)) for c in acc_outs]
    out_shape = [jax.ShapeDtypeStruct((t, c), dt) for c, dt in row_outs]
    out_shape += [jax.ShapeDtypeStruct((1, c), F32) for c in acc_outs]
    return pl.pallas_call(
        body,
        name=name,
        grid=(t // tm,),
        in_specs=in_specs + [pl.BlockSpec(memory_space=pl.ANY)] * len(extra),
        out_specs=out_specs,
        out_shape=out_shape,
        compiler_params=_params(("arbitrary",)),
    )(*row_ins, *vec_ins, *extra)


def _mean(v):
    return jnp.mean(v, axis=-1, keepdims=True)


def _colsum(v):
    return jnp.sum(v, axis=0, keepdims=True)


def _rms_fwd(v, g):
    r = lax.rsqrt(_mean(v * v) + EPS)
    vhat = v * r
    return vhat * g, vhat, r


def _rms_bwd(dn, vhat, r, g):
    dng = dn * g
    return r * (dng - vhat * _mean(dng * vhat)), _colsum(dn * vhat)


def _swap_rope_halves(v):
    n = v.shape[-1]
    lane = lax.broadcasted_iota(jnp.int32, v.shape, v.ndim - 1)
    return jnp.where(lane % QK_ROPE < QK_ROPE // 2, pltpu.roll(v, n - QK_ROPE // 2, v.ndim - 1),
                     pltpu.roll(v, QK_ROPE // 2, v.ndim - 1))


def _rope(v, cos2, sin2):
    return v * cos2 + _swap_rope_halves(v) * sin2


def _rope_transposed(dv, cos2, sin2):
    return dv * cos2 + _swap_rope_halves(dv * sin2)


def _sigmoid(v):
    return 1.0 / (1.0 + jnp.exp(-v))


CONV_ROWS = 256


def _conv_fwd(ag, conv_w, conv_b):
    t = ag.shape[0]
    cb = LANES

    def body(ag_ref, w_ref, b_ref, o_ref, scr):
        a = ag_ref[:, :cb].astype(F32)
        g = ag_ref[:, cb:].astype(F32)
        scr[pl.ds(0, CONV_K_PAD), :] = jnp.zeros((CONV_K_PAD, cb), F32)
        scr[pl.ds(CONV_K_PAD, t), :] = a * _sigmoid(g)
        for r0 in range(0, t, CONV_ROWS):
            acc = jnp.zeros((CONV_ROWS, cb), F32) + b_ref[...]
            for k in range(CONV_K):
                acc = acc + w_ref[k:k + 1, :] * scr[pl.ds(r0 + CONV_K_PAD - (CONV_K - 1) + k, CONV_ROWS), :]
            o_ref[pl.ds(r0, CONV_ROWS), :] = acc

    return pl.pallas_call(
        body,
        name="conv_fwd",
        grid=(CONV_CH // cb,),
        in_specs=[pl.BlockSpec((t, 2 * cb), lambda j: (0, j)),
                  pl.BlockSpec((CONV_K_PAD, cb), lambda j: (0, j)),
                  pl.BlockSpec((1, cb), lambda j: (0, j))],
        out_specs=pl.BlockSpec((t, cb), lambda j: (0, j)),
        out_shape=jax.ShapeDtypeStruct((t, CONV_CH), F32),
        scratch_shapes=[pltpu.VMEM((t + CONV_K_PAD, cb), F32)],
        compiler_params=_params(("parallel",)),
    )(ag, conv_w, conv_b)


def _conv_bwd(d_u1, ag, conv_w):
    t = ag.shape[0]
    cb = LANES

    def body(du_ref, ag_ref, w_ref, dag_ref, dw_ref, db_ref, su, sd):
        a = ag_ref[:, :cb].astype(F32)
        g = ag_ref[:, cb:].astype(F32)
        sg = _sigmoid(g)
        su[pl.ds(0, CONV_K_PAD), :] = jnp.zeros((CONV_K_PAD, cb), F32)
        su[pl.ds(CONV_K_PAD, t), :] = a * sg
        sd[pl.ds(0, t), :] = du_ref[...]
        sd[pl.ds(t, CONV_K_PAD), :] = jnp.zeros((CONV_K_PAD, cb), F32)
        db_ref[...] = _colsum(du_ref[...])
        dw_ref[...] = jnp.zeros_like(dw_ref)
        for r0 in range(0, t, CONV_ROWS):
            du = sd[pl.ds(r0, CONV_ROWS), :]
            acc = jnp.zeros((CONV_ROWS, cb), F32)
            for k in range(CONV_K):
                acc = acc + w_ref[k:k + 1, :] * sd[pl.ds(r0 + (CONV_K - 1) - k, CONV_ROWS), :]
                dw_ref[k:k + 1, :] += _colsum(du * su[pl.ds(r0 + CONV_K_PAD - (CONV_K - 1) + k, CONV_ROWS), :])
            sgc = sg[r0:r0 + CONV_ROWS]
            ac = a[r0:r0 + CONV_ROWS]
            dag_ref[pl.ds(r0, CONV_ROWS), :cb] = (acc * sgc).astype(dag_ref.dtype)
            dag_ref[pl.ds(r0, CONV_ROWS), cb:] = (acc * ac * sgc * (1.0 - sgc)).astype(dag_ref.dtype)

    return pl.pallas_call(
        body,
        name="conv_bwd",
        grid=(CONV_CH // cb,),
        in_specs=[pl.BlockSpec((t, cb), lambda j: (0, j)),
                  pl.BlockSpec((t, 2 * cb), lambda j: (0, j)),
                  pl.BlockSpec((CONV_K_PAD, cb), lambda j: (0, j))],
        out_specs=[pl.BlockSpec((t, 2 * cb), lambda j: (0, j)),
                   pl.BlockSpec((CONV_K_PAD, cb), lambda j: (0, j)),
                   pl.BlockSpec((1, cb), lambda j: (0, j))],
        out_shape=[jax.ShapeDtypeStruct((t, 2 * CONV_CH), BF16),
                   jax.ShapeDtypeStruct((CONV_K_PAD, CONV_CH), F32),
                   jax.ShapeDtypeStruct((1, CONV_CH), F32)],
        scratch_shapes=[pltpu.VMEM((t + CONV_K_PAD, cb), F32), pltpu.VMEM((t + CONV_K_PAD, cb), F32)],
        compiler_params=_params(("parallel",)),
    )(d_u1, ag, conv_w)


ATT_TQ = 256
NEG = float(jnp.finfo(jnp.float32).min)
SCALE = QK_HEAD ** -0.5
NT_DIMS = (((1,), (1,)), ((), ()))
TN_DIMS = (((0,), (0,)), ((), ()))


def _att_probs(qf, kf, row0):
    s = lax.dot_general(qf, kf, NT_DIMS, preferred_element_type=F32) * SCALE
    tq, t = s.shape
    qpos = row0 + lax.broadcasted_iota(jnp.int32, (tq, t), 0)
    kpos = lax.broadcasted_iota(jnp.int32, (tq, t), 1)
    s = jnp.where(kpos <= qpos, s, NEG)
    p = jnp.exp(s - jnp.max(s, axis=-1, keepdims=True))
    return p * (1.0 / jnp.sum(p, axis=-1, keepdims=True))


def _half_mask(shape, which):
    lane = lax.broadcasted_iota(jnp.int32, shape, len(shape) - 1)
    return (lane // QK_ROPE == which).astype(F32)


def _attention_fwd(q, kv, kpe2, cos2, sin2):
    t = q.shape[0]
    tq = ATT_TQ

    def body(qn_ref, qp_ref, c_ref, s_ref, kv_ref, kpe_ref, o_ref):
        row0 = pl.program_id(1) * tq
        roped = _rope(qp_ref[...].astype(F32), c_ref[...], s_ref[...])
        for e in range(2):
            qf = jnp.concatenate([qn_ref[:, e * QK_NOPE:(e + 1) * QK_NOPE],
                                  (roped * _half_mask(roped.shape, e)).astype(BF16)], axis=1)
            kf = jnp.concatenate([kv_ref[:, e * 256:e * 256 + QK_NOPE], kpe_ref[...]], axis=1)
            p = _att_probs(qf, kf, row0)
            v = kv_ref[:, e * 256 + QK_NOPE:(e + 1) * 256]
            o_ref[:, e * V_HEAD:(e + 1) * V_HEAD] = jnp.dot(p.astype(BF16), v, preferred_element_type=F32).astype(o_ref.dtype)

    return pl.pallas_call(
        body,
        name="attention_fwd",
        grid=(N_HEADS // 2, t // tq),
        in_specs=[pl.BlockSpec((tq, 2 * QK_NOPE), lambda h, i: (i, h)),
                  pl.BlockSpec((tq, LANES), lambda h, i: (i, N_HEADS + h)),
                  pl.BlockSpec((tq, LANES), lambda h, i: (i, 0)),
                  pl.BlockSpec((tq, LANES), lambda h, i: (i, 0)),
                  pl.BlockSpec((t, 512), lambda h, i: (0, h)),
                  pl.BlockSpec((t, LANES), lambda h, i: (0, 0))],
        out_specs=pl.BlockSpec((tq, 2 * V_HEAD), lambda h, i: (i, h)),
        out_shape=jax.ShapeDtypeStruct((t, ATTN_CH), BF16),
        compiler_params=_params(("parallel", "parallel")),
    )(q, q, cos2, sin2, kv, kpe2)


def _attention_bwd(q, kv, kpe2, cos2, sin2, d_attn):
    t = q.shape[0]
    tq = ATT_TQ
    n_q = t // tq

    def body(qn_ref, qp_ref, c_ref, s_ref, kv_ref, kpe_ref, do_ref, dqn_ref, dqp_ref, dkv_ref, dkpe_ref, dkv_acc):
        h, i = pl.program_id(0), pl.program_id(1)
        row0 = i * tq

        @pl.when(i == 0)
        def _():
            dkv_acc[...] = jnp.zeros_like(dkv_acc)

        @pl.when((i == 0) & (h == 0))
        def _():
            dkpe_ref[...] = jnp.zeros_like(dkpe_ref)

        roped = _rope(qp_ref[...].astype(F32), c_ref[...], s_ref[...])
        d_roped = jnp.zeros((tq, LANES), F32)
        for e in range(2):
            mask = _half_mask(roped.shape, e)
            qf = jnp.concatenate([qn_ref[:, e * QK_NOPE:(e + 1) * QK_NOPE], (roped * mask).astype(BF16)], axis=1)
            kf = jnp.concatenate([kv_ref[:, e * 256:e * 256 + QK_NOPE], kpe_ref[...]], axis=1)
            v = kv_ref[:, e * 256 + QK_NOPE:(e + 1) * 256]
            do = do_ref[:, e * V_HEAD:(e + 1) * V_HEAD]
            p = _att_probs(qf, kf, row0)
            dp = lax.dot_general(do, v, NT_DIMS, preferred_element_type=F32)
            ds = (p * (dp - jnp.sum(p * dp, axis=-1, keepdims=True)) * SCALE).astype(BF16)
            dqf = jnp.dot(ds, kf, preferred_element_type=F32)
            dkf = lax.dot_general(ds, qf, TN_DIMS, preferred_element_type=F32)
            dv = lax.dot_general(p.astype(BF16), do, TN_DIMS, preferred_element_type=F32)
            dqn_ref[:, e * QK_NOPE:(e + 1) * QK_NOPE] = dqf[:, :QK_NOPE].astype(dqn_ref.dtype)
            d_roped = d_roped + dqf[:, QK_NOPE:] * mask
            dkv_acc[:, e * 256:e * 256 + QK_NOPE] += dkf[:, :QK_NOPE]
            dkv_acc[:, e * 256 + QK_NOPE:(e + 1) * 256] += dv
            dkpe_ref[...] += dkf[:, QK_NOPE:]
        dqp_ref[...] = _rope_transposed(d_roped, c_ref[...], s_ref[...]).astype(dqp_ref.dtype)

        @pl.when(i == n_q - 1)
        def _():
            dkv_ref[...] = dkv_acc[...].astype(dkv_ref.dtype)

    return pl.pallas_call(
        body,
        name="attention_bwd",
        grid=(N_HEADS // 2, n_q),
        in_specs=[pl.BlockSpec((tq, 2 * QK_NOPE), lambda h, i: (i, h)),
                  pl.BlockSpec((tq, LANES), lambda h, i: (i, N_HEADS + h)),
                  pl.BlockSpec((tq, LANES), lambda h, i: (i, 0)),
                  pl.BlockSpec((tq, LANES), lambda h, i: (i, 0)),
                  pl.BlockSpec((t, 512), lambda h, i: (0, h)),
                  pl.BlockSpec((t, LANES), lambda h, i: (0, 0)),
                  pl.BlockSpec((tq, 2 * V_HEAD), lambda h, i: (i, h))],
        out_specs=[pl.BlockSpec((tq, 2 * QK_NOPE), lambda h, i: (i, h)),
                   pl.BlockSpec((tq, LANES), lambda h, i: (i, h)),
                   pl.BlockSpec((t, 512), lambda h, i: (0, h)),
                   pl.BlockSpec((t, LANES), lambda h, i: (0, 0))],
        out_shape=[jax.ShapeDtypeStruct((t, N_HEADS * QK_NOPE), BF16),
                   jax.ShapeDtypeStruct((t, N_HEADS * QK_ROPE), BF16),
                   jax.ShapeDtypeStruct((t, N_HEADS * 256), BF16),
                   jax.ShapeDtypeStruct((t, LANES), F32)],
        scratch_shapes=[pltpu.VMEM((t, 512), F32)],
        compiler_params=_params(("arbitrary", "arbitrary")),
    )(q, q, cos2, sin2, kv, kpe2, d_attn)


def _local_step(x, target, cos2, sin2, vec, w, ffn):
    d = D_MODEL

    (h,) = _rowwise(lambda xv, g: (_rms_fwd(xv, g)[0],), [x], [vec["pre_mix_norm"]], [(d, BF16)], [], "pre_mix_norm_fwd")
    ag = _matmul([(h, w["w_ag"])], "nn", BF16, "in_proj_ag")
    z2 = _matmul([(h, w["w_z2"])], "nn", BF16, "in_proj_z2")
    w = {**w, **ffn["mixer_rest"](z2)}
    u1 = _conv_fwd(ag, w["conv_w"], vec["conv_b"])

    def latents_fwd(z, c2, s2, qg, kvg):
        z = z.astype(F32)
        qn = _rms_fwd(z[:, :Q_LORA], qg)[0]
        kvn = _rms_fwd(z[:, Q_LORA:Q_LORA + KV_LORA], kvg)[0]
        kr = z[:, Q_LORA + KV_LORA:]
        kr2 = kr + pltpu.roll(kr, QK_ROPE, 1)
        return qn, kvn, _rope(kr2, c2, s2)

    qn, kvn, kpe2 = _rowwise(latents_fwd, [z2, cos2, sin2], [vec["q_norm"], vec["kv_norm"]],
                             [(Q_LORA, BF16), (KV_LORA, BF16), (LANES, BF16)], [], "latents_fwd")
    q = _matmul([(qn, w["w_uq"])], "nn", BF16, "q_up")
    kv = _matmul([(kvn, w["w_ukv"])], "nn", BF16, "kv_up")
    attn = _attention_fwd(q, kv, kpe2, cos2, sin2)
    landed = ffn["w_gu_landed"](attn)

    def conv_post(u, lg, lb):
        mu = _mean(u)
        uc = u - mu
        rstd = lax.rsqrt(_mean(uc * uc) + EPS)
        uhat = uc * rstd
        u2 = uhat * lg + lb
        sg = _sigmoid(u2)
        return uhat, rstd, u2, sg, u2 * sg

    def mix_in_fwd(u, at, lg, lb, cg, ag_):
        u3 = conv_post(u, lg, lb)[4]
        cn = _rms_fwd(u3, cg)[0]
        an = _rms_fwd(at.astype(F32), ag_)[0]
        return (jnp.concatenate([cn, an], axis=1),)

    (cat,) = _rowwise(mix_in_fwd, [u1, attn], [vec["conv_ln_g"], vec["conv_ln_b"], vec["conv_out_norm"], vec["attn_out_norm"]],
                      [(2 * CONV_CH, BF16)], [], "mix_in_fwd", after=landed)
    mix = _matmul([(cat, w["w_out"])], "nn", F32, "out_proj")
    landed = ffn["w_down_landed"](mix)

    def residual1(xv, mv, gpm, gpf):
        x1 = xv + _rms_fwd(mv, gpm)[0]
        return x1, _rms_fwd(x1, gpf)[0]

    x1, hf = _rowwise(residual1, [x, mix], [vec["post_mix_norm"], vec["pre_ffn_norm"]], [(d, F32), (d, BF16)], [],
                      "residual1_fwd", after=landed)
    w_gu = ffn["w_gu"](hf)
    gu, act = _ffn_up(hf, w_gu)
    w_down = ffn["w_down"](act)
    ff = _matmul([(act, w_down)], "nn", F32, "ffn_down")

    def loss_head(x1v, ffv, tg, g):
        n, fhat, r = _rms_fwd(ffv, g)
        err = x1v + n - tg
        loss = 0.5 * jnp.sum(_mean(err * err), axis=0, keepdims=True)
        dy = err * (1.0 / d)
        d_ff, dg = _rms_bwd(dy, fhat, r, g)
        return dy, d_ff, dg, jnp.broadcast_to(loss, (1, LANES))

    dy, d_ff, g_post_ffn, loss = _rowwise(loss_head, [x1, ff, target], [vec["post_ffn_norm"]],
                                          [(d, F32), (d, BF16)], [d, LANES], "loss_head")
    d_gu = _ffn_down_dx(d_ff, w_down, gu)
    dw_down = _matmul([(act, d_ff)], "tn", BF16, "ffn_down_dw", tiles=(F4, None, None))
    started = ffn["dw_down"](dw_down)
    dw_gu = _matmul([(hf, d_gu)], "tn", BF16, "ffn_gate_up_dw", out_parts=True, tiles=(None, F4, None), after=started)
    started = ffn["dw_gu"](dw_gu)
    d_hf = _matmul([(d_gu, w_gu)], "nt", F32, "ffn_gate_up_dx", b_parts="k", tiles=(None, None, F4), after=started)
    started = ffn["grads_exchanged"](d_hf)

    def residual1_bwd(dyv, dhf, x1v, mv, gpf, gpm):
        _, x1hat, r1 = _rms_fwd(x1v, gpf)
        dn, dgpf = _rms_bwd(dhf, x1hat, r1, gpf)
        d_x1 = dyv + dn
        _, mhat, rm = _rms_fwd(mv, gpm)
        d_mix, dgpm = _rms_bwd(d_x1, mhat, rm, gpm)
        return d_x1, d_mix, dgpf, dgpm

    d_x1, d_mix, g_pre_ffn, g_post_mix = _rowwise(residual1_bwd, [dy, d_hf, x1, mix], [vec["pre_ffn_norm"], vec["post_mix_norm"]],
                                                  [(d, F32), (d, BF16)], [d, d], "residual1_bwd", after=started)
    d_cat = _matmul([(d_mix, w["w_out"])], "nt", BF16, "out_proj_dx")
    dw_out = _matmul([(cat, d_mix)], "tn", BF16, "out_proj_dw")

    def mix_in_bwd(dc, u, at, lg, lb, cg, ag_):
        dc = dc.astype(F32)
        uhat, rstd, u2, sg, u3 = conv_post(u, lg, lb)
        _, u3hat, rc = _rms_fwd(u3, cg)
        d_u3, dcg = _rms_bwd(dc[:, :CONV_CH], u3hat, rc, cg)
        d_u2 = d_u3 * sg * (1.0 + u2 * (1.0 - sg))
        dgl = d_u2 * lg
        d_u1 = rstd * (dgl - _mean(dgl) - uhat * _mean(dgl * uhat))
        _, ahat, ra = _rms_fwd(at.astype(F32), ag_)
        d_at, dag = _rms_bwd(dc[:, CONV_CH:], ahat, ra, ag_)
        return d_u1, d_at, dcg, _colsum(d_u2 * uhat), _colsum(d_u2), dag

    d_u1, d_attn, g_conv_out, g_ln_g, g_ln_b, g_attn_out = _rowwise(
        mix_in_bwd, [d_cat, u1, attn], [vec["conv_ln_g"], vec["conv_ln_b"], vec["conv_out_norm"], vec["attn_out_norm"]],
        [(CONV_CH, F32), (ATTN_CH, BF16)], [CONV_CH] * 4, "mix_in_bwd")
    d_ag, d_conv_w, g_conv_b = _conv_bwd(d_u1, ag, w["conv_w"])
    d_qn_, d_qp_, d_kv, d_kpe2 = _attention_bwd(q, kv, kpe2, cos2, sin2, d_attn)
    d_q = jnp.concatenate([d_qn_, d_qp_], axis=1)
    d_qn = _matmul([(d_q, w["w_uq"])], "nt", BF16, "q_up_dx")
    dw_uq = _matmul([(qn, d_q)], "tn", BF16, "q_up_dw")
    d_kvn = _matmul([(d_kv, w["w_ukv"])], "nt", BF16, "kv_up_dx")
    dw_ukv = _matmul([(kvn, d_kv)], "tn", BF16, "kv_up_dw")

    def latents_bwd(z, dq, dk, dkp, c2, s2, qg, kvg):
        z = z.astype(F32)
        _, qhat, rq = _rms_fwd(z[:, :Q_LORA], qg)
        d_ql, dqg = _rms_bwd(dq.astype(F32), qhat, rq, qg)
        _, khat, rk = _rms_fwd(z[:, Q_LORA:Q_LORA + KV_LORA], kvg)
        d_kl, dkg = _rms_bwd(dk.astype(F32), khat, rk, kvg)
        both = dkp + pltpu.roll(dkp, QK_ROPE, 1)
        d_kr = _rope_transposed(both, c2, s2) * _half_mask(both.shape, 0)
        return jnp.concatenate([d_ql, d_kl, d_kr], axis=1), dqg, dkg

    d_z2, g_q_norm, g_kv_norm = _rowwise(latents_bwd, [z2, d_qn, d_kvn, d_kpe2, cos2, sin2], [vec["q_norm"], vec["kv_norm"]],
                                         [(Z2_COLS, BF16)], [Q_LORA, KV_LORA], "latents_bwd")
    d_h = _matmul([(d_ag, w["w_ag"]), (d_z2, w["w_z2"])], "nt", F32, "in_proj_dx")
    dw_ag = _matmul([(h, d_ag)], "tn", BF16, "in_proj_ag_dw")
    dw_z2 = _matmul([(h, d_z2)], "tn", BF16, "in_proj_z2_dw")

    def pre_mix_bwd(dx1, dh, xv, g):
        _, xhat, r = _rms_fwd(xv, g)
        dn, dg = _rms_bwd(dh, xhat, r, g)
        return dx1 + dn, dg

    grad_x, g_pre_mix = _rowwise(pre_mix_bwd, [d_x1, d_h, x], [vec["pre_mix_norm"]], [(d, F32)], [d], "pre_mix_norm_bwd")

    dw = dict(w_ag=dw_ag, w_z2=dw_z2, w_uq=dw_uq, w_ukv=dw_ukv, conv_w=d_conv_w, w_out=dw_out, w_gu=dw_gu, w_down=dw_down)
    dvec = dict(pre_mix_norm=g_pre_mix, q_norm=g_q_norm, kv_norm=g_kv_norm, conv_b=g_conv_b, conv_ln_g=g_ln_g,
                conv_ln_b=g_ln_b, conv_out_norm=g_conv_out, attn_out_norm=g_attn_out, post_mix_norm=g_post_mix,
                pre_ffn_norm=g_pre_ffn, post_ffn_norm=g_post_ffn)
    return loss, grad_x, dw, dvec


ANY = pl.BlockSpec(memory_space=pl.ANY)


def _place():
    x, y, c = lax.axis_index("x"), lax.axis_index("y"), lax.axis_index("c")
    chips = [(1 - x, y), (x, 1 - y), (1 - x, 1 - y)]
    return x, y, c, chips


def _to_parts(chip, pieces, dtype, name):
    r = pieces[0].shape[0]
    widths = [a.shape[1] for a in pieces]
    tr = r if r <= 512 else _first_divisor(r, (512, 256, 128))

    def body(p_ref, *refs):
        o_ref = refs[len(pieces)]
        off = 0
        for a_ref, wdt in zip(refs, widths):
            o_ref[:, off:off + wdt] = a_ref[...].astype(o_ref.dtype)
            off += wdt

    return pl.pallas_call(
        body,
        name=name,
        grid_spec=pltpu.PrefetchScalarGridSpec(
            num_scalar_prefetch=1,
            grid=(r // tr,),
            in_specs=[pl.BlockSpec((tr, wdt), lambda i, p_ref: (i, 0)) for wdt in widths],
            out_specs=pl.BlockSpec((None, tr, sum(widths)), lambda i, p_ref: (p_ref[0], i, 0))),
        out_shape=jax.ShapeDtypeStruct((N_CHIPS, r, sum(widths)), dtype),
        compiler_params=_params(("parallel",)),
    )(chip, *pieces)


HBM = pl.BlockSpec(memory_space=pltpu.HBM)
SEM = pl.BlockSpec(memory_space=pltpu.SEMAPHORE)
EFFECT = pltpu.SideEffectType.DATAFLOW_SIDE_EFFECTING


def _in_hbm(a):
    return pltpu.with_memory_space_constraint(a, pltpu.HBM)


def _gather_rows(buf, whole, half):
    r = buf.shape[1]
    return pl.ds(0, r) if whole else pl.ds(half * (r // 2), r // 2)


def _gather_start(bufs, whole, groups, name):
    n = len(bufs)
    n_g = len(groups)

    def body(*refs):
        sems = refs[n:n + 2 * n_g]
        outs = refs[n + 2 * n_g:2 * n + 2 * n_g]
        x, y, c, chips = _place()
        p = 2 * x + y
        for gi, group in enumerate(groups):
            for ki, k in enumerate(group):
                blk = outs[k].at[p, _gather_rows(bufs[k], whole[k], c), :]
                for j, (px, py) in enumerate(chips):
                    pltpu.make_async_remote_copy(src_ref=blk, dst_ref=blk, send_sem=sems[2 * gi].at[3 * ki + j],
                                                 recv_sem=sems[2 * gi + 1].at[3 * ki + j],
                                                 device_id=(px, py, c), device_id_type=MESH).start()

    sem_shapes = []
    for group in groups:
        sem_shapes += [pltpu.SemaphoreType.DMA((3 * len(group),))] * 2
    res = pl.pallas_call(
        body,
        name=name,
        in_specs=[HBM] * n,
        out_specs=[SEM] * (2 * n_g) + [HBM] * n,
        out_shape=sem_shapes + [pltpu.HBM(a.shape, a.dtype) for a in bufs],
        input_output_aliases={k: 2 * n_g + k for k in range(n)},
        compiler_params=pltpu.CompilerParams(has_side_effects=EFFECT),
    )(*[_in_hbm(a) for a in bufs])
    sems = [(res[2 * gi], res[2 * gi + 1]) for gi in range(n_g)]
    return sems, list(res[2 * n_g:2 * n_g + n])


def _gather_wait(bufs, whole, send, recv, after, name):
    n = len(bufs)

    def body(*refs):
        ins = refs[:n]
        send_ref, recv_ref = refs[n], refs[n + 1]
        x, y, c, chips = _place()
        p = 2 * x + y
        for ki in range(n):
            rows = _gather_rows(bufs[ki], whole[ki], c)
            for j, (px, py) in enumerate(chips):
                cp = pltpu.make_async_remote_copy(src_ref=ins[ki].at[p, rows, :], dst_ref=ins[ki].at[2 * px + py, rows, :],
                                                  send_sem=send_ref.at[3 * ki + j], recv_sem=recv_ref.at[3 * ki + j],
                                                  device_id=(px, py, c), device_id_type=MESH)
                cp.wait_send()
                cp.wait_recv()

    res = pl.pallas_call(
        body,
        name=name,
        in_specs=[HBM] * n + [SEM, SEM, ANY],
        out_specs=[HBM] * n,
        out_shape=[pltpu.HBM(a.shape, a.dtype) for a in bufs],
        input_output_aliases={k: k for k in range(n)},
        compiler_params=pltpu.CompilerParams(has_side_effects=EFFECT),
    )(*bufs, send, recv, after)
    return list(res)


def _gather_hand_on(bufs, name):
    n = len(bufs)

    def body(*refs):
        outs = refs[n:2 * n]
        send, recv = refs[2 * n:]
        x, y, c, chips = _place()

        def d2d(k, j, half):
            px, py = chips[j]
            blk = outs[k].at[2 * px + py, _gather_rows(bufs[k], False, half), :]
            return pltpu.make_async_remote_copy(src_ref=blk, dst_ref=blk, send_sem=send.at[3 * k + j], recv_sem=recv.at[3 * k + j],
                                                device_id=(x, y, 1 - c), device_id_type=MESH)

        sent = [d2d(k, j, c) for k in range(n) for j in range(3)]
        for cp in sent:
            cp.start()
        for k in range(n):
            for j in range(3):
                d2d(k, j, 1 - c).wait_recv()
        for cp in sent:
            cp.wait_send()

    return pl.pallas_call(
        body,
        name=name,
        in_specs=[ANY] * n,
        out_specs=[ANY] * n,
        out_shape=[jax.ShapeDtypeStruct(a.shape, a.dtype) for a in bufs],
        input_output_aliases={k: k for k in range(n)},
        scratch_shapes=[pltpu.SemaphoreType.DMA((3 * n,)), pltpu.SemaphoreType.DMA((3 * n,))],
        compiler_params=pltpu.CompilerParams(has_side_effects=True),
    )(*bufs)


def _hand_on_start(bufs, name):
    n = len(bufs)

    def body(*refs):
        send, recv = refs[n], refs[n + 1]
        outs = refs[n + 2:]
        x, y, c, chips = _place()
        for k in range(n):
            for j, (px, py) in enumerate(chips):
                blk = outs[k].at[2 * px + py, _gather_rows(bufs[k], False, c), :]
                pltpu.make_async_remote_copy(src_ref=blk, dst_ref=blk, send_sem=send.at[3 * k + j], recv_sem=recv.at[3 * k + j],
                                             device_id=(x, y, 1 - c), device_id_type=MESH).start()

    res = pl.pallas_call(
        body,
        name=name,
        in_specs=[HBM] * n,
        out_specs=[SEM, SEM] + [HBM] * n,
        out_shape=[pltpu.SemaphoreType.DMA((3 * n,))] * 2 + [pltpu.HBM(a.shape, a.dtype) for a in bufs],
        input_output_aliases={k: 2 + k for k in range(n)},
        compiler_params=pltpu.CompilerParams(has_side_effects=EFFECT),
    )(*[_in_hbm(a) for a in bufs])
    return res[0], res[1], list(res[2:])


def _hand_on_wait(bufs, send, recv, after, name):
    n = len(bufs)

    def body(*refs):
        ins = refs[:n]
        send_ref, recv_ref = refs[n], refs[n + 1]
        x, y, c, chips = _place()
        for k in range(n):
            for j, (px, py) in enumerate(chips):
                q = 2 * px + py
                cp = pltpu.make_async_remote_copy(src_ref=ins[k].at[q, _gather_rows(bufs[k], False, c), :],
                                                  dst_ref=ins[k].at[q, _gather_rows(bufs[k], False, 1 - c), :],
                                                  send_sem=send_ref.at[3 * k + j], recv_sem=recv_ref.at[3 * k + j],
                                                  device_id=(x, y, 1 - c), device_id_type=MESH)
                cp.wait_send()
                cp.wait_recv()

    res = pl.pallas_call(
        body,
        name=name,
        in_specs=[HBM] * n + [SEM, SEM, ANY],
        out_specs=[HBM] * n,
        out_shape=[pltpu.HBM(a.shape, a.dtype) for a in bufs],
        input_output_aliases={k: k for k in range(n)},
        compiler_params=pltpu.CompilerParams(has_side_effects=EFFECT),
    )(*bufs, send, recv, after)
    return list(res)


def _pair_exchange_start(part, name):
    rh = part.shape[1] // 2
    land_shape = (N_CHIPS, rh, part.shape[2])

    def body(part_ref, land_ref, send, recv, part_out, land_out):
        x, y, c, _ = _place()
        pltpu.make_async_remote_copy(src_ref=part_out.at[:, pl.ds((1 - c) * rh, rh), :], dst_ref=land_out,
                                     send_sem=send, recv_sem=recv, device_id=(x, y, 1 - c), device_id_type=MESH).start()

    res = pl.pallas_call(
        body,
        name=name,
        in_specs=[HBM, HBM],
        out_specs=[SEM, SEM, HBM, HBM],
        out_shape=[pltpu.SemaphoreType.DMA(()), pltpu.SemaphoreType.DMA(()), pltpu.HBM(part.shape, part.dtype),
                   pltpu.HBM(land_shape, part.dtype)],
        input_output_aliases={0: 2, 1: 3},
        compiler_params=pltpu.CompilerParams(has_side_effects=EFFECT),
    )(_in_hbm(part), _in_hbm(lax.empty(land_shape, part.dtype)))
    return res


def _pair_exchange_wait(send, recv, part, land, after, name):
    rh = part.shape[1] // 2

    def body(part_ref, land_ref, send_ref, recv_ref, after_ref, part_out, land_out):
        x, y, c, _ = _place()
        cp = pltpu.make_async_remote_copy(src_ref=part_ref.at[:, pl.ds((1 - c) * rh, rh), :], dst_ref=land_ref,
                                          send_sem=send_ref, recv_sem=recv_ref, device_id=(x, y, 1 - c), device_id_type=MESH)
        cp.wait_send()
        cp.wait_recv()

    return pl.pallas_call(
        body,
        name=name,
        in_specs=[HBM, HBM, SEM, SEM, ANY],
        out_specs=[HBM, HBM],
        out_shape=[pltpu.HBM(part.shape, part.dtype), pltpu.HBM(land.shape, land.dtype)],
        input_output_aliases={0: 0, 1: 1},
        compiler_params=pltpu.CompilerParams(has_side_effects=EFFECT),
    )(part, land, send, recv, after)


def _chip_exchange_start(sums, name):
    n = len(sums)

    def body(*refs):
        send, recv = refs[2 * n], refs[2 * n + 1]
        src = refs[2 * n + 2:3 * n + 2]
        dst = refs[3 * n + 2:4 * n + 2]
        x, y, c, chips = _place()
        p = 2 * x + y
        for k in range(n):
            for j, (px, py) in enumerate(chips):
                pltpu.make_async_remote_copy(src_ref=src[k].at[2 * px + py], dst_ref=dst[k].at[p],
                                             send_sem=send.at[3 * k + j], recv_sem=recv.at[3 * k + j],
                                             device_id=(px, py, c), device_id_type=MESH).start()

    res = pl.pallas_call(
        body,
        name=name,
        in_specs=[HBM] * (2 * n),
        out_specs=[SEM, SEM] + [HBM] * (2 * n),
        out_shape=[pltpu.SemaphoreType.DMA((3 * n,))] * 2 + [pltpu.HBM(a.shape, a.dtype) for a in sums] * 2,
        input_output_aliases={k: 2 + k for k in range(2 * n)},
        compiler_params=pltpu.CompilerParams(has_side_effects=EFFECT),
    )(*[_in_hbm(a) for a in sums], *[_in_hbm(lax.empty(a.shape, a.dtype)) for a in sums])
    return res[0], res[1], list(res[2:2 + n]), list(res[2 + n:2 + 2 * n])


def _chip_exchange_wait(sums, slots, send, recv, after, name):
    n = len(sums)

    def body(*refs):
        src, dst = refs[:n], refs[n:2 * n]
        send_ref, recv_ref = refs[2 * n], refs[2 * n + 1]
        x, y, c, chips = _place()
        for k in range(n):
            for j, (px, py) in enumerate(chips):
                cp = pltpu.make_async_remote_copy(src_ref=src[k].at[2 * px + py], dst_ref=dst[k].at[2 * px + py],
                                                  send_sem=send_ref.at[3 * k + j], recv_sem=recv_ref.at[3 * k + j],
                                                  device_id=(px, py, c), device_id_type=MESH)
                cp.wait_send()
                cp.wait_recv()

    res = pl.pallas_call(
        body,
        name=name,
        in_specs=[HBM] * (2 * n) + [SEM, SEM] + [ANY] * len(after),
        out_specs=[HBM] * (2 * n),
        out_shape=[pltpu.HBM(a.shape, a.dtype) for a in sums] * 2,
        input_output_aliases={k: k for k in range(2 * n)},
        compiler_params=pltpu.CompilerParams(has_side_effects=EFFECT),
    )(*sums, *slots, send, recv, *after)
    return list(res[:n]), list(res[n:])


def _pair_exchange(parts, name):
    n = len(parts)

    def body(*refs):
        ins, outs = refs[:n], refs[n:2 * n]
        send, recv = refs[2 * n:]
        x, y, c, _ = _place()
        copies = []
        for k in range(n):
            rh = parts[k].shape[1] // 2
            cp = pltpu.make_async_remote_copy(
                src_ref=ins[k].at[:, pl.ds((1 - c) * rh, rh), :], dst_ref=outs[k],
                send_sem=send.at[k], recv_sem=recv.at[k], device_id=(x, y, 1 - c), device_id_type=MESH)
            cp.start()
            copies.append(cp)
        for cp in copies:
            cp.wait()

    return pl.pallas_call(
        body,
        name=name,
        in_specs=[ANY] * n,
        out_specs=[ANY] * n,
        out_shape=[jax.ShapeDtypeStruct((N_CHIPS, a.shape[1] // 2, a.shape[2]), a.dtype) for a in parts],
        scratch_shapes=[pltpu.SemaphoreType.DMA((n,)), pltpu.SemaphoreType.DMA((n,))],
        compiler_params=pltpu.CompilerParams(has_side_effects=True),
    )(*parts)


def _pair_sum(core, part, landed, name):
    _, r, cdim = part.shape
    rh = r // 2
    tr = _first_divisor(rh, (256, 128, 64, 32, 16))
    nb = rh // tr

    def body(c_ref, a_ref, b_ref, o_ref):
        o_ref[...] = (a_ref[...].astype(F32) + b_ref[...].astype(F32)).astype(o_ref.dtype)

    return pl.pallas_call(
        body,
        name=name,
        grid_spec=pltpu.PrefetchScalarGridSpec(
            num_scalar_prefetch=1,
            grid=(N_CHIPS, nb),
            in_specs=[pl.BlockSpec((None, tr, cdim), lambda q, i, c_ref: (q, c_ref[0] * nb + i, 0)),
                      pl.BlockSpec((None, tr, cdim), lambda q, i, c_ref: (q, i, 0))],
            out_specs=pl.BlockSpec((None, tr, cdim), lambda q, i, c_ref: (q, i, 0))),
        out_shape=jax.ShapeDtypeStruct((N_CHIPS, rh, cdim), BF16),
        compiler_params=_params(("parallel", "parallel")),
    )(core, part, landed)


def _chip_sum(place, own, slots, name):
    _, rh, cdim = slots.shape
    tr = _first_divisor(rh, (256, 128, 64, 32, 16))
    nb = rh // tr

    def body(place_ref, own_ref, s1_ref, s2_ref, s3_ref, o_ref):
        acc = own_ref[...].astype(F32)
        for s_ref in (s1_ref, s2_ref, s3_ref):
            acc = acc + s_ref[...].astype(F32)
        o_ref[...] = acc

    def other(j):
        return lambda i, place_ref: ((place_ref[0] + j) % N_CHIPS, i, 0)

    return pl.pallas_call(
        body,
        name=name,
        grid_spec=pltpu.PrefetchScalarGridSpec(
            num_scalar_prefetch=1,
            grid=(nb,),
            in_specs=[pl.BlockSpec((None, tr, cdim), other(0))] + [pl.BlockSpec((None, tr, cdim), other(j)) for j in (1, 2, 3)],
            out_specs=pl.BlockSpec((tr, cdim), lambda i, place_ref: (place_ref[1] * nb + i, 0))),
        out_shape=jax.ShapeDtypeStruct((2 * rh, cdim), F32),
        compiler_params=_params(("parallel",)),
    )(place, own, slots, slots, slots)


def _half_exchange(bufs, name):
    n = len(bufs)

    def body(*refs):
        outs = refs[n:2 * n]
        send, recv = refs[2 * n:]
        x, y, c, _ = _place()
        copies = []
        for k in range(n):
            rh = bufs[k].shape[0] // 2
            mine = outs[k].at[pl.ds(c * rh, rh), :]
            cp = pltpu.make_async_remote_copy(src_ref=mine, dst_ref=mine, send_sem=send.at[k], recv_sem=recv.at[k],
                                              device_id=(x, y, 1 - c), device_id_type=MESH)
            cp.start()
            copies.append(cp)
        for k in range(n):
            rh = bufs[k].shape[0] // 2
            theirs = outs[k].at[pl.ds((1 - c) * rh, rh), :]
            copies[k].wait_send()
            pltpu.make_async_remote_copy(src_ref=theirs, dst_ref=theirs, send_sem=send.at[k], recv_sem=recv.at[k],
                                         device_id=(x, y, 1 - c), device_id_type=MESH).wait_recv()

    return pl.pallas_call(
        body,
        name=name,
        in_specs=[ANY] * n,
        out_specs=[ANY] * n,
        out_shape=[jax.ShapeDtypeStruct(a.shape, a.dtype) for a in bufs],
        input_output_aliases={k: k for k in range(n)},
        scratch_shapes=[pltpu.SemaphoreType.DMA((n,)), pltpu.SemaphoreType.DMA((n,))],
        compiler_params=pltpu.CompilerParams(has_side_effects=True),
    )(*bufs)


SMALL_ROWS = 32


def _all_reduce_small(pack):
    def body(in_ref, out_ref, gath, send, recv):
        x, y, c, _ = _place()
        me = 4 * x + 2 * y + c
        gath[me] = in_ref[...]
        copies = []
        for k in range(1, N_DEV):
            dx, dy, dc = (k >> 2) & 1, (k >> 1) & 1, k & 1
            peer = (x ^ dx, y ^ dy, c ^ dc)
            cp = pltpu.make_async_remote_copy(src_ref=in_ref, dst_ref=gath.at[me], send_sem=send.at[k], recv_sem=recv.at[k],
                                              device_id=peer, device_id_type=MESH)
            cp.start()
            copies.append((cp, 4 * peer[0] + 2 * peer[1] + peer[2]))
        for k, (cp, peer_id) in enumerate(copies, start=1):
            cp.wait_send()
            pltpu.make_async_remote_copy(src_ref=in_ref, dst_ref=gath.at[peer_id], send_sem=send.at[k], recv_sem=recv.at[k],
                                         device_id=(x, y, c), device_id_type=MESH).wait_recv()
        acc = gath[0]
        for dev in range(1, N_DEV):
            acc = acc + gath[dev]
        out_ref[...] = acc

    return pl.pallas_call(
        body,
        name="all_reduce_small",
        in_specs=[pl.BlockSpec(memory_space=pltpu.VMEM)],
        out_specs=pl.BlockSpec(memory_space=pltpu.VMEM),
        out_shape=jax.ShapeDtypeStruct(pack.shape, F32),
        scratch_shapes=[pltpu.VMEM((N_DEV,) + pack.shape, F32), pltpu.SemaphoreType.DMA((N_DEV,)), pltpu.SemaphoreType.DMA((N_DEV,))],
        compiler_params=pltpu.CompilerParams(has_side_effects=True, vmem_limit_bytes=VMEM_LIMIT),
    )(pack)


def _adamw_update(g_ref, w_ref, m_ref, v_ref, go_ref, d_ref, mo_ref, vo_ref):
    bc1 = 1.0 - ADAM_B1 ** ADAM_STEP
    bc2 = 1.0 - ADAM_B2 ** ADAM_STEP
    gv = g_ref[...]
    mn = ADAM_B1 * m_ref[...] + (1.0 - ADAM_B1) * gv
    vn = ADAM_B2 * v_ref[...] + (1.0 - ADAM_B2) * (gv * gv)
    go_ref[...] = gv
    mo_ref[...] = mn
    vo_ref[...] = vn
    d_ref[...] = -ADAM_LR * ((mn / bc1) / (jnp.sqrt(vn / bc2) + ADAM_EPS) + ADAM_WD * w_ref[...])


def _adamw_small(gs, ws, ms, vs):
    n = len(gs)

    def body(*refs):
        ins, outs = refs[:4 * n], refs[4 * n:]
        for k in range(n):
            _adamw_update(*[ins[i * n + k] for i in range(4)], *outs[4 * k:4 * k + 4])

    vmem = pl.BlockSpec(memory_space=pltpu.VMEM)
    res = pl.pallas_call(
        body,
        name="adamw_small",
        in_specs=[vmem] * (4 * n),
        out_specs=[vmem] * (4 * n),
        out_shape=[jax.ShapeDtypeStruct(w.shape, F32) for w in ws for _ in range(4)],
        compiler_params=pltpu.CompilerParams(vmem_limit_bytes=VMEM_LIMIT),
    )(*gs, *ws, *ms, *vs)
    return [tuple(res[4 * k:4 * k + 4]) for k in range(n)]


def _adamw(g, w, m, v, name, g_block=0):
    r, cdim = w.shape
    tr = r if r * cdim * 4 <= (1 << 20) else _first_divisor(r, (128, 64, 32, 16, 8))

    def body(*refs):
        _adamw_update(*refs)

    spec = pl.BlockSpec((tr, cdim), lambda i: (i, 0))
    return pl.pallas_call(
        body,
        name=name,
        grid=(r // tr,),
        in_specs=[pl.BlockSpec((tr, cdim), lambda i: (i, g_block))] + [spec] * 3,
        out_specs=[spec] * 4,
        out_shape=[jax.ShapeDtypeStruct((r, cdim), F32)] * 4,
        compiler_params=_params(("parallel",)),
    )(g, w, m, v)


VEC_NAMES = ["pre_mix_norm", "q_norm", "kv_norm", "conv_b", "conv_ln_g", "conv_ln_b", "conv_out_norm",
             "attn_out_norm", "post_mix_norm", "pre_ffn_norm", "post_ffn_norm"]
LOSS_ROW = len(VEC_NAMES)
CONV_W_ROW = 16


def _cols_to_full(parts):
    _, r, cdim = parts.shape
    return parts.transpose(1, 0, 2).reshape(r, N_CHIPS * cdim)


def _full_to_cols(full):
    r, n = full.shape
    return full.reshape(r, N_CHIPS, n // N_CHIPS).transpose(1, 0, 2)


W_IN_SHARD = (2 * CONV_CH + Q_LORA + KV_LORA + QK_ROPE) // N_CHIPS
W_IN_PART = 1024
W_IN_BLOCKS = (2 * CONV_CH + Z2_COLS) // LANES
W_IN_BASE = [p * W_IN_SHARD // LANES for p in range(N_CHIPS)]
W_IN_SPAN = [-(-(p * W_IN_SHARD % LANES + W_IN_SHARD) // LANES) for p in range(N_CHIPS)]


def _w_in_block_home(b):
    n = CONV_CH // LANES
    if b < n:
        return 0, 2 * b
    if b < 2 * n:
        return 0, 2 * (b - n) + 1
    return 1, b - 2 * n


def _to_parts_w_in(shift_chip, w_in):
    r = w_in.shape[0]
    tr = 512

    def body(s_ref, a_ref, o_ref):
        o_ref[...] = jnp.zeros_like(o_ref)
        o_ref[:, :W_IN_SHARD] = a_ref[...].astype(o_ref.dtype)
        o_ref[...] = pltpu.roll(o_ref[...].astype(F32), s_ref[0], 1).astype(o_ref.dtype)

    return pl.pallas_call(
        body,
        name="to_parts_w_in",
        grid_spec=pltpu.PrefetchScalarGridSpec(
            num_scalar_prefetch=1,
            grid=(r // tr,),
            in_specs=[pl.BlockSpec((tr, W_IN_SHARD), lambda i, s_ref: (i, 0))],
            out_specs=pl.BlockSpec((None, tr, W_IN_PART), lambda i, s_ref: (s_ref[1], i, 0))),
        out_shape=jax.ShapeDtypeStruct((N_CHIPS, r, W_IN_PART), BF16),
        compiler_params=_params(("parallel",)),
    )(shift_chip, w_in)


def _assemble_w_in(parts):
    r = parts.shape[1]
    tr = ROW_TILE

    def body(p_ref, ag_ref, z2_ref):
        outs = (ag_ref, z2_ref)
        for b in range(W_IN_BLOCKS):
            blk = None
            for p in range(N_CHIPS):
                i = b - W_IN_BASE[p]
                if 0 <= i < W_IN_SPAN[p]:
                    piece = p_ref[p, :, i * LANES:(i + 1) * LANES]
                    blk = piece if blk is None else blk + piece
            which, at = _w_in_block_home(b)
            outs[which][:, at * LANES:(at + 1) * LANES] = blk

    w_ag, w_z2 = pl.pallas_call(
        body,
        name="assemble_w_in",
        grid=(r // tr,),
        in_specs=[pl.BlockSpec((N_CHIPS, tr, W_IN_PART), lambda i: (0, i, 0))],
        out_specs=[pl.BlockSpec((tr, 2 * CONV_CH), lambda i: (i, 0)), pl.BlockSpec((tr, Z2_COLS), lambda i: (i, 0))],
        out_shape=[jax.ShapeDtypeStruct((r, 2 * CONV_CH), parts.dtype), jax.ShapeDtypeStruct((r, Z2_COLS), parts.dtype)],
        compiler_params=_params(("parallel",)),
    )(parts)
    return dict(w_ag=w_ag, w_z2=w_z2)


def _w_in_grad_parts(dw_ag, dw_z2):
    r = dw_ag.shape[0]
    tr = ROW_TILE

    def body(ag_ref, z2_ref, o_ref):
        ins = (ag_ref, z2_ref)
        for p in range(N_CHIPS):
            for i in range(W_IN_PART // LANES):
                if i < W_IN_SPAN[p]:
                    which, at = _w_in_block_home(W_IN_BASE[p] + i)
                    o_ref[p, :, i * LANES:(i + 1) * LANES] = ins[which][:, at * LANES:(at + 1) * LANES]
                else:
                    o_ref[p, :, i * LANES:(i + 1) * LANES] = jnp.zeros((tr, LANES), o_ref.dtype)

    return pl.pallas_call(
        body,
        name="w_in_grad_parts",
        grid=(r // tr,),
        in_specs=[pl.BlockSpec((tr, 2 * CONV_CH), lambda i: (i, 0)), pl.BlockSpec((tr, Z2_COLS), lambda i: (i, 0))],
        out_specs=pl.BlockSpec((N_CHIPS, tr, W_IN_PART), lambda i: (0, i, 0)),
        out_shape=jax.ShapeDtypeStruct((N_CHIPS, r, W_IN_PART), dw_ag.dtype),
        compiler_params=_params(("parallel",)),
    )(dw_ag, dw_z2)


def _assemble_mixer_rest(g):
    uq = _cols_to_full(g["w_uq"]).reshape(Q_LORA, N_HEADS, QK_HEAD)
    w_uq = jnp.concatenate([uq[:, :, :QK_NOPE].reshape(Q_LORA, N_HEADS * QK_NOPE),
                            uq[:, :, QK_NOPE:].reshape(Q_LORA, N_HEADS * QK_ROPE)], axis=1)
    return dict(w_uq=w_uq, w_ukv=_cols_to_full(g["w_ukv"]), conv_w=_cols_to_full(g["conv_w"]),
                w_out=g["w_out"].reshape(-1, g["w_out"].shape[2]))


def _grads_to_parts(dw):
    uq = dw["w_uq"]
    d_uq = jnp.concatenate([uq[:, :N_HEADS * QK_NOPE].reshape(Q_LORA, N_HEADS, QK_NOPE),
                            uq[:, N_HEADS * QK_NOPE:].reshape(Q_LORA, N_HEADS, QK_ROPE)], axis=2).reshape(Q_LORA, N_HEADS * QK_HEAD)
    return dict(w_in=_w_in_grad_parts(dw["w_ag"], dw["w_z2"]), w_uq=_full_to_cols(d_uq), w_ukv=_full_to_cols(dw["w_ukv"]),
                w_out=dw["w_out"].reshape(N_CHIPS, -1, dw["w_out"].shape[1]))


MIXER = ["w_in", "w_uq", "w_ukv", "w_out"]
FFN = ["w_gu", "w_down"]
BIG = MIXER + FFN


def _pad_lanes(v, n):
    return jnp.pad(v, ((0, 0), (0, n - v.shape[1])))


def kernel(x, positions, pre_mix_norm, w_in, q_norm, w_uq, kv_norm, w_ukv, conv_w, conv_b, conv_ln_g, conv_ln_b, conv_out_norm, attn_out_norm, w_out, post_mix_norm, pre_ffn_norm, w_gate, w_up, w_down, post_ffn_norm, loss_target, m_pre_mix_norm, m_w_in, m_q_norm, m_w_uq, m_kv_norm, m_w_ukv, m_conv_w, m_conv_b, m_conv_ln_g, m_conv_ln_b, m_conv_out_norm, m_attn_out_norm, m_w_out, m_post_mix_norm, m_pre_ffn_norm, m_w_gate, m_w_up, m_w_down, m_post_ffn_norm, v_pre_mix_norm, v_w_in, v_q_norm, v_w_uq, v_kv_norm, v_w_ukv, v_conv_w, v_conv_b, v_conv_ln_g, v_conv_ln_b, v_conv_out_norm, v_attn_out_norm, v_w_out, v_post_mix_norm, v_pre_ffn_norm, v_w_gate, v_w_up, v_w_down, v_post_ffn_norm):
    given = dict(locals())
    names = ["pre_mix_norm", "w_in", "q_norm", "w_uq", "kv_norm", "w_ukv", "conv_w", "conv_b", "conv_ln_g", "conv_ln_b",
             "conv_out_norm", "attn_out_norm", "w_out", "post_mix_norm", "pre_ffn_norm", "w_gate", "w_up", "w_down", "post_ffn_norm"]
    def as_2d(a):
        return a if a.ndim == 2 else a[0]

    weights = {n: as_2d(given[n]) for n in names}
    mom = {n: as_2d(given["m_" + n]) for n in names}
    var = {n: as_2d(given["v_" + n]) for n in names}
    d = D_MODEL

    inv_freq = ROPE_THETA ** (-jnp.arange(0, QK_ROPE, 2, dtype=F32) / QK_ROPE)
    ang = positions[0].astype(F32)[:, None] * inv_freq
    cos, sin = jnp.cos(ang), jnp.sin(ang)
    cos2 = jnp.concatenate([cos, cos, cos, cos], axis=1)
    sin2 = jnp.concatenate([-sin, sin, -sin, sin], axis=1)

    chip = 2 * lax.axis_index("x") + lax.axis_index("y")
    core = lax.axis_index("c")
    chip1 = chip.astype(jnp.int32).reshape(1)
    pieces = {n: [weights[n]] for n in BIG if n != "w_gu"}
    pieces["w_gu"] = [weights["w_gate"], weights["w_up"]]
    core1 = core.astype(jnp.int32).reshape(1)
    place = jnp.stack([chip, core]).astype(jnp.int32)
    rest = ["w_uq", "w_ukv", "w_out"]
    w_in_shift = (chip * W_IN_SHARD) % LANES
    mix_bufs = [_to_parts_w_in(jnp.stack([w_in_shift, chip]).astype(jnp.int32), weights["w_in"])]
    mix_bufs += [_to_parts(chip1, pieces[n], BF16, "to_parts_" + n) for n in rest]
    mix_bufs.append(_to_parts(chip1, [jnp.pad(weights["conv_w"], ((0, CONV_K_PAD - CONV_K), (0, 0)))], F32, "to_parts_conv_w"))
    mix_whole = [False] * 4 + [True]
    mix_sems, mix_thru = _gather_start(mix_bufs, mix_whole, [[0], [1, 2, 3, 4]], "gather_start_mixer")
    ffn_bufs = [_to_parts(chip1, pieces[n], BF16, "to_parts_" + n) for n in FFN]
    ffn_sems, ffn_thru = _gather_start(ffn_bufs, [False, False], [[0], [1]], "gather_start_ffn")
    got = _gather_wait(mix_thru[:1], [False], *mix_sems[0], cos2, "gather_wait_w_in")
    full = _assemble_w_in(_gather_hand_on(got, "gather_hand_on_w_in")[0])
    vec = {n: weights[n] for n in VEC_NAMES}
    rs = {}

    def get_mixer_rest(after):
        got = _gather_wait(mix_thru[1:], mix_whole[1:], *mix_sems[1], after, "gather_wait_mixer_rest")
        got = list(_gather_hand_on(got[:3], "gather_hand_on_mixer_rest")) + [got[3]]
        return _assemble_mixer_rest(dict(zip(rest + ["conv_w"], got)))

    def ffn_landed(k, key):
        def hook(after):
            got = _gather_wait(ffn_thru[k:k + 1], [False], *ffn_sems[k], after, "gather_wait_" + key)
            rs[key] = _hand_on_start(got, "hand_on_start_" + key)
            return rs[key][2][0]
        return hook

    def ffn_weight(key):
        def hook(after):
            send, recv, bufs = rs[key]
            return _hand_on_wait(bufs, send, recv, after, "hand_on_wait_" + key)[0]
        return hook

    def get_w_down(after):
        got = ffn_weight("w_down")(after)
        return got.reshape(-1, got.shape[2])

    def pair_start(key):
        def hook(dw):
            rs[key] = _pair_exchange_start(dw.reshape(N_CHIPS, -1, dw.shape[-1]), "grad_pair_start_" + key)
            return rs[key][2]
        return hook

    def reduce_start(group, plist, landed):
        sums = [_pair_sum(core1, a, b, "grad_pair_sum_%s_%d" % (group, k)) for k, (a, b) in enumerate(zip(plist, landed))]
        rs[group] = _chip_exchange_start(sums, "grad_chip_exchange_start_" + group)
        return rs[group][2][0]

    def reduce_finish(group, after):
        send, recv, sums, slots = rs[group]
        sums, slots = _chip_exchange_wait(sums, slots, send, recv, after, "grad_chip_exchange_wait_" + group)
        halves = [_chip_sum(place, s, sl, "grad_chip_sum_%s_%d" % (group, k)) for k, (s, sl) in enumerate(zip(sums, slots))]
        return list(_half_exchange(halves, "grad_half_exchange_" + group))

    def ffn_grads_exchanged(after):
        pairs = [_pair_exchange_wait(*rs[key], after, "grad_pair_wait_" + key) for key in ("dw_gu", "dw_down")]
        return reduce_start("ffn", [p[0] for p in pairs], [p[1] for p in pairs])

    hooks = dict(mixer_rest=get_mixer_rest, w_gu_landed=ffn_landed(0, "w_gu"), w_down_landed=ffn_landed(1, "w_down"),
                 w_gu=ffn_weight("w_gu"), w_down=get_w_down, dw_down=pair_start("dw_down"), dw_gu=pair_start("dw_gu"),
                 grads_exchanged=ffn_grads_exchanged)
    loss, grad_x, dw, dvec = _local_step(x[0], loss_target[0], cos2, sin2, vec, full, hooks)

    rows = [_pad_lanes(dvec[n], d) for n in VEC_NAMES] + [_pad_lanes(loss, d)]
    rows.append(jnp.zeros((CONV_W_ROW - len(rows), d), F32))
    rows.append(dw["conv_w"].reshape(SMALL_ROWS - CONV_W_ROW, d))
    small = _all_reduce_small(jnp.concatenate(rows, axis=0))
    g_conv_w_full = small[CONV_W_ROW:].reshape(CONV_K_PAD, CONV_CH)
    g_small = {n: small[i:i + 1, :weights[n].shape[1]] for i, n in enumerate(VEC_NAMES)}
    g_small["conv_w"] = lax.dynamic_slice(g_conv_w_full, (0, chip * (CONV_CH // N_CHIPS)), (CONV_K_PAD, CONV_CH // N_CHIPS))[:CONV_K]
    loss_out = small[LOSS_ROW, 0]

    parts = _grads_to_parts(dw)
    plist = [parts[n] for n in MIXER]
    started = reduce_start("mixer", plist, _pair_exchange(plist, "grad_pair_exchange_mixer"))
    g_gu, g_down = reduce_finish("ffn", [started])
    res = {}
    for n, g, blk in (("w_gate", g_gu, 0), ("w_up", g_gu, 1), ("w_down", g_down, 0)):
        res[n] = _adamw(g, weights[n], mom[n], var[n], "adamw_" + n, g_block=blk)
    small_names = VEC_NAMES + ["conv_w"]
    res.update(zip(small_names, _adamw_small([g_small[n] for n in small_names], [weights[n] for n in small_names],
                                             [mom[n] for n in small_names], [var[n] for n in small_names])))
    done_meanwhile = [res["w_gate"][1], res["w_up"][1], res["w_down"][1], res["conv_w"][1], grad_x]
    g_mixer = reduce_finish("mixer", done_meanwhile)
    g_mixer[0] = lax.dynamic_slice(g_mixer[0], (0, w_in_shift), (g_mixer[0].shape[0], W_IN_SHARD))
    for n, g in zip(MIXER, g_mixer):
        res[n] = _adamw(g, weights[n], mom[n], var[n], "adamw_" + n)
    outs = [loss_out, grad_x[None]]
    for i in range(4):
        outs += [res[n][i].reshape(given[n].shape) for n in names]
    return tuple(outs)
```

```python
import functools

import jax
import jax.numpy as jnp
from jax import lax
from jax.experimental import pallas as pl
from jax.experimental.pallas import tpu as pltpu

F32 = jnp.float32
BF16 = jnp.bfloat16

D_MODEL = 2048
CONV_CH = 1024
CONV_K = 31
CONV_K_PAD = 32
N_HEADS = 8
QK_NOPE = 128
QK_ROPE = 64
V_HEAD = 128
QK_HEAD = QK_NOPE + QK_ROPE
Q_LORA = 768
KV_LORA = 512
ATTN_CH = N_HEADS * V_HEAD
Z2_COLS = Q_LORA + KV_LORA + 128
D_FF = 5632
ROPE_THETA = 10000.0
EPS = 1e-6
LANES = 128
N_CHIPS = 4
N_DEV = 8

ADAM_LR = 0.001
ADAM_B1 = 0.9
ADAM_B2 = 0.999
ADAM_EPS = 1e-08
ADAM_WD = 0.01
ADAM_STEP = 10

VMEM_LIMIT = 56 * 1024 * 1024
ROW_TILE = 256
MAX_TK = 2816
MESH = pl.DeviceIdType.MESH


def _params(sem=None):
    return pltpu.CompilerParams(dimension_semantics=sem, vmem_limit_bytes=VMEM_LIMIT)


def _first_divisor(n, cands):
    for c in cands:
        if n % c == 0:
            return c
    return n


def _matmul(pairs, mode, out_dtype, name, b_parts=None, out_parts=False, tiles=(None, None, None), after=None):
    a0, b0 = pairs[0]
    part_c = b0.shape[2] if b_parts else None
    if mode == "nn":
        m, n = a0.shape[0], (N_CHIPS * part_c if b_parts else b0.shape[1])
        ks = [a.shape[1] for a, _ in pairs]
    elif mode == "nt":
        m, n = a0.shape[0], b0.shape[-2]
        ks = [a.shape[1] for a, _ in pairs]
    else:
        m, n = a0.shape[1], b0.shape[1]
        ks = [a.shape[0] for a, _ in pairs]
    tm = tiles[0] or _first_divisor(m, (1024, 768, 512, 256))
    tn = tiles[1] or (n if n <= 1536 else _first_divisor(n, (1024, 512, 256, 128)))
    tks = [tiles[2] or (k if k <= MAX_TK else MAX_TK) for k in ks]
    nks = [k // tk for k, tk in zip(ks, tks)]
    offs = [sum(nks[:p]) for p in range(len(pairs))]
    nk = sum(nks)
    n_pairs = len(pairs)
    assert not (b_parts or out_parts) or n_pairs == 1

    def kk(k, p):
        return jnp.clip(k - offs[p], 0, nks[p] - 1)

    in_specs = []
    for p in range(n_pairs):
        tk = tks[p]
        if mode == "nn":
            in_specs.append(pl.BlockSpec((tm, tk), lambda i, j, k, p=p: (i, kk(k, p))))
            if b_parts == "n":
                per = part_c // tn
                in_specs.append(pl.BlockSpec((None, tk, tn), lambda i, j, k: (j // per, k, j % per)))
            else:
                in_specs.append(pl.BlockSpec((tk, tn), lambda i, j, k, p=p: (kk(k, p), j)))
        elif mode == "nt":
            in_specs.append(pl.BlockSpec((tm, tk), lambda i, j, k, p=p: (i, kk(k, p))))
            if b_parts == "k":
                per = part_c // tk
                in_specs.append(pl.BlockSpec((None, tn, tk), lambda i, j, k: (k // per, j, k % per)))
            else:
                in_specs.append(pl.BlockSpec((tn, tk), lambda i, j, k, p=p: (j, kk(k, p))))
        else:
            in_specs.append(pl.BlockSpec((tk, tm), lambda i, j, k, p=p: (kk(k, p), i)))
            in_specs.append(pl.BlockSpec((tk, tn), lambda i, j, k, p=p: (kk(k, p), j)))
    if out_parts:
        out_per = (n // N_CHIPS) // tn
        out_spec = pl.BlockSpec((None, tm, tn), lambda i, j, k: (j // out_per, i, j % out_per))
        out_shape = jax.ShapeDtypeStruct((N_CHIPS, m, n // N_CHIPS), out_dtype)
    else:
        out_spec = pl.BlockSpec((tm, tn), lambda i, j, k: (i, j))
        out_shape = jax.ShapeDtypeStruct((m, n), out_dtype)
    dims = {"nn": (((1,), (0,)), ((), ())), "nt": (((1,), (1,)), ((), ())), "tn": (((0,), (0,)), ((), ()))}[mode]

    n_after = 0 if after is None else 1

    def body(*refs):
        o_ref = refs[2 * n_pairs + n_after]
        k = pl.program_id(2)

        def prod(p):
            return lax.dot_general(refs[2 * p][...], refs[2 * p + 1][...], dims, preferred_element_type=F32)

        if nk == 1:
            o_ref[...] = prod(0).astype(o_ref.dtype)
            return
        acc = refs[2 * n_pairs + n_after + 1]
        for p in range(n_pairs):
            first, last = offs[p], offs[p] + nks[p] - 1
            lo, hi = max(first, 1), min(last, nk - 2)
            if first == 0:
                @pl.when(k == 0)
                def _(p=p):
                    acc[...] = prod(p)

            if lo <= hi:
                @pl.when((k >= lo) & (k <= hi))
                def _(p=p):
                    acc[...] += prod(p)

            if last == nk - 1:
                @pl.when(k == nk - 1)
                def _(p=p):
                    o_ref[...] = (acc[...] + prod(p)).astype(o_ref.dtype)

    flat = [t for pr in pairs for t in pr] + ([] if after is None else [after])
    return pl.pallas_call(
        body,
        name=name,
        grid=(m // tm, n // tn, nk),
        in_specs=in_specs + [pl.BlockSpec(memory_space=pl.ANY)] * n_after,
        out_specs=out_spec,
        out_shape=out_shape,
        scratch_shapes=[pltpu.VMEM((tm, tn), F32)] if nk > 1 else [],
        compiler_params=_params(("parallel", "parallel", "arbitrary")),
    )(*flat)


F4 = D_FF // N_CHIPS
FFN_TM = 512
FFN_STRIP = 256


def _ffn_up(hf, w_gu):
    t, d = hf.shape

    def body(a_ref, b_ref, gu_ref, act_ref):
        for r in range(0, FFN_TM, FFN_STRIP):
            acc = jnp.dot(a_ref[r:r + FFN_STRIP, :], b_ref[...], preferred_element_type=F32)
            g = acc[:, :F4]
            gu_ref[r:r + FFN_STRIP, :] = acc.astype(gu_ref.dtype)
            act_ref[r:r + FFN_STRIP, :] = (g * _sigmoid(g) * acc[:, F4:]).astype(act_ref.dtype)

    return pl.pallas_call(
        body,
        name="ffn_up",
        grid=(N_CHIPS, t // FFN_TM),
        in_specs=[pl.BlockSpec((FFN_TM, d), lambda q, i: (i, 0)),
                  pl.BlockSpec((None, d, 2 * F4), lambda q, i: (q, 0, 0))],
        out_specs=[pl.BlockSpec((FFN_TM, 2 * F4), lambda q, i: (i, q)),
                   pl.BlockSpec((FFN_TM, F4), lambda q, i: (i, q))],
        out_shape=[jax.ShapeDtypeStruct((t, 2 * D_FF), BF16), jax.ShapeDtypeStruct((t, D_FF), BF16)],
        compiler_params=_params(("parallel", "parallel")),
    )(hf, w_gu)


def _ffn_down_dx(d_ff, w_down, gu):
    t, d = d_ff.shape

    def body(a_ref, b_ref, gu_ref, o_ref):
        for r in range(0, FFN_TM, FFN_STRIP):
            rows = slice(r, r + FFN_STRIP)
            d_act = lax.dot_general(a_ref[rows, :], b_ref[...], NT_DIMS, preferred_element_type=F32)
            g = gu_ref[rows, :F4].astype(F32)
            u = gu_ref[rows, F4:].astype(F32)
            sg = _sigmoid(g)
            o_ref[rows, :F4] = (d_act * u * sg * (1.0 + g * (1.0 - sg))).astype(o_ref.dtype)
            o_ref[rows, F4:] = (d_act * g * sg).astype(o_ref.dtype)

    return pl.pallas_call(
        body,
        name="ffn_down_dx",
        grid=(N_CHIPS, t // FFN_TM),
        in_specs=[pl.BlockSpec((FFN_TM, d), lambda q, i: (i, 0)),
                  pl.BlockSpec((F4, d), lambda q, i: (q, 0)),
                  pl.BlockSpec((FFN_TM, 2 * F4), lambda q, i: (i, q))],
        out_specs=pl.BlockSpec((FFN_TM, 2 * F4), lambda q, i: (i, q)),
        out_shape=jax.ShapeDtypeStruct((t, 2 * D_FF), BF16),
        compiler_params=_params(("parallel", "parallel")),
    )(d_ff, w_down, gu)


def _rowwise(fn, row_ins, vec_ins, row_outs, acc_outs, name, after=None):
    t = row_ins[0].shape[0]
    tm = ROW_TILE
    n_in = len(row_ins) + len(vec_ins)
    n_row = len(row_outs)
    extra = [] if after is None else [after]

    def body(*refs):
        ins = [r[...] for r in refs[:n_in]]
        outs = refs[n_in + len(extra):]
        vals = fn(*ins)
        for r, v in zip(outs[:n_row], vals[:n_row]):
            r[...] = v.astype(r.dtype)
        if acc_outs:
            @pl.when(pl.program_id(0) == 0)
            def _():
                for r in outs[n_row:]:
                    r[...] = jnp.zeros_like(r)

            for r, v in zip(outs[n_row:], vals[n_row:]):
                r[...] += v

    in_specs = [pl.BlockSpec((tm, a.shape[1]), lambda i: (i, 0)) for a in row_ins]
    in_specs += [pl.BlockSpec(a.shape, lambda i: (0, 0)) for a in vec_ins]
    out_specs = [pl.BlockSpec((tm, c), lambda i: (i, 0)) for c, _ in row_outs]
    out_specs += [pl.BlockSpec((1, c), lambda i: (0, 0)) for c in acc_outs]
    out_shape = [jax.ShapeDtypeStruct((t, c), dt) for c, dt in row_outs]
    out_shape += [jax.ShapeDtypeStruct((1, c), F32) for c in acc_outs]
    return pl.pallas_call(
        body,
        name=name,
        grid=(t // tm,),
        in_specs=in_specs + [pl.BlockSpec(memory_space=pl.ANY)] * len(extra),
        out_specs=out_specs,
        out_shape=out_shape,
        compiler_params=_params(("arbitrary",)),
    )(*row_ins, *vec_ins, *extra)


def _mean(v):
    return jnp.mean(v, axis=-1, keepdims=True)


def _colsum(v):
    return jnp.sum(v, axis=0, keepdims=True)


def _rms_fwd(v, g):
    r = lax.rsqrt(_mean(v * v) + EPS)
    vhat = v * r
    return vhat * g, vhat, r


def _rms_bwd(dn, vhat, r, g):
    dng = dn * g
    return r * (dng - vhat * _mean(dng * vhat)), _colsum(dn * vhat)


def _swap_rope_halves(v):
    n = v.shape[-1]
    lane = lax.broadcasted_iota(jnp.int32, v.shape, v.ndim - 1)
    return jnp.where(lane % QK_ROPE < QK_ROPE // 2, pltpu.roll(v, n - QK_ROPE // 2, v.ndim - 1),
                     pltpu.roll(v, QK_ROPE // 2, v.ndim - 1))


def _rope(v, cos2, sin2):
    return v * cos2 + _swap_rope_halves(v) * sin2


def _rope_transposed(dv, cos2, sin2):
    return dv * cos2 + _swap_rope_halves(dv * sin2)


def _sigmoid(v):
    return 1.0 / (1.0 + jnp.exp(-v))


CONV_ROWS = 256


def _conv_fwd(ag, conv_w, conv_b):
    t = ag.shape[0]
    cb = LANES

    def body(ag_ref, w_ref, b_ref, o_ref, scr):
        a = ag_ref[:, :cb].astype(F32)
        g = ag_ref[:, cb:].astype(F32)
        scr[pl.ds(0, CONV_K_PAD), :] = jnp.zeros((CONV_K_PAD, cb), F32)
        scr[pl.ds(CONV_K_PAD, t), :] = a * _sigmoid(g)
        for r0 in range(0, t, CONV_ROWS):
            acc = jnp.zeros((CONV_ROWS, cb), F32) + b_ref[...]
            for k in range(CONV_K):
                acc = acc + w_ref[k:k + 1, :] * scr[pl.ds(r0 + CONV_K_PAD - (CONV_K - 1) + k, CONV_ROWS), :]
            o_ref[pl.ds(r0, CONV_ROWS), :] = acc

    return pl.pallas_call(
        body,
        name="conv_fwd",
        grid=(CONV_CH // cb,),
        in_specs=[pl.BlockSpec((t, 2 * cb), lambda j: (0, j)),
                  pl.BlockSpec((CONV_K_PAD, cb), lambda j: (0, j)),
                  pl.BlockSpec((1, cb), lambda j: (0, j))],
        out_specs=pl.BlockSpec((t, cb), lambda j: (0, j)),
        out_shape=jax.ShapeDtypeStruct((t, CONV_CH), F32),
        scratch_shapes=[pltpu.VMEM((t + CONV_K_PAD, cb), F32)],
        compiler_params=_params(("parallel",)),
    )(ag, conv_w, conv_b)


def _conv_bwd(d_u1, ag, conv_w):
    t = ag.shape[0]
    cb = LANES

    def body(du_ref, ag_ref, w_ref, dag_ref, dw_ref, db_ref, su, sd):
        a = ag_ref[:, :cb].astype(F32)
        g = ag_ref[:, cb:].astype(F32)
        sg = _sigmoid(g)
        su[pl.ds(0, CONV_K_PAD), :] = jnp.zeros((CONV_K_PAD, cb), F32)
        su[pl.ds(CONV_K_PAD, t), :] = a * sg
        sd[pl.ds(0, t), :] = du_ref[...]
        sd[pl.ds(t, CONV_K_PAD), :] = jnp.zeros((CONV_K_PAD, cb), F32)
        db_ref[...] = _colsum(du_ref[...])
        dw_ref[...] = jnp.zeros_like(dw_ref)
        for r0 in range(0, t, CONV_ROWS):
            du = sd[pl.ds(r0, CONV_ROWS), :]
            acc = jnp.zeros((CONV_ROWS, cb), F32)
            for k in range(CONV_K):
                acc = acc + w_ref[k:k + 1, :] * sd[pl.ds(r0 + (CONV_K - 1) - k, CONV_ROWS), :]
                dw_ref[k:k + 1, :] += _colsum(du * su[pl.ds(r0 + CONV_K_PAD - (CONV_K - 1) + k, CONV_ROWS), :])
            sgc = sg[r0:r0 + CONV_ROWS]
            ac = a[r0:r0 + CONV_ROWS]
            dag_ref[pl.ds(r0, CONV_ROWS), :cb] = (acc * sgc).astype(dag_ref.dtype)
            dag_ref[pl.ds(r0, CONV_ROWS), cb:] = (acc * ac * sgc * (1.0 - sgc)).astype(dag_ref.dtype)

    return pl.pallas_call(
        body,
        name="conv_bwd",
        grid=(CONV_CH // cb,),
        in_specs=[pl.BlockSpec((t, cb), lambda j: (0, j)),
                  pl.BlockSpec((t, 2 * cb), lambda j: (0, j)),
                  pl.BlockSpec((CONV_K_PAD, cb), lambda j: (0, j))],
        out_specs=[pl.BlockSpec((t, 2 * cb), lambda j: (0, j)),
                   pl.BlockSpec((CONV_K_PAD, cb), lambda j: (0, j)),
                   pl.BlockSpec((1, cb), lambda j: (0, j))],
        out_shape=[jax.ShapeDtypeStruct((t, 2 * CONV_CH), BF16),
                   jax.ShapeDtypeStruct((CONV_K_PAD, CONV_CH), F32),
                   jax.ShapeDtypeStruct((1, CONV_CH), F32)],
        scratch_shapes=[pltpu.VMEM((t + CONV_K_PAD, cb), F32), pltpu.VMEM((t + CONV_K_PAD, cb), F32)],
        compiler_params=_params(("parallel",)),
    )(d_u1, ag, conv_w)


ATT_TQ = 256
NEG = float(jnp.finfo(jnp.float32).min)
SCALE = QK_HEAD ** -0.5
NT_DIMS = (((1,), (1,)), ((), ()))
TN_DIMS = (((0,), (0,)), ((), ()))


def _att_probs(qf, kf, row0):
    s = lax.dot_general(qf, kf, NT_DIMS, preferred_element_type=F32) * SCALE
    tq, t = s.shape
    qpos = row0 + lax.broadcasted_iota(jnp.int32, (tq, t), 0)
    kpos = lax.broadcasted_iota(jnp.int32, (tq, t), 1)
    s = jnp.where(kpos <= qpos, s, NEG)
    p = jnp.exp(s - jnp.max(s, axis=-1, keepdims=True))
    return p * (1.0 / jnp.sum(p, axis=-1, keepdims=True))


def _half_mask(shape, which):
    lane = lax.broadcasted_iota(jnp.int32, shape, len(shape) - 1)
    return (lane // QK_ROPE == which).astype(F32)


def _attention_fwd(q, kv, kpe2, cos2, sin2):
    t = q.shape[0]
    tq = ATT_TQ

    def body(qn_ref, qp_ref, c_ref, s_ref, kv_ref, kpe_ref, o_ref):
        row0 = pl.program_id(1) * tq
        roped = _rope(qp_ref[...].astype(F32), c_ref[...], s_ref[...])
        for e in range(2):
            qf = jnp.concatenate([qn_ref[:, e * QK_NOPE:(e + 1) * QK_NOPE],
                                  (roped * _half_mask(roped.shape, e)).astype(BF16)], axis=1)
            kf = jnp.concatenate([kv_ref[:, e * 256:e * 256 + QK_NOPE], kpe_ref[...]], axis=1)
            p = _att_probs(qf, kf, row0)
            v = kv_ref[:, e * 256 + QK_NOPE:(e + 1) * 256]
            o_ref[:, e * V_HEAD:(e + 1) * V_HEAD] = jnp.dot(p.astype(BF16), v, preferred_element_type=F32).astype(o_ref.dtype)

    return pl.pallas_call(
        body,
        name="attention_fwd",
        grid=(N_HEADS // 2, t // tq),
        in_specs=[pl.BlockSpec((tq, 2 * QK_NOPE), lambda h, i: (i, h)),
                  pl.BlockSpec((tq, LANES), lambda h, i: (i, N_HEADS + h)),
                  pl.BlockSpec((tq, LANES), lambda h, i: (i, 0)),
                  pl.BlockSpec((tq, LANES), lambda h, i: (i, 0)),
                  pl.BlockSpec((t, 512), lambda h, i: (0, h)),
                  pl.BlockSpec((t, LANES), lambda h, i: (0, 0))],
        out_specs=pl.BlockSpec((tq, 2 * V_HEAD), lambda h, i: (i, h)),
        out_shape=jax.ShapeDtypeStruct((t, ATTN_CH), BF16),
        compiler_params=_params(("parallel", "parallel")),
    )(q, q, cos2, sin2, kv, kpe2)


def _attention_bwd(q, kv, kpe2, cos2, sin2, d_attn):
    t = q.shape[0]
    tq = ATT_TQ
    n_q = t // tq

    def body(qn_ref, qp_ref, c_ref, s_ref, kv_ref, kpe_ref, do_ref, dqn_ref, dqp_ref, dkv_ref, dkpe_ref, dkv_acc):
        h, i = pl.program_id(0), pl.program_id(1)
        row0 = i * tq

        @pl.when(i == 0)
        def _():
            dkv_acc[...] = jnp.zeros_like(dkv_acc)

        @pl.when((i == 0) & (h == 0))
        def _():
            dkpe_ref[...] = jnp.zeros_like(dkpe_ref)

        roped = _rope(qp_ref[...].astype(F32), c_ref[...], s_ref[...])
        d_roped = jnp.zeros((tq, LANES), F32)
        for e in range(2):
            mask = _half_mask(roped.shape, e)
            qf = jnp.concatenate([qn_ref[:, e * QK_NOPE:(e + 1) * QK_NOPE], (roped * mask).astype(BF16)], axis=1)
            kf = jnp.concatenate([kv_ref[:, e * 256:e * 256 + QK_NOPE], kpe_ref[...]], axis=1)
            v = kv_ref[:, e * 256 + QK_NOPE:(e + 1) * 256]
            do = do_ref[:, e * V_HEAD:(e + 1) * V_HEAD]
            p = _att_probs(qf, kf, row0)
            dp = lax.dot_general(do, v, NT_DIMS, preferred_element_type=F32)
            ds = (p * (dp - jnp.sum(p * dp, axis=-1, keepdims=True)) * SCALE).astype(BF16)
            dqf = jnp.dot(ds, kf, preferred_element_type=F32)
            dkf = lax.dot_general(ds, qf, TN_DIMS, preferred_element_type=F32)
            dv = lax.dot_general(p.astype(BF16), do, TN_DIMS, preferred_element_type=F32)
            dqn_ref[:, e * QK_NOPE:(e + 1) * QK_NOPE] = dqf[:, :QK_NOPE].astype(dqn_ref.dtype)
            d_roped = d_roped + dqf[:, QK_NOPE:] * mask
            dkv_acc[:, e * 256:e * 256 + QK_NOPE] += dkf[:, :QK_NOPE]
            dkv_acc[:, e * 256 + QK_NOPE:(e + 1) * 256] += dv
            dkpe_ref[...] += dkf[:, QK_NOPE:]
        dqp_ref[...] = _rope_transposed(d_roped, c_ref[...], s_ref[...]).astype(dqp_ref.dtype)

        @pl.when(i == n_q - 1)
        def _():
            dkv_ref[...] = dkv_acc[...].astype(dkv_ref.dtype)

    return pl.pallas_call(
        body,
        name="attention_bwd",
        grid=(N_HEADS // 2, n_q),
        in_specs=[pl.BlockSpec((tq, 2 * QK_NOPE), lambda h, i: (i, h)),
                  pl.BlockSpec((tq, LANES), lambda h, i: (i, N_HEADS + h)),
                  pl.BlockSpec((tq, LANES), lambda h, i: (i, 0)),
                  pl.BlockSpec((tq, LANES), lambda h, i: (i, 0)),
                  pl.BlockSpec((t, 512), lambda h, i: (0, h)),
                  pl.BlockSpec((t, LANES), lambda h, i: (0, 0)),
                  pl.BlockSpec((tq, 2 * V_HEAD), lambda h, i: (i, h))],
        out_specs=[pl.BlockSpec((tq, 2 * QK_NOPE), lambda h, i: (i, h)),
                   pl.BlockSpec((tq, LANES), lambda h, i: (i, h)),
                   pl.BlockSpec((t, 512), lambda h, i: (0, h)),
                   pl.BlockSpec((t, LANES), lambda h, i: (0, 0))],
        out_shape=[jax.ShapeDtypeStruct((t, N_HEADS * QK_NOPE), BF16),
                   jax.ShapeDtypeStruct((t, N_HEADS * QK_ROPE), BF16),
                   jax.ShapeDtypeStruct((t, N_HEADS * 256), BF16),
                   jax.ShapeDtypeStruct((t, LANES), F32)],
        scratch_shapes=[pltpu.VMEM((t, 512), F32)],
        compiler_params=_params(("arbitrary", "arbitrary")),
    )(q, q, cos2, sin2, kv, kpe2, d_attn)


def _local_step(x, target, cos2, sin2, vec, w, ffn):
    d = D_MODEL

    (h,) = _rowwise(lambda xv, g: (_rms_fwd(xv, g)[0],), [x], [vec["pre_mix_norm"]], [(d, BF16)], [], "pre_mix_norm_fwd")
    ag = _matmul([(h, w["w_ag"])], "nn", BF16, "in_proj_ag")
    z2 = _matmul([(h, w["w_z2"])], "nn", BF16, "in_proj_z2")
    w = {**w, **ffn["mixer_rest"](z2)}
    u1 = _conv_fwd(ag, w["conv_w"], vec["conv_b"])

    def latents_fwd(z, c2, s2, qg, kvg):
        z = z.astype(F32)
        qn = _rms_fwd(z[:, :Q_LORA], qg)[0]
        kvn = _rms_fwd(z[:, Q_LORA:Q_LORA + KV_LORA], kvg)[0]
        kr = z[:, Q_LORA + KV_LORA:]
        kr2 = kr + pltpu.roll(kr, QK_ROPE, 1)
        return qn, kvn, _rope(kr2, c2, s2)

    qn, kvn, kpe2 = _rowwise(latents_fwd, [z2, cos2, sin2], [vec["q_norm"], vec["kv_norm"]],
                             [(Q_LORA, BF16), (KV_LORA, BF16), (LANES, BF16)], [], "latents_fwd")
    q = _matmul([(qn, w["w_uq"])], "nn", BF16, "q_up")
    kv = _matmul([(kvn, w["w_ukv"])], "nn", BF16, "kv_up")
    attn = _attention_fwd(q, kv, kpe2, cos2, sin2)

    def conv_post(u, lg, lb):
        mu = _mean(u)
        uc = u - mu
        rstd = lax.rsqrt(_mean(uc * uc) + EPS)
        uhat = uc * rstd
        u2 = uhat * lg + lb
        sg = _sigmoid(u2)
        return uhat, rstd, u2, sg, u2 * sg

    def mix_in_fwd(u, at, lg, lb, cg, ag_):
        u3 = conv_post(u, lg, lb)[4]
        cn = _rms_fwd(u3, cg)[0]
        an = _rms_fwd(at.astype(F32), ag_)[0]
        return (jnp.concatenate([cn, an], axis=1),)

    (cat,) = _rowwise(mix_in_fwd, [u1, attn], [vec["conv_ln_g"], vec["conv_ln_b"], vec["conv_out_norm"], vec["attn_out_norm"]],
                      [(2 * CONV_CH, BF16)], [], "mix_in_fwd")
    mix = _matmul([(cat, w["w_out"])], "nn", F32, "out_proj")
    landed = ffn["w_gu_landed"](mix)

    def residual1(xv, mv, gpm, gpf):
        x1 = xv + _rms_fwd(mv, gpm)[0]
        return x1, _rms_fwd(x1, gpf)[0]

    x1, hf = _rowwise(residual1, [x, mix], [vec["post_mix_norm"], vec["pre_ffn_norm"]], [(d, F32), (d, BF16)], [],
                      "residual1_fwd", after=landed)
    w_gu = ffn["w_gu"](hf)
    gu, act = _ffn_up(hf, w_gu)
    w_down = ffn["w_down"](act)
    ff = _matmul([(act, w_down)], "nn", F32, "ffn_down")

    def loss_head(x1v, ffv, tg, g):
        n, fhat, r = _rms_fwd(ffv, g)
        err = x1v + n - tg
        loss = 0.5 * jnp.sum(_mean(err * err), axis=0, keepdims=True)
        dy = err * (1.0 / d)
        d_ff, dg = _rms_bwd(dy, fhat, r, g)
        return dy, d_ff, dg, jnp.broadcast_to(loss, (1, LANES))

    dy, d_ff, g_post_ffn, loss = _rowwise(loss_head, [x1, ff, target], [vec["post_ffn_norm"]],
                                          [(d, F32), (d, BF16)], [d, LANES], "loss_head")
    d_gu = _ffn_down_dx(d_ff, w_down, gu)
    dw_down = _matmul([(act, d_ff)], "tn", BF16, "ffn_down_dw", tiles=(F4, None, None))
    started = ffn["dw_down"](dw_down)
    dw_gu = _matmul([(hf, d_gu)], "tn", BF16, "ffn_gate_up_dw", out_parts=True, tiles=(None, F4, None), after=started)
    started = ffn["dw_gu"](dw_gu)
    d_hf = _matmul([(d_gu, w_gu)], "nt", F32, "ffn_gate_up_dx", b_parts="k", after=started)
    started = ffn["grads_exchanged"](d_hf)

    def residual1_bwd(dyv, dhf, x1v, mv, gpf, gpm):
        _, x1hat, r1 = _rms_fwd(x1v, gpf)
        dn, dgpf = _rms_bwd(dhf, x1hat, r1, gpf)
        d_x1 = dyv + dn
        _, mhat, rm = _rms_fwd(mv, gpm)
        d_mix, dgpm = _rms_bwd(d_x1, mhat, rm, gpm)
        return d_x1, d_mix, dgpf, dgpm

    d_x1, d_mix, g_pre_ffn, g_post_mix = _rowwise(residual1_bwd, [dy, d_hf, x1, mix], [vec["pre_ffn_norm"], vec["post_mix_norm"]],
                                                  [(d, F32), (d, BF16)], [d, d], "residual1_bwd", after=started)
    d_cat = _matmul([(d_mix, w["w_out"])], "nt", BF16, "out_proj_dx")
    dw_out = _matmul([(cat, d_mix)], "tn", BF16, "out_proj_dw")

    def mix_in_bwd(dc, u, at, lg, lb, cg, ag_):
        dc = dc.astype(F32)
        uhat, rstd, u2, sg, u3 = conv_post(u, lg, lb)
        _, u3hat, rc = _rms_fwd(u3, cg)
        d_u3, dcg = _rms_bwd(dc[:, :CONV_CH], u3hat, rc, cg)
        d_u2 = d_u3 * sg * (1.0 + u2 * (1.0 - sg))
        dgl = d_u2 * lg
        d_u1 = rstd * (dgl - _mean(dgl) - uhat * _mean(dgl * uhat))
        _, ahat, ra = _rms_fwd(at.astype(F32), ag_)
        d_at, dag = _rms_bwd(dc[:, CONV_CH:], ahat, ra, ag_)
        return d_u1, d_at, dcg, _colsum(d_u2 * uhat), _colsum(d_u2), dag

    d_u1, d_attn, g_conv_out, g_ln_g, g_ln_b, g_attn_out = _rowwise(
        mix_in_bwd, [d_cat, u1, attn], [vec["conv_ln_g"], vec["conv_ln_b"], vec["conv_out_norm"], vec["attn_out_norm"]],
        [(CONV_CH, F32), (ATTN_CH, BF16)], [CONV_CH] * 4, "mix_in_bwd")
    d_ag, d_conv_w, g_conv_b = _conv_bwd(d_u1, ag, w["conv_w"])
    d_qn_, d_qp_, d_kv, d_kpe2 = _attention_bwd(q, kv, kpe2, cos2, sin2, d_attn)
    d_q = jnp.concatenate([d_qn_, d_qp_], axis=1)
    d_qn = _matmul([(d_q, w["w_uq"])], "nt", BF16, "q_up_dx")
    dw_uq = _matmul([(qn, d_q)], "tn", BF16, "q_up_dw")
    d_kvn = _matmul([(d_kv, w["w_ukv"])], "nt", BF16, "kv_up_dx")
    dw_ukv = _matmul([(kvn, d_kv)], "tn", BF16, "kv_up_dw")

    def latents_bwd(z, dq, dk, dkp, c2, s2, qg, kvg):
        z = z.astype(F32)
        _, qhat, rq = _rms_fwd(z[:, :Q_LORA], qg)
        d_ql, dqg = _rms_bwd(dq.astype(F32), qhat, rq, qg)
        _, khat, rk = _rms_fwd(z[:, Q_LORA:Q_LORA + KV_LORA], kvg)
        d_kl, dkg = _rms_bwd(dk.astype(F32), khat, rk, kvg)
        both = dkp + pltpu.roll(dkp, QK_ROPE, 1)
        d_kr = _rope_transposed(both, c2, s2) * _half_mask(both.shape, 0)
        return jnp.concatenate([d_ql, d_kl, d_kr], axis=1), dqg, dkg

    d_z2, g_q_norm, g_kv_norm = _rowwise(latents_bwd, [z2, d_qn, d_kvn, d_kpe2, cos2, sin2], [vec["q_norm"], vec["kv_norm"]],
                                         [(Z2_COLS, BF16)], [Q_LORA, KV_LORA], "latents_bwd")
    d_h = _matmul([(d_ag, w["w_ag"]), (d_z2, w["w_z2"])], "nt", F32, "in_proj_dx")
    dw_ag = _matmul([(h, d_ag)], "tn", BF16, "in_proj_ag_dw")
    dw_z2 = _matmul([(h, d_z2)], "tn", BF16, "in_proj_z2_dw")

    def pre_mix_bwd(dx1, dh, xv, g):
        _, xhat, r = _rms_fwd(xv, g)
        dn, dg = _rms_bwd(dh, xhat, r, g)
        return dx1 + dn, dg

    grad_x, g_pre_mix = _rowwise(pre_mix_bwd, [d_x1, d_h, x], [vec["pre_mix_norm"]], [(d, F32)], [d], "pre_mix_norm_bwd")

    dw = dict(w_ag=dw_ag, w_z2=dw_z2, w_uq=dw_uq, w_ukv=dw_ukv, conv_w=d_conv_w, w_out=dw_out, w_gu=dw_gu, w_down=dw_down)
    dvec = dict(pre_mix_norm=g_pre_mix, q_norm=g_q_norm, kv_norm=g_kv_norm, conv_b=g_conv_b, conv_ln_g=g_ln_g,
                conv_ln_b=g_ln_b, conv_out_norm=g_conv_out, attn_out_norm=g_attn_out, post_mix_norm=g_post_mix,
                pre_ffn_norm=g_pre_ffn, post_ffn_norm=g_post_ffn)
    return loss, grad_x, dw, dvec


ANY = pl.BlockSpec(memory_space=pl.ANY)


def _place():
    x, y, c = lax.axis_index("x"), lax.axis_index("y"), lax.axis_index("c")
    chips = [(1 - x, y), (x, 1 - y), (1 - x, 1 - y)]
    return x, y, c, chips


def _to_parts(chip, pieces, dtype, name):
    r = pieces[0].shape[0]
    widths = [a.shape[1] for a in pieces]
    tr = r if r <= 512 else _first_divisor(r, (512, 256, 128))

    def body(p_ref, *refs):
        o_ref = refs[len(pieces)]
        off = 0
        for a_ref, wdt in zip(refs, widths):
            o_ref[:, off:off + wdt] = a_ref[...].astype(o_ref.dtype)
            off += wdt

    return pl.pallas_call(
        body,
        name=name,
        grid_spec=pltpu.PrefetchScalarGridSpec(
            num_scalar_prefetch=1,
            grid=(r // tr,),
            in_specs=[pl.BlockSpec((tr, wdt), lambda i, p_ref: (i, 0)) for wdt in widths],
            out_specs=pl.BlockSpec((None, tr, sum(widths)), lambda i, p_ref: (p_ref[0], i, 0))),
        out_shape=jax.ShapeDtypeStruct((N_CHIPS, r, sum(widths)), dtype),
        compiler_params=_params(("parallel",)),
    )(chip, *pieces)


HBM = pl.BlockSpec(memory_space=pltpu.HBM)
SEM = pl.BlockSpec(memory_space=pltpu.SEMAPHORE)
EFFECT = pltpu.SideEffectType.DATAFLOW_SIDE_EFFECTING


def _in_hbm(a):
    return pltpu.with_memory_space_constraint(a, pltpu.HBM)


def _gather_rows(buf, whole, half):
    r = buf.shape[1]
    return pl.ds(0, r) if whole else pl.ds(half * (r // 2), r // 2)


def _gather_start(bufs, whole, groups, name):
    n = len(bufs)
    n_g = len(groups)

    def body(*refs):
        sems = refs[n:n + 2 * n_g]
        outs = refs[n + 2 * n_g:2 * n + 2 * n_g]
        x, y, c, chips = _place()
        p = 2 * x + y
        for gi, group in enumerate(groups):
            for ki, k in enumerate(group):
                blk = outs[k].at[p, _gather_rows(bufs[k], whole[k], c), :]
                for j, (px, py) in enumerate(chips):
                    pltpu.make_async_remote_copy(src_ref=blk, dst_ref=blk, send_sem=sems[2 * gi].at[3 * ki + j],
                                                 recv_sem=sems[2 * gi + 1].at[3 * ki + j],
                                                 device_id=(px, py, c), device_id_type=MESH).start()

    sem_shapes = []
    for group in groups:
        sem_shapes += [pltpu.SemaphoreType.DMA((3 * len(group),))] * 2
    res = pl.pallas_call(
        body,
        name=name,
        in_specs=[HBM] * n,
        out_specs=[SEM] * (2 * n_g) + [HBM] * n,
        out_shape=sem_shapes + [pltpu.HBM(a.shape, a.dtype) for a in bufs],
        input_output_aliases={k: 2 * n_g + k for k in range(n)},
        compiler_params=pltpu.CompilerParams(has_side_effects=EFFECT),
    )(*[_in_hbm(a) for a in bufs])
    sems = [(res[2 * gi], res[2 * gi + 1]) for gi in range(n_g)]
    return sems, list(res[2 * n_g:2 * n_g + n])


def _gather_wait(bufs, whole, send, recv, after, name):
    n = len(bufs)

    def body(*refs):
        ins = refs[:n]
        send_ref, recv_ref = refs[n], refs[n + 1]
        x, y, c, chips = _place()
        p = 2 * x + y
        for ki in range(n):
            rows = _gather_rows(bufs[ki], whole[ki], c)
            for j, (px, py) in enumerate(chips):
                cp = pltpu.make_async_remote_copy(src_ref=ins[ki].at[p, rows, :], dst_ref=ins[ki].at[2 * px + py, rows, :],
                                                  send_sem=send_ref.at[3 * ki + j], recv_sem=recv_ref.at[3 * ki + j],
                                                  device_id=(px, py, c), device_id_type=MESH)
                cp.wait_send()
                cp.wait_recv()

    res = pl.pallas_call(
        body,
        name=name,
        in_specs=[HBM] * n + [SEM, SEM, ANY],
        out_specs=[HBM] * n,
        out_shape=[pltpu.HBM(a.shape, a.dtype) for a in bufs],
        input_output_aliases={k: k for k in range(n)},
        compiler_params=pltpu.CompilerParams(has_side_effects=EFFECT),
    )(*bufs, send, recv, after)
    return list(res)


def _gather_hand_on(bufs, name):
    n = len(bufs)

    def body(*refs):
        outs = refs[n:2 * n]
        send, recv = refs[2 * n:]
        x, y, c, chips = _place()

        def d2d(k, j, half):
            px, py = chips[j]
            blk = outs[k].at[2 * px + py, _gather_rows(bufs[k], False, half), :]
            return pltpu.make_async_remote_copy(src_ref=blk, dst_ref=blk, send_sem=send.at[3 * k + j], recv_sem=recv.at[3 * k + j],
                                                device_id=(x, y, 1 - c), device_id_type=MESH)

        sent = [d2d(k, j, c) for k in range(n) for j in range(3)]
        for cp in sent:
            cp.start()
        for k in range(n):
            for j in range(3):
                d2d(k, j, 1 - c).wait_recv()
        for cp in sent:
            cp.wait_send()

    return pl.pallas_call(
        body,
        name=name,
        in_specs=[ANY] * n,
        out_specs=[ANY] * n,
        out_shape=[jax.ShapeDtypeStruct(a.shape, a.dtype) for a in bufs],
        input_output_aliases={k: k for k in range(n)},
        scratch_shapes=[pltpu.SemaphoreType.DMA((3 * n,)), pltpu.SemaphoreType.DMA((3 * n,))],
        compiler_params=pltpu.CompilerParams(has_side_effects=True),
    )(*bufs)


def _hand_on_start(bufs, name):
    n = len(bufs)

    def body(*refs):
        send, recv = refs[n], refs[n + 1]
        outs = refs[n + 2:]
        x, y, c, chips = _place()
        for k in range(n):
            for j, (px, py) in enumerate(chips):
                blk = outs[k].at[2 * px + py, _gather_rows(bufs[k], False, c), :]
                pltpu.make_async_remote_copy(src_ref=blk, dst_ref=blk, send_sem=send.at[3 * k + j], recv_sem=recv.at[3 * k + j],
                                             device_id=(x, y, 1 - c), device_id_type=MESH).start()

    res = pl.pallas_call(
        body,
        name=name,
        in_specs=[HBM] * n,
        out_specs=[SEM, SEM] + [HBM] * n,
        out_shape=[pltpu.SemaphoreType.DMA((3 * n,))] * 2 + [pltpu.HBM(a.shape, a.dtype) for a in bufs],
        input_output_aliases={k: 2 + k for k in range(n)},
        compiler_params=pltpu.CompilerParams(has_side_effects=EFFECT),
    )(*[_in_hbm(a) for a in bufs])
    return res[0], res[1], list(res[2:])


def _hand_on_wait(bufs, send, recv, after, name):
    n = len(bufs)

    def body(*refs):
        ins = refs[:n]
        send_ref, recv_ref = refs[n], refs[n + 1]
        x, y, c, chips = _place()
        for k in range(n):
            for j, (px, py) in enumerate(chips):
                q = 2 * px + py
                cp = pltpu.make_async_remote_copy(src_ref=ins[k].at[q, _gather_rows(bufs[k], False, c), :],
                                                  dst_ref=ins[k].at[q, _gather_rows(bufs[k], False, 1 - c), :],
                                                  send_sem=send_ref.at[3 * k + j], recv_sem=recv_ref.at[3 * k + j],
                                                  device_id=(x, y, 1 - c), device_id_type=MESH)
                cp.wait_send()
                cp.wait_recv()

    res = pl.pallas_call(
        body,
        name=name,
        in_specs=[HBM] * n + [SEM, SEM, ANY],
        out_specs=[HBM] * n,
        out_shape=[pltpu.HBM(a.shape, a.dtype) for a in bufs],
        input_output_aliases={k: k for k in range(n)},
        compiler_params=pltpu.CompilerParams(has_side_effects=EFFECT),
    )(*bufs, send, recv, after)
    return list(res)


def _pair_exchange_start(part, name):
    rh = part.shape[1] // 2
    land_shape = (N_CHIPS, rh, part.shape[2])

    def body(part_ref, land_ref, send, recv, part_out, land_out):
        x, y, c, _ = _place()
        pltpu.make_async_remote_copy(src_ref=part_out.at[:, pl.ds((1 - c) * rh, rh), :], dst_ref=land_out,
                                     send_sem=send, recv_sem=recv, device_id=(x, y, 1 - c), device_id_type=MESH).start()

    res = pl.pallas_call(
        body,
        name=name,
        in_specs=[HBM, HBM],
        out_specs=[SEM, SEM, HBM, HBM],
        out_shape=[pltpu.SemaphoreType.DMA(()), pltpu.SemaphoreType.DMA(()), pltpu.HBM(part.shape, part.dtype),
                   pltpu.HBM(land_shape, part.dtype)],
        input_output_aliases={0: 2, 1: 3},
        compiler_params=pltpu.CompilerParams(has_side_effects=EFFECT),
    )(_in_hbm(part), _in_hbm(lax.empty(land_shape, part.dtype)))
    return res


def _pair_exchange_wait(send, recv, part, land, after, name):
    rh = part.shape[1] // 2

    def body(part_ref, land_ref, send_ref, recv_ref, after_ref, part_out, land_out):
        x, y, c, _ = _place()
        cp = pltpu.make_async_remote_copy(src_ref=part_ref.at[:, pl.ds((1 - c) * rh, rh), :], dst_ref=land_ref,
                                          send_sem=send_ref, recv_sem=recv_ref, device_id=(x, y, 1 - c), device_id_type=MESH)
        cp.wait_send()
        cp.wait_recv()

    return pl.pallas_call(
        body,
        name=name,
        in_specs=[HBM, HBM, SEM, SEM, ANY],
        out_specs=[HBM, HBM],
        out_shape=[pltpu.HBM(part.shape, part.dtype), pltpu.HBM(land.shape, land.dtype)],
        input_output_aliases={0: 0, 1: 1},
        compiler_params=pltpu.CompilerParams(has_side_effects=EFFECT),
    )(part, land, send, recv, after)


def _chip_exchange_start(sums, name):
    n = len(sums)

    def body(*refs):
        send, recv = refs[2 * n], refs[2 * n + 1]
        src = refs[2 * n + 2:3 * n + 2]
        dst = refs[3 * n + 2:4 * n + 2]
        x, y, c, chips = _place()
        p = 2 * x + y
        for k in range(n):
            for j, (px, py) in enumerate(chips):
                pltpu.make_async_remote_copy(src_ref=src[k].at[2 * px + py], dst_ref=dst[k].at[p],
                                             send_sem=send.at[3 * k + j], recv_sem=recv.at[3 * k + j],
                                             device_id=(px, py, c), device_id_type=MESH).start()

    res = pl.pallas_call(
        body,
        name=name,
        in_specs=[HBM] * (2 * n),
        out_specs=[SEM, SEM] + [HBM] * (2 * n),
        out_shape=[pltpu.SemaphoreType.DMA((3 * n,))] * 2 + [pltpu.HBM(a.shape, a.dtype) for a in sums] * 2,
        input_output_aliases={k: 2 + k for k in range(2 * n)},
        compiler_params=pltpu.CompilerParams(has_side_effects=EFFECT),
    )(*[_in_hbm(a) for a in sums], *[_in_hbm(lax.empty(a.shape, a.dtype)) for a in sums])
    return res[0], res[1], list(res[2:2 + n]), list(res[2 + n:2 + 2 * n])


def _chip_exchange_wait(sums, slots, send, recv, after, name):
    n = len(sums)

    def body(*refs):
        src, dst = refs[:n], refs[n:2 * n]
        send_ref, recv_ref = refs[2 * n], refs[2 * n + 1]
        x, y, c, chips = _place()
        for k in range(n):
            for j, (px, py) in enumerate(chips):
                cp = pltpu.make_async_remote_copy(src_ref=src[k].at[2 * px + py], dst_ref=dst[k].at[2 * px + py],
                                                  send_sem=send_ref.at[3 * k + j], recv_sem=recv_ref.at[3 * k + j],
                                                  device_id=(px, py, c), device_id_type=MESH)
                cp.wait_send()
                cp.wait_recv()

    res = pl.pallas_call(
        body,
        name=name,
        in_specs=[HBM] * (2 * n) + [SEM, SEM] + [ANY] * len(after),
        out_specs=[HBM] * (2 * n),
        out_shape=[pltpu.HBM(a.shape, a.dtype) for a in sums] * 2,
        input_output_aliases={k: k for k in range(2 * n)},
        compiler_params=pltpu.CompilerParams(has_side_effects=EFFECT),
    )(*sums, *slots, send, recv, *after)
    return list(res[:n]), list(res[n:])


def _pair_exchange(parts, name):
    n = len(parts)

    def body(*refs):
        ins, outs = refs[:n], refs[n:2 * n]
        send, recv = refs[2 * n:]
        x, y, c, _ = _place()
        copies = []
        for k in range(n):
            rh = parts[k].shape[1] // 2
            cp = pltpu.make_async_remote_copy(
                src_ref=ins[k].at[:, pl.ds((1 - c) * rh, rh), :], dst_ref=outs[k],
                send_sem=send.at[k], recv_sem=recv.at[k], device_id=(x, y, 1 - c), device_id_type=MESH)
            cp.start()
            copies.append(cp)
        for cp in copies:
            cp.wait()

    return pl.pallas_call(
        body,
        name=name,
        in_specs=[ANY] * n,
        out_specs=[ANY] * n,
        out_shape=[jax.ShapeDtypeStruct((N_CHIPS, a.shape[1] // 2, a.shape[2]), a.dtype) for a in parts],
        scratch_shapes=[pltpu.SemaphoreType.DMA((n,)), pltpu.SemaphoreType.DMA((n,))],
        compiler_params=pltpu.CompilerParams(has_side_effects=True),
    )(*parts)


def _pair_sum(core, part, landed, name):
    _, r, cdim = part.shape
    rh = r // 2
    tr = _first_divisor(rh, (256, 128, 64, 32, 16))
    nb = rh // tr

    def body(c_ref, a_ref, b_ref, o_ref):
        o_ref[...] = (a_ref[...].astype(F32) + b_ref[...].astype(F32)).astype(o_ref.dtype)

    return pl.pallas_call(
        body,
        name=name,
        grid_spec=pltpu.PrefetchScalarGridSpec(
            num_scalar_prefetch=1,
            grid=(N_CHIPS, nb),
            in_specs=[pl.BlockSpec((None, tr, cdim), lambda q, i, c_ref: (q, c_ref[0] * nb + i, 0)),
                      pl.BlockSpec((None, tr, cdim), lambda q, i, c_ref: (q, i, 0))],
            out_specs=pl.BlockSpec((None, tr, cdim), lambda q, i, c_ref: (q, i, 0))),
        out_shape=jax.ShapeDtypeStruct((N_CHIPS, rh, cdim), BF16),
        compiler_params=_params(("parallel", "parallel")),
    )(core, part, landed)


def _chip_sum(place, own, slots, name):
    _, rh, cdim = slots.shape
    tr = _first_divisor(rh, (256, 128, 64, 32, 16))
    nb = rh // tr

    def body(place_ref, own_ref, s1_ref, s2_ref, s3_ref, o_ref):
        acc = own_ref[...].astype(F32)
        for s_ref in (s1_ref, s2_ref, s3_ref):
            acc = acc + s_ref[...].astype(F32)
        o_ref[...] = acc

    def other(j):
        return lambda i, place_ref: ((place_ref[0] + j) % N_CHIPS, i, 0)

    return pl.pallas_call(
        body,
        name=name,
        grid_spec=pltpu.PrefetchScalarGridSpec(
            num_scalar_prefetch=1,
            grid=(nb,),
            in_specs=[pl.BlockSpec((None, tr, cdim), other(0))] + [pl.BlockSpec((None, tr, cdim), other(j)) for j in (1, 2, 3)],
            out_specs=pl.BlockSpec((tr, cdim), lambda i, place_ref: (place_ref[1] * nb + i, 0))),
        out_shape=jax.ShapeDtypeStruct((2 * rh, cdim), F32),
        compiler_params=_params(("parallel",)),
    )(place, own, slots, slots, slots)


def _half_exchange(bufs, name):
    n = len(bufs)

    def body(*refs):
        outs = refs[n:2 * n]
        send, recv = refs[2 * n:]
        x, y, c, _ = _place()
        copies = []
        for k in range(n):
            rh = bufs[k].shape[0] // 2
            mine = outs[k].at[pl.ds(c * rh, rh), :]
            cp = pltpu.make_async_remote_copy(src_ref=mine, dst_ref=mine, send_sem=send.at[k], recv_sem=recv.at[k],
                                              device_id=(x, y, 1 - c), device_id_type=MESH)
            cp.start()
            copies.append(cp)
        for k in range(n):
            rh = bufs[k].shape[0] // 2
            theirs = outs[k].at[pl.ds((1 - c) * rh, rh), :]
            copies[k].wait_send()
            pltpu.make_async_remote_copy(src_ref=theirs, dst_ref=theirs, send_sem=send.at[k], recv_sem=recv.at[k],
                                         device_id=(x, y, 1 - c), device_id_type=MESH).wait_recv()

    return pl.pallas_call(
        body,
        name=name,
        in_specs=[ANY] * n,
        out_specs=[ANY] * n,
        out_shape=[jax.ShapeDtypeStruct(a.shape, a.dtype) for a in bufs],
        input_output_aliases={k: k for k in range(n)},
        scratch_shapes=[pltpu.SemaphoreType.DMA((n,)), pltpu.SemaphoreType.DMA((n,))],
        compiler_params=pltpu.CompilerParams(has_side_effects=True),
    )(*bufs)


SMALL_ROWS = 32


def _all_reduce_small(pack):
    def body(in_ref, out_ref, gath, send, recv):
        x, y, c, _ = _place()
        me = 4 * x + 2 * y + c
        gath[me] = in_ref[...]
        copies = []
        for k in range(1, N_DEV):
            dx, dy, dc = (k >> 2) & 1, (k >> 1) & 1, k & 1
            peer = (x ^ dx, y ^ dy, c ^ dc)
            cp = pltpu.make_async_remote_copy(src_ref=in_ref, dst_ref=gath.at[me], send_sem=send.at[k], recv_sem=recv.at[k],
                                              device_id=peer, device_id_type=MESH)
            cp.start()
            copies.append((cp, 4 * peer[0] + 2 * peer[1] + peer[2]))
        for k, (cp, peer_id) in enumerate(copies, start=1):
            cp.wait_send()
            pltpu.make_async_remote_copy(src_ref=in_ref, dst_ref=gath.at[peer_id], send_sem=send.at[k], recv_sem=recv.at[k],
                                         device_id=(x, y, c), device_id_type=MESH).wait_recv()
        acc = gath[0]
        for dev in range(1, N_DEV):
            acc = acc + gath[dev]
        out_ref[...] = acc

    return pl.pallas_call(
        body,
        name="all_reduce_small",
        in_specs=[pl.BlockSpec(memory_space=pltpu.VMEM)],
        out_specs=pl.BlockSpec(memory_space=pltpu.VMEM),
        out_shape=jax.ShapeDtypeStruct(pack.shape, F32),
        scratch_shapes=[pltpu.VMEM((N_DEV,) + pack.shape, F32), pltpu.SemaphoreType.DMA((N_DEV,)), pltpu.SemaphoreType.DMA((N_DEV,))],
        compiler_params=pltpu.CompilerParams(has_side_effects=True, vmem_limit_bytes=VMEM_LIMIT),
    )(pack)


def _adamw_update(g_ref, w_ref, m_ref, v_ref, go_ref, d_ref, mo_ref, vo_ref):
    bc1 = 1.0 - ADAM_B1 ** ADAM_STEP
    bc2 = 1.0 - ADAM_B2 ** ADAM_STEP
    gv = g_ref[...]
    mn = ADAM_B1 * m_ref[...] + (1.0 - ADAM_B1) * gv
    vn = ADAM_B2 * v_ref[...] + (1.0 - ADAM_B2) * (gv * gv)
    go_ref[...] = gv
    mo_ref[...] = mn
    vo_ref[...] = vn
    d_ref[...] = -ADAM_LR * ((mn / bc1) / (jnp.sqrt(vn / bc2) + ADAM_EPS) + ADAM_WD * w_ref[...])


def _adamw_small(gs, ws, ms, vs):
    n = len(gs)

    def body(*refs):
        ins, outs = refs[:4 * n], refs[4 * n:]
        for k in range(n):
            _adamw_update(*[ins[i * n + k] for i in range(4)], *outs[4 * k:4 * k + 4])

    vmem = pl.BlockSpec(memory_space=pltpu.VMEM)
    res = pl.pallas_call(
        body,
        name="adamw_small",
        in_specs=[vmem] * (4 * n),
        out_specs=[vmem] * (4 * n),
        out_shape=[jax.ShapeDtypeStruct(w.shape, F32) for w in ws for _ in range(4)],
        compiler_params=pltpu.CompilerParams(vmem_limit_bytes=VMEM_LIMIT),
    )(*gs, *ws, *ms, *vs)
    return [tuple(res[4 * k:4 * k + 4]) for k in range(n)]


def _adamw(g, w, m, v, name, g_block=0):
    r, cdim = w.shape
    tr = r if r * cdim * 4 <= (1 << 20) else _first_divisor(r, (128, 64, 32, 16, 8))

    def body(*refs):
        _adamw_update(*refs)

    spec = pl.BlockSpec((tr, cdim), lambda i: (i, 0))
    return pl.pallas_call(
        body,
        name=name,
        grid=(r // tr,),
        in_specs=[pl.BlockSpec((tr, cdim), lambda i: (i, g_block))] + [spec] * 3,
        out_specs=[spec] * 4,
        out_shape=[jax.ShapeDtypeStruct((r, cdim), F32)] * 4,
        compiler_params=_params(("parallel",)),
    )(g, w, m, v)


VEC_NAMES = ["pre_mix_norm", "q_norm", "kv_norm", "conv_b", "conv_ln_g", "conv_ln_b", "conv_out_norm",
             "attn_out_norm", "post_mix_norm", "pre_ffn_norm", "post_ffn_norm"]
LOSS_ROW = len(VEC_NAMES)
CONV_W_ROW = 16


def _cols_to_full(parts):
    _, r, cdim = parts.shape
    return parts.transpose(1, 0, 2).reshape(r, N_CHIPS * cdim)


def _full_to_cols(full):
    r, n = full.shape
    return full.reshape(r, N_CHIPS, n // N_CHIPS).transpose(1, 0, 2)


W_IN_SHARD = (2 * CONV_CH + Q_LORA + KV_LORA + QK_ROPE) // N_CHIPS
W_IN_PART = 1024
W_IN_BLOCKS = (2 * CONV_CH + Z2_COLS) // LANES
W_IN_BASE = [p * W_IN_SHARD // LANES for p in range(N_CHIPS)]
W_IN_SPAN = [-(-(p * W_IN_SHARD % LANES + W_IN_SHARD) // LANES) for p in range(N_CHIPS)]


def _w_in_block_home(b):
    n = CONV_CH // LANES
    if b < n:
        return 0, 2 * b
    if b < 2 * n:
        return 0, 2 * (b - n) + 1
    return 1, b - 2 * n


def _to_parts_w_in(shift_chip, w_in):
    r = w_in.shape[0]
    tr = 512

    def body(s_ref, a_ref, o_ref):
        o_ref[...] = jnp.zeros_like(o_ref)
        o_ref[:, :W_IN_SHARD] = a_ref[...].astype(o_ref.dtype)
        o_ref[...] = pltpu.roll(o_ref[...].astype(F32), s_ref[0], 1).astype(o_ref.dtype)

    return pl.pallas_call(
        body,
        name="to_parts_w_in",
        grid_spec=pltpu.PrefetchScalarGridSpec(
            num_scalar_prefetch=1,
            grid=(r // tr,),
            in_specs=[pl.BlockSpec((tr, W_IN_SHARD), lambda i, s_ref: (i, 0))],
            out_specs=pl.BlockSpec((None, tr, W_IN_PART), lambda i, s_ref: (s_ref[1], i, 0))),
        out_shape=jax.ShapeDtypeStruct((N_CHIPS, r, W_IN_PART), BF16),
        compiler_params=_params(("parallel",)),
    )(shift_chip, w_in)


def _assemble_w_in(parts):
    r = parts.shape[1]
    tr = ROW_TILE

    def body(p_ref, ag_ref, z2_ref):
        outs = (ag_ref, z2_ref)
        for b in range(W_IN_BLOCKS):
            blk = None
            for p in range(N_CHIPS):
                i = b - W_IN_BASE[p]
                if 0 <= i < W_IN_SPAN[p]:
                    piece = p_ref[p, :, i * LANES:(i + 1) * LANES]
                    blk = piece if blk is None else blk + piece
            which, at = _w_in_block_home(b)
            outs[which][:, at * LANES:(at + 1) * LANES] = blk

    w_ag, w_z2 = pl.pallas_call(
        body,
        name="assemble_w_in",
        grid=(r // tr,),
        in_specs=[pl.BlockSpec((N_CHIPS, tr, W_IN_PART), lambda i: (0, i, 0))],
        out_specs=[pl.BlockSpec((tr, 2 * CONV_CH), lambda i: (i, 0)), pl.BlockSpec((tr, Z2_COLS), lambda i: (i, 0))],
        out_shape=[jax.ShapeDtypeStruct((r, 2 * CONV_CH), parts.dtype), jax.ShapeDtypeStruct((r, Z2_COLS), parts.dtype)],
        compiler_params=_params(("parallel",)),
    )(parts)
    return dict(w_ag=w_ag, w_z2=w_z2)


def _w_in_grad_parts(dw_ag, dw_z2):
    r = dw_ag.shape[0]
    tr = ROW_TILE

    def body(ag_ref, z2_ref, o_ref):
        ins = (ag_ref, z2_ref)
        for p in range(N_CHIPS):
            for i in range(W_IN_PART // LANES):
                if i < W_IN_SPAN[p]:
                    which, at = _w_in_block_home(W_IN_BASE[p] + i)
                    o_ref[p, :, i * LANES:(i + 1) * LANES] = ins[which][:, at * LANES:(at + 1) * LANES]
                else:
                    o_ref[p, :, i * LANES:(i + 1) * LANES] = jnp.zeros((tr, LANES), o_ref.dtype)

    return pl.pallas_call(
        body,
        name="w_in_grad_parts",
        grid=(r // tr,),
        in_specs=[pl.BlockSpec((tr, 2 * CONV_CH), lambda i: (i, 0)), pl.BlockSpec((tr, Z2_COLS), lambda i: (i, 0))],
        out_specs=pl.BlockSpec((N_CHIPS, tr, W_IN_PART), lambda i: (0, i, 0)),
        out_shape=jax.ShapeDtypeStruct((N_CHIPS, r, W_IN_PART), dw_ag.dtype),
        compiler_params=_params(("parallel",)),
    )(dw_ag, dw_z2)


def _assemble_mixer_rest(g):
    uq = _cols_to_full(g["w_uq"]).reshape(Q_LORA, N_HEADS, QK_HEAD)
    w_uq = jnp.concatenate([uq[:, :, :QK_NOPE].reshape(Q_LORA, N_HEADS * QK_NOPE),
                            uq[:, :, QK_NOPE:].reshape(Q_LORA, N_HEADS * QK_ROPE)], axis=1)
    return dict(w_uq=w_uq, w_ukv=_cols_to_full(g["w_ukv"]), conv_w=_cols_to_full(g["conv_w"]),
                w_out=g["w_out"].reshape(-1, g["w_out"].shape[2]))


def _grads_to_parts(dw):
    uq = dw["w_uq"]
    d_uq = jnp.concatenate([uq[:, :N_HEADS * QK_NOPE].reshape(Q_LORA, N_HEADS, QK_NOPE),
                            uq[:, N_HEADS * QK_NOPE:].reshape(Q_LORA, N_HEADS, QK_ROPE)], axis=2).reshape(Q_LORA, N_HEADS * QK_HEAD)
    return dict(w_in=_w_in_grad_parts(dw["w_ag"], dw["w_z2"]), w_uq=_full_to_cols(d_uq), w_ukv=_full_to_cols(dw["w_ukv"]),
                w_out=dw["w_out"].reshape(N_CHIPS, -1, dw["w_out"].shape[1]))


MIXER = ["w_in", "w_uq", "w_ukv", "w_out"]
FFN = ["w_gu", "w_down"]
BIG = MIXER + FFN


def _pad_lanes(v, n):
    return jnp.pad(v, ((0, 0), (0, n - v.shape[1])))


def kernel(x, positions, pre_mix_norm, w_in, q_norm, w_uq, kv_norm, w_ukv, conv_w, conv_b, conv_ln_g, conv_ln_b, conv_out_norm, attn_out_norm, w_out, post_mix_norm, pre_ffn_norm, w_gate, w_up, w_down, post_ffn_norm, loss_target, m_pre_mix_norm, m_w_in, m_q_norm, m_w_uq, m_kv_norm, m_w_ukv, m_conv_w, m_conv_b, m_conv_ln_g, m_conv_ln_b, m_conv_out_norm, m_attn_out_norm, m_w_out, m_post_mix_norm, m_pre_ffn_norm, m_w_gate, m_w_up, m_w_down, m_post_ffn_norm, v_pre_mix_norm, v_w_in, v_q_norm, v_w_uq, v_kv_norm, v_w_ukv, v_conv_w, v_conv_b, v_conv_ln_g, v_conv_ln_b, v_conv_out_norm, v_attn_out_norm, v_w_out, v_post_mix_norm, v_pre_ffn_norm, v_w_gate, v_w_up, v_w_down, v_post_ffn_norm):
    given = dict(locals())
    names = ["pre_mix_norm", "w_in", "q_norm", "w_uq", "kv_norm", "w_ukv", "conv_w", "conv_b", "conv_ln_g", "conv_ln_b",
             "conv_out_norm", "attn_out_norm", "w_out", "post_mix_norm", "pre_ffn_norm", "w_gate", "w_up", "w_down", "post_ffn_norm"]
    def as_2d(a):
        return a if a.ndim == 2 else a[0]

    weights = {n: as_2d(given[n]) for n in names}
    mom = {n: as_2d(given["m_" + n]) for n in names}
    var = {n: as_2d(given["v_" + n]) for n in names}
    d = D_MODEL

    inv_freq = ROPE_THETA ** (-jnp.arange(0, QK_ROPE, 2, dtype=F32) / QK_ROPE)
    ang = positions[0].astype(F32)[:, None] * inv_freq
    cos, sin = jnp.cos(ang), jnp.sin(ang)
    cos2 = jnp.concatenate([cos, cos, cos, cos], axis=1)
    sin2 = jnp.concatenate([-sin, sin, -sin, sin], axis=1)

    chip = 2 * lax.axis_index("x") + lax.axis_index("y")
    core = lax.axis_index("c")
    chip1 = chip.astype(jnp.int32).reshape(1)
    pieces = {n: [weights[n]] for n in BIG if n != "w_gu"}
    pieces["w_gu"] = [weights["w_gate"], weights["w_up"]]
    core1 = core.astype(jnp.int32).reshape(1)
    place = jnp.stack([chip, core]).astype(jnp.int32)
    rest = ["w_uq", "w_ukv", "w_out"]
    w_in_shift = (chip * W_IN_SHARD) % LANES
    mix_bufs = [_to_parts_w_in(jnp.stack([w_in_shift, chip]).astype(jnp.int32), weights["w_in"])]
    mix_bufs += [_to_parts(chip1, pieces[n], BF16, "to_parts_" + n) for n in rest]
    mix_bufs.append(_to_parts(chip1, [jnp.pad(weights["conv_w"], ((0, CONV_K_PAD - CONV_K), (0, 0)))], F32, "to_parts_conv_w"))
    mix_whole = [False] * 4 + [True]
    mix_sems, mix_thru = _gather_start(mix_bufs, mix_whole, [[0], [1, 2, 3, 4]], "gather_start_mixer")
    ffn_bufs = [_to_parts(chip1, pieces[n], BF16, "to_parts_" + n) for n in FFN]
    ffn_sems, ffn_thru = _gather_start(ffn_bufs, [False, False], [[0], [1]], "gather_start_ffn")
    got = _gather_wait(mix_thru[:1], [False], *mix_sems[0], cos2, "gather_wait_w_in")
    full = _assemble_w_in(_gather_hand_on(got, "gather_hand_on_w_in")[0])
    vec = {n: weights[n] for n in VEC_NAMES}
    rs = {}

    def get_mixer_rest(after):
        got = _gather_wait(mix_thru[1:], mix_whole[1:], *mix_sems[1], after, "gather_wait_mixer_rest")
        got = list(_gather_hand_on(got[:3], "gather_hand_on_mixer_rest")) + [got[3]]
        return _assemble_mixer_rest(dict(zip(rest + ["conv_w"], got)))

    def ffn_landed(k, key):
        def hook(after):
            got = _gather_wait(ffn_thru[k:k + 1], [False], *ffn_sems[k], after, "gather_wait_" + key)
            rs[key] = _hand_on_start(got, "hand_on_start_" + key)
            return rs[key][2][0]
        return hook

    def ffn_weight(key):
        def hook(after):
            send, recv, bufs = rs[key]
            return _hand_on_wait(bufs, send, recv, after, "hand_on_wait_" + key)[0]
        return hook

    def get_w_down(after):
        got = _gather_wait(ffn_thru[1:], [False], *ffn_sems[1], after, "gather_wait_w_down")
        got = _gather_hand_on(got, "gather_hand_on_w_down")[0]
        return got.reshape(-1, got.shape[2])

    def pair_start(key):
        def hook(dw):
            rs[key] = _pair_exchange_start(dw.reshape(N_CHIPS, -1, dw.shape[-1]), "grad_pair_start_" + key)
            return rs[key][2]
        return hook

    def reduce_start(group, plist, landed):
        sums = [_pair_sum(core1, a, b, "grad_pair_sum_%s_%d" % (group, k)) for k, (a, b) in enumerate(zip(plist, landed))]
        rs[group] = _chip_exchange_start(sums, "grad_chip_exchange_start_" + group)
        return rs[group][2][0]

    def reduce_finish(group, after):
        send, recv, sums, slots = rs[group]
        sums, slots = _chip_exchange_wait(sums, slots, send, recv, after, "grad_chip_exchange_wait_" + group)
        halves = [_chip_sum(place, s, sl, "grad_chip_sum_%s_%d" % (group, k)) for k, (s, sl) in enumerate(zip(sums, slots))]
        return list(_half_exchange(halves, "grad_half_exchange_" + group))

    def ffn_grads_exchanged(after):
        pairs = [_pair_exchange_wait(*rs[key], after, "grad_pair_wait_" + key) for key in ("dw_gu", "dw_down")]
        return reduce_start("ffn", [p[0] for p in pairs], [p[1] for p in pairs])

    hooks = dict(mixer_rest=get_mixer_rest, w_gu_landed=ffn_landed(0, "w_gu"), w_gu=ffn_weight("w_gu"), w_down=get_w_down, dw_down=pair_start("dw_down"), dw_gu=pair_start("dw_gu"),
                 grads_exchanged=ffn_grads_exchanged)
    loss, grad_x, dw, dvec = _local_step(x[0], loss_target[0], cos2, sin2, vec, full, hooks)

    rows = [_pad_lanes(dvec[n], d) for n in VEC_NAMES] + [_pad_lanes(loss, d)]
    rows.append(jnp.zeros((CONV_W_ROW - len(rows), d), F32))
    rows.append(dw["conv_w"].reshape(SMALL_ROWS - CONV_W_ROW, d))
    small = _all_reduce_small(jnp.concatenate(rows, axis=0))
    g_conv_w_full = small[CONV_W_ROW:].reshape(CONV_K_PAD, CONV_CH)
    g_small = {n: small[i:i + 1, :weights[n].shape[1]] for i, n in enumerate(VEC_NAMES)}
    g_small["conv_w"] = lax.dynamic_slice(g_conv_w_full, (0, chip * (CONV_CH // N_CHIPS)), (CONV_K_PAD, CONV_CH // N_CHIPS))[:CONV_K]
    loss_out = small[LOSS_ROW, 0]

    parts = _grads_to_parts(dw)
    plist = [parts[n] for n in MIXER]
    started = reduce_start("mixer", plist, _pair_exchange(plist, "grad_pair_exchange_mixer"))
    g_gu, g_down = reduce_finish("ffn", [started])
    res = {}
    for n, g, blk in (("w_gate", g_gu, 0), ("w_up", g_gu, 1), ("w_down", g_down, 0)):
        res[n] = _adamw(g, weights[n], mom[n], var[n], "adamw_" + n, g_block=blk)
    small_names = VEC_NAMES + ["conv_w"]
    res.update(zip(small_names, _adamw_small([g_small[n] for n in small_names], [weights[n] for n in small_names],
                                             [mom[n] for n in small_names], [var[n] for n in small_names])))
    done_meanwhile = [res["w_gate"][1], res["w_up"][1], res["w_down"][1], res["conv_w"][1], grad_x]
    g_mixer = reduce_finish("mixer", done_meanwhile)
    g_mixer[0] = lax.dynamic_slice(g_mixer[0], (0, w_in_shift), (g_mixer[0].shape[0], W_IN_SHARD))
    for n, g in zip(MIXER, g_mixer):
        res[n] = _adamw(g, weights[n], mom[n], var[n], "adamw_" + n)
    outs = [loss_out, grad_x[None]]
    for i in range(4):
        outs += [res[n][i].reshape(given[n].shape) for n in names]
    return tuple(outs)
```

```python
import functools

import jax
import jax.numpy as jnp
from jax import lax
from jax.experimental import pallas as pl
from jax.experimental.pallas import tpu as pltpu

F32 = jnp.float32
BF16 = jnp.bfloat16

D_MODEL = 2048
CONV_CH = 1024
CONV_K = 31
CONV_K_PAD = 32
N_HEADS = 8
QK_NOPE = 128
QK_ROPE = 64
V_HEAD = 128
QK_HEAD = QK_NOPE + QK_ROPE
Q_LORA = 768
KV_LORA = 512
ATTN_CH = N_HEADS * V_HEAD
Z2_COLS = Q_LORA + KV_LORA + 128
D_FF = 5632
ROPE_THETA = 10000.0
EPS = 1e-6
LANES = 128
N_CHIPS = 4
N_DEV = 8

ADAM_LR = 0.001
ADAM_B1 = 0.9
ADAM_B2 = 0.999
ADAM_EPS = 1e-08
ADAM_WD = 0.01
ADAM_STEP = 10

VMEM_LIMIT = 56 * 1024 * 1024
ROW_TILE = 256
MAX_TK = 2816
MESH = pl.DeviceIdType.MESH


def _params(sem=None):
    return pltpu.CompilerParams(dimension_semantics=sem, vmem_limit_bytes=VMEM_LIMIT)


def _first_divisor(n, cands):
    for c in cands:
        if n % c == 0:
            return c
    return n


def _matmul(pairs, mode, out_dtype, name, b_parts=None, out_parts=False, tiles=(None, None, None), after=None):
    a0, b0 = pairs[0]
    part_c = b0.shape[2] if b_parts else None
    if mode == "nn":
        m, n = a0.shape[0], (N_CHIPS * part_c if b_parts else b0.shape[1])
        ks = [a.shape[1] for a, _ in pairs]
    elif mode == "nt":
        m, n = a0.shape[0], b0.shape[-2]
        ks = [a.shape[1] for a, _ in pairs]
    else:
        m, n = a0.shape[1], b0.shape[1]
        ks = [a.shape[0] for a, _ in pairs]
    tm = tiles[0] or _first_divisor(m, (1024, 768, 512, 256))
    tn = tiles[1] or (n if n <= 1536 else _first_divisor(n, (1024, 512, 256, 128)))
    tks = [tiles[2] or (k if k <= MAX_TK else MAX_TK) for k in ks]
    nks = [k // tk for k, tk in zip(ks, tks)]
    offs = [sum(nks[:p]) for p in range(len(pairs))]
    nk = sum(nks)
    n_pairs = len(pairs)
    assert not (b_parts or out_parts) or n_pairs == 1

    def kk(k, p):
        return jnp.clip(k - offs[p], 0, nks[p] - 1)

    in_specs = []
    for p in range(n_pairs):
        tk = tks[p]
        if mode == "nn":
            in_specs.append(pl.BlockSpec((tm, tk), lambda i, j, k, p=p: (i, kk(k, p))))
            if b_parts == "n":
                per = part_c // tn
                in_specs.append(pl.BlockSpec((None, tk, tn), lambda i, j, k: (j // per, k, j % per)))
            else:
                in_specs.append(pl.BlockSpec((tk, tn), lambda i, j, k, p=p: (kk(k, p), j)))
        elif mode == "nt":
            in_specs.append(pl.BlockSpec((tm, tk), lambda i, j, k, p=p: (i, kk(k, p))))
            if b_parts == "k":
                per = part_c // tk
                in_specs.append(pl.BlockSpec((None, tn, tk), lambda i, j, k: (k // per, j, k % per)))
            else:
                in_specs.append(pl.BlockSpec((tn, tk), lambda i, j, k, p=p: (j, kk(k, p))))
        else:
            in_specs.append(pl.BlockSpec((tk, tm), lambda i, j, k, p=p: (kk(k, p), i)))
            in_specs.append(pl.BlockSpec((tk, tn), lambda i, j, k, p=p: (kk(k, p), j)))
    if out_parts:
        out_per = (n // N_CHIPS) // tn
        out_spec = pl.BlockSpec((None, tm, tn), lambda i, j, k: (j // out_per, i, j % out_per))
        out_shape = jax.ShapeDtypeStruct((N_CHIPS, m, n // N_CHIPS), out_dtype)
    else:
        out_spec = pl.BlockSpec((tm, tn), lambda i, j, k: (i, j))
        out_shape = jax.ShapeDtypeStruct((m, n), out_dtype)
    dims = {"nn": (((1,), (0,)), ((), ())), "nt": (((1,), (1,)), ((), ())), "tn": (((0,), (0,)), ((), ()))}[mode]

    n_after = 0 if after is None else 1

    def body(*refs):
        o_ref = refs[2 * n_pairs + n_after]
        k = pl.program_id(2)

        def prod(p):
            return lax.dot_general(refs[2 * p][...], refs[2 * p + 1][...], dims, preferred_element_type=F32)

        if nk == 1:
            o_ref[...] = prod(0).astype(o_ref.dtype)
            return
        acc = refs[2 * n_pairs + n_after + 1]
        for p in range(n_pairs):
            first, last = offs[p], offs[p] + nks[p] - 1
            lo, hi = max(first, 1), min(last, nk - 2)
            if first == 0:
                @pl.when(k == 0)
                def _(p=p):
                    acc[...] = prod(p)

            if lo <= hi:
                @pl.when((k >= lo) & (k <= hi))
                def _(p=p):
                    acc[...] += prod(p)

            if last == nk - 1:
                @pl.when(k == nk - 1)
                def _(p=p):
                    o_ref[...] = (acc[...] + prod(p)).astype(o_ref.dtype)

    flat = [t for pr in pairs for t in pr] + ([] if after is None else [after])
    return pl.pallas_call(
        body,
        name=name,
        grid=(m // tm, n // tn, nk),
        in_specs=in_specs + [pl.BlockSpec(memory_space=pl.ANY)] * n_after,
        out_specs=out_spec,
        out_shape=out_shape,
        scratch_shapes=[pltpu.VMEM((tm, tn), F32)] if nk > 1 else [],
        compiler_params=_params(("parallel", "parallel", "arbitrary")),
    )(*flat)


F4 = D_FF // N_CHIPS
FFN_TM = 512
FFN_STRIP = 256


def _ffn_up(hf, w_gu):
    t, d = hf.shape

    def body(a_ref, b_ref, gu_ref, act_ref):
        for r in range(0, FFN_TM, FFN_STRIP):
            acc = jnp.dot(a_ref[r:r + FFN_STRIP, :], b_ref[...], preferred_element_type=F32)
            g = acc[:, :F4]
            gu_ref[r:r + FFN_STRIP, :] = acc.astype(gu_ref.dtype)
            act_ref[r:r + FFN_STRIP, :] = (g * _sigmoid(g) * acc[:, F4:]).astype(act_ref.dtype)

    return pl.pallas_call(
        body,
        name="ffn_up",
        grid=(N_CHIPS, t // FFN_TM),
        in_specs=[pl.BlockSpec((FFN_TM, d), lambda q, i: (i, 0)),
                  pl.BlockSpec((None, d, 2 * F4), lambda q, i: (q, 0, 0))],
        out_specs=[pl.BlockSpec((FFN_TM, 2 * F4), lambda q, i: (i, q)),
                   pl.BlockSpec((FFN_TM, F4), lambda q, i: (i, q))],
        out_shape=[jax.ShapeDtypeStruct((t, 2 * D_FF), BF16), jax.ShapeDtypeStruct((t, D_FF), BF16)],
        compiler_params=_params(("parallel", "parallel")),
    )(hf, w_gu)


def _ffn_down_dx(d_ff, w_down, gu):
    t, d = d_ff.shape

    def body(a_ref, b_ref, gu_ref, o_ref):
        for r in range(0, FFN_TM, FFN_STRIP):
            rows = slice(r, r + FFN_STRIP)
            d_act = lax.dot_general(a_ref[rows, :], b_ref[...], NT_DIMS, preferred_element_type=F32)
            g = gu_ref[rows, :F4].astype(F32)
            u = gu_ref[rows, F4:].astype(F32)
            sg = _sigmoid(g)
            o_ref[rows, :F4] = (d_act * u * sg * (1.0 + g * (1.0 - sg))).astype(o_ref.dtype)
            o_ref[rows, F4:] = (d_act * g * sg).astype(o_ref.dtype)

    return pl.pallas_call(
        body,
        name="ffn_down_dx",
        grid=(N_CHIPS, t // FFN_TM),
        in_specs=[pl.BlockSpec((FFN_TM, d), lambda q, i: (i, 0)),
                  pl.BlockSpec((F4, d), lambda q, i: (q, 0)),
                  pl.BlockSpec((FFN_TM, 2 * F4), lambda q, i: (i, q))],
        out_specs=pl.BlockSpec((FFN_TM, 2 * F4), lambda q, i: (i, q)),
        out_shape=jax.ShapeDtypeStruct((t, 2 * D_FF), BF16),
        compiler_params=_params(("parallel", "parallel")),
    )(d_ff, w_down, gu)


def _rowwise(fn, row_ins, vec_ins, row_outs, acc_outs, name, after=None):
    t = row_ins[0].shape[0]
    tm = ROW_TILE
    n_in = len(row_ins) + len(vec_ins)
    n_row = len(row_outs)
    extra = [] if after is None else [after]

    def body(*refs):
        ins = [r[...] for r in refs[:n_in]]
        outs = refs[n_in + len(extra):]
        vals = fn(*ins)
        for r, v in zip(outs[:n_row], vals[:n_row]):
            r[...] = v.astype(r.dtype)
        if acc_outs:
            @pl.when(pl.program_id(0) == 0)
            def _():
                for r in outs[n_row:]:
                    r[...] = jnp.zeros_like(r)

            for r, v in zip(outs[n_row:], vals[n_row:]):
                r[...] += v

    in_specs = [pl.BlockSpec((tm, a.shape[1]), lambda i: (i, 0)) for a in row_ins]
    in_specs += [pl.BlockSpec(a.shape, lambda i: (0, 0)) for a in vec_ins]
    out_specs = [pl.BlockSpec((tm, c), lambda i: (i, 0)) for c, _ in row_outs]
    out_specs += [pl.BlockSpec((1, c), lambda i: (0, 0)) for c in acc_outs]
    out_shape = [jax.ShapeDtypeStruct((t, c), dt) for c, dt in row_outs]
    out_shape += [jax.ShapeDtypeStruct((1, c), F32) for c in acc_outs]
    return pl.pallas_call(
        body,
        name=name,
        grid=(t // tm,),
        in_specs=in_specs + [pl.BlockSpec(memory_space=pl.ANY)] * len(extra),
        out_specs=out_specs,
        out_shape=out_shape,
        compiler_params=_params(("arbitrary",)),
    )(*row_ins, *vec_ins, *extra)


def _mean(v):
    return jnp.mean(v, axis=-1, keepdims=True)


def _colsum(v):
    return jnp.sum(v, axis=0, keepdims=True)


def _rms_fwd(v, g):
    r = lax.rsqrt(_mean(v * v) + EPS)
    vhat = v * r
    return vhat * g, vhat, r


def _rms_bwd(dn, vhat, r, g):
    dng = dn * g
    return r * (dng - vhat * _mean(dng * vhat)), _colsum(dn * vhat)


def _swap_rope_halves(v):
    n = v.shape[-1]
    lane = lax.broadcasted_iota(jnp.int32, v.shape, v.ndim - 1)
    return jnp.where(lane % QK_ROPE < QK_ROPE // 2, pltpu.roll(v, n - QK_ROPE // 2, v.ndim - 1),
                     pltpu.roll(v, QK_ROPE // 2, v.ndim - 1))


def _rope(v, cos2, sin2):
    return v * cos2 + _swap_rope_halves(v) * sin2


def _rope_transposed(dv, cos2, sin2):
    return dv * cos2 + _swap_rope_halves(dv * sin2)


def _sigmoid(v):
    return 1.0 / (1.0 + jnp.exp(-v))


CONV_ROWS = 256


def _conv_fwd(ag, conv_w, conv_b):
    t = ag.shape[0]
    cb = LANES

    def body(ag_ref, w_ref, b_ref, o_ref, scr):
        a = ag_ref[:, :cb].astype(F32)
        g = ag_ref[:, cb:].astype(F32)
        scr[pl.ds(0, CONV_K_PAD), :] = jnp.zeros((CONV_K_PAD, cb), F32)
        scr[pl.ds(CONV_K_PAD, t), :] = a * _sigmoid(g)
        for r0 in range(0, t, CONV_ROWS):
            acc = jnp.zeros((CONV_ROWS, cb), F32) + b_ref[...]
            for k in range(CONV_K):
                acc = acc + w_ref[k:k + 1, :] * scr[pl.ds(r0 + CONV_K_PAD - (CONV_K - 1) + k, CONV_ROWS), :]
            o_ref[pl.ds(r0, CONV_ROWS), :] = acc

    return pl.pallas_call(
        body,
        name="conv_fwd",
        grid=(CONV_CH // cb,),
        in_specs=[pl.BlockSpec((t, 2 * cb), lambda j: (0, j)),
                  pl.BlockSpec((CONV_K_PAD, cb), lambda j: (0, j)),
                  pl.BlockSpec((1, cb), lambda j: (0, j))],
        out_specs=pl.BlockSpec((t, cb), lambda j: (0, j)),
        out_shape=jax.ShapeDtypeStruct((t, CONV_CH), F32),
        scratch_shapes=[pltpu.VMEM((t + CONV_K_PAD, cb), F32)],
        compiler_params=_params(("parallel",)),
    )(ag, conv_w, conv_b)


def _conv_bwd(d_u1, ag, conv_w):
    t = ag.shape[0]
    cb = LANES

    def body(du_ref, ag_ref, w_ref, dag_ref, dw_ref, db_ref, su, sd):
        a = ag_ref[:, :cb].astype(F32)
        g = ag_ref[:, cb:].astype(F32)
        sg = _sigmoid(g)
        su[pl.ds(0, CONV_K_PAD), :] = jnp.zeros((CONV_K_PAD, cb), F32)
        su[pl.ds(CONV_K_PAD, t), :] = a * sg
        sd[pl.ds(0, t), :] = du_ref[...]
        sd[pl.ds(t, CONV_K_PAD), :] = jnp.zeros((CONV_K_PAD, cb), F32)
        db_ref[...] = _colsum(du_ref[...])
        dw_ref[...] = jnp.zeros_like(dw_ref)
        for r0 in range(0, t, CONV_ROWS):
            du = sd[pl.ds(r0, CONV_ROWS), :]
            acc = jnp.zeros((CONV_ROWS, cb), F32)
            for k in range(CONV_K):
                acc = acc + w_ref[k:k + 1, :] * sd[pl.ds(r0 + (CONV_K - 1) - k, CONV_ROWS), :]
                dw_ref[k:k + 1, :] += _colsum(du * su[pl.ds(r0 + CONV_K_PAD - (CONV_K - 1) + k, CONV_ROWS), :])
            sgc = sg[r0:r0 + CONV_ROWS]
            ac = a[r0:r0 + CONV_ROWS]
            dag_ref[pl.ds(r0, CONV_ROWS), :cb] = (acc * sgc).astype(dag_ref.dtype)
            dag_ref[pl.ds(r0, CONV_ROWS), cb:] = (acc * ac * sgc * (1.0 - sgc)).astype(dag_ref.dtype)

    return pl.pallas_call(
        body,
        name="conv_bwd",
        grid=(CONV_CH // cb,),
        in_specs=[pl.BlockSpec((t, cb), lambda j: (0, j)),
                  pl.BlockSpec((t, 2 * cb), lambda j: (0, j)),
                  pl.BlockSpec((CONV_K_PAD, cb), lambda j: (0, j))],
        out_specs=[pl.BlockSpec((t, 2 * cb), lambda j: (0, j)),
                   pl.BlockSpec((CONV_K_PAD, cb), lambda j: (0, j)),
                   pl.BlockSpec((1, cb), lambda j: (0, j))],
        out_shape=[jax.ShapeDtypeStruct((t, 2 * CONV_CH), BF16),
                   jax.ShapeDtypeStruct((CONV_K_PAD, CONV_CH), F32),
                   jax.ShapeDtypeStruct((1, CONV_CH), F32)],
        scratch_shapes=[pltpu.VMEM((t + CONV_K_PAD, cb), F32), pltpu.VMEM((t + CONV_K_PAD, cb), F32)],
        compiler_params=_params(("parallel",)),
    )(d_u1, ag, conv_w)


ATT_TQ = 256
NEG = float(jnp.finfo(jnp.float32).min)
SCALE = QK_HEAD ** -0.5
NT_DIMS = (((1,), (1,)), ((), ()))
TN_DIMS = (((0,), (0,)), ((), ()))


def _att_probs(qf, kf, row0):
    s = lax.dot_general(qf, kf, NT_DIMS, preferred_element_type=F32) * SCALE
    tq, t = s.shape
    qpos = row0 + lax.broadcasted_iota(jnp.int32, (tq, t), 0)
    kpos = lax.broadcasted_iota(jnp.int32, (tq, t), 1)
    s = jnp.where(kpos <= qpos, s, NEG)
    p = jnp.exp(s - jnp.max(s, axis=-1, keepdims=True))
    return p * (1.0 / jnp.sum(p, axis=-1, keepdims=True))


def _half_mask(shape, which):
    lane = lax.broadcasted_iota(jnp.int32, shape, len(shape) - 1)
    return (lane // QK_ROPE == which).astype(F32)


def _attention_fwd(q, kv, kpe2, cos2, sin2):
    t = q.shape[0]
    tq = ATT_TQ

    def body(qn_ref, qp_ref, c_ref, s_ref, kv_ref, kpe_ref, o_ref):
        roped = _rope(qp_ref[...].astype(F32), c_ref[...], s_ref[...])

        def block(i):
            keys = slice(0, (i + 1) * tq)
            for e in range(2):
                qf = jnp.concatenate([qn_ref[:, e * QK_NOPE:(e + 1) * QK_NOPE],
                                      (roped * _half_mask(roped.shape, e)).astype(BF16)], axis=1)
                kf = jnp.concatenate([kv_ref[keys, e * 256:e * 256 + QK_NOPE], kpe_ref[keys, :]], axis=1)
                p = _att_probs(qf, kf, i * tq)
                v = kv_ref[keys, e * 256 + QK_NOPE:(e + 1) * 256]
                o_ref[:, e * V_HEAD:(e + 1) * V_HEAD] = jnp.dot(p.astype(BF16), v, preferred_element_type=F32).astype(o_ref.dtype)

        for i in range(t // tq):
            pl.when(pl.program_id(1) == i)(functools.partial(block, i))

    return pl.pallas_call(
        body,
        name="attention_fwd",
        grid=(N_HEADS // 2, t // tq),
        in_specs=[pl.BlockSpec((tq, 2 * QK_NOPE), lambda h, i: (i, h)),
                  pl.BlockSpec((tq, LANES), lambda h, i: (i, N_HEADS + h)),
                  pl.BlockSpec((tq, LANES), lambda h, i: (i, 0)),
                  pl.BlockSpec((tq, LANES), lambda h, i: (i, 0)),
                  pl.BlockSpec((t, 512), lambda h, i: (0, h)),
                  pl.BlockSpec((t, LANES), lambda h, i: (0, 0))],
        out_specs=pl.BlockSpec((tq, 2 * V_HEAD), lambda h, i: (i, h)),
        out_shape=jax.ShapeDtypeStruct((t, ATTN_CH), BF16),
        compiler_params=_params(("parallel", "parallel")),
    )(q, q, cos2, sin2, kv, kpe2)


def _attention_bwd(q, kv, kpe2, cos2, sin2, d_attn):
    t = q.shape[0]
    tq = ATT_TQ
    n_q = t // tq

    def body(qn_ref, qp_ref, c_ref, s_ref, kv_ref, kpe_ref, do_ref, dqn_ref, dqp_ref, dkv_ref, dkpe_ref, dkv_acc):
        h, i = pl.program_id(0), pl.program_id(1)

        @pl.when(i == 0)
        def _():
            dkv_acc[...] = jnp.zeros_like(dkv_acc)

        @pl.when((i == 0) & (h == 0))
        def _():
            dkpe_ref[...] = jnp.zeros_like(dkpe_ref)

        roped = _rope(qp_ref[...].astype(F32), c_ref[...], s_ref[...])

        def block(ib):
            keys = slice(0, (ib + 1) * tq)
            d_roped = jnp.zeros((tq, LANES), F32)
            for e in range(2):
                mask = _half_mask(roped.shape, e)
                qf = jnp.concatenate([qn_ref[:, e * QK_NOPE:(e + 1) * QK_NOPE], (roped * mask).astype(BF16)], axis=1)
                kf = jnp.concatenate([kv_ref[keys, e * 256:e * 256 + QK_NOPE], kpe_ref[keys, :]], axis=1)
                v = kv_ref[keys, e * 256 + QK_NOPE:(e + 1) * 256]
                do = do_ref[:, e * V_HEAD:(e + 1) * V_HEAD]
                p = _att_probs(qf, kf, ib * tq)
                dp = lax.dot_general(do, v, NT_DIMS, preferred_element_type=F32)
                ds = (p * (dp - jnp.sum(p * dp, axis=-1, keepdims=True)) * SCALE).astype(BF16)
                dqf = jnp.dot(ds, kf, preferred_element_type=F32)
                dkf = lax.dot_general(ds, qf, TN_DIMS, preferred_element_type=F32)
                dv = lax.dot_general(p.astype(BF16), do, TN_DIMS, preferred_element_type=F32)
                dqn_ref[:, e * QK_NOPE:(e + 1) * QK_NOPE] = dqf[:, :QK_NOPE].astype(dqn_ref.dtype)
                d_roped = d_roped + dqf[:, QK_NOPE:] * mask
                dkv_acc[keys, e * 256:e * 256 + QK_NOPE] += dkf[:, :QK_NOPE]
                dkv_acc[keys, e * 256 + QK_NOPE:(e + 1) * 256] += dv
                dkpe_ref[keys, :] += dkf[:, QK_NOPE:]
            dqp_ref[...] = _rope_transposed(d_roped, c_ref[...], s_ref[...]).astype(dqp_ref.dtype)

        for ib in range(n_q):
            pl.when(i == ib)(functools.partial(block, ib))

        @pl.when(i == n_q - 1)
        def _():
            dkv_ref[...] = dkv_acc[...].astype(dkv_ref.dtype)

    return pl.pallas_call(
        body,
        name="attention_bwd",
        grid=(N_HEADS // 2, n_q),
        in_specs=[pl.BlockSpec((tq, 2 * QK_NOPE), lambda h, i: (i, h)),
                  pl.BlockSpec((tq, LANES), lambda h, i: (i, N_HEADS + h)),
                  pl.BlockSpec((tq, LANES), lambda h, i: (i, 0)),
                  pl.BlockSpec((tq, LANES), lambda h, i: (i, 0)),
                  pl.BlockSpec((t, 512), lambda h, i: (0, h)),
                  pl.BlockSpec((t, LANES), lambda h, i: (0, 0)),
                  pl.BlockSpec((tq, 2 * V_HEAD), lambda h, i: (i, h))],
        out_specs=[pl.BlockSpec((tq, 2 * QK_NOPE), lambda h, i: (i, h)),
                   pl.BlockSpec((tq, LANES), lambda h, i: (i, h)),
                   pl.BlockSpec((t, 512), lambda h, i: (0, h)),
                   pl.BlockSpec((t, LANES), lambda h, i: (0, 0))],
        out_shape=[jax.ShapeDtypeStruct((t, N_HEADS * QK_NOPE), BF16),
                   jax.ShapeDtypeStruct((t, N_HEADS * QK_ROPE), BF16),
                   jax.ShapeDtypeStruct((t, N_HEADS * 256), BF16),
                   jax.ShapeDtypeStruct((t, LANES), F32)],
        scratch_shapes=[pltpu.VMEM((t, 512), F32)],
        compiler_params=_params(("arbitrary", "arbitrary")),
    )(q, q, cos2, sin2, kv, kpe2, d_attn)


def _local_step(x, target, cos2, sin2, vec, w, ffn):
    d = D_MODEL

    (h,) = _rowwise(lambda xv, g: (_rms_fwd(xv, g)[0],), [x], [vec["pre_mix_norm"]], [(d, BF16)], [], "pre_mix_norm_fwd")
    ag = _matmul([(h, w["w_ag"])], "nn", BF16, "in_proj_ag")
    z2 = _matmul([(h, w["w_z2"])], "nn", BF16, "in_proj_z2")
    w = {**w, **ffn["mixer_rest"](z2)}
    u1 = _conv_fwd(ag, w["conv_w"], vec["conv_b"])

    def latents_fwd(z, c2, s2, qg, kvg):
        z = z.astype(F32)
        qn = _rms_fwd(z[:, :Q_LORA], qg)[0]
        kvn = _rms_fwd(z[:, Q_LORA:Q_LORA + KV_LORA], kvg)[0]
        kr = z[:, Q_LORA + KV_LORA:]
        kr2 = kr + pltpu.roll(kr, QK_ROPE, 1)
        return qn, kvn, _rope(kr2, c2, s2)

    qn, kvn, kpe2 = _rowwise(latents_fwd, [z2, cos2, sin2], [vec["q_norm"], vec["kv_norm"]],
                             [(Q_LORA, BF16), (KV_LORA, BF16), (LANES, BF16)], [], "latents_fwd")
    q = _matmul([(qn, w["w_uq"])], "nn", BF16, "q_up")
    kv = _matmul([(kvn, w["w_ukv"])], "nn", BF16, "kv_up")
    attn = _attention_fwd(q, kv, kpe2, cos2, sin2)

    def conv_post(u, lg, lb):
        mu = _mean(u)
        uc = u - mu
        rstd = lax.rsqrt(_mean(uc * uc) + EPS)
        uhat = uc * rstd
        u2 = uhat * lg + lb
        sg = _sigmoid(u2)
        return uhat, rstd, u2, sg, u2 * sg

    def mix_in_fwd(u, at, lg, lb, cg, ag_):
        u3 = conv_post(u, lg, lb)[4]
        cn = _rms_fwd(u3, cg)[0]
        an = _rms_fwd(at.astype(F32), ag_)[0]
        return (jnp.concatenate([cn, an], axis=1),)

    (cat,) = _rowwise(mix_in_fwd, [u1, attn], [vec["conv_ln_g"], vec["conv_ln_b"], vec["conv_out_norm"], vec["attn_out_norm"]],
                      [(2 * CONV_CH, BF16)], [], "mix_in_fwd")
    mix = _matmul([(cat, w["w_out"])], "nn", F32, "out_proj")
    landed = ffn["w_gu_landed"](mix)

    def residual1(xv, mv, gpm, gpf):
        x1 = xv + _rms_fwd(mv, gpm)[0]
        return x1, _rms_fwd(x1, gpf)[0]

    x1, hf = _rowwise(residual1, [x, mix], [vec["post_mix_norm"], vec["pre_ffn_norm"]], [(d, F32), (d, BF16)], [],
                      "residual1_fwd", after=landed)
    w_gu = ffn["w_gu"](hf)
    gu, act = _ffn_up(hf, w_gu)
    w_down = ffn["w_down"](act)
    ff = _matmul([(act, w_down)], "nn", F32, "ffn_down")

    def loss_head(x1v, ffv, tg, g):
        n, fhat, r = _rms_fwd(ffv, g)
        err = x1v + n - tg
        loss = 0.5 * jnp.sum(_mean(err * err), axis=0, keepdims=True)
        dy = err * (1.0 / d)
        d_ff, dg = _rms_bwd(dy, fhat, r, g)
        return dy, d_ff, dg, jnp.broadcast_to(loss, (1, LANES))

    dy, d_ff, g_post_ffn, loss = _rowwise(loss_head, [x1, ff, target], [vec["post_ffn_norm"]],
                                          [(d, F32), (d, BF16)], [d, LANES], "loss_head")
    d_gu = _ffn_down_dx(d_ff, w_down, gu)
    dw_down = _matmul([(act, d_ff)], "tn", BF16, "ffn_down_dw", tiles=(F4, None, None))
    started = ffn["dw_down"](dw_down)
    dw_gu = _matmul([(hf, d_gu)], "tn", BF16, "ffn_gate_up_dw", out_parts=True, tiles=(None, F4, None), after=started)
    started = ffn["dw_gu"](dw_gu)
    d_hf = _matmul([(d_gu, w_gu)], "nt", F32, "ffn_gate_up_dx", b_parts="k", after=started)
    started = ffn["grads_exchanged"](d_hf)

    def residual1_bwd(dyv, dhf, x1v, mv, gpf, gpm):
        _, x1hat, r1 = _rms_fwd(x1v, gpf)
        dn, dgpf = _rms_bwd(dhf, x1hat, r1, gpf)
        d_x1 = dyv + dn
        _, mhat, rm = _rms_fwd(mv, gpm)
        d_mix, dgpm = _rms_bwd(d_x1, mhat, rm, gpm)
        return d_x1, d_mix, dgpf, dgpm

    d_x1, d_mix, g_pre_ffn, g_post_mix = _rowwise(residual1_bwd, [dy, d_hf, x1, mix], [vec["pre_ffn_norm"], vec["post_mix_norm"]],
                                                  [(d, F32), (d, BF16)], [d, d], "residual1_bwd", after=started)
    d_cat = _matmul([(d_mix, w["w_out"])], "nt", BF16, "out_proj_dx")
    dw_out = _matmul([(cat, d_mix)], "tn", BF16, "out_proj_dw")

    def mix_in_bwd(dc, u, at, lg, lb, cg, ag_):
        dc = dc.astype(F32)
        uhat, rstd, u2, sg, u3 = conv_post(u, lg, lb)
        _, u3hat, rc = _rms_fwd(u3, cg)
        d_u3, dcg = _rms_bwd(dc[:, :CONV_CH], u3hat, rc, cg)
        d_u2 = d_u3 * sg * (1.0 + u2 * (1.0 - sg))
        dgl = d_u2 * lg
        d_u1 = rstd * (dgl - _mean(dgl) - uhat * _mean(dgl * uhat))
        _, ahat, ra = _rms_fwd(at.astype(F32), ag_)
        d_at, dag = _rms_bwd(dc[:, CONV_CH:], ahat, ra, ag_)
        return d_u1, d_at, dcg, _colsum(d_u2 * uhat), _colsum(d_u2), dag

    d_u1, d_attn, g_conv_out, g_ln_g, g_ln_b, g_attn_out = _rowwise(
        mix_in_bwd, [d_cat, u1, attn], [vec["conv_ln_g"], vec["conv_ln_b"], vec["conv_out_norm"], vec["attn_out_norm"]],
        [(CONV_CH, F32), (ATTN_CH, BF16)], [CONV_CH] * 4, "mix_in_bwd")
    d_ag, d_conv_w, g_conv_b = _conv_bwd(d_u1, ag, w["conv_w"])
    d_qn_, d_qp_, d_kv, d_kpe2 = _attention_bwd(q, kv, kpe2, cos2, sin2, d_attn)
    d_q = jnp.concatenate([d_qn_, d_qp_], axis=1)
    d_qn = _matmul([(d_q, w["w_uq"])], "nt", BF16, "q_up_dx")
    dw_uq = _matmul([(qn, d_q)], "tn", BF16, "q_up_dw")
    d_kvn = _matmul([(d_kv, w["w_ukv"])], "nt", BF16, "kv_up_dx")
    dw_ukv = _matmul([(kvn, d_kv)], "tn", BF16, "kv_up_dw")

    def latents_bwd(z, dq, dk, dkp, c2, s2, qg, kvg):
        z = z.astype(F32)
        _, qhat, rq = _rms_fwd(z[:, :Q_LORA], qg)
        d_ql, dqg = _rms_bwd(dq.astype(F32), qhat, rq, qg)
        _, khat, rk = _rms_fwd(z[:, Q_LORA:Q_LORA + KV_LORA], kvg)
        d_kl, dkg = _rms_bwd(dk.astype(F32), khat, rk, kvg)
        both = dkp + pltpu.roll(dkp, QK_ROPE, 1)
        d_kr = _rope_transposed(both, c2, s2) * _half_mask(both.shape, 0)
        return jnp.concatenate([d_ql, d_kl, d_kr], axis=1), dqg, dkg

    d_z2, g_q_norm, g_kv_norm = _rowwise(latents_bwd, [z2, d_qn, d_kvn, d_kpe2, cos2, sin2], [vec["q_norm"], vec["kv_norm"]],
                                         [(Z2_COLS, BF16)], [Q_LORA, KV_LORA], "latents_bwd")
    d_h = _matmul([(d_ag, w["w_ag"]), (d_z2, w["w_z2"])], "nt", F32, "in_proj_dx")
    dw_ag = _matmul([(h, d_ag)], "tn", BF16, "in_proj_ag_dw")
    dw_z2 = _matmul([(h, d_z2)], "tn", BF16, "in_proj_z2_dw")

    def pre_mix_bwd(dx1, dh, xv, g):
        _, xhat, r = _rms_fwd(xv, g)
        dn, dg = _rms_bwd(dh, xhat, r, g)
        return dx1 + dn, dg

    grad_x, g_pre_mix = _rowwise(pre_mix_bwd, [d_x1, d_h, x], [vec["pre_mix_norm"]], [(d, F32)], [d], "pre_mix_norm_bwd")

    dw = dict(w_ag=dw_ag, w_z2=dw_z2, w_uq=dw_uq, w_ukv=dw_ukv, conv_w=d_conv_w, w_out=dw_out, w_gu=dw_gu, w_down=dw_down)
    dvec = dict(pre_mix_norm=g_pre_mix, q_norm=g_q_norm, kv_norm=g_kv_norm, conv_b=g_conv_b, conv_ln_g=g_ln_g,
                conv_ln_b=g_ln_b, conv_out_norm=g_conv_out, attn_out_norm=g_attn_out, post_mix_norm=g_post_mix,
                pre_ffn_norm=g_pre_ffn, post_ffn_norm=g_post_ffn)
    return loss, grad_x, dw, dvec


ANY = pl.BlockSpec(memory_space=pl.ANY)


def _place():
    x, y, c = lax.axis_index("x"), lax.axis_index("y"), lax.axis_index("c")
    chips = [(1 - x, y), (x, 1 - y), (1 - x, 1 - y)]
    return x, y, c, chips


def _to_parts(chip, pieces, dtype, name):
    r = pieces[0].shape[0]
    widths = [a.shape[1] for a in pieces]
    tr = r if r <= 512 else _first_divisor(r, (512, 256, 128))

    def body(p_ref, *refs):
        o_ref = refs[len(pieces)]
        off = 0
        for a_ref, wdt in zip(refs, widths):
            o_ref[:, off:off + wdt] = a_ref[...].astype(o_ref.dtype)
            off += wdt

    return pl.pallas_call(
        body,
        name=name,
        grid_spec=pltpu.PrefetchScalarGridSpec(
            num_scalar_prefetch=1,
            grid=(r // tr,),
            in_specs=[pl.BlockSpec((tr, wdt), lambda i, p_ref: (i, 0)) for wdt in widths],
            out_specs=pl.BlockSpec((None, tr, sum(widths)), lambda i, p_ref: (p_ref[0], i, 0))),
        out_shape=jax.ShapeDtypeStruct((N_CHIPS, r, sum(widths)), dtype),
        compiler_params=_params(("parallel",)),
    )(chip, *pieces)


HBM = pl.BlockSpec(memory_space=pltpu.HBM)
SEM = pl.BlockSpec(memory_space=pltpu.SEMAPHORE)
EFFECT = pltpu.SideEffectType.DATAFLOW_SIDE_EFFECTING
VMEM_SPEC = pl.BlockSpec(memory_space=pltpu.VMEM)
TOKEN = jax.ShapeDtypeStruct((8, LANES), F32)


def _in_hbm(a):
    return pltpu.with_memory_space_constraint(a, pltpu.HBM)


def _gather_rows(buf, whole, half):
    r = buf.shape[1]
    return pl.ds(0, r) if whole else pl.ds(half * (r // 2), r // 2)


def _gather_start(bufs, whole, groups, name):
    n = len(bufs)
    n_g = len(groups)

    def body(*refs):
        sems = refs[n:n + 2 * n_g]
        outs = refs[n + 2 * n_g:2 * n + 2 * n_g]
        token = refs[2 * n + 2 * n_g]
        token[...] = jnp.zeros_like(token)
        x, y, c, chips = _place()
        p = 2 * x + y
        for gi, group in enumerate(groups):
            for ki, k in enumerate(group):
                blk = outs[k].at[p, _gather_rows(bufs[k], whole[k], c), :]
                for j, (px, py) in enumerate(chips):
                    pltpu.make_async_remote_copy(src_ref=blk, dst_ref=blk, send_sem=sems[2 * gi].at[3 * ki + j],
                                                 recv_sem=sems[2 * gi + 1].at[3 * ki + j],
                                                 device_id=(px, py, c), device_id_type=MESH).start()

    sem_shapes = []
    for group in groups:
        sem_shapes += [pltpu.SemaphoreType.DMA((3 * len(group),))] * 2
    res = pl.pallas_call(
        body,
        name=name,
        in_specs=[HBM] * n,
        out_specs=[SEM] * (2 * n_g) + [HBM] * n + [VMEM_SPEC],
        out_shape=sem_shapes + [pltpu.HBM(a.shape, a.dtype) for a in bufs] + [TOKEN],
        input_output_aliases={k: 2 * n_g + k for k in range(n)},
        compiler_params=pltpu.CompilerParams(has_side_effects=EFFECT),
    )(*[_in_hbm(a) for a in bufs])
    sems = [(res[2 * gi], res[2 * gi + 1]) for gi in range(n_g)]
    return sems, list(res[2 * n_g:2 * n_g + n]), res[2 * n_g + n]


def _gather_wait(bufs, whole, send, recv, after, name):
    n = len(bufs)

    def body(*refs):
        ins = refs[:n]
        send_ref, recv_ref = refs[n], refs[n + 1]
        x, y, c, chips = _place()
        p = 2 * x + y
        for ki in range(n):
            rows = _gather_rows(bufs[ki], whole[ki], c)
            for j, (px, py) in enumerate(chips):
                cp = pltpu.make_async_remote_copy(src_ref=ins[ki].at[p, rows, :], dst_ref=ins[ki].at[2 * px + py, rows, :],
                                                  send_sem=send_ref.at[3 * ki + j], recv_sem=recv_ref.at[3 * ki + j],
                                                  device_id=(px, py, c), device_id_type=MESH)
                cp.wait_send()
                cp.wait_recv()

    res = pl.pallas_call(
        body,
        name=name,
        in_specs=[HBM] * n + [SEM, SEM, ANY],
        out_specs=[HBM] * n,
        out_shape=[pltpu.HBM(a.shape, a.dtype) for a in bufs],
        input_output_aliases={k: k for k in range(n)},
        compiler_params=pltpu.CompilerParams(has_side_effects=EFFECT),
    )(*bufs, send, recv, after)
    return list(res)


def _gather_hand_on(bufs, name):
    n = len(bufs)

    def body(*refs):
        outs = refs[n:2 * n]
        send, recv = refs[2 * n:]
        x, y, c, chips = _place()

        def d2d(k, j, half):
            px, py = chips[j]
            blk = outs[k].at[2 * px + py, _gather_rows(bufs[k], False, half), :]
            return pltpu.make_async_remote_copy(src_ref=blk, dst_ref=blk, send_sem=send.at[3 * k + j], recv_sem=recv.at[3 * k + j],
                                                device_id=(x, y, 1 - c), device_id_type=MESH)

        sent = [d2d(k, j, c) for k in range(n) for j in range(3)]
        for cp in sent:
            cp.start()
        for k in range(n):
            for j in range(3):
                d2d(k, j, 1 - c).wait_recv()
        for cp in sent:
            cp.wait_send()

    return pl.pallas_call(
        body,
        name=name,
        in_specs=[ANY] * n,
        out_specs=[ANY] * n,
        out_shape=[jax.ShapeDtypeStruct(a.shape, a.dtype) for a in bufs],
        input_output_aliases={k: k for k in range(n)},
        scratch_shapes=[pltpu.SemaphoreType.DMA((3 * n,)), pltpu.SemaphoreType.DMA((3 * n,))],
        compiler_params=pltpu.CompilerParams(has_side_effects=True),
    )(*bufs)


def _hand_on_start(bufs, name):
    n = len(bufs)

    def body(*refs):
        send, recv = refs[n], refs[n + 1]
        outs = refs[n + 2:2 * n + 2]
        refs[2 * n + 2][...] = jnp.zeros(TOKEN.shape, TOKEN.dtype)
        x, y, c, chips = _place()
        for k in range(n):
            for j, (px, py) in enumerate(chips):
                blk = outs[k].at[2 * px + py, _gather_rows(bufs[k], False, c), :]
                pltpu.make_async_remote_copy(src_ref=blk, dst_ref=blk, send_sem=send.at[3 * k + j], recv_sem=recv.at[3 * k + j],
                                             device_id=(x, y, 1 - c), device_id_type=MESH).start()

    res = pl.pallas_call(
        body,
        name=name,
        in_specs=[HBM] * n,
        out_specs=[SEM, SEM] + [HBM] * n + [VMEM_SPEC],
        out_shape=[pltpu.SemaphoreType.DMA((3 * n,))] * 2 + [pltpu.HBM(a.shape, a.dtype) for a in bufs] + [TOKEN],
        input_output_aliases={k: 2 + k for k in range(n)},
        compiler_params=pltpu.CompilerParams(has_side_effects=EFFECT),
    )(*[_in_hbm(a) for a in bufs])
    return res[0], res[1], list(res[2:2 + n]), res[2 + n]


def _hand_on_wait(bufs, send, recv, after, name):
    n = len(bufs)

    def body(*refs):
        ins = refs[:n]
        send_ref, recv_ref = refs[n], refs[n + 1]
        x, y, c, chips = _place()
        for k in range(n):
            for j, (px, py) in enumerate(chips):
                q = 2 * px + py
                cp = pltpu.make_async_remote_copy(src_ref=ins[k].at[q, _gather_rows(bufs[k], False, c), :],
                                                  dst_ref=ins[k].at[q, _gather_rows(bufs[k], False, 1 - c), :],
                                                  send_sem=send_ref.at[3 * k + j], recv_sem=recv_ref.at[3 * k + j],
                                                  device_id=(x, y, 1 - c), device_id_type=MESH)
                cp.wait_send()
                cp.wait_recv()

    res = pl.pallas_call(
        body,
        name=name,
        in_specs=[HBM] * n + [SEM, SEM, ANY],
        out_specs=[HBM] * n,
        out_shape=[pltpu.HBM(a.shape, a.dtype) for a in bufs],
        input_output_aliases={k: k for k in range(n)},
        compiler_params=pltpu.CompilerParams(has_side_effects=EFFECT),
    )(*bufs, send, recv, after)
    return list(res)


def _pair_exchange_start(part, name):
    rh = part.shape[1] // 2
    land_shape = (N_CHIPS, rh, part.shape[2])

    def body(part_ref, land_ref, send, recv, part_out, land_out, token):
        token[...] = jnp.zeros_like(token)
        x, y, c, _ = _place()
        pltpu.make_async_remote_copy(src_ref=part_out.at[:, pl.ds((1 - c) * rh, rh), :], dst_ref=land_out,
                                     send_sem=send, recv_sem=recv, device_id=(x, y, 1 - c), device_id_type=MESH).start()

    res = pl.pallas_call(
        body,
        name=name,
        in_specs=[HBM, HBM],
        out_specs=[SEM, SEM, HBM, HBM, VMEM_SPEC],
        out_shape=[pltpu.SemaphoreType.DMA(()), pltpu.SemaphoreType.DMA(()), pltpu.HBM(part.shape, part.dtype),
                   pltpu.HBM(land_shape, part.dtype), TOKEN],
        input_output_aliases={0: 2, 1: 3},
        compiler_params=pltpu.CompilerParams(has_side_effects=EFFECT),
    )(_in_hbm(part), _in_hbm(lax.empty(land_shape, part.dtype)))
    return res


def _pair_exchange_wait(send, recv, part, land, after, name):
    rh = part.shape[1] // 2

    def body(part_ref, land_ref, send_ref, recv_ref, after_ref, part_out, land_out):
        x, y, c, _ = _place()
        cp = pltpu.make_async_remote_copy(src_ref=part_ref.at[:, pl.ds((1 - c) * rh, rh), :], dst_ref=land_ref,
                                          send_sem=send_ref, recv_sem=recv_ref, device_id=(x, y, 1 - c), device_id_type=MESH)
        cp.wait_send()
        cp.wait_recv()

    return pl.pallas_call(
        body,
        name=name,
        in_specs=[HBM, HBM, SEM, SEM, ANY],
        out_specs=[HBM, HBM],
        out_shape=[pltpu.HBM(part.shape, part.dtype), pltpu.HBM(land.shape, land.dtype)],
        input_output_aliases={0: 0, 1: 1},
        compiler_params=pltpu.CompilerParams(has_side_effects=EFFECT),
    )(part, land, send, recv, after)


def _chip_exchange_start(sums, name):
    n = len(sums)

    def body(*refs):
        send, recv = refs[2 * n], refs[2 * n + 1]
        src = refs[2 * n + 2:3 * n + 2]
        dst = refs[3 * n + 2:4 * n + 2]
        refs[4 * n + 2][...] = jnp.zeros(TOKEN.shape, TOKEN.dtype)
        x, y, c, chips = _place()
        p = 2 * x + y
        for k in range(n):
            for j, (px, py) in enumerate(chips):
                pltpu.make_async_remote_copy(src_ref=src[k].at[2 * px + py], dst_ref=dst[k].at[p],
                                             send_sem=send.at[3 * k + j], recv_sem=recv.at[3 * k + j],
                                             device_id=(px, py, c), device_id_type=MESH).start()

    res = pl.pallas_call(
        body,
        name=name,
        in_specs=[HBM] * (2 * n),
        out_specs=[SEM, SEM] + [HBM] * (2 * n) + [VMEM_SPEC],
        out_shape=[pltpu.SemaphoreType.DMA((3 * n,))] * 2 + [pltpu.HBM(a.shape, a.dtype) for a in sums] * 2 + [TOKEN],
        input_output_aliases={k: 2 + k for k in range(2 * n)},
        compiler_params=pltpu.CompilerParams(has_side_effects=EFFECT),
    )(*[_in_hbm(a) for a in sums], *[_in_hbm(lax.empty(a.shape, a.dtype)) for a in sums])
    return res[0], res[1], list(res[2:2 + n]), list(res[2 + n:2 + 2 * n]), res[2 + 2 * n]


def _chip_exchange_wait(sums, slots, send, recv, after, name):
    n = len(sums)

    def body(*refs):
        src, dst = refs[:n], refs[n:2 * n]
        send_ref, recv_ref = refs[2 * n], refs[2 * n + 1]
        x, y, c, chips = _place()
        for k in range(n):
            for j, (px, py) in enumerate(chips):
                cp = pltpu.make_async_remote_copy(src_ref=src[k].at[2 * px + py], dst_ref=dst[k].at[2 * px + py],
                                                  send_sem=send_ref.at[3 * k + j], recv_sem=recv_ref.at[3 * k + j],
                                                  device_id=(px, py, c), device_id_type=MESH)
                cp.wait_send()
                cp.wait_recv()

    res = pl.pallas_call(
        body,
        name=name,
        in_specs=[HBM] * (2 * n) + [SEM, SEM] + [ANY] * len(after),
        out_specs=[HBM] * (2 * n),
        out_shape=[pltpu.HBM(a.shape, a.dtype) for a in sums] * 2,
        input_output_aliases={k: k for k in range(2 * n)},
        compiler_params=pltpu.CompilerParams(has_side_effects=EFFECT),
    )(*sums, *slots, send, recv, *after)
    return list(res[:n]), list(res[n:])


def _pair_exchange(parts, name):
    n = len(parts)

    def body(*refs):
        ins, outs = refs[:n], refs[n:2 * n]
        send, recv = refs[2 * n:]
        x, y, c, _ = _place()
        copies = []
        for k in range(n):
            rh = parts[k].shape[1] // 2
            cp = pltpu.make_async_remote_copy(
                src_ref=ins[k].at[:, pl.ds((1 - c) * rh, rh), :], dst_ref=outs[k],
                send_sem=send.at[k], recv_sem=recv.at[k], device_id=(x, y, 1 - c), device_id_type=MESH)
            cp.start()
            copies.append(cp)
        for cp in copies:
            cp.wait()

    return pl.pallas_call(
        body,
        name=name,
        in_specs=[ANY] * n,
        out_specs=[ANY] * n,
        out_shape=[jax.ShapeDtypeStruct((N_CHIPS, a.shape[1] // 2, a.shape[2]), a.dtype) for a in parts],
        scratch_shapes=[pltpu.SemaphoreType.DMA((n,)), pltpu.SemaphoreType.DMA((n,))],
        compiler_params=pltpu.CompilerParams(has_side_effects=True),
    )(*parts)


def _pair_sum(core, part, landed, name):
    _, r, cdim = part.shape
    rh = r // 2
    tr = _first_divisor(rh, (256, 128, 64, 32, 16))
    nb = rh // tr

    def body(c_ref, a_ref, b_ref, o_ref):
        o_ref[...] = (a_ref[...].astype(F32) + b_ref[...].astype(F32)).astype(o_ref.dtype)

    return pl.pallas_call(
        body,
        name=name,
        grid_spec=pltpu.PrefetchScalarGridSpec(
            num_scalar_prefetch=1,
            grid=(N_CHIPS, nb),
            in_specs=[pl.BlockSpec((None, tr, cdim), lambda q, i, c_ref: (q, c_ref[0] * nb + i, 0)),
                      pl.BlockSpec((None, tr, cdim), lambda q, i, c_ref: (q, i, 0))],
            out_specs=pl.BlockSpec((None, tr, cdim), lambda q, i, c_ref: (q, i, 0))),
        out_shape=jax.ShapeDtypeStruct((N_CHIPS, rh, cdim), BF16),
        compiler_params=_params(("parallel", "parallel")),
    )(core, part, landed)


def _chip_sum(place, own, slots, name):
    _, rh, cdim = slots.shape
    tr = _first_divisor(rh, (256, 128, 64, 32, 16))
    nb = rh // tr

    def body(place_ref, own_ref, s1_ref, s2_ref, s3_ref, o_ref):
        acc = own_ref[...].astype(F32)
        for s_ref in (s1_ref, s2_ref, s3_ref):
            acc = acc + s_ref[...].astype(F32)
        o_ref[...] = acc

    def other(j):
        return lambda i, place_ref: ((place_ref[0] + j) % N_CHIPS, i, 0)

    return pl.pallas_call(
        body,
        name=name,
        grid_spec=pltpu.PrefetchScalarGridSpec(
            num_scalar_prefetch=1,
            grid=(nb,),
            in_specs=[pl.BlockSpec((None, tr, cdim), other(0))] + [pl.BlockSpec((None, tr, cdim), other(j)) for j in (1, 2, 3)],
            out_specs=pl.BlockSpec((tr, cdim), lambda i, place_ref: (place_ref[1] * nb + i, 0))),
        out_shape=jax.ShapeDtypeStruct((2 * rh, cdim), F32),
        compiler_params=_params(("parallel",)),
    )(place, own, slots, slots, slots)


def _half_exchange(bufs, name):
    n = len(bufs)

    def body(*refs):
        outs = refs[n:2 * n]
        send, recv = refs[2 * n:]
        x, y, c, _ = _place()
        copies = []
        for k in range(n):
            rh = bufs[k].shape[0] // 2
            mine = outs[k].at[pl.ds(c * rh, rh), :]
            cp = pltpu.make_async_remote_copy(src_ref=mine, dst_ref=mine, send_sem=send.at[k], recv_sem=recv.at[k],
                                              device_id=(x, y, 1 - c), device_id_type=MESH)
            cp.start()
            copies.append(cp)
        for k in range(n):
            rh = bufs[k].shape[0] // 2
            theirs = outs[k].at[pl.ds((1 - c) * rh, rh), :]
            copies[k].wait_send()
            pltpu.make_async_remote_copy(src_ref=theirs, dst_ref=theirs, send_sem=send.at[k], recv_sem=recv.at[k],
                                         device_id=(x, y, 1 - c), device_id_type=MESH).wait_recv()

    return pl.pallas_call(
        body,
        name=name,
        in_specs=[ANY] * n,
        out_specs=[ANY] * n,
        out_shape=[jax.ShapeDtypeStruct(a.shape, a.dtype) for a in bufs],
        input_output_aliases={k: k for k in range(n)},
        scratch_shapes=[pltpu.SemaphoreType.DMA((n,)), pltpu.SemaphoreType.DMA((n,))],
        compiler_params=pltpu.CompilerParams(has_side_effects=True),
    )(*bufs)


SMALL_ROWS = 32


def _all_reduce_small(pack):
    def body(in_ref, out_ref, gath, send, recv):
        x, y, c, _ = _place()
        me = 4 * x + 2 * y + c
        gath[me] = in_ref[...]
        copies = []
        for k in range(1, N_DEV):
            dx, dy, dc = (k >> 2) & 1, (k >> 1) & 1, k & 1
            peer = (x ^ dx, y ^ dy, c ^ dc)
            cp = pltpu.make_async_remote_copy(src_ref=in_ref, dst_ref=gath.at[me], send_sem=send.at[k], recv_sem=recv.at[k],
                                              device_id=peer, device_id_type=MESH)
            cp.start()
            copies.append((cp, 4 * peer[0] + 2 * peer[1] + peer[2]))
        for k, (cp, peer_id) in enumerate(copies, start=1):
            cp.wait_send()
            pltpu.make_async_remote_copy(src_ref=in_ref, dst_ref=gath.at[peer_id], send_sem=send.at[k], recv_sem=recv.at[k],
                                         device_id=(x, y, c), device_id_type=MESH).wait_recv()
        acc = gath[0]
        for dev in range(1, N_DEV):
            acc = acc + gath[dev]
        out_ref[...] = acc

    return pl.pallas_call(
        body,
        name="all_reduce_small",
        in_specs=[pl.BlockSpec(memory_space=pltpu.VMEM)],
        out_specs=pl.BlockSpec(memory_space=pltpu.VMEM),
        out_shape=jax.ShapeDtypeStruct(pack.shape, F32),
        scratch_shapes=[pltpu.VMEM((N_DEV,) + pack.shape, F32), pltpu.SemaphoreType.DMA((N_DEV,)), pltpu.SemaphoreType.DMA((N_DEV,))],
        compiler_params=pltpu.CompilerParams(has_side_effects=True, vmem_limit_bytes=VMEM_LIMIT),
    )(pack)


def _adamw_update(g_ref, w_ref, m_ref, v_ref, go_ref, d_ref, mo_ref, vo_ref):
    bc1 = 1.0 - ADAM_B1 ** ADAM_STEP
    bc2 = 1.0 - ADAM_B2 ** ADAM_STEP
    gv = g_ref[...]
    mn = ADAM_B1 * m_ref[...] + (1.0 - ADAM_B1) * gv
    vn = ADAM_B2 * v_ref[...] + (1.0 - ADAM_B2) * (gv * gv)
    go_ref[...] = gv
    mo_ref[...] = mn
    vo_ref[...] = vn
    d_ref[...] = -ADAM_LR * ((mn / bc1) / (jnp.sqrt(vn / bc2) + ADAM_EPS) + ADAM_WD * w_ref[...])


def _adamw_small(gs, ws, ms, vs):
    n = len(gs)

    def body(*refs):
        ins, outs = refs[:4 * n], refs[4 * n:]
        for k in range(n):
            _adamw_update(*[ins[i * n + k] for i in range(4)], *outs[4 * k:4 * k + 4])

    vmem = pl.BlockSpec(memory_space=pltpu.VMEM)
    res = pl.pallas_call(
        body,
        name="adamw_small",
        in_specs=[vmem] * (4 * n),
        out_specs=[vmem] * (4 * n),
        out_shape=[jax.ShapeDtypeStruct(w.shape, F32) for w in ws for _ in range(4)],
        compiler_params=pltpu.CompilerParams(vmem_limit_bytes=VMEM_LIMIT),
    )(*gs, *ws, *ms, *vs)
    return [tuple(res[4 * k:4 * k + 4]) for k in range(n)]


def _adamw(g, w, m, v, name, g_block=0):
    r, cdim = w.shape
    tr = r if r * cdim * 4 <= (1 << 20) else _first_divisor(r, (128, 64, 32, 16, 8))

    def body(*refs):
        _adamw_update(*refs)

    spec = pl.BlockSpec((tr, cdim), lambda i: (i, 0))
    return pl.pallas_call(
        body,
        name=name,
        grid=(r // tr,),
        in_specs=[pl.BlockSpec((tr, cdim), lambda i: (i, g_block))] + [spec] * 3,
        out_specs=[spec] * 4,
        out_shape=[jax.ShapeDtypeStruct((r, cdim), F32)] * 4,
        compiler_params=_params(("parallel",)),
    )(g, w, m, v)


VEC_NAMES = ["pre_mix_norm", "q_norm", "kv_norm", "conv_b", "conv_ln_g", "conv_ln_b", "conv_out_norm",
             "attn_out_norm", "post_mix_norm", "pre_ffn_norm", "post_ffn_norm"]
LOSS_ROW = len(VEC_NAMES)
CONV_W_ROW = 16


def _cols_to_full(parts):
    _, r, cdim = parts.shape
    return parts.transpose(1, 0, 2).reshape(r, N_CHIPS * cdim)


def _full_to_cols(full):
    r, n = full.shape
    return full.reshape(r, N_CHIPS, n // N_CHIPS).transpose(1, 0, 2)


W_IN_SHARD = (2 * CONV_CH + Q_LORA + KV_LORA + QK_ROPE) // N_CHIPS
W_IN_PART = 1024
W_IN_BLOCKS = (2 * CONV_CH + Z2_COLS) // LANES
W_IN_BASE = [p * W_IN_SHARD // LANES for p in range(N_CHIPS)]
W_IN_SPAN = [-(-(p * W_IN_SHARD % LANES + W_IN_SHARD) // LANES) for p in range(N_CHIPS)]


def _w_in_block_home(b):
    n = CONV_CH // LANES
    if b < n:
        return 0, 2 * b
    if b < 2 * n:
        return 0, 2 * (b - n) + 1
    return 1, b - 2 * n


def _to_parts_w_in(shift_chip, w_in):
    r = w_in.shape[0]
    tr = 512

    def body(s_ref, a_ref, o_ref):
        o_ref[...] = jnp.zeros_like(o_ref)
        o_ref[:, :W_IN_SHARD] = a_ref[...].astype(o_ref.dtype)
        o_ref[...] = pltpu.roll(o_ref[...].astype(F32), s_ref[0], 1).astype(o_ref.dtype)

    return pl.pallas_call(
        body,
        name="to_parts_w_in",
        grid_spec=pltpu.PrefetchScalarGridSpec(
            num_scalar_prefetch=1,
            grid=(r // tr,),
            in_specs=[pl.BlockSpec((tr, W_IN_SHARD), lambda i, s_ref: (i, 0))],
            out_specs=pl.BlockSpec((None, tr, W_IN_PART), lambda i, s_ref: (s_ref[1], i, 0))),
        out_shape=jax.ShapeDtypeStruct((N_CHIPS, r, W_IN_PART), BF16),
        compiler_params=_params(("parallel",)),
    )(shift_chip, w_in)


def _assemble_w_in(parts):
    r = parts.shape[1]
    tr = ROW_TILE

    def body(p_ref, ag_ref, z2_ref):
        outs = (ag_ref, z2_ref)
        for b in range(W_IN_BLOCKS):
            blk = None
            for p in range(N_CHIPS):
                i = b - W_IN_BASE[p]
                if 0 <= i < W_IN_SPAN[p]:
                    piece = p_ref[p, :, i * LANES:(i + 1) * LANES]
                    blk = piece if blk is None else blk + piece
            which, at = _w_in_block_home(b)
            outs[which][:, at * LANES:(at + 1) * LANES] = blk

    w_ag, w_z2 = pl.pallas_call(
        body,
        name="assemble_w_in",
        grid=(r // tr,),
        in_specs=[pl.BlockSpec((N_CHIPS, tr, W_IN_PART), lambda i: (0, i, 0))],
        out_specs=[pl.BlockSpec((tr, 2 * CONV_CH), lambda i: (i, 0)), pl.BlockSpec((tr, Z2_COLS), lambda i: (i, 0))],
        out_shape=[jax.ShapeDtypeStruct((r, 2 * CONV_CH), parts.dtype), jax.ShapeDtypeStruct((r, Z2_COLS), parts.dtype)],
        compiler_params=_params(("parallel",)),
    )(parts)
    return dict(w_ag=w_ag, w_z2=w_z2)


def _w_in_grad_parts(dw_ag, dw_z2):
    r = dw_ag.shape[0]
    tr = ROW_TILE

    def body(ag_ref, z2_ref, o_ref):
        ins = (ag_ref, z2_ref)
        for p in range(N_CHIPS):
            for i in range(W_IN_PART // LANES):
                if i < W_IN_SPAN[p]:
                    which, at = _w_in_block_home(W_IN_BASE[p] + i)
                    o_ref[p, :, i * LANES:(i + 1) * LANES] = ins[which][:, at * LANES:(at + 1) * LANES]
                else:
                    o_ref[p, :, i * LANES:(i + 1) * LANES] = jnp.zeros((tr, LANES), o_ref.dtype)

    return pl.pallas_call(
        body,
        name="w_in_grad_parts",
        grid=(r // tr,),
        in_specs=[pl.BlockSpec((tr, 2 * CONV_CH), lambda i: (i, 0)), pl.BlockSpec((tr, Z2_COLS), lambda i: (i, 0))],
        out_specs=pl.BlockSpec((N_CHIPS, tr, W_IN_PART), lambda i: (0, i, 0)),
        out_shape=jax.ShapeDtypeStruct((N_CHIPS, r, W_IN_PART), dw_ag.dtype),
        compiler_params=_params(("parallel",)),
    )(dw_ag, dw_z2)


def _assemble_mixer_rest(g):
    uq = _cols_to_full(g["w_uq"]).reshape(Q_LORA, N_HEADS, QK_HEAD)
    w_uq = jnp.concatenate([uq[:, :, :QK_NOPE].reshape(Q_LORA, N_HEADS * QK_NOPE),
                            uq[:, :, QK_NOPE:].reshape(Q_LORA, N_HEADS * QK_ROPE)], axis=1)
    return dict(w_uq=w_uq, w_ukv=_cols_to_full(g["w_ukv"]), conv_w=_cols_to_full(g["conv_w"]),
                w_out=g["w_out"].reshape(-1, g["w_out"].shape[2]))


def _grads_to_parts(dw):
    uq = dw["w_uq"]
    d_uq = jnp.concatenate([uq[:, :N_HEADS * QK_NOPE].reshape(Q_LORA, N_HEADS, QK_NOPE),
                            uq[:, N_HEADS * QK_NOPE:].reshape(Q_LORA, N_HEADS, QK_ROPE)], axis=2).reshape(Q_LORA, N_HEADS * QK_HEAD)
    return dict(w_in=_w_in_grad_parts(dw["w_ag"], dw["w_z2"]), w_uq=_full_to_cols(d_uq), w_ukv=_full_to_cols(dw["w_ukv"]),
                w_out=dw["w_out"].reshape(N_CHIPS, -1, dw["w_out"].shape[1]))


MIXER = ["w_in", "w_uq", "w_ukv", "w_out"]
FFN = ["w_gu", "w_down"]
BIG = MIXER + FFN


def _pad_lanes(v, n):
    return jnp.pad(v, ((0, 0), (0, n - v.shape[1])))


def kernel(x, positions, pre_mix_norm, w_in, q_norm, w_uq, kv_norm, w_ukv, conv_w, conv_b, conv_ln_g, conv_ln_b, conv_out_norm, attn_out_norm, w_out, post_mix_norm, pre_ffn_norm, w_gate, w_up, w_down, post_ffn_norm, loss_target, m_pre_mix_norm, m_w_in, m_q_norm, m_w_uq, m_kv_norm, m_w_ukv, m_conv_w, m_conv_b, m_conv_ln_g, m_conv_ln_b, m_conv_out_norm, m_attn_out_norm, m_w_out, m_post_mix_norm, m_pre_ffn_norm, m_w_gate, m_w_up, m_w_down, m_post_ffn_norm, v_pre_mix_norm, v_w_in, v_q_norm, v_w_uq, v_kv_norm, v_w_ukv, v_conv_w, v_conv_b, v_conv_ln_g, v_conv_ln_b, v_conv_out_norm, v_attn_out_norm, v_w_out, v_post_mix_norm, v_pre_ffn_norm, v_w_gate, v_w_up, v_w_down, v_post_ffn_norm):
    given = dict(locals())
    names = ["pre_mix_norm", "w_in", "q_norm", "w_uq", "kv_norm", "w_ukv", "conv_w", "conv_b", "conv_ln_g", "conv_ln_b",
             "conv_out_norm", "attn_out_norm", "w_out", "post_mix_norm", "pre_ffn_norm", "w_gate", "w_up", "w_down", "post_ffn_norm"]
    def as_2d(a):
        return a if a.ndim == 2 else a[0]

    weights = {n: as_2d(given[n]) for n in names}
    mom = {n: as_2d(given["m_" + n]) for n in names}
    var = {n: as_2d(given["v_" + n]) for n in names}
    d = D_MODEL

    inv_freq = ROPE_THETA ** (-jnp.arange(0, QK_ROPE, 2, dtype=F32) / QK_ROPE)
    ang = positions[0].astype(F32)[:, None] * inv_freq
    cos, sin = jnp.cos(ang), jnp.sin(ang)
    cos2 = jnp.concatenate([cos, cos, cos, cos], axis=1)
    sin2 = jnp.concatenate([-sin, sin, -sin, sin], axis=1)

    chip = 2 * lax.axis_index("x") + lax.axis_index("y")
    core = lax.axis_index("c")
    chip1 = chip.astype(jnp.int32).reshape(1)
    pieces = {n: [weights[n]] for n in BIG if n != "w_gu"}
    pieces["w_gu"] = [weights["w_gate"], weights["w_up"]]
    core1 = core.astype(jnp.int32).reshape(1)
    place = jnp.stack([chip, core]).astype(jnp.int32)
    rest = ["w_uq", "w_ukv", "w_out"]
    w_in_shift = (chip * W_IN_SHARD) % LANES
    mix_bufs = [_to_parts_w_in(jnp.stack([w_in_shift, chip]).astype(jnp.int32), weights["w_in"])]
    mix_bufs += [_to_parts(chip1, pieces[n], BF16, "to_parts_" + n) for n in rest]
    mix_bufs.append(_to_parts(chip1, [jnp.pad(weights["conv_w"], ((0, CONV_K_PAD - CONV_K), (0, 0)))], F32, "to_parts_conv_w"))
    mix_whole = [False] * 4 + [True]
    mix_sems, mix_thru, _ = _gather_start(mix_bufs, mix_whole, [[0], [1, 2, 3, 4]], "gather_start_mixer")
    ffn_bufs = [_to_parts(chip1, pieces[n], BF16, "to_parts_" + n) for n in FFN]
    ffn_sems, ffn_thru, ffn_started = _gather_start(ffn_bufs, [False, False], [[0], [1]], "gather_start_ffn")
    got = _gather_wait(mix_thru[:1], [False], *mix_sems[0], ffn_started, "gather_wait_w_in")
    full = _assemble_w_in(_gather_hand_on(got, "gather_hand_on_w_in")[0])
    vec = {n: weights[n] for n in VEC_NAMES}
    rs = {}

    def get_mixer_rest(after):
        got = _gather_wait(mix_thru[1:], mix_whole[1:], *mix_sems[1], after, "gather_wait_mixer_rest")
        got = list(_gather_hand_on(got[:3], "gather_hand_on_mixer_rest")) + [got[3]]
        return _assemble_mixer_rest(dict(zip(rest + ["conv_w"], got)))

    def ffn_landed(k, key):
        def hook(after):
            got = _gather_wait(ffn_thru[k:k + 1], [False], *ffn_sems[k], after, "gather_wait_" + key)
            rs[key] = _hand_on_start(got, "hand_on_start_" + key)
            return rs[key][3]
        return hook

    def ffn_weight(key):
        def hook(after):
            send, recv, bufs, _ = rs[key]
            return _hand_on_wait(bufs, send, recv, after, "hand_on_wait_" + key)[0]
        return hook

    def get_w_down(after):
        got = _gather_wait(ffn_thru[1:], [False], *ffn_sems[1], after, "gather_wait_w_down")
        got = _gather_hand_on(got, "gather_hand_on_w_down")[0]
        return got.reshape(-1, got.shape[2])

    def pair_start(key):
        def hook(dw):
            rs[key] = _pair_exchange_start(dw.reshape(N_CHIPS, -1, dw.shape[-1]), "grad_pair_start_" + key)
            return rs[key][4]
        return hook

    def reduce_start(group, plist, landed):
        sums = [_pair_sum(core1, a, b, "grad_pair_sum_%s_%d" % (group, k)) for k, (a, b) in enumerate(zip(plist, landed))]
        rs[group] = _chip_exchange_start(sums, "grad_chip_exchange_start_" + group)
        return rs[group][4]

    def reduce_finish(group, after):
        send, recv, sums, slots, _ = rs[group]
        sums, slots = _chip_exchange_wait(sums, slots, send, recv, after, "grad_chip_exchange_wait_" + group)
        halves = [_chip_sum(place, s, sl, "grad_chip_sum_%s_%d" % (group, k)) for k, (s, sl) in enumerate(zip(sums, slots))]
        return list(_half_exchange(halves, "grad_half_exchange_" + group))

    def ffn_grads_exchanged(after):
        pairs = [_pair_exchange_wait(*rs[key][:4], after, "grad_pair_wait_" + key) for key in ("dw_gu", "dw_down")]
        return reduce_start("ffn", [p[0] for p in pairs], [p[1] for p in pairs])

    hooks = dict(mixer_rest=get_mixer_rest, w_gu_landed=ffn_landed(0, "w_gu"), w_gu=ffn_weight("w_gu"), w_down=get_w_down, dw_down=pair_start("dw_down"), dw_gu=pair_start("dw_gu"),
                 grads_exchanged=ffn_grads_exchanged)
    loss, grad_x, dw, dvec = _local_step(x[0], loss_target[0], cos2, sin2, vec, full, hooks)

    rows = [_pad_lanes(dvec[n], d) for n in VEC_NAMES] + [_pad_lanes(loss, d)]
    rows.append(jnp.zeros((CONV_W_ROW - len(rows), d), F32))
    rows.append(dw["conv_w"].reshape(SMALL_ROWS - CONV_W_ROW, d))
    small = _all_reduce_small(jnp.concatenate(rows, axis=0))
    g_conv_w_full = small[CONV_W_ROW:].reshape(CONV_K_PAD, CONV_CH)
    g_small = {n: small[i:i + 1, :weights[n].shape[1]] for i, n in enumerate(VEC_NAMES)}
    g_small["conv_w"] = lax.dynamic_slice(g_conv_w_full, (0, chip * (CONV_CH // N_CHIPS)), (CONV_K_PAD, CONV_CH // N_CHIPS))[:CONV_K]
    loss_out = small[LOSS_ROW, 0]

    parts = _grads_to_parts(dw)
    plist = [parts[n] for n in MIXER]
    started = reduce_start("mixer", plist, _pair_exchange(plist, "grad_pair_exchange_mixer"))
    g_gu, g_down = reduce_finish("ffn", [started])
    res = {}
    for n, g, blk in (("w_gate", g_gu, 0), ("w_up", g_gu, 1), ("w_down", g_down, 0)):
        res[n] = _adamw(g, weights[n], mom[n], var[n], "adamw_" + n, g_block=blk)
    small_names = VEC_NAMES + ["conv_w"]
    res.update(zip(small_names, _adamw_small([g_small[n] for n in small_names], [weights[n] for n in small_names],
                                             [mom[n] for n in small_names], [var[n] for n in small_names])))
    done_meanwhile = [res["w_gate"][1], res["w_up"][1], res["w_down"][1], res["conv_w"][1], grad_x]
    g_mixer = reduce_finish("mixer", done_meanwhile)
    g_mixer[0] = lax.dynamic_slice(g_mixer[0], (0, w_in_shift), (g_mixer[0].shape[0], W_IN_SHARD))
    for n, g in zip(MIXER, g_mixer):
        res[n] = _adamw(g, weights[n], mom[n], var[n], "adamw_" + n)
    outs = [loss_out, grad_x[None]]
    for i in range(4):
        outs += [res[n][i].reshape(given[n].shape) for n in names]
    return tuple(outs)
```

```python
import functools

import jax
import jax.numpy as jnp
from jax import lax
from jax.experimental import pallas as pl
from jax.experimental.pallas import tpu as pltpu

F32 = jnp.float32
BF16 = jnp.bfloat16

D_MODEL = 2048
CONV_CH = 1024
CONV_K = 31
CONV_K_PAD = 32
N_HEADS = 8
QK_NOPE = 128
QK_ROPE = 64
V_HEAD = 128
QK_HEAD = QK_NOPE + QK_ROPE
Q_LORA = 768
KV_LORA = 512
ATTN_CH = N_HEADS * V_HEAD
Z2_COLS = Q_LORA + KV_LORA + 128
D_FF = 5632
ROPE_THETA = 10000.0
EPS = 1e-6
LANES = 128
N_CHIPS = 4
N_DEV = 8

ADAM_LR = 0.001
ADAM_B1 = 0.9
ADAM_B2 = 0.999
ADAM_EPS = 1e-08
ADAM_WD = 0.01
ADAM_STEP = 10

VMEM_LIMIT = 56 * 1024 * 1024
ROW_TILE = 256
MAX_TK = 2816
MESH = pl.DeviceIdType.MESH


def _params(sem=None):
    return pltpu.CompilerParams(dimension_semantics=sem, vmem_limit_bytes=VMEM_LIMIT)


def _first_divisor(n, cands):
    for c in cands:
        if n % c == 0:
            return c
    return n


def _matmul(pairs, mode, out_dtype, name, b_parts=None, out_parts=False, tiles=(None, None, None), after=None):
    a0, b0 = pairs[0]
    part_c = b0.shape[2] if b_parts else None
    if mode == "nn":
        m, n = a0.shape[0], (N_CHIPS * part_c if b_parts else b0.shape[1])
        ks = [a.shape[1] for a, _ in pairs]
    elif mode == "nt":
        m, n = a0.shape[0], b0.shape[-2]
        ks = [a.shape[1] for a, _ in pairs]
    else:
        m, n = a0.shape[1], b0.shape[1]
        ks = [a.shape[0] for a, _ in pairs]
    tm = tiles[0] or _first_divisor(m, (1024, 768, 512, 256))
    tn = tiles[1] or (n if n <= 1536 else _first_divisor(n, (1024, 512, 256, 128)))
    tks = [tiles[2] or (k if k <= MAX_TK else MAX_TK) for k in ks]
    nks = [k // tk for k, tk in zip(ks, tks)]
    offs = [sum(nks[:p]) for p in range(len(pairs))]
    nk = sum(nks)
    n_pairs = len(pairs)
    assert not (b_parts or out_parts) or n_pairs == 1

    def kk(k, p):
        return jnp.clip(k - offs[p], 0, nks[p] - 1)

    in_specs = []
    for p in range(n_pairs):
        tk = tks[p]
        if mode == "nn":
            in_specs.append(pl.BlockSpec((tm, tk), lambda i, j, k, p=p: (i, kk(k, p))))
            if b_parts == "n":
                per = part_c // tn
                in_specs.append(pl.BlockSpec((None, tk, tn), lambda i, j, k: (j // per, k, j % per)))
            else:
                in_specs.append(pl.BlockSpec((tk, tn), lambda i, j, k, p=p: (kk(k, p), j)))
        elif mode == "nt":
            in_specs.append(pl.BlockSpec((tm, tk), lambda i, j, k, p=p: (i, kk(k, p))))
            if b_parts == "k":
                per = part_c // tk
                in_specs.append(pl.BlockSpec((None, tn, tk), lambda i, j, k: (k // per, j, k % per)))
            else:
                in_specs.append(pl.BlockSpec((tn, tk), lambda i, j, k, p=p: (j, kk(k, p))))
        else:
            in_specs.append(pl.BlockSpec((tk, tm), lambda i, j, k, p=p: (kk(k, p), i)))
            in_specs.append(pl.BlockSpec((tk, tn), lambda i, j, k, p=p: (kk(k, p), j)))
    if out_parts:
        out_per = (n // N_CHIPS) // tn
        out_spec = pl.BlockSpec((None, tm, tn), lambda i, j, k: (j // out_per, i, j % out_per))
        out_shape = jax.ShapeDtypeStruct((N_CHIPS, m, n // N_CHIPS), out_dtype)
    else:
        out_spec = pl.BlockSpec((tm, tn), lambda i, j, k: (i, j))
        out_shape = jax.ShapeDtypeStruct((m, n), out_dtype)
    dims = {"nn": (((1,), (0,)), ((), ())), "nt": (((1,), (1,)), ((), ())), "tn": (((0,), (0,)), ((), ()))}[mode]

    n_after = 0 if after is None else 1

    def body(*refs):
        o_ref = refs[2 * n_pairs + n_after]
        k = pl.program_id(2)

        def prod(p):
            return lax.dot_general(refs[2 * p][...], refs[2 * p + 1][...], dims, preferred_element_type=F32)

        if nk == 1:
            o_ref[...] = prod(0).astype(o_ref.dtype)
            return
        acc = refs[2 * n_pairs + n_after + 1]
        for p in range(n_pairs):
            first, last = offs[p], offs[p] + nks[p] - 1
            lo, hi = max(first, 1), min(last, nk - 2)
            if first == 0:
                @pl.when(k == 0)
                def _(p=p):
                    acc[...] = prod(p)

            if lo <= hi:
                @pl.when((k >= lo) & (k <= hi))
                def _(p=p):
                    acc[...] += prod(p)

            if last == nk - 1:
                @pl.when(k == nk - 1)
                def _(p=p):
                    o_ref[...] = (acc[...] + prod(p)).astype(o_ref.dtype)

    flat = [t for pr in pairs for t in pr] + ([] if after is None else [after])
    return pl.pallas_call(
        body,
        name=name,
        grid=(m // tm, n // tn, nk),
        in_specs=in_specs + [pl.BlockSpec(memory_space=pl.ANY)] * n_after,
        out_specs=out_spec,
        out_shape=out_shape,
        scratch_shapes=[pltpu.VMEM((tm, tn), F32)] if nk > 1 else [],
        compiler_params=_params(("parallel", "parallel", "arbitrary")),
    )(*flat)


F4 = D_FF // N_CHIPS
FFN_TM = 512
FFN_STRIP = 256


def _ffn_up(hf, w_gu):
    t, d = hf.shape

    def body(a_ref, b_ref, gu_ref, act_ref):
        for r in range(0, FFN_TM, FFN_STRIP):
            acc = jnp.dot(a_ref[r:r + FFN_STRIP, :], b_ref[...], preferred_element_type=F32)
            g = acc[:, :F4]
            gu_ref[r:r + FFN_STRIP, :] = acc.astype(gu_ref.dtype)
            act_ref[r:r + FFN_STRIP, :] = (g * _sigmoid(g) * acc[:, F4:]).astype(act_ref.dtype)

    return pl.pallas_call(
        body,
        name="ffn_up",
        grid=(N_CHIPS, t // FFN_TM),
        in_specs=[pl.BlockSpec((FFN_TM, d), lambda q, i: (i, 0)),
                  pl.BlockSpec((None, d, 2 * F4), lambda q, i: (q, 0, 0))],
        out_specs=[pl.BlockSpec((FFN_TM, 2 * F4), lambda q, i: (i, q)),
                   pl.BlockSpec((FFN_TM, F4), lambda q, i: (i, q))],
        out_shape=[jax.ShapeDtypeStruct((t, 2 * D_FF), BF16), jax.ShapeDtypeStruct((t, D_FF), BF16)],
        compiler_params=_params(("parallel", "parallel")),
    )(hf, w_gu)


def _ffn_down_dx(d_ff, w_down, gu):
    t, d = d_ff.shape

    def body(a_ref, b_ref, gu_ref, o_ref):
        for r in range(0, FFN_TM, FFN_STRIP):
            rows = slice(r, r + FFN_STRIP)
            d_act = lax.dot_general(a_ref[rows, :], b_ref[...], NT_DIMS, preferred_element_type=F32)
            g = gu_ref[rows, :F4].astype(F32)
            u = gu_ref[rows, F4:].astype(F32)
            sg = _sigmoid(g)
            o_ref[rows, :F4] = (d_act * u * sg * (1.0 + g * (1.0 - sg))).astype(o_ref.dtype)
            o_ref[rows, F4:] = (d_act * g * sg).astype(o_ref.dtype)

    return pl.pallas_call(
        body,
        name="ffn_down_dx",
        grid=(N_CHIPS, t // FFN_TM),
        in_specs=[pl.BlockSpec((FFN_TM, d), lambda q, i: (i, 0)),
                  pl.BlockSpec((F4, d), lambda q, i: (q, 0)),
                  pl.BlockSpec((FFN_TM, 2 * F4), lambda q, i: (i, q))],
        out_specs=pl.BlockSpec((FFN_TM, 2 * F4), lambda q, i: (i, q)),
        out_shape=jax.ShapeDtypeStruct((t, 2 * D_FF), BF16),
        compiler_params=_params(("parallel", "parallel")),
    )(d_ff, w_down, gu)


def _rowwise(fn, row_ins, vec_ins, row_outs, acc_outs, name, after=None):
    t = row_ins[0].shape[0]
    tm = ROW_TILE
    n_in = len(row_ins) + len(vec_ins)
    n_row = len(row_outs)
    extra = [] if after is None else [after]

    def body(*refs):
        ins = [r[...] for r in refs[:n_in]]
        outs = refs[n_in + len(extra):]
        vals = fn(*ins)
        for r, v in zip(outs[:n_row], vals[:n_row]):
            r[...] = v.astype(r.dtype)
        if acc_outs:
            @pl.when(pl.program_id(0) == 0)
            def _():
                for r in outs[n_row:]:
                    r[...] = jnp.zeros_like(r)

            for r, v in zip(outs[n_row:], vals[n_row:]):
                r[...] += v

    in_specs = [pl.BlockSpec((tm, a.shape[1]), lambda i: (i, 0)) for a in row_ins]
    in_specs += [pl.BlockSpec(a.shape, lambda i: (0, 0)) for a in vec_ins]
    out_specs = [pl.BlockSpec((tm, c), lambda i: (i, 0)) for c, _ in row_outs]
    out_specs += [pl.BlockSpec((1, c), lambda i: (0, 0)) for c in acc_outs]
    out_shape = [jax.ShapeDtypeStruct((t, c), dt) for c, dt in row_outs]
    out_shape += [jax.ShapeDtypeStruct((1, c), F32) for c in acc_outs]
    return pl.pallas_call(
        body,
        name=name,
        grid=(t // tm,),
        in_specs=in_specs + [pl.BlockSpec(memory_space=pl.ANY)] * len(extra),
        out_specs=out_specs,
        out_shape=out_shape,
        compiler_params=_params(("arbitrary",)),
    )(*row_ins, *vec_ins, *extra)


def _mean(v):
    return jnp.mean(v, axis=-1, keepdims=True)


def _colsum(v):
    return jnp.sum(v, axis=0, keepdims=True)


def _rms_fwd(v, g):
    r = lax.rsqrt(_mean(v * v) + EPS)
    vhat = v * r
    return vhat * g, vhat, r


def _rms_bwd(dn, vhat, r, g):
    dng = dn * g
    return r * (dng - vhat * _mean(dng * vhat)), _colsum(dn * vhat)


def _swap_rope_halves(v):
    n = v.shape[-1]
    lane = lax.broadcasted_iota(jnp.int32, v.shape, v.ndim - 1)
    return jnp.where(lane % QK_ROPE < QK_ROPE // 2, pltpu.roll(v, n - QK_ROPE // 2, v.ndim - 1),
                     pltpu.roll(v, QK_ROPE // 2, v.ndim - 1))


def _rope(v, cos2, sin2):
    return v * cos2 + _swap_rope_halves(v) * sin2


def _rope_transposed(dv, cos2, sin2):
    return dv * cos2 + _swap_rope_halves(dv * sin2)


def _sigmoid(v):
    return 1.0 / (1.0 + jnp.exp(-v))


CONV_ROWS = 256


def _conv_fwd(ag, conv_w, conv_b):
    t = ag.shape[0]
    cb = LANES

    def body(ag_ref, w_ref, b_ref, o_ref, scr):
        a = ag_ref[:, :cb].astype(F32)
        g = ag_ref[:, cb:].astype(F32)
        scr[pl.ds(0, CONV_K_PAD), :] = jnp.zeros((CONV_K_PAD, cb), F32)
        scr[pl.ds(CONV_K_PAD, t), :] = a * _sigmoid(g)
        for r0 in range(0, t, CONV_ROWS):
            acc = jnp.zeros((CONV_ROWS, cb), F32) + b_ref[...]
            for k in range(CONV_K):
                acc = acc + w_ref[k:k + 1, :] * scr[pl.ds(r0 + CONV_K_PAD - (CONV_K - 1) + k, CONV_ROWS), :]
            o_ref[pl.ds(r0, CONV_ROWS), :] = acc

    return pl.pallas_call(
        body,
        name="conv_fwd",
        grid=(CONV_CH // cb,),
        in_specs=[pl.BlockSpec((t, 2 * cb), lambda j: (0, j)),
                  pl.BlockSpec((CONV_K_PAD, cb), lambda j: (0, j)),
                  pl.BlockSpec((1, cb), lambda j: (0, j))],
        out_specs=pl.BlockSpec((t, cb), lambda j: (0, j)),
        out_shape=jax.ShapeDtypeStruct((t, CONV_CH), F32),
        scratch_shapes=[pltpu.VMEM((t + CONV_K_PAD, cb), F32)],
        compiler_params=_params(("parallel",)),
    )(ag, conv_w, conv_b)


def _conv_bwd(d_u1, ag, conv_w):
    t = ag.shape[0]
    cb = LANES

    def body(du_ref, ag_ref, w_ref, dag_ref, dw_ref, db_ref, su, sd):
        a = ag_ref[:, :cb].astype(F32)
        g = ag_ref[:, cb:].astype(F32)
        sg = _sigmoid(g)
        su[pl.ds(0, CONV_K_PAD), :] = jnp.zeros((CONV_K_PAD, cb), F32)
        su[pl.ds(CONV_K_PAD, t), :] = a * sg
        sd[pl.ds(0, t), :] = du_ref[...]
        sd[pl.ds(t, CONV_K_PAD), :] = jnp.zeros((CONV_K_PAD, cb), F32)
        db_ref[...] = _colsum(du_ref[...])
        dw_ref[...] = jnp.zeros_like(dw_ref)
        for r0 in range(0, t, CONV_ROWS):
            du = sd[pl.ds(r0, CONV_ROWS), :]
            acc = jnp.zeros((CONV_ROWS, cb), F32)
            for k in range(CONV_K):
                acc = acc + w_ref[k:k + 1, :] * sd[pl.ds(r0 + (CONV_K - 1) - k, CONV_ROWS), :]
                dw_ref[k:k + 1, :] += _colsum(du * su[pl.ds(r0 + CONV_K_PAD - (CONV_K - 1) + k, CONV_ROWS), :])
            sgc = sg[r0:r0 + CONV_ROWS]
            ac = a[r0:r0 + CONV_ROWS]
            dag_ref[pl.ds(r0, CONV_ROWS), :cb] = (acc * sgc).astype(dag_ref.dtype)
            dag_ref[pl.ds(r0, CONV_ROWS), cb:] = (acc * ac * sgc * (1.0 - sgc)).astype(dag_ref.dtype)

    return pl.pallas_call(
        body,
        name="conv_bwd",
        grid=(CONV_CH // cb,),
        in_specs=[pl.BlockSpec((t, cb), lambda j: (0, j)),
                  pl.BlockSpec((t, 2 * cb), lambda j: (0, j)),
                  pl.BlockSpec((CONV_K_PAD, cb), lambda j: (0, j))],
        out_specs=[pl.BlockSpec((t, 2 * cb), lambda j: (0, j)),
                   pl.BlockSpec((CONV_K_PAD, cb), lambda j: (0, j)),
                   pl.BlockSpec((1, cb), lambda j: (0, j))],
        out_shape=[jax.ShapeDtypeStruct((t, 2 * CONV_CH), BF16),
                   jax.ShapeDtypeStruct((CONV_K_PAD, CONV_CH), F32),
                   jax.ShapeDtypeStruct((1, CONV_CH), F32)],
        scratch_shapes=[pltpu.VMEM((t + CONV_K_PAD, cb), F32), pltpu.VMEM((t + CONV_K_PAD, cb), F32)],
        compiler_params=_params(("parallel",)),
    )(d_u1, ag, conv_w)


ATT_TQ = 256
NEG = float(jnp.finfo(jnp.float32).min)
SCALE = QK_HEAD ** -0.5
NT_DIMS = (((1,), (1,)), ((), ()))
TN_DIMS = (((0,), (0,)), ((), ()))


def _att_probs(qf, kf, row0):
    s = lax.dot_general(qf, kf, NT_DIMS, preferred_element_type=F32) * SCALE
    tq, t = s.shape
    qpos = row0 + lax.broadcasted_iota(jnp.int32, (tq, t), 0)
    kpos = lax.broadcasted_iota(jnp.int32, (tq, t), 1)
    s = jnp.where(kpos <= qpos, s, NEG)
    p = jnp.exp(s - jnp.max(s, axis=-1, keepdims=True))
    return p * (1.0 / jnp.sum(p, axis=-1, keepdims=True))


def _half_mask(shape, which):
    lane = lax.broadcasted_iota(jnp.int32, shape, len(shape) - 1)
    return (lane // QK_ROPE == which).astype(F32)


def _attention_fwd(q, kv, kpe2, cos2, sin2):
    t = q.shape[0]
    tq = ATT_TQ

    def body(qn_ref, qp_ref, c_ref, s_ref, kv_ref, kpe_ref, o_ref):
        roped = _rope(qp_ref[...].astype(F32), c_ref[...], s_ref[...])

        def block(i):
            keys = slice(0, (i + 1) * tq)
            for e in range(2):
                qf = jnp.concatenate([qn_ref[:, e * QK_NOPE:(e + 1) * QK_NOPE],
                                      (roped * _half_mask(roped.shape, e)).astype(BF16)], axis=1)
                kf = jnp.concatenate([kv_ref[keys, e * 256:e * 256 + QK_NOPE], kpe_ref[keys, :]], axis=1)
                p = _att_probs(qf, kf, i * tq)
                v = kv_ref[keys, e * 256 + QK_NOPE:(e + 1) * 256]
                o_ref[:, e * V_HEAD:(e + 1) * V_HEAD] = jnp.dot(p.astype(BF16), v, preferred_element_type=F32).astype(o_ref.dtype)

        for i in range(t // tq):
            pl.when(pl.program_id(1) == i)(functools.partial(block, i))

    return pl.pallas_call(
        body,
        name="attention_fwd",
        grid=(N_HEADS // 2, t // tq),
        in_specs=[pl.BlockSpec((tq, 2 * QK_NOPE), lambda h, i: (i, h)),
                  pl.BlockSpec((tq, LANES), lambda h, i: (i, N_HEADS + h)),
                  pl.BlockSpec((tq, LANES), lambda h, i: (i, 0)),
                  pl.BlockSpec((tq, LANES), lambda h, i: (i, 0)),
                  pl.BlockSpec((t, 512), lambda h, i: (0, h)),
                  pl.BlockSpec((t, LANES), lambda h, i: (0, 0))],
        out_specs=pl.BlockSpec((tq, 2 * V_HEAD), lambda h, i: (i, h)),
        out_shape=jax.ShapeDtypeStruct((t, ATTN_CH), BF16),
        compiler_params=_params(("parallel", "parallel")),
    )(q, q, cos2, sin2, kv, kpe2)


def _attention_bwd(q, kv, kpe2, cos2, sin2, d_attn):
    t = q.shape[0]
    tq = ATT_TQ
    n_q = t // tq

    def body(qn_ref, qp_ref, c_ref, s_ref, kv_ref, kpe_ref, do_ref, dqn_ref, dqp_ref, dkv_ref, dkpe_ref, dkv_acc):
        h, i = pl.program_id(0), pl.program_id(1)

        @pl.when(i == 0)
        def _():
            dkv_acc[...] = jnp.zeros_like(dkv_acc)

        @pl.when((i == 0) & (h == 0))
        def _():
            dkpe_ref[...] = jnp.zeros_like(dkpe_ref)

        roped = _rope(qp_ref[...].astype(F32), c_ref[...], s_ref[...])

        def block(ib):
            keys = slice(0, (ib + 1) * tq)
            d_roped = jnp.zeros((tq, LANES), F32)
            for e in range(2):
                mask = _half_mask(roped.shape, e)
                qf = jnp.concatenate([qn_ref[:, e * QK_NOPE:(e + 1) * QK_NOPE], (roped * mask).astype(BF16)], axis=1)
                kf = jnp.concatenate([kv_ref[keys, e * 256:e * 256 + QK_NOPE], kpe_ref[keys, :]], axis=1)
                v = kv_ref[keys, e * 256 + QK_NOPE:(e + 1) * 256]
                do = do_ref[:, e * V_HEAD:(e + 1) * V_HEAD]
                p = _att_probs(qf, kf, ib * tq)
                dp = lax.dot_general(do, v, NT_DIMS, preferred_element_type=F32)
                ds = (p * (dp - jnp.sum(p * dp, axis=-1, keepdims=True)) * SCALE).astype(BF16)
                dqf = jnp.dot(ds, kf, preferred_element_type=F32)
                dkf = lax.dot_general(ds, qf, TN_DIMS, preferred_element_type=F32)
                dv = lax.dot_general(p.astype(BF16), do, TN_DIMS, preferred_element_type=F32)
                dqn_ref[:, e * QK_NOPE:(e + 1) * QK_NOPE] = dqf[:, :QK_NOPE].astype(dqn_ref.dtype)
                d_roped = d_roped + dqf[:, QK_NOPE:] * mask
                dkv_acc[keys, e * 256:e * 256 + QK_NOPE] += dkf[:, :QK_NOPE]
                dkv_acc[keys, e * 256 + QK_NOPE:(e + 1) * 256] += dv
                dkpe_ref[keys, :] += dkf[:, QK_NOPE:]
            dqp_ref[...] = _rope_transposed(d_roped, c_ref[...], s_ref[...]).astype(dqp_ref.dtype)

        for ib in range(n_q):
            pl.when(i == ib)(functools.partial(block, ib))

        @pl.when(i == n_q - 1)
        def _():
            dkv_ref[...] = dkv_acc[...].astype(dkv_ref.dtype)

    return pl.pallas_call(
        body,
        name="attention_bwd",
        grid=(N_HEADS // 2, n_q),
        in_specs=[pl.BlockSpec((tq, 2 * QK_NOPE), lambda h, i: (i, h)),
                  pl.BlockSpec((tq, LANES), lambda h, i: (i, N_HEADS + h)),
                  pl.BlockSpec((tq, LANES), lambda h, i: (i, 0)),
                  pl.BlockSpec((tq, LANES), lambda h, i: (i, 0)),
                  pl.BlockSpec((t, 512), lambda h, i: (0, h)),
                  pl.BlockSpec((t, LANES), lambda h, i: (0, 0)),
                  pl.BlockSpec((tq, 2 * V_HEAD), lambda h, i: (i, h))],
        out_specs=[pl.BlockSpec((tq, 2 * QK_NOPE), lambda h, i: (i, h)),
                   pl.BlockSpec((tq, LANES), lambda h, i: (i, h)),
                   pl.BlockSpec((t, 512), lambda h, i: (0, h)),
                   pl.BlockSpec((t, LANES), lambda h, i: (0, 0))],
        out_shape=[jax.ShapeDtypeStruct((t, N_HEADS * QK_NOPE), BF16),
                   jax.ShapeDtypeStruct((t, N_HEADS * QK_ROPE), BF16),
                   jax.ShapeDtypeStruct((t, N_HEADS * 256), BF16),
                   jax.ShapeDtypeStruct((t, LANES), F32)],
        scratch_shapes=[pltpu.VMEM((t, 512), F32)],
        compiler_params=_params(("arbitrary", "arbitrary")),
    )(q, q, cos2, sin2, kv, kpe2, d_attn)


def _local_step(x, target, cos2, sin2, vec, w, ffn):
    d = D_MODEL

    (h,) = _rowwise(lambda xv, g: (_rms_fwd(xv, g)[0],), [x], [vec["pre_mix_norm"]], [(d, BF16)], [], "pre_mix_norm_fwd")
    ag = _matmul([(h, w["w_ag"])], "nn", BF16, "in_proj_ag")
    z2 = _matmul([(h, w["w_z2"])], "nn", BF16, "in_proj_z2")
    w = {**w, **ffn["mixer_rest"](z2)}
    u1 = _conv_fwd(ag, w["conv_w"], vec["conv_b"])

    def latents_fwd(z, c2, s2, qg, kvg):
        z = z.astype(F32)
        qn = _rms_fwd(z[:, :Q_LORA], qg)[0]
        kvn = _rms_fwd(z[:, Q_LORA:Q_LORA + KV_LORA], kvg)[0]
        kr = z[:, Q_LORA + KV_LORA:]
        kr2 = kr + pltpu.roll(kr, QK_ROPE, 1)
        return qn, kvn, _rope(kr2, c2, s2)

    qn, kvn, kpe2 = _rowwise(latents_fwd, [z2, cos2, sin2], [vec["q_norm"], vec["kv_norm"]],
                             [(Q_LORA, BF16), (KV_LORA, BF16), (LANES, BF16)], [], "latents_fwd")
    q = _matmul([(qn, w["w_uq"])], "nn", BF16, "q_up")
    kv = _matmul([(kvn, w["w_ukv"])], "nn", BF16, "kv_up")
    attn = _attention_fwd(q, kv, kpe2, cos2, sin2)

    def conv_post(u, lg, lb):
        mu = _mean(u)
        uc = u - mu
        rstd = lax.rsqrt(_mean(uc * uc) + EPS)
        uhat = uc * rstd
        u2 = uhat * lg + lb
        sg = _sigmoid(u2)
        return uhat, rstd, u2, sg, u2 * sg

    def mix_in_fwd(u, at, lg, lb, cg, ag_):
        u3 = conv_post(u, lg, lb)[4]
        cn = _rms_fwd(u3, cg)[0]
        an = _rms_fwd(at.astype(F32), ag_)[0]
        return (jnp.concatenate([cn, an], axis=1),)

    (cat,) = _rowwise(mix_in_fwd, [u1, attn], [vec["conv_ln_g"], vec["conv_ln_b"], vec["conv_out_norm"], vec["attn_out_norm"]],
                      [(2 * CONV_CH, BF16)], [], "mix_in_fwd")
    mix = _matmul([(cat, w["w_out"])], "nn", F32, "out_proj")
    landed = ffn["w_gu_landed"](mix)

    def residual1(xv, mv, gpm, gpf):
        x1 = xv + _rms_fwd(mv, gpm)[0]
        return x1, _rms_fwd(x1, gpf)[0]

    x1, hf = _rowwise(residual1, [x, mix], [vec["post_mix_norm"], vec["pre_ffn_norm"]], [(d, F32), (d, BF16)], [],
                      "residual1_fwd", after=landed)
    w_gu = ffn["w_gu"](hf)
    gu, act = _ffn_up(hf, w_gu)
    w_down = ffn["w_down"](act)
    ff = _matmul([(act, w_down)], "nn", F32, "ffn_down")

    def loss_head(x1v, ffv, tg, g):
        n, fhat, r = _rms_fwd(ffv, g)
        err = x1v + n - tg
        loss = 0.5 * jnp.sum(_mean(err * err), axis=0, keepdims=True)
        dy = err * (1.0 / d)
        d_ff, dg = _rms_bwd(dy, fhat, r, g)
        return dy, d_ff, dg, jnp.broadcast_to(loss, (1, LANES))

    dy, d_ff, g_post_ffn, loss = _rowwise(loss_head, [x1, ff, target], [vec["post_ffn_norm"]],
                                          [(d, F32), (d, BF16)], [d, LANES], "loss_head")
    d_gu = _ffn_down_dx(d_ff, w_down, gu)
    dw_down = _matmul([(act, d_ff)], "tn", BF16, "ffn_down_dw", tiles=(F4, None, None))
    started = ffn["dw_down"](dw_down)
    dw_gu = _matmul([(hf, d_gu)], "tn", BF16, "ffn_gate_up_dw", out_parts=True, tiles=(None, F4, None), after=started)
    started = ffn["dw_gu"](dw_gu)
    d_hf = _matmul([(d_gu, w_gu)], "nt", F32, "ffn_gate_up_dx", b_parts="k", after=started)
    started = ffn["grads_exchanged"](d_hf)

    def residual1_bwd(dyv, dhf, x1v, mv, gpf, gpm):
        _, x1hat, r1 = _rms_fwd(x1v, gpf)
        dn, dgpf = _rms_bwd(dhf, x1hat, r1, gpf)
        d_x1 = dyv + dn
        _, mhat, rm = _rms_fwd(mv, gpm)
        d_mix, dgpm = _rms_bwd(d_x1, mhat, rm, gpm)
        return d_x1, d_mix, dgpf, dgpm

    d_x1, d_mix, g_pre_ffn, g_post_mix = _rowwise(residual1_bwd, [dy, d_hf, x1, mix], [vec["pre_ffn_norm"], vec["post_mix_norm"]],
                                                  [(d, F32), (d, BF16)], [d, d], "residual1_bwd", after=started)
    d_cat = _matmul([(d_mix, w["w_out"])], "nt", BF16, "out_proj_dx")
    dw_out = _matmul([(cat, d_mix)], "tn", BF16, "out_proj_dw")

    def mix_in_bwd(dc, u, at, lg, lb, cg, ag_):
        dc = dc.astype(F32)
        uhat, rstd, u2, sg, u3 = conv_post(u, lg, lb)
        _, u3hat, rc = _rms_fwd(u3, cg)
        d_u3, dcg = _rms_bwd(dc[:, :CONV_CH], u3hat, rc, cg)
        d_u2 = d_u3 * sg * (1.0 + u2 * (1.0 - sg))
        dgl = d_u2 * lg
        d_u1 = rstd * (dgl - _mean(dgl) - uhat * _mean(dgl * uhat))
        _, ahat, ra = _rms_fwd(at.astype(F32), ag_)
        d_at, dag = _rms_bwd(dc[:, CONV_CH:], ahat, ra, ag_)
        return d_u1, d_at, dcg, _colsum(d_u2 * uhat), _colsum(d_u2), dag

    d_u1, d_attn, g_conv_out, g_ln_g, g_ln_b, g_attn_out = _rowwise(
        mix_in_bwd, [d_cat, u1, attn], [vec["conv_ln_g"], vec["conv_ln_b"], vec["conv_out_norm"], vec["attn_out_norm"]],
        [(CONV_CH, F32), (ATTN_CH, BF16)], [CONV_CH] * 4, "mix_in_bwd")
    d_ag, d_conv_w, g_conv_b = _conv_bwd(d_u1, ag, w["conv_w"])
    d_qn_, d_qp_, d_kv, d_kpe2 = _attention_bwd(q, kv, kpe2, cos2, sin2, d_attn)
    d_q = jnp.concatenate([d_qn_, d_qp_], axis=1)
    d_qn = _matmul([(d_q, w["w_uq"])], "nt", BF16, "q_up_dx")
    dw_uq = _matmul([(qn, d_q)], "tn", BF16, "q_up_dw")
    d_kvn = _matmul([(d_kv, w["w_ukv"])], "nt", BF16, "kv_up_dx")
    dw_ukv = _matmul([(kvn, d_kv)], "tn", BF16, "kv_up_dw")

    def latents_bwd(z, dq, dk, dkp, c2, s2, qg, kvg):
        z = z.astype(F32)
        _, qhat, rq = _rms_fwd(z[:, :Q_LORA], qg)
        d_ql, dqg = _rms_bwd(dq.astype(F32), qhat, rq, qg)
        _, khat, rk = _rms_fwd(z[:, Q_LORA:Q_LORA + KV_LORA], kvg)
        d_kl, dkg = _rms_bwd(dk.astype(F32), khat, rk, kvg)
        both = dkp + pltpu.roll(dkp, QK_ROPE, 1)
        d_kr = _rope_transposed(both, c2, s2) * _half_mask(both.shape, 0)
        return jnp.concatenate([d_ql, d_kl, d_kr], axis=1), dqg, dkg

    d_z2, g_q_norm, g_kv_norm = _rowwise(latents_bwd, [z2, d_qn, d_kvn, d_kpe2, cos2, sin2], [vec["q_norm"], vec["kv_norm"]],
                                         [(Z2_COLS, BF16)], [Q_LORA, KV_LORA], "latents_bwd")
    d_h = _matmul([(d_ag, w["w_ag"]), (d_z2, w["w_z2"])], "nt", F32, "in_proj_dx")
    dw_ag = _matmul([(h, d_ag)], "tn", BF16, "in_proj_ag_dw")
    dw_z2 = _matmul([(h, d_z2)], "tn", BF16, "in_proj_z2_dw")

    def pre_mix_bwd(dx1, dh, xv, g):
        _, xhat, r = _rms_fwd(xv, g)
        dn, dg = _rms_bwd(dh, xhat, r, g)
        return dx1 + dn, dg

    grad_x, g_pre_mix = _rowwise(pre_mix_bwd, [d_x1, d_h, x], [vec["pre_mix_norm"]], [(d, F32)], [d], "pre_mix_norm_bwd")

    dw = dict(w_ag=dw_ag, w_z2=dw_z2, w_uq=dw_uq, w_ukv=dw_ukv, conv_w=d_conv_w, w_out=dw_out, w_gu=dw_gu, w_down=dw_down)
    dvec = dict(pre_mix_norm=g_pre_mix, q_norm=g_q_norm, kv_norm=g_kv_norm, conv_b=g_conv_b, conv_ln_g=g_ln_g,
                conv_ln_b=g_ln_b, conv_out_norm=g_conv_out, attn_out_norm=g_attn_out, post_mix_norm=g_post_mix,
                pre_ffn_norm=g_pre_ffn, post_ffn_norm=g_post_ffn)
    return loss, grad_x, dw, dvec


ANY = pl.BlockSpec(memory_space=pl.ANY)


def _place():
    x, y, c = lax.axis_index("x"), lax.axis_index("y"), lax.axis_index("c")
    chips = [(1 - x, y), (x, 1 - y), (1 - x, 1 - y)]
    return x, y, c, chips


def _to_parts(chip, pieces, dtype, name):
    r = pieces[0].shape[0]
    widths = [a.shape[1] for a in pieces]
    tr = r if r <= 512 else _first_divisor(r, (512, 256, 128))

    def body(p_ref, *refs):
        o_ref = refs[len(pieces)]
        off = 0
        for a_ref, wdt in zip(refs, widths):
            o_ref[:, off:off + wdt] = a_ref[...].astype(o_ref.dtype)
            off += wdt

    return pl.pallas_call(
        body,
        name=name,
        grid_spec=pltpu.PrefetchScalarGridSpec(
            num_scalar_prefetch=1,
            grid=(r // tr,),
            in_specs=[pl.BlockSpec((tr, wdt), lambda i, p_ref: (i, 0)) for wdt in widths],
            out_specs=pl.BlockSpec((None, tr, sum(widths)), lambda i, p_ref: (p_ref[0], i, 0))),
        out_shape=jax.ShapeDtypeStruct((N_CHIPS, r, sum(widths)), dtype),
        compiler_params=_params(("parallel",)),
    )(chip, *pieces)


HBM = pl.BlockSpec(memory_space=pltpu.HBM)
SEM = pl.BlockSpec(memory_space=pltpu.SEMAPHORE)
EFFECT = pltpu.SideEffectType.DATAFLOW_SIDE_EFFECTING
VMEM_SPEC = pl.BlockSpec(memory_space=pltpu.VMEM)
TOKEN = jax.ShapeDtypeStruct((8, LANES), F32)


def _in_hbm(a):
    return pltpu.with_memory_space_constraint(a, pltpu.HBM)


def _gather_rows(buf, whole, half):
    r = buf.shape[1]
    return pl.ds(0, r) if whole else pl.ds(half * (r // 2), r // 2)


def _gather_start(bufs, whole, groups, name, after=()):
    n = len(bufs)
    n_g = len(groups)

    def body(*refs):
        refs = refs[:n] + refs[n + len(after):]
        sems = refs[n:n + 2 * n_g]
        outs = refs[n + 2 * n_g:2 * n + 2 * n_g]
        token = refs[2 * n + 2 * n_g]
        token[...] = jnp.zeros_like(token)
        x, y, c, chips = _place()
        p = 2 * x + y
        for gi, group in enumerate(groups):
            for ki, k in enumerate(group):
                blk = outs[k].at[p, _gather_rows(bufs[k], whole[k], c), :]
                for j, (px, py) in enumerate(chips):
                    pltpu.make_async_remote_copy(src_ref=blk, dst_ref=blk, send_sem=sems[2 * gi].at[3 * ki + j],
                                                 recv_sem=sems[2 * gi + 1].at[3 * ki + j],
                                                 device_id=(px, py, c), device_id_type=MESH).start()

    sem_shapes = []
    for group in groups:
        sem_shapes += [pltpu.SemaphoreType.DMA((3 * len(group),))] * 2
    res = pl.pallas_call(
        body,
        name=name,
        in_specs=[HBM] * n + [ANY] * len(after),
        out_specs=[SEM] * (2 * n_g) + [HBM] * n + [VMEM_SPEC],
        out_shape=sem_shapes + [pltpu.HBM(a.shape, a.dtype) for a in bufs] + [TOKEN],
        input_output_aliases={k: 2 * n_g + k for k in range(n)},
        compiler_params=pltpu.CompilerParams(has_side_effects=EFFECT),
    )(*[_in_hbm(a) for a in bufs], *after)
    sems = [(res[2 * gi], res[2 * gi + 1]) for gi in range(n_g)]
    return sems, list(res[2 * n_g:2 * n_g + n]), res[2 * n_g + n]


def _gather_wait(bufs, whole, send, recv, after, name):
    n = len(bufs)

    def body(*refs):
        ins = refs[:n]
        send_ref, recv_ref = refs[n], refs[n + 1]
        x, y, c, chips = _place()
        p = 2 * x + y
        for ki in range(n):
            rows = _gather_rows(bufs[ki], whole[ki], c)
            for j, (px, py) in enumerate(chips):
                cp = pltpu.make_async_remote_copy(src_ref=ins[ki].at[p, rows, :], dst_ref=ins[ki].at[2 * px + py, rows, :],
                                                  send_sem=send_ref.at[3 * ki + j], recv_sem=recv_ref.at[3 * ki + j],
                                                  device_id=(px, py, c), device_id_type=MESH)
                cp.wait_send()
                cp.wait_recv()

    res = pl.pallas_call(
        body,
        name=name,
        in_specs=[HBM] * n + [SEM, SEM] + [ANY] * len(after),
        out_specs=[HBM] * n,
        out_shape=[pltpu.HBM(a.shape, a.dtype) for a in bufs],
        input_output_aliases={k: k for k in range(n)},
        compiler_params=pltpu.CompilerParams(has_side_effects=EFFECT),
    )(*bufs, send, recv, *after)
    return list(res)


def _gather_hand_on(bufs, name, after=()):
    n = len(bufs)

    def body(*refs):
        refs = refs[n + len(after):]
        outs = refs[:n]
        send, recv = refs[n:]
        x, y, c, chips = _place()

        def d2d(k, j, half):
            px, py = chips[j]
            blk = outs[k].at[2 * px + py, _gather_rows(bufs[k], False, half), :]
            return pltpu.make_async_remote_copy(src_ref=blk, dst_ref=blk, send_sem=send.at[3 * k + j], recv_sem=recv.at[3 * k + j],
                                                device_id=(x, y, 1 - c), device_id_type=MESH)

        sent = [d2d(k, j, c) for k in range(n) for j in range(3)]
        for cp in sent:
            cp.start()
        for k in range(n):
            for j in range(3):
                d2d(k, j, 1 - c).wait_recv()
        for cp in sent:
            cp.wait_send()

    return pl.pallas_call(
        body,
        name=name,
        in_specs=[ANY] * (n + len(after)),
        out_specs=[ANY] * n,
        out_shape=[jax.ShapeDtypeStruct(a.shape, a.dtype) for a in bufs],
        input_output_aliases={k: k for k in range(n)},
        scratch_shapes=[pltpu.SemaphoreType.DMA((3 * n,)), pltpu.SemaphoreType.DMA((3 * n,))],
        compiler_params=pltpu.CompilerParams(has_side_effects=True),
    )(*bufs, *after)


def _hand_on_start(bufs, name, after=()):
    n = len(bufs)

    def body(*refs):
        refs = refs[:n] + refs[n + len(after):]
        send, recv = refs[n], refs[n + 1]
        outs = refs[n + 2:2 * n + 2]
        refs[2 * n + 2][...] = jnp.zeros(TOKEN.shape, TOKEN.dtype)
        x, y, c, chips = _place()
        for k in range(n):
            for j, (px, py) in enumerate(chips):
                blk = outs[k].at[2 * px + py, _gather_rows(bufs[k], False, c), :]
                pltpu.make_async_remote_copy(src_ref=blk, dst_ref=blk, send_sem=send.at[3 * k + j], recv_sem=recv.at[3 * k + j],
                                             device_id=(x, y, 1 - c), device_id_type=MESH).start()

    res = pl.pallas_call(
        body,
        name=name,
        in_specs=[HBM] * n + [ANY] * len(after),
        out_specs=[SEM, SEM] + [HBM] * n + [VMEM_SPEC],
        out_shape=[pltpu.SemaphoreType.DMA((3 * n,))] * 2 + [pltpu.HBM(a.shape, a.dtype) for a in bufs] + [TOKEN],
        input_output_aliases={k: 2 + k for k in range(n)},
        compiler_params=pltpu.CompilerParams(has_side_effects=EFFECT),
    )(*[_in_hbm(a) for a in bufs], *after)
    return res[0], res[1], list(res[2:2 + n]), res[2 + n]


def _hand_on_wait(bufs, send, recv, after, name):
    n = len(bufs)

    def body(*refs):
        ins = refs[:n]
        send_ref, recv_ref = refs[n], refs[n + 1]
        x, y, c, chips = _place()
        for k in range(n):
            for j, (px, py) in enumerate(chips):
                q = 2 * px + py
                cp = pltpu.make_async_remote_copy(src_ref=ins[k].at[q, _gather_rows(bufs[k], False, c), :],
                                                  dst_ref=ins[k].at[q, _gather_rows(bufs[k], False, 1 - c), :],
                                                  send_sem=send_ref.at[3 * k + j], recv_sem=recv_ref.at[3 * k + j],
                                                  device_id=(x, y, 1 - c), device_id_type=MESH)
                cp.wait_send()
                cp.wait_recv()

    res = pl.pallas_call(
        body,
        name=name,
        in_specs=[HBM] * n + [SEM, SEM] + [ANY] * len(after),
        out_specs=[HBM] * n,
        out_shape=[pltpu.HBM(a.shape, a.dtype) for a in bufs],
        input_output_aliases={k: k for k in range(n)},
        compiler_params=pltpu.CompilerParams(has_side_effects=EFFECT),
    )(*bufs, send, recv, *after)
    return list(res)


def _pair_exchange_start(part, name):
    rh = part.shape[1] // 2
    land_shape = (N_CHIPS, rh, part.shape[2])

    def body(part_ref, land_ref, send, recv, part_out, land_out, token):
        token[...] = jnp.zeros_like(token)
        x, y, c, _ = _place()
        pltpu.make_async_remote_copy(src_ref=part_out.at[:, pl.ds((1 - c) * rh, rh), :], dst_ref=land_out,
                                     send_sem=send, recv_sem=recv, device_id=(x, y, 1 - c), device_id_type=MESH).start()

    res = pl.pallas_call(
        body,
        name=name,
        in_specs=[HBM, HBM],
        out_specs=[SEM, SEM, HBM, HBM, VMEM_SPEC],
        out_shape=[pltpu.SemaphoreType.DMA(()), pltpu.SemaphoreType.DMA(()), pltpu.HBM(part.shape, part.dtype),
                   pltpu.HBM(land_shape, part.dtype), TOKEN],
        input_output_aliases={0: 2, 1: 3},
        compiler_params=pltpu.CompilerParams(has_side_effects=EFFECT),
    )(_in_hbm(part), _in_hbm(lax.empty(land_shape, part.dtype)))
    return res


def _pair_exchange_wait(send, recv, part, land, after, name):
    rh = part.shape[1] // 2

    def body(part_ref, land_ref, send_ref, recv_ref, after_ref, part_out, land_out):
        x, y, c, _ = _place()
        cp = pltpu.make_async_remote_copy(src_ref=part_ref.at[:, pl.ds((1 - c) * rh, rh), :], dst_ref=land_ref,
                                          send_sem=send_ref, recv_sem=recv_ref, device_id=(x, y, 1 - c), device_id_type=MESH)
        cp.wait_send()
        cp.wait_recv()

    return pl.pallas_call(
        body,
        name=name,
        in_specs=[HBM, HBM, SEM, SEM, ANY],
        out_specs=[HBM, HBM],
        out_shape=[pltpu.HBM(part.shape, part.dtype), pltpu.HBM(land.shape, land.dtype)],
        input_output_aliases={0: 0, 1: 1},
        compiler_params=pltpu.CompilerParams(has_side_effects=EFFECT),
    )(part, land, send, recv, after)


def _chip_exchange_start(sums, name):
    n = len(sums)

    def body(*refs):
        send, recv = refs[2 * n], refs[2 * n + 1]
        src = refs[2 * n + 2:3 * n + 2]
        dst = refs[3 * n + 2:4 * n + 2]
        refs[4 * n + 2][...] = jnp.zeros(TOKEN.shape, TOKEN.dtype)
        x, y, c, chips = _place()
        p = 2 * x + y
        for k in range(n):
            for j, (px, py) in enumerate(chips):
                pltpu.make_async_remote_copy(src_ref=src[k].at[2 * px + py], dst_ref=dst[k].at[p],
                                             send_sem=send.at[3 * k + j], recv_sem=recv.at[3 * k + j],
                                             device_id=(px, py, c), device_id_type=MESH).start()

    res = pl.pallas_call(
        body,
        name=name,
        in_specs=[HBM] * (2 * n),
        out_specs=[SEM, SEM] + [HBM] * (2 * n) + [VMEM_SPEC],
        out_shape=[pltpu.SemaphoreType.DMA((3 * n,))] * 2 + [pltpu.HBM(a.shape, a.dtype) for a in sums] * 2 + [TOKEN],
        input_output_aliases={k: 2 + k for k in range(2 * n)},
        compiler_params=pltpu.CompilerParams(has_side_effects=EFFECT),
    )(*[_in_hbm(a) for a in sums], *[_in_hbm(lax.empty(a.shape, a.dtype)) for a in sums])
    return res[0], res[1], list(res[2:2 + n]), list(res[2 + n:2 + 2 * n]), res[2 + 2 * n]


def _chip_exchange_wait(sums, slots, send, recv, after, name):
    n = len(sums)

    def body(*refs):
        src, dst = refs[:n], refs[n:2 * n]
        send_ref, recv_ref = refs[2 * n], refs[2 * n + 1]
        x, y, c, chips = _place()
        for k in range(n):
            for j, (px, py) in enumerate(chips):
                cp = pltpu.make_async_remote_copy(src_ref=src[k].at[2 * px + py], dst_ref=dst[k].at[2 * px + py],
                                                  send_sem=send_ref.at[3 * k + j], recv_sem=recv_ref.at[3 * k + j],
                                                  device_id=(px, py, c), device_id_type=MESH)
                cp.wait_send()
                cp.wait_recv()

    res = pl.pallas_call(
        body,
        name=name,
        in_specs=[HBM] * (2 * n) + [SEM, SEM] + [ANY] * len(after),
        out_specs=[HBM] * (2 * n),
        out_shape=[pltpu.HBM(a.shape, a.dtype) for a in sums] * 2,
        input_output_aliases={k: k for k in range(2 * n)},
        compiler_params=pltpu.CompilerParams(has_side_effects=EFFECT),
    )(*sums, *slots, send, recv, *after)
    return list(res[:n]), list(res[n:])


def _pair_exchange(parts, name):
    n = len(parts)

    def body(*refs):
        ins, outs = refs[:n], refs[n:2 * n]
        send, recv = refs[2 * n:]
        x, y, c, _ = _place()
        copies = []
        for k in range(n):
            rh = parts[k].shape[1] // 2
            cp = pltpu.make_async_remote_copy(
                src_ref=ins[k].at[:, pl.ds((1 - c) * rh, rh), :], dst_ref=outs[k],
                send_sem=send.at[k], recv_sem=recv.at[k], device_id=(x, y, 1 - c), device_id_type=MESH)
            cp.start()
            copies.append(cp)
        for cp in copies:
            cp.wait()

    return pl.pallas_call(
        body,
        name=name,
        in_specs=[ANY] * n,
        out_specs=[ANY] * n,
        out_shape=[jax.ShapeDtypeStruct((N_CHIPS, a.shape[1] // 2, a.shape[2]), a.dtype) for a in parts],
        scratch_shapes=[pltpu.SemaphoreType.DMA((n,)), pltpu.SemaphoreType.DMA((n,))],
        compiler_params=pltpu.CompilerParams(has_side_effects=True),
    )(*parts)


def _pair_sum(core, part, landed, name):
    _, r, cdim = part.shape
    rh = r // 2
    tr = _first_divisor(rh, (256, 128, 64, 32, 16))
    nb = rh // tr

    def body(c_ref, a_ref, b_ref, o_ref):
        o_ref[...] = (a_ref[...].astype(F32) + b_ref[...].astype(F32)).astype(o_ref.dtype)

    return pl.pallas_call(
        body,
        name=name,
        grid_spec=pltpu.PrefetchScalarGridSpec(
            num_scalar_prefetch=1,
            grid=(N_CHIPS, nb),
            in_specs=[pl.BlockSpec((None, tr, cdim), lambda q, i, c_ref: (q, c_ref[0] * nb + i, 0)),
                      pl.BlockSpec((None, tr, cdim), lambda q, i, c_ref: (q, i, 0))],
            out_specs=pl.BlockSpec((None, tr, cdim), lambda q, i, c_ref: (q, i, 0))),
        out_shape=jax.ShapeDtypeStruct((N_CHIPS, rh, cdim), BF16),
        compiler_params=_params(("parallel", "parallel")),
    )(core, part, landed)


def _chip_sum(place, own, slots, name):
    _, rh, cdim = slots.shape
    tr = _first_divisor(rh, (256, 128, 64, 32, 16))
    nb = rh // tr

    def body(place_ref, own_ref, s1_ref, s2_ref, s3_ref, o_ref):
        acc = own_ref[...].astype(F32)
        for s_ref in (s1_ref, s2_ref, s3_ref):
            acc = acc + s_ref[...].astype(F32)
        o_ref[...] = acc

    def other(j):
        return lambda i, place_ref: ((place_ref[0] + j) % N_CHIPS, i, 0)

    return pl.pallas_call(
        body,
        name=name,
        grid_spec=pltpu.PrefetchScalarGridSpec(
            num_scalar_prefetch=1,
            grid=(nb,),
            in_specs=[pl.BlockSpec((None, tr, cdim), other(0))] + [pl.BlockSpec((None, tr, cdim), other(j)) for j in (1, 2, 3)],
            out_specs=pl.BlockSpec((tr, cdim), lambda i, place_ref: (place_ref[1] * nb + i, 0))),
        out_shape=jax.ShapeDtypeStruct((2 * rh, cdim), F32),
        compiler_params=_params(("parallel",)),
    )(place, own, slots, slots, slots)


def _half_exchange(bufs, name):
    n = len(bufs)

    def body(*refs):
        outs = refs[n:2 * n]
        send, recv = refs[2 * n:]
        x, y, c, _ = _place()
        copies = []
        for k in range(n):
            rh = bufs[k].shape[0] // 2
            mine = outs[k].at[pl.ds(c * rh, rh), :]
            cp = pltpu.make_async_remote_copy(src_ref=mine, dst_ref=mine, send_sem=send.at[k], recv_sem=recv.at[k],
                                              device_id=(x, y, 1 - c), device_id_type=MESH)
            cp.start()
            copies.append(cp)
        for k in range(n):
            rh = bufs[k].shape[0] // 2
            theirs = outs[k].at[pl.ds((1 - c) * rh, rh), :]
            copies[k].wait_send()
            pltpu.make_async_remote_copy(src_ref=theirs, dst_ref=theirs, send_sem=send.at[k], recv_sem=recv.at[k],
                                         device_id=(x, y, 1 - c), device_id_type=MESH).wait_recv()

    return pl.pallas_call(
        body,
        name=name,
        in_specs=[ANY] * n,
        out_specs=[ANY] * n,
        out_shape=[jax.ShapeDtypeStruct(a.shape, a.dtype) for a in bufs],
        input_output_aliases={k: k for k in range(n)},
        scratch_shapes=[pltpu.SemaphoreType.DMA((n,)), pltpu.SemaphoreType.DMA((n,))],
        compiler_params=pltpu.CompilerParams(has_side_effects=True),
    )(*bufs)


SMALL_ROWS = 32


def _all_reduce_small(pack):
    def body(in_ref, out_ref, gath, send, recv):
        x, y, c, _ = _place()
        me = 4 * x + 2 * y + c
        gath[me] = in_ref[...]
        copies = []
        for k in range(1, N_DEV):
            dx, dy, dc = (k >> 2) & 1, (k >> 1) & 1, k & 1
            peer = (x ^ dx, y ^ dy, c ^ dc)
            cp = pltpu.make_async_remote_copy(src_ref=in_ref, dst_ref=gath.at[me], send_sem=send.at[k], recv_sem=recv.at[k],
                                              device_id=peer, device_id_type=MESH)
            cp.start()
            copies.append((cp, 4 * peer[0] + 2 * peer[1] + peer[2]))
        for k, (cp, peer_id) in enumerate(copies, start=1):
            cp.wait_send()
            pltpu.make_async_remote_copy(src_ref=in_ref, dst_ref=gath.at[peer_id], send_sem=send.at[k], recv_sem=recv.at[k],
                                         device_id=(x, y, c), device_id_type=MESH).wait_recv()
        acc = gath[0]
        for dev in range(1, N_DEV):
            acc = acc + gath[dev]
        out_ref[...] = acc

    return pl.pallas_call(
        body,
        name="all_reduce_small",
        in_specs=[pl.BlockSpec(memory_space=pltpu.VMEM)],
        out_specs=pl.BlockSpec(memory_space=pltpu.VMEM),
        out_shape=jax.ShapeDtypeStruct(pack.shape, F32),
        scratch_shapes=[pltpu.VMEM((N_DEV,) + pack.shape, F32), pltpu.SemaphoreType.DMA((N_DEV,)), pltpu.SemaphoreType.DMA((N_DEV,))],
        compiler_params=pltpu.CompilerParams(has_side_effects=True, vmem_limit_bytes=VMEM_LIMIT),
    )(pack)


def _adamw_update(g_ref, w_ref, m_ref, v_ref, go_ref, d_ref, mo_ref, vo_ref):
    bc1 = 1.0 - ADAM_B1 ** ADAM_STEP
    bc2 = 1.0 - ADAM_B2 ** ADAM_STEP
    gv = g_ref[...]
    mn = ADAM_B1 * m_ref[...] + (1.0 - ADAM_B1) * gv
    vn = ADAM_B2 * v_ref[...] + (1.0 - ADAM_B2) * (gv * gv)
    go_ref[...] = gv
    mo_ref[...] = mn
    vo_ref[...] = vn
    d_ref[...] = -ADAM_LR * ((mn / bc1) / (jnp.sqrt(vn / bc2) + ADAM_EPS) + ADAM_WD * w_ref[...])


def _adamw_small(gs, ws, ms, vs):
    n = len(gs)

    def body(*refs):
        ins, outs = refs[:4 * n], refs[4 * n:]
        for k in range(n):
            _adamw_update(*[ins[i * n + k] for i in range(4)], *outs[4 * k:4 * k + 4])

    vmem = pl.BlockSpec(memory_space=pltpu.VMEM)
    res = pl.pallas_call(
        body,
        name="adamw_small",
        in_specs=[vmem] * (4 * n),
        out_specs=[vmem] * (4 * n),
        out_shape=[jax.ShapeDtypeStruct(w.shape, F32) for w in ws for _ in range(4)],
        compiler_params=pltpu.CompilerParams(vmem_limit_bytes=VMEM_LIMIT),
    )(*gs, *ws, *ms, *vs)
    return [tuple(res[4 * k:4 * k + 4]) for k in range(n)]


def _adamw(g, w, m, v, name, g_block=0):
    r, cdim = w.shape
    tr = r if r * cdim * 4 <= (1 << 20) else _first_divisor(r, (128, 64, 32, 16, 8))

    def body(*refs):
        _adamw_update(*refs)

    spec = pl.BlockSpec((tr, cdim), lambda i: (i, 0))
    return pl.pallas_call(
        body,
        name=name,
        grid=(r // tr,),
        in_specs=[pl.BlockSpec((tr, cdim), lambda i: (i, g_block))] + [spec] * 3,
        out_specs=[spec] * 4,
        out_shape=[jax.ShapeDtypeStruct((r, cdim), F32)] * 4,
        compiler_params=_params(("parallel",)),
    )(g, w, m, v)


VEC_NAMES = ["pre_mix_norm", "q_norm", "kv_norm", "conv_b", "conv_ln_g", "conv_ln_b", "conv_out_norm",
             "attn_out_norm", "post_mix_norm", "pre_ffn_norm", "post_ffn_norm"]
LOSS_ROW = len(VEC_NAMES)
CONV_W_ROW = 16


def _cols_to_full(parts):
    _, r, cdim = parts.shape
    return parts.transpose(1, 0, 2).reshape(r, N_CHIPS * cdim)


def _full_to_cols(full):
    r, n = full.shape
    return full.reshape(r, N_CHIPS, n // N_CHIPS).transpose(1, 0, 2)


W_IN_SHARD = (2 * CONV_CH + Q_LORA + KV_LORA + QK_ROPE) // N_CHIPS
W_IN_PART = 1024
W_IN_BLOCKS = (2 * CONV_CH + Z2_COLS) // LANES
W_IN_BASE = [p * W_IN_SHARD // LANES for p in range(N_CHIPS)]
W_IN_SPAN = [-(-(p * W_IN_SHARD % LANES + W_IN_SHARD) // LANES) for p in range(N_CHIPS)]


def _w_in_block_home(b):
    n = CONV_CH // LANES
    if b < n:
        return 0, 2 * b
    if b < 2 * n:
        return 0, 2 * (b - n) + 1
    return 1, b - 2 * n


def _to_parts_w_in(shift_chip, w_in):
    r = w_in.shape[0]
    tr = 512

    def body(s_ref, a_ref, o_ref):
        o_ref[...] = jnp.zeros_like(o_ref)
        o_ref[:, :W_IN_SHARD] = a_ref[...].astype(o_ref.dtype)
        o_ref[...] = pltpu.roll(o_ref[...].astype(F32), s_ref[0], 1).astype(o_ref.dtype)

    return pl.pallas_call(
        body,
        name="to_parts_w_in",
        grid_spec=pltpu.PrefetchScalarGridSpec(
            num_scalar_prefetch=1,
            grid=(r // tr,),
            in_specs=[pl.BlockSpec((tr, W_IN_SHARD), lambda i, s_ref: (i, 0))],
            out_specs=pl.BlockSpec((None, tr, W_IN_PART), lambda i, s_ref: (s_ref[1], i, 0))),
        out_shape=jax.ShapeDtypeStruct((N_CHIPS, r, W_IN_PART), BF16),
        compiler_params=_params(("parallel",)),
    )(shift_chip, w_in)


def _assemble_w_in(parts):
    r = parts.shape[1]
    tr = ROW_TILE

    def body(p_ref, ag_ref, z2_ref):
        outs = (ag_ref, z2_ref)
        for b in range(W_IN_BLOCKS):
            blk = None
            for p in range(N_CHIPS):
                i = b - W_IN_BASE[p]
                if 0 <= i < W_IN_SPAN[p]:
                    piece = p_ref[p, :, i * LANES:(i + 1) * LANES]
                    blk = piece if blk is None else blk + piece
            which, at = _w_in_block_home(b)
            outs[which][:, at * LANES:(at + 1) * LANES] = blk

    w_ag, w_z2 = pl.pallas_call(
        body,
        name="assemble_w_in",
        grid=(r // tr,),
        in_specs=[pl.BlockSpec((N_CHIPS, tr, W_IN_PART), lambda i: (0, i, 0))],
        out_specs=[pl.BlockSpec((tr, 2 * CONV_CH), lambda i: (i, 0)), pl.BlockSpec((tr, Z2_COLS), lambda i: (i, 0))],
        out_shape=[jax.ShapeDtypeStruct((r, 2 * CONV_CH), parts.dtype), jax.ShapeDtypeStruct((r, Z2_COLS), parts.dtype)],
        compiler_params=_params(("parallel",)),
    )(parts)
    return dict(w_ag=w_ag, w_z2=w_z2)


def _w_in_grad_parts(dw_ag, dw_z2):
    r = dw_ag.shape[0]
    tr = ROW_TILE

    def body(ag_ref, z2_ref, o_ref):
        ins = (ag_ref, z2_ref)
        for p in range(N_CHIPS):
            for i in range(W_IN_PART // LANES):
                if i < W_IN_SPAN[p]:
                    which, at = _w_in_block_home(W_IN_BASE[p] + i)
                    o_ref[p, :, i * LANES:(i + 1) * LANES] = ins[which][:, at * LANES:(at + 1) * LANES]
                else:
                    o_ref[p, :, i * LANES:(i + 1) * LANES] = jnp.zeros((tr, LANES), o_ref.dtype)

    return pl.pallas_call(
        body,
        name="w_in_grad_parts",
        grid=(r // tr,),
        in_specs=[pl.BlockSpec((tr, 2 * CONV_CH), lambda i: (i, 0)), pl.BlockSpec((tr, Z2_COLS), lambda i: (i, 0))],
        out_specs=pl.BlockSpec((N_CHIPS, tr, W_IN_PART), lambda i: (0, i, 0)),
        out_shape=jax.ShapeDtypeStruct((N_CHIPS, r, W_IN_PART), dw_ag.dtype),
        compiler_params=_params(("parallel",)),
    )(dw_ag, dw_z2)


def _assemble_mixer_rest(g):
    uq = _cols_to_full(g["w_uq"]).reshape(Q_LORA, N_HEADS, QK_HEAD)
    w_uq = jnp.concatenate([uq[:, :, :QK_NOPE].reshape(Q_LORA, N_HEADS * QK_NOPE),
                            uq[:, :, QK_NOPE:].reshape(Q_LORA, N_HEADS * QK_ROPE)], axis=1)
    return dict(w_uq=w_uq, w_ukv=_cols_to_full(g["w_ukv"]), conv_w=_cols_to_full(g["conv_w"]),
                w_out=g["w_out"].reshape(-1, g["w_out"].shape[2]))


def _grads_to_parts(dw):
    uq = dw["w_uq"]
    d_uq = jnp.concatenate([uq[:, :N_HEADS * QK_NOPE].reshape(Q_LORA, N_HEADS, QK_NOPE),
                            uq[:, N_HEADS * QK_NOPE:].reshape(Q_LORA, N_HEADS, QK_ROPE)], axis=2).reshape(Q_LORA, N_HEADS * QK_HEAD)
    return dict(w_in=_w_in_grad_parts(dw["w_ag"], dw["w_z2"]), w_uq=_full_to_cols(d_uq), w_ukv=_full_to_cols(dw["w_ukv"]),
                w_out=dw["w_out"].reshape(N_CHIPS, -1, dw["w_out"].shape[1]))


MIXER = ["w_in", "w_uq", "w_ukv", "w_out"]
FFN = ["w_gu", "w_down"]
BIG = MIXER + FFN


def _pad_lanes(v, n):
    return jnp.pad(v, ((0, 0), (0, n - v.shape[1])))


def kernel(x, positions, pre_mix_norm, w_in, q_norm, w_uq, kv_norm, w_ukv, conv_w, conv_b, conv_ln_g, conv_ln_b, conv_out_norm, attn_out_norm, w_out, post_mix_norm, pre_ffn_norm, w_gate, w_up, w_down, post_ffn_norm, loss_target, m_pre_mix_norm, m_w_in, m_q_norm, m_w_uq, m_kv_norm, m_w_ukv, m_conv_w, m_conv_b, m_conv_ln_g, m_conv_ln_b, m_conv_out_norm, m_attn_out_norm, m_w_out, m_post_mix_norm, m_pre_ffn_norm, m_w_gate, m_w_up, m_w_down, m_post_ffn_norm, v_pre_mix_norm, v_w_in, v_q_norm, v_w_uq, v_kv_norm, v_w_ukv, v_conv_w, v_conv_b, v_conv_ln_g, v_conv_ln_b, v_conv_out_norm, v_attn_out_norm, v_w_out, v_post_mix_norm, v_pre_ffn_norm, v_w_gate, v_w_up, v_w_down, v_post_ffn_norm):
    given = dict(locals())
    names = ["pre_mix_norm", "w_in", "q_norm", "w_uq", "kv_norm", "w_ukv", "conv_w", "conv_b", "conv_ln_g", "conv_ln_b",
             "conv_out_norm", "attn_out_norm", "w_out", "post_mix_norm", "pre_ffn_norm", "w_gate", "w_up", "w_down", "post_ffn_norm"]
    def as_2d(a):
        return a if a.ndim == 2 else a[0]

    weights = {n: as_2d(given[n]) for n in names}
    mom = {n: as_2d(given["m_" + n]) for n in names}
    var = {n: as_2d(given["v_" + n]) for n in names}
    d = D_MODEL

    inv_freq = ROPE_THETA ** (-jnp.arange(0, QK_ROPE, 2, dtype=F32) / QK_ROPE)
    ang = positions[0].astype(F32)[:, None] * inv_freq
    cos, sin = jnp.cos(ang), jnp.sin(ang)
    cos2 = jnp.concatenate([cos, cos, cos, cos], axis=1)
    sin2 = jnp.concatenate([-sin, sin, -sin, sin], axis=1)

    chip = 2 * lax.axis_index("x") + lax.axis_index("y")
    core = lax.axis_index("c")
    chip1 = chip.astype(jnp.int32).reshape(1)
    pieces = {n: [weights[n]] for n in BIG if n != "w_gu"}
    pieces["w_gu"] = [weights["w_gate"], weights["w_up"]]
    core1 = core.astype(jnp.int32).reshape(1)
    place = jnp.stack([chip, core]).astype(jnp.int32)
    rest = ["w_uq", "w_ukv", "w_out"]
    w_in_shift = (chip * W_IN_SHARD) % LANES
    w_in_buf = _to_parts_w_in(jnp.stack([w_in_shift, chip]).astype(jnp.int32), weights["w_in"])
    (w_in_sems,), w_in_thru, _ = _gather_start([w_in_buf], [False], [[0]], "gather_start_w_in")
    rest_bufs = [_to_parts(chip1, pieces[n], BF16, "to_parts_" + n) for n in rest]
    rest_bufs.append(_to_parts(chip1, [jnp.pad(weights["conv_w"], ((0, CONV_K_PAD - CONV_K), (0, 0)))], F32, "to_parts_conv_w"))
    rest_whole = [False] * 3 + [True]
    ffn_bufs = [_to_parts(chip1, pieces[n], BF16, "to_parts_" + n) for n in FFN]
    got = _gather_wait(w_in_thru, [False], *w_in_sems, rest_bufs + ffn_bufs, "gather_wait_w_in")
    (rest_sems,), rest_thru, started = _gather_start(rest_bufs, rest_whole, [[0, 1, 2, 3]], "gather_start_mixer_rest", after=got)
    full = _assemble_w_in(_gather_hand_on(got, "gather_hand_on_w_in", after=[started])[0])
    vec = {n: weights[n] for n in VEC_NAMES}
    rs = {}

    def get_mixer_rest(after):
        got = _gather_wait(rest_thru, rest_whole, *rest_sems, [after], "gather_wait_mixer_rest")
        (rs["w_gu_sems"],), rs["w_gu_thru"], started = _gather_start(ffn_bufs[:1], [False], [[0]], "gather_start_w_gu", after=got[:1])
        got = list(_gather_hand_on(got[:3], "gather_hand_on_mixer_rest", after=[started])) + [got[3]]
        return _assemble_mixer_rest(dict(zip(rest + ["conv_w"], got)))

    def w_gu_landed(after):
        got = _gather_wait(rs["w_gu_thru"], [False], *rs["w_gu_sems"], [after], "gather_wait_w_gu")
        (rs["w_down_sems"],), rs["w_down_thru"], started = _gather_start(ffn_bufs[1:], [False], [[0]], "gather_start_w_down", after=got)
        rs["w_gu"] = _hand_on_start(got, "hand_on_start_w_gu", after=[started])
        return rs["w_gu"][3]

    def get_w_gu(after):
        send, recv, bufs, _ = rs["w_gu"]
        return _hand_on_wait(bufs, send, recv, [after], "hand_on_wait_w_gu")[0]

    def get_w_down(after):
        got = _gather_wait(rs["w_down_thru"], [False], *rs["w_down_sems"], [after], "gather_wait_w_down")
        got = _gather_hand_on(got, "gather_hand_on_w_down")[0]
        return got.reshape(-1, got.shape[2])

    def pair_start(key):
        def hook(dw):
            rs[key] = _pair_exchange_start(dw.reshape(N_CHIPS, -1, dw.shape[-1]), "grad_pair_start_" + key)
            return rs[key][4]
        return hook

    def reduce_start(group, plist, landed):
        sums = [_pair_sum(core1, a, b, "grad_pair_sum_%s_%d" % (group, k)) for k, (a, b) in enumerate(zip(plist, landed))]
        rs[group] = _chip_exchange_start(sums, "grad_chip_exchange_start_" + group)
        return rs[group][4]

    def reduce_finish(group, after):
        send, recv, sums, slots, _ = rs[group]
        sums, slots = _chip_exchange_wait(sums, slots, send, recv, after, "grad_chip_exchange_wait_" + group)
        halves = [_chip_sum(place, s, sl, "grad_chip_sum_%s_%d" % (group, k)) for k, (s, sl) in enumerate(zip(sums, slots))]
        return list(_half_exchange(halves, "grad_half_exchange_" + group))

    def ffn_grads_exchanged(after):
        pairs = [_pair_exchange_wait(*rs[key][:4], after, "grad_pair_wait_" + key) for key in ("dw_gu", "dw_down")]
        return reduce_start("ffn", [p[0] for p in pairs], [p[1] for p in pairs])

    hooks = dict(mixer_rest=get_mixer_rest, w_gu_landed=w_gu_landed, w_gu=get_w_gu, w_down=get_w_down, dw_down=pair_start("dw_down"), dw_gu=pair_start("dw_gu"),
                 grads_exchanged=ffn_grads_exchanged)
    loss, grad_x, dw, dvec = _local_step(x[0], loss_target[0], cos2, sin2, vec, full, hooks)

    rows = [_pad_lanes(dvec[n], d) for n in VEC_NAMES] + [_pad_lanes(loss, d)]
    rows.append(jnp.zeros((CONV_W_ROW - len(rows), d), F32))
    rows.append(dw["conv_w"].reshape(SMALL_ROWS - CONV_W_ROW, d))
    small = _all_reduce_small(jnp.concatenate(rows, axis=0))
    g_conv_w_full = small[CONV_W_ROW:].reshape(CONV_K_PAD, CONV_CH)
    g_small = {n: small[i:i + 1, :weights[n].shape[1]] for i, n in enumerate(VEC_NAMES)}
    g_small["conv_w"] = lax.dynamic_slice(g_conv_w_full, (0, chip * (CONV_CH // N_CHIPS)), (CONV_K_PAD, CONV_CH // N_CHIPS))[:CONV_K]
    loss_out = small[LOSS_ROW, 0]

    parts = _grads_to_parts(dw)
    plist = [parts[n] for n in MIXER]
    started = reduce_start("mixer", plist, _pair_exchange(plist, "grad_pair_exchange_mixer"))
    g_gu, g_down = reduce_finish("ffn", [started])
    res = {}
    for n, g, blk in (("w_gate", g_gu, 0), ("w_up", g_gu, 1), ("w_down", g_down, 0)):
        res[n] = _adamw(g, weights[n], mom[n], var[n], "adamw_" + n, g_block=blk)
    small_names = VEC_NAMES + ["conv_w"]
    res.update(zip(small_names, _adamw_small([g_small[n] for n in small_names], [weights[n] for n in small_names],
                                             [mom[n] for n in small_names], [var[n] for n in small_names])))
    done_meanwhile = [res["w_gate"][1], res["w_up"][1], res["w_down"][1], res["conv_w"][1], grad_x]
    g_mixer = reduce_finish("mixer", done_meanwhile)
    g_mixer[0] = lax.dynamic_slice(g_mixer[0], (0, w_in_shift), (g_mixer[0].shape[0], W_IN_SHARD))
    for n, g in zip(MIXER, g_mixer):
        res[n] = _adamw(g, weights[n], mom[n], var[n], "adamw_" + n)
    outs = [loss_out, grad_x[None]]
    for i in range(4):
        outs += [res[n][i].reshape(given[n].shape) for n in names]
    return tuple(outs)
```

```python
import functools

import jax
import jax.numpy as jnp
from jax import lax
from jax.experimental import pallas as pl
from jax.experimental.pallas import tpu as pltpu

F32 = jnp.float32
BF16 = jnp.bfloat16

D_MODEL = 2048
CONV_CH = 1024
CONV_K = 31
CONV_K_PAD = 32
N_HEADS = 8
QK_NOPE = 128
QK_ROPE = 64
V_HEAD = 128
QK_HEAD = QK_NOPE + QK_ROPE
Q_LORA = 768
KV_LORA = 512
ATTN_CH = N_HEADS * V_HEAD
Z2_COLS = Q_LORA + KV_LORA + 128
D_FF = 5632
ROPE_THETA = 10000.0
EPS = 1e-6
LANES = 128
N_CHIPS = 4
N_DEV = 8

ADAM_LR = 0.001
ADAM_B1 = 0.9
ADAM_B2 = 0.999
ADAM_EPS = 1e-08
ADAM_WD = 0.01
ADAM_STEP = 10

VMEM_LIMIT = 56 * 1024 * 1024
ROW_TILE = 256
MAX_TK = 2816
MESH = pl.DeviceIdType.MESH


def _params(sem=None):
    return pltpu.CompilerParams(dimension_semantics=sem, vmem_limit_bytes=VMEM_LIMIT)


def _first_divisor(n, cands):
    for c in cands:
        if n % c == 0:
            return c
    return n


STREAM_BLOCK_BYTES = 3 << 19


def _row_tile(rows, cols, itemsize):
    for tr in (1024, 704, 512, 384, 352, 256, 176, 128, 64, 32, 16):
        if rows % tr == 0 and tr * cols * itemsize <= STREAM_BLOCK_BYTES:
            return tr
    return rows


def _matmul(pairs, mode, out_dtype, name, b_parts=None, out_parts=False, tiles=(None, None, None), after=None):
    a0, b0 = pairs[0]
    part_c = b0.shape[2] if b_parts else None
    if mode == "nn":
        m, n = a0.shape[0], (N_CHIPS * part_c if b_parts else b0.shape[1])
        ks = [a.shape[1] for a, _ in pairs]
    elif mode == "nt":
        m, n = a0.shape[0], b0.shape[-2]
        ks = [a.shape[1] for a, _ in pairs]
    else:
        m, n = a0.shape[1], b0.shape[1]
        ks = [a.shape[0] for a, _ in pairs]
    tm = tiles[0] or _first_divisor(m, (1024, 768, 512, 256))
    tn = tiles[1] or (n if n <= 1536 else _first_divisor(n, (1024, 512, 256, 128)))
    tks = [tiles[2] or (k if k <= MAX_TK else MAX_TK) for k in ks]
    nks = [k // tk for k, tk in zip(ks, tks)]
    offs = [sum(nks[:p]) for p in range(len(pairs))]
    nk = sum(nks)
    n_pairs = len(pairs)
    assert not (b_parts or out_parts) or n_pairs == 1

    def kk(k, p):
        return jnp.clip(k - offs[p], 0, nks[p] - 1)

    in_specs = []
    for p in range(n_pairs):
        tk = tks[p]
        if mode == "nn":
            in_specs.append(pl.BlockSpec((tm, tk), lambda i, j, k, p=p: (i, kk(k, p))))
            if b_parts == "n":
                per = part_c // tn
                in_specs.append(pl.BlockSpec((None, tk, tn), lambda i, j, k: (j // per, k, j % per)))
            else:
                in_specs.append(pl.BlockSpec((tk, tn), lambda i, j, k, p=p: (kk(k, p), j)))
        elif mode == "nt":
            in_specs.append(pl.BlockSpec((tm, tk), lambda i, j, k, p=p: (i, kk(k, p))))
            if b_parts == "k":
                per = part_c // tk
                in_specs.append(pl.BlockSpec((None, tn, tk), lambda i, j, k: (k // per, j, k % per)))
            else:
                in_specs.append(pl.BlockSpec((tn, tk), lambda i, j, k, p=p: (j, kk(k, p))))
        else:
            in_specs.append(pl.BlockSpec((tk, tm), lambda i, j, k, p=p: (kk(k, p), i)))
            in_specs.append(pl.BlockSpec((tk, tn), lambda i, j, k, p=p: (kk(k, p), j)))
    if out_parts:
        out_per = (n // N_CHIPS) // tn
        out_spec = pl.BlockSpec((None, tm, tn), lambda i, j, k: (j // out_per, i, j % out_per))
        out_shape = jax.ShapeDtypeStruct((N_CHIPS, m, n // N_CHIPS), out_dtype)
    else:
        out_spec = pl.BlockSpec((tm, tn), lambda i, j, k: (i, j))
        out_shape = jax.ShapeDtypeStruct((m, n), out_dtype)
    dims = {"nn": (((1,), (0,)), ((), ())), "nt": (((1,), (1,)), ((), ())), "tn": (((0,), (0,)), ((), ()))}[mode]

    n_after = 0 if after is None else 1

    def body(*refs):
        o_ref = refs[2 * n_pairs + n_after]
        k = pl.program_id(2)

        def prod(p):
            return lax.dot_general(refs[2 * p][...], refs[2 * p + 1][...], dims, preferred_element_type=F32)

        if nk == 1:
            o_ref[...] = prod(0).astype(o_ref.dtype)
            return
        acc = refs[2 * n_pairs + n_after + 1]
        for p in range(n_pairs):
            first, last = offs[p], offs[p] + nks[p] - 1
            lo, hi = max(first, 1), min(last, nk - 2)
            if first == 0:
                @pl.when(k == 0)
                def _(p=p):
                    acc[...] = prod(p)

            if lo <= hi:
                @pl.when((k >= lo) & (k <= hi))
                def _(p=p):
                    acc[...] += prod(p)

            if last == nk - 1:
                @pl.when(k == nk - 1)
                def _(p=p):
                    o_ref[...] = (acc[...] + prod(p)).astype(o_ref.dtype)

    flat = [t for pr in pairs for t in pr] + ([] if after is None else [after])
    return pl.pallas_call(
        body,
        name=name,
        grid=(m // tm, n // tn, nk),
        in_specs=in_specs + [pl.BlockSpec(memory_space=pl.ANY)] * n_after,
        out_specs=out_spec,
        out_shape=out_shape,
        scratch_shapes=[pltpu.VMEM((tm, tn), F32)] if nk > 1 else [],
        compiler_params=_params(("parallel", "parallel", "arbitrary")),
    )(*flat)


F4 = D_FF // N_CHIPS
FFN_TM = 512
FFN_STRIP = 256


def _ffn_up(hf, w_gu):
    t, d = hf.shape

    def body(a_ref, b_ref, gu_ref, act_ref):
        for r in range(0, FFN_TM, FFN_STRIP):
            acc = jnp.dot(a_ref[r:r + FFN_STRIP, :], b_ref[...], preferred_element_type=F32)
            g = acc[:, :F4]
            gu_ref[r:r + FFN_STRIP, :] = acc.astype(gu_ref.dtype)
            act_ref[r:r + FFN_STRIP, :] = (g * _sigmoid(g) * acc[:, F4:]).astype(act_ref.dtype)

    return pl.pallas_call(
        body,
        name="ffn_up",
        grid=(N_CHIPS, t // FFN_TM),
        in_specs=[pl.BlockSpec((FFN_TM, d), lambda q, i: (i, 0)),
                  pl.BlockSpec((None, d, 2 * F4), lambda q, i: (q, 0, 0))],
        out_specs=[pl.BlockSpec((FFN_TM, 2 * F4), lambda q, i: (i, q)),
                   pl.BlockSpec((FFN_TM, F4), lambda q, i: (i, q))],
        out_shape=[jax.ShapeDtypeStruct((t, 2 * D_FF), BF16), jax.ShapeDtypeStruct((t, D_FF), BF16)],
        compiler_params=_params(("parallel", "parallel")),
    )(hf, w_gu)


def _ffn_down_dx(d_ff, w_down, gu):
    t, d = d_ff.shape

    def body(a_ref, b_ref, gu_ref, o_ref):
        for r in range(0, FFN_TM, FFN_STRIP):
            rows = slice(r, r + FFN_STRIP)
            d_act = lax.dot_general(a_ref[rows, :], b_ref[...], NT_DIMS, preferred_element_type=F32)
            g = gu_ref[rows, :F4].astype(F32)
            u = gu_ref[rows, F4:].astype(F32)
            sg = _sigmoid(g)
            o_ref[rows, :F4] = (d_act * u * sg * (1.0 + g * (1.0 - sg))).astype(o_ref.dtype)
            o_ref[rows, F4:] = (d_act * g * sg).astype(o_ref.dtype)

    return pl.pallas_call(
        body,
        name="ffn_down_dx",
        grid=(N_CHIPS, t // FFN_TM),
        in_specs=[pl.BlockSpec((FFN_TM, d), lambda q, i: (i, 0)),
                  pl.BlockSpec((F4, d), lambda q, i: (q, 0)),
                  pl.BlockSpec((FFN_TM, 2 * F4), lambda q, i: (i, q))],
        out_specs=pl.BlockSpec((FFN_TM, 2 * F4), lambda q, i: (i, q)),
        out_shape=jax.ShapeDtypeStruct((t, 2 * D_FF), BF16),
        compiler_params=_params(("parallel", "parallel")),
    )(d_ff, w_down, gu)


def _rowwise(fn, row_ins, vec_ins, row_outs, acc_outs, name, after=None):
    t = row_ins[0].shape[0]
    tm = ROW_TILE
    n_in = len(row_ins) + len(vec_ins)
    n_row = len(row_outs)
    extra = [] if after is None else [after]

    def body(*refs):
        ins = [r[...] for r in refs[:n_in]]
        outs = refs[n_in + len(extra):]
        vals = fn(*ins)
        for r, v in zip(outs[:n_row], vals[:n_row]):
            r[...] = v.astype(r.dtype)
        if acc_outs:
            @pl.when(pl.program_id(0) == 0)
            def _():
                for r in outs[n_row:]:
                    r[...] = jnp.zeros_like(r)

            for r, v in zip(outs[n_row:], vals[n_row:]):
                r[...] += v

    in_specs = [pl.BlockSpec((tm, a.shape[1]), lambda i: (i, 0)) for a in row_ins]
    in_specs += [pl.BlockSpec(a.shape, lambda i: (0, 0)) for a in vec_ins]
    out_specs = [pl.BlockSpec((tm, c), lambda i: (i, 0)) for c, _ in row_outs]
    out_specs += [pl.BlockSpec((1, c), lambda i: (0, 0)) for c in acc_outs]
    out_shape = [jax.ShapeDtypeStruct((t, c), dt) for c, dt in row_outs]
    out_shape += [jax.ShapeDtypeStruct((1, c), F32) for c in acc_outs]
    return pl.pallas_call(
        body,
        name=name,
        grid=(t // tm,),
        in_specs=in_specs + [pl.BlockSpec(memory_space=pl.ANY)] * len(extra),
        out_specs=out_specs,
        out_shape=out_shape,
        compiler_params=_params(("arbitrary",)),
    )(*row_ins, *vec_ins, *extra)


def _mean(v):
    return jnp.mean(v, axis=-1, keepdims=True)


def _colsum(v):
    return jnp.sum(v, axis=0, keepdims=True)


def _rms_fwd(v, g):
    r = lax.rsqrt(_mean(v * v) + EPS)
    vhat = v * r
    return vhat * g, vhat, r


def _rms_bwd(dn, vhat, r, g):
    dng = dn * g
    return r * (dng - vhat * _mean(dng * vhat)), _colsum(dn * vhat)


def _swap_rope_halves(v):
    n = v.shape[-1]
    lane = lax.broadcasted_iota(jnp.int32, v.shape, v.ndim - 1)
    return jnp.where(lane % QK_ROPE < QK_ROPE // 2, pltpu.roll(v, n - QK_ROPE // 2, v.ndim - 1),
                     pltpu.roll(v, QK_ROPE // 2, v.ndim - 1))


def _rope(v, cos2, sin2):
    return v * cos2 + _swap_rope_halves(v) * sin2


def _rope_transposed(dv, cos2, sin2):
    return dv * cos2 + _swap_rope_halves(dv * sin2)


def _sigmoid(v):
    return 1.0 / (1.0 + jnp.exp(-v))


CONV_ROWS = 256


def _conv_fwd(ag, conv_w, conv_b):
    t = ag.shape[0]
    cb = LANES

    def body(ag_ref, w_ref, b_ref, o_ref, scr):
        a = ag_ref[:, :cb].astype(F32)
        g = ag_ref[:, cb:].astype(F32)
        scr[pl.ds(0, CONV_K_PAD), :] = jnp.zeros((CONV_K_PAD, cb), F32)
        scr[pl.ds(CONV_K_PAD, t), :] = a * _sigmoid(g)
        for r0 in range(0, t, CONV_ROWS):
            acc = jnp.zeros((CONV_ROWS, cb), F32) + b_ref[...]
            for k in range(CONV_K):
                acc = acc + w_ref[k:k + 1, :] * scr[pl.ds(r0 + CONV_K_PAD - (CONV_K - 1) + k, CONV_ROWS), :]
            o_ref[pl.ds(r0, CONV_ROWS), :] = acc

    return pl.pallas_call(
        body,
        name="conv_fwd",
        grid=(CONV_CH // cb,),
        in_specs=[pl.BlockSpec((t, 2 * cb), lambda j: (0, j)),
                  pl.BlockSpec((CONV_K_PAD, cb), lambda j: (0, j)),
                  pl.BlockSpec((1, cb), lambda j: (0, j))],
        out_specs=pl.BlockSpec((t, cb), lambda j: (0, j)),
        out_shape=jax.ShapeDtypeStruct((t, CONV_CH), F32),
        scratch_shapes=[pltpu.VMEM((t + CONV_K_PAD, cb), F32)],
        compiler_params=_params(("parallel",)),
    )(ag, conv_w, conv_b)


def _conv_bwd(d_u1, ag, conv_w):
    t = ag.shape[0]
    cb = LANES

    def body(du_ref, ag_ref, w_ref, dag_ref, dw_ref, db_ref, su, sd):
        a = ag_ref[:, :cb].astype(F32)
        g = ag_ref[:, cb:].astype(F32)
        sg = _sigmoid(g)
        su[pl.ds(0, CONV_K_PAD), :] = jnp.zeros((CONV_K_PAD, cb), F32)
        su[pl.ds(CONV_K_PAD, t), :] = a * sg
        sd[pl.ds(0, t), :] = du_ref[...]
        sd[pl.ds(t, CONV_K_PAD), :] = jnp.zeros((CONV_K_PAD, cb), F32)
        db_ref[...] = _colsum(du_ref[...])
        dw_ref[...] = jnp.zeros_like(dw_ref)
        for r0 in range(0, t, CONV_ROWS):
            du = sd[pl.ds(r0, CONV_ROWS), :]
            acc = jnp.zeros((CONV_ROWS, cb), F32)
            for k in range(CONV_K):
                acc = acc + w_ref[k:k + 1, :] * sd[pl.ds(r0 + (CONV_K - 1) - k, CONV_ROWS), :]
                dw_ref[k:k + 1, :] += _colsum(du * su[pl.ds(r0 + CONV_K_PAD - (CONV_K - 1) + k, CONV_ROWS), :])
            sgc = sg[r0:r0 + CONV_ROWS]
            ac = a[r0:r0 + CONV_ROWS]
            dag_ref[pl.ds(r0, CONV_ROWS), :cb] = (acc * sgc).astype(dag_ref.dtype)
            dag_ref[pl.ds(r0, CONV_ROWS), cb:] = (acc * ac * sgc * (1.0 - sgc)).astype(dag_ref.dtype)

    return pl.pallas_call(
        body,
        name="conv_bwd",
        grid=(CONV_CH // cb,),
        in_specs=[pl.BlockSpec((t, cb), lambda j: (0, j)),
                  pl.BlockSpec((t, 2 * cb), lambda j: (0, j)),
                  pl.BlockSpec((CONV_K_PAD, cb), lambda j: (0, j))],
        out_specs=[pl.BlockSpec((t, 2 * cb), lambda j: (0, j)),
                   pl.BlockSpec((CONV_K_PAD, cb), lambda j: (0, j)),
                   pl.BlockSpec((1, cb), lambda j: (0, j))],
        out_shape=[jax.ShapeDtypeStruct((t, 2 * CONV_CH), BF16),
                   jax.ShapeDtypeStruct((CONV_K_PAD, CONV_CH), F32),
                   jax.ShapeDtypeStruct((1, CONV_CH), F32)],
        scratch_shapes=[pltpu.VMEM((t + CONV_K_PAD, cb), F32), pltpu.VMEM((t + CONV_K_PAD, cb), F32)],
        compiler_params=_params(("parallel",)),
    )(d_u1, ag, conv_w)


ATT_TQ = 256
NEG = float(jnp.finfo(jnp.float32).min)
SCALE = QK_HEAD ** -0.5
NT_DIMS = (((1,), (1,)), ((), ()))
TN_DIMS = (((0,), (0,)), ((), ()))


def _att_probs(qf, kf, row0):
    s = lax.dot_general(qf, kf, NT_DIMS, preferred_element_type=F32) * SCALE
    tq, t = s.shape
    qpos = row0 + lax.broadcasted_iota(jnp.int32, (tq, t), 0)
    kpos = lax.broadcasted_iota(jnp.int32, (tq, t), 1)
    s = jnp.where(kpos <= qpos, s, NEG)
    p = jnp.exp(s - jnp.max(s, axis=-1, keepdims=True))
    return p * (1.0 / jnp.sum(p, axis=-1, keepdims=True))


def _half_mask(shape, which):
    lane = lax.broadcasted_iota(jnp.int32, shape, len(shape) - 1)
    return (lane // QK_ROPE == which).astype(F32)


def _attention_fwd(q, kv, kpe2, cos2, sin2):
    t = q.shape[0]
    tq = ATT_TQ

    def body(qn_ref, qp_ref, c_ref, s_ref, kv_ref, kpe_ref, o_ref):
        roped = _rope(qp_ref[...].astype(F32), c_ref[...], s_ref[...])

        def block(i):
            keys = slice(0, (i + 1) * tq)
            for e in range(2):
                qf = jnp.concatenate([qn_ref[:, e * QK_NOPE:(e + 1) * QK_NOPE],
                                      (roped * _half_mask(roped.shape, e)).astype(BF16)], axis=1)
                kf = jnp.concatenate([kv_ref[keys, e * 256:e * 256 + QK_NOPE], kpe_ref[keys, :]], axis=1)
                p = _att_probs(qf, kf, i * tq)
                v = kv_ref[keys, e * 256 + QK_NOPE:(e + 1) * 256]
                o_ref[:, e * V_HEAD:(e + 1) * V_HEAD] = jnp.dot(p.astype(BF16), v, preferred_element_type=F32).astype(o_ref.dtype)

        for i in range(t // tq):
            pl.when(pl.program_id(1) == i)(functools.partial(block, i))

    return pl.pallas_call(
        body,
        name="attention_fwd",
        grid=(N_HEADS // 2, t // tq),
        in_specs=[pl.BlockSpec((tq, 2 * QK_NOPE), lambda h, i: (i, h)),
                  pl.BlockSpec((tq, LANES), lambda h, i: (i, N_HEADS + h)),
                  pl.BlockSpec((tq, LANES), lambda h, i: (i, 0)),
                  pl.BlockSpec((tq, LANES), lambda h, i: (i, 0)),
                  pl.BlockSpec((t, 512), lambda h, i: (0, h)),
                  pl.BlockSpec((t, LANES), lambda h, i: (0, 0))],
        out_specs=pl.BlockSpec((tq, 2 * V_HEAD), lambda h, i: (i, h)),
        out_shape=jax.ShapeDtypeStruct((t, ATTN_CH), BF16),
        compiler_params=_params(("parallel", "parallel")),
    )(q, q, cos2, sin2, kv, kpe2)


def _attention_bwd(q, kv, kpe2, cos2, sin2, d_attn):
    t = q.shape[0]
    tq = ATT_TQ
    n_q = t // tq

    def body(qn_ref, qp_ref, c_ref, s_ref, kv_ref, kpe_ref, do_ref, dqn_ref, dqp_ref, dkv_ref, dkpe_ref, dkv_acc):
        h, i = pl.program_id(0), pl.program_id(1)

        @pl.when(i == 0)
        def _():
            dkv_acc[...] = jnp.zeros_like(dkv_acc)

        @pl.when((i == 0) & (h == 0))
        def _():
            dkpe_ref[...] = jnp.zeros_like(dkpe_ref)

        roped = _rope(qp_ref[...].astype(F32), c_ref[...], s_ref[...])

        def block(ib):
            keys = slice(0, (ib + 1) * tq)
            d_roped = jnp.zeros((tq, LANES), F32)
            for e in range(2):
                mask = _half_mask(roped.shape, e)
                qf = jnp.concatenate([qn_ref[:, e * QK_NOPE:(e + 1) * QK_NOPE], (roped * mask).astype(BF16)], axis=1)
                kf = jnp.concatenate([kv_ref[keys, e * 256:e * 256 + QK_NOPE], kpe_ref[keys, :]], axis=1)
                v = kv_ref[keys, e * 256 + QK_NOPE:(e + 1) * 256]
                do = do_ref[:, e * V_HEAD:(e + 1) * V_HEAD]
                p = _att_probs(qf, kf, ib * tq)
                dp = lax.dot_general(do, v, NT_DIMS, preferred_element_type=F32)
                ds = (p * (dp - jnp.sum(p * dp, axis=-1, keepdims=True)) * SCALE).astype(BF16)
                dqf = jnp.dot(ds, kf, preferred_element_type=F32)
                dkf = lax.dot_general(ds, qf, TN_DIMS, preferred_element_type=F32)
                dv = lax.dot_general(p.astype(BF16), do, TN_DIMS, preferred_element_type=F32)
                dqn_ref[:, e * QK_NOPE:(e + 1) * QK_NOPE] = dqf[:, :QK_NOPE].astype(dqn_ref.dtype)
                d_roped = d_roped + dqf[:, QK_NOPE:] * mask
                dkv_acc[keys, e * 256:e * 256 + QK_NOPE] += dkf[:, :QK_NOPE]
                dkv_acc[keys, e * 256 + QK_NOPE:(e + 1) * 256] += dv
                dkpe_ref[keys, :] += dkf[:, QK_NOPE:]
            dqp_ref[...] = _rope_transposed(d_roped, c_ref[...], s_ref[...]).astype(dqp_ref.dtype)

        for ib in range(n_q):
            pl.when(i == ib)(functools.partial(block, ib))

        @pl.when(i == n_q - 1)
        def _():
            dkv_ref[...] = dkv_acc[...].astype(dkv_ref.dtype)

    return pl.pallas_call(
        body,
        name="attention_bwd",
        grid=(N_HEADS // 2, n_q),
        in_specs=[pl.BlockSpec((tq, 2 * QK_NOPE), lambda h, i: (i, h)),
                  pl.BlockSpec((tq, LANES), lambda h, i: (i, N_HEADS + h)),
                  pl.BlockSpec((tq, LANES), lambda h, i: (i, 0)),
                  pl.BlockSpec((tq, LANES), lambda h, i: (i, 0)),
                  pl.BlockSpec((t, 512), lambda h, i: (0, h)),
                  pl.BlockSpec((t, LANES), lambda h, i: (0, 0)),
                  pl.BlockSpec((tq, 2 * V_HEAD), lambda h, i: (i, h))],
        out_specs=[pl.BlockSpec((tq, 2 * QK_NOPE), lambda h, i: (i, h)),
                   pl.BlockSpec((tq, LANES), lambda h, i: (i, h)),
                   pl.BlockSpec((t, 512), lambda h, i: (0, h)),
                   pl.BlockSpec((t, LANES), lambda h, i: (0, 0))],
        out_shape=[jax.ShapeDtypeStruct((t, N_HEADS * QK_NOPE), BF16),
                   jax.ShapeDtypeStruct((t, N_HEADS * QK_ROPE), BF16),
                   jax.ShapeDtypeStruct((t, N_HEADS * 256), BF16),
                   jax.ShapeDtypeStruct((t, LANES), F32)],
        scratch_shapes=[pltpu.VMEM((t, 512), F32)],
        compiler_params=_params(("arbitrary", "arbitrary")),
    )(q, q, cos2, sin2, kv, kpe2, d_attn)


def _local_step(x, target, cos2, sin2, vec, w, ffn):
    d = D_MODEL

    (h,) = _rowwise(lambda xv, g: (_rms_fwd(xv, g)[0],), [x], [vec["pre_mix_norm"]], [(d, BF16)], [], "pre_mix_norm_fwd")
    ag = _matmul([(h, w["w_ag"])], "nn", BF16, "in_proj_ag")
    z2 = _matmul([(h, w["w_z2"])], "nn", BF16, "in_proj_z2")
    w = {**w, **ffn["mixer_rest"](z2)}
    u1 = _conv_fwd(ag, w["conv_w"], vec["conv_b"])

    def latents_fwd(z, c2, s2, qg, kvg):
        z = z.astype(F32)
        qn = _rms_fwd(z[:, :Q_LORA], qg)[0]
        kvn = _rms_fwd(z[:, Q_LORA:Q_LORA + KV_LORA], kvg)[0]
        kr = z[:, Q_LORA + KV_LORA:]
        kr2 = kr + pltpu.roll(kr, QK_ROPE, 1)
        return qn, kvn, _rope(kr2, c2, s2)

    qn, kvn, kpe2 = _rowwise(latents_fwd, [z2, cos2, sin2], [vec["q_norm"], vec["kv_norm"]],
                             [(Q_LORA, BF16), (KV_LORA, BF16), (LANES, BF16)], [], "latents_fwd")
    q = _matmul([(qn, w["w_uq"])], "nn", BF16, "q_up")
    kv = _matmul([(kvn, w["w_ukv"])], "nn", BF16, "kv_up")
    attn = _attention_fwd(q, kv, kpe2, cos2, sin2)

    def conv_post(u, lg, lb):
        mu = _mean(u)
        uc = u - mu
        rstd = lax.rsqrt(_mean(uc * uc) + EPS)
        uhat = uc * rstd
        u2 = uhat * lg + lb
        sg = _sigmoid(u2)
        return uhat, rstd, u2, sg, u2 * sg

    def mix_in_fwd(u, at, lg, lb, cg, ag_):
        u3 = conv_post(u, lg, lb)[4]
        cn = _rms_fwd(u3, cg)[0]
        an = _rms_fwd(at.astype(F32), ag_)[0]
        return (jnp.concatenate([cn, an], axis=1),)

    (cat,) = _rowwise(mix_in_fwd, [u1, attn], [vec["conv_ln_g"], vec["conv_ln_b"], vec["conv_out_norm"], vec["attn_out_norm"]],
                      [(2 * CONV_CH, BF16)], [], "mix_in_fwd")
    mix = _matmul([(cat, w["w_out"])], "nn", F32, "out_proj")
    landed = ffn["w_gu_landed"](mix)

    def residual1(xv, mv, gpm, gpf):
        x1 = xv + _rms_fwd(mv, gpm)[0]
        return x1, _rms_fwd(x1, gpf)[0]

    x1, hf = _rowwise(residual1, [x, mix], [vec["post_mix_norm"], vec["pre_ffn_norm"]], [(d, F32), (d, BF16)], [],
                      "residual1_fwd", after=landed)
    w_gu = ffn["w_gu"](hf)
    gu, act = _ffn_up(hf, w_gu)
    w_down = ffn["w_down"](act)
    ff = _matmul([(act, w_down)], "nn", F32, "ffn_down")

    def loss_head(x1v, ffv, tg, g):
        n, fhat, r = _rms_fwd(ffv, g)
        err = x1v + n - tg
        loss = 0.5 * jnp.sum(_mean(err * err), axis=0, keepdims=True)
        dy = err * (1.0 / d)
        d_ff, dg = _rms_bwd(dy, fhat, r, g)
        return dy, d_ff, dg, jnp.broadcast_to(loss, (1, LANES))

    dy, d_ff, g_post_ffn, loss = _rowwise(loss_head, [x1, ff, target], [vec["post_ffn_norm"]],
                                          [(d, F32), (d, BF16)], [d, LANES], "loss_head")
    d_gu = _ffn_down_dx(d_ff, w_down, gu)
    dw_down = _matmul([(act, d_ff)], "tn", BF16, "ffn_down_dw", tiles=(F4, None, None))
    started = ffn["dw_down"](dw_down)
    dw_gu = _matmul([(hf, d_gu)], "tn", BF16, "ffn_gate_up_dw", out_parts=True, tiles=(None, F4, None), after=started)
    started = ffn["dw_gu"](dw_gu)
    d_hf = _matmul([(d_gu, w_gu)], "nt", F32, "ffn_gate_up_dx", b_parts="k", after=started)
    started = ffn["grads_exchanged"](d_hf)

    def residual1_bwd(dyv, dhf, x1v, mv, gpf, gpm):
        _, x1hat, r1 = _rms_fwd(x1v, gpf)
        dn, dgpf = _rms_bwd(dhf, x1hat, r1, gpf)
        d_x1 = dyv + dn
        _, mhat, rm = _rms_fwd(mv, gpm)
        d_mix, dgpm = _rms_bwd(d_x1, mhat, rm, gpm)
        return d_x1, d_mix, dgpf, dgpm

    d_x1, d_mix, g_pre_ffn, g_post_mix = _rowwise(residual1_bwd, [dy, d_hf, x1, mix], [vec["pre_ffn_norm"], vec["post_mix_norm"]],
                                                  [(d, F32), (d, BF16)], [d, d], "residual1_bwd", after=started)
    d_cat = _matmul([(d_mix, w["w_out"])], "nt", BF16, "out_proj_dx")
    dw_out = _matmul([(cat, d_mix)], "tn", BF16, "out_proj_dw")

    def mix_in_bwd(dc, u, at, lg, lb, cg, ag_):
        dc = dc.astype(F32)
        uhat, rstd, u2, sg, u3 = conv_post(u, lg, lb)
        _, u3hat, rc = _rms_fwd(u3, cg)
        d_u3, dcg = _rms_bwd(dc[:, :CONV_CH], u3hat, rc, cg)
        d_u2 = d_u3 * sg * (1.0 + u2 * (1.0 - sg))
        dgl = d_u2 * lg
        d_u1 = rstd * (dgl - _mean(dgl) - uhat * _mean(dgl * uhat))
        _, ahat, ra = _rms_fwd(at.astype(F32), ag_)
        d_at, dag = _rms_bwd(dc[:, CONV_CH:], ahat, ra, ag_)
        return d_u1, d_at, dcg, _colsum(d_u2 * uhat), _colsum(d_u2), dag

    d_u1, d_attn, g_conv_out, g_ln_g, g_ln_b, g_attn_out = _rowwise(
        mix_in_bwd, [d_cat, u1, attn], [vec["conv_ln_g"], vec["conv_ln_b"], vec["conv_out_norm"], vec["attn_out_norm"]],
        [(CONV_CH, F32), (ATTN_CH, BF16)], [CONV_CH] * 4, "mix_in_bwd")
    d_ag, d_conv_w, g_conv_b = _conv_bwd(d_u1, ag, w["conv_w"])
    d_qn_, d_qp_, d_kv, d_kpe2 = _attention_bwd(q, kv, kpe2, cos2, sin2, d_attn)
    d_q = jnp.concatenate([d_qn_, d_qp_], axis=1)
    d_qn = _matmul([(d_q, w["w_uq"])], "nt", BF16, "q_up_dx")
    dw_uq = _matmul([(qn, d_q)], "tn", BF16, "q_up_dw")
    d_kvn = _matmul([(d_kv, w["w_ukv"])], "nt", BF16, "kv_up_dx")
    dw_ukv = _matmul([(kvn, d_kv)], "tn", BF16, "kv_up_dw")

    def latents_bwd(z, dq, dk, dkp, c2, s2, qg, kvg):
        z = z.astype(F32)
        _, qhat, rq = _rms_fwd(z[:, :Q_LORA], qg)
        d_ql, dqg = _rms_bwd(dq.astype(F32), qhat, rq, qg)
        _, khat, rk = _rms_fwd(z[:, Q_LORA:Q_LORA + KV_LORA], kvg)
        d_kl, dkg = _rms_bwd(dk.astype(F32), khat, rk, kvg)
        both = dkp + pltpu.roll(dkp, QK_ROPE, 1)
        d_kr = _rope_transposed(both, c2, s2) * _half_mask(both.shape, 0)
        return jnp.concatenate([d_ql, d_kl, d_kr], axis=1), dqg, dkg

    d_z2, g_q_norm, g_kv_norm = _rowwise(latents_bwd, [z2, d_qn, d_kvn, d_kpe2, cos2, sin2], [vec["q_norm"], vec["kv_norm"]],
                                         [(Z2_COLS, BF16)], [Q_LORA, KV_LORA], "latents_bwd")
    d_h = _matmul([(d_ag, w["w_ag"]), (d_z2, w["w_z2"])], "nt", F32, "in_proj_dx")
    dw_ag = _matmul([(h, d_ag)], "tn", BF16, "in_proj_ag_dw")
    dw_z2 = _matmul([(h, d_z2)], "tn", BF16, "in_proj_z2_dw")

    def pre_mix_bwd(dx1, dh, xv, g):
        _, xhat, r = _rms_fwd(xv, g)
        dn, dg = _rms_bwd(dh, xhat, r, g)
        return dx1 + dn, dg

    grad_x, g_pre_mix = _rowwise(pre_mix_bwd, [d_x1, d_h, x], [vec["pre_mix_norm"]], [(d, F32)], [d], "pre_mix_norm_bwd")

    dw = dict(w_ag=dw_ag, w_z2=dw_z2, w_uq=dw_uq, w_ukv=dw_ukv, conv_w=d_conv_w, w_out=dw_out, w_gu=dw_gu, w_down=dw_down)
    dvec = dict(pre_mix_norm=g_pre_mix, q_norm=g_q_norm, kv_norm=g_kv_norm, conv_b=g_conv_b, conv_ln_g=g_ln_g,
                conv_ln_b=g_ln_b, conv_out_norm=g_conv_out, attn_out_norm=g_attn_out, post_mix_norm=g_post_mix,
                pre_ffn_norm=g_pre_ffn, post_ffn_norm=g_post_ffn)
    return loss, grad_x, dw, dvec


ANY = pl.BlockSpec(memory_space=pl.ANY)


def _place():
    x, y, c = lax.axis_index("x"), lax.axis_index("y"), lax.axis_index("c")
    chips = [(1 - x, y), (x, 1 - y), (1 - x, 1 - y)]
    return x, y, c, chips


def _to_parts(chip, pieces, dtype, name, n_parts=N_CHIPS):
    r = pieces[0].shape[0]
    widths = [a.shape[1] for a in pieces]
    tr = r if r <= 512 else _first_divisor(r, (512, 256, 128))

    def body(p_ref, *refs):
        o_ref = refs[len(pieces)]
        off = 0
        for a_ref, wdt in zip(refs, widths):
            o_ref[:, off:off + wdt] = a_ref[...].astype(o_ref.dtype)
            off += wdt

    return pl.pallas_call(
        body,
        name=name,
        grid_spec=pltpu.PrefetchScalarGridSpec(
            num_scalar_prefetch=1,
            grid=(r // tr,),
            in_specs=[pl.BlockSpec((tr, wdt), lambda i, p_ref: (i, 0)) for wdt in widths],
            out_specs=pl.BlockSpec((None, tr, sum(widths)), lambda i, p_ref: (p_ref[0], i, 0))),
        out_shape=jax.ShapeDtypeStruct((n_parts, r, sum(widths)), dtype),
        compiler_params=_params(("parallel",)),
    )(chip, *pieces)


HBM = pl.BlockSpec(memory_space=pltpu.HBM)
SEM = pl.BlockSpec(memory_space=pltpu.SEMAPHORE)
EFFECT = pltpu.SideEffectType.DATAFLOW_SIDE_EFFECTING
VMEM_SPEC = pl.BlockSpec(memory_space=pltpu.VMEM)
TOKEN = jax.ShapeDtypeStruct((8, LANES), F32)


def _in_hbm(a):
    return pltpu.with_memory_space_constraint(a, pltpu.HBM)


def _gather_rows(buf, whole, half):
    r = buf.shape[1]
    return pl.ds(0, r) if whole else pl.ds(half * (r // 2), r // 2)


def _gather_start(bufs, whole, groups, name, after=()):
    n = len(bufs)
    n_g = len(groups)

    def body(*refs):
        refs = refs[:n] + refs[n + len(after):]
        sems = refs[n:n + 2 * n_g]
        outs = refs[n + 2 * n_g:2 * n + 2 * n_g]
        token = refs[2 * n + 2 * n_g]
        token[...] = jnp.zeros_like(token)
        x, y, c, chips = _place()
        p = 2 * x + y
        for gi, group in enumerate(groups):
            for ki, k in enumerate(group):
                blk = outs[k].at[p, _gather_rows(bufs[k], whole[k], c), :]
                for j, (px, py) in enumerate(chips):
                    pltpu.make_async_remote_copy(src_ref=blk, dst_ref=blk, send_sem=sems[2 * gi].at[3 * ki + j],
                                                 recv_sem=sems[2 * gi + 1].at[3 * ki + j],
                                                 device_id=(px, py, c), device_id_type=MESH).start()

    sem_shapes = []
    for group in groups:
        sem_shapes += [pltpu.SemaphoreType.DMA((3 * len(group),))] * 2
    res = pl.pallas_call(
        body,
        name=name,
        in_specs=[HBM] * n + [ANY] * len(after),
        out_specs=[SEM] * (2 * n_g) + [HBM] * n + [VMEM_SPEC],
        out_shape=sem_shapes + [pltpu.HBM(a.shape, a.dtype) for a in bufs] + [TOKEN],
        input_output_aliases={k: 2 * n_g + k for k in range(n)},
        compiler_params=pltpu.CompilerParams(has_side_effects=EFFECT),
    )(*[_in_hbm(a) for a in bufs], *after)
    sems = [(res[2 * gi], res[2 * gi + 1]) for gi in range(n_g)]
    return sems, list(res[2 * n_g:2 * n_g + n]), res[2 * n_g + n]


def _gather_wait(bufs, whole, send, recv, after, name):
    n = len(bufs)

    def body(*refs):
        ins = refs[:n]
        send_ref, recv_ref = refs[n], refs[n + 1]
        x, y, c, chips = _place()
        p = 2 * x + y
        for ki in range(n):
            rows = _gather_rows(bufs[ki], whole[ki], c)
            for j, (px, py) in enumerate(chips):
                cp = pltpu.make_async_remote_copy(src_ref=ins[ki].at[p, rows, :], dst_ref=ins[ki].at[2 * px + py, rows, :],
                                                  send_sem=send_ref.at[3 * ki + j], recv_sem=recv_ref.at[3 * ki + j],
                                                  device_id=(px, py, c), device_id_type=MESH)
                cp.wait_send()
                cp.wait_recv()

    res = pl.pallas_call(
        body,
        name=name,
        in_specs=[HBM] * n + [SEM, SEM] + [ANY] * len(after),
        out_specs=[HBM] * n,
        out_shape=[pltpu.HBM(a.shape, a.dtype) for a in bufs],
        input_output_aliases={k: k for k in range(n)},
        compiler_params=pltpu.CompilerParams(has_side_effects=EFFECT),
    )(*bufs, send, recv, *after)
    return list(res)


def _gather_hand_on(bufs, name, after=()):
    n = len(bufs)

    def body(*refs):
        refs = refs[n + len(after):]
        outs = refs[:n]
        send, recv = refs[n:]
        x, y, c, chips = _place()

        def d2d(k, j, half):
            px, py = chips[j]
            blk = outs[k].at[2 * px + py, _gather_rows(bufs[k], False, half), :]
            return pltpu.make_async_remote_copy(src_ref=blk, dst_ref=blk, send_sem=send.at[3 * k + j], recv_sem=recv.at[3 * k + j],
                                                device_id=(x, y, 1 - c), device_id_type=MESH)

        sent = [d2d(k, j, c) for k in range(n) for j in range(3)]
        for cp in sent:
            cp.start()
        for k in range(n):
            for j in range(3):
                d2d(k, j, 1 - c).wait_recv()
        for cp in sent:
            cp.wait_send()

    return pl.pallas_call(
        body,
        name=name,
        in_specs=[ANY] * (n + len(after)),
        out_specs=[ANY] * n,
        out_shape=[jax.ShapeDtypeStruct(a.shape, a.dtype) for a in bufs],
        input_output_aliases={k: k for k in range(n)},
        scratch_shapes=[pltpu.SemaphoreType.DMA((3 * n,)), pltpu.SemaphoreType.DMA((3 * n,))],
        compiler_params=pltpu.CompilerParams(has_side_effects=True),
    )(*bufs, *after)


def _hand_on_start(bufs, name, after=()):
    n = len(bufs)

    def body(*refs):
        refs = refs[:n] + refs[n + len(after):]
        send, recv = refs[n], refs[n + 1]
        outs = refs[n + 2:2 * n + 2]
        refs[2 * n + 2][...] = jnp.zeros(TOKEN.shape, TOKEN.dtype)
        x, y, c, chips = _place()
        for k in range(n):
            for j, (px, py) in enumerate(chips):
                blk = outs[k].at[2 * px + py, _gather_rows(bufs[k], False, c), :]
                pltpu.make_async_remote_copy(src_ref=blk, dst_ref=blk, send_sem=send.at[3 * k + j], recv_sem=recv.at[3 * k + j],
                                             device_id=(x, y, 1 - c), device_id_type=MESH).start()

    res = pl.pallas_call(
        body,
        name=name,
        in_specs=[HBM] * n + [ANY] * len(after),
        out_specs=[SEM, SEM] + [HBM] * n + [VMEM_SPEC],
        out_shape=[pltpu.SemaphoreType.DMA((3 * n,))] * 2 + [pltpu.HBM(a.shape, a.dtype) for a in bufs] + [TOKEN],
        input_output_aliases={k: 2 + k for k in range(n)},
        compiler_params=pltpu.CompilerParams(has_side_effects=EFFECT),
    )(*[_in_hbm(a) for a in bufs], *after)
    return res[0], res[1], list(res[2:2 + n]), res[2 + n]


def _hand_on_wait(bufs, send, recv, after, name):
    n = len(bufs)

    def body(*refs):
        ins = refs[:n]
        send_ref, recv_ref = refs[n], refs[n + 1]
        x, y, c, chips = _place()
        for k in range(n):
            for j, (px, py) in enumerate(chips):
                q = 2 * px + py
                cp = pltpu.make_async_remote_copy(src_ref=ins[k].at[q, _gather_rows(bufs[k], False, c), :],
                                                  dst_ref=ins[k].at[q, _gather_rows(bufs[k], False, 1 - c), :],
                                                  send_sem=send_ref.at[3 * k + j], recv_sem=recv_ref.at[3 * k + j],
                                                  device_id=(x, y, 1 - c), device_id_type=MESH)
                cp.wait_send()
                cp.wait_recv()

    res = pl.pallas_call(
        body,
        name=name,
        in_specs=[HBM] * n + [SEM, SEM] + [ANY] * len(after),
        out_specs=[HBM] * n,
        out_shape=[pltpu.HBM(a.shape, a.dtype) for a in bufs],
        input_output_aliases={k: k for k in range(n)},
        compiler_params=pltpu.CompilerParams(has_side_effects=EFFECT),
    )(*bufs, send, recv, *after)
    return list(res)


def _pair_exchange_start(part, name):
    rh = part.shape[1] // 2
    land_shape = (N_CHIPS, rh, part.shape[2])

    def body(part_ref, land_ref, send, recv, part_out, land_out, token):
        token[...] = jnp.zeros_like(token)
        x, y, c, _ = _place()
        pltpu.make_async_remote_copy(src_ref=part_out.at[:, pl.ds((1 - c) * rh, rh), :], dst_ref=land_out,
                                     send_sem=send, recv_sem=recv, device_id=(x, y, 1 - c), device_id_type=MESH).start()

    res = pl.pallas_call(
        body,
        name=name,
        in_specs=[HBM, HBM],
        out_specs=[SEM, SEM, HBM, HBM, VMEM_SPEC],
        out_shape=[pltpu.SemaphoreType.DMA(()), pltpu.SemaphoreType.DMA(()), pltpu.HBM(part.shape, part.dtype),
                   pltpu.HBM(land_shape, part.dtype), TOKEN],
        input_output_aliases={0: 2, 1: 3},
        compiler_params=pltpu.CompilerParams(has_side_effects=EFFECT),
    )(_in_hbm(part), _in_hbm(lax.empty(land_shape, part.dtype)))
    return res


def _pair_exchange_wait(send, recv, part, land, after, name):
    rh = part.shape[1] // 2

    def body(part_ref, land_ref, send_ref, recv_ref, after_ref, part_out, land_out):
        x, y, c, _ = _place()
        cp = pltpu.make_async_remote_copy(src_ref=part_ref.at[:, pl.ds((1 - c) * rh, rh), :], dst_ref=land_ref,
                                          send_sem=send_ref, recv_sem=recv_ref, device_id=(x, y, 1 - c), device_id_type=MESH)
        cp.wait_send()
        cp.wait_recv()

    return pl.pallas_call(
        body,
        name=name,
        in_specs=[HBM, HBM, SEM, SEM, ANY],
        out_specs=[HBM, HBM],
        out_shape=[pltpu.HBM(part.shape, part.dtype), pltpu.HBM(land.shape, land.dtype)],
        input_output_aliases={0: 0, 1: 1},
        compiler_params=pltpu.CompilerParams(has_side_effects=EFFECT),
    )(part, land, send, recv, after)


def _chip_exchange_start(sums, name):
    n = len(sums)

    def body(*refs):
        send, recv = refs[2 * n], refs[2 * n + 1]
        src = refs[2 * n + 2:3 * n + 2]
        dst = refs[3 * n + 2:4 * n + 2]
        refs[4 * n + 2][...] = jnp.zeros(TOKEN.shape, TOKEN.dtype)
        x, y, c, chips = _place()
        p = 2 * x + y
        for k in range(n):
            for j, (px, py) in enumerate(chips):
                pltpu.make_async_remote_copy(src_ref=src[k].at[2 * px + py], dst_ref=dst[k].at[p],
                                             send_sem=send.at[3 * k + j], recv_sem=recv.at[3 * k + j],
                                             device_id=(px, py, c), device_id_type=MESH).start()

    res = pl.pallas_call(
        body,
        name=name,
        in_specs=[HBM] * (2 * n),
        out_specs=[SEM, SEM] + [HBM] * (2 * n) + [VMEM_SPEC],
        out_shape=[pltpu.SemaphoreType.DMA((3 * n,))] * 2 + [pltpu.HBM(a.shape, a.dtype) for a in sums] * 2 + [TOKEN],
        input_output_aliases={k: 2 + k for k in range(2 * n)},
        compiler_params=pltpu.CompilerParams(has_side_effects=EFFECT),
    )(*[_in_hbm(a) for a in sums], *[_in_hbm(lax.empty(a.shape, a.dtype)) for a in sums])
    return res[0], res[1], list(res[2:2 + n]), list(res[2 + n:2 + 2 * n]), res[2 + 2 * n]


def _chip_exchange_wait(sums, slots, send, recv, after, name):
    n = len(sums)

    def body(*refs):
        src, dst = refs[:n], refs[n:2 * n]
        send_ref, recv_ref = refs[2 * n], refs[2 * n + 1]
        x, y, c, chips = _place()
        for k in range(n):
            for j, (px, py) in enumerate(chips):
                cp = pltpu.make_async_remote_copy(src_ref=src[k].at[2 * px + py], dst_ref=dst[k].at[2 * px + py],
                                                  send_sem=send_ref.at[3 * k + j], recv_sem=recv_ref.at[3 * k + j],
                                                  device_id=(px, py, c), device_id_type=MESH)
                cp.wait_send()
                cp.wait_recv()

    res = pl.pallas_call(
        body,
        name=name,
        in_specs=[HBM] * (2 * n) + [SEM, SEM] + [ANY] * len(after),
        out_specs=[HBM] * (2 * n),
        out_shape=[pltpu.HBM(a.shape, a.dtype) for a in sums] * 2,
        input_output_aliases={k: k for k in range(2 * n)},
        compiler_params=pltpu.CompilerParams(has_side_effects=EFFECT),
    )(*sums, *slots, send, recv, *after)
    return list(res[:n]), list(res[n:])


def _pair_exchange(parts, name):
    n = len(parts)

    def body(*refs):
        ins, outs = refs[:n], refs[n:2 * n]
        send, recv = refs[2 * n:]
        x, y, c, _ = _place()
        copies = []
        for k in range(n):
            rh = parts[k].shape[1] // 2
            cp = pltpu.make_async_remote_copy(
                src_ref=ins[k].at[:, pl.ds((1 - c) * rh, rh), :], dst_ref=outs[k],
                send_sem=send.at[k], recv_sem=recv.at[k], device_id=(x, y, 1 - c), device_id_type=MESH)
            cp.start()
            copies.append(cp)
        for cp in copies:
            cp.wait()

    return pl.pallas_call(
        body,
        name=name,
        in_specs=[ANY] * n,
        out_specs=[ANY] * n,
        out_shape=[jax.ShapeDtypeStruct((N_CHIPS, a.shape[1] // 2, a.shape[2]), a.dtype) for a in parts],
        scratch_shapes=[pltpu.SemaphoreType.DMA((n,)), pltpu.SemaphoreType.DMA((n,))],
        compiler_params=pltpu.CompilerParams(has_side_effects=True),
    )(*parts)


def _pair_sum(core, part, landed, name):
    _, r, cdim = part.shape
    rh = r // 2
    tr = _row_tile(rh, cdim, 2)
    nb = rh // tr

    def body(c_ref, a_ref, b_ref, o_ref):
        o_ref[...] = (a_ref[...].astype(F32) + b_ref[...].astype(F32)).astype(o_ref.dtype)

    return pl.pallas_call(
        body,
        name=name,
        grid_spec=pltpu.PrefetchScalarGridSpec(
            num_scalar_prefetch=1,
            grid=(N_CHIPS, nb),
            in_specs=[pl.BlockSpec((None, tr, cdim), lambda q, i, c_ref: (q, c_ref[0] * nb + i, 0)),
                      pl.BlockSpec((None, tr, cdim), lambda q, i, c_ref: (q, i, 0))],
            out_specs=pl.BlockSpec((None, tr, cdim), lambda q, i, c_ref: (q, i, 0))),
        out_shape=jax.ShapeDtypeStruct((N_CHIPS, rh, cdim), BF16),
        compiler_params=_params(("parallel", "parallel")),
    )(core, part, landed)


def _chip_sum(place, own, slots, name, after=()):
    _, rh, cdim = slots.shape
    tr = _row_tile(rh, cdim, 2)
    nb = rh // tr

    def body(place_ref, own_ref, s1_ref, s2_ref, s3_ref, *rest):
        o_ref = rest[len(after)]
        acc = own_ref[...].astype(F32)
        for s_ref in (s1_ref, s2_ref, s3_ref):
            acc = acc + s_ref[...].astype(F32)
        o_ref[...] = acc

    def other(j):
        return lambda i, place_ref: ((place_ref[0] + j) % N_CHIPS, i, 0)

    return pl.pallas_call(
        body,
        name=name,
        grid_spec=pltpu.PrefetchScalarGridSpec(
            num_scalar_prefetch=1,
            grid=(nb,),
            in_specs=[pl.BlockSpec((None, tr, cdim), other(j)) for j in (0, 1, 2, 3)] + [ANY] * len(after),
            out_specs=pl.BlockSpec((tr, cdim), lambda i, place_ref: (place_ref[1] * nb + i, 0))),
        out_shape=jax.ShapeDtypeStruct((2 * rh, cdim), F32),
        compiler_params=_params(("parallel",)),
    )(place, own, slots, slots, slots, *after)


def _half_exchange_start(buf, name):
    rh = buf.shape[0] // 2

    def body(buf_ref, send, recv, out_ref, token):
        token[...] = jnp.zeros_like(token)
        x, y, c, _ = _place()
        mine = out_ref.at[pl.ds(c * rh, rh), :]
        pltpu.make_async_remote_copy(src_ref=mine, dst_ref=mine, send_sem=send, recv_sem=recv,
                                     device_id=(x, y, 1 - c), device_id_type=MESH).start()

    return pl.pallas_call(
        body,
        name=name,
        in_specs=[HBM],
        out_specs=[SEM, SEM, HBM, VMEM_SPEC],
        out_shape=[pltpu.SemaphoreType.DMA(()), pltpu.SemaphoreType.DMA(()), pltpu.HBM(buf.shape, buf.dtype), TOKEN],
        input_output_aliases={0: 2},
        compiler_params=pltpu.CompilerParams(has_side_effects=EFFECT),
    )(_in_hbm(buf))


def _half_exchange_wait(send, recv, buf, after, name):
    rh = buf.shape[0] // 2

    def body(buf_ref, send_ref, recv_ref, *rest):
        x, y, c, _ = _place()
        cp = pltpu.make_async_remote_copy(src_ref=buf_ref.at[pl.ds(c * rh, rh), :], dst_ref=buf_ref.at[pl.ds((1 - c) * rh, rh), :],
                                          send_sem=send_ref, recv_sem=recv_ref, device_id=(x, y, 1 - c), device_id_type=MESH)
        cp.wait_send()
        cp.wait_recv()

    return pl.pallas_call(
        body,
        name=name,
        in_specs=[HBM, SEM, SEM] + [ANY] * len(after),
        out_specs=HBM,
        out_shape=pltpu.HBM(buf.shape, buf.dtype),
        input_output_aliases={0: 0},
        compiler_params=pltpu.CompilerParams(has_side_effects=EFFECT),
    )(buf, send, recv, *after)


def _half_exchange(bufs, name):
    n = len(bufs)

    def body(*refs):
        outs = refs[n:2 * n]
        send, recv = refs[2 * n:]
        x, y, c, _ = _place()
        copies = []
        for k in range(n):
            rh = bufs[k].shape[0] // 2
            mine = outs[k].at[pl.ds(c * rh, rh), :]
            cp = pltpu.make_async_remote_copy(src_ref=mine, dst_ref=mine, send_sem=send.at[k], recv_sem=recv.at[k],
                                              device_id=(x, y, 1 - c), device_id_type=MESH)
            cp.start()
            copies.append(cp)
        for k in range(n):
            rh = bufs[k].shape[0] // 2
            theirs = outs[k].at[pl.ds((1 - c) * rh, rh), :]
            copies[k].wait_send()
            pltpu.make_async_remote_copy(src_ref=theirs, dst_ref=theirs, send_sem=send.at[k], recv_sem=recv.at[k],
                                         device_id=(x, y, 1 - c), device_id_type=MESH).wait_recv()

    return pl.pallas_call(
        body,
        name=name,
        in_specs=[ANY] * n,
        out_specs=[ANY] * n,
        out_shape=[jax.ShapeDtypeStruct(a.shape, a.dtype) for a in bufs],
        input_output_aliases={k: k for k in range(n)},
        scratch_shapes=[pltpu.SemaphoreType.DMA((n,)), pltpu.SemaphoreType.DMA((n,))],
        compiler_params=pltpu.CompilerParams(has_side_effects=True),
    )(*bufs)


SMALL_ROWS = 32


def _small_peers():
    x, y, c, _ = _place()
    peers = []
    for k in range(1, N_DEV):
        px, py, pc = x ^ ((k >> 2) & 1), y ^ ((k >> 1) & 1), c ^ (k & 1)
        peers.append((k, (px, py, pc), 4 * px + 2 * py + pc))
    return 4 * x + 2 * y + c, peers


def _all_gather_small_start(gath):
    def body(in_ref, send, recv, out_ref, token):
        token[...] = jnp.zeros_like(token)
        me, peers = _small_peers()
        for k, peer, _ in peers:
            pltpu.make_async_remote_copy(src_ref=out_ref.at[me], dst_ref=out_ref.at[me], send_sem=send.at[k], recv_sem=recv.at[k],
                                         device_id=peer, device_id_type=MESH).start()

    return pl.pallas_call(
        body,
        name="small_gather_start",
        in_specs=[HBM],
        out_specs=[SEM, SEM, HBM, VMEM_SPEC],
        out_shape=[pltpu.SemaphoreType.DMA((N_DEV,)), pltpu.SemaphoreType.DMA((N_DEV,)), pltpu.HBM(gath.shape, gath.dtype), TOKEN],
        input_output_aliases={0: 2},
        compiler_params=pltpu.CompilerParams(has_side_effects=EFFECT),
    )(_in_hbm(gath))


def _all_gather_small_wait(send, recv, gath, after):
    def body(in_ref, send_ref, recv_ref, *rest):
        me, peers = _small_peers()
        for k, peer, peer_id in peers:
            cp = pltpu.make_async_remote_copy(src_ref=in_ref.at[me], dst_ref=in_ref.at[peer_id], send_sem=send_ref.at[k],
                                              recv_sem=recv_ref.at[k], device_id=peer, device_id_type=MESH)
            cp.wait_send()
            cp.wait_recv()

    return pl.pallas_call(
        body,
        name="small_gather_wait",
        in_specs=[HBM, SEM, SEM] + [ANY] * len(after),
        out_specs=HBM,
        out_shape=pltpu.HBM(gath.shape, gath.dtype),
        input_output_aliases={0: 0},
        compiler_params=pltpu.CompilerParams(has_side_effects=EFFECT),
    )(gath, send, recv, *after)


def _sum_small(gath):
    def body(g_ref, o_ref):
        acc = g_ref[0]
        for dev in range(1, N_DEV):
            acc = acc + g_ref[dev]
        o_ref[...] = acc

    return pl.pallas_call(
        body,
        name="small_sum",
        in_specs=[VMEM_SPEC],
        out_specs=VMEM_SPEC,
        out_shape=jax.ShapeDtypeStruct(gath.shape[1:], F32),
        compiler_params=pltpu.CompilerParams(vmem_limit_bytes=VMEM_LIMIT),
    )(gath)


def _adamw_update(g_ref, w_ref, m_ref, v_ref, go_ref, d_ref, mo_ref, vo_ref):
    bc1 = 1.0 - ADAM_B1 ** ADAM_STEP
    bc2 = 1.0 - ADAM_B2 ** ADAM_STEP
    gv = g_ref[...]
    mn = ADAM_B1 * m_ref[...] + (1.0 - ADAM_B1) * gv
    vn = ADAM_B2 * v_ref[...] + (1.0 - ADAM_B2) * (gv * gv)
    go_ref[...] = gv
    mo_ref[...] = mn
    vo_ref[...] = vn
    d_ref[...] = -ADAM_LR * ((mn / bc1) / (jnp.sqrt(vn / bc2) + ADAM_EPS) + ADAM_WD * w_ref[...])


def _adamw_small(gs, ws, ms, vs):
    n = len(gs)

    def body(*refs):
        ins, outs = refs[:4 * n], refs[4 * n:]
        for k in range(n):
            _adamw_update(*[ins[i * n + k] for i in range(4)], *outs[4 * k:4 * k + 4])

    vmem = pl.BlockSpec(memory_space=pltpu.VMEM)
    res = pl.pallas_call(
        body,
        name="adamw_small",
        in_specs=[vmem] * (4 * n),
        out_specs=[vmem] * (4 * n),
        out_shape=[jax.ShapeDtypeStruct(w.shape, F32) for w in ws for _ in range(4)],
        compiler_params=pltpu.CompilerParams(vmem_limit_bytes=VMEM_LIMIT),
    )(*gs, *ws, *ms, *vs)
    return [tuple(res[4 * k:4 * k + 4]) for k in range(n)]


def _adamw(g, w, m, v, name, g_block=0):
    r, cdim = w.shape
    tr = _row_tile(r, cdim, 4)

    def body(*refs):
        _adamw_update(*refs)

    spec = pl.BlockSpec((tr, cdim), lambda i: (i, 0))
    return pl.pallas_call(
        body,
        name=name,
        grid=(r // tr,),
        in_specs=[pl.BlockSpec((tr, cdim), lambda i: (i, g_block))] + [spec] * 3,
        out_specs=[spec] * 4,
        out_shape=[jax.ShapeDtypeStruct((r, cdim), F32)] * 4,
        compiler_params=_params(("parallel",)),
    )(g, w, m, v)


VEC_NAMES = ["pre_mix_norm", "q_norm", "kv_norm", "conv_b", "conv_ln_g", "conv_ln_b", "conv_out_norm",
             "attn_out_norm", "post_mix_norm", "pre_ffn_norm", "post_ffn_norm"]
LOSS_ROW = len(VEC_NAMES)
CONV_W_ROW = 16


def _cols_to_full(parts):
    _, r, cdim = parts.shape
    return parts.transpose(1, 0, 2).reshape(r, N_CHIPS * cdim)


def _full_to_cols(full):
    r, n = full.shape
    return full.reshape(r, N_CHIPS, n // N_CHIPS).transpose(1, 0, 2)


W_IN_SHARD = (2 * CONV_CH + Q_LORA + KV_LORA + QK_ROPE) // N_CHIPS
W_IN_PART = 1024
W_IN_BLOCKS = (2 * CONV_CH + Z2_COLS) // LANES
W_IN_BASE = [p * W_IN_SHARD // LANES for p in range(N_CHIPS)]
W_IN_SPAN = [-(-(p * W_IN_SHARD % LANES + W_IN_SHARD) // LANES) for p in range(N_CHIPS)]


def _w_in_block_home(b):
    n = CONV_CH // LANES
    if b < n:
        return 0, 2 * b
    if b < 2 * n:
        return 0, 2 * (b - n) + 1
    return 1, b - 2 * n


def _to_parts_w_in(shift_chip, w_in):
    r = w_in.shape[0]
    tr = 512

    def body(s_ref, a_ref, o_ref):
        o_ref[...] = jnp.zeros_like(o_ref)
        o_ref[:, :W_IN_SHARD] = a_ref[...].astype(o_ref.dtype)
        o_ref[...] = pltpu.roll(o_ref[...].astype(F32), s_ref[0], 1).astype(o_ref.dtype)

    return pl.pallas_call(
        body,
        name="to_parts_w_in",
        grid_spec=pltpu.PrefetchScalarGridSpec(
            num_scalar_prefetch=1,
            grid=(r // tr,),
            in_specs=[pl.BlockSpec((tr, W_IN_SHARD), lambda i, s_ref: (i, 0))],
            out_specs=pl.BlockSpec((None, tr, W_IN_PART), lambda i, s_ref: (s_ref[1], i, 0))),
        out_shape=jax.ShapeDtypeStruct((N_CHIPS, r, W_IN_PART), BF16),
        compiler_params=_params(("parallel",)),
    )(shift_chip, w_in)


def _assemble_w_in(parts):
    r = parts.shape[1]
    tr = ROW_TILE

    def body(p_ref, ag_ref, z2_ref):
        outs = (ag_ref, z2_ref)
        for b in range(W_IN_BLOCKS):
            blk = None
            for p in range(N_CHIPS):
                i = b - W_IN_BASE[p]
                if 0 <= i < W_IN_SPAN[p]:
                    piece = p_ref[p, :, i * LANES:(i + 1) * LANES]
                    blk = piece if blk is None else blk + piece
            which, at = _w_in_block_home(b)
            outs[which][:, at * LANES:(at + 1) * LANES] = blk

    w_ag, w_z2 = pl.pallas_call(
        body,
        name="assemble_w_in",
        grid=(r // tr,),
        in_specs=[pl.BlockSpec((N_CHIPS, tr, W_IN_PART), lambda i: (0, i, 0))],
        out_specs=[pl.BlockSpec((tr, 2 * CONV_CH), lambda i: (i, 0)), pl.BlockSpec((tr, Z2_COLS), lambda i: (i, 0))],
        out_shape=[jax.ShapeDtypeStruct((r, 2 * CONV_CH), parts.dtype), jax.ShapeDtypeStruct((r, Z2_COLS), parts.dtype)],
        compiler_params=_params(("parallel",)),
    )(parts)
    return dict(w_ag=w_ag, w_z2=w_z2)


def _w_in_grad_parts(dw_ag, dw_z2):
    r = dw_ag.shape[0]
    tr = ROW_TILE

    def body(ag_ref, z2_ref, o_ref):
        ins = (ag_ref, z2_ref)
        for p in range(N_CHIPS):
            for i in range(W_IN_PART // LANES):
                if i < W_IN_SPAN[p]:
                    which, at = _w_in_block_home(W_IN_BASE[p] + i)
                    o_ref[p, :, i * LANES:(i + 1) * LANES] = ins[which][:, at * LANES:(at + 1) * LANES]
                else:
                    o_ref[p, :, i * LANES:(i + 1) * LANES] = jnp.zeros((tr, LANES), o_ref.dtype)

    return pl.pallas_call(
        body,
        name="w_in_grad_parts",
        grid=(r // tr,),
        in_specs=[pl.BlockSpec((tr, 2 * CONV_CH), lambda i: (i, 0)), pl.BlockSpec((tr, Z2_COLS), lambda i: (i, 0))],
        out_specs=pl.BlockSpec((N_CHIPS, tr, W_IN_PART), lambda i: (0, i, 0)),
        out_shape=jax.ShapeDtypeStruct((N_CHIPS, r, W_IN_PART), dw_ag.dtype),
        compiler_params=_params(("parallel",)),
    )(dw_ag, dw_z2)


def _assemble_mixer_rest(g):
    uq = _cols_to_full(g["w_uq"]).reshape(Q_LORA, N_HEADS, QK_HEAD)
    w_uq = jnp.concatenate([uq[:, :, :QK_NOPE].reshape(Q_LORA, N_HEADS * QK_NOPE),
                            uq[:, :, QK_NOPE:].reshape(Q_LORA, N_HEADS * QK_ROPE)], axis=1)
    return dict(w_uq=w_uq, w_ukv=_cols_to_full(g["w_ukv"]), conv_w=_cols_to_full(g["conv_w"]),
                w_out=g["w_out"].reshape(-1, g["w_out"].shape[2]))


def _grads_to_parts(dw):
    uq = dw["w_uq"]
    d_uq = jnp.concatenate([uq[:, :N_HEADS * QK_NOPE].reshape(Q_LORA, N_HEADS, QK_NOPE),
                            uq[:, N_HEADS * QK_NOPE:].reshape(Q_LORA, N_HEADS, QK_ROPE)], axis=2).reshape(Q_LORA, N_HEADS * QK_HEAD)
    return dict(w_in=_w_in_grad_parts(dw["w_ag"], dw["w_z2"]), w_uq=_full_to_cols(d_uq), w_ukv=_full_to_cols(dw["w_ukv"]),
                w_out=dw["w_out"].reshape(N_CHIPS, -1, dw["w_out"].shape[1]))


MIXER = ["w_in", "w_uq", "w_ukv", "w_out"]
FFN = ["w_gu", "w_down"]
BIG = MIXER + FFN


def _pad_lanes(v, n):
    return jnp.pad(v, ((0, 0), (0, n - v.shape[1])))


def kernel(x, positions, pre_mix_norm, w_in, q_norm, w_uq, kv_norm, w_ukv, conv_w, conv_b, conv_ln_g, conv_ln_b, conv_out_norm, attn_out_norm, w_out, post_mix_norm, pre_ffn_norm, w_gate, w_up, w_down, post_ffn_norm, loss_target, m_pre_mix_norm, m_w_in, m_q_norm, m_w_uq, m_kv_norm, m_w_ukv, m_conv_w, m_conv_b, m_conv_ln_g, m_conv_ln_b, m_conv_out_norm, m_attn_out_norm, m_w_out, m_post_mix_norm, m_pre_ffn_norm, m_w_gate, m_w_up, m_w_down, m_post_ffn_norm, v_pre_mix_norm, v_w_in, v_q_norm, v_w_uq, v_kv_norm, v_w_ukv, v_conv_w, v_conv_b, v_conv_ln_g, v_conv_ln_b, v_conv_out_norm, v_attn_out_norm, v_w_out, v_post_mix_norm, v_pre_ffn_norm, v_w_gate, v_w_up, v_w_down, v_post_ffn_norm):
    given = dict(locals())
    names = ["pre_mix_norm", "w_in", "q_norm", "w_uq", "kv_norm", "w_ukv", "conv_w", "conv_b", "conv_ln_g", "conv_ln_b",
             "conv_out_norm", "attn_out_norm", "w_out", "post_mix_norm", "pre_ffn_norm", "w_gate", "w_up", "w_down", "post_ffn_norm"]
    def as_2d(a):
        return a if a.ndim == 2 else a[0]

    weights = {n: as_2d(given[n]) for n in names}
    mom = {n: as_2d(given["m_" + n]) for n in names}
    var = {n: as_2d(given["v_" + n]) for n in names}
    d = D_MODEL

    inv_freq = ROPE_THETA ** (-jnp.arange(0, QK_ROPE, 2, dtype=F32) / QK_ROPE)
    ang = positions[0].astype(F32)[:, None] * inv_freq
    cos, sin = jnp.cos(ang), jnp.sin(ang)
    cos2 = jnp.concatenate([cos, cos, cos, cos], axis=1)
    sin2 = jnp.concatenate([-sin, sin, -sin, sin], axis=1)

    chip = 2 * lax.axis_index("x") + lax.axis_index("y")
    core = lax.axis_index("c")
    chip1 = chip.astype(jnp.int32).reshape(1)
    pieces = {n: [weights[n]] for n in BIG if n != "w_gu"}
    pieces["w_gu"] = [weights["w_gate"], weights["w_up"]]
    core1 = core.astype(jnp.int32).reshape(1)
    place = jnp.stack([chip, core]).astype(jnp.int32)
    rest = ["w_uq", "w_ukv", "w_out"]
    w_in_shift = (chip * W_IN_SHARD) % LANES
    w_in_buf = _to_parts_w_in(jnp.stack([w_in_shift, chip]).astype(jnp.int32), weights["w_in"])
    (w_in_sems,), w_in_thru, _ = _gather_start([w_in_buf], [False], [[0]], "gather_start_w_in")
    rest_bufs = [_to_parts(chip1, pieces[n], BF16, "to_parts_" + n) for n in rest]
    rest_bufs.append(_to_parts(chip1, [jnp.pad(weights["conv_w"], ((0, CONV_K_PAD - CONV_K), (0, 0)))], F32, "to_parts_conv_w"))
    rest_whole = [False] * 3 + [True]
    ffn_bufs = [_to_parts(chip1, pieces[n], BF16, "to_parts_" + n) for n in FFN]
    got = _gather_wait(w_in_thru, [False], *w_in_sems, rest_bufs + ffn_bufs, "gather_wait_w_in")
    (rest_sems,), rest_thru, started = _gather_start(rest_bufs, rest_whole, [[0, 1, 2, 3]], "gather_start_mixer_rest", after=got)
    full = _assemble_w_in(_gather_hand_on(got, "gather_hand_on_w_in", after=[started])[0])
    vec = {n: weights[n] for n in VEC_NAMES}
    rs = {}

    def get_mixer_rest(after):
        got = _gather_wait(rest_thru, rest_whole, *rest_sems, [after], "gather_wait_mixer_rest")
        (rs["w_gu_sems"],), rs["w_gu_thru"], started = _gather_start(ffn_bufs[:1], [False], [[0]], "gather_start_w_gu", after=got[:1])
        got = list(_gather_hand_on(got[:3], "gather_hand_on_mixer_rest", after=[started])) + [got[3]]
        return _assemble_mixer_rest(dict(zip(rest + ["conv_w"], got)))

    def w_gu_landed(after):
        got = _gather_wait(rs["w_gu_thru"], [False], *rs["w_gu_sems"], [after], "gather_wait_w_gu")
        (rs["w_down_sems"],), rs["w_down_thru"], started = _gather_start(ffn_bufs[1:], [False], [[0]], "gather_start_w_down", after=got)
        rs["w_gu"] = _hand_on_start(got, "hand_on_start_w_gu", after=[started])
        return rs["w_gu"][3]

    def get_w_gu(after):
        send, recv, bufs, _ = rs["w_gu"]
        return _hand_on_wait(bufs, send, recv, [after], "hand_on_wait_w_gu")[0]

    def get_w_down(after):
        got = _gather_wait(rs["w_down_thru"], [False], *rs["w_down_sems"], [after], "gather_wait_w_down")
        got = _gather_hand_on(got, "gather_hand_on_w_down")[0]
        return got.reshape(-1, got.shape[2])

    def pair_start(key):
        def hook(dw):
            rs[key] = _pair_exchange_start(dw.reshape(N_CHIPS, -1, dw.shape[-1]), "grad_pair_start_" + key)
            return rs[key][4]
        return hook

    def reduce_start(group, plist, landed):
        sums = [_pair_sum(core1, a, b, "grad_pair_sum_%s_%d" % (group, k)) for k, (a, b) in enumerate(zip(plist, landed))]
        rs[group] = _chip_exchange_start(sums, "grad_chip_exchange_start_" + group)
        return rs[group][4]

    def reduce_finish(group, after):
        send, recv, sums, slots, _ = rs[group]
        sums, slots = _chip_exchange_wait(sums, slots, send, recv, after, "grad_chip_exchange_wait_" + group)
        halves = [_chip_sum(place, s, sl, "grad_chip_sum_%s_%d" % (group, k)) for k, (s, sl) in enumerate(zip(sums, slots))]
        return list(_half_exchange(halves, "grad_half_exchange_" + group))

    def ffn_grads_exchanged(after):
        pairs = [_pair_exchange_wait(*rs[key][:4], after, "grad_pair_wait_" + key) for key in ("dw_gu", "dw_down")]
        return reduce_start("ffn", [p[0] for p in pairs], [p[1] for p in pairs])

    hooks = dict(mixer_rest=get_mixer_rest, w_gu_landed=w_gu_landed, w_gu=get_w_gu, w_down=get_w_down, dw_down=pair_start("dw_down"), dw_gu=pair_start("dw_gu"),
                 grads_exchanged=ffn_grads_exchanged)
    loss, grad_x, dw, dvec = _local_step(x[0], loss_target[0], cos2, sin2, vec, full, hooks)

    rows = [_pad_lanes(dvec[n], d) for n in VEC_NAMES] + [_pad_lanes(loss, d)]
    rows.append(jnp.zeros((CONV_W_ROW - len(rows), d), F32))
    rows.append(dw["conv_w"].reshape(SMALL_ROWS - CONV_W_ROW, d))
    device1 = (2 * chip + core).astype(jnp.int32).reshape(1)
    small_gather = _all_gather_small_start(_to_parts(device1, [jnp.concatenate(rows, axis=0)], F32, "small_to_slot", n_parts=N_DEV))

    parts = _grads_to_parts(dw)
    plist = [parts[n] for n in MIXER]
    started = reduce_start("mixer", plist, _pair_exchange(plist, "grad_pair_exchange_mixer"))
    send, recv, sums, slots, _ = rs["ffn"]
    sums, slots = _chip_exchange_wait(sums, slots, send, recv, [started, small_gather[3]], "grad_chip_exchange_wait_ffn")
    down = _half_exchange_start(_chip_sum(place, sums[1], slots[1], "grad_chip_sum_ffn_1"), "grad_half_start_w_down")
    gu = _half_exchange_start(_chip_sum(place, sums[0], slots[0], "grad_chip_sum_ffn_0", after=[down[3]]), "grad_half_start_w_gu")
    res = {}
    g_down = _half_exchange_wait(*down[:3], [gu[3]], "grad_half_wait_w_down")
    res["w_down"] = _adamw(g_down, weights["w_down"], mom["w_down"], var["w_down"], "adamw_w_down")
    g_gu = _half_exchange_wait(*gu[:3], [res["w_down"][1]], "grad_half_wait_w_gu")
    for n, blk in (("w_gate", 0), ("w_up", 1)):
        res[n] = _adamw(g_gu, weights[n], mom[n], var[n], "adamw_" + n, g_block=blk)
    small = _sum_small(_all_gather_small_wait(*small_gather[:3], [res["w_up"][1]]))
    g_conv_w_full = small[CONV_W_ROW:].reshape(CONV_K_PAD, CONV_CH)
    g_small = {n: small[i:i + 1, :weights[n].shape[1]] for i, n in enumerate(VEC_NAMES)}
    g_small["conv_w"] = lax.dynamic_slice(g_conv_w_full, (0, chip * (CONV_CH // N_CHIPS)), (CONV_K_PAD, CONV_CH // N_CHIPS))[:CONV_K]
    loss_out = small[LOSS_ROW, 0]
    small_names = VEC_NAMES + ["conv_w"]
    res.update(zip(small_names, _adamw_small([g_small[n] for n in small_names], [weights[n] for n in small_names],
                                             [mom[n] for n in small_names], [var[n] for n in small_names])))
    done_meanwhile = [res["w_gate"][1], res["w_up"][1], res["w_down"][1], res["conv_w"][1], grad_x]
    g_mixer = reduce_finish("mixer", done_meanwhile)
    g_mixer[0] = lax.dynamic_slice(g_mixer[0], (0, w_in_shift), (g_mixer[0].shape[0], W_IN_SHARD))
    for n, g in zip(MIXER, g_mixer):
        res[n] = _adamw(g, weights[n], mom[n], var[n], "adamw_" + n)
    outs = [loss_out, grad_x[None]]
    for i in range(4):
        outs += [res[n][i].reshape(given[n].shape) for n in names]
    return tuple(outs)
```

```python
import functools

import jax
import jax.numpy as jnp
from jax import lax
from jax.experimental import pallas as pl
from jax.experimental.pallas import tpu as pltpu

F32 = jnp.float32
BF16 = jnp.bfloat16

D_MODEL = 2048
CONV_CH = 1024
CONV_K = 31
CONV_K_PAD = 32
N_HEADS = 8
QK_NOPE = 128
QK_ROPE = 64
V_HEAD = 128
QK_HEAD = QK_NOPE + QK_ROPE
Q_LORA = 768
KV_LORA = 512
ATTN_CH = N_HEADS * V_HEAD
Z2_COLS = Q_LORA + KV_LORA + 128
D_FF = 5632
ROPE_THETA = 10000.0
EPS = 1e-6
LANES = 128
N_CHIPS = 4
N_DEV = 8

ADAM_LR = 0.001
ADAM_B1 = 0.9
ADAM_B2 = 0.999
ADAM_EPS = 1e-08
ADAM_WD = 0.01
ADAM_STEP = 10

VMEM_LIMIT = 56 * 1024 * 1024
ROW_TILE = 256
MAX_TK = 2816
MESH = pl.DeviceIdType.MESH


def _params(sem=None):
    return pltpu.CompilerParams(dimension_semantics=sem, vmem_limit_bytes=VMEM_LIMIT)


def _first_divisor(n, cands):
    for c in cands:
        if n % c == 0:
            return c
    return n


STREAM_BLOCK_BYTES = 3 << 19


def _row_tile(rows, cols, itemsize):
    for tr in (1024, 704, 512, 384, 352, 256, 176, 128, 64, 32, 16):
        if rows % tr == 0 and tr * cols * itemsize <= STREAM_BLOCK_BYTES:
            return tr
    return rows


def _matmul(pairs, mode, out_dtype, name, b_parts=None, out_parts=False, tiles=(None, None, None), after=None):
    a0, b0 = pairs[0]
    part_c = b0.shape[2] if b_parts else None
    if mode == "nn":
        m, n = a0.shape[0], (N_CHIPS * part_c if b_parts else b0.shape[1])
        ks = [a.shape[1] for a, _ in pairs]
    elif mode == "nt":
        m, n = a0.shape[0], b0.shape[-2]
        ks = [a.shape[1] for a, _ in pairs]
    else:
        m, n = a0.shape[1], b0.shape[1]
        ks = [a.shape[0] for a, _ in pairs]
    tm = tiles[0] or _first_divisor(m, (1024, 768, 512, 256))
    tn = tiles[1] or (n if n <= 1536 else _first_divisor(n, (1024, 512, 256, 128)))
    tks = [tiles[2] or (k if k <= MAX_TK else MAX_TK) for k in ks]
    nks = [k // tk for k, tk in zip(ks, tks)]
    offs = [sum(nks[:p]) for p in range(len(pairs))]
    nk = sum(nks)
    n_pairs = len(pairs)
    assert not (b_parts or out_parts) or n_pairs == 1

    def kk(k, p):
        return jnp.clip(k - offs[p], 0, nks[p] - 1)

    in_specs = []
    for p in range(n_pairs):
        tk = tks[p]
        if mode == "nn":
            in_specs.append(pl.BlockSpec((tm, tk), lambda i, j, k, p=p: (i, kk(k, p))))
            if b_parts == "n":
                per = part_c // tn
                in_specs.append(pl.BlockSpec((None, tk, tn), lambda i, j, k: (j // per, k, j % per)))
            else:
                in_specs.append(pl.BlockSpec((tk, tn), lambda i, j, k, p=p: (kk(k, p), j)))
        elif mode == "nt":
            in_specs.append(pl.BlockSpec((tm, tk), lambda i, j, k, p=p: (i, kk(k, p))))
            if b_parts == "k":
                per = part_c // tk
                in_specs.append(pl.BlockSpec((None, tn, tk), lambda i, j, k: (k // per, j, k % per)))
            else:
                in_specs.append(pl.BlockSpec((tn, tk), lambda i, j, k, p=p: (j, kk(k, p))))
        else:
            in_specs.append(pl.BlockSpec((tk, tm), lambda i, j, k, p=p: (kk(k, p), i)))
            in_specs.append(pl.BlockSpec((tk, tn), lambda i, j, k, p=p: (kk(k, p), j)))
    if out_parts:
        out_per = (n // N_CHIPS) // tn
        out_spec = pl.BlockSpec((None, tm, tn), lambda i, j, k: (j // out_per, i, j % out_per))
        out_shape = jax.ShapeDtypeStruct((N_CHIPS, m, n // N_CHIPS), out_dtype)
    else:
        out_spec = pl.BlockSpec((tm, tn), lambda i, j, k: (i, j))
        out_shape = jax.ShapeDtypeStruct((m, n), out_dtype)
    dims = {"nn": (((1,), (0,)), ((), ())), "nt": (((1,), (1,)), ((), ())), "tn": (((0,), (0,)), ((), ()))}[mode]

    n_after = 0 if after is None else 1

    def body(*refs):
        o_ref = refs[2 * n_pairs + n_after]
        k = pl.program_id(2)

        def prod(p):
            return lax.dot_general(refs[2 * p][...], refs[2 * p + 1][...], dims, preferred_element_type=F32)

        if nk == 1:
            o_ref[...] = prod(0).astype(o_ref.dtype)
            return
        acc = refs[2 * n_pairs + n_after + 1]
        for p in range(n_pairs):
            first, last = offs[p], offs[p] + nks[p] - 1
            lo, hi = max(first, 1), min(last, nk - 2)
            if first == 0:
                @pl.when(k == 0)
                def _(p=p):
                    acc[...] = prod(p)

            if lo <= hi:
                @pl.when((k >= lo) & (k <= hi))
                def _(p=p):
                    acc[...] += prod(p)

            if last == nk - 1:
                @pl.when(k == nk - 1)
                def _(p=p):
                    o_ref[...] = (acc[...] + prod(p)).astype(o_ref.dtype)

    flat = [t for pr in pairs for t in pr] + ([] if after is None else [after])
    return pl.pallas_call(
        body,
        name=name,
        grid=(m // tm, n // tn, nk),
        in_specs=in_specs + [pl.BlockSpec(memory_space=pl.ANY)] * n_after,
        out_specs=out_spec,
        out_shape=out_shape,
        scratch_shapes=[pltpu.VMEM((tm, tn), F32)] if nk > 1 else [],
        compiler_params=_params(("parallel", "parallel", "arbitrary")),
    )(*flat)


F4 = D_FF // N_CHIPS
FFN_TM = 512
FFN_STRIP = 256


def _ffn_up(hf, w_gu):
    t, d = hf.shape

    def body(a_ref, b_ref, gu_ref, act_ref):
        for r in range(0, FFN_TM, FFN_STRIP):
            acc = jnp.dot(a_ref[r:r + FFN_STRIP, :], b_ref[...], preferred_element_type=F32)
            g = acc[:, :F4]
            gu_ref[r:r + FFN_STRIP, :] = acc.astype(gu_ref.dtype)
            act_ref[r:r + FFN_STRIP, :] = (g * _sigmoid(g) * acc[:, F4:]).astype(act_ref.dtype)

    return pl.pallas_call(
        body,
        name="ffn_up",
        grid=(N_CHIPS, t // FFN_TM),
        in_specs=[pl.BlockSpec((FFN_TM, d), lambda q, i: (i, 0)),
                  pl.BlockSpec((None, d, 2 * F4), lambda q, i: (q, 0, 0))],
        out_specs=[pl.BlockSpec((FFN_TM, 2 * F4), lambda q, i: (i, q)),
                   pl.BlockSpec((FFN_TM, F4), lambda q, i: (i, q))],
        out_shape=[jax.ShapeDtypeStruct((t, 2 * D_FF), BF16), jax.ShapeDtypeStruct((t, D_FF), BF16)],
        compiler_params=_params(("parallel", "parallel")),
    )(hf, w_gu)


def _ffn_down_dx(d_ff, w_down, gu):
    t, d = d_ff.shape

    def body(a_ref, b_ref, gu_ref, o_ref):
        for r in range(0, FFN_TM, FFN_STRIP):
            rows = slice(r, r + FFN_STRIP)
            d_act = lax.dot_general(a_ref[rows, :], b_ref[...], NT_DIMS, preferred_element_type=F32)
            g = gu_ref[rows, :F4].astype(F32)
            u = gu_ref[rows, F4:].astype(F32)
            sg = _sigmoid(g)
            dsg = d_act * sg
            o_ref[rows, :F4] = (dsg * u * (1.0 + g - g * sg)).astype(o_ref.dtype)
            o_ref[rows, F4:] = (dsg * g).astype(o_ref.dtype)

    return pl.pallas_call(
        body,
        name="ffn_down_dx",
        grid=(N_CHIPS, t // FFN_TM),
        in_specs=[pl.BlockSpec((FFN_TM, d), lambda q, i: (i, 0)),
                  pl.BlockSpec((F4, d), lambda q, i: (q, 0)),
                  pl.BlockSpec((FFN_TM, 2 * F4), lambda q, i: (i, q))],
        out_specs=pl.BlockSpec((FFN_TM, 2 * F4), lambda q, i: (i, q)),
        out_shape=jax.ShapeDtypeStruct((t, 2 * D_FF), BF16),
        compiler_params=_params(("parallel", "parallel")),
    )(d_ff, w_down, gu)


def _rowwise(fn, row_ins, vec_ins, row_outs, acc_outs, name, after=None):
    t = row_ins[0].shape[0]
    tm = ROW_TILE
    n_in = len(row_ins) + len(vec_ins)
    n_row = len(row_outs)
    extra = [] if after is None else [after]

    def body(*refs):
        ins = [r[...] for r in refs[:n_in]]
        outs = refs[n_in + len(extra):]
        vals = fn(*ins)
        for r, v in zip(outs[:n_row], vals[:n_row]):
            r[...] = v.astype(r.dtype)
        if acc_outs:
            @pl.when(pl.program_id(0) == 0)
            def _():
                for r in outs[n_row:]:
                    r[...] = jnp.zeros_like(r)

            for r, v in zip(outs[n_row:], vals[n_row:]):
                r[...] += v

    in_specs = [pl.BlockSpec((tm, a.shape[1]), lambda i: (i, 0)) for a in row_ins]
    in_specs += [pl.BlockSpec(a.shape, lambda i: (0, 0)) for a in vec_ins]
    out_specs = [pl.BlockSpec((tm, c), lambda i: (i, 0)) for c, _ in row_outs]
    out_specs += [pl.BlockSpec((1, c), lambda i: (0, 0)) for c in acc_outs]
    out_shape = [jax.ShapeDtypeStruct((t, c), dt) for c, dt in row_outs]
    out_shape += [jax.ShapeDtypeStruct((1, c), F32) for c in acc_outs]
    return pl.pallas_call(
        body,
        name=name,
        grid=(t // tm,),
        in_specs=in_specs + [pl.BlockSpec(memory_space=pl.ANY)] * len(extra),
        out_specs=out_specs,
        out_shape=out_shape,
        compiler_params=_params(("arbitrary",)),
    )(*row_ins, *vec_ins, *extra)


def _mean(v):
    return jnp.mean(v, axis=-1, keepdims=True)


def _colsum(v):
    return jnp.sum(v, axis=0, keepdims=True)


def _rms_fwd(v, g):
    r = lax.rsqrt(_mean(v * v) + EPS)
    vhat = v * r
    return vhat * g, vhat, r


def _rms_bwd(dn, vhat, r, g):
    dng = dn * g
    return r * (dng - vhat * _mean(dng * vhat)), _colsum(dn * vhat)


def _swap_rope_halves(v):
    n = v.shape[-1]
    lane = lax.broadcasted_iota(jnp.int32, v.shape, v.ndim - 1)
    return jnp.where(lane % QK_ROPE < QK_ROPE // 2, pltpu.roll(v, n - QK_ROPE // 2, v.ndim - 1),
                     pltpu.roll(v, QK_ROPE // 2, v.ndim - 1))


def _rope(v, cos2, sin2):
    return v * cos2 + _swap_rope_halves(v) * sin2


def _rope_transposed(dv, cos2, sin2):
    return dv * cos2 + _swap_rope_halves(dv * sin2)


def _sigmoid(v):
    return 1.0 / (1.0 + jnp.exp(-v))


CONV_ROWS = 256


def _conv_fwd(ag, conv_w, conv_b):
    t = ag.shape[0]
    cb = LANES

    def body(ag_ref, w_ref, b_ref, o_ref, scr):
        a = ag_ref[:, :cb].astype(F32)
        g = ag_ref[:, cb:].astype(F32)
        scr[pl.ds(0, CONV_K_PAD), :] = jnp.zeros((CONV_K_PAD, cb), F32)
        scr[pl.ds(CONV_K_PAD, t), :] = a * _sigmoid(g)
        for r0 in range(0, t, CONV_ROWS):
            acc = jnp.zeros((CONV_ROWS, cb), F32) + b_ref[...]
            for k in range(CONV_K):
                acc = acc + w_ref[k:k + 1, :] * scr[pl.ds(r0 + CONV_K_PAD - (CONV_K - 1) + k, CONV_ROWS), :]
            o_ref[pl.ds(r0, CONV_ROWS), :] = acc

    return pl.pallas_call(
        body,
        name="conv_fwd",
        grid=(CONV_CH // cb,),
        in_specs=[pl.BlockSpec((t, 2 * cb), lambda j: (0, j)),
                  pl.BlockSpec((CONV_K_PAD, cb), lambda j: (0, j)),
                  pl.BlockSpec((1, cb), lambda j: (0, j))],
        out_specs=pl.BlockSpec((t, cb), lambda j: (0, j)),
        out_shape=jax.ShapeDtypeStruct((t, CONV_CH), F32),
        scratch_shapes=[pltpu.VMEM((t + CONV_K_PAD, cb), F32)],
        compiler_params=_params(("parallel",)),
    )(ag, conv_w, conv_b)


def _conv_bwd(d_u1, ag, conv_w):
    t = ag.shape[0]
    cb = LANES

    def body(du_ref, ag_ref, w_ref, dag_ref, dw_ref, db_ref, su, sd):
        a = ag_ref[:, :cb].astype(F32)
        g = ag_ref[:, cb:].astype(F32)
        sg = _sigmoid(g)
        su[pl.ds(0, CONV_K_PAD), :] = jnp.zeros((CONV_K_PAD, cb), F32)
        su[pl.ds(CONV_K_PAD, t), :] = a * sg
        sd[pl.ds(0, t), :] = du_ref[...]
        sd[pl.ds(t, CONV_K_PAD), :] = jnp.zeros((CONV_K_PAD, cb), F32)
        db_ref[...] = _colsum(du_ref[...])
        dw_ref[...] = jnp.zeros_like(dw_ref)
        for r0 in range(0, t, CONV_ROWS):
            du = sd[pl.ds(r0, CONV_ROWS), :]
            acc = jnp.zeros((CONV_ROWS, cb), F32)
            for k in range(CONV_K):
                acc = acc + w_ref[k:k + 1, :] * sd[pl.ds(r0 + (CONV_K - 1) - k, CONV_ROWS), :]
                dw_ref[k:k + 1, :] += _colsum(du * su[pl.ds(r0 + CONV_K_PAD - (CONV_K - 1) + k, CONV_ROWS), :])
            sgc = sg[r0:r0 + CONV_ROWS]
            ac = a[r0:r0 + CONV_ROWS]
            dag_ref[pl.ds(r0, CONV_ROWS), :cb] = (acc * sgc).astype(dag_ref.dtype)
            dag_ref[pl.ds(r0, CONV_ROWS), cb:] = (acc * ac * sgc * (1.0 - sgc)).astype(dag_ref.dtype)

    return pl.pallas_call(
        body,
        name="conv_bwd",
        grid=(CONV_CH // cb,),
        in_specs=[pl.BlockSpec((t, cb), lambda j: (0, j)),
                  pl.BlockSpec((t, 2 * cb), lambda j: (0, j)),
                  pl.BlockSpec((CONV_K_PAD, cb), lambda j: (0, j))],
        out_specs=[pl.BlockSpec((t, 2 * cb), lambda j: (0, j)),
                   pl.BlockSpec((CONV_K_PAD, cb), lambda j: (0, j)),
                   pl.BlockSpec((1, cb), lambda j: (0, j))],
        out_shape=[jax.ShapeDtypeStruct((t, 2 * CONV_CH), BF16),
                   jax.ShapeDtypeStruct((CONV_K_PAD, CONV_CH), F32),
                   jax.ShapeDtypeStruct((1, CONV_CH), F32)],
        scratch_shapes=[pltpu.VMEM((t + CONV_K_PAD, cb), F32), pltpu.VMEM((t + CONV_K_PAD, cb), F32)],
        compiler_params=_params(("parallel",)),
    )(d_u1, ag, conv_w)


ATT_TQ = 256
NEG = float(jnp.finfo(jnp.float32).min)
SCALE = QK_HEAD ** -0.5
NT_DIMS = (((1,), (1,)), ((), ()))
TN_DIMS = (((0,), (0,)), ((), ()))


def _att_weights(qf, kf, row0):
    s = lax.dot_general(qf, kf, NT_DIMS, preferred_element_type=F32)
    tq, t = s.shape
    qpos = row0 + lax.broadcasted_iota(jnp.int32, (tq, t), 0)
    kpos = lax.broadcasted_iota(jnp.int32, (tq, t), 1)
    s = jnp.where(kpos <= qpos, s, NEG)
    p = jnp.exp(s - jnp.max(s, axis=-1, keepdims=True))
    return p, 1.0 / jnp.sum(p, axis=-1, keepdims=True)


def _scaled_query(qn, roped_half):
    return jnp.concatenate([(qn.astype(F32) * SCALE).astype(BF16), (roped_half * SCALE).astype(BF16)], axis=1)


def _half_mask(shape, which):
    lane = lax.broadcasted_iota(jnp.int32, shape, len(shape) - 1)
    return (lane // QK_ROPE == which).astype(F32)


def _attention_fwd(q, kv, kpe2, cos2, sin2):
    t = q.shape[0]
    tq = ATT_TQ

    def body(qn_ref, qp_ref, c_ref, s_ref, kv_ref, kpe_ref, o_ref):
        roped = _rope(qp_ref[...].astype(F32), c_ref[...], s_ref[...])

        def block(i):
            keys = slice(0, (i + 1) * tq)
            for e in range(2):
                qf = _scaled_query(qn_ref[:, e * QK_NOPE:(e + 1) * QK_NOPE], roped * _half_mask(roped.shape, e))
                kf = jnp.concatenate([kv_ref[keys, e * 256:e * 256 + QK_NOPE], kpe_ref[keys, :]], axis=1)
                p, inv_l = _att_weights(qf, kf, i * tq)
                v = kv_ref[keys, e * 256 + QK_NOPE:(e + 1) * 256]
                o = jnp.dot(p.astype(BF16), v, preferred_element_type=F32) * inv_l
                o_ref[:, e * V_HEAD:(e + 1) * V_HEAD] = o.astype(o_ref.dtype)

        for i in range(t // tq):
            pl.when(pl.program_id(1) == i)(functools.partial(block, i))

    return pl.pallas_call(
        body,
        name="attention_fwd",
        grid=(N_HEADS // 2, t // tq),
        in_specs=[pl.BlockSpec((tq, 2 * QK_NOPE), lambda h, i: (i, h)),
                  pl.BlockSpec((tq, LANES), lambda h, i: (i, N_HEADS + h)),
                  pl.BlockSpec((tq, LANES), lambda h, i: (i, 0)),
                  pl.BlockSpec((tq, LANES), lambda h, i: (i, 0)),
                  pl.BlockSpec((t, 512), lambda h, i: (0, h)),
                  pl.BlockSpec((t, LANES), lambda h, i: (0, 0))],
        out_specs=pl.BlockSpec((tq, 2 * V_HEAD), lambda h, i: (i, h)),
        out_shape=jax.ShapeDtypeStruct((t, ATTN_CH), BF16),
        compiler_params=_params(("parallel", "parallel")),
    )(q, q, cos2, sin2, kv, kpe2)


def _attention_bwd(q, kv, kpe2, cos2, sin2, d_attn):
    t = q.shape[0]
    tq = ATT_TQ
    n_q = t // tq

    def body(qn_ref, qp_ref, c_ref, s_ref, kv_ref, kpe_ref, do_ref, dqn_ref, dqp_ref, dkv_ref, dkpe_ref, dkv_acc):
        h, i = pl.program_id(0), pl.program_id(1)

        @pl.when(i == 0)
        def _():
            dkv_acc[...] = jnp.zeros_like(dkv_acc)

        @pl.when((i == 0) & (h == 0))
        def _():
            dkpe_ref[...] = jnp.zeros_like(dkpe_ref)

        roped = _rope(qp_ref[...].astype(F32), c_ref[...], s_ref[...])

        def block(ib):
            keys = slice(0, (ib + 1) * tq)
            d_roped = jnp.zeros((tq, LANES), F32)
            for e in range(2):
                mask = _half_mask(roped.shape, e)
                qf = _scaled_query(qn_ref[:, e * QK_NOPE:(e + 1) * QK_NOPE], roped * mask)
                kf = jnp.concatenate([kv_ref[keys, e * 256:e * 256 + QK_NOPE], kpe_ref[keys, :]], axis=1)
                v = kv_ref[keys, e * 256 + QK_NOPE:(e + 1) * 256]
                p, inv_l = _att_weights(qf, kf, ib * tq)
                do = (do_ref[:, e * V_HEAD:(e + 1) * V_HEAD].astype(F32) * inv_l).astype(BF16)
                dp = lax.dot_general(do, v, NT_DIMS, preferred_element_type=F32)
                ds = (p * (dp - jnp.sum(p * dp, axis=-1, keepdims=True) * inv_l)).astype(BF16)
                dqf = jnp.dot(ds, kf, preferred_element_type=F32) * SCALE
                dkf = lax.dot_general(ds, qf, TN_DIMS, preferred_element_type=F32)
                dv = lax.dot_general(p.astype(BF16), do, TN_DIMS, preferred_element_type=F32)
                dqn_ref[:, e * QK_NOPE:(e + 1) * QK_NOPE] = dqf[:, :QK_NOPE].astype(dqn_ref.dtype)
                d_roped = d_roped + dqf[:, QK_NOPE:] * mask
                dkv_acc[keys, e * 256:e * 256 + QK_NOPE] += dkf[:, :QK_NOPE]
                dkv_acc[keys, e * 256 + QK_NOPE:(e + 1) * 256] += dv
                dkpe_ref[keys, :] += dkf[:, QK_NOPE:]
            dqp_ref[...] = _rope_transposed(d_roped, c_ref[...], s_ref[...]).astype(dqp_ref.dtype)

        for ib in range(n_q):
            pl.when(i == ib)(functools.partial(block, ib))

        @pl.when(i == n_q - 1)
        def _():
            dkv_ref[...] = dkv_acc[...].astype(dkv_ref.dtype)

    return pl.pallas_call(
        body,
        name="attention_bwd",
        grid=(N_HEADS // 2, n_q),
        in_specs=[pl.BlockSpec((tq, 2 * QK_NOPE), lambda h, i: (i, h)),
                  pl.BlockSpec((tq, LANES), lambda h, i: (i, N_HEADS + h)),
                  pl.BlockSpec((tq, LANES), lambda h, i: (i, 0)),
                  pl.BlockSpec((tq, LANES), lambda h, i: (i, 0)),
                  pl.BlockSpec((t, 512), lambda h, i: (0, h)),
                  pl.BlockSpec((t, LANES), lambda h, i: (0, 0)),
                  pl.BlockSpec((tq, 2 * V_HEAD), lambda h, i: (i, h))],
        out_specs=[pl.BlockSpec((tq, 2 * QK_NOPE), lambda h, i: (i, h)),
                   pl.BlockSpec((tq, LANES), lambda h, i: (i, h)),
                   pl.BlockSpec((t, 512), lambda h, i: (0, h)),
                   pl.BlockSpec((t, LANES), lambda h, i: (0, 0))],
        out_shape=[jax.ShapeDtypeStruct((t, N_HEADS * QK_NOPE), BF16),
                   jax.ShapeDtypeStruct((t, N_HEADS * QK_ROPE), BF16),
                   jax.ShapeDtypeStruct((t, N_HEADS * 256), BF16),
                   jax.ShapeDtypeStruct((t, LANES), F32)],
        scratch_shapes=[pltpu.VMEM((t, 512), F32)],
        compiler_params=_params(("arbitrary", "arbitrary")),
    )(q, q, cos2, sin2, kv, kpe2, d_attn)


def _local_step(x, target, cos2, sin2, vec, w, ffn):
    d = D_MODEL

    (h,) = _rowwise(lambda xv, g: (_rms_fwd(xv, g)[0],), [x], [vec["pre_mix_norm"]], [(d, BF16)], [], "pre_mix_norm_fwd")
    ag = _matmul([(h, w["w_ag"])], "nn", BF16, "in_proj_ag")
    z2 = _matmul([(h, w["w_z2"])], "nn", BF16, "in_proj_z2")
    w = {**w, **ffn["mixer_rest"](z2)}
    u1 = _conv_fwd(ag, w["conv_w"], vec["conv_b"])

    def latents_fwd(z, c2, s2, qg, kvg):
        z = z.astype(F32)
        qn = _rms_fwd(z[:, :Q_LORA], qg)[0]
        kvn = _rms_fwd(z[:, Q_LORA:Q_LORA + KV_LORA], kvg)[0]
        kr = z[:, Q_LORA + KV_LORA:]
        kr2 = kr + pltpu.roll(kr, QK_ROPE, 1)
        return qn, kvn, _rope(kr2, c2, s2)

    qn, kvn, kpe2 = _rowwise(latents_fwd, [z2, cos2, sin2], [vec["q_norm"], vec["kv_norm"]],
                             [(Q_LORA, BF16), (KV_LORA, BF16), (LANES, BF16)], [], "latents_fwd")
    q = _matmul([(qn, w["w_uq"])], "nn", BF16, "q_up")
    kv = _matmul([(kvn, w["w_ukv"])], "nn", BF16, "kv_up")
    attn = _attention_fwd(q, kv, kpe2, cos2, sin2)

    def conv_post(u, lg, lb):
        mu = _mean(u)
        uc = u - mu
        rstd = lax.rsqrt(_mean(uc * uc) + EPS)
        uhat = uc * rstd
        u2 = uhat * lg + lb
        sg = _sigmoid(u2)
        return uhat, rstd, u2, sg, u2 * sg

    def mix_in_fwd(u, at, lg, lb, cg, ag_):
        u3 = conv_post(u, lg, lb)[4]
        cn = _rms_fwd(u3, cg)[0]
        an = _rms_fwd(at.astype(F32), ag_)[0]
        return (jnp.concatenate([cn, an], axis=1),)

    (cat,) = _rowwise(mix_in_fwd, [u1, attn], [vec["conv_ln_g"], vec["conv_ln_b"], vec["conv_out_norm"], vec["attn_out_norm"]],
                      [(2 * CONV_CH, BF16)], [], "mix_in_fwd")
    mix = _matmul([(cat, w["w_out"])], "nn", F32, "out_proj")
    landed = ffn["w_gu_landed"](mix)

    def residual1(xv, mv, gpm, gpf):
        x1 = xv + _rms_fwd(mv, gpm)[0]
        return x1, _rms_fwd(x1, gpf)[0]

    x1, hf = _rowwise(residual1, [x, mix], [vec["post_mix_norm"], vec["pre_ffn_norm"]], [(d, F32), (d, BF16)], [],
                      "residual1_fwd", after=landed)
    w_gu = ffn["w_gu"](hf)
    gu, act = _ffn_up(hf, w_gu)
    w_down = ffn["w_down"](act)
    ff = _matmul([(act, w_down)], "nn", F32, "ffn_down")

    def loss_head(x1v, ffv, tg, g):
        n, fhat, r = _rms_fwd(ffv, g)
        err = x1v + n - tg
        loss = 0.5 * jnp.sum(_mean(err * err), axis=0, keepdims=True)
        dy = err * (1.0 / d)
        d_ff, dg = _rms_bwd(dy, fhat, r, g)
        return dy, d_ff, dg, jnp.broadcast_to(loss, (1, LANES))

    dy, d_ff, g_post_ffn, loss = _rowwise(loss_head, [x1, ff, target], [vec["post_ffn_norm"]],
                                          [(d, F32), (d, BF16)], [d, LANES], "loss_head")
    d_gu = _ffn_down_dx(d_ff, w_down, gu)
    dw_down = _matmul([(act, d_ff)], "tn", BF16, "ffn_down_dw", tiles=(F4, None, None))
    started = ffn["dw_down"](dw_down)
    dw_gu = _matmul([(hf, d_gu)], "tn", BF16, "ffn_gate_up_dw", out_parts=True, tiles=(None, F4, None), after=started)
    started = ffn["dw_gu"](dw_gu)
    d_hf = _matmul([(d_gu, w_gu)], "nt", F32, "ffn_gate_up_dx", b_parts="k", after=started)
    started = ffn["grads_exchanged"](d_hf)

    def residual1_bwd(dyv, dhf, x1v, mv, gpf, gpm):
        _, x1hat, r1 = _rms_fwd(x1v, gpf)
        dn, dgpf = _rms_bwd(dhf, x1hat, r1, gpf)
        d_x1 = dyv + dn
        _, mhat, rm = _rms_fwd(mv, gpm)
        d_mix, dgpm = _rms_bwd(d_x1, mhat, rm, gpm)
        return d_x1, d_mix, dgpf, dgpm

    d_x1, d_mix, g_pre_ffn, g_post_mix = _rowwise(residual1_bwd, [dy, d_hf, x1, mix], [vec["pre_ffn_norm"], vec["post_mix_norm"]],
                                                  [(d, F32), (d, BF16)], [d, d], "residual1_bwd", after=started)
    d_cat = _matmul([(d_mix, w["w_out"])], "nt", BF16, "out_proj_dx")
    dw_out = _matmul([(cat, d_mix)], "tn", BF16, "out_proj_dw")

    def mix_in_bwd(dc, u, at, lg, lb, cg, ag_):
        dc = dc.astype(F32)
        uhat, rstd, u2, sg, u3 = conv_post(u, lg, lb)
        _, u3hat, rc = _rms_fwd(u3, cg)
        d_u3, dcg = _rms_bwd(dc[:, :CONV_CH], u3hat, rc, cg)
        d_u2 = d_u3 * sg * (1.0 + u2 * (1.0 - sg))
        dgl = d_u2 * lg
        d_u1 = rstd * (dgl - _mean(dgl) - uhat * _mean(dgl * uhat))
        _, ahat, ra = _rms_fwd(at.astype(F32), ag_)
        d_at, dag = _rms_bwd(dc[:, CONV_CH:], ahat, ra, ag_)
        return d_u1, d_at, dcg, _colsum(d_u2 * uhat), _colsum(d_u2), dag

    d_u1, d_attn, g_conv_out, g_ln_g, g_ln_b, g_attn_out = _rowwise(
        mix_in_bwd, [d_cat, u1, attn], [vec["conv_ln_g"], vec["conv_ln_b"], vec["conv_out_norm"], vec["attn_out_norm"]],
        [(CONV_CH, F32), (ATTN_CH, BF16)], [CONV_CH] * 4, "mix_in_bwd")
    d_ag, d_conv_w, g_conv_b = _conv_bwd(d_u1, ag, w["conv_w"])
    d_qn_, d_qp_, d_kv, d_kpe2 = _attention_bwd(q, kv, kpe2, cos2, sin2, d_attn)
    d_q = jnp.concatenate([d_qn_, d_qp_], axis=1)
    d_qn = _matmul([(d_q, w["w_uq"])], "nt", BF16, "q_up_dx")
    dw_uq = _matmul([(qn, d_q)], "tn", BF16, "q_up_dw")
    d_kvn = _matmul([(d_kv, w["w_ukv"])], "nt", BF16, "kv_up_dx")
    dw_ukv = _matmul([(kvn, d_kv)], "tn", BF16, "kv_up_dw")

    def latents_bwd(z, dq, dk, dkp, c2, s2, qg, kvg):
        z = z.astype(F32)
        _, qhat, rq = _rms_fwd(z[:, :Q_LORA], qg)
        d_ql, dqg = _rms_bwd(dq.astype(F32), qhat, rq, qg)
        _, khat, rk = _rms_fwd(z[:, Q_LORA:Q_LORA + KV_LORA], kvg)
        d_kl, dkg = _rms_bwd(dk.astype(F32), khat, rk, kvg)
        both = dkp + pltpu.roll(dkp, QK_ROPE, 1)
        d_kr = _rope_transposed(both, c2, s2) * _half_mask(both.shape, 0)
        return jnp.concatenate([d_ql, d_kl, d_kr], axis=1), dqg, dkg

    d_z2, g_q_norm, g_kv_norm = _rowwise(latents_bwd, [z2, d_qn, d_kvn, d_kpe2, cos2, sin2], [vec["q_norm"], vec["kv_norm"]],
                                         [(Z2_COLS, BF16)], [Q_LORA, KV_LORA], "latents_bwd")
    d_h = _matmul([(d_ag, w["w_ag"]), (d_z2, w["w_z2"])], "nt", F32, "in_proj_dx")
    dw_ag = _matmul([(h, d_ag)], "tn", BF16, "in_proj_ag_dw")
    dw_z2 = _matmul([(h, d_z2)], "tn", BF16, "in_proj_z2_dw")

    def pre_mix_bwd(dx1, dh, xv, g):
        _, xhat, r = _rms_fwd(xv, g)
        dn, dg = _rms_bwd(dh, xhat, r, g)
        return dx1 + dn, dg

    grad_x, g_pre_mix = _rowwise(pre_mix_bwd, [d_x1, d_h, x], [vec["pre_mix_norm"]], [(d, F32)], [d], "pre_mix_norm_bwd")

    dw = dict(w_ag=dw_ag, w_z2=dw_z2, w_uq=dw_uq, w_ukv=dw_ukv, conv_w=d_conv_w, w_out=dw_out, w_gu=dw_gu, w_down=dw_down)
    dvec = dict(pre_mix_norm=g_pre_mix, q_norm=g_q_norm, kv_norm=g_kv_norm, conv_b=g_conv_b, conv_ln_g=g_ln_g,
                conv_ln_b=g_ln_b, conv_out_norm=g_conv_out, attn_out_norm=g_attn_out, post_mix_norm=g_post_mix,
                pre_ffn_norm=g_pre_ffn, post_ffn_norm=g_post_ffn)
    return loss, grad_x, dw, dvec


ANY = pl.BlockSpec(memory_space=pl.ANY)


def _place():
    x, y, c = lax.axis_index("x"), lax.axis_index("y"), lax.axis_index("c")
    chips = [(1 - x, y), (x, 1 - y), (1 - x, 1 - y)]
    return x, y, c, chips


def _to_parts(chip, pieces, dtype, name, n_parts=N_CHIPS):
    r = pieces[0].shape[0]
    widths = [a.shape[1] for a in pieces]
    tr = r if r <= 512 else _first_divisor(r, (512, 256, 128))

    def body(p_ref, *refs):
        o_ref = refs[len(pieces)]
        off = 0
        for a_ref, wdt in zip(refs, widths):
            o_ref[:, off:off + wdt] = a_ref[...].astype(o_ref.dtype)
            off += wdt

    return pl.pallas_call(
        body,
        name=name,
        grid_spec=pltpu.PrefetchScalarGridSpec(
            num_scalar_prefetch=1,
            grid=(r // tr,),
            in_specs=[pl.BlockSpec((tr, wdt), lambda i, p_ref: (i, 0)) for wdt in widths],
            out_specs=pl.BlockSpec((None, tr, sum(widths)), lambda i, p_ref: (p_ref[0], i, 0))),
        out_shape=jax.ShapeDtypeStruct((n_parts, r, sum(widths)), dtype),
        compiler_params=_params(("parallel",)),
    )(chip, *pieces)


HBM = pl.BlockSpec(memory_space=pltpu.HBM)
SEM = pl.BlockSpec(memory_space=pltpu.SEMAPHORE)
EFFECT = pltpu.SideEffectType.DATAFLOW_SIDE_EFFECTING
VMEM_SPEC = pl.BlockSpec(memory_space=pltpu.VMEM)
TOKEN = jax.ShapeDtypeStruct((8, LANES), F32)


def _in_hbm(a):
    return pltpu.with_memory_space_constraint(a, pltpu.HBM)


def _gather_rows(buf, whole, half):
    r = buf.shape[1]
    return pl.ds(0, r) if whole else pl.ds(half * (r // 2), r // 2)


def _gather_start(bufs, whole, groups, name, after=()):
    n = len(bufs)
    n_g = len(groups)

    def body(*refs):
        refs = refs[:n] + refs[n + len(after):]
        sems = refs[n:n + 2 * n_g]
        outs = refs[n + 2 * n_g:2 * n + 2 * n_g]
        token = refs[2 * n + 2 * n_g]
        token[...] = jnp.zeros_like(token)
        x, y, c, chips = _place()
        p = 2 * x + y
        for gi, group in enumerate(groups):
            for ki, k in enumerate(group):
                blk = outs[k].at[p, _gather_rows(bufs[k], whole[k], c), :]
                for j, (px, py) in enumerate(chips):
                    pltpu.make_async_remote_copy(src_ref=blk, dst_ref=blk, send_sem=sems[2 * gi].at[3 * ki + j],
                                                 recv_sem=sems[2 * gi + 1].at[3 * ki + j],
                                                 device_id=(px, py, c), device_id_type=MESH).start()

    sem_shapes = []
    for group in groups:
        sem_shapes += [pltpu.SemaphoreType.DMA((3 * len(group),))] * 2
    res = pl.pallas_call(
        body,
        name=name,
        in_specs=[HBM] * n + [ANY] * len(after),
        out_specs=[SEM] * (2 * n_g) + [HBM] * n + [VMEM_SPEC],
        out_shape=sem_shapes + [pltpu.HBM(a.shape, a.dtype) for a in bufs] + [TOKEN],
        input_output_aliases={k: 2 * n_g + k for k in range(n)},
        compiler_params=pltpu.CompilerParams(has_side_effects=EFFECT),
    )(*[_in_hbm(a) for a in bufs], *after)
    sems = [(res[2 * gi], res[2 * gi + 1]) for gi in range(n_g)]
    return sems, list(res[2 * n_g:2 * n_g + n]), res[2 * n_g + n]


def _gather_wait(bufs, whole, send, recv, after, name):
    n = len(bufs)

    def body(*refs):
        ins = refs[:n]
        send_ref, recv_ref = refs[n], refs[n + 1]
        x, y, c, chips = _place()
        p = 2 * x + y
        for ki in range(n):
            rows = _gather_rows(bufs[ki], whole[ki], c)
            for j, (px, py) in enumerate(chips):
                cp = pltpu.make_async_remote_copy(src_ref=ins[ki].at[p, rows, :], dst_ref=ins[ki].at[2 * px + py, rows, :],
                                                  send_sem=send_ref.at[3 * ki + j], recv_sem=recv_ref.at[3 * ki + j],
                                                  device_id=(px, py, c), device_id_type=MESH)
                cp.wait_send()
                cp.wait_recv()

    res = pl.pallas_call(
        body,
        name=name,
        in_specs=[HBM] * n + [SEM, SEM] + [ANY] * len(after),
        out_specs=[HBM] * n,
        out_shape=[pltpu.HBM(a.shape, a.dtype) for a in bufs],
        input_output_aliases={k: k for k in range(n)},
        compiler_params=pltpu.CompilerParams(has_side_effects=EFFECT),
    )(*bufs, send, recv, *after)
    return list(res)


def _gather_hand_on(bufs, name, after=()):
    n = len(bufs)

    def body(*refs):
        refs = refs[n + len(after):]
        outs = refs[:n]
        send, recv = refs[n:]
        x, y, c, chips = _place()

        def d2d(k, j, half):
            px, py = chips[j]
            blk = outs[k].at[2 * px + py, _gather_rows(bufs[k], False, half), :]
            return pltpu.make_async_remote_copy(src_ref=blk, dst_ref=blk, send_sem=send.at[3 * k + j], recv_sem=recv.at[3 * k + j],
                                                device_id=(x, y, 1 - c), device_id_type=MESH)

        sent = [d2d(k, j, c) for k in range(n) for j in range(3)]
        for cp in sent:
            cp.start()
        for k in range(n):
            for j in range(3):
                d2d(k, j, 1 - c).wait_recv()
        for cp in sent:
            cp.wait_send()

    return pl.pallas_call(
        body,
        name=name,
        in_specs=[ANY] * (n + len(after)),
        out_specs=[ANY] * n,
        out_shape=[jax.ShapeDtypeStruct(a.shape, a.dtype) for a in bufs],
        input_output_aliases={k: k for k in range(n)},
        scratch_shapes=[pltpu.SemaphoreType.DMA((3 * n,)), pltpu.SemaphoreType.DMA((3 * n,))],
        compiler_params=pltpu.CompilerParams(has_side_effects=True),
    )(*bufs, *after)


def _hand_on_start(bufs, name, after=()):
    n = len(bufs)

    def body(*refs):
        refs = refs[:n] + refs[n + len(after):]
        send, recv = refs[n], refs[n + 1]
        outs = refs[n + 2:2 * n + 2]
        refs[2 * n + 2][...] = jnp.zeros(TOKEN.shape, TOKEN.dtype)
        x, y, c, chips = _place()
        for k in range(n):
            for j, (px, py) in enumerate(chips):
                blk = outs[k].at[2 * px + py, _gather_rows(bufs[k], False, c), :]
                pltpu.make_async_remote_copy(src_ref=blk, dst_ref=blk, send_sem=send.at[3 * k + j], recv_sem=recv.at[3 * k + j],
                                             device_id=(x, y, 1 - c), device_id_type=MESH).start()

    res = pl.pallas_call(
        body,
        name=name,
        in_specs=[HBM] * n + [ANY] * len(after),
        out_specs=[SEM, SEM] + [HBM] * n + [VMEM_SPEC],
        out_shape=[pltpu.SemaphoreType.DMA((3 * n,))] * 2 + [pltpu.HBM(a.shape, a.dtype) for a in bufs] + [TOKEN],
        input_output_aliases={k: 2 + k for k in range(n)},
        compiler_params=pltpu.CompilerParams(has_side_effects=EFFECT),
    )(*[_in_hbm(a) for a in bufs], *after)
    return res[0], res[1], list(res[2:2 + n]), res[2 + n]


def _hand_on_wait(bufs, send, recv, after, name):
    n = len(bufs)

    def body(*refs):
        ins = refs[:n]
        send_ref, recv_ref = refs[n], refs[n + 1]
        x, y, c, chips = _place()
        for k in range(n):
            for j, (px, py) in enumerate(chips):
                q = 2 * px + py
                cp = pltpu.make_async_remote_copy(src_ref=ins[k].at[q, _gather_rows(bufs[k], False, c), :],
                                                  dst_ref=ins[k].at[q, _gather_rows(bufs[k], False, 1 - c), :],
                                                  send_sem=send_ref.at[3 * k + j], recv_sem=recv_ref.at[3 * k + j],
                                                  device_id=(x, y, 1 - c), device_id_type=MESH)
                cp.wait_send()
                cp.wait_recv()

    res = pl.pallas_call(
        body,
        name=name,
        in_specs=[HBM] * n + [SEM, SEM] + [ANY] * len(after),
        out_specs=[HBM] * n,
        out_shape=[pltpu.HBM(a.shape, a.dtype) for a in bufs],
        input_output_aliases={k: k for k in range(n)},
        compiler_params=pltpu.CompilerParams(has_side_effects=EFFECT),
    )(*bufs, send, recv, *after)
    return list(res)


def _pair_exchange_start(part, name):
    rh = part.shape[1] // 2
    land_shape = (N_CHIPS, rh, part.shape[2])

    def body(part_ref, land_ref, send, recv, part_out, land_out, token):
        token[...] = jnp.zeros_like(token)
        x, y, c, _ = _place()
        pltpu.make_async_remote_copy(src_ref=part_out.at[:, pl.ds((1 - c) * rh, rh), :], dst_ref=land_out,
                                     send_sem=send, recv_sem=recv, device_id=(x, y, 1 - c), device_id_type=MESH).start()

    res = pl.pallas_call(
        body,
        name=name,
        in_specs=[HBM, HBM],
        out_specs=[SEM, SEM, HBM, HBM, VMEM_SPEC],
        out_shape=[pltpu.SemaphoreType.DMA(()), pltpu.SemaphoreType.DMA(()), pltpu.HBM(part.shape, part.dtype),
                   pltpu.HBM(land_shape, part.dtype), TOKEN],
        input_output_aliases={0: 2, 1: 3},
        compiler_params=pltpu.CompilerParams(has_side_effects=EFFECT),
    )(_in_hbm(part), _in_hbm(lax.empty(land_shape, part.dtype)))
    return res


def _pair_exchange_wait(send, recv, part, land, after, name):
    rh = part.shape[1] // 2

    def body(part_ref, land_ref, send_ref, recv_ref, after_ref, part_out, land_out):
        x, y, c, _ = _place()
        cp = pltpu.make_async_remote_copy(src_ref=part_ref.at[:, pl.ds((1 - c) * rh, rh), :], dst_ref=land_ref,
                                          send_sem=send_ref, recv_sem=recv_ref, device_id=(x, y, 1 - c), device_id_type=MESH)
        cp.wait_send()
        cp.wait_recv()

    return pl.pallas_call(
        body,
        name=name,
        in_specs=[HBM, HBM, SEM, SEM, ANY],
        out_specs=[HBM, HBM],
        out_shape=[pltpu.HBM(part.shape, part.dtype), pltpu.HBM(land.shape, land.dtype)],
        input_output_aliases={0: 0, 1: 1},
        compiler_params=pltpu.CompilerParams(has_side_effects=EFFECT),
    )(part, land, send, recv, after)


def _chip_exchange_start(sums, name):
    n = len(sums)

    def body(*refs):
        send, recv = refs[2 * n], refs[2 * n + 1]
        src = refs[2 * n + 2:3 * n + 2]
        dst = refs[3 * n + 2:4 * n + 2]
        refs[4 * n + 2][...] = jnp.zeros(TOKEN.shape, TOKEN.dtype)
        x, y, c, chips = _place()
        p = 2 * x + y
        for k in range(n):
            for j, (px, py) in enumerate(chips):
                pltpu.make_async_remote_copy(src_ref=src[k].at[2 * px + py], dst_ref=dst[k].at[p],
                                             send_sem=send.at[3 * k + j], recv_sem=recv.at[3 * k + j],
                                             device_id=(px, py, c), device_id_type=MESH).start()

    res = pl.pallas_call(
        body,
        name=name,
        in_specs=[HBM] * (2 * n),
        out_specs=[SEM, SEM] + [HBM] * (2 * n) + [VMEM_SPEC],
        out_shape=[pltpu.SemaphoreType.DMA((3 * n,))] * 2 + [pltpu.HBM(a.shape, a.dtype) for a in sums] * 2 + [TOKEN],
        input_output_aliases={k: 2 + k for k in range(2 * n)},
        compiler_params=pltpu.CompilerParams(has_side_effects=EFFECT),
    )(*[_in_hbm(a) for a in sums], *[_in_hbm(lax.empty(a.shape, a.dtype)) for a in sums])
    return res[0], res[1], list(res[2:2 + n]), list(res[2 + n:2 + 2 * n]), res[2 + 2 * n]


def _chip_exchange_wait(sums, slots, send, recv, after, name):
    n = len(sums)

    def body(*refs):
        src, dst = refs[:n], refs[n:2 * n]
        send_ref, recv_ref = refs[2 * n], refs[2 * n + 1]
        x, y, c, chips = _place()
        for k in range(n):
            for j, (px, py) in enumerate(chips):
                cp = pltpu.make_async_remote_copy(src_ref=src[k].at[2 * px + py], dst_ref=dst[k].at[2 * px + py],
                                                  send_sem=send_ref.at[3 * k + j], recv_sem=recv_ref.at[3 * k + j],
                                                  device_id=(px, py, c), device_id_type=MESH)
                cp.wait_send()
                cp.wait_recv()

    res = pl.pallas_call(
        body,
        name=name,
        in_specs=[HBM] * (2 * n) + [SEM, SEM] + [ANY] * len(after),
        out_specs=[HBM] * (2 * n),
        out_shape=[pltpu.HBM(a.shape, a.dtype) for a in sums] * 2,
        input_output_aliases={k: k for k in range(2 * n)},
        compiler_params=pltpu.CompilerParams(has_side_effects=EFFECT),
    )(*sums, *slots, send, recv, *after)
    return list(res[:n]), list(res[n:])


def _pair_exchange(parts, name):
    n = len(parts)

    def body(*refs):
        ins, outs = refs[:n], refs[n:2 * n]
        send, recv = refs[2 * n:]
        x, y, c, _ = _place()
        copies = []
        for k in range(n):
            rh = parts[k].shape[1] // 2
            cp = pltpu.make_async_remote_copy(
                src_ref=ins[k].at[:, pl.ds((1 - c) * rh, rh), :], dst_ref=outs[k],
                send_sem=send.at[k], recv_sem=recv.at[k], device_id=(x, y, 1 - c), device_id_type=MESH)
            cp.start()
            copies.append(cp)
        for cp in copies:
            cp.wait()

    return pl.pallas_call(
        body,
        name=name,
        in_specs=[ANY] * n,
        out_specs=[ANY] * n,
        out_shape=[jax.ShapeDtypeStruct((N_CHIPS, a.shape[1] // 2, a.shape[2]), a.dtype) for a in parts],
        scratch_shapes=[pltpu.SemaphoreType.DMA((n,)), pltpu.SemaphoreType.DMA((n,))],
        compiler_params=pltpu.CompilerParams(has_side_effects=True),
    )(*parts)


def _pair_sum(core, part, landed, name):
    _, r, cdim = part.shape
    rh = r // 2
    tr = _row_tile(rh, cdim, 2)
    nb = rh // tr

    def body(c_ref, a_ref, b_ref, o_ref):
        o_ref[...] = (a_ref[...].astype(F32) + b_ref[...].astype(F32)).astype(o_ref.dtype)

    return pl.pallas_call(
        body,
        name=name,
        grid_spec=pltpu.PrefetchScalarGridSpec(
            num_scalar_prefetch=1,
            grid=(N_CHIPS, nb),
            in_specs=[pl.BlockSpec((None, tr, cdim), lambda q, i, c_ref: (q, c_ref[0] * nb + i, 0)),
                      pl.BlockSpec((None, tr, cdim), lambda q, i, c_ref: (q, i, 0))],
            out_specs=pl.BlockSpec((None, tr, cdim), lambda q, i, c_ref: (q, i, 0))),
        out_shape=jax.ShapeDtypeStruct((N_CHIPS, rh, cdim), BF16),
        compiler_params=_params(("parallel", "parallel")),
    )(core, part, landed)


def _chip_sum(place, own, slots, name, after=()):
    _, rh, cdim = slots.shape
    tr = _row_tile(rh, cdim, 2)
    nb = rh // tr

    def body(place_ref, own_ref, s1_ref, s2_ref, s3_ref, *rest):
        o_ref = rest[len(after)]
        acc = own_ref[...].astype(F32)
        for s_ref in (s1_ref, s2_ref, s3_ref):
            acc = acc + s_ref[...].astype(F32)
        o_ref[...] = acc

    def other(j):
        return lambda i, place_ref: ((place_ref[0] + j) % N_CHIPS, i, 0)

    return pl.pallas_call(
        body,
        name=name,
        grid_spec=pltpu.PrefetchScalarGridSpec(
            num_scalar_prefetch=1,
            grid=(nb,),
            in_specs=[pl.BlockSpec((None, tr, cdim), other(j)) for j in (0, 1, 2, 3)] + [ANY] * len(after),
            out_specs=pl.BlockSpec((tr, cdim), lambda i, place_ref: (place_ref[1] * nb + i, 0))),
        out_shape=jax.ShapeDtypeStruct((2 * rh, cdim), F32),
        compiler_params=_params(("parallel",)),
    )(place, own, slots, slots, slots, *after)


def _half_exchange_start(buf, name):
    rh = buf.shape[0] // 2

    def body(buf_ref, send, recv, out_ref, token):
        token[...] = jnp.zeros_like(token)
        x, y, c, _ = _place()
        mine = out_ref.at[pl.ds(c * rh, rh), :]
        pltpu.make_async_remote_copy(src_ref=mine, dst_ref=mine, send_sem=send, recv_sem=recv,
                                     device_id=(x, y, 1 - c), device_id_type=MESH).start()

    return pl.pallas_call(
        body,
        name=name,
        in_specs=[HBM],
        out_specs=[SEM, SEM, HBM, VMEM_SPEC],
        out_shape=[pltpu.SemaphoreType.DMA(()), pltpu.SemaphoreType.DMA(()), pltpu.HBM(buf.shape, buf.dtype), TOKEN],
        input_output_aliases={0: 2},
        compiler_params=pltpu.CompilerParams(has_side_effects=EFFECT),
    )(_in_hbm(buf))


def _half_exchange_wait(send, recv, buf, after, name):
    rh = buf.shape[0] // 2

    def body(buf_ref, send_ref, recv_ref, *rest):
        x, y, c, _ = _place()
        cp = pltpu.make_async_remote_copy(src_ref=buf_ref.at[pl.ds(c * rh, rh), :], dst_ref=buf_ref.at[pl.ds((1 - c) * rh, rh), :],
                                          send_sem=send_ref, recv_sem=recv_ref, device_id=(x, y, 1 - c), device_id_type=MESH)
        cp.wait_send()
        cp.wait_recv()

    return pl.pallas_call(
        body,
        name=name,
        in_specs=[HBM, SEM, SEM] + [ANY] * len(after),
        out_specs=HBM,
        out_shape=pltpu.HBM(buf.shape, buf.dtype),
        input_output_aliases={0: 0},
        compiler_params=pltpu.CompilerParams(has_side_effects=EFFECT),
    )(buf, send, recv, *after)


def _half_exchange(bufs, name):
    n = len(bufs)

    def body(*refs):
        outs = refs[n:2 * n]
        send, recv = refs[2 * n:]
        x, y, c, _ = _place()
        copies = []
        for k in range(n):
            rh = bufs[k].shape[0] // 2
            mine = outs[k].at[pl.ds(c * rh, rh), :]
            cp = pltpu.make_async_remote_copy(src_ref=mine, dst_ref=mine, send_sem=send.at[k], recv_sem=recv.at[k],
                                              device_id=(x, y, 1 - c), device_id_type=MESH)
            cp.start()
            copies.append(cp)
        for k in range(n):
            rh = bufs[k].shape[0] // 2
            theirs = outs[k].at[pl.ds((1 - c) * rh, rh), :]
            copies[k].wait_send()
            pltpu.make_async_remote_copy(src_ref=theirs, dst_ref=theirs, send_sem=send.at[k], recv_sem=recv.at[k],
                                         device_id=(x, y, 1 - c), device_id_type=MESH).wait_recv()

    return pl.pallas_call(
        body,
        name=name,
        in_specs=[ANY] * n,
        out_specs=[ANY] * n,
        out_shape=[jax.ShapeDtypeStruct(a.shape, a.dtype) for a in bufs],
        input_output_aliases={k: k for k in range(n)},
        scratch_shapes=[pltpu.SemaphoreType.DMA((n,)), pltpu.SemaphoreType.DMA((n,))],
        compiler_params=pltpu.CompilerParams(has_side_effects=True),
    )(*bufs)


SMALL_ROWS = 32


def _small_peers():
    x, y, c, _ = _place()
    peers = []
    for k in range(1, N_DEV):
        px, py, pc = x ^ ((k >> 2) & 1), y ^ ((k >> 1) & 1), c ^ (k & 1)
        peers.append((k, (px, py, pc), 4 * px + 2 * py + pc))
    return 4 * x + 2 * y + c, peers


def _all_gather_small_start(gath):
    def body(in_ref, send, recv, out_ref, token):
        token[...] = jnp.zeros_like(token)
        me, peers = _small_peers()
        for k, peer, _ in peers:
            pltpu.make_async_remote_copy(src_ref=out_ref.at[me], dst_ref=out_ref.at[me], send_sem=send.at[k], recv_sem=recv.at[k],
                                         device_id=peer, device_id_type=MESH).start()

    return pl.pallas_call(
        body,
        name="small_gather_start",
        in_specs=[HBM],
        out_specs=[SEM, SEM, HBM, VMEM_SPEC],
        out_shape=[pltpu.SemaphoreType.DMA((N_DEV,)), pltpu.SemaphoreType.DMA((N_DEV,)), pltpu.HBM(gath.shape, gath.dtype), TOKEN],
        input_output_aliases={0: 2},
        compiler_params=pltpu.CompilerParams(has_side_effects=EFFECT),
    )(_in_hbm(gath))


def _all_gather_small_wait(send, recv, gath, after):
    def body(in_ref, send_ref, recv_ref, *rest):
        me, peers = _small_peers()
        for k, peer, peer_id in peers:
            cp = pltpu.make_async_remote_copy(src_ref=in_ref.at[me], dst_ref=in_ref.at[peer_id], send_sem=send_ref.at[k],
                                              recv_sem=recv_ref.at[k], device_id=peer, device_id_type=MESH)
            cp.wait_send()
            cp.wait_recv()

    return pl.pallas_call(
        body,
        name="small_gather_wait",
        in_specs=[HBM, SEM, SEM] + [ANY] * len(after),
        out_specs=HBM,
        out_shape=pltpu.HBM(gath.shape, gath.dtype),
        input_output_aliases={0: 0},
        compiler_params=pltpu.CompilerParams(has_side_effects=EFFECT),
    )(gath, send, recv, *after)


def _sum_small(gath):
    def body(g_ref, o_ref):
        acc = g_ref[0]
        for dev in range(1, N_DEV):
            acc = acc + g_ref[dev]
        o_ref[...] = acc

    return pl.pallas_call(
        body,
        name="small_sum",
        in_specs=[VMEM_SPEC],
        out_specs=VMEM_SPEC,
        out_shape=jax.ShapeDtypeStruct(gath.shape[1:], F32),
        compiler_params=pltpu.CompilerParams(vmem_limit_bytes=VMEM_LIMIT),
    )(gath)


def _adamw_update(g_ref, w_ref, m_ref, v_ref, go_ref, d_ref, mo_ref, vo_ref):
    bc1 = 1.0 - ADAM_B1 ** ADAM_STEP
    bc2 = 1.0 - ADAM_B2 ** ADAM_STEP
    gv = g_ref[...]
    mn = ADAM_B1 * m_ref[...] + (1.0 - ADAM_B1) * gv
    vn = ADAM_B2 * v_ref[...] + (1.0 - ADAM_B2) * (gv * gv)
    go_ref[...] = gv
    mo_ref[...] = mn
    vo_ref[...] = vn
    d_ref[...] = -ADAM_LR * ((mn / bc1) / (jnp.sqrt(vn / bc2) + ADAM_EPS) + ADAM_WD * w_ref[...])


def _adamw_small(gs, ws, ms, vs):
    n = len(gs)

    def body(*refs):
        ins, outs = refs[:4 * n], refs[4 * n:]
        for k in range(n):
            _adamw_update(*[ins[i * n + k] for i in range(4)], *outs[4 * k:4 * k + 4])

    vmem = pl.BlockSpec(memory_space=pltpu.VMEM)
    res = pl.pallas_call(
        body,
        name="adamw_small",
        in_specs=[vmem] * (4 * n),
        out_specs=[vmem] * (4 * n),
        out_shape=[jax.ShapeDtypeStruct(w.shape, F32) for w in ws for _ in range(4)],
        compiler_params=pltpu.CompilerParams(vmem_limit_bytes=VMEM_LIMIT),
    )(*gs, *ws, *ms, *vs)
    return [tuple(res[4 * k:4 * k + 4]) for k in range(n)]


def _adamw(g, w, m, v, name, g_block=0):
    r, cdim = w.shape
    tr = _row_tile(r, cdim, 4)

    def body(*refs):
        _adamw_update(*refs)

    spec = pl.BlockSpec((tr, cdim), lambda i: (i, 0))
    return pl.pallas_call(
        body,
        name=name,
        grid=(r // tr,),
        in_specs=[pl.BlockSpec((tr, cdim), lambda i: (i, g_block))] + [spec] * 3,
        out_specs=[spec] * 4,
        out_shape=[jax.ShapeDtypeStruct((r, cdim), F32)] * 4,
        compiler_params=_params(("parallel",)),
    )(g, w, m, v)


VEC_NAMES = ["pre_mix_norm", "q_norm", "kv_norm", "conv_b", "conv_ln_g", "conv_ln_b", "conv_out_norm",
             "attn_out_norm", "post_mix_norm", "pre_ffn_norm", "post_ffn_norm"]
LOSS_ROW = len(VEC_NAMES)
CONV_W_ROW = 16


def _cols_to_full(parts):
    _, r, cdim = parts.shape
    return parts.transpose(1, 0, 2).reshape(r, N_CHIPS * cdim)


def _full_to_cols(full):
    r, n = full.shape
    return full.reshape(r, N_CHIPS, n // N_CHIPS).transpose(1, 0, 2)


W_IN_SHARD = (2 * CONV_CH + Q_LORA + KV_LORA + QK_ROPE) // N_CHIPS
W_IN_PART = 1024
W_IN_BLOCKS = (2 * CONV_CH + Z2_COLS) // LANES
W_IN_BASE = [p * W_IN_SHARD // LANES for p in range(N_CHIPS)]
W_IN_SPAN = [-(-(p * W_IN_SHARD % LANES + W_IN_SHARD) // LANES) for p in range(N_CHIPS)]


def _w_in_block_home(b):
    n = CONV_CH // LANES
    if b < n:
        return 0, 2 * b
    if b < 2 * n:
        return 0, 2 * (b - n) + 1
    return 1, b - 2 * n


def _to_parts_w_in(shift_chip, w_in):
    r = w_in.shape[0]
    tr = 512

    def body(s_ref, a_ref, o_ref):
        o_ref[...] = jnp.zeros_like(o_ref)
        o_ref[:, :W_IN_SHARD] = a_ref[...].astype(o_ref.dtype)
        o_ref[...] = pltpu.roll(o_ref[...].astype(F32), s_ref[0], 1).astype(o_ref.dtype)

    return pl.pallas_call(
        body,
        name="to_parts_w_in",
        grid_spec=pltpu.PrefetchScalarGridSpec(
            num_scalar_prefetch=1,
            grid=(r // tr,),
            in_specs=[pl.BlockSpec((tr, W_IN_SHARD), lambda i, s_ref: (i, 0))],
            out_specs=pl.BlockSpec((None, tr, W_IN_PART), lambda i, s_ref: (s_ref[1], i, 0))),
        out_shape=jax.ShapeDtypeStruct((N_CHIPS, r, W_IN_PART), BF16),
        compiler_params=_params(("parallel",)),
    )(shift_chip, w_in)


def _assemble_w_in(parts):
    r = parts.shape[1]
    tr = ROW_TILE

    def body(p_ref, ag_ref, z2_ref):
        outs = (ag_ref, z2_ref)
        for b in range(W_IN_BLOCKS):
            blk = None
            for p in range(N_CHIPS):
                i = b - W_IN_BASE[p]
                if 0 <= i < W_IN_SPAN[p]:
                    piece = p_ref[p, :, i * LANES:(i + 1) * LANES]
                    blk = piece if blk is None else blk + piece
            which, at = _w_in_block_home(b)
            outs[which][:, at * LANES:(at + 1) * LANES] = blk

    w_ag, w_z2 = pl.pallas_call(
        body,
        name="assemble_w_in",
        grid=(r // tr,),
        in_specs=[pl.BlockSpec((N_CHIPS, tr, W_IN_PART), lambda i: (0, i, 0))],
        out_specs=[pl.BlockSpec((tr, 2 * CONV_CH), lambda i: (i, 0)), pl.BlockSpec((tr, Z2_COLS), lambda i: (i, 0))],
        out_shape=[jax.ShapeDtypeStruct((r, 2 * CONV_CH), parts.dtype), jax.ShapeDtypeStruct((r, Z2_COLS), parts.dtype)],
        compiler_params=_params(("parallel",)),
    )(parts)
    return dict(w_ag=w_ag, w_z2=w_z2)


def _w_in_grad_parts(dw_ag, dw_z2):
    r = dw_ag.shape[0]
    tr = ROW_TILE

    def body(ag_ref, z2_ref, o_ref):
        ins = (ag_ref, z2_ref)
        for p in range(N_CHIPS):
            for i in range(W_IN_PART // LANES):
                if i < W_IN_SPAN[p]:
                    which, at = _w_in_block_home(W_IN_BASE[p] + i)
                    o_ref[p, :, i * LANES:(i + 1) * LANES] = ins[which][:, at * LANES:(at + 1) * LANES]
                else:
                    o_ref[p, :, i * LANES:(i + 1) * LANES] = jnp.zeros((tr, LANES), o_ref.dtype)

    return pl.pallas_call(
        body,
        name="w_in_grad_parts",
        grid=(r // tr,),
        in_specs=[pl.BlockSpec((tr, 2 * CONV_CH), lambda i: (i, 0)), pl.BlockSpec((tr, Z2_COLS), lambda i: (i, 0))],
        out_specs=pl.BlockSpec((N_CHIPS, tr, W_IN_PART), lambda i: (0, i, 0)),
        out_shape=jax.ShapeDtypeStruct((N_CHIPS, r, W_IN_PART), dw_ag.dtype),
        compiler_params=_params(("parallel",)),
    )(dw_ag, dw_z2)


def _assemble_mixer_rest(g):
    uq = _cols_to_full(g["w_uq"]).reshape(Q_LORA, N_HEADS, QK_HEAD)
    w_uq = jnp.concatenate([uq[:, :, :QK_NOPE].reshape(Q_LORA, N_HEADS * QK_NOPE),
                            uq[:, :, QK_NOPE:].reshape(Q_LORA, N_HEADS * QK_ROPE)], axis=1)
    return dict(w_uq=w_uq, w_ukv=_cols_to_full(g["w_ukv"]), conv_w=_cols_to_full(g["conv_w"]),
                w_out=g["w_out"].reshape(-1, g["w_out"].shape[2]))


def _grads_to_parts(dw):
    uq = dw["w_uq"]
    d_uq = jnp.concatenate([uq[:, :N_HEADS * QK_NOPE].reshape(Q_LORA, N_HEADS, QK_NOPE),
                            uq[:, N_HEADS * QK_NOPE:].reshape(Q_LORA, N_HEADS, QK_ROPE)], axis=2).reshape(Q_LORA, N_HEADS * QK_HEAD)
    return dict(w_in=_w_in_grad_parts(dw["w_ag"], dw["w_z2"]), w_uq=_full_to_cols(d_uq), w_ukv=_full_to_cols(dw["w_ukv"]),
                w_out=dw["w_out"].reshape(N_CHIPS, -1, dw["w_out"].shape[1]))


MIXER = ["w_in", "w_uq", "w_ukv", "w_out"]
FFN = ["w_gu", "w_down"]
BIG = MIXER + FFN


def _pad_lanes(v, n):
    return jnp.pad(v, ((0, 0), (0, n - v.shape[1])))


def kernel(x, positions, pre_mix_norm, w_in, q_norm, w_uq, kv_norm, w_ukv, conv_w, conv_b, conv_ln_g, conv_ln_b, conv_out_norm, attn_out_norm, w_out, post_mix_norm, pre_ffn_norm, w_gate, w_up, w_down, post_ffn_norm, loss_target, m_pre_mix_norm, m_w_in, m_q_norm, m_w_uq, m_kv_norm, m_w_ukv, m_conv_w, m_conv_b, m_conv_ln_g, m_conv_ln_b, m_conv_out_norm, m_attn_out_norm, m_w_out, m_post_mix_norm, m_pre_ffn_norm, m_w_gate, m_w_up, m_w_down, m_post_ffn_norm, v_pre_mix_norm, v_w_in, v_q_norm, v_w_uq, v_kv_norm, v_w_ukv, v_conv_w, v_conv_b, v_conv_ln_g, v_conv_ln_b, v_conv_out_norm, v_attn_out_norm, v_w_out, v_post_mix_norm, v_pre_ffn_norm, v_w_gate, v_w_up, v_w_down, v_post_ffn_norm):
    given = dict(locals())
    names = ["pre_mix_norm", "w_in", "q_norm", "w_uq", "kv_norm", "w_ukv", "conv_w", "conv_b", "conv_ln_g", "conv_ln_b",
             "conv_out_norm", "attn_out_norm", "w_out", "post_mix_norm", "pre_ffn_norm", "w_gate", "w_up", "w_down", "post_ffn_norm"]
    def as_2d(a):
        return a if a.ndim == 2 else a[0]

    weights = {n: as_2d(given[n]) for n in names}
    mom = {n: as_2d(given["m_" + n]) for n in names}
    var = {n: as_2d(given["v_" + n]) for n in names}
    d = D_MODEL

    inv_freq = ROPE_THETA ** (-jnp.arange(0, QK_ROPE, 2, dtype=F32) / QK_ROPE)
    ang = positions[0].astype(F32)[:, None] * inv_freq
    cos, sin = jnp.cos(ang), jnp.sin(ang)
    cos2 = jnp.concatenate([cos, cos, cos, cos], axis=1)
    sin2 = jnp.concatenate([-sin, sin, -sin, sin], axis=1)

    chip = 2 * lax.axis_index("x") + lax.axis_index("y")
    core = lax.axis_index("c")
    chip1 = chip.astype(jnp.int32).reshape(1)
    pieces = {n: [weights[n]] for n in BIG if n != "w_gu"}
    pieces["w_gu"] = [weights["w_gate"], weights["w_up"]]
    core1 = core.astype(jnp.int32).reshape(1)
    place = jnp.stack([chip, core]).astype(jnp.int32)
    rest = ["w_uq", "w_ukv", "w_out"]
    w_in_shift = (chip * W_IN_SHARD) % LANES
    w_in_buf = _to_parts_w_in(jnp.stack([w_in_shift, chip]).astype(jnp.int32), weights["w_in"])
    (w_in_sems,), w_in_thru, _ = _gather_start([w_in_buf], [False], [[0]], "gather_start_w_in")
    rest_bufs = [_to_parts(chip1, pieces[n], BF16, "to_parts_" + n) for n in rest]
    rest_bufs.append(_to_parts(chip1, [jnp.pad(weights["conv_w"], ((0, CONV_K_PAD - CONV_K), (0, 0)))], F32, "to_parts_conv_w"))
    rest_whole = [False] * 3 + [True]
    ffn_bufs = [_to_parts(chip1, pieces[n], BF16, "to_parts_" + n) for n in FFN]
    got = _gather_wait(w_in_thru, [False], *w_in_sems, rest_bufs + ffn_bufs, "gather_wait_w_in")
    (rest_sems,), rest_thru, started = _gather_start(rest_bufs, rest_whole, [[0, 1, 2, 3]], "gather_start_mixer_rest", after=got)
    full = _assemble_w_in(_gather_hand_on(got, "gather_hand_on_w_in", after=[started])[0])
    vec = {n: weights[n] for n in VEC_NAMES}
    rs = {}

    def get_mixer_rest(after):
        got = _gather_wait(rest_thru, rest_whole, *rest_sems, [after], "gather_wait_mixer_rest")
        (rs["w_gu_sems"],), rs["w_gu_thru"], started = _gather_start(ffn_bufs[:1], [False], [[0]], "gather_start_w_gu", after=got[:1])
        got = list(_gather_hand_on(got[:3], "gather_hand_on_mixer_rest", after=[started])) + [got[3]]
        return _assemble_mixer_rest(dict(zip(rest + ["conv_w"], got)))

    def w_gu_landed(after):
        got = _gather_wait(rs["w_gu_thru"], [False], *rs["w_gu_sems"], [after], "gather_wait_w_gu")
        (rs["w_down_sems"],), rs["w_down_thru"], started = _gather_start(ffn_bufs[1:], [False], [[0]], "gather_start_w_down", after=got)
        rs["w_gu"] = _hand_on_start(got, "hand_on_start_w_gu", after=[started])
        return rs["w_gu"][3]

    def get_w_gu(after):
        send, recv, bufs, _ = rs["w_gu"]
        return _hand_on_wait(bufs, send, recv, [after], "hand_on_wait_w_gu")[0]

    def get_w_down(after):
        got = _gather_wait(rs["w_down_thru"], [False], *rs["w_down_sems"], [after], "gather_wait_w_down")
        got = _gather_hand_on(got, "gather_hand_on_w_down")[0]
        return got.reshape(-1, got.shape[2])

    def pair_start(key):
        def hook(dw):
            rs[key] = _pair_exchange_start(dw.reshape(N_CHIPS, -1, dw.shape[-1]), "grad_pair_start_" + key)
            return rs[key][4]
        return hook

    def reduce_start(group, plist, landed):
        sums = [_pair_sum(core1, a, b, "grad_pair_sum_%s_%d" % (group, k)) for k, (a, b) in enumerate(zip(plist, landed))]
        rs[group] = _chip_exchange_start(sums, "grad_chip_exchange_start_" + group)
        return rs[group][4]

    def reduce_finish(group, after):
        send, recv, sums, slots, _ = rs[group]
        sums, slots = _chip_exchange_wait(sums, slots, send, recv, after, "grad_chip_exchange_wait_" + group)
        halves = [_chip_sum(place, s, sl, "grad_chip_sum_%s_%d" % (group, k)) for k, (s, sl) in enumerate(zip(sums, slots))]
        return list(_half_exchange(halves, "grad_half_exchange_" + group))

    def ffn_grads_exchanged(after):
        pairs = [_pair_exchange_wait(*rs[key][:4], after, "grad_pair_wait_" + key) for key in ("dw_gu", "dw_down")]
        return reduce_start("ffn", [p[0] for p in pairs], [p[1] for p in pairs])

    hooks = dict(mixer_rest=get_mixer_rest, w_gu_landed=w_gu_landed, w_gu=get_w_gu, w_down=get_w_down, dw_down=pair_start("dw_down"), dw_gu=pair_start("dw_gu"),
                 grads_exchanged=ffn_grads_exchanged)
    loss, grad_x, dw, dvec = _local_step(x[0], loss_target[0], cos2, sin2, vec, full, hooks)

    rows = [_pad_lanes(dvec[n], d) for n in VEC_NAMES] + [_pad_lanes(loss, d)]
    rows.append(jnp.zeros((CONV_W_ROW - len(rows), d), F32))
    rows.append(dw["conv_w"].reshape(SMALL_ROWS - CONV_W_ROW, d))
    device1 = (2 * chip + core).astype(jnp.int32).reshape(1)
    small_gather = _all_gather_small_start(_to_parts(device1, [jnp.concatenate(rows, axis=0)], F32, "small_to_slot", n_parts=N_DEV))

    parts = _grads_to_parts(dw)
    plist = [parts[n] for n in MIXER]
    started = reduce_start("mixer", plist, _pair_exchange(plist, "grad_pair_exchange_mixer"))
    send, recv, sums, slots, _ = rs["ffn"]
    sums, slots = _chip_exchange_wait(sums, slots, send, recv, [started, small_gather[3]], "grad_chip_exchange_wait_ffn")
    down = _half_exchange_start(_chip_sum(place, sums[1], slots[1], "grad_chip_sum_ffn_1"), "grad_half_start_w_down")
    gu = _half_exchange_start(_chip_sum(place, sums[0], slots[0], "grad_chip_sum_ffn_0", after=[down[3]]), "grad_half_start_w_gu")
    res = {}
    g_down = _half_exchange_wait(*down[:3], [gu[3]], "grad_half_wait_w_down")
    res["w_down"] = _adamw(g_down, weights["w_down"], mom["w_down"], var["w_down"], "adamw_w_down")
    g_gu = _half_exchange_wait(*gu[:3], [res["w_down"][1]], "grad_half_wait_w_gu")
    for n, blk in (("w_gate", 0), ("w_up", 1)):
        res[n] = _adamw(g_gu, weights[n], mom[n], var[n], "adamw_" + n, g_block=blk)
    small = _sum_small(_all_gather_small_wait(*small_gather[:3], [res["w_up"][1]]))
    g_conv_w_full = small[CONV_W_ROW:].reshape(CONV_K_PAD, CONV_CH)
    g_small = {n: small[i:i + 1, :weights[n].shape[1]] for i, n in enumerate(VEC_NAMES)}
    g_small["conv_w"] = lax.dynamic_slice(g_conv_w_full, (0, chip * (CONV_CH // N_CHIPS)), (CONV_K_PAD, CONV_CH // N_CHIPS))[:CONV_K]
    loss_out = small[LOSS_ROW, 0]
    small_names = VEC_NAMES + ["conv_w"]
    res.update(zip(small_names, _adamw_small([g_small[n] for n in small_names], [weights[n] for n in small_names],
                                             [mom[n] for n in small_names], [var[n] for n in small_names])))
    done_meanwhile = [res["w_gate"][1], res["w_up"][1], res["w_down"][1], res["conv_w"][1], grad_x]
    g_mixer = reduce_finish("mixer", done_meanwhile)
    g_mixer[0] = lax.dynamic_slice(g_mixer[0], (0, w_in_shift), (g_mixer[0].shape[0], W_IN_SHARD))
    for n, g in zip(MIXER, g_mixer):
        res[n] = _adamw(g, weights[n], mom[n], var[n], "adamw_" + n)
    outs = [loss_out, grad_x[None]]
    for i in range(4):
        outs += [res[n][i].reshape(given[n].shape) for n in names]
    return tuple(outs)
```

```python
import functools

import jax
import jax.numpy as jnp
from jax import lax
from jax.experimental import pallas as pl
from jax.experimental.pallas import tpu as pltpu

F32 = jnp.float32
BF16 = jnp.bfloat16

D_MODEL = 2048
CONV_CH = 1024
CONV_K = 31
CONV_K_PAD = 32
N_HEADS = 8
QK_NOPE = 128
QK_ROPE = 64
V_HEAD = 128
QK_HEAD = QK_NOPE + QK_ROPE
Q_LORA = 768
KV_LORA = 512
ATTN_CH = N_HEADS * V_HEAD
Z2_COLS = Q_LORA + KV_LORA + 128
D_FF = 5632
ROPE_THETA = 10000.0
EPS = 1e-6
LANES = 128
N_CHIPS = 4
N_DEV = 8

ADAM_LR = 0.001
ADAM_B1 = 0.9
ADAM_B2 = 0.999
ADAM_EPS = 1e-08
ADAM_WD = 0.01
ADAM_STEP = 10

VMEM_LIMIT = 56 * 1024 * 1024
ROW_TILE = 256
MAX_TK = 2816
MESH = pl.DeviceIdType.MESH


def _params(sem=None):
    return pltpu.CompilerParams(dimension_semantics=sem, vmem_limit_bytes=VMEM_LIMIT)


def _first_divisor(n, cands):
    for c in cands:
        if n % c == 0:
            return c
    return n


STREAM_BLOCK_BYTES = 3 << 19


def _row_tile(rows, cols, itemsize):
    for tr in (1024, 704, 512, 384, 352, 256, 176, 128, 64, 32, 16):
        if rows % tr == 0 and tr * cols * itemsize <= STREAM_BLOCK_BYTES:
            return tr
    return rows


def _matmul(pairs, mode, out_dtype, name, b_parts=None, out_parts=False, tiles=(None, None, None), after=None):
    a0, b0 = pairs[0]
    part_c = b0.shape[2] if b_parts else None
    if mode == "nn":
        m, n = a0.shape[0], (N_CHIPS * part_c if b_parts else b0.shape[1])
        ks = [a.shape[1] for a, _ in pairs]
    elif mode == "nt":
        m, n = a0.shape[0], b0.shape[-2]
        ks = [a.shape[1] for a, _ in pairs]
    else:
        m, n = a0.shape[1], b0.shape[1]
        ks = [a.shape[0] for a, _ in pairs]
    tm = tiles[0] or _first_divisor(m, (1024, 768, 512, 256))
    tn = tiles[1] or (n if n <= 1536 else _first_divisor(n, (1024, 512, 256, 128)))
    tks = [tiles[2] or (k if k <= MAX_TK else MAX_TK) for k in ks]
    nks = [k // tk for k, tk in zip(ks, tks)]
    offs = [sum(nks[:p]) for p in range(len(pairs))]
    nk = sum(nks)
    n_pairs = len(pairs)
    assert not (b_parts or out_parts) or n_pairs == 1

    def kk(k, p):
        return jnp.clip(k - offs[p], 0, nks[p] - 1)

    in_specs = []
    for p in range(n_pairs):
        tk = tks[p]
        if mode == "nn":
            in_specs.append(pl.BlockSpec((tm, tk), lambda i, j, k, p=p: (i, kk(k, p))))
            if b_parts == "n":
                per = part_c // tn
                in_specs.append(pl.BlockSpec((None, tk, tn), lambda i, j, k: (j // per, k, j % per)))
            else:
                in_specs.append(pl.BlockSpec((tk, tn), lambda i, j, k, p=p: (kk(k, p), j)))
        elif mode == "nt":
            in_specs.append(pl.BlockSpec((tm, tk), lambda i, j, k, p=p: (i, kk(k, p))))
            if b_parts == "k":
                per = part_c // tk
                in_specs.append(pl.BlockSpec((None, tn, tk), lambda i, j, k: (k // per, j, k % per)))
            else:
                in_specs.append(pl.BlockSpec((tn, tk), lambda i, j, k, p=p: (j, kk(k, p))))
        else:
            in_specs.append(pl.BlockSpec((tk, tm), lambda i, j, k, p=p: (kk(k, p), i)))
            in_specs.append(pl.BlockSpec((tk, tn), lambda i, j, k, p=p: (kk(k, p), j)))
    if out_parts:
        out_per = (n // N_CHIPS) // tn
        out_spec = pl.BlockSpec((None, tm, tn), lambda i, j, k: (j // out_per, i, j % out_per))
        out_shape = jax.ShapeDtypeStruct((N_CHIPS, m, n // N_CHIPS), out_dtype)
    else:
        out_spec = pl.BlockSpec((tm, tn), lambda i, j, k: (i, j))
        out_shape = jax.ShapeDtypeStruct((m, n), out_dtype)
    dims = {"nn": (((1,), (0,)), ((), ())), "nt": (((1,), (1,)), ((), ())), "tn": (((0,), (0,)), ((), ()))}[mode]

    n_after = 0 if after is None else 1

    def body(*refs):
        o_ref = refs[2 * n_pairs + n_after]
        k = pl.program_id(2)

        def prod(p):
            return lax.dot_general(refs[2 * p][...], refs[2 * p + 1][...], dims, preferred_element_type=F32)

        if nk == 1:
            o_ref[...] = prod(0).astype(o_ref.dtype)
            return
        acc = refs[2 * n_pairs + n_after + 1]
        for p in range(n_pairs):
            first, last = offs[p], offs[p] + nks[p] - 1
            lo, hi = max(first, 1), min(last, nk - 2)
            if first == 0:
                @pl.when(k == 0)
                def _(p=p):
                    acc[...] = prod(p)

            if lo <= hi:
                @pl.when((k >= lo) & (k <= hi))
                def _(p=p):
                    acc[...] += prod(p)

            if last == nk - 1:
                @pl.when(k == nk - 1)
                def _(p=p):
                    o_ref[...] = (acc[...] + prod(p)).astype(o_ref.dtype)

    flat = [t for pr in pairs for t in pr] + ([] if after is None else [after])
    return pl.pallas_call(
        body,
        name=name,
        grid=(m // tm, n // tn, nk),
        in_specs=in_specs + [pl.BlockSpec(memory_space=pl.ANY)] * n_after,
        out_specs=out_spec,
        out_shape=out_shape,
        scratch_shapes=[pltpu.VMEM((tm, tn), F32)] if nk > 1 else [],
        compiler_params=_params(("parallel", "parallel", "arbitrary")),
    )(*flat)


F4 = D_FF // N_CHIPS
FFN_TM = 512
FFN_STRIP = 256


def _ffn_up(hf, w_gu):
    t, d = hf.shape

    def body(a_ref, b_ref, gu_ref, act_ref):
        for r in range(0, FFN_TM, FFN_STRIP):
            acc = jnp.dot(a_ref[r:r + FFN_STRIP, :], b_ref[...], preferred_element_type=F32)
            g = acc[:, :F4]
            gu_ref[r:r + FFN_STRIP, :] = acc.astype(gu_ref.dtype)
            act_ref[r:r + FFN_STRIP, :] = (g * _sigmoid(g) * acc[:, F4:]).astype(act_ref.dtype)

    return pl.pallas_call(
        body,
        name="ffn_up",
        grid=(N_CHIPS, t // FFN_TM),
        in_specs=[pl.BlockSpec((FFN_TM, d), lambda q, i: (i, 0)),
                  pl.BlockSpec((None, d, 2 * F4), lambda q, i: (q, 0, 0))],
        out_specs=[pl.BlockSpec((FFN_TM, 2 * F4), lambda q, i: (i, q)),
                   pl.BlockSpec((FFN_TM, F4), lambda q, i: (i, q))],
        out_shape=[jax.ShapeDtypeStruct((t, 2 * D_FF), BF16), jax.ShapeDtypeStruct((t, D_FF), BF16)],
        compiler_params=_params(("parallel", "parallel")),
    )(hf, w_gu)


def _ffn_down_dx(d_ff, w_down, gu):
    t, d = d_ff.shape

    def body(a_ref, b_ref, gu_ref, o_ref):
        for r in range(0, FFN_TM, FFN_STRIP):
            rows = slice(r, r + FFN_STRIP)
            d_act = lax.dot_general(a_ref[rows, :], b_ref[...], NT_DIMS, preferred_element_type=F32)
            g = gu_ref[rows, :F4].astype(F32)
            u = gu_ref[rows, F4:].astype(F32)
            sg = _sigmoid(g)
            dsg = d_act * sg
            o_ref[rows, :F4] = (dsg * u * (1.0 + g - g * sg)).astype(o_ref.dtype)
            o_ref[rows, F4:] = (dsg * g).astype(o_ref.dtype)

    return pl.pallas_call(
        body,
        name="ffn_down_dx",
        grid=(N_CHIPS, t // FFN_TM),
        in_specs=[pl.BlockSpec((FFN_TM, d), lambda q, i: (i, 0)),
                  pl.BlockSpec((F4, d), lambda q, i: (q, 0)),
                  pl.BlockSpec((FFN_TM, 2 * F4), lambda q, i: (i, q))],
        out_specs=pl.BlockSpec((FFN_TM, 2 * F4), lambda q, i: (i, q)),
        out_shape=jax.ShapeDtypeStruct((t, 2 * D_FF), BF16),
        compiler_params=_params(("parallel", "parallel")),
    )(d_ff, w_down, gu)


def _rowwise(fn, row_ins, vec_ins, row_outs, acc_outs, name, after=None):
    t = row_ins[0].shape[0]
    tm = ROW_TILE
    n_in = len(row_ins) + len(vec_ins)
    n_row = len(row_outs)
    extra = [] if after is None else [after]

    def body(*refs):
        ins = [r[...] for r in refs[:n_in]]
        outs = refs[n_in + len(extra):]
        vals = fn(*ins)
        for r, v in zip(outs[:n_row], vals[:n_row]):
            r[...] = v.astype(r.dtype)
        if acc_outs:
            @pl.when(pl.program_id(0) == 0)
            def _():
                for r in outs[n_row:]:
                    r[...] = jnp.zeros_like(r)

            for r, v in zip(outs[n_row:], vals[n_row:]):
                r[...] += v

    in_specs = [pl.BlockSpec((tm, a.shape[1]), lambda i: (i, 0)) for a in row_ins]
    in_specs += [pl.BlockSpec(a.shape, lambda i: (0, 0)) for a in vec_ins]
    out_specs = [pl.BlockSpec((tm, c), lambda i: (i, 0)) for c, _ in row_outs]
    out_specs += [pl.BlockSpec((1, c), lambda i: (0, 0)) for c in acc_outs]
    out_shape = [jax.ShapeDtypeStruct((t, c), dt) for c, dt in row_outs]
    out_shape += [jax.ShapeDtypeStruct((1, c), F32) for c in acc_outs]
    return pl.pallas_call(
        body,
        name=name,
        grid=(t // tm,),
        in_specs=in_specs + [pl.BlockSpec(memory_space=pl.ANY)] * len(extra),
        out_specs=out_specs,
        out_shape=out_shape,
        compiler_params=_params(("arbitrary",)),
    )(*row_ins, *vec_ins, *extra)


def _mean(v):
    return jnp.mean(v, axis=-1, keepdims=True)


def _colsum(v):
    return jnp.sum(v, axis=0, keepdims=True)


def _rms_fwd(v, g):
    r = lax.rsqrt(_mean(v * v) + EPS)
    vhat = v * r
    return vhat * g, vhat, r


def _rms_bwd(dn, vhat, r, g):
    dng = dn * g
    return r * (dng - vhat * _mean(dng * vhat)), _colsum(dn * vhat)


def _swap_rope_halves(v):
    n = v.shape[-1]
    lane = lax.broadcasted_iota(jnp.int32, v.shape, v.ndim - 1)
    return jnp.where(lane % QK_ROPE < QK_ROPE // 2, pltpu.roll(v, n - QK_ROPE // 2, v.ndim - 1),
                     pltpu.roll(v, QK_ROPE // 2, v.ndim - 1))


def _rope(v, cos2, sin2):
    return v * cos2 + _swap_rope_halves(v) * sin2


def _rope_transposed(dv, cos2, sin2):
    return dv * cos2 + _swap_rope_halves(dv * sin2)


def _sigmoid(v):
    return 1.0 / (1.0 + jnp.exp(-v))


CONV_ROWS = 256


def _conv_fwd(ag, conv_w, conv_b):
    t = ag.shape[0]
    cb = LANES

    def body(ag_ref, w_ref, b_ref, o_ref, scr):
        a = ag_ref[:, :cb].astype(F32)
        g = ag_ref[:, cb:].astype(F32)
        scr[pl.ds(0, CONV_K_PAD), :] = jnp.zeros((CONV_K_PAD, cb), F32)
        scr[pl.ds(CONV_K_PAD, t), :] = a * _sigmoid(g)
        for r0 in range(0, t, CONV_ROWS):
            acc = jnp.zeros((CONV_ROWS, cb), F32) + b_ref[...]
            for k in range(CONV_K):
                acc = acc + w_ref[k:k + 1, :] * scr[pl.ds(r0 + CONV_K_PAD - (CONV_K - 1) + k, CONV_ROWS), :]
            o_ref[pl.ds(r0, CONV_ROWS), :] = acc

    return pl.pallas_call(
        body,
        name="conv_fwd",
        grid=(CONV_CH // cb,),
        in_specs=[pl.BlockSpec((t, 2 * cb), lambda j: (0, j)),
                  pl.BlockSpec((CONV_K_PAD, cb), lambda j: (0, j)),
                  pl.BlockSpec((1, cb), lambda j: (0, j))],
        out_specs=pl.BlockSpec((t, cb), lambda j: (0, j)),
        out_shape=jax.ShapeDtypeStruct((t, CONV_CH), F32),
        scratch_shapes=[pltpu.VMEM((t + CONV_K_PAD, cb), F32)],
        compiler_params=_params(("parallel",)),
    )(ag, conv_w, conv_b)


def _conv_bwd(d_u1, ag, conv_w):
    t = ag.shape[0]
    cb = LANES

    def body(du_ref, ag_ref, w_ref, dag_ref, dw_ref, db_ref, su, sd):
        a = ag_ref[:, :cb].astype(F32)
        g = ag_ref[:, cb:].astype(F32)
        sg = _sigmoid(g)
        su[pl.ds(0, CONV_K_PAD), :] = jnp.zeros((CONV_K_PAD, cb), F32)
        su[pl.ds(CONV_K_PAD, t), :] = a * sg
        sd[pl.ds(0, t), :] = du_ref[...]
        sd[pl.ds(t, CONV_K_PAD), :] = jnp.zeros((CONV_K_PAD, cb), F32)
        db_ref[...] = _colsum(du_ref[...])
        dw_ref[...] = jnp.zeros_like(dw_ref)
        for r0 in range(0, t, CONV_ROWS):
            du = sd[pl.ds(r0, CONV_ROWS), :]
            acc = jnp.zeros((CONV_ROWS, cb), F32)
            for k in range(CONV_K):
                acc = acc + w_ref[k:k + 1, :] * sd[pl.ds(r0 + (CONV_K - 1) - k, CONV_ROWS), :]
                dw_ref[k:k + 1, :] += _colsum(du * su[pl.ds(r0 + CONV_K_PAD - (CONV_K - 1) + k, CONV_ROWS), :])
            sgc = sg[r0:r0 + CONV_ROWS]
            ac = a[r0:r0 + CONV_ROWS]
            dag_ref[pl.ds(r0, CONV_ROWS), :cb] = (acc * sgc).astype(dag_ref.dtype)
            dag_ref[pl.ds(r0, CONV_ROWS), cb:] = (acc * ac * sgc * (1.0 - sgc)).astype(dag_ref.dtype)

    return pl.pallas_call(
        body,
        name="conv_bwd",
        grid=(CONV_CH // cb,),
        in_specs=[pl.BlockSpec((t, cb), lambda j: (0, j)),
                  pl.BlockSpec((t, 2 * cb), lambda j: (0, j)),
                  pl.BlockSpec((CONV_K_PAD, cb), lambda j: (0, j))],
        out_specs=[pl.BlockSpec((t, 2 * cb), lambda j: (0, j)),
                   pl.BlockSpec((CONV_K_PAD, cb), lambda j: (0, j)),
                   pl.BlockSpec((1, cb), lambda j: (0, j))],
        out_shape=[jax.ShapeDtypeStruct((t, 2 * CONV_CH), BF16),
                   jax.ShapeDtypeStruct((CONV_K_PAD, CONV_CH), F32),
                   jax.ShapeDtypeStruct((1, CONV_CH), F32)],
        scratch_shapes=[pltpu.VMEM((t + CONV_K_PAD, cb), F32), pltpu.VMEM((t + CONV_K_PAD, cb), F32)],
        compiler_params=_params(("parallel",)),
    )(d_u1, ag, conv_w)


ATT_TQ = 256
NEG = float(jnp.finfo(jnp.float32).min)
SCALE = QK_HEAD ** -0.5
NT_DIMS = (((1,), (1,)), ((), ()))
TN_DIMS = (((0,), (0,)), ((), ()))


def _att_weights(qf, kf, row0):
    s = lax.dot_general(qf, kf, NT_DIMS, preferred_element_type=F32)
    tq, t = s.shape
    qpos = row0 + lax.broadcasted_iota(jnp.int32, (tq, t), 0)
    kpos = lax.broadcasted_iota(jnp.int32, (tq, t), 1)
    s = jnp.where(kpos <= qpos, s, NEG)
    p = jnp.exp(s - jnp.max(s, axis=-1, keepdims=True))
    return p, 1.0 / jnp.sum(p, axis=-1, keepdims=True)


def _scaled_query(qn, roped_half):
    return jnp.concatenate([(qn.astype(F32) * SCALE).astype(BF16), (roped_half * SCALE).astype(BF16)], axis=1)


def _half_mask(shape, which):
    lane = lax.broadcasted_iota(jnp.int32, shape, len(shape) - 1)
    return (lane // QK_ROPE == which).astype(F32)


def _attention_fwd(q, kv, kpe2, cos2, sin2):
    t = q.shape[0]
    tq = ATT_TQ

    def body(qn_ref, qp_ref, c_ref, s_ref, kv_ref, kpe_ref, o_ref):
        roped = _rope(qp_ref[...].astype(F32), c_ref[...], s_ref[...])

        def block(i):
            keys = slice(0, (i + 1) * tq)
            for e in range(2):
                qf = _scaled_query(qn_ref[:, e * QK_NOPE:(e + 1) * QK_NOPE], roped * _half_mask(roped.shape, e))
                kf = jnp.concatenate([kv_ref[keys, e * 256:e * 256 + QK_NOPE], kpe_ref[keys, :]], axis=1)
                p, inv_l = _att_weights(qf, kf, i * tq)
                v = kv_ref[keys, e * 256 + QK_NOPE:(e + 1) * 256]
                o = jnp.dot(p.astype(BF16), v, preferred_element_type=F32) * inv_l
                o_ref[:, e * V_HEAD:(e + 1) * V_HEAD] = o.astype(o_ref.dtype)

        for i in range(t // tq):
            pl.when(pl.program_id(1) == i)(functools.partial(block, i))

    return pl.pallas_call(
        body,
        name="attention_fwd",
        grid=(N_HEADS // 2, t // tq),
        in_specs=[pl.BlockSpec((tq, 2 * QK_NOPE), lambda h, i: (i, h)),
                  pl.BlockSpec((tq, LANES), lambda h, i: (i, N_HEADS + h)),
                  pl.BlockSpec((tq, LANES), lambda h, i: (i, 0)),
                  pl.BlockSpec((tq, LANES), lambda h, i: (i, 0)),
                  pl.BlockSpec((t, 512), lambda h, i: (0, h)),
                  pl.BlockSpec((t, LANES), lambda h, i: (0, 0))],
        out_specs=pl.BlockSpec((tq, 2 * V_HEAD), lambda h, i: (i, h)),
        out_shape=jax.ShapeDtypeStruct((t, ATTN_CH), BF16),
        compiler_params=_params(("parallel", "parallel")),
    )(q, q, cos2, sin2, kv, kpe2)


def _attention_bwd(q, kv, kpe2, cos2, sin2, d_attn):
    t = q.shape[0]
    tq = ATT_TQ
    n_q = t // tq

    def body(qn_ref, qp_ref, c_ref, s_ref, kv_ref, kpe_ref, do_ref, dqn_ref, dqp_ref, dkv_ref, dkpe_ref, dkv_acc):
        h, i = pl.program_id(0), pl.program_id(1)

        @pl.when(i == 0)
        def _():
            dkv_acc[...] = jnp.zeros_like(dkv_acc)

        @pl.when((i == 0) & (h == 0))
        def _():
            dkpe_ref[...] = jnp.zeros_like(dkpe_ref)

        roped = _rope(qp_ref[...].astype(F32), c_ref[...], s_ref[...])

        def block(ib):
            keys = slice(0, (ib + 1) * tq)
            d_roped = jnp.zeros((tq, LANES), F32)
            for e in range(2):
                mask = _half_mask(roped.shape, e)
                qf = _scaled_query(qn_ref[:, e * QK_NOPE:(e + 1) * QK_NOPE], roped * mask)
                kf = jnp.concatenate([kv_ref[keys, e * 256:e * 256 + QK_NOPE], kpe_ref[keys, :]], axis=1)
                v = kv_ref[keys, e * 256 + QK_NOPE:(e + 1) * 256]
                do = do_ref[:, e * V_HEAD:(e + 1) * V_HEAD]
                p, inv_l = _att_weights(qf, kf, ib * tq)
                p = p * inv_l
                dp = lax.dot_general(do, v, NT_DIMS, preferred_element_type=F32)
                ds = (p * (dp - jnp.sum(p * dp, axis=-1, keepdims=True))).astype(BF16)
                dqf = jnp.dot(ds, kf, preferred_element_type=F32) * SCALE
                dkf = lax.dot_general(ds, qf, TN_DIMS, preferred_element_type=F32)
                dv = lax.dot_general(p.astype(BF16), do, TN_DIMS, preferred_element_type=F32)
                dqn_ref[:, e * QK_NOPE:(e + 1) * QK_NOPE] = dqf[:, :QK_NOPE].astype(dqn_ref.dtype)
                d_roped = d_roped + dqf[:, QK_NOPE:] * mask
                dkv_acc[keys, e * 256:e * 256 + QK_NOPE] += dkf[:, :QK_NOPE]
                dkv_acc[keys, e * 256 + QK_NOPE:(e + 1) * 256] += dv
                dkpe_ref[keys, :] += dkf[:, QK_NOPE:]
            dqp_ref[...] = _rope_transposed(d_roped, c_ref[...], s_ref[...]).astype(dqp_ref.dtype)

        for ib in range(n_q):
            pl.when(i == ib)(functools.partial(block, ib))

        @pl.when(i == n_q - 1)
        def _():
            dkv_ref[...] = dkv_acc[...].astype(dkv_ref.dtype)

    return pl.pallas_call(
        body,
        name="attention_bwd",
        grid=(N_HEADS // 2, n_q),
        in_specs=[pl.BlockSpec((tq, 2 * QK_NOPE), lambda h, i: (i, h)),
                  pl.BlockSpec((tq, LANES), lambda h, i: (i, N_HEADS + h)),
                  pl.BlockSpec((tq, LANES), lambda h, i: (i, 0)),
                  pl.BlockSpec((tq, LANES), lambda h, i: (i, 0)),
                  pl.BlockSpec((t, 512), lambda h, i: (0, h)),
                  pl.BlockSpec((t, LANES), lambda h, i: (0, 0)),
                  pl.BlockSpec((tq, 2 * V_HEAD), lambda h, i: (i, h))],
        out_specs=[pl.BlockSpec((tq, 2 * QK_NOPE), lambda h, i: (i, h)),
                   pl.BlockSpec((tq, LANES), lambda h, i: (i, h)),
                   pl.BlockSpec((t, 512), lambda h, i: (0, h)),
                   pl.BlockSpec((t, LANES), lambda h, i: (0, 0))],
        out_shape=[jax.ShapeDtypeStruct((t, N_HEADS * QK_NOPE), BF16),
                   jax.ShapeDtypeStruct((t, N_HEADS * QK_ROPE), BF16),
                   jax.ShapeDtypeStruct((t, N_HEADS * 256), BF16),
                   jax.ShapeDtypeStruct((t, LANES), F32)],
        scratch_shapes=[pltpu.VMEM((t, 512), F32)],
        compiler_params=_params(("arbitrary", "arbitrary")),
    )(q, q, cos2, sin2, kv, kpe2, d_attn)


def _local_step(x, target, cos2, sin2, vec, w, ffn):
    d = D_MODEL

    (h,) = _rowwise(lambda xv, g: (_rms_fwd(xv, g)[0],), [x], [vec["pre_mix_norm"]], [(d, BF16)], [], "pre_mix_norm_fwd")
    ag = _matmul([(h, w["w_ag"])], "nn", BF16, "in_proj_ag")
    z2 = _matmul([(h, w["w_z2"])], "nn", BF16, "in_proj_z2")
    w = {**w, **ffn["mixer_rest"](z2)}
    u1 = _conv_fwd(ag, w["conv_w"], vec["conv_b"])

    def latents_fwd(z, c2, s2, qg, kvg):
        z = z.astype(F32)
        qn = _rms_fwd(z[:, :Q_LORA], qg)[0]
        kvn = _rms_fwd(z[:, Q_LORA:Q_LORA + KV_LORA], kvg)[0]
        kr = z[:, Q_LORA + KV_LORA:]
        kr2 = kr + pltpu.roll(kr, QK_ROPE, 1)
        return qn, kvn, _rope(kr2, c2, s2)

    qn, kvn, kpe2 = _rowwise(latents_fwd, [z2, cos2, sin2], [vec["q_norm"], vec["kv_norm"]],
                             [(Q_LORA, BF16), (KV_LORA, BF16), (LANES, BF16)], [], "latents_fwd")
    q = _matmul([(qn, w["w_uq"])], "nn", BF16, "q_up")
    kv = _matmul([(kvn, w["w_ukv"])], "nn", BF16, "kv_up")
    attn = _attention_fwd(q, kv, kpe2, cos2, sin2)

    def conv_post(u, lg, lb):
        mu = _mean(u)
        uc = u - mu
        rstd = lax.rsqrt(_mean(uc * uc) + EPS)
        uhat = uc * rstd
        u2 = uhat * lg + lb
        sg = _sigmoid(u2)
        return uhat, rstd, u2, sg, u2 * sg

    def mix_in_fwd(u, at, lg, lb, cg, ag_):
        u3 = conv_post(u, lg, lb)[4]
        cn = _rms_fwd(u3, cg)[0]
        an = _rms_fwd(at.astype(F32), ag_)[0]
        return (jnp.concatenate([cn, an], axis=1),)

    (cat,) = _rowwise(mix_in_fwd, [u1, attn], [vec["conv_ln_g"], vec["conv_ln_b"], vec["conv_out_norm"], vec["attn_out_norm"]],
                      [(2 * CONV_CH, BF16)], [], "mix_in_fwd")
    mix = _matmul([(cat, w["w_out"])], "nn", F32, "out_proj")
    landed = ffn["w_gu_landed"](mix)

    def residual1(xv, mv, gpm, gpf):
        x1 = xv + _rms_fwd(mv, gpm)[0]
        return x1, _rms_fwd(x1, gpf)[0]

    x1, hf = _rowwise(residual1, [x, mix], [vec["post_mix_norm"], vec["pre_ffn_norm"]], [(d, F32), (d, BF16)], [],
                      "residual1_fwd", after=landed)
    w_gu = ffn["w_gu"](hf)
    gu, act = _ffn_up(hf, w_gu)
    w_down = ffn["w_down"](act)
    ff = _matmul([(act, w_down)], "nn", F32, "ffn_down")

    def loss_head(x1v, ffv, tg, g):
        n, fhat, r = _rms_fwd(ffv, g)
        err = x1v + n - tg
        loss = 0.5 * jnp.sum(_mean(err * err), axis=0, keepdims=True)
        dy = err * (1.0 / d)
        d_ff, dg = _rms_bwd(dy, fhat, r, g)
        return dy, d_ff, dg, jnp.broadcast_to(loss, (1, LANES))

    dy, d_ff, g_post_ffn, loss = _rowwise(loss_head, [x1, ff, target], [vec["post_ffn_norm"]],
                                          [(d, F32), (d, BF16)], [d, LANES], "loss_head")
    d_gu = _ffn_down_dx(d_ff, w_down, gu)
    dw_down = _matmul([(act, d_ff)], "tn", BF16, "ffn_down_dw", tiles=(F4, None, None))
    started = ffn["dw_down"](dw_down)
    dw_gu = _matmul([(hf, d_gu)], "tn", BF16, "ffn_gate_up_dw", out_parts=True, tiles=(None, F4, None), after=started)
    started = ffn["dw_gu"](dw_gu)
    d_hf = _matmul([(d_gu, w_gu)], "nt", F32, "ffn_gate_up_dx", b_parts="k", after=started)
    started = ffn["grads_exchanged"](d_hf)

    def residual1_bwd(dyv, dhf, x1v, mv, gpf, gpm):
        _, x1hat, r1 = _rms_fwd(x1v, gpf)
        dn, dgpf = _rms_bwd(dhf, x1hat, r1, gpf)
        d_x1 = dyv + dn
        _, mhat, rm = _rms_fwd(mv, gpm)
        d_mix, dgpm = _rms_bwd(d_x1, mhat, rm, gpm)
        return d_x1, d_mix, dgpf, dgpm

    d_x1, d_mix, g_pre_ffn, g_post_mix = _rowwise(residual1_bwd, [dy, d_hf, x1, mix], [vec["pre_ffn_norm"], vec["post_mix_norm"]],
                                                  [(d, F32), (d, BF16)], [d, d], "residual1_bwd", after=started)
    d_cat = _matmul([(d_mix, w["w_out"])], "nt", BF16, "out_proj_dx")
    dw_out = _matmul([(cat, d_mix)], "tn", BF16, "out_proj_dw")

    def mix_in_bwd(dc, u, at, lg, lb, cg, ag_):
        dc = dc.astype(F32)
        uhat, rstd, u2, sg, u3 = conv_post(u, lg, lb)
        _, u3hat, rc = _rms_fwd(u3, cg)
        d_u3, dcg = _rms_bwd(dc[:, :CONV_CH], u3hat, rc, cg)
        d_u2 = d_u3 * sg * (1.0 + u2 * (1.0 - sg))
        dgl = d_u2 * lg
        d_u1 = rstd * (dgl - _mean(dgl) - uhat * _mean(dgl * uhat))
        _, ahat, ra = _rms_fwd(at.astype(F32), ag_)
        d_at, dag = _rms_bwd(dc[:, CONV_CH:], ahat, ra, ag_)
        return d_u1, d_at, dcg, _colsum(d_u2 * uhat), _colsum(d_u2), dag

    d_u1, d_attn, g_conv_out, g_ln_g, g_ln_b, g_attn_out = _rowwise(
        mix_in_bwd, [d_cat, u1, attn], [vec["conv_ln_g"], vec["conv_ln_b"], vec["conv_out_norm"], vec["attn_out_norm"]],
        [(CONV_CH, F32), (ATTN_CH, BF16)], [CONV_CH] * 4, "mix_in_bwd")
    d_ag, d_conv_w, g_conv_b = _conv_bwd(d_u1, ag, w["conv_w"])
    d_qn_, d_qp_, d_kv, d_kpe2 = _attention_bwd(q, kv, kpe2, cos2, sin2, d_attn)
    d_q = jnp.concatenate([d_qn_, d_qp_], axis=1)
    d_qn = _matmul([(d_q, w["w_uq"])], "nt", BF16, "q_up_dx")
    dw_uq = _matmul([(qn, d_q)], "tn", BF16, "q_up_dw")
    d_kvn = _matmul([(d_kv, w["w_ukv"])], "nt", BF16, "kv_up_dx")
    dw_ukv = _matmul([(kvn, d_kv)], "tn", BF16, "kv_up_dw")

    def latents_bwd(z, dq, dk, dkp, c2, s2, qg, kvg):
        z = z.astype(F32)
        _, qhat, rq = _rms_fwd(z[:, :Q_LORA], qg)
        d_ql, dqg = _rms_bwd(dq.astype(F32), qhat, rq, qg)
        _, khat, rk = _rms_fwd(z[:, Q_LORA:Q_LORA + KV_LORA], kvg)
        d_kl, dkg = _rms_bwd(dk.astype(F32), khat, rk, kvg)
        both = dkp + pltpu.roll(dkp, QK_ROPE, 1)
        d_kr = _rope_transposed(both, c2, s2) * _half_mask(both.shape, 0)
        return jnp.concatenate([d_ql, d_kl, d_kr], axis=1), dqg, dkg

    d_z2, g_q_norm, g_kv_norm = _rowwise(latents_bwd, [z2, d_qn, d_kvn, d_kpe2, cos2, sin2], [vec["q_norm"], vec["kv_norm"]],
                                         [(Z2_COLS, BF16)], [Q_LORA, KV_LORA], "latents_bwd")
    d_h = _matmul([(d_ag, w["w_ag"]), (d_z2, w["w_z2"])], "nt", F32, "in_proj_dx")
    dw_ag = _matmul([(h, d_ag)], "tn", BF16, "in_proj_ag_dw")
    dw_z2 = _matmul([(h, d_z2)], "tn", BF16, "in_proj_z2_dw")

    def pre_mix_bwd(dx1, dh, xv, g):
        _, xhat, r = _rms_fwd(xv, g)
        dn, dg = _rms_bwd(dh, xhat, r, g)
        return dx1 + dn, dg

    grad_x, g_pre_mix = _rowwise(pre_mix_bwd, [d_x1, d_h, x], [vec["pre_mix_norm"]], [(d, F32)], [d], "pre_mix_norm_bwd")

    dw = dict(w_ag=dw_ag, w_z2=dw_z2, w_uq=dw_uq, w_ukv=dw_ukv, conv_w=d_conv_w, w_out=dw_out, w_gu=dw_gu, w_down=dw_down)
    dvec = dict(pre_mix_norm=g_pre_mix, q_norm=g_q_norm, kv_norm=g_kv_norm, conv_b=g_conv_b, conv_ln_g=g_ln_g,
                conv_ln_b=g_ln_b, conv_out_norm=g_conv_out, attn_out_norm=g_attn_out, post_mix_norm=g_post_mix,
                pre_ffn_norm=g_pre_ffn, post_ffn_norm=g_post_ffn)
    return loss, grad_x, dw, dvec


ANY = pl.BlockSpec(memory_space=pl.ANY)


def _place():
    x, y, c = lax.axis_index("x"), lax.axis_index("y"), lax.axis_index("c")
    chips = [(1 - x, y), (x, 1 - y), (1 - x, 1 - y)]
    return x, y, c, chips


def _to_parts(chip, pieces, dtype, name, n_parts=N_CHIPS):
    r = pieces[0].shape[0]
    widths = [a.shape[1] for a in pieces]
    tr = r if r <= 512 else _first_divisor(r, (512, 256, 128))

    def body(p_ref, *refs):
        o_ref = refs[len(pieces)]
        off = 0
        for a_ref, wdt in zip(refs, widths):
            o_ref[:, off:off + wdt] = a_ref[...].astype(o_ref.dtype)
            off += wdt

    return pl.pallas_call(
        body,
        name=name,
        grid_spec=pltpu.PrefetchScalarGridSpec(
            num_scalar_prefetch=1,
            grid=(r // tr,),
            in_specs=[pl.BlockSpec((tr, wdt), lambda i, p_ref: (i, 0)) for wdt in widths],
            out_specs=pl.BlockSpec((None, tr, sum(widths)), lambda i, p_ref: (p_ref[0], i, 0))),
        out_shape=jax.ShapeDtypeStruct((n_parts, r, sum(widths)), dtype),
        compiler_params=_params(("parallel",)),
    )(chip, *pieces)


HBM = pl.BlockSpec(memory_space=pltpu.HBM)
SEM = pl.BlockSpec(memory_space=pltpu.SEMAPHORE)
EFFECT = pltpu.SideEffectType.DATAFLOW_SIDE_EFFECTING
VMEM_SPEC = pl.BlockSpec(memory_space=pltpu.VMEM)
TOKEN = jax.ShapeDtypeStruct((8, LANES), F32)


def _in_hbm(a):
    return pltpu.with_memory_space_constraint(a, pltpu.HBM)


def _gather_rows(buf, whole, half):
    r = buf.shape[1]
    return pl.ds(0, r) if whole else pl.ds(half * (r // 2), r // 2)


def _gather_start(bufs, whole, groups, name, after=()):
    n = len(bufs)
    n_g = len(groups)

    def body(*refs):
        refs = refs[:n] + refs[n + len(after):]
        sems = refs[n:n + 2 * n_g]
        outs = refs[n + 2 * n_g:2 * n + 2 * n_g]
        token = refs[2 * n + 2 * n_g]
        token[...] = jnp.zeros_like(token)
        x, y, c, chips = _place()
        p = 2 * x + y
        for gi, group in enumerate(groups):
            for ki, k in enumerate(group):
                blk = outs[k].at[p, _gather_rows(bufs[k], whole[k], c), :]
                for j, (px, py) in enumerate(chips):
                    pltpu.make_async_remote_copy(src_ref=blk, dst_ref=blk, send_sem=sems[2 * gi].at[3 * ki + j],
                                                 recv_sem=sems[2 * gi + 1].at[3 * ki + j],
                                                 device_id=(px, py, c), device_id_type=MESH).start()

    sem_shapes = []
    for group in groups:
        sem_shapes += [pltpu.SemaphoreType.DMA((3 * len(group),))] * 2
    res = pl.pallas_call(
        body,
        name=name,
        in_specs=[HBM] * n + [ANY] * len(after),
        out_specs=[SEM] * (2 * n_g) + [HBM] * n + [VMEM_SPEC],
        out_shape=sem_shapes + [pltpu.HBM(a.shape, a.dtype) for a in bufs] + [TOKEN],
        input_output_aliases={k: 2 * n_g + k for k in range(n)},
        compiler_params=pltpu.CompilerParams(has_side_effects=EFFECT),
    )(*[_in_hbm(a) for a in bufs], *after)
    sems = [(res[2 * gi], res[2 * gi + 1]) for gi in range(n_g)]
    return sems, list(res[2 * n_g:2 * n_g + n]), res[2 * n_g + n]


def _gather_wait(bufs, whole, send, recv, after, name):
    n = len(bufs)

    def body(*refs):
        ins = refs[:n]
        send_ref, recv_ref = refs[n], refs[n + 1]
        x, y, c, chips = _place()
        p = 2 * x + y
        for ki in range(n):
            rows = _gather_rows(bufs[ki], whole[ki], c)
            for j, (px, py) in enumerate(chips):
                cp = pltpu.make_async_remote_copy(src_ref=ins[ki].at[p, rows, :], dst_ref=ins[ki].at[2 * px + py, rows, :],
                                                  send_sem=send_ref.at[3 * ki + j], recv_sem=recv_ref.at[3 * ki + j],
                                                  device_id=(px, py, c), device_id_type=MESH)
                cp.wait_send()
                cp.wait_recv()

    res = pl.pallas_call(
        body,
        name=name,
        in_specs=[HBM] * n + [SEM, SEM] + [ANY] * len(after),
        out_specs=[HBM] * n,
        out_shape=[pltpu.HBM(a.shape, a.dtype) for a in bufs],
        input_output_aliases={k: k for k in range(n)},
        compiler_params=pltpu.CompilerParams(has_side_effects=EFFECT),
    )(*bufs, send, recv, *after)
    return list(res)


def _gather_hand_on(bufs, name, after=()):
    n = len(bufs)

    def body(*refs):
        refs = refs[n + len(after):]
        outs = refs[:n]
        send, recv = refs[n:]
        x, y, c, chips = _place()

        def d2d(k, j, half):
            px, py = chips[j]
            blk = outs[k].at[2 * px + py, _gather_rows(bufs[k], False, half), :]
            return pltpu.make_async_remote_copy(src_ref=blk, dst_ref=blk, send_sem=send.at[3 * k + j], recv_sem=recv.at[3 * k + j],
                                                device_id=(x, y, 1 - c), device_id_type=MESH)

        sent = [d2d(k, j, c) for k in range(n) for j in range(3)]
        for cp in sent:
            cp.start()
        for k in range(n):
            for j in range(3):
                d2d(k, j, 1 - c).wait_recv()
        for cp in sent:
            cp.wait_send()

    return pl.pallas_call(
        body,
        name=name,
        in_specs=[ANY] * (n + len(after)),
        out_specs=[ANY] * n,
        out_shape=[jax.ShapeDtypeStruct(a.shape, a.dtype) for a in bufs],
        input_output_aliases={k: k for k in range(n)},
        scratch_shapes=[pltpu.SemaphoreType.DMA((3 * n,)), pltpu.SemaphoreType.DMA((3 * n,))],
        compiler_params=pltpu.CompilerParams(has_side_effects=True),
    )(*bufs, *after)


def _hand_on_start(bufs, name, after=()):
    n = len(bufs)

    def body(*refs):
        refs = refs[:n] + refs[n + len(after):]
        send, recv = refs[n], refs[n + 1]
        outs = refs[n + 2:2 * n + 2]
        refs[2 * n + 2][...] = jnp.zeros(TOKEN.shape, TOKEN.dtype)
        x, y, c, chips = _place()
        for k in range(n):
            for j, (px, py) in enumerate(chips):
                blk = outs[k].at[2 * px + py, _gather_rows(bufs[k], False, c), :]
                pltpu.make_async_remote_copy(src_ref=blk, dst_ref=blk, send_sem=send.at[3 * k + j], recv_sem=recv.at[3 * k + j],
                                             device_id=(x, y, 1 - c), device_id_type=MESH).start()

    res = pl.pallas_call(
        body,
        name=name,
        in_specs=[HBM] * n + [ANY] * len(after),
        out_specs=[SEM, SEM] + [HBM] * n + [VMEM_SPEC],
        out_shape=[pltpu.SemaphoreType.DMA((3 * n,))] * 2 + [pltpu.HBM(a.shape, a.dtype) for a in bufs] + [TOKEN],
        input_output_aliases={k: 2 + k for k in range(n)},
        compiler_params=pltpu.CompilerParams(has_side_effects=EFFECT),
    )(*[_in_hbm(a) for a in bufs], *after)
    return res[0], res[1], list(res[2:2 + n]), res[2 + n]


def _hand_on_wait(bufs, send, recv, after, name):
    n = len(bufs)

    def body(*refs):
        ins = refs[:n]
        send_ref, recv_ref = refs[n], refs[n + 1]
        x, y, c, chips = _place()
        for k in range(n):
            for j, (px, py) in enumerate(chips):
                q = 2 * px + py
                cp = pltpu.make_async_remote_copy(src_ref=ins[k].at[q, _gather_rows(bufs[k], False, c), :],
                                                  dst_ref=ins[k].at[q, _gather_rows(bufs[k], False, 1 - c), :],
                                                  send_sem=send_ref.at[3 * k + j], recv_sem=recv_ref.at[3 * k + j],
                                                  device_id=(x, y, 1 - c), device_id_type=MESH)
                cp.wait_send()
                cp.wait_recv()

    res = pl.pallas_call(
        body,
        name=name,
        in_specs=[HBM] * n + [SEM, SEM] + [ANY] * len(after),
        out_specs=[HBM] * n,
        out_shape=[pltpu.HBM(a.shape, a.dtype) for a in bufs],
        input_output_aliases={k: k for k in range(n)},
        compiler_params=pltpu.CompilerParams(has_side_effects=EFFECT),
    )(*bufs, send, recv, *after)
    return list(res)


def _pair_exchange_start(part, name):
    rh = part.shape[1] // 2
    land_shape = (N_CHIPS, rh, part.shape[2])

    def body(part_ref, land_ref, send, recv, part_out, land_out, token):
        token[...] = jnp.zeros_like(token)
        x, y, c, _ = _place()
        pltpu.make_async_remote_copy(src_ref=part_out.at[:, pl.ds((1 - c) * rh, rh), :], dst_ref=land_out,
                                     send_sem=send, recv_sem=recv, device_id=(x, y, 1 - c), device_id_type=MESH).start()

    res = pl.pallas_call(
        body,
        name=name,
        in_specs=[HBM, HBM],
        out_specs=[SEM, SEM, HBM, HBM, VMEM_SPEC],
        out_shape=[pltpu.SemaphoreType.DMA(()), pltpu.SemaphoreType.DMA(()), pltpu.HBM(part.shape, part.dtype),
                   pltpu.HBM(land_shape, part.dtype), TOKEN],
        input_output_aliases={0: 2, 1: 3},
        compiler_params=pltpu.CompilerParams(has_side_effects=EFFECT),
    )(_in_hbm(part), _in_hbm(lax.empty(land_shape, part.dtype)))
    return res


def _pair_exchange_wait(send, recv, part, land, after, name):
    rh = part.shape[1] // 2

    def body(part_ref, land_ref, send_ref, recv_ref, after_ref, part_out, land_out):
        x, y, c, _ = _place()
        cp = pltpu.make_async_remote_copy(src_ref=part_ref.at[:, pl.ds((1 - c) * rh, rh), :], dst_ref=land_ref,
                                          send_sem=send_ref, recv_sem=recv_ref, device_id=(x, y, 1 - c), device_id_type=MESH)
        cp.wait_send()
        cp.wait_recv()

    return pl.pallas_call(
        body,
        name=name,
        in_specs=[HBM, HBM, SEM, SEM, ANY],
        out_specs=[HBM, HBM],
        out_shape=[pltpu.HBM(part.shape, part.dtype), pltpu.HBM(land.shape, land.dtype)],
        input_output_aliases={0: 0, 1: 1},
        compiler_params=pltpu.CompilerParams(has_side_effects=EFFECT),
    )(part, land, send, recv, after)


def _chip_exchange_start(sums, name):
    n = len(sums)

    def body(*refs):
        send, recv = refs[2 * n], refs[2 * n + 1]
        src = refs[2 * n + 2:3 * n + 2]
        dst = refs[3 * n + 2:4 * n + 2]
        refs[4 * n + 2][...] = jnp.zeros(TOKEN.shape, TOKEN.dtype)
        x, y, c, chips = _place()
        p = 2 * x + y
        for k in range(n):
            for j, (px, py) in enumerate(chips):
                pltpu.make_async_remote_copy(src_ref=src[k].at[2 * px + py], dst_ref=dst[k].at[p],
                                             send_sem=send.at[3 * k + j], recv_sem=recv.at[3 * k + j],
                                             device_id=(px, py, c), device_id_type=MESH).start()

    res = pl.pallas_call(
        body,
        name=name,
        in_specs=[HBM] * (2 * n),
        out_specs=[SEM, SEM] + [HBM] * (2 * n) + [VMEM_SPEC],
        out_shape=[pltpu.SemaphoreType.DMA((3 * n,))] * 2 + [pltpu.HBM(a.shape, a.dtype) for a in sums] * 2 + [TOKEN],
        input_output_aliases={k: 2 + k for k in range(2 * n)},
        compiler_params=pltpu.CompilerParams(has_side_effects=EFFECT),
    )(*[_in_hbm(a) for a in sums], *[_in_hbm(lax.empty(a.shape, a.dtype)) for a in sums])
    return res[0], res[1], list(res[2:2 + n]), list(res[2 + n:2 + 2 * n]), res[2 + 2 * n]


def _chip_exchange_wait(sums, slots, send, recv, after, name):
    n = len(sums)

    def body(*refs):
        src, dst = refs[:n], refs[n:2 * n]
        send_ref, recv_ref = refs[2 * n], refs[2 * n + 1]
        x, y, c, chips = _place()
        for k in range(n):
            for j, (px, py) in enumerate(chips):
                cp = pltpu.make_async_remote_copy(src_ref=src[k].at[2 * px + py], dst_ref=dst[k].at[2 * px + py],
                                                  send_sem=send_ref.at[3 * k + j], recv_sem=recv_ref.at[3 * k + j],
                                                  device_id=(px, py, c), device_id_type=MESH)
                cp.wait_send()
                cp.wait_recv()

    res = pl.pallas_call(
        body,
        name=name,
        in_specs=[HBM] * (2 * n) + [SEM, SEM] + [ANY] * len(after),
        out_specs=[HBM] * (2 * n),
        out_shape=[pltpu.HBM(a.shape, a.dtype) for a in sums] * 2,
        input_output_aliases={k: k for k in range(2 * n)},
        compiler_params=pltpu.CompilerParams(has_side_effects=EFFECT),
    )(*sums, *slots, send, recv, *after)
    return list(res[:n]), list(res[n:])


def _pair_exchange(parts, name):
    n = len(parts)

    def body(*refs):
        ins, outs = refs[:n], refs[n:2 * n]
        send, recv = refs[2 * n:]
        x, y, c, _ = _place()
        copies = []
        for k in range(n):
            rh = parts[k].shape[1] // 2
            cp = pltpu.make_async_remote_copy(
                src_ref=ins[k].at[:, pl.ds((1 - c) * rh, rh), :], dst_ref=outs[k],
                send_sem=send.at[k], recv_sem=recv.at[k], device_id=(x, y, 1 - c), device_id_type=MESH)
            cp.start()
            copies.append(cp)
        for cp in copies:
            cp.wait()

    return pl.pallas_call(
        body,
        name=name,
        in_specs=[ANY] * n,
        out_specs=[ANY] * n,
        out_shape=[jax.ShapeDtypeStruct((N_CHIPS, a.shape[1] // 2, a.shape[2]), a.dtype) for a in parts],
        scratch_shapes=[pltpu.SemaphoreType.DMA((n,)), pltpu.SemaphoreType.DMA((n,))],
        compiler_params=pltpu.CompilerParams(has_side_effects=True),
    )(*parts)


def _pair_sum(core, part, landed, name):
    _, r, cdim = part.shape
    rh = r // 2
    tr = _row_tile(rh, cdim, 2)
    nb = rh // tr

    def body(c_ref, a_ref, b_ref, o_ref):
        o_ref[...] = (a_ref[...].astype(F32) + b_ref[...].astype(F32)).astype(o_ref.dtype)

    return pl.pallas_call(
        body,
        name=name,
        grid_spec=pltpu.PrefetchScalarGridSpec(
            num_scalar_prefetch=1,
            grid=(N_CHIPS, nb),
            in_specs=[pl.BlockSpec((None, tr, cdim), lambda q, i, c_ref: (q, c_ref[0] * nb + i, 0)),
                      pl.BlockSpec((None, tr, cdim), lambda q, i, c_ref: (q, i, 0))],
            out_specs=pl.BlockSpec((None, tr, cdim), lambda q, i, c_ref: (q, i, 0))),
        out_shape=jax.ShapeDtypeStruct((N_CHIPS, rh, cdim), BF16),
        compiler_params=_params(("parallel", "parallel")),
    )(core, part, landed)


def _chip_sum(place, own, slots, name, after=()):
    _, rh, cdim = slots.shape
    tr = _row_tile(rh, cdim, 2)
    nb = rh // tr

    def body(place_ref, own_ref, s1_ref, s2_ref, s3_ref, *rest):
        o_ref = rest[len(after)]
        acc = own_ref[...].astype(F32)
        for s_ref in (s1_ref, s2_ref, s3_ref):
            acc = acc + s_ref[...].astype(F32)
        o_ref[...] = acc

    def other(j):
        return lambda i, place_ref: ((place_ref[0] + j) % N_CHIPS, i, 0)

    return pl.pallas_call(
        body,
        name=name,
        grid_spec=pltpu.PrefetchScalarGridSpec(
            num_scalar_prefetch=1,
            grid=(nb,),
            in_specs=[pl.BlockSpec((None, tr, cdim), other(j)) for j in (0, 1, 2, 3)] + [ANY] * len(after),
            out_specs=pl.BlockSpec((tr, cdim), lambda i, place_ref: (place_ref[1] * nb + i, 0))),
        out_shape=jax.ShapeDtypeStruct((2 * rh, cdim), F32),
        compiler_params=_params(("parallel",)),
    )(place, own, slots, slots, slots, *after)


def _half_exchange_start(buf, name):
    rh = buf.shape[0] // 2

    def body(buf_ref, send, recv, out_ref, token):
        token[...] = jnp.zeros_like(token)
        x, y, c, _ = _place()
        mine = out_ref.at[pl.ds(c * rh, rh), :]
        pltpu.make_async_remote_copy(src_ref=mine, dst_ref=mine, send_sem=send, recv_sem=recv,
                                     device_id=(x, y, 1 - c), device_id_type=MESH).start()

    return pl.pallas_call(
        body,
        name=name,
        in_specs=[HBM],
        out_specs=[SEM, SEM, HBM, VMEM_SPEC],
        out_shape=[pltpu.SemaphoreType.DMA(()), pltpu.SemaphoreType.DMA(()), pltpu.HBM(buf.shape, buf.dtype), TOKEN],
        input_output_aliases={0: 2},
        compiler_params=pltpu.CompilerParams(has_side_effects=EFFECT),
    )(_in_hbm(buf))


def _half_exchange_wait(send, recv, buf, after, name):
    rh = buf.shape[0] // 2

    def body(buf_ref, send_ref, recv_ref, *rest):
        x, y, c, _ = _place()
        cp = pltpu.make_async_remote_copy(src_ref=buf_ref.at[pl.ds(c * rh, rh), :], dst_ref=buf_ref.at[pl.ds((1 - c) * rh, rh), :],
                                          send_sem=send_ref, recv_sem=recv_ref, device_id=(x, y, 1 - c), device_id_type=MESH)
        cp.wait_send()
        cp.wait_recv()

    return pl.pallas_call(
        body,
        name=name,
        in_specs=[HBM, SEM, SEM] + [ANY] * len(after),
        out_specs=HBM,
        out_shape=pltpu.HBM(buf.shape, buf.dtype),
        input_output_aliases={0: 0},
        compiler_params=pltpu.CompilerParams(has_side_effects=EFFECT),
    )(buf, send, recv, *after)


def _half_exchange(bufs, name):
    n = len(bufs)

    def body(*refs):
        outs = refs[n:2 * n]
        send, recv = refs[2 * n:]
        x, y, c, _ = _place()
        copies = []
        for k in range(n):
            rh = bufs[k].shape[0] // 2
            mine = outs[k].at[pl.ds(c * rh, rh), :]
            cp = pltpu.make_async_remote_copy(src_ref=mine, dst_ref=mine, send_sem=send.at[k], recv_sem=recv.at[k],
                                              device_id=(x, y, 1 - c), device_id_type=MESH)
            cp.start()
            copies.append(cp)
        for k in range(n):
            rh = bufs[k].shape[0] // 2
            theirs = outs[k].at[pl.ds((1 - c) * rh, rh), :]
            copies[k].wait_send()
            pltpu.make_async_remote_copy(src_ref=theirs, dst_ref=theirs, send_sem=send.at[k], recv_sem=recv.at[k],
                                         device_id=(x, y, 1 - c), device_id_type=MESH).wait_recv()

    return pl.pallas_call(
        body,
        name=name,
        in_specs=[ANY] * n,
        out_specs=[ANY] * n,
        out_shape=[jax.ShapeDtypeStruct(a.shape, a.dtype) for a in bufs],
        input_output_aliases={k: k for k in range(n)},
        scratch_shapes=[pltpu.SemaphoreType.DMA((n,)), pltpu.SemaphoreType.DMA((n,))],
        compiler_params=pltpu.CompilerParams(has_side_effects=True),
    )(*bufs)


SMALL_ROWS = 32


def _small_peers():
    x, y, c, _ = _place()
    peers = []
    for k in range(1, N_DEV):
        px, py, pc = x ^ ((k >> 2) & 1), y ^ ((k >> 1) & 1), c ^ (k & 1)
        peers.append((k, (px, py, pc), 4 * px + 2 * py + pc))
    return 4 * x + 2 * y + c, peers


def _all_gather_small_start(gath):
    def body(in_ref, send, recv, out_ref, token):
        token[...] = jnp.zeros_like(token)
        me, peers = _small_peers()
        for k, peer, _ in peers:
            pltpu.make_async_remote_copy(src_ref=out_ref.at[me], dst_ref=out_ref.at[me], send_sem=send.at[k], recv_sem=recv.at[k],
                                         device_id=peer, device_id_type=MESH).start()

    return pl.pallas_call(
        body,
        name="small_gather_start",
        in_specs=[HBM],
        out_specs=[SEM, SEM, HBM, VMEM_SPEC],
        out_shape=[pltpu.SemaphoreType.DMA((N_DEV,)), pltpu.SemaphoreType.DMA((N_DEV,)), pltpu.HBM(gath.shape, gath.dtype), TOKEN],
        input_output_aliases={0: 2},
        compiler_params=pltpu.CompilerParams(has_side_effects=EFFECT),
    )(_in_hbm(gath))


def _all_gather_small_wait(send, recv, gath, after):
    def body(in_ref, send_ref, recv_ref, *rest):
        me, peers = _small_peers()
        for k, peer, peer_id in peers:
            cp = pltpu.make_async_remote_copy(src_ref=in_ref.at[me], dst_ref=in_ref.at[peer_id], send_sem=send_ref.at[k],
                                              recv_sem=recv_ref.at[k], device_id=peer, device_id_type=MESH)
            cp.wait_send()
            cp.wait_recv()

    return pl.pallas_call(
        body,
        name="small_gather_wait",
        in_specs=[HBM, SEM, SEM] + [ANY] * len(after),
        out_specs=HBM,
        out_shape=pltpu.HBM(gath.shape, gath.dtype),
        input_output_aliases={0: 0},
        compiler_params=pltpu.CompilerParams(has_side_effects=EFFECT),
    )(gath, send, recv, *after)


def _sum_small(gath):
    def body(g_ref, o_ref):
        acc = g_ref[0]
        for dev in range(1, N_DEV):
            acc = acc + g_ref[dev]
        o_ref[...] = acc

    return pl.pallas_call(
        body,
        name="small_sum",
        in_specs=[VMEM_SPEC],
        out_specs=VMEM_SPEC,
        out_shape=jax.ShapeDtypeStruct(gath.shape[1:], F32),
        compiler_params=pltpu.CompilerParams(vmem_limit_bytes=VMEM_LIMIT),
    )(gath)


def _adamw_update(g_ref, w_ref, m_ref, v_ref, go_ref, d_ref, mo_ref, vo_ref):
    bc1 = 1.0 - ADAM_B1 ** ADAM_STEP
    bc2 = 1.0 - ADAM_B2 ** ADAM_STEP
    gv = g_ref[...]
    mn = ADAM_B1 * m_ref[...] + (1.0 - ADAM_B1) * gv
    vn = ADAM_B2 * v_ref[...] + (1.0 - ADAM_B2) * (gv * gv)
    go_ref[...] = gv
    mo_ref[...] = mn
    vo_ref[...] = vn
    d_ref[...] = -ADAM_LR * ((mn / bc1) / (jnp.sqrt(vn / bc2) + ADAM_EPS) + ADAM_WD * w_ref[...])


def _adamw_small(gs, ws, ms, vs):
    n = len(gs)

    def body(*refs):
        ins, outs = refs[:4 * n], refs[4 * n:]
        for k in range(n):
            _adamw_update(*[ins[i * n + k] for i in range(4)], *outs[4 * k:4 * k + 4])

    vmem = pl.BlockSpec(memory_space=pltpu.VMEM)
    res = pl.pallas_call(
        body,
        name="adamw_small",
        in_specs=[vmem] * (4 * n),
        out_specs=[vmem] * (4 * n),
        out_shape=[jax.ShapeDtypeStruct(w.shape, F32) for w in ws for _ in range(4)],
        compiler_params=pltpu.CompilerParams(vmem_limit_bytes=VMEM_LIMIT),
    )(*gs, *ws, *ms, *vs)
    return [tuple(res[4 * k:4 * k + 4]) for k in range(n)]


def _adamw(g, w, m, v, name, g_block=0):
    r, cdim = w.shape
    tr = _row_tile(r, cdim, 4)

    def body(*refs):
        _adamw_update(*refs)

    spec = pl.BlockSpec((tr, cdim), lambda i: (i, 0))
    return pl.pallas_call(
        body,
        name=name,
        grid=(r // tr,),
        in_specs=[pl.BlockSpec((tr, cdim), lambda i: (i, g_block))] + [spec] * 3,
        out_specs=[spec] * 4,
        out_shape=[jax.ShapeDtypeStruct((r, cdim), F32)] * 4,
        compiler_params=_params(("parallel",)),
    )(g, w, m, v)


VEC_NAMES = ["pre_mix_norm", "q_norm", "kv_norm", "conv_b", "conv_ln_g", "conv_ln_b", "conv_out_norm",
             "attn_out_norm", "post_mix_norm", "pre_ffn_norm", "post_ffn_norm"]
LOSS_ROW = len(VEC_NAMES)
CONV_W_ROW = 16


def _cols_to_full(parts):
    _, r, cdim = parts.shape
    return parts.transpose(1, 0, 2).reshape(r, N_CHIPS * cdim)


def _full_to_cols(full):
    r, n = full.shape
    return full.reshape(r, N_CHIPS, n // N_CHIPS).transpose(1, 0, 2)


W_IN_SHARD = (2 * CONV_CH + Q_LORA + KV_LORA + QK_ROPE) // N_CHIPS
W_IN_PART = 1024
W_IN_BLOCKS = (2 * CONV_CH + Z2_COLS) // LANES
W_IN_BASE = [p * W_IN_SHARD // LANES for p in range(N_CHIPS)]
W_IN_SPAN = [-(-(p * W_IN_SHARD % LANES + W_IN_SHARD) // LANES) for p in range(N_CHIPS)]


def _w_in_block_home(b):
    n = CONV_CH // LANES
    if b < n:
        return 0, 2 * b
    if b < 2 * n:
        return 0, 2 * (b - n) + 1
    return 1, b - 2 * n


def _to_parts_w_in(shift_chip, w_in):
    r = w_in.shape[0]
    tr = 512

    def body(s_ref, a_ref, o_ref):
        o_ref[...] = jnp.zeros_like(o_ref)
        o_ref[:, :W_IN_SHARD] = a_ref[...].astype(o_ref.dtype)
        o_ref[...] = pltpu.roll(o_ref[...].astype(F32), s_ref[0], 1).astype(o_ref.dtype)

    return pl.pallas_call(
        body,
        name="to_parts_w_in",
        grid_spec=pltpu.PrefetchScalarGridSpec(
            num_scalar_prefetch=1,
            grid=(r // tr,),
            in_specs=[pl.BlockSpec((tr, W_IN_SHARD), lambda i, s_ref: (i, 0))],
            out_specs=pl.BlockSpec((None, tr, W_IN_PART), lambda i, s_ref: (s_ref[1], i, 0))),
        out_shape=jax.ShapeDtypeStruct((N_CHIPS, r, W_IN_PART), BF16),
        compiler_params=_params(("parallel",)),
    )(shift_chip, w_in)


def _assemble_w_in(parts):
    r = parts.shape[1]
    tr = ROW_TILE

    def body(p_ref, ag_ref, z2_ref):
        outs = (ag_ref, z2_ref)
        for b in range(W_IN_BLOCKS):
            blk = None
            for p in range(N_CHIPS):
                i = b - W_IN_BASE[p]
                if 0 <= i < W_IN_SPAN[p]:
                    piece = p_ref[p, :, i * LANES:(i + 1) * LANES]
                    blk = piece if blk is None else blk + piece
            which, at = _w_in_block_home(b)
            outs[which][:, at * LANES:(at + 1) * LANES] = blk

    w_ag, w_z2 = pl.pallas_call(
        body,
        name="assemble_w_in",
        grid=(r // tr,),
        in_specs=[pl.BlockSpec((N_CHIPS, tr, W_IN_PART), lambda i: (0, i, 0))],
        out_specs=[pl.BlockSpec((tr, 2 * CONV_CH), lambda i: (i, 0)), pl.BlockSpec((tr, Z2_COLS), lambda i: (i, 0))],
        out_shape=[jax.ShapeDtypeStruct((r, 2 * CONV_CH), parts.dtype), jax.ShapeDtypeStruct((r, Z2_COLS), parts.dtype)],
        compiler_params=_params(("parallel",)),
    )(parts)
    return dict(w_ag=w_ag, w_z2=w_z2)


def _w_in_grad_parts(dw_ag, dw_z2):
    r = dw_ag.shape[0]
    tr = ROW_TILE

    def body(ag_ref, z2_ref, o_ref):
        ins = (ag_ref, z2_ref)
        for p in range(N_CHIPS):
            for i in range(W_IN_PART // LANES):
                if i < W_IN_SPAN[p]:
                    which, at = _w_in_block_home(W_IN_BASE[p] + i)
                    o_ref[p, :, i * LANES:(i + 1) * LANES] = ins[which][:, at * LANES:(at + 1) * LANES]
                else:
                    o_ref[p, :, i * LANES:(i + 1) * LANES] = jnp.zeros((tr, LANES), o_ref.dtype)

    return pl.pallas_call(
        body,
        name="w_in_grad_parts",
        grid=(r // tr,),
        in_specs=[pl.BlockSpec((tr, 2 * CONV_CH), lambda i: (i, 0)), pl.BlockSpec((tr, Z2_COLS), lambda i: (i, 0))],
        out_specs=pl.BlockSpec((N_CHIPS, tr, W_IN_PART), lambda i: (0, i, 0)),
        out_shape=jax.ShapeDtypeStruct((N_CHIPS, r, W_IN_PART), dw_ag.dtype),
        compiler_params=_params(("parallel",)),
    )(dw_ag, dw_z2)


def _assemble_mixer_rest(g):
    uq = _cols_to_full(g["w_uq"]).reshape(Q_LORA, N_HEADS, QK_HEAD)
    w_uq = jnp.concatenate([uq[:, :, :QK_NOPE].reshape(Q_LORA, N_HEADS * QK_NOPE),
                            uq[:, :, QK_NOPE:].reshape(Q_LORA, N_HEADS * QK_ROPE)], axis=1)
    return dict(w_uq=w_uq, w_ukv=_cols_to_full(g["w_ukv"]), conv_w=_cols_to_full(g["conv_w"]),
                w_out=g["w_out"].reshape(-1, g["w_out"].shape[2]))


def _grads_to_parts(dw):
    uq = dw["w_uq"]
    d_uq = jnp.concatenate([uq[:, :N_HEADS * QK_NOPE].reshape(Q_LORA, N_HEADS, QK_NOPE),
                            uq[:, N_HEADS * QK_NOPE:].reshape(Q_LORA, N_HEADS, QK_ROPE)], axis=2).reshape(Q_LORA, N_HEADS * QK_HEAD)
    return dict(w_in=_w_in_grad_parts(dw["w_ag"], dw["w_z2"]), w_uq=_full_to_cols(d_uq), w_ukv=_full_to_cols(dw["w_ukv"]),
                w_out=dw["w_out"].reshape(N_CHIPS, -1, dw["w_out"].shape[1]))


MIXER = ["w_in", "w_uq", "w_ukv", "w_out"]
FFN = ["w_gu", "w_down"]
BIG = MIXER + FFN


def _pad_lanes(v, n):
    return jnp.pad(v, ((0, 0), (0, n - v.shape[1])))


def kernel(x, positions, pre_mix_norm, w_in, q_norm, w_uq, kv_norm, w_ukv, conv_w, conv_b, conv_ln_g, conv_ln_b, conv_out_norm, attn_out_norm, w_out, post_mix_norm, pre_ffn_norm, w_gate, w_up, w_down, post_ffn_norm, loss_target, m_pre_mix_norm, m_w_in, m_q_norm, m_w_uq, m_kv_norm, m_w_ukv, m_conv_w, m_conv_b, m_conv_ln_g, m_conv_ln_b, m_conv_out_norm, m_attn_out_norm, m_w_out, m_post_mix_norm, m_pre_ffn_norm, m_w_gate, m_w_up, m_w_down, m_post_ffn_norm, v_pre_mix_norm, v_w_in, v_q_norm, v_w_uq, v_kv_norm, v_w_ukv, v_conv_w, v_conv_b, v_conv_ln_g, v_conv_ln_b, v_conv_out_norm, v_attn_out_norm, v_w_out, v_post_mix_norm, v_pre_ffn_norm, v_w_gate, v_w_up, v_w_down, v_post_ffn_norm):
    given = dict(locals())
    names = ["pre_mix_norm", "w_in", "q_norm", "w_uq", "kv_norm", "w_ukv", "conv_w", "conv_b", "conv_ln_g", "conv_ln_b",
             "conv_out_norm", "attn_out_norm", "w_out", "post_mix_norm", "pre_ffn_norm", "w_gate", "w_up", "w_down", "post_ffn_norm"]
    def as_2d(a):
        return a if a.ndim == 2 else a[0]

    weights = {n: as_2d(given[n]) for n in names}
    mom = {n: as_2d(given["m_" + n]) for n in names}
    var = {n: as_2d(given["v_" + n]) for n in names}
    d = D_MODEL

    inv_freq = ROPE_THETA ** (-jnp.arange(0, QK_ROPE, 2, dtype=F32) / QK_ROPE)
    ang = positions[0].astype(F32)[:, None] * inv_freq
    cos, sin = jnp.cos(ang), jnp.sin(ang)
    cos2 = jnp.concatenate([cos, cos, cos, cos], axis=1)
    sin2 = jnp.concatenate([-sin, sin, -sin, sin], axis=1)

    chip = 2 * lax.axis_index("x") + lax.axis_index("y")
    core = lax.axis_index("c")
    chip1 = chip.astype(jnp.int32).reshape(1)
    pieces = {n: [weights[n]] for n in ("w_uq", "w_ukv", "w_out", "w_down")}
    pieces["w_gu"] = [weights["w_gate"], weights["w_up"]]
    core1 = core.astype(jnp.int32).reshape(1)
    place = jnp.stack([chip, core]).astype(jnp.int32)
    rest = ["w_uq", "w_ukv", "w_out"]
    w_in_shift = (chip * W_IN_SHARD) % LANES
    w_in_buf = _to_parts_w_in(jnp.stack([w_in_shift, chip]).astype(jnp.int32), weights["w_in"])
    (w_in_sems,), w_in_thru, _ = _gather_start([w_in_buf], [False], [[0]], "gather_start_w_in")
    rest_bufs = [_to_parts(chip1, pieces[n], BF16, "to_parts_" + n) for n in rest]
    rest_bufs.append(_to_parts(chip1, [jnp.pad(weights["conv_w"], ((0, CONV_K_PAD - CONV_K), (0, 0)))], F32, "to_parts_conv_w"))
    rest_whole = [False] * 3 + [True]
    ffn_bufs = [_to_parts(chip1, pieces[n], BF16, "to_parts_" + n) for n in FFN]
    got = _gather_wait(w_in_thru, [False], *w_in_sems, rest_bufs + ffn_bufs, "gather_wait_w_in")
    (rest_sems,), rest_thru, started = _gather_start(rest_bufs, rest_whole, [[0, 1, 2, 3]], "gather_start_mixer_rest", after=got)
    full = _assemble_w_in(_gather_hand_on(got, "gather_hand_on_w_in", after=[started])[0])
    vec = {n: weights[n] for n in VEC_NAMES}
    rs = {}

    def get_mixer_rest(after):
        got = _gather_wait(rest_thru, rest_whole, *rest_sems, [after], "gather_wait_mixer_rest")
        (rs["w_gu_sems"],), rs["w_gu_thru"], started = _gather_start(ffn_bufs[:1], [False], [[0]], "gather_start_w_gu", after=got[:1])
        got = list(_gather_hand_on(got[:3], "gather_hand_on_mixer_rest", after=[started])) + [got[3]]
        return _assemble_mixer_rest(dict(zip(rest + ["conv_w"], got)))

    def w_gu_landed(after):
        got = _gather_wait(rs["w_gu_thru"], [False], *rs["w_gu_sems"], [after], "gather_wait_w_gu")
        (rs["w_down_sems"],), rs["w_down_thru"], started = _gather_start(ffn_bufs[1:], [False], [[0]], "gather_start_w_down", after=got)
        rs["w_gu"] = _hand_on_start(got, "hand_on_start_w_gu", after=[started])
        return rs["w_gu"][3]

    def get_w_gu(after):
        send, recv, bufs, _ = rs["w_gu"]
        return _hand_on_wait(bufs, send, recv, [after], "hand_on_wait_w_gu")[0]

    def get_w_down(after):
        got = _gather_wait(rs["w_down_thru"], [False], *rs["w_down_sems"], [after], "gather_wait_w_down")
        got = _gather_hand_on(got, "gather_hand_on_w_down")[0]
        return got.reshape(-1, got.shape[2])

    def pair_start(key):
        def hook(dw):
            rs[key] = _pair_exchange_start(dw.reshape(N_CHIPS, -1, dw.shape[-1]), "grad_pair_start_" + key)
            return rs[key][4]
        return hook

    def reduce_start(group, plist, landed):
        sums = [_pair_sum(core1, a, b, "grad_pair_sum_%s_%d" % (group, k)) for k, (a, b) in enumerate(zip(plist, landed))]
        rs[group] = _chip_exchange_start(sums, "grad_chip_exchange_start_" + group)
        return rs[group][4]

    def reduce_finish(group, after):
        send, recv, sums, slots, _ = rs[group]
        sums, slots = _chip_exchange_wait(sums, slots, send, recv, after, "grad_chip_exchange_wait_" + group)
        halves = [_chip_sum(place, s, sl, "grad_chip_sum_%s_%d" % (group, k)) for k, (s, sl) in enumerate(zip(sums, slots))]
        return list(_half_exchange(halves, "grad_half_exchange_" + group))

    def ffn_grads_exchanged(after):
        pairs = [_pair_exchange_wait(*rs[key][:4], after, "grad_pair_wait_" + key) for key in ("dw_gu", "dw_down")]
        return reduce_start("ffn", [p[0] for p in pairs], [p[1] for p in pairs])

    hooks = dict(mixer_rest=get_mixer_rest, w_gu_landed=w_gu_landed, w_gu=get_w_gu, w_down=get_w_down, dw_down=pair_start("dw_down"), dw_gu=pair_start("dw_gu"),
                 grads_exchanged=ffn_grads_exchanged)
    loss, grad_x, dw, dvec = _local_step(x[0], loss_target[0], cos2, sin2, vec, full, hooks)

    rows = [_pad_lanes(dvec[n], d) for n in VEC_NAMES] + [_pad_lanes(loss, d)]
    rows.append(jnp.zeros((CONV_W_ROW - len(rows), d), F32))
    rows.append(dw["conv_w"].reshape(SMALL_ROWS - CONV_W_ROW, d))
    device1 = (2 * chip + core).astype(jnp.int32).reshape(1)
    small_gather = _all_gather_small_start(_to_parts(device1, [jnp.concatenate(rows, axis=0)], F32, "small_to_slot", n_parts=N_DEV))

    parts = _grads_to_parts(dw)
    plist = [parts[n] for n in MIXER]
    started = reduce_start("mixer", plist, _pair_exchange(plist, "grad_pair_exchange_mixer"))
    send, recv, sums, slots, _ = rs["ffn"]
    sums, slots = _chip_exchange_wait(sums, slots, send, recv, [started, small_gather[3]], "grad_chip_exchange_wait_ffn")
    down = _half_exchange_start(_chip_sum(place, sums[1], slots[1], "grad_chip_sum_ffn_1"), "grad_half_start_w_down")
    gu = _half_exchange_start(_chip_sum(place, sums[0], slots[0], "grad_chip_sum_ffn_0", after=[down[3]]), "grad_half_start_w_gu")
    res = {}
    g_down = _half_exchange_wait(*down[:3], [gu[3]], "grad_half_wait_w_down")
    res["w_down"] = _adamw(g_down, weights["w_down"], mom["w_down"], var["w_down"], "adamw_w_down")
    g_gu = _half_exchange_wait(*gu[:3], [res["w_down"][1]], "grad_half_wait_w_gu")
    for n, blk in (("w_gate", 0), ("w_up", 1)):
        res[n] = _adamw(g_gu, weights[n], mom[n], var[n], "adamw_" + n, g_block=blk)
    small = _sum_small(_all_gather_small_wait(*small_gather[:3], [res["w_up"][1]]))
    g_conv_w_full = small[CONV_W_ROW:].reshape(CONV_K_PAD, CONV_CH)
    g_small = {n: small[i:i + 1, :weights[n].shape[1]] for i, n in enumerate(VEC_NAMES)}
    g_small["conv_w"] = lax.dynamic_slice(g_conv_w_full, (0, chip * (CONV_CH // N_CHIPS)), (CONV_K_PAD, CONV_CH // N_CHIPS))[:CONV_K]
    loss_out = small[LOSS_ROW, 0]
    small_names = VEC_NAMES + ["conv_w"]
    res.update(zip(small_names, _adamw_small([g_small[n] for n in small_names], [weights[n] for n in small_names],
                                             [mom[n] for n in small_names], [var[n] for n in small_names])))
    done_meanwhile = [res["w_gate"][1], res["w_up"][1], res["w_down"][1], res["conv_w"][1], grad_x]
    g_mixer = reduce_finish("mixer", done_meanwhile)
    g_mixer[0] = lax.dynamic_slice(g_mixer[0], (0, w_in_shift), (g_mixer[0].shape[0], W_IN_SHARD))
    for n, g in zip(MIXER, g_mixer):
        res[n] = _adamw(g, weights[n], mom[n], var[n], "adamw_" + n)
    outs = [loss_out, grad_x[None]]
    for i in range(4):
        outs += [res[n][i].reshape(given[n].shape) for n in names]
    return tuple(outs)
```

```python
import functools

import jax
import jax.numpy as jnp
from jax import lax
from jax.experimental import pallas as pl
from jax.experimental.pallas import tpu as pltpu

F32 = jnp.float32
BF16 = jnp.bfloat16

D_MODEL = 2048
CONV_CH = 1024
CONV_K = 31
CONV_K_PAD = 32
N_HEADS = 8
QK_NOPE = 128
QK_ROPE = 64
V_HEAD = 128
QK_HEAD = QK_NOPE + QK_ROPE
Q_LORA = 768
KV_LORA = 512
ATTN_CH = N_HEADS * V_HEAD
Z2_COLS = Q_LORA + KV_LORA + 128
D_FF = 5632
ROPE_THETA = 10000.0
EPS = 1e-6
LANES = 128
N_CHIPS = 4
N_DEV = 8

ADAM_LR = 0.001
ADAM_B1 = 0.9
ADAM_B2 = 0.999
ADAM_EPS = 1e-08
ADAM_WD = 0.01
ADAM_STEP = 10

VMEM_LIMIT = 56 * 1024 * 1024
ROW_TILE = 256
MAX_TK = 2816
MESH = pl.DeviceIdType.MESH


def _params(sem=None):
    return pltpu.CompilerParams(dimension_semantics=sem, vmem_limit_bytes=VMEM_LIMIT)


def _first_divisor(n, cands):
    for c in cands:
        if n % c == 0:
            return c
    return n


STREAM_BLOCK_BYTES = 3 << 19


def _row_tile(rows, cols, itemsize):
    for tr in (1024, 704, 512, 384, 352, 256, 176, 128, 64, 32, 16):
        if rows % tr == 0 and tr * cols * itemsize <= STREAM_BLOCK_BYTES:
            return tr
    return rows


def _matmul(pairs, mode, out_dtype, name, b_parts=None, out_parts=False, tiles=(None, None, None), after=None):
    a0, b0 = pairs[0]
    part_c = b0.shape[2] if b_parts else None
    if mode == "nn":
        m, n = a0.shape[0], (N_CHIPS * part_c if b_parts else b0.shape[1])
        ks = [a.shape[1] for a, _ in pairs]
    elif mode == "nt":
        m, n = a0.shape[0], b0.shape[-2]
        ks = [a.shape[1] for a, _ in pairs]
    else:
        m, n = a0.shape[1], b0.shape[1]
        ks = [a.shape[0] for a, _ in pairs]
    tm = tiles[0] or _first_divisor(m, (1024, 768, 512, 256))
    tn = tiles[1] or (n if n <= 1536 else _first_divisor(n, (1024, 512, 256, 128)))
    tks = [tiles[2] or (k if k <= MAX_TK else MAX_TK) for k in ks]
    nks = [k // tk for k, tk in zip(ks, tks)]
    offs = [sum(nks[:p]) for p in range(len(pairs))]
    nk = sum(nks)
    n_pairs = len(pairs)
    assert not (b_parts or out_parts) or n_pairs == 1

    def kk(k, p):
        return jnp.clip(k - offs[p], 0, nks[p] - 1)

    in_specs = []
    for p in range(n_pairs):
        tk = tks[p]
        if mode == "nn":
            in_specs.append(pl.BlockSpec((tm, tk), lambda i, j, k, p=p: (i, kk(k, p))))
            if b_parts == "n":
                per = part_c // tn
                in_specs.append(pl.BlockSpec((None, tk, tn), lambda i, j, k: (j // per, k, j % per)))
            else:
                in_specs.append(pl.BlockSpec((tk, tn), lambda i, j, k, p=p: (kk(k, p), j)))
        elif mode == "nt":
            in_specs.append(pl.BlockSpec((tm, tk), lambda i, j, k, p=p: (i, kk(k, p))))
            if b_parts == "k":
                per = part_c // tk
                in_specs.append(pl.BlockSpec((None, tn, tk), lambda i, j, k: (k // per, j, k % per)))
            else:
                in_specs.append(pl.BlockSpec((tn, tk), lambda i, j, k, p=p: (j, kk(k, p))))
        else:
            in_specs.append(pl.BlockSpec((tk, tm), lambda i, j, k, p=p: (kk(k, p), i)))
            in_specs.append(pl.BlockSpec((tk, tn), lambda i, j, k, p=p: (kk(k, p), j)))
    if out_parts:
        out_per = (n // N_CHIPS) // tn
        out_spec = pl.BlockSpec((None, tm, tn), lambda i, j, k: (j // out_per, i, j % out_per))
        out_shape = jax.ShapeDtypeStruct((N_CHIPS, m, n // N_CHIPS), out_dtype)
    else:
        out_spec = pl.BlockSpec((tm, tn), lambda i, j, k: (i, j))
        out_shape = jax.ShapeDtypeStruct((m, n), out_dtype)
    dims = {"nn": (((1,), (0,)), ((), ())), "nt": (((1,), (1,)), ((), ())), "tn": (((0,), (0,)), ((), ()))}[mode]

    n_after = 0 if after is None else 1

    def body(*refs):
        o_ref = refs[2 * n_pairs + n_after]
        k = pl.program_id(2)

        def prod(p):
            return lax.dot_general(refs[2 * p][...], refs[2 * p + 1][...], dims, preferred_element_type=F32)

        if nk == 1:
            o_ref[...] = prod(0).astype(o_ref.dtype)
            return
        acc = refs[2 * n_pairs + n_after + 1]
        for p in range(n_pairs):
            first, last = offs[p], offs[p] + nks[p] - 1
            lo, hi = max(first, 1), min(last, nk - 2)
            if first == 0:
                @pl.when(k == 0)
                def _(p=p):
                    acc[...] = prod(p)

            if lo <= hi:
                @pl.when((k >= lo) & (k <= hi))
                def _(p=p):
                    acc[...] += prod(p)

            if last == nk - 1:
                @pl.when(k == nk - 1)
                def _(p=p):
                    o_ref[...] = (acc[...] + prod(p)).astype(o_ref.dtype)

    flat = [t for pr in pairs for t in pr] + ([] if after is None else [after])
    return pl.pallas_call(
        body,
        name=name,
        grid=(m // tm, n // tn, nk),
        in_specs=in_specs + [pl.BlockSpec(memory_space=pl.ANY)] * n_after,
        out_specs=out_spec,
        out_shape=out_shape,
        scratch_shapes=[pltpu.VMEM((tm, tn), F32)] if nk > 1 else [],
        compiler_params=_params(("parallel", "parallel", "arbitrary")),
    )(*flat)


F4 = D_FF // N_CHIPS
FFN_TM = 512
FFN_STRIP = 256


def _ffn_up(hf, w_gu):
    t, d = hf.shape

    def body(a_ref, b_ref, gu_ref, act_ref):
        for r in range(0, FFN_TM, FFN_STRIP):
            acc = jnp.dot(a_ref[r:r + FFN_STRIP, :], b_ref[...], preferred_element_type=F32)
            g = acc[:, :F4]
            gu_ref[r:r + FFN_STRIP, :] = acc.astype(gu_ref.dtype)
            act_ref[r:r + FFN_STRIP, :] = (g * _sigmoid(g) * acc[:, F4:]).astype(act_ref.dtype)

    return pl.pallas_call(
        body,
        name="ffn_up",
        grid=(N_CHIPS, t // FFN_TM),
        in_specs=[pl.BlockSpec((FFN_TM, d), lambda q, i: (i, 0)),
                  pl.BlockSpec((None, d, 2 * F4), lambda q, i: (q, 0, 0))],
        out_specs=[pl.BlockSpec((FFN_TM, 2 * F4), lambda q, i: (i, q)),
                   pl.BlockSpec((FFN_TM, F4), lambda q, i: (i, q))],
        out_shape=[jax.ShapeDtypeStruct((t, 2 * D_FF), BF16), jax.ShapeDtypeStruct((t, D_FF), BF16)],
        compiler_params=_params(("parallel", "parallel")),
    )(hf, w_gu)


def _ffn_down_dx(d_ff, w_down, gu):
    t, d = d_ff.shape

    def body(a_ref, b_ref, gu_ref, o_ref):
        for r in range(0, FFN_TM, FFN_STRIP):
            rows = slice(r, r + FFN_STRIP)
            d_act = lax.dot_general(a_ref[rows, :], b_ref[...], NT_DIMS, preferred_element_type=F32)
            g = gu_ref[rows, :F4].astype(F32)
            u = gu_ref[rows, F4:].astype(F32)
            sg = _sigmoid(g)
            dsg = d_act * sg
            o_ref[rows, :F4] = (dsg * u * (1.0 + g - g * sg)).astype(o_ref.dtype)
            o_ref[rows, F4:] = (dsg * g).astype(o_ref.dtype)

    return pl.pallas_call(
        body,
        name="ffn_down_dx",
        grid=(N_CHIPS, t // FFN_TM),
        in_specs=[pl.BlockSpec((FFN_TM, d), lambda q, i: (i, 0)),
                  pl.BlockSpec((F4, d), lambda q, i: (q, 0)),
                  pl.BlockSpec((FFN_TM, 2 * F4), lambda q, i: (i, q))],
        out_specs=pl.BlockSpec((FFN_TM, 2 * F4), lambda q, i: (i, q)),
        out_shape=jax.ShapeDtypeStruct((t, 2 * D_FF), BF16),
        compiler_params=_params(("parallel", "parallel")),
    )(d_ff, w_down, gu)


def _rowwise(fn, row_ins, vec_ins, row_outs, acc_outs, name, after=None):
    t = row_ins[0].shape[0]
    tm = ROW_TILE
    n_in = len(row_ins) + len(vec_ins)
    n_row = len(row_outs)
    extra = [] if after is None else [after]

    def body(*refs):
        ins = [r[...] for r in refs[:n_in]]
        outs = refs[n_in + len(extra):]
        vals = fn(*ins)
        for r, v in zip(outs[:n_row], vals[:n_row]):
            r[...] = v.astype(r.dtype)
        if acc_outs:
            @pl.when(pl.program_id(0) == 0)
            def _():
                for r in outs[n_row:]:
                    r[...] = jnp.zeros_like(r)

            for r, v in zip(outs[n_row:], vals[n_row:]):
                r[...] += v

    in_specs = [pl.BlockSpec((tm, a.shape[1]), lambda i: (i, 0)) for a in row_ins]
    in_specs += [pl.BlockSpec(a.shape, lambda i: (0, 0)) for a in vec_ins]
    out_specs = [pl.BlockSpec((tm, c), lambda i: (i, 0)) for c, _ in row_outs]
    out_specs += [pl.BlockSpec((1, c), lambda i: (0, 0)) for c in acc_outs]
    out_shape = [jax.ShapeDtypeStruct((t, c), dt) for c, dt in row_outs]
    out_shape += [jax.ShapeDtypeStruct((1, c), F32) for c in acc_outs]
    return pl.pallas_call(
        body,
        name=name,
        grid=(t // tm,),
        in_specs=in_specs + [pl.BlockSpec(memory_space=pl.ANY)] * len(extra),
        out_specs=out_specs,
        out_shape=out_shape,
        compiler_params=_params(("arbitrary",)),
    )(*row_ins, *vec_ins, *extra)


def _mean(v):
    return jnp.mean(v, axis=-1, keepdims=True)


def _colsum(v):
    return jnp.sum(v, axis=0, keepdims=True)


def _rms_fwd(v, g):
    r = lax.rsqrt(_mean(v * v) + EPS)
    vhat = v * r
    return vhat * g, vhat, r


def _rms_bwd(dn, vhat, r, g):
    dng = dn * g
    return r * (dng - vhat * _mean(dng * vhat)), _colsum(dn * vhat)


def _swap_rope_halves(v):
    n = v.shape[-1]
    lane = lax.broadcasted_iota(jnp.int32, v.shape, v.ndim - 1)
    return jnp.where(lane % QK_ROPE < QK_ROPE // 2, pltpu.roll(v, n - QK_ROPE // 2, v.ndim - 1),
                     pltpu.roll(v, QK_ROPE // 2, v.ndim - 1))


def _rope(v, cos2, sin2):
    return v * cos2 + _swap_rope_halves(v) * sin2


def _rope_transposed(dv, cos2, sin2):
    return dv * cos2 + _swap_rope_halves(dv * sin2)


def _sigmoid(v):
    return 1.0 / (1.0 + jnp.exp(-v))


CONV_ROWS = 256


def _conv_fwd(ag, conv_w, conv_b):
    t = ag.shape[0]
    cb = LANES

    def body(ag_ref, w_ref, b_ref, o_ref, scr):
        a = ag_ref[:, :cb].astype(F32)
        g = ag_ref[:, cb:].astype(F32)
        scr[pl.ds(0, CONV_K_PAD), :] = jnp.zeros((CONV_K_PAD, cb), F32)
        scr[pl.ds(CONV_K_PAD, t), :] = a * _sigmoid(g)
        for r0 in range(0, t, CONV_ROWS):
            acc = jnp.zeros((CONV_ROWS, cb), F32) + b_ref[...]
            for k in range(CONV_K):
                acc = acc + w_ref[k:k + 1, :] * scr[pl.ds(r0 + CONV_K_PAD - (CONV_K - 1) + k, CONV_ROWS), :]
            o_ref[pl.ds(r0, CONV_ROWS), :] = acc

    return pl.pallas_call(
        body,
        name="conv_fwd",
        grid=(CONV_CH // cb,),
        in_specs=[pl.BlockSpec((t, 2 * cb), lambda j: (0, j)),
                  pl.BlockSpec((CONV_K_PAD, cb), lambda j: (0, j)),
                  pl.BlockSpec((1, cb), lambda j: (0, j))],
        out_specs=pl.BlockSpec((t, cb), lambda j: (0, j)),
        out_shape=jax.ShapeDtypeStruct((t, CONV_CH), F32),
        scratch_shapes=[pltpu.VMEM((t + CONV_K_PAD, cb), F32)],
        compiler_params=_params(("parallel",)),
    )(ag, conv_w, conv_b)


def _conv_bwd(d_u1, ag, conv_w):
    t = ag.shape[0]
    cb = LANES

    def body(du_ref, ag_ref, w_ref, dag_ref, dw_ref, db_ref, su, sd):
        a = ag_ref[:, :cb].astype(F32)
        g = ag_ref[:, cb:].astype(F32)
        sg = _sigmoid(g)
        su[pl.ds(0, CONV_K_PAD), :] = jnp.zeros((CONV_K_PAD, cb), F32)
        su[pl.ds(CONV_K_PAD, t), :] = a * sg
        sd[pl.ds(0, t), :] = du_ref[...]
        sd[pl.ds(t, CONV_K_PAD), :] = jnp.zeros((CONV_K_PAD, cb), F32)
        db_ref[...] = _colsum(du_ref[...])
        dw_ref[...] = jnp.zeros_like(dw_ref)
        for r0 in range(0, t, CONV_ROWS):
            du = sd[pl.ds(r0, CONV_ROWS), :]
            acc = jnp.zeros((CONV_ROWS, cb), F32)
            for k in range(CONV_K):
                acc = acc + w_ref[k:k + 1, :] * sd[pl.ds(r0 + (CONV_K - 1) - k, CONV_ROWS), :]
                dw_ref[k:k + 1, :] += _colsum(du * su[pl.ds(r0 + CONV_K_PAD - (CONV_K - 1) + k, CONV_ROWS), :])
            sgc = sg[r0:r0 + CONV_ROWS]
            ac = a[r0:r0 + CONV_ROWS]
            dag_ref[pl.ds(r0, CONV_ROWS), :cb] = (acc * sgc).astype(dag_ref.dtype)
            dag_ref[pl.ds(r0, CONV_ROWS), cb:] = (acc * ac * sgc * (1.0 - sgc)).astype(dag_ref.dtype)

    return pl.pallas_call(
        body,
        name="conv_bwd",
        grid=(CONV_CH // cb,),
        in_specs=[pl.BlockSpec((t, cb), lambda j: (0, j)),
                  pl.BlockSpec((t, 2 * cb), lambda j: (0, j)),
                  pl.BlockSpec((CONV_K_PAD, cb), lambda j: (0, j))],
        out_specs=[pl.BlockSpec((t, 2 * cb), lambda j: (0, j)),
                   pl.BlockSpec((CONV_K_PAD, cb), lambda j: (0, j)),
                   pl.BlockSpec((1, cb), lambda j: (0, j))],
        out_shape=[jax.ShapeDtypeStruct((t, 2 * CONV_CH), BF16),
                   jax.ShapeDtypeStruct((CONV_K_PAD, CONV_CH), F32),
                   jax.ShapeDtypeStruct((1, CONV_CH), F32)],
        scratch_shapes=[pltpu.VMEM((t + CONV_K_PAD, cb), F32), pltpu.VMEM((t + CONV_K_PAD, cb), F32)],
        compiler_params=_params(("parallel",)),
    )(d_u1, ag, conv_w)


ATT_TQ = 256
NEG = float(jnp.finfo(jnp.float32).min)
SCALE = QK_HEAD ** -0.5
NT_DIMS = (((1,), (1,)), ((), ()))
TN_DIMS = (((0,), (0,)), ((), ()))


def _att_weights(qf, kf, row0):
    s = lax.dot_general(qf, kf, NT_DIMS, preferred_element_type=F32)
    tq, t = s.shape
    qpos = row0 + lax.broadcasted_iota(jnp.int32, (tq, t), 0)
    kpos = lax.broadcasted_iota(jnp.int32, (tq, t), 1)
    s = jnp.where(kpos <= qpos, s, NEG)
    p = jnp.exp(s - jnp.max(s, axis=-1, keepdims=True))
    return p, 1.0 / jnp.sum(p, axis=-1, keepdims=True)


def _scaled_query(qn, roped_half):
    return jnp.concatenate([(qn.astype(F32) * SCALE).astype(BF16), (roped_half * SCALE).astype(BF16)], axis=1)


def _half_mask(shape, which):
    lane = lax.broadcasted_iota(jnp.int32, shape, len(shape) - 1)
    return (lane // QK_ROPE == which).astype(F32)


def _attention_fwd(q, kv, kpe2, cos2, sin2):
    t = q.shape[0]
    tq = ATT_TQ

    def body(qn_ref, qp_ref, c_ref, s_ref, kv_ref, kpe_ref, o_ref):
        roped = _rope(qp_ref[...].astype(F32), c_ref[...], s_ref[...])

        def block(i):
            keys = slice(0, (i + 1) * tq)
            for e in range(2):
                qf = _scaled_query(qn_ref[:, e * QK_NOPE:(e + 1) * QK_NOPE], roped * _half_mask(roped.shape, e))
                kf = jnp.concatenate([kv_ref[keys, e * 256:e * 256 + QK_NOPE], kpe_ref[keys, :]], axis=1)
                p, inv_l = _att_weights(qf, kf, i * tq)
                v = kv_ref[keys, e * 256 + QK_NOPE:(e + 1) * 256]
                o = jnp.dot(p.astype(BF16), v, preferred_element_type=F32) * inv_l
                o_ref[:, e * V_HEAD:(e + 1) * V_HEAD] = o.astype(o_ref.dtype)

        for i in range(t // tq):
            pl.when(pl.program_id(1) == i)(functools.partial(block, i))

    return pl.pallas_call(
        body,
        name="attention_fwd",
        grid=(N_HEADS // 2, t // tq),
        in_specs=[pl.BlockSpec((tq, 2 * QK_NOPE), lambda h, i: (i, h)),
                  pl.BlockSpec((tq, LANES), lambda h, i: (i, N_HEADS + h)),
                  pl.BlockSpec((tq, LANES), lambda h, i: (i, 0)),
                  pl.BlockSpec((tq, LANES), lambda h, i: (i, 0)),
                  pl.BlockSpec((t, 512), lambda h, i: (0, h)),
                  pl.BlockSpec((t, LANES), lambda h, i: (0, 0))],
        out_specs=pl.BlockSpec((tq, 2 * V_HEAD), lambda h, i: (i, h)),
        out_shape=jax.ShapeDtypeStruct((t, ATTN_CH), BF16),
        compiler_params=_params(("parallel", "parallel")),
    )(q, q, cos2, sin2, kv, kpe2)


def _attention_bwd(q, kv, kpe2, cos2, sin2, d_attn):
    t = q.shape[0]
    tq = ATT_TQ
    n_q = t // tq

    def body(qn_ref, qp_ref, c_ref, s_ref, kv_ref, kpe_ref, do_ref, dqn_ref, dqp_ref, dkv_ref, dkpe_ref, dkv_acc):
        h, i = pl.program_id(0), pl.program_id(1)

        @pl.when(i == 0)
        def _():
            dkv_acc[...] = jnp.zeros_like(dkv_acc)

        @pl.when((i == 0) & (h == 0))
        def _():
            dkpe_ref[...] = jnp.zeros_like(dkpe_ref)

        roped = _rope(qp_ref[...].astype(F32), c_ref[...], s_ref[...])

        def block(ib):
            keys = slice(0, (ib + 1) * tq)
            d_roped = jnp.zeros((tq, LANES), F32)
            for e in range(2):
                mask = _half_mask(roped.shape, e)
                qf = _scaled_query(qn_ref[:, e * QK_NOPE:(e + 1) * QK_NOPE], roped * mask)
                kf = jnp.concatenate([kv_ref[keys, e * 256:e * 256 + QK_NOPE], kpe_ref[keys, :]], axis=1)
                v = kv_ref[keys, e * 256 + QK_NOPE:(e + 1) * 256]
                do = do_ref[:, e * V_HEAD:(e + 1) * V_HEAD]
                p, inv_l = _att_weights(qf, kf, ib * tq)
                p = p * inv_l
                dp = lax.dot_general(do, v, NT_DIMS, preferred_element_type=F32)
                ds = (p * (dp - jnp.sum(p * dp, axis=-1, keepdims=True))).astype(BF16)
                dqf = jnp.dot(ds, kf, preferred_element_type=F32) * SCALE
                dkf = lax.dot_general(ds, qf, TN_DIMS, preferred_element_type=F32)
                dv = lax.dot_general(p.astype(BF16), do, TN_DIMS, preferred_element_type=F32)
                dqn_ref[:, e * QK_NOPE:(e + 1) * QK_NOPE] = dqf[:, :QK_NOPE].astype(dqn_ref.dtype)
                d_roped = d_roped + dqf[:, QK_NOPE:] * mask
                dkv_acc[keys, e * 256:e * 256 + QK_NOPE] += dkf[:, :QK_NOPE]
                dkv_acc[keys, e * 256 + QK_NOPE:(e + 1) * 256] += dv
                dkpe_ref[keys, :] += dkf[:, QK_NOPE:]
            dqp_ref[...] = _rope_transposed(d_roped, c_ref[...], s_ref[...]).astype(dqp_ref.dtype)

        for ib in range(n_q):
            pl.when(i == ib)(functools.partial(block, ib))

        @pl.when(i == n_q - 1)
        def _():
            dkv_ref[...] = dkv_acc[...].astype(dkv_ref.dtype)

    return pl.pallas_call(
        body,
        name="attention_bwd",
        grid=(N_HEADS // 2, n_q),
        in_specs=[pl.BlockSpec((tq, 2 * QK_NOPE), lambda h, i: (i, h)),
                  pl.BlockSpec((tq, LANES), lambda h, i: (i, N_HEADS + h)),
                  pl.BlockSpec((tq, LANES), lambda h, i: (i, 0)),
                  pl.BlockSpec((tq, LANES), lambda h, i: (i, 0)),
                  pl.BlockSpec((t, 512), lambda h, i: (0, h)),
                  pl.BlockSpec((t, LANES), lambda h, i: (0, 0)),
                  pl.BlockSpec((tq, 2 * V_HEAD), lambda h, i: (i, h))],
        out_specs=[pl.BlockSpec((tq, 2 * QK_NOPE), lambda h, i: (i, h)),
                   pl.BlockSpec((tq, LANES), lambda h, i: (i, h)),
                   pl.BlockSpec((t, 512), lambda h, i: (0, h)),
                   pl.BlockSpec((t, LANES), lambda h, i: (0, 0))],
        out_shape=[jax.ShapeDtypeStruct((t, N_HEADS * QK_NOPE), BF16),
                   jax.ShapeDtypeStruct((t, N_HEADS * QK_ROPE), BF16),
                   jax.ShapeDtypeStruct((t, N_HEADS * 256), BF16),
                   jax.ShapeDtypeStruct((t, LANES), F32)],
        scratch_shapes=[pltpu.VMEM((t, 512), F32)],
        compiler_params=_params(("arbitrary", "arbitrary")),
    )(q, q, cos2, sin2, kv, kpe2, d_attn)


def _local_step(x, target, cos2, sin2, vec, w, ffn):
    d = D_MODEL

    (h,) = _rowwise(lambda xv, g: (_rms_fwd(xv, g)[0],), [x], [vec["pre_mix_norm"]], [(d, BF16)], [], "pre_mix_norm_fwd")
    ag = _matmul([(h, w["w_ag"])], "nn", BF16, "in_proj_ag")
    z2 = _matmul([(h, w["w_z2"])], "nn", BF16, "in_proj_z2")
    w = {**w, **ffn["mixer_rest"](z2)}
    u1 = _conv_fwd(ag, w["conv_w"], vec["conv_b"])

    def latents_fwd(z, c2, s2, qg, kvg):
        z = z.astype(F32)
        qn = _rms_fwd(z[:, :Q_LORA], qg)[0]
        kvn = _rms_fwd(z[:, Q_LORA:Q_LORA + KV_LORA], kvg)[0]
        kr = z[:, Q_LORA + KV_LORA:]
        kr2 = kr + pltpu.roll(kr, QK_ROPE, 1)
        return qn, kvn, _rope(kr2, c2, s2)

    qn, kvn, kpe2 = _rowwise(latents_fwd, [z2, cos2, sin2], [vec["q_norm"], vec["kv_norm"]],
                             [(Q_LORA, BF16), (KV_LORA, BF16), (LANES, BF16)], [], "latents_fwd")
    q = _matmul([(qn, w["w_uq"])], "nn", BF16, "q_up")
    kv = _matmul([(kvn, w["w_ukv"])], "nn", BF16, "kv_up")
    attn = _attention_fwd(q, kv, kpe2, cos2, sin2)

    def conv_post(u, lg, lb):
        mu = _mean(u)
        uc = u - mu
        rstd = lax.rsqrt(_mean(uc * uc) + EPS)
        uhat = uc * rstd
        u2 = uhat * lg + lb
        sg = _sigmoid(u2)
        return uhat, rstd, u2, sg, u2 * sg

    def mix_in_fwd(u, at, lg, lb, cg, ag_):
        u3 = conv_post(u, lg, lb)[4]
        cn = _rms_fwd(u3, cg)[0]
        an = _rms_fwd(at.astype(F32), ag_)[0]
        return (jnp.concatenate([cn, an], axis=1),)

    (cat,) = _rowwise(mix_in_fwd, [u1, attn], [vec["conv_ln_g"], vec["conv_ln_b"], vec["conv_out_norm"], vec["attn_out_norm"]],
                      [(2 * CONV_CH, BF16)], [], "mix_in_fwd")
    mix = _matmul([(cat, w["w_out"])], "nn", F32, "out_proj")
    landed = ffn["w_gu_landed"](mix)

    def residual1(xv, mv, gpm, gpf):
        x1 = xv + _rms_fwd(mv, gpm)[0]
        return x1, _rms_fwd(x1, gpf)[0]

    x1, hf = _rowwise(residual1, [x, mix], [vec["post_mix_norm"], vec["pre_ffn_norm"]], [(d, F32), (d, BF16)], [],
                      "residual1_fwd", after=landed)
    w_gu = ffn["w_gu"](hf)
    gu, act = _ffn_up(hf, w_gu)
    w_down = ffn["w_down"](act)
    ff = _matmul([(act, w_down)], "nn", F32, "ffn_down")

    def loss_head(x1v, ffv, tg, g):
        n, fhat, r = _rms_fwd(ffv, g)
        err = x1v + n - tg
        loss = 0.5 * jnp.sum(_mean(err * err), axis=0, keepdims=True)
        dy = err * (1.0 / d)
        d_ff, dg = _rms_bwd(dy, fhat, r, g)
        return dy, d_ff, dg, jnp.broadcast_to(loss, (1, LANES))

    dy, d_ff, g_post_ffn, loss = _rowwise(loss_head, [x1, ff, target], [vec["post_ffn_norm"]],
                                          [(d, F32), (d, BF16)], [d, LANES], "loss_head")
    d_gu = _ffn_down_dx(d_ff, w_down, gu)
    dw_down = _matmul([(act, d_ff)], "tn", BF16, "ffn_down_dw", tiles=(F4, None, None))
    started = ffn["dw_down"](dw_down)
    dw_gu = _matmul([(hf, d_gu)], "tn", BF16, "ffn_gate_up_dw", out_parts=True, tiles=(None, F4, None), after=started)
    started = ffn["dw_gu"](dw_gu)
    d_hf = _matmul([(d_gu, w_gu)], "nt", F32, "ffn_gate_up_dx", b_parts="k", after=started)
    started = ffn["grads_exchanged"](d_hf)

    def residual1_bwd(dyv, dhf, x1v, mv, gpf, gpm):
        _, x1hat, r1 = _rms_fwd(x1v, gpf)
        dn, dgpf = _rms_bwd(dhf, x1hat, r1, gpf)
        d_x1 = dyv + dn
        _, mhat, rm = _rms_fwd(mv, gpm)
        d_mix, dgpm = _rms_bwd(d_x1, mhat, rm, gpm)
        return d_x1, d_mix, dgpf, dgpm

    d_x1, d_mix, g_pre_ffn, g_post_mix = _rowwise(residual1_bwd, [dy, d_hf, x1, mix], [vec["pre_ffn_norm"], vec["post_mix_norm"]],
                                                  [(d, F32), (d, BF16)], [d, d], "residual1_bwd", after=started)
    d_cat = _matmul([(d_mix, w["w_out"])], "nt", BF16, "out_proj_dx")
    dw_out = _matmul([(cat, d_mix)], "tn", BF16, "out_proj_dw")

    def mix_in_bwd(dc, u, at, lg, lb, cg, ag_):
        dc = dc.astype(F32)
        uhat, rstd, u2, sg, u3 = conv_post(u, lg, lb)
        _, u3hat, rc = _rms_fwd(u3, cg)
        d_u3, dcg = _rms_bwd(dc[:, :CONV_CH], u3hat, rc, cg)
        d_u2 = d_u3 * sg * (1.0 + u2 * (1.0 - sg))
        dgl = d_u2 * lg
        d_u1 = rstd * (dgl - _mean(dgl) - uhat * _mean(dgl * uhat))
        _, ahat, ra = _rms_fwd(at.astype(F32), ag_)
        d_at, dag = _rms_bwd(dc[:, CONV_CH:], ahat, ra, ag_)
        return d_u1, d_at, dcg, _colsum(d_u2 * uhat), _colsum(d_u2), dag

    d_u1, d_attn, g_conv_out, g_ln_g, g_ln_b, g_attn_out = _rowwise(
        mix_in_bwd, [d_cat, u1, attn], [vec["conv_ln_g"], vec["conv_ln_b"], vec["conv_out_norm"], vec["attn_out_norm"]],
        [(CONV_CH, F32), (ATTN_CH, BF16)], [CONV_CH] * 4, "mix_in_bwd")
    d_ag, d_conv_w, g_conv_b = _conv_bwd(d_u1, ag, w["conv_w"])
    d_qn_, d_qp_, d_kv, d_kpe2 = _attention_bwd(q, kv, kpe2, cos2, sin2, d_attn)
    d_q = jnp.concatenate([d_qn_, d_qp_], axis=1)
    d_qn = _matmul([(d_q, w["w_uq"])], "nt", BF16, "q_up_dx")
    dw_uq = _matmul([(qn, d_q)], "tn", BF16, "q_up_dw")
    d_kvn = _matmul([(d_kv, w["w_ukv"])], "nt", BF16, "kv_up_dx")
    dw_ukv = _matmul([(kvn, d_kv)], "tn", BF16, "kv_up_dw")

    def latents_bwd(z, dq, dk, dkp, c2, s2, qg, kvg):
        z = z.astype(F32)
        _, qhat, rq = _rms_fwd(z[:, :Q_LORA], qg)
        d_ql, dqg = _rms_bwd(dq.astype(F32), qhat, rq, qg)
        _, khat, rk = _rms_fwd(z[:, Q_LORA:Q_LORA + KV_LORA], kvg)
        d_kl, dkg = _rms_bwd(dk.astype(F32), khat, rk, kvg)
        both = dkp + pltpu.roll(dkp, QK_ROPE, 1)
        d_kr = _rope_transposed(both, c2, s2) * _half_mask(both.shape, 0)
        return jnp.concatenate([d_ql, d_kl, d_kr], axis=1), dqg, dkg

    d_z2, g_q_norm, g_kv_norm = _rowwise(latents_bwd, [z2, d_qn, d_kvn, d_kpe2, cos2, sin2], [vec["q_norm"], vec["kv_norm"]],
                                         [(Z2_COLS, BF16)], [Q_LORA, KV_LORA], "latents_bwd")
    d_h = _matmul([(d_ag, w["w_ag"]), (d_z2, w["w_z2"])], "nt", F32, "in_proj_dx")
    dw_ag = _matmul([(h, d_ag)], "tn", BF16, "in_proj_ag_dw")
    dw_z2 = _matmul([(h, d_z2)], "tn", BF16, "in_proj_z2_dw")

    def pre_mix_bwd(dx1, dh, xv, g):
        _, xhat, r = _rms_fwd(xv, g)
        dn, dg = _rms_bwd(dh, xhat, r, g)
        return dx1 + dn, dg

    grad_x, g_pre_mix = _rowwise(pre_mix_bwd, [d_x1, d_h, x], [vec["pre_mix_norm"]], [(d, F32)], [d], "pre_mix_norm_bwd")

    dw = dict(w_ag=dw_ag, w_z2=dw_z2, w_uq=dw_uq, w_ukv=dw_ukv, conv_w=d_conv_w, w_out=dw_out, w_gu=dw_gu, w_down=dw_down)
    dvec = dict(pre_mix_norm=g_pre_mix, q_norm=g_q_norm, kv_norm=g_kv_norm, conv_b=g_conv_b, conv_ln_g=g_ln_g,
                conv_ln_b=g_ln_b, conv_out_norm=g_conv_out, attn_out_norm=g_attn_out, post_mix_norm=g_post_mix,
                pre_ffn_norm=g_pre_ffn, post_ffn_norm=g_post_ffn)
    return loss, grad_x, dw, dvec


ANY = pl.BlockSpec(memory_space=pl.ANY)


def _place():
    x, y, c = lax.axis_index("x"), lax.axis_index("y"), lax.axis_index("c")
    chips = [(1 - x, y), (x, 1 - y), (1 - x, 1 - y)]
    return x, y, c, chips


def _to_parts(chip, pieces, dtype, name, n_parts=N_CHIPS):
    r = pieces[0].shape[0]
    widths = [a.shape[1] for a in pieces]
    tr = r if r <= 512 else _first_divisor(r, (512, 256, 128))

    def body(p_ref, *refs):
        o_ref = refs[len(pieces)]
        off = 0
        for a_ref, wdt in zip(refs, widths):
            o_ref[:, off:off + wdt] = a_ref[...].astype(o_ref.dtype)
            off += wdt

    return pl.pallas_call(
        body,
        name=name,
        grid_spec=pltpu.PrefetchScalarGridSpec(
            num_scalar_prefetch=1,
            grid=(r // tr,),
            in_specs=[pl.BlockSpec((tr, wdt), lambda i, p_ref: (i, 0)) for wdt in widths],
            out_specs=pl.BlockSpec((None, tr, sum(widths)), lambda i, p_ref: (p_ref[0], i, 0))),
        out_shape=jax.ShapeDtypeStruct((n_parts, r, sum(widths)), dtype),
        compiler_params=_params(("parallel",)),
    )(chip, *pieces)


HBM = pl.BlockSpec(memory_space=pltpu.HBM)
SEM = pl.BlockSpec(memory_space=pltpu.SEMAPHORE)
EFFECT = pltpu.SideEffectType.DATAFLOW_SIDE_EFFECTING
VMEM_SPEC = pl.BlockSpec(memory_space=pltpu.VMEM)
TOKEN = jax.ShapeDtypeStruct((8, LANES), F32)


def _in_hbm(a):
    return pltpu.with_memory_space_constraint(a, pltpu.HBM)


def _gather_rows(buf, whole, half):
    r = buf.shape[1]
    return pl.ds(0, r) if whole else pl.ds(half * (r // 2), r // 2)


def _gather_start(bufs, whole, groups, name, after=()):
    n = len(bufs)
    n_g = len(groups)

    def body(*refs):
        refs = refs[:n] + refs[n + len(after):]
        sems = refs[n:n + 2 * n_g]
        outs = refs[n + 2 * n_g:2 * n + 2 * n_g]
        token = refs[2 * n + 2 * n_g]
        token[...] = jnp.zeros_like(token)
        x, y, c, chips = _place()
        p = 2 * x + y
        for gi, group in enumerate(groups):
            for ki, k in enumerate(group):
                blk = outs[k].at[p, _gather_rows(bufs[k], whole[k], c), :]
                for j, (px, py) in enumerate(chips):
                    pltpu.make_async_remote_copy(src_ref=blk, dst_ref=blk, send_sem=sems[2 * gi].at[3 * ki + j],
                                                 recv_sem=sems[2 * gi + 1].at[3 * ki + j],
                                                 device_id=(px, py, c), device_id_type=MESH).start()

    sem_shapes = []
    for group in groups:
        sem_shapes += [pltpu.SemaphoreType.DMA((3 * len(group),))] * 2
    res = pl.pallas_call(
        body,
        name=name,
        in_specs=[HBM] * n + [ANY] * len(after),
        out_specs=[SEM] * (2 * n_g) + [HBM] * n + [VMEM_SPEC],
        out_shape=sem_shapes + [pltpu.HBM(a.shape, a.dtype) for a in bufs] + [TOKEN],
        input_output_aliases={k: 2 * n_g + k for k in range(n)},
        compiler_params=pltpu.CompilerParams(has_side_effects=EFFECT),
    )(*[_in_hbm(a) for a in bufs], *after)
    sems = [(res[2 * gi], res[2 * gi + 1]) for gi in range(n_g)]
    return sems, list(res[2 * n_g:2 * n_g + n]), res[2 * n_g + n]


def _gather_wait(bufs, whole, send, recv, after, name):
    n = len(bufs)

    def body(*refs):
        ins = refs[:n]
        send_ref, recv_ref = refs[n], refs[n + 1]
        x, y, c, chips = _place()
        p = 2 * x + y
        for ki in range(n):
            rows = _gather_rows(bufs[ki], whole[ki], c)
            for j, (px, py) in enumerate(chips):
                cp = pltpu.make_async_remote_copy(src_ref=ins[ki].at[p, rows, :], dst_ref=ins[ki].at[2 * px + py, rows, :],
                                                  send_sem=send_ref.at[3 * ki + j], recv_sem=recv_ref.at[3 * ki + j],
                                                  device_id=(px, py, c), device_id_type=MESH)
                cp.wait_send()
                cp.wait_recv()

    res = pl.pallas_call(
        body,
        name=name,
        in_specs=[HBM] * n + [SEM, SEM] + [ANY] * len(after),
        out_specs=[HBM] * n,
        out_shape=[pltpu.HBM(a.shape, a.dtype) for a in bufs],
        input_output_aliases={k: k for k in range(n)},
        compiler_params=pltpu.CompilerParams(has_side_effects=EFFECT),
    )(*bufs, send, recv, *after)
    return list(res)


def _gather_hand_on(bufs, name, after=()):
    n = len(bufs)

    def body(*refs):
        refs = refs[n + len(after):]
        outs = refs[:n]
        send, recv = refs[n:]
        x, y, c, chips = _place()

        def d2d(k, j, half):
            px, py = chips[j]
            blk = outs[k].at[2 * px + py, _gather_rows(bufs[k], False, half), :]
            return pltpu.make_async_remote_copy(src_ref=blk, dst_ref=blk, send_sem=send.at[3 * k + j], recv_sem=recv.at[3 * k + j],
                                                device_id=(x, y, 1 - c), device_id_type=MESH)

        sent = [d2d(k, j, c) for k in range(n) for j in range(3)]
        for cp in sent:
            cp.start()
        for k in range(n):
            for j in range(3):
                d2d(k, j, 1 - c).wait_recv()
        for cp in sent:
            cp.wait_send()

    return pl.pallas_call(
        body,
        name=name,
        in_specs=[ANY] * (n + len(after)),
        out_specs=[ANY] * n,
        out_shape=[jax.ShapeDtypeStruct(a.shape, a.dtype) for a in bufs],
        input_output_aliases={k: k for k in range(n)},
        scratch_shapes=[pltpu.SemaphoreType.DMA((3 * n,)), pltpu.SemaphoreType.DMA((3 * n,))],
        compiler_params=pltpu.CompilerParams(has_side_effects=True),
    )(*bufs, *after)


def _hand_on_start(bufs, name, after=()):
    n = len(bufs)

    def body(*refs):
        refs = refs[:n] + refs[n + len(after):]
        send, recv = refs[n], refs[n + 1]
        outs = refs[n + 2:2 * n + 2]
        refs[2 * n + 2][...] = jnp.zeros(TOKEN.shape, TOKEN.dtype)
        x, y, c, chips = _place()
        for k in range(n):
            for j, (px, py) in enumerate(chips):
                blk = outs[k].at[2 * px + py, _gather_rows(bufs[k], False, c), :]
                pltpu.make_async_remote_copy(src_ref=blk, dst_ref=blk, send_sem=send.at[3 * k + j], recv_sem=recv.at[3 * k + j],
                                             device_id=(x, y, 1 - c), device_id_type=MESH).start()

    res = pl.pallas_call(
        body,
        name=name,
        in_specs=[HBM] * n + [ANY] * len(after),
        out_specs=[SEM, SEM] + [HBM] * n + [VMEM_SPEC],
        out_shape=[pltpu.SemaphoreType.DMA((3 * n,))] * 2 + [pltpu.HBM(a.shape, a.dtype) for a in bufs] + [TOKEN],
        input_output_aliases={k: 2 + k for k in range(n)},
        compiler_params=pltpu.CompilerParams(has_side_effects=EFFECT),
    )(*[_in_hbm(a) for a in bufs], *after)
    return res[0], res[1], list(res[2:2 + n]), res[2 + n]


def _hand_on_wait(bufs, send, recv, after, name):
    n = len(bufs)

    def body(*refs):
        ins = refs[:n]
        send_ref, recv_ref = refs[n], refs[n + 1]
        x, y, c, chips = _place()
        for k in range(n):
            for j, (px, py) in enumerate(chips):
                q = 2 * px + py
                cp = pltpu.make_async_remote_copy(src_ref=ins[k].at[q, _gather_rows(bufs[k], False, c), :],
                                                  dst_ref=ins[k].at[q, _gather_rows(bufs[k], False, 1 - c), :],
                                                  send_sem=send_ref.at[3 * k + j], recv_sem=recv_ref.at[3 * k + j],
                                                  device_id=(x, y, 1 - c), device_id_type=MESH)
                cp.wait_send()
                cp.wait_recv()

    res = pl.pallas_call(
        body,
        name=name,
        in_specs=[HBM] * n + [SEM, SEM] + [ANY] * len(after),
        out_specs=[HBM] * n,
        out_shape=[pltpu.HBM(a.shape, a.dtype) for a in bufs],
        input_output_aliases={k: k for k in range(n)},
        compiler_params=pltpu.CompilerParams(has_side_effects=EFFECT),
    )(*bufs, send, recv, *after)
    return list(res)


def _pair_exchange_start(part, name):
    rh = part.shape[1] // 2
    land_shape = (N_CHIPS, rh, part.shape[2])

    def body(part_ref, land_ref, send, recv, part_out, land_out, token):
        token[...] = jnp.zeros_like(token)
        x, y, c, _ = _place()
        pltpu.make_async_remote_copy(src_ref=part_out.at[:, pl.ds((1 - c) * rh, rh), :], dst_ref=land_out,
                                     send_sem=send, recv_sem=recv, device_id=(x, y, 1 - c), device_id_type=MESH).start()

    res = pl.pallas_call(
        body,
        name=name,
        in_specs=[HBM, HBM],
        out_specs=[SEM, SEM, HBM, HBM, VMEM_SPEC],
        out_shape=[pltpu.SemaphoreType.DMA(()), pltpu.SemaphoreType.DMA(()), pltpu.HBM(part.shape, part.dtype),
                   pltpu.HBM(land_shape, part.dtype), TOKEN],
        input_output_aliases={0: 2, 1: 3},
        compiler_params=pltpu.CompilerParams(has_side_effects=EFFECT),
    )(_in_hbm(part), _in_hbm(lax.empty(land_shape, part.dtype)))
    return res


def _pair_exchange_wait(send, recv, part, land, after, name):
    rh = part.shape[1] // 2

    def body(part_ref, land_ref, send_ref, recv_ref, after_ref, part_out, land_out):
        x, y, c, _ = _place()
        cp = pltpu.make_async_remote_copy(src_ref=part_ref.at[:, pl.ds((1 - c) * rh, rh), :], dst_ref=land_ref,
                                          send_sem=send_ref, recv_sem=recv_ref, device_id=(x, y, 1 - c), device_id_type=MESH)
        cp.wait_send()
        cp.wait_recv()

    return pl.pallas_call(
        body,
        name=name,
        in_specs=[HBM, HBM, SEM, SEM, ANY],
        out_specs=[HBM, HBM],
        out_shape=[pltpu.HBM(part.shape, part.dtype), pltpu.HBM(land.shape, land.dtype)],
        input_output_aliases={0: 0, 1: 1},
        compiler_params=pltpu.CompilerParams(has_side_effects=EFFECT),
    )(part, land, send, recv, after)


def _chip_exchange_start(sums, name):
    n = len(sums)

    def body(*refs):
        send, recv = refs[2 * n], refs[2 * n + 1]
        src = refs[2 * n + 2:3 * n + 2]
        dst = refs[3 * n + 2:4 * n + 2]
        refs[4 * n + 2][...] = jnp.zeros(TOKEN.shape, TOKEN.dtype)
        x, y, c, chips = _place()
        p = 2 * x + y
        for k in range(n):
            for j, (px, py) in enumerate(chips):
                pltpu.make_async_remote_copy(src_ref=src[k].at[2 * px + py], dst_ref=dst[k].at[p],
                                             send_sem=send.at[3 * k + j], recv_sem=recv.at[3 * k + j],
                                             device_id=(px, py, c), device_id_type=MESH).start()

    res = pl.pallas_call(
        body,
        name=name,
        in_specs=[HBM] * (2 * n),
        out_specs=[SEM, SEM] + [HBM] * (2 * n) + [VMEM_SPEC],
        out_shape=[pltpu.SemaphoreType.DMA((3 * n,))] * 2 + [pltpu.HBM(a.shape, a.dtype) for a in sums] * 2 + [TOKEN],
        input_output_aliases={k: 2 + k for k in range(2 * n)},
        compiler_params=pltpu.CompilerParams(has_side_effects=EFFECT),
    )(*[_in_hbm(a) for a in sums], *[_in_hbm(lax.empty(a.shape, a.dtype)) for a in sums])
    return res[0], res[1], list(res[2:2 + n]), list(res[2 + n:2 + 2 * n]), res[2 + 2 * n]


def _chip_exchange_wait(sums, slots, send, recv, after, name):
    n = len(sums)

    def body(*refs):
        src, dst = refs[:n], refs[n:2 * n]
        send_ref, recv_ref = refs[2 * n], refs[2 * n + 1]
        x, y, c, chips = _place()
        for k in range(n):
            for j, (px, py) in enumerate(chips):
                cp = pltpu.make_async_remote_copy(src_ref=src[k].at[2 * px + py], dst_ref=dst[k].at[2 * px + py],
                                                  send_sem=send_ref.at[3 * k + j], recv_sem=recv_ref.at[3 * k + j],
                                                  device_id=(px, py, c), device_id_type=MESH)
                cp.wait_send()
                cp.wait_recv()

    res = pl.pallas_call(
        body,
        name=name,
        in_specs=[HBM] * (2 * n) + [SEM, SEM] + [ANY] * len(after),
        out_specs=[HBM] * (2 * n),
        out_shape=[pltpu.HBM(a.shape, a.dtype) for a in sums] * 2,
        input_output_aliases={k: k for k in range(2 * n)},
        compiler_params=pltpu.CompilerParams(has_side_effects=EFFECT),
    )(*sums, *slots, send, recv, *after)
    return list(res[:n]), list(res[n:])


def _pair_exchange(parts, name):
    n = len(parts)

    def body(*refs):
        ins, outs = refs[:n], refs[n:2 * n]
        send, recv = refs[2 * n:]
        x, y, c, _ = _place()
        copies = []
        for k in range(n):
            rh = parts[k].shape[1] // 2
            cp = pltpu.make_async_remote_copy(
                src_ref=ins[k].at[:, pl.ds((1 - c) * rh, rh), :], dst_ref=outs[k],
                send_sem=send.at[k], recv_sem=recv.at[k], device_id=(x, y, 1 - c), device_id_type=MESH)
            cp.start()
            copies.append(cp)
        for cp in copies:
            cp.wait()

    return pl.pallas_call(
        body,
        name=name,
        in_specs=[ANY] * n,
        out_specs=[ANY] * n,
        out_shape=[jax.ShapeDtypeStruct((N_CHIPS, a.shape[1] // 2, a.shape[2]), a.dtype) for a in parts],
        scratch_shapes=[pltpu.SemaphoreType.DMA((n,)), pltpu.SemaphoreType.DMA((n,))],
        compiler_params=pltpu.CompilerParams(has_side_effects=True),
    )(*parts)


def _pair_sum(core, part, landed, name):
    _, r, cdim = part.shape
    rh = r // 2
    tr = _row_tile(rh, cdim, 2)
    nb = rh // tr

    def body(c_ref, a_ref, b_ref, o_ref):
        o_ref[...] = (a_ref[...].astype(F32) + b_ref[...].astype(F32)).astype(o_ref.dtype)

    return pl.pallas_call(
        body,
        name=name,
        grid_spec=pltpu.PrefetchScalarGridSpec(
            num_scalar_prefetch=1,
            grid=(N_CHIPS, nb),
            in_specs=[pl.BlockSpec((None, tr, cdim), lambda q, i, c_ref: (q, c_ref[0] * nb + i, 0)),
                      pl.BlockSpec((None, tr, cdim), lambda q, i, c_ref: (q, i, 0))],
            out_specs=pl.BlockSpec((None, tr, cdim), lambda q, i, c_ref: (q, i, 0))),
        out_shape=jax.ShapeDtypeStruct((N_CHIPS, rh, cdim), BF16),
        compiler_params=_params(("parallel", "parallel")),
    )(core, part, landed)


def _chip_sum(place, own, slots, name, after=()):
    _, rh, cdim = slots.shape
    tr = _row_tile(rh, cdim, 2)
    nb = rh // tr

    def body(place_ref, own_ref, s1_ref, s2_ref, s3_ref, *rest):
        o_ref = rest[len(after)]
        acc = own_ref[...].astype(F32)
        for s_ref in (s1_ref, s2_ref, s3_ref):
            acc = acc + s_ref[...].astype(F32)
        o_ref[...] = acc

    def other(j):
        return lambda i, place_ref: ((place_ref[0] + j) % N_CHIPS, i, 0)

    return pl.pallas_call(
        body,
        name=name,
        grid_spec=pltpu.PrefetchScalarGridSpec(
            num_scalar_prefetch=1,
            grid=(nb,),
            in_specs=[pl.BlockSpec((None, tr, cdim), other(j)) for j in (0, 1, 2, 3)] + [ANY] * len(after),
            out_specs=pl.BlockSpec((tr, cdim), lambda i, place_ref: (place_ref[1] * nb + i, 0))),
        out_shape=jax.ShapeDtypeStruct((2 * rh, cdim), F32),
        compiler_params=_params(("parallel",)),
    )(place, own, slots, slots, slots, *after)


def _half_exchange_start(buf, name):
    rh = buf.shape[0] // 2

    def body(buf_ref, send, recv, out_ref, token):
        token[...] = jnp.zeros_like(token)
        x, y, c, _ = _place()
        mine = out_ref.at[pl.ds(c * rh, rh), :]
        pltpu.make_async_remote_copy(src_ref=mine, dst_ref=mine, send_sem=send, recv_sem=recv,
                                     device_id=(x, y, 1 - c), device_id_type=MESH).start()

    return pl.pallas_call(
        body,
        name=name,
        in_specs=[HBM],
        out_specs=[SEM, SEM, HBM, VMEM_SPEC],
        out_shape=[pltpu.SemaphoreType.DMA(()), pltpu.SemaphoreType.DMA(()), pltpu.HBM(buf.shape, buf.dtype), TOKEN],
        input_output_aliases={0: 2},
        compiler_params=pltpu.CompilerParams(has_side_effects=EFFECT),
    )(_in_hbm(buf))


def _half_exchange_wait(send, recv, buf, after, name):
    rh = buf.shape[0] // 2

    def body(buf_ref, send_ref, recv_ref, *rest):
        x, y, c, _ = _place()
        cp = pltpu.make_async_remote_copy(src_ref=buf_ref.at[pl.ds(c * rh, rh), :], dst_ref=buf_ref.at[pl.ds((1 - c) * rh, rh), :],
                                          send_sem=send_ref, recv_sem=recv_ref, device_id=(x, y, 1 - c), device_id_type=MESH)
        cp.wait_send()
        cp.wait_recv()

    return pl.pallas_call(
        body,
        name=name,
        in_specs=[HBM, SEM, SEM] + [ANY] * len(after),
        out_specs=HBM,
        out_shape=pltpu.HBM(buf.shape, buf.dtype),
        input_output_aliases={0: 0},
        compiler_params=pltpu.CompilerParams(has_side_effects=EFFECT),
    )(buf, send, recv, *after)


def _half_exchange(bufs, name):
    n = len(bufs)

    def body(*refs):
        outs = refs[n:2 * n]
        send, recv = refs[2 * n:]
        x, y, c, _ = _place()
        copies = []
        for k in range(n):
            rh = bufs[k].shape[0] // 2
            mine = outs[k].at[pl.ds(c * rh, rh), :]
            cp = pltpu.make_async_remote_copy(src_ref=mine, dst_ref=mine, send_sem=send.at[k], recv_sem=recv.at[k],
                                              device_id=(x, y, 1 - c), device_id_type=MESH)
            cp.start()
            copies.append(cp)
        for k in range(n):
            rh = bufs[k].shape[0] // 2
            theirs = outs[k].at[pl.ds((1 - c) * rh, rh), :]
            copies[k].wait_send()
            pltpu.make_async_remote_copy(src_ref=theirs, dst_ref=theirs, send_sem=send.at[k], recv_sem=recv.at[k],
                                         device_id=(x, y, 1 - c), device_id_type=MESH).wait_recv()

    return pl.pallas_call(
        body,
        name=name,
        in_specs=[ANY] * n,
        out_specs=[ANY] * n,
        out_shape=[jax.ShapeDtypeStruct(a.shape, a.dtype) for a in bufs],
        input_output_aliases={k: k for k in range(n)},
        scratch_shapes=[pltpu.SemaphoreType.DMA((n,)), pltpu.SemaphoreType.DMA((n,))],
        compiler_params=pltpu.CompilerParams(has_side_effects=True),
    )(*bufs)


SMALL_ROWS = 32


def _small_peers():
    x, y, c, _ = _place()
    peers = []
    for k in range(1, N_DEV):
        px, py, pc = x ^ ((k >> 2) & 1), y ^ ((k >> 1) & 1), c ^ (k & 1)
        peers.append((k, (px, py, pc), 4 * px + 2 * py + pc))
    return 4 * x + 2 * y + c, peers


def _all_gather_small_start(gath):
    def body(in_ref, send, recv, out_ref, token):
        token[...] = jnp.zeros_like(token)
        me, peers = _small_peers()
        for k, peer, _ in peers:
            pltpu.make_async_remote_copy(src_ref=out_ref.at[me], dst_ref=out_ref.at[me], send_sem=send.at[k], recv_sem=recv.at[k],
                                         device_id=peer, device_id_type=MESH).start()

    return pl.pallas_call(
        body,
        name="small_gather_start",
        in_specs=[HBM],
        out_specs=[SEM, SEM, HBM, VMEM_SPEC],
        out_shape=[pltpu.SemaphoreType.DMA((N_DEV,)), pltpu.SemaphoreType.DMA((N_DEV,)), pltpu.HBM(gath.shape, gath.dtype), TOKEN],
        input_output_aliases={0: 2},
        compiler_params=pltpu.CompilerParams(has_side_effects=EFFECT),
    )(_in_hbm(gath))


def _all_gather_small_wait(send, recv, gath, after):
    def body(in_ref, send_ref, recv_ref, *rest):
        me, peers = _small_peers()
        for k, peer, peer_id in peers:
            cp = pltpu.make_async_remote_copy(src_ref=in_ref.at[me], dst_ref=in_ref.at[peer_id], send_sem=send_ref.at[k],
                                              recv_sem=recv_ref.at[k], device_id=peer, device_id_type=MESH)
            cp.wait_send()
            cp.wait_recv()

    return pl.pallas_call(
        body,
        name="small_gather_wait",
        in_specs=[HBM, SEM, SEM] + [ANY] * len(after),
        out_specs=HBM,
        out_shape=pltpu.HBM(gath.shape, gath.dtype),
        input_output_aliases={0: 0},
        compiler_params=pltpu.CompilerParams(has_side_effects=EFFECT),
    )(gath, send, recv, *after)


def _sum_small(gath):
    def body(g_ref, o_ref):
        acc = g_ref[0]
        for dev in range(1, N_DEV):
            acc = acc + g_ref[dev]
        o_ref[...] = acc

    return pl.pallas_call(
        body,
        name="small_sum",
        in_specs=[VMEM_SPEC],
        out_specs=VMEM_SPEC,
        out_shape=jax.ShapeDtypeStruct(gath.shape[1:], F32),
        compiler_params=pltpu.CompilerParams(vmem_limit_bytes=VMEM_LIMIT),
    )(gath)


def _adamw_update(g_ref, w_ref, m_ref, v_ref, go_ref, d_ref, mo_ref, vo_ref):
    bc1 = 1.0 - ADAM_B1 ** ADAM_STEP
    bc2 = 1.0 - ADAM_B2 ** ADAM_STEP
    gv = g_ref[...]
    mn = ADAM_B1 * m_ref[...] + (1.0 - ADAM_B1) * gv
    vn = ADAM_B2 * v_ref[...] + (1.0 - ADAM_B2) * (gv * gv)
    go_ref[...] = gv
    mo_ref[...] = mn
    vo_ref[...] = vn
    d_ref[...] = -ADAM_LR * ((mn / bc1) / (jnp.sqrt(vn / bc2) + ADAM_EPS) + ADAM_WD * w_ref[...])


def _adamw_small(gs, ws, ms, vs):
    n = len(gs)

    def body(*refs):
        ins, outs = refs[:4 * n], refs[4 * n:]
        for k in range(n):
            _adamw_update(*[ins[i * n + k] for i in range(4)], *outs[4 * k:4 * k + 4])

    vmem = pl.BlockSpec(memory_space=pltpu.VMEM)
    res = pl.pallas_call(
        body,
        name="adamw_small",
        in_specs=[vmem] * (4 * n),
        out_specs=[vmem] * (4 * n),
        out_shape=[jax.ShapeDtypeStruct(w.shape, F32) for w in ws for _ in range(4)],
        compiler_params=pltpu.CompilerParams(vmem_limit_bytes=VMEM_LIMIT),
    )(*gs, *ws, *ms, *vs)
    return [tuple(res[4 * k:4 * k + 4]) for k in range(n)]


def _adamw(g, w, m, v, name, g_block=0):
    r, cdim = w.shape
    tr = _row_tile(r, cdim, 4)

    def body(*refs):
        _adamw_update(*refs)

    spec = pl.BlockSpec((tr, cdim), lambda i: (i, 0))
    return pl.pallas_call(
        body,
        name=name,
        grid=(r // tr,),
        in_specs=[pl.BlockSpec((tr, cdim), lambda i: (i, g_block))] + [spec] * 3,
        out_specs=[spec] * 4,
        out_shape=[jax.ShapeDtypeStruct((r, cdim), F32)] * 4,
        compiler_params=_params(("parallel",)),
    )(g, w, m, v)


VEC_NAMES = ["pre_mix_norm", "q_norm", "kv_norm", "conv_b", "conv_ln_g", "conv_ln_b", "conv_out_norm",
             "attn_out_norm", "post_mix_norm", "pre_ffn_norm", "post_ffn_norm"]
LOSS_ROW = len(VEC_NAMES)
CONV_W_ROW = 16


def _cols_to_full(parts):
    _, r, cdim = parts.shape
    return parts.transpose(1, 0, 2).reshape(r, N_CHIPS * cdim)


def _full_to_cols(full):
    r, n = full.shape
    return full.reshape(r, N_CHIPS, n // N_CHIPS).transpose(1, 0, 2)


W_IN_SHARD = (2 * CONV_CH + Q_LORA + KV_LORA + QK_ROPE) // N_CHIPS
W_IN_PART = 1024
W_IN_BLOCKS = (2 * CONV_CH + Z2_COLS) // LANES
W_IN_BASE = [p * W_IN_SHARD // LANES for p in range(N_CHIPS)]
W_IN_SPAN = [-(-(p * W_IN_SHARD % LANES + W_IN_SHARD) // LANES) for p in range(N_CHIPS)]


def _w_in_block_home(b):
    n = CONV_CH // LANES
    if b < n:
        return 0, 2 * b
    if b < 2 * n:
        return 0, 2 * (b - n) + 1
    return 1, b - 2 * n


def _to_parts_w_in(shift_chip, w_in_t):
    r = w_in_t.shape[1]
    tr = 512

    def body(s_ref, a_ref, o_ref, scr):
        scr[...] = jnp.zeros_like(scr)
        scr[pl.ds(pl.multiple_of(s_ref[0], 8), W_IN_SHARD), :] = a_ref[...]
        o_ref[...] = scr[...].T.astype(o_ref.dtype)

    return pl.pallas_call(
        body,
        name="to_parts_w_in",
        grid_spec=pltpu.PrefetchScalarGridSpec(
            num_scalar_prefetch=1,
            grid=(r // tr,),
            in_specs=[pl.BlockSpec((W_IN_SHARD, tr), lambda i, s_ref: (0, i))],
            out_specs=pl.BlockSpec((None, tr, W_IN_PART), lambda i, s_ref: (s_ref[1], i, 0)),
            scratch_shapes=[pltpu.VMEM((W_IN_PART, tr), F32)]),
        out_shape=jax.ShapeDtypeStruct((N_CHIPS, r, W_IN_PART), BF16),
        compiler_params=_params(("parallel",)),
    )(shift_chip, w_in_t)


def _adamw_w_in(shift, g_shifted, w_t, m_t, v_t):
    cols = ROW_TILE

    def body(s_ref, g_ref, w_ref, m_ref, v_ref, go_ref, d_ref, mo_ref, vo_ref, scr):
        scr[...] = g_ref[...].T
        _adamw_update(scr.at[pl.ds(pl.multiple_of(s_ref[0], 8), W_IN_SHARD), :], w_ref, m_ref, v_ref, go_ref, d_ref, mo_ref, vo_ref)

    spec = pl.BlockSpec((W_IN_SHARD, cols), lambda i, s_ref: (0, i))
    return pl.pallas_call(
        body,
        name="adamw_w_in",
        grid_spec=pltpu.PrefetchScalarGridSpec(
            num_scalar_prefetch=1,
            grid=(w_t.shape[1] // cols,),
            in_specs=[pl.BlockSpec((cols, W_IN_PART), lambda i, s_ref: (i, 0))] + [spec] * 3,
            out_specs=[spec] * 4,
            scratch_shapes=[pltpu.VMEM((W_IN_PART, cols), F32)]),
        out_shape=[jax.ShapeDtypeStruct(w_t.shape, F32)] * 4,
        compiler_params=_params(("parallel",)),
    )(shift, g_shifted, w_t, m_t, v_t)


def _assemble_w_in(parts):
    r = parts.shape[1]
    tr = ROW_TILE

    def body(p_ref, ag_ref, z2_ref):
        outs = (ag_ref, z2_ref)
        for b in range(W_IN_BLOCKS):
            blk = None
            for p in range(N_CHIPS):
                i = b - W_IN_BASE[p]
                if 0 <= i < W_IN_SPAN[p]:
                    piece = p_ref[p, :, i * LANES:(i + 1) * LANES]
                    blk = piece if blk is None else blk + piece
            which, at = _w_in_block_home(b)
            outs[which][:, at * LANES:(at + 1) * LANES] = blk

    w_ag, w_z2 = pl.pallas_call(
        body,
        name="assemble_w_in",
        grid=(r // tr,),
        in_specs=[pl.BlockSpec((N_CHIPS, tr, W_IN_PART), lambda i: (0, i, 0))],
        out_specs=[pl.BlockSpec((tr, 2 * CONV_CH), lambda i: (i, 0)), pl.BlockSpec((tr, Z2_COLS), lambda i: (i, 0))],
        out_shape=[jax.ShapeDtypeStruct((r, 2 * CONV_CH), parts.dtype), jax.ShapeDtypeStruct((r, Z2_COLS), parts.dtype)],
        compiler_params=_params(("parallel",)),
    )(parts)
    return dict(w_ag=w_ag, w_z2=w_z2)


def _w_in_grad_parts(dw_ag, dw_z2):
    r = dw_ag.shape[0]
    tr = ROW_TILE

    def body(ag_ref, z2_ref, o_ref):
        ins = (ag_ref, z2_ref)
        for p in range(N_CHIPS):
            for i in range(W_IN_PART // LANES):
                if i < W_IN_SPAN[p]:
                    which, at = _w_in_block_home(W_IN_BASE[p] + i)
                    o_ref[p, :, i * LANES:(i + 1) * LANES] = ins[which][:, at * LANES:(at + 1) * LANES]
                else:
                    o_ref[p, :, i * LANES:(i + 1) * LANES] = jnp.zeros((tr, LANES), o_ref.dtype)

    return pl.pallas_call(
        body,
        name="w_in_grad_parts",
        grid=(r // tr,),
        in_specs=[pl.BlockSpec((tr, 2 * CONV_CH), lambda i: (i, 0)), pl.BlockSpec((tr, Z2_COLS), lambda i: (i, 0))],
        out_specs=pl.BlockSpec((N_CHIPS, tr, W_IN_PART), lambda i: (0, i, 0)),
        out_shape=jax.ShapeDtypeStruct((N_CHIPS, r, W_IN_PART), dw_ag.dtype),
        compiler_params=_params(("parallel",)),
    )(dw_ag, dw_z2)


def _assemble_mixer_rest(g):
    uq = _cols_to_full(g["w_uq"]).reshape(Q_LORA, N_HEADS, QK_HEAD)
    w_uq = jnp.concatenate([uq[:, :, :QK_NOPE].reshape(Q_LORA, N_HEADS * QK_NOPE),
                            uq[:, :, QK_NOPE:].reshape(Q_LORA, N_HEADS * QK_ROPE)], axis=1)
    return dict(w_uq=w_uq, w_ukv=_cols_to_full(g["w_ukv"]), conv_w=_cols_to_full(g["conv_w"]),
                w_out=g["w_out"].reshape(-1, g["w_out"].shape[2]))


def _grads_to_parts(dw):
    uq = dw["w_uq"]
    d_uq = jnp.concatenate([uq[:, :N_HEADS * QK_NOPE].reshape(Q_LORA, N_HEADS, QK_NOPE),
                            uq[:, N_HEADS * QK_NOPE:].reshape(Q_LORA, N_HEADS, QK_ROPE)], axis=2).reshape(Q_LORA, N_HEADS * QK_HEAD)
    return dict(w_in=_w_in_grad_parts(dw["w_ag"], dw["w_z2"]), w_uq=_full_to_cols(d_uq), w_ukv=_full_to_cols(dw["w_ukv"]),
                w_out=dw["w_out"].reshape(N_CHIPS, -1, dw["w_out"].shape[1]))


MIXER = ["w_in", "w_uq", "w_ukv", "w_out"]
FFN = ["w_gu", "w_down"]
BIG = MIXER + FFN


def _pad_lanes(v, n):
    return jnp.pad(v, ((0, 0), (0, n - v.shape[1])))


def kernel(x, positions, pre_mix_norm, w_in, q_norm, w_uq, kv_norm, w_ukv, conv_w, conv_b, conv_ln_g, conv_ln_b, conv_out_norm, attn_out_norm, w_out, post_mix_norm, pre_ffn_norm, w_gate, w_up, w_down, post_ffn_norm, loss_target, m_pre_mix_norm, m_w_in, m_q_norm, m_w_uq, m_kv_norm, m_w_ukv, m_conv_w, m_conv_b, m_conv_ln_g, m_conv_ln_b, m_conv_out_norm, m_attn_out_norm, m_w_out, m_post_mix_norm, m_pre_ffn_norm, m_w_gate, m_w_up, m_w_down, m_post_ffn_norm, v_pre_mix_norm, v_w_in, v_q_norm, v_w_uq, v_kv_norm, v_w_ukv, v_conv_w, v_conv_b, v_conv_ln_g, v_conv_ln_b, v_conv_out_norm, v_attn_out_norm, v_w_out, v_post_mix_norm, v_pre_ffn_norm, v_w_gate, v_w_up, v_w_down, v_post_ffn_norm):
    given = dict(locals())
    names = ["pre_mix_norm", "w_in", "q_norm", "w_uq", "kv_norm", "w_ukv", "conv_w", "conv_b", "conv_ln_g", "conv_ln_b",
             "conv_out_norm", "attn_out_norm", "w_out", "post_mix_norm", "pre_ffn_norm", "w_gate", "w_up", "w_down", "post_ffn_norm"]
    def as_2d(a):
        return a if a.ndim == 2 else a[0]

    weights = {n: as_2d(given[n]) for n in names}
    mom = {n: as_2d(given["m_" + n]) for n in names}
    var = {n: as_2d(given["v_" + n]) for n in names}
    d = D_MODEL

    inv_freq = ROPE_THETA ** (-jnp.arange(0, QK_ROPE, 2, dtype=F32) / QK_ROPE)
    ang = positions[0].astype(F32)[:, None] * inv_freq
    cos, sin = jnp.cos(ang), jnp.sin(ang)
    cos2 = jnp.concatenate([cos, cos, cos, cos], axis=1)
    sin2 = jnp.concatenate([-sin, sin, -sin, sin], axis=1)

    chip = 2 * lax.axis_index("x") + lax.axis_index("y")
    core = lax.axis_index("c")
    chip1 = chip.astype(jnp.int32).reshape(1)
    pieces = {n: [weights[n]] for n in ("w_uq", "w_ukv", "w_out", "w_down")}
    pieces["w_gu"] = [weights["w_gate"], weights["w_up"]]
    core1 = core.astype(jnp.int32).reshape(1)
    place = jnp.stack([chip, core]).astype(jnp.int32)
    rest = ["w_uq", "w_ukv", "w_out"]
    w_in_shift = (chip * W_IN_SHARD) % LANES
    w_in_t, m_in_t, v_in_t = (jnp.swapaxes(given[n][0], 0, 1) for n in ("w_in", "m_w_in", "v_w_in"))
    w_in_buf = _to_parts_w_in(jnp.stack([w_in_shift, chip]).astype(jnp.int32), w_in_t)
    (w_in_sems,), w_in_thru, _ = _gather_start([w_in_buf], [False], [[0]], "gather_start_w_in")
    rest_bufs = [_to_parts(chip1, pieces[n], BF16, "to_parts_" + n) for n in rest]
    rest_bufs.append(_to_parts(chip1, [jnp.pad(weights["conv_w"], ((0, CONV_K_PAD - CONV_K), (0, 0)))], F32, "to_parts_conv_w"))
    rest_whole = [False] * 3 + [True]
    ffn_bufs = [_to_parts(chip1, pieces[n], BF16, "to_parts_" + n) for n in FFN]
    got = _gather_wait(w_in_thru, [False], *w_in_sems, rest_bufs + ffn_bufs, "gather_wait_w_in")
    (rest_sems,), rest_thru, started = _gather_start(rest_bufs, rest_whole, [[0, 1, 2, 3]], "gather_start_mixer_rest", after=got)
    full = _assemble_w_in(_gather_hand_on(got, "gather_hand_on_w_in", after=[started])[0])
    vec = {n: weights[n] for n in VEC_NAMES}
    rs = {}

    def get_mixer_rest(after):
        got = _gather_wait(rest_thru, rest_whole, *rest_sems, [after], "gather_wait_mixer_rest")
        (rs["w_gu_sems"],), rs["w_gu_thru"], started = _gather_start(ffn_bufs[:1], [False], [[0]], "gather_start_w_gu", after=got[:1])
        got = list(_gather_hand_on(got[:3], "gather_hand_on_mixer_rest", after=[started])) + [got[3]]
        return _assemble_mixer_rest(dict(zip(rest + ["conv_w"], got)))

    def w_gu_landed(after):
        got = _gather_wait(rs["w_gu_thru"], [False], *rs["w_gu_sems"], [after], "gather_wait_w_gu")
        (rs["w_down_sems"],), rs["w_down_thru"], started = _gather_start(ffn_bufs[1:], [False], [[0]], "gather_start_w_down", after=got)
        rs["w_gu"] = _hand_on_start(got, "hand_on_start_w_gu", after=[started])
        return rs["w_gu"][3]

    def get_w_gu(after):
        send, recv, bufs, _ = rs["w_gu"]
        return _hand_on_wait(bufs, send, recv, [after], "hand_on_wait_w_gu")[0]

    def get_w_down(after):
        got = _gather_wait(rs["w_down_thru"], [False], *rs["w_down_sems"], [after], "gather_wait_w_down")
        got = _gather_hand_on(got, "gather_hand_on_w_down")[0]
        return got.reshape(-1, got.shape[2])

    def pair_start(key):
        def hook(dw):
            rs[key] = _pair_exchange_start(dw.reshape(N_CHIPS, -1, dw.shape[-1]), "grad_pair_start_" + key)
            return rs[key][4]
        return hook

    def reduce_start(group, plist, landed):
        sums = [_pair_sum(core1, a, b, "grad_pair_sum_%s_%d" % (group, k)) for k, (a, b) in enumerate(zip(plist, landed))]
        rs[group] = _chip_exchange_start(sums, "grad_chip_exchange_start_" + group)
        return rs[group][4]

    def reduce_finish(group, after):
        send, recv, sums, slots, _ = rs[group]
        sums, slots = _chip_exchange_wait(sums, slots, send, recv, after, "grad_chip_exchange_wait_" + group)
        halves = [_chip_sum(place, s, sl, "grad_chip_sum_%s_%d" % (group, k)) for k, (s, sl) in enumerate(zip(sums, slots))]
        return list(_half_exchange(halves, "grad_half_exchange_" + group))

    def ffn_grads_exchanged(after):
        pairs = [_pair_exchange_wait(*rs[key][:4], after, "grad_pair_wait_" + key) for key in ("dw_gu", "dw_down")]
        return reduce_start("ffn", [p[0] for p in pairs], [p[1] for p in pairs])

    hooks = dict(mixer_rest=get_mixer_rest, w_gu_landed=w_gu_landed, w_gu=get_w_gu, w_down=get_w_down, dw_down=pair_start("dw_down"), dw_gu=pair_start("dw_gu"),
                 grads_exchanged=ffn_grads_exchanged)
    loss, grad_x, dw, dvec = _local_step(x[0], loss_target[0], cos2, sin2, vec, full, hooks)

    rows = [_pad_lanes(dvec[n], d) for n in VEC_NAMES] + [_pad_lanes(loss, d)]
    rows.append(jnp.zeros((CONV_W_ROW - len(rows), d), F32))
    rows.append(dw["conv_w"].reshape(SMALL_ROWS - CONV_W_ROW, d))
    device1 = (2 * chip + core).astype(jnp.int32).reshape(1)
    small_gather = _all_gather_small_start(_to_parts(device1, [jnp.concatenate(rows, axis=0)], F32, "small_to_slot", n_parts=N_DEV))

    parts = _grads_to_parts(dw)
    plist = [parts[n] for n in MIXER]
    started = reduce_start("mixer", plist, _pair_exchange(plist, "grad_pair_exchange_mixer"))
    send, recv, sums, slots, _ = rs["ffn"]
    sums, slots = _chip_exchange_wait(sums, slots, send, recv, [started, small_gather[3]], "grad_chip_exchange_wait_ffn")
    down = _half_exchange_start(_chip_sum(place, sums[1], slots[1], "grad_chip_sum_ffn_1"), "grad_half_start_w_down")
    gu = _half_exchange_start(_chip_sum(place, sums[0], slots[0], "grad_chip_sum_ffn_0", after=[down[3]]), "grad_half_start_w_gu")
    res = {}
    g_down = _half_exchange_wait(*down[:3], [gu[3]], "grad_half_wait_w_down")
    res["w_down"] = _adamw(g_down, weights["w_down"], mom["w_down"], var["w_down"], "adamw_w_down")
    g_gu = _half_exchange_wait(*gu[:3], [res["w_down"][1]], "grad_half_wait_w_gu")
    for n, blk in (("w_gate", 0), ("w_up", 1)):
        res[n] = _adamw(g_gu, weights[n], mom[n], var[n], "adamw_" + n, g_block=blk)
    small = _sum_small(_all_gather_small_wait(*small_gather[:3], [res["w_up"][1]]))
    g_conv_w_full = small[CONV_W_ROW:].reshape(CONV_K_PAD, CONV_CH)
    g_small = {n: small[i:i + 1, :weights[n].shape[1]] for i, n in enumerate(VEC_NAMES)}
    g_small["conv_w"] = lax.dynamic_slice(g_conv_w_full, (0, chip * (CONV_CH // N_CHIPS)), (CONV_K_PAD, CONV_CH // N_CHIPS))[:CONV_K]
    loss_out = small[LOSS_ROW, 0]
    small_names = VEC_NAMES + ["conv_w"]
    res.update(zip(small_names, _adamw_small([g_small[n] for n in small_names], [weights[n] for n in small_names],
                                             [mom[n] for n in small_names], [var[n] for n in small_names])))
    done_meanwhile = [res["w_gate"][1], res["w_up"][1], res["w_down"][1], res["conv_w"][1], grad_x]
    g_mixer = reduce_finish("mixer", done_meanwhile)
    in_t = _adamw_w_in(w_in_shift.astype(jnp.int32).reshape(1), g_mixer[0], w_in_t, m_in_t, v_in_t)
    res["w_in"] = tuple(jnp.swapaxes(a, 0, 1) for a in in_t)
    for n, g in zip(MIXER[1:], g_mixer[1:]):
        res[n] = _adamw(g, weights[n], mom[n], var[n], "adamw_" + n)
    outs = [loss_out, grad_x[None]]
    for i in range(4):
        outs += [res[n][i].reshape(given[n].shape) for n in names]
    return tuple(outs)
```

```python
import functools

import jax
import jax.numpy as jnp
from jax import lax
from jax.experimental import pallas as pl
from jax.experimental.pallas import tpu as pltpu

F32 = jnp.float32
BF16 = jnp.bfloat16

D_MODEL = 2048
CONV_CH = 1024
CONV_K = 31
CONV_K_PAD = 32
N_HEADS = 8
QK_NOPE = 128
QK_ROPE = 64
V_HEAD = 128
QK_HEAD = QK_NOPE + QK_ROPE
Q_LORA = 768
KV_LORA = 512
ATTN_CH = N_HEADS * V_HEAD
Z2_COLS = Q_LORA + KV_LORA + 128
D_FF = 5632
ROPE_THETA = 10000.0
EPS = 1e-6
LANES = 128
N_CHIPS = 4
N_DEV = 8

ADAM_LR = 0.001
ADAM_B1 = 0.9
ADAM_B2 = 0.999
ADAM_EPS = 1e-08
ADAM_WD = 0.01
ADAM_STEP = 10

VMEM_LIMIT = 56 * 1024 * 1024
ROW_TILE = 256
MAX_TK = 2816
MESH = pl.DeviceIdType.MESH


def _params(sem=None):
    return pltpu.CompilerParams(dimension_semantics=sem, vmem_limit_bytes=VMEM_LIMIT)


def _first_divisor(n, cands):
    for c in cands:
        if n % c == 0:
            return c
    return n


STREAM_BLOCK_BYTES = 3 << 19


def _row_tile(rows, cols, itemsize):
    for tr in (1024, 704, 512, 384, 352, 256, 176, 128, 64, 32, 16):
        if rows % tr == 0 and tr * cols * itemsize <= STREAM_BLOCK_BYTES:
            return tr
    return rows


def _matmul(pairs, mode, out_dtype, name, b_parts=None, out_parts=False, tiles=(None, None, None), after=None):
    a0, b0 = pairs[0]
    part_c = b0.shape[2] if b_parts else None
    if mode == "nn":
        m, n = a0.shape[0], (N_CHIPS * part_c if b_parts else b0.shape[1])
        ks = [a.shape[1] for a, _ in pairs]
    elif mode == "nt":
        m, n = a0.shape[0], b0.shape[-2]
        ks = [a.shape[1] for a, _ in pairs]
    else:
        m, n = a0.shape[1], b0.shape[1]
        ks = [a.shape[0] for a, _ in pairs]
    tm = tiles[0] or _first_divisor(m, (1024, 768, 512, 256))
    tn = tiles[1] or (n if n <= 1536 else _first_divisor(n, (1024, 512, 256, 128)))
    tks = [tiles[2] or (k if k <= MAX_TK else MAX_TK) for k in ks]
    nks = [k // tk for k, tk in zip(ks, tks)]
    offs = [sum(nks[:p]) for p in range(len(pairs))]
    nk = sum(nks)
    n_pairs = len(pairs)
    assert not (b_parts or out_parts) or n_pairs == 1

    def kk(k, p):
        return jnp.clip(k - offs[p], 0, nks[p] - 1)

    in_specs = []
    for p in range(n_pairs):
        tk = tks[p]
        if mode == "nn":
            in_specs.append(pl.BlockSpec((tm, tk), lambda i, j, k, p=p: (i, kk(k, p))))
            if b_parts == "n":
                per = part_c // tn
                in_specs.append(pl.BlockSpec((None, tk, tn), lambda i, j, k: (j // per, k, j % per)))
            else:
                in_specs.append(pl.BlockSpec((tk, tn), lambda i, j, k, p=p: (kk(k, p), j)))
        elif mode == "nt":
            in_specs.append(pl.BlockSpec((tm, tk), lambda i, j, k, p=p: (i, kk(k, p))))
            if b_parts == "k":
                per = part_c // tk
                in_specs.append(pl.BlockSpec((None, tn, tk), lambda i, j, k: (k // per, j, k % per)))
            else:
                in_specs.append(pl.BlockSpec((tn, tk), lambda i, j, k, p=p: (j, kk(k, p))))
        else:
            in_specs.append(pl.BlockSpec((tk, tm), lambda i, j, k, p=p: (kk(k, p), i)))
            in_specs.append(pl.BlockSpec((tk, tn), lambda i, j, k, p=p: (kk(k, p), j)))
    if out_parts:
        out_per = (n // N_CHIPS) // tn
        out_spec = pl.BlockSpec((None, tm, tn), lambda i, j, k: (j // out_per, i, j % out_per))
        out_shape = jax.ShapeDtypeStruct((N_CHIPS, m, n // N_CHIPS), out_dtype)
    else:
        out_spec = pl.BlockSpec((tm, tn), lambda i, j, k: (i, j))
        out_shape = jax.ShapeDtypeStruct((m, n), out_dtype)
    dims = {"nn": (((1,), (0,)), ((), ())), "nt": (((1,), (1,)), ((), ())), "tn": (((0,), (0,)), ((), ()))}[mode]

    n_after = 0 if after is None else 1

    def body(*refs):
        o_ref = refs[2 * n_pairs + n_after]
        k = pl.program_id(2)

        def prod(p):
            return lax.dot_general(refs[2 * p][...], refs[2 * p + 1][...], dims, preferred_element_type=F32)

        if nk == 1:
            o_ref[...] = prod(0).astype(o_ref.dtype)
            return
        acc = refs[2 * n_pairs + n_after + 1]
        for p in range(n_pairs):
            first, last = offs[p], offs[p] + nks[p] - 1
            lo, hi = max(first, 1), min(last, nk - 2)
            if first == 0:
                @pl.when(k == 0)
                def _(p=p):
                    acc[...] = prod(p)

            if lo <= hi:
                @pl.when((k >= lo) & (k <= hi))
                def _(p=p):
                    acc[...] += prod(p)

            if last == nk - 1:
                @pl.when(k == nk - 1)
                def _(p=p):
                    o_ref[...] = (acc[...] + prod(p)).astype(o_ref.dtype)

    flat = [t for pr in pairs for t in pr] + ([] if after is None else [after])
    return pl.pallas_call(
        body,
        name=name,
        grid=(m // tm, n // tn, nk),
        in_specs=in_specs + [pl.BlockSpec(memory_space=pl.ANY)] * n_after,
        out_specs=out_spec,
        out_shape=out_shape,
        scratch_shapes=[pltpu.VMEM((tm, tn), F32)] if nk > 1 else [],
        compiler_params=_params(("parallel", "parallel", "arbitrary")),
    )(*flat)


F4 = D_FF // N_CHIPS
FFN_TM = 512
FFN_STRIP = 256


def _ffn_up(hf, w_gu):
    t, d = hf.shape

    def body(a_ref, b_ref, gu_ref, act_ref):
        for r in range(0, FFN_TM, FFN_STRIP):
            acc = jnp.dot(a_ref[r:r + FFN_STRIP, :], b_ref[...], preferred_element_type=F32)
            g = acc[:, :F4]
            gu_ref[r:r + FFN_STRIP, :] = acc.astype(gu_ref.dtype)
            act_ref[r:r + FFN_STRIP, :] = (g * _sigmoid(g) * acc[:, F4:]).astype(act_ref.dtype)

    return pl.pallas_call(
        body,
        name="ffn_up",
        grid=(N_CHIPS, t // FFN_TM),
        in_specs=[pl.BlockSpec((FFN_TM, d), lambda q, i: (i, 0)),
                  pl.BlockSpec((None, d, 2 * F4), lambda q, i: (q, 0, 0))],
        out_specs=[pl.BlockSpec((FFN_TM, 2 * F4), lambda q, i: (i, q)),
                   pl.BlockSpec((FFN_TM, F4), lambda q, i: (i, q))],
        out_shape=[jax.ShapeDtypeStruct((t, 2 * D_FF), BF16), jax.ShapeDtypeStruct((t, D_FF), BF16)],
        compiler_params=_params(("parallel", "parallel")),
    )(hf, w_gu)


def _ffn_down_dx(d_ff, w_down, gu):
    t, d = d_ff.shape

    def body(a_ref, b_ref, gu_ref, o_ref):
        for r in range(0, FFN_TM, FFN_STRIP):
            rows = slice(r, r + FFN_STRIP)
            d_act = lax.dot_general(a_ref[rows, :], b_ref[...], NT_DIMS, preferred_element_type=F32)
            g = gu_ref[rows, :F4].astype(F32)
            u = gu_ref[rows, F4:].astype(F32)
            sg = _sigmoid(g)
            dsg = d_act * sg
            o_ref[rows, :F4] = (dsg * u * (1.0 + g - g * sg)).astype(o_ref.dtype)
            o_ref[rows, F4:] = (dsg * g).astype(o_ref.dtype)

    return pl.pallas_call(
        body,
        name="ffn_down_dx",
        grid=(N_CHIPS, t // FFN_TM),
        in_specs=[pl.BlockSpec((FFN_TM, d), lambda q, i: (i, 0)),
                  pl.BlockSpec((F4, d), lambda q, i: (q, 0)),
                  pl.BlockSpec((FFN_TM, 2 * F4), lambda q, i: (i, q))],
        out_specs=pl.BlockSpec((FFN_TM, 2 * F4), lambda q, i: (i, q)),
        out_shape=jax.ShapeDtypeStruct((t, 2 * D_FF), BF16),
        compiler_params=_params(("parallel", "parallel")),
    )(d_ff, w_down, gu)


def _rowwise(fn, row_ins, vec_ins, row_outs, acc_outs, name, after=None):
    t = row_ins[0].shape[0]
    tm = ROW_TILE
    n_in = len(row_ins) + len(vec_ins)
    n_row = len(row_outs)
    extra = [] if after is None else [after]

    def body(*refs):
        ins = [r[...] for r in refs[:n_in]]
        outs = refs[n_in + len(extra):]
        vals = fn(*ins)
        for r, v in zip(outs[:n_row], vals[:n_row]):
            r[...] = v.astype(r.dtype)
        if acc_outs:
            @pl.when(pl.program_id(0) == 0)
            def _():
                for r in outs[n_row:]:
                    r[...] = jnp.zeros_like(r)

            for r, v in zip(outs[n_row:], vals[n_row:]):
                r[...] += v

    in_specs = [pl.BlockSpec((tm, a.shape[1]), lambda i: (i, 0)) for a in row_ins]
    in_specs += [pl.BlockSpec(a.shape, lambda i: (0, 0)) for a in vec_ins]
    out_specs = [pl.BlockSpec((tm, c), lambda i: (i, 0)) for c, _ in row_outs]
    out_specs += [pl.BlockSpec((1, c), lambda i: (0, 0)) for c in acc_outs]
    out_shape = [jax.ShapeDtypeStruct((t, c), dt) for c, dt in row_outs]
    out_shape += [jax.ShapeDtypeStruct((1, c), F32) for c in acc_outs]
    return pl.pallas_call(
        body,
        name=name,
        grid=(t // tm,),
        in_specs=in_specs + [pl.BlockSpec(memory_space=pl.ANY)] * len(extra),
        out_specs=out_specs,
        out_shape=out_shape,
        compiler_params=_params(("arbitrary",)),
    )(*row_ins, *vec_ins, *extra)


def _mean(v):
    return jnp.mean(v, axis=-1, keepdims=True)


def _colsum(v):
    return jnp.sum(v, axis=0, keepdims=True)


def _rms_fwd(v, g):
    r = lax.rsqrt(_mean(v * v) + EPS)
    vhat = v * r
    return vhat * g, vhat, r


def _rms_bwd(dn, vhat, r, g):
    dng = dn * g
    return r * (dng - vhat * _mean(dng * vhat)), _colsum(dn * vhat)


def _swap_rope_halves(v):
    n = v.shape[-1]
    lane = lax.broadcasted_iota(jnp.int32, v.shape, v.ndim - 1)
    return jnp.where(lane % QK_ROPE < QK_ROPE // 2, pltpu.roll(v, n - QK_ROPE // 2, v.ndim - 1),
                     pltpu.roll(v, QK_ROPE // 2, v.ndim - 1))


def _rope(v, cos2, sin2):
    return v * cos2 + _swap_rope_halves(v) * sin2


def _rope_transposed(dv, cos2, sin2):
    return dv * cos2 + _swap_rope_halves(dv * sin2)


def _sigmoid(v):
    return 1.0 / (1.0 + jnp.exp(-v))


CONV_ROWS = 256


def _conv_fwd(ag, conv_w, conv_b):
    t = ag.shape[0]
    cb = LANES

    def body(ag_ref, w_ref, b_ref, o_ref, scr):
        a = ag_ref[:, :cb].astype(F32)
        g = ag_ref[:, cb:].astype(F32)
        scr[pl.ds(0, CONV_K_PAD), :] = jnp.zeros((CONV_K_PAD, cb), F32)
        scr[pl.ds(CONV_K_PAD, t), :] = a * _sigmoid(g)
        for r0 in range(0, t, CONV_ROWS):
            acc = jnp.zeros((CONV_ROWS, cb), F32) + b_ref[...]
            for k in range(CONV_K):
                acc = acc + w_ref[k:k + 1, :] * scr[pl.ds(r0 + CONV_K_PAD - (CONV_K - 1) + k, CONV_ROWS), :]
            o_ref[pl.ds(r0, CONV_ROWS), :] = acc

    return pl.pallas_call(
        body,
        name="conv_fwd",
        grid=(CONV_CH // cb,),
        in_specs=[pl.BlockSpec((t, 2 * cb), lambda j: (0, j)),
                  pl.BlockSpec((CONV_K_PAD, cb), lambda j: (0, j)),
                  pl.BlockSpec((1, cb), lambda j: (0, j))],
        out_specs=pl.BlockSpec((t, cb), lambda j: (0, j)),
        out_shape=jax.ShapeDtypeStruct((t, CONV_CH), F32),
        scratch_shapes=[pltpu.VMEM((t + CONV_K_PAD, cb), F32)],
        compiler_params=_params(("parallel",)),
    )(ag, conv_w, conv_b)


def _conv_bwd(d_u1, ag, conv_w):
    t = ag.shape[0]
    cb = LANES

    def body(du_ref, ag_ref, w_ref, dag_ref, dw_ref, db_ref, su, sd):
        a = ag_ref[:, :cb].astype(F32)
        g = ag_ref[:, cb:].astype(F32)
        sg = _sigmoid(g)
        su[pl.ds(0, CONV_K_PAD), :] = jnp.zeros((CONV_K_PAD, cb), F32)
        su[pl.ds(CONV_K_PAD, t), :] = a * sg
        sd[pl.ds(0, t), :] = du_ref[...]
        sd[pl.ds(t, CONV_K_PAD), :] = jnp.zeros((CONV_K_PAD, cb), F32)
        db_ref[...] = _colsum(du_ref[...])
        dw_ref[...] = jnp.zeros_like(dw_ref)
        for r0 in range(0, t, CONV_ROWS):
            du = sd[pl.ds(r0, CONV_ROWS), :]
            acc = jnp.zeros((CONV_ROWS, cb), F32)
            for k in range(CONV_K):
                acc = acc + w_ref[k:k + 1, :] * sd[pl.ds(r0 + (CONV_K - 1) - k, CONV_ROWS), :]
                dw_ref[k:k + 1, :] += _colsum(du * su[pl.ds(r0 + CONV_K_PAD - (CONV_K - 1) + k, CONV_ROWS), :])
            sgc = sg[r0:r0 + CONV_ROWS]
            ac = a[r0:r0 + CONV_ROWS]
            dag_ref[pl.ds(r0, CONV_ROWS), :cb] = (acc * sgc).astype(dag_ref.dtype)
            dag_ref[pl.ds(r0, CONV_ROWS), cb:] = (acc * ac * sgc * (1.0 - sgc)).astype(dag_ref.dtype)

    return pl.pallas_call(
        body,
        name="conv_bwd",
        grid=(CONV_CH // cb,),
        in_specs=[pl.BlockSpec((t, cb), lambda j: (0, j)),
                  pl.BlockSpec((t, 2 * cb), lambda j: (0, j)),
                  pl.BlockSpec((CONV_K_PAD, cb), lambda j: (0, j))],
        out_specs=[pl.BlockSpec((t, 2 * cb), lambda j: (0, j)),
                   pl.BlockSpec((CONV_K_PAD, cb), lambda j: (0, j)),
                   pl.BlockSpec((1, cb), lambda j: (0, j))],
        out_shape=[jax.ShapeDtypeStruct((t, 2 * CONV_CH), BF16),
                   jax.ShapeDtypeStruct((CONV_K_PAD, CONV_CH), F32),
                   jax.ShapeDtypeStruct((1, CONV_CH), F32)],
        scratch_shapes=[pltpu.VMEM((t + CONV_K_PAD, cb), F32), pltpu.VMEM((t + CONV_K_PAD, cb), F32)],
        compiler_params=_params(("parallel",)),
    )(d_u1, ag, conv_w)


ATT_TQ = 256
NEG = float(jnp.finfo(jnp.float32).min)
SCALE = QK_HEAD ** -0.5
NT_DIMS = (((1,), (1,)), ((), ()))
TN_DIMS = (((0,), (0,)), ((), ()))


def _att_weights(qf, kf, row0):
    s = lax.dot_general(qf, kf, NT_DIMS, preferred_element_type=F32)
    tq, t = s.shape
    qpos = row0 + lax.broadcasted_iota(jnp.int32, (tq, t), 0)
    kpos = lax.broadcasted_iota(jnp.int32, (tq, t), 1)
    s = jnp.where(kpos <= qpos, s, NEG)
    p = jnp.exp(s - jnp.max(s, axis=-1, keepdims=True))
    return p, 1.0 / jnp.sum(p, axis=-1, keepdims=True)


def _scaled_query(qn, roped_half):
    return jnp.concatenate([(qn.astype(F32) * SCALE).astype(BF16), (roped_half * SCALE).astype(BF16)], axis=1)


def _half_mask(shape, which):
    lane = lax.broadcasted_iota(jnp.int32, shape, len(shape) - 1)
    return (lane // QK_ROPE == which).astype(F32)


def _attention_fwd(q, kv, kpe2, cos2, sin2):
    t = q.shape[0]
    tq = ATT_TQ

    def body(qn_ref, qp_ref, c_ref, s_ref, kv_ref, kpe_ref, o_ref):
        roped = _rope(qp_ref[...].astype(F32), c_ref[...], s_ref[...])

        def block(i):
            keys = slice(0, (i + 1) * tq)
            for e in range(2):
                qf = _scaled_query(qn_ref[:, e * QK_NOPE:(e + 1) * QK_NOPE], roped * _half_mask(roped.shape, e))
                kf = jnp.concatenate([kv_ref[keys, e * 256:e * 256 + QK_NOPE], kpe_ref[keys, :]], axis=1)
                p, inv_l = _att_weights(qf, kf, i * tq)
                v = kv_ref[keys, e * 256 + QK_NOPE:(e + 1) * 256]
                o = jnp.dot(p.astype(BF16), v, preferred_element_type=F32) * inv_l
                o_ref[:, e * V_HEAD:(e + 1) * V_HEAD] = o.astype(o_ref.dtype)

        for i in range(t // tq):
            pl.when(pl.program_id(1) == i)(functools.partial(block, i))

    return pl.pallas_call(
        body,
        name="attention_fwd",
        grid=(N_HEADS // 2, t // tq),
        in_specs=[pl.BlockSpec((tq, 2 * QK_NOPE), lambda h, i: (i, h)),
                  pl.BlockSpec((tq, LANES), lambda h, i: (i, N_HEADS + h)),
                  pl.BlockSpec((tq, LANES), lambda h, i: (i, 0)),
                  pl.BlockSpec((tq, LANES), lambda h, i: (i, 0)),
                  pl.BlockSpec((t, 512), lambda h, i: (0, h)),
                  pl.BlockSpec((t, LANES), lambda h, i: (0, 0))],
        out_specs=pl.BlockSpec((tq, 2 * V_HEAD), lambda h, i: (i, h)),
        out_shape=jax.ShapeDtypeStruct((t, ATTN_CH), BF16),
        compiler_params=_params(("parallel", "parallel")),
    )(q, q, cos2, sin2, kv, kpe2)


def _attention_bwd(q, kv, kpe2, cos2, sin2, d_attn):
    t = q.shape[0]
    tq = ATT_TQ
    n_q = t // tq

    def body(qn_ref, qp_ref, c_ref, s_ref, kv_ref, kpe_ref, do_ref, dqn_ref, dqp_ref, dkv_ref, dkpe_ref, dkv_acc):
        h, i = pl.program_id(0), pl.program_id(1)

        @pl.when(i == 0)
        def _():
            dkv_acc[...] = jnp.zeros_like(dkv_acc)

        @pl.when((i == 0) & (h == 0))
        def _():
            dkpe_ref[...] = jnp.zeros_like(dkpe_ref)

        roped = _rope(qp_ref[...].astype(F32), c_ref[...], s_ref[...])

        def block(ib):
            keys = slice(0, (ib + 1) * tq)
            d_roped = jnp.zeros((tq, LANES), F32)
            for e in range(2):
                mask = _half_mask(roped.shape, e)
                qf = _scaled_query(qn_ref[:, e * QK_NOPE:(e + 1) * QK_NOPE], roped * mask)
                kf = jnp.concatenate([kv_ref[keys, e * 256:e * 256 + QK_NOPE], kpe_ref[keys, :]], axis=1)
                v = kv_ref[keys, e * 256 + QK_NOPE:(e + 1) * 256]
                do = do_ref[:, e * V_HEAD:(e + 1) * V_HEAD]
                p, inv_l = _att_weights(qf, kf, ib * tq)
                p = p * inv_l
                dp = lax.dot_general(do, v, NT_DIMS, preferred_element_type=F32)
                ds = (p * (dp - jnp.sum(p * dp, axis=-1, keepdims=True))).astype(BF16)
                dqf = jnp.dot(ds, kf, preferred_element_type=F32) * SCALE
                dkf = lax.dot_general(ds, qf, TN_DIMS, preferred_element_type=F32)
                dv = lax.dot_general(p.astype(BF16), do, TN_DIMS, preferred_element_type=F32)
                dqn_ref[:, e * QK_NOPE:(e + 1) * QK_NOPE] = dqf[:, :QK_NOPE].astype(dqn_ref.dtype)
                d_roped = d_roped + dqf[:, QK_NOPE:] * mask
                dkv_acc[keys, e * 256:e * 256 + QK_NOPE] += dkf[:, :QK_NOPE]
                dkv_acc[keys, e * 256 + QK_NOPE:(e + 1) * 256] += dv
                dkpe_ref[keys, :] += dkf[:, QK_NOPE:]
            dqp_ref[...] = _rope_transposed(d_roped, c_ref[...], s_ref[...]).astype(dqp_ref.dtype)

        for ib in range(n_q):
            pl.when(i == ib)(functools.partial(block, ib))

        @pl.when(i == n_q - 1)
        def _():
            dkv_ref[...] = dkv_acc[...].astype(dkv_ref.dtype)

    return pl.pallas_call(
        body,
        name="attention_bwd",
        grid=(N_HEADS // 2, n_q),
        in_specs=[pl.BlockSpec((tq, 2 * QK_NOPE), lambda h, i: (i, h)),
                  pl.BlockSpec((tq, LANES), lambda h, i: (i, N_HEADS + h)),
                  pl.BlockSpec((tq, LANES), lambda h, i: (i, 0)),
                  pl.BlockSpec((tq, LANES), lambda h, i: (i, 0)),
                  pl.BlockSpec((t, 512), lambda h, i: (0, h)),
                  pl.BlockSpec((t, LANES), lambda h, i: (0, 0)),
                  pl.BlockSpec((tq, 2 * V_HEAD), lambda h, i: (i, h))],
        out_specs=[pl.BlockSpec((tq, 2 * QK_NOPE), lambda h, i: (i, h)),
                   pl.BlockSpec((tq, LANES), lambda h, i: (i, h)),
                   pl.BlockSpec((t, 512), lambda h, i: (0, h)),
                   pl.BlockSpec((t, LANES), lambda h, i: (0, 0))],
        out_shape=[jax.ShapeDtypeStruct((t, N_HEADS * QK_NOPE), BF16),
                   jax.ShapeDtypeStruct((t, N_HEADS * QK_ROPE), BF16),
                   jax.ShapeDtypeStruct((t, N_HEADS * 256), BF16),
                   jax.ShapeDtypeStruct((t, LANES), F32)],
        scratch_shapes=[pltpu.VMEM((t, 512), F32)],
        compiler_params=_params(("arbitrary", "arbitrary")),
    )(q, q, cos2, sin2, kv, kpe2, d_attn)


def _local_step(x, target, cos2, sin2, vec, w, ffn):
    d = D_MODEL

    (h,) = _rowwise(lambda xv, g: (_rms_fwd(xv, g)[0],), [x], [vec["pre_mix_norm"]], [(d, BF16)], [], "pre_mix_norm_fwd")
    ag = _matmul([(h, w["w_ag"])], "nn", BF16, "in_proj_ag")
    z2 = _matmul([(h, w["w_z2"])], "nn", BF16, "in_proj_z2")
    w = {**w, **ffn["mixer_rest"](z2)}
    u1 = _conv_fwd(ag, w["conv_w"], vec["conv_b"])

    def latents_fwd(z, c2, s2, qg, kvg):
        z = z.astype(F32)
        qn = _rms_fwd(z[:, :Q_LORA], qg)[0]
        kvn = _rms_fwd(z[:, Q_LORA:Q_LORA + KV_LORA], kvg)[0]
        kr = z[:, Q_LORA + KV_LORA:]
        kr2 = kr + pltpu.roll(kr, QK_ROPE, 1)
        return qn, kvn, _rope(kr2, c2, s2)

    qn, kvn, kpe2 = _rowwise(latents_fwd, [z2, cos2, sin2], [vec["q_norm"], vec["kv_norm"]],
                             [(Q_LORA, BF16), (KV_LORA, BF16), (LANES, BF16)], [], "latents_fwd")
    q = _matmul([(qn, w["w_uq"])], "nn", BF16, "q_up")
    kv = _matmul([(kvn, w["w_ukv"])], "nn", BF16, "kv_up")
    attn = _attention_fwd(q, kv, kpe2, cos2, sin2)

    def conv_post(u, lg, lb):
        mu = _mean(u)
        uc = u - mu
        rstd = lax.rsqrt(_mean(uc * uc) + EPS)
        uhat = uc * rstd
        u2 = uhat * lg + lb
        sg = _sigmoid(u2)
        return uhat, rstd, u2, sg, u2 * sg

    def mix_in_fwd(u, at, lg, lb, cg, ag_):
        u3 = conv_post(u, lg, lb)[4]
        cn = _rms_fwd(u3, cg)[0]
        an = _rms_fwd(at.astype(F32), ag_)[0]
        return (jnp.concatenate([cn, an], axis=1),)

    (cat,) = _rowwise(mix_in_fwd, [u1, attn], [vec["conv_ln_g"], vec["conv_ln_b"], vec["conv_out_norm"], vec["attn_out_norm"]],
                      [(2 * CONV_CH, BF16)], [], "mix_in_fwd")
    mix = _matmul([(cat, w["w_out"])], "nn", F32, "out_proj")
    landed = ffn["w_gu_landed"](mix)

    def residual1(xv, mv, gpm, gpf):
        x1 = xv + _rms_fwd(mv, gpm)[0]
        return x1, _rms_fwd(x1, gpf)[0]

    x1, hf = _rowwise(residual1, [x, mix], [vec["post_mix_norm"], vec["pre_ffn_norm"]], [(d, F32), (d, BF16)], [],
                      "residual1_fwd", after=landed)
    w_gu = ffn["w_gu"](hf)
    gu, act = _ffn_up(hf, w_gu)
    w_down = ffn["w_down"](act)
    ff = _matmul([(act, w_down)], "nn", F32, "ffn_down")

    def loss_head(x1v, ffv, tg, g):
        n, fhat, r = _rms_fwd(ffv, g)
        err = x1v + n - tg
        loss = 0.5 * jnp.sum(_mean(err * err), axis=0, keepdims=True)
        dy = err * (1.0 / d)
        d_ff, dg = _rms_bwd(dy, fhat, r, g)
        return dy, d_ff, dg, jnp.broadcast_to(loss, (1, LANES))

    dy, d_ff, g_post_ffn, loss = _rowwise(loss_head, [x1, ff, target], [vec["post_ffn_norm"]],
                                          [(d, F32), (d, BF16)], [d, LANES], "loss_head")
    d_gu = _ffn_down_dx(d_ff, w_down, gu)
    dw_down = _matmul([(act, d_ff)], "tn", BF16, "ffn_down_dw", tiles=(F4, None, None))
    started = ffn["dw_down"](dw_down)
    dw_gu = _matmul([(hf, d_gu)], "tn", BF16, "ffn_gate_up_dw", out_parts=True, tiles=(None, F4, None), after=started)
    started = ffn["dw_gu"](dw_gu)
    d_hf = _matmul([(d_gu, w_gu)], "nt", F32, "ffn_gate_up_dx", b_parts="k", after=started)
    started = ffn["grads_exchanged"](d_hf)

    def residual1_bwd(dyv, dhf, x1v, mv, gpf, gpm):
        _, x1hat, r1 = _rms_fwd(x1v, gpf)
        dn, dgpf = _rms_bwd(dhf, x1hat, r1, gpf)
        d_x1 = dyv + dn
        _, mhat, rm = _rms_fwd(mv, gpm)
        d_mix, dgpm = _rms_bwd(d_x1, mhat, rm, gpm)
        return d_x1, d_mix, dgpf, dgpm

    d_x1, d_mix, g_pre_ffn, g_post_mix = _rowwise(residual1_bwd, [dy, d_hf, x1, mix], [vec["pre_ffn_norm"], vec["post_mix_norm"]],
                                                  [(d, F32), (d, BF16)], [d, d], "residual1_bwd", after=started)
    d_cat = _matmul([(d_mix, w["w_out"])], "nt", BF16, "out_proj_dx")
    dw_out = _matmul([(cat, d_mix)], "tn", BF16, "out_proj_dw")

    def mix_in_bwd(dc, u, at, lg, lb, cg, ag_):
        dc = dc.astype(F32)
        uhat, rstd, u2, sg, u3 = conv_post(u, lg, lb)
        _, u3hat, rc = _rms_fwd(u3, cg)
        d_u3, dcg = _rms_bwd(dc[:, :CONV_CH], u3hat, rc, cg)
        d_u2 = d_u3 * sg * (1.0 + u2 * (1.0 - sg))
        dgl = d_u2 * lg
        d_u1 = rstd * (dgl - _mean(dgl) - uhat * _mean(dgl * uhat))
        _, ahat, ra = _rms_fwd(at.astype(F32), ag_)
        d_at, dag = _rms_bwd(dc[:, CONV_CH:], ahat, ra, ag_)
        return d_u1, d_at, dcg, _colsum(d_u2 * uhat), _colsum(d_u2), dag

    d_u1, d_attn, g_conv_out, g_ln_g, g_ln_b, g_attn_out = _rowwise(
        mix_in_bwd, [d_cat, u1, attn], [vec["conv_ln_g"], vec["conv_ln_b"], vec["conv_out_norm"], vec["attn_out_norm"]],
        [(CONV_CH, F32), (ATTN_CH, BF16)], [CONV_CH] * 4, "mix_in_bwd")
    d_ag, d_conv_w, g_conv_b = _conv_bwd(d_u1, ag, w["conv_w"])
    d_qn_, d_qp_, d_kv, d_kpe2 = _attention_bwd(q, kv, kpe2, cos2, sin2, d_attn)
    d_q = jnp.concatenate([d_qn_, d_qp_], axis=1)
    d_qn = _matmul([(d_q, w["w_uq"])], "nt", BF16, "q_up_dx")
    dw_uq = _matmul([(qn, d_q)], "tn", BF16, "q_up_dw")
    d_kvn = _matmul([(d_kv, w["w_ukv"])], "nt", BF16, "kv_up_dx")
    dw_ukv = _matmul([(kvn, d_kv)], "tn", BF16, "kv_up_dw")

    def latents_bwd(z, dq, dk, dkp, c2, s2, qg, kvg):
        z = z.astype(F32)
        _, qhat, rq = _rms_fwd(z[:, :Q_LORA], qg)
        d_ql, dqg = _rms_bwd(dq.astype(F32), qhat, rq, qg)
        _, khat, rk = _rms_fwd(z[:, Q_LORA:Q_LORA + KV_LORA], kvg)
        d_kl, dkg = _rms_bwd(dk.astype(F32), khat, rk, kvg)
        both = dkp + pltpu.roll(dkp, QK_ROPE, 1)
        d_kr = _rope_transposed(both, c2, s2) * _half_mask(both.shape, 0)
        return jnp.concatenate([d_ql, d_kl, d_kr], axis=1), dqg, dkg

    d_z2, g_q_norm, g_kv_norm = _rowwise(latents_bwd, [z2, d_qn, d_kvn, d_kpe2, cos2, sin2], [vec["q_norm"], vec["kv_norm"]],
                                         [(Z2_COLS, BF16)], [Q_LORA, KV_LORA], "latents_bwd")
    dw_ag = _matmul([(h, d_ag)], "tn", BF16, "in_proj_ag_dw")
    dw_z2 = _matmul([(h, d_z2)], "tn", BF16, "in_proj_z2_dw")
    started = ffn["mixer_grads"](dict(w_ag=dw_ag, w_z2=dw_z2, w_uq=dw_uq, w_ukv=dw_ukv, w_out=dw_out))
    d_h = _matmul([(d_ag, w["w_ag"]), (d_z2, w["w_z2"])], "nt", F32, "in_proj_dx", after=started)

    def pre_mix_bwd(dx1, dh, xv, g):
        _, xhat, r = _rms_fwd(xv, g)
        dn, dg = _rms_bwd(dh, xhat, r, g)
        return dx1 + dn, dg

    grad_x, g_pre_mix = _rowwise(pre_mix_bwd, [d_x1, d_h, x], [vec["pre_mix_norm"]], [(d, F32)], [d], "pre_mix_norm_bwd")

    dw = dict(w_ag=dw_ag, w_z2=dw_z2, w_uq=dw_uq, w_ukv=dw_ukv, conv_w=d_conv_w, w_out=dw_out, w_gu=dw_gu, w_down=dw_down)
    dvec = dict(pre_mix_norm=g_pre_mix, q_norm=g_q_norm, kv_norm=g_kv_norm, conv_b=g_conv_b, conv_ln_g=g_ln_g,
                conv_ln_b=g_ln_b, conv_out_norm=g_conv_out, attn_out_norm=g_attn_out, post_mix_norm=g_post_mix,
                pre_ffn_norm=g_pre_ffn, post_ffn_norm=g_post_ffn)
    return loss, grad_x, dw, dvec


ANY = pl.BlockSpec(memory_space=pl.ANY)


def _place():
    x, y, c = lax.axis_index("x"), lax.axis_index("y"), lax.axis_index("c")
    chips = [(1 - x, y), (x, 1 - y), (1 - x, 1 - y)]
    return x, y, c, chips


def _to_parts(chip, pieces, dtype, name, n_parts=N_CHIPS):
    r = pieces[0].shape[0]
    widths = [a.shape[1] for a in pieces]
    tr = r if r <= 512 else _first_divisor(r, (512, 256, 128))

    def body(p_ref, *refs):
        o_ref = refs[len(pieces)]
        off = 0
        for a_ref, wdt in zip(refs, widths):
            o_ref[:, off:off + wdt] = a_ref[...].astype(o_ref.dtype)
            off += wdt

    return pl.pallas_call(
        body,
        name=name,
        grid_spec=pltpu.PrefetchScalarGridSpec(
            num_scalar_prefetch=1,
            grid=(r // tr,),
            in_specs=[pl.BlockSpec((tr, wdt), lambda i, p_ref: (i, 0)) for wdt in widths],
            out_specs=pl.BlockSpec((None, tr, sum(widths)), lambda i, p_ref: (p_ref[0], i, 0))),
        out_shape=jax.ShapeDtypeStruct((n_parts, r, sum(widths)), dtype),
        compiler_params=_params(("parallel",)),
    )(chip, *pieces)


HBM = pl.BlockSpec(memory_space=pltpu.HBM)
SEM = pl.BlockSpec(memory_space=pltpu.SEMAPHORE)
EFFECT = pltpu.SideEffectType.DATAFLOW_SIDE_EFFECTING
VMEM_SPEC = pl.BlockSpec(memory_space=pltpu.VMEM)
TOKEN = jax.ShapeDtypeStruct((8, LANES), F32)


def _in_hbm(a):
    return pltpu.with_memory_space_constraint(a, pltpu.HBM)


def _gather_rows(buf, whole, half):
    r = buf.shape[1]
    return pl.ds(0, r) if whole else pl.ds(half * (r // 2), r // 2)


def _gather_start(bufs, whole, groups, name, after=()):
    n = len(bufs)
    n_g = len(groups)

    def body(*refs):
        refs = refs[:n] + refs[n + len(after):]
        sems = refs[n:n + 2 * n_g]
        outs = refs[n + 2 * n_g:2 * n + 2 * n_g]
        token = refs[2 * n + 2 * n_g]
        token[...] = jnp.zeros_like(token)
        x, y, c, chips = _place()
        p = 2 * x + y
        for gi, group in enumerate(groups):
            for ki, k in enumerate(group):
                blk = outs[k].at[p, _gather_rows(bufs[k], whole[k], c), :]
                for j, (px, py) in enumerate(chips):
                    pltpu.make_async_remote_copy(src_ref=blk, dst_ref=blk, send_sem=sems[2 * gi].at[3 * ki + j],
                                                 recv_sem=sems[2 * gi + 1].at[3 * ki + j],
                                                 device_id=(px, py, c), device_id_type=MESH).start()

    sem_shapes = []
    for group in groups:
        sem_shapes += [pltpu.SemaphoreType.DMA((3 * len(group),))] * 2
    res = pl.pallas_call(
        body,
        name=name,
        in_specs=[HBM] * n + [ANY] * len(after),
        out_specs=[SEM] * (2 * n_g) + [HBM] * n + [VMEM_SPEC],
        out_shape=sem_shapes + [pltpu.HBM(a.shape, a.dtype) for a in bufs] + [TOKEN],
        input_output_aliases={k: 2 * n_g + k for k in range(n)},
        compiler_params=pltpu.CompilerParams(has_side_effects=EFFECT),
    )(*[_in_hbm(a) for a in bufs], *after)
    sems = [(res[2 * gi], res[2 * gi + 1]) for gi in range(n_g)]
    return sems, list(res[2 * n_g:2 * n_g + n]), res[2 * n_g + n]


def _gather_wait(bufs, whole, send, recv, after, name):
    n = len(bufs)

    def body(*refs):
        ins = refs[:n]
        send_ref, recv_ref = refs[n], refs[n + 1]
        x, y, c, chips = _place()
        p = 2 * x + y
        for ki in range(n):
            rows = _gather_rows(bufs[ki], whole[ki], c)
            for j, (px, py) in enumerate(chips):
                cp = pltpu.make_async_remote_copy(src_ref=ins[ki].at[p, rows, :], dst_ref=ins[ki].at[2 * px + py, rows, :],
                                                  send_sem=send_ref.at[3 * ki + j], recv_sem=recv_ref.at[3 * ki + j],
                                                  device_id=(px, py, c), device_id_type=MESH)
                cp.wait_send()
                cp.wait_recv()

    res = pl.pallas_call(
        body,
        name=name,
        in_specs=[HBM] * n + [SEM, SEM] + [ANY] * len(after),
        out_specs=[HBM] * n,
        out_shape=[pltpu.HBM(a.shape, a.dtype) for a in bufs],
        input_output_aliases={k: k for k in range(n)},
        compiler_params=pltpu.CompilerParams(has_side_effects=EFFECT),
    )(*bufs, send, recv, *after)
    return list(res)


def _gather_hand_on(bufs, name, after=()):
    n = len(bufs)

    def body(*refs):
        refs = refs[n + len(after):]
        outs = refs[:n]
        send, recv = refs[n:]
        x, y, c, chips = _place()

        def d2d(k, j, half):
            px, py = chips[j]
            blk = outs[k].at[2 * px + py, _gather_rows(bufs[k], False, half), :]
            return pltpu.make_async_remote_copy(src_ref=blk, dst_ref=blk, send_sem=send.at[3 * k + j], recv_sem=recv.at[3 * k + j],
                                                device_id=(x, y, 1 - c), device_id_type=MESH)

        sent = [d2d(k, j, c) for k in range(n) for j in range(3)]
        for cp in sent:
            cp.start()
        for k in range(n):
            for j in range(3):
                d2d(k, j, 1 - c).wait_recv()
        for cp in sent:
            cp.wait_send()

    return pl.pallas_call(
        body,
        name=name,
        in_specs=[ANY] * (n + len(after)),
        out_specs=[ANY] * n,
        out_shape=[jax.ShapeDtypeStruct(a.shape, a.dtype) for a in bufs],
        input_output_aliases={k: k for k in range(n)},
        scratch_shapes=[pltpu.SemaphoreType.DMA((3 * n,)), pltpu.SemaphoreType.DMA((3 * n,))],
        compiler_params=pltpu.CompilerParams(has_side_effects=True),
    )(*bufs, *after)


def _hand_on_start(bufs, name, after=()):
    n = len(bufs)

    def body(*refs):
        refs = refs[:n] + refs[n + len(after):]
        send, recv = refs[n], refs[n + 1]
        outs = refs[n + 2:2 * n + 2]
        refs[2 * n + 2][...] = jnp.zeros(TOKEN.shape, TOKEN.dtype)
        x, y, c, chips = _place()
        for k in range(n):
            for j, (px, py) in enumerate(chips):
                blk = outs[k].at[2 * px + py, _gather_rows(bufs[k], False, c), :]
                pltpu.make_async_remote_copy(src_ref=blk, dst_ref=blk, send_sem=send.at[3 * k + j], recv_sem=recv.at[3 * k + j],
                                             device_id=(x, y, 1 - c), device_id_type=MESH).start()

    res = pl.pallas_call(
        body,
        name=name,
        in_specs=[HBM] * n + [ANY] * len(after),
        out_specs=[SEM, SEM] + [HBM] * n + [VMEM_SPEC],
        out_shape=[pltpu.SemaphoreType.DMA((3 * n,))] * 2 + [pltpu.HBM(a.shape, a.dtype) for a in bufs] + [TOKEN],
        input_output_aliases={k: 2 + k for k in range(n)},
        compiler_params=pltpu.CompilerParams(has_side_effects=EFFECT),
    )(*[_in_hbm(a) for a in bufs], *after)
    return res[0], res[1], list(res[2:2 + n]), res[2 + n]


def _hand_on_wait(bufs, send, recv, after, name):
    n = len(bufs)

    def body(*refs):
        ins = refs[:n]
        send_ref, recv_ref = refs[n], refs[n + 1]
        x, y, c, chips = _place()
        for k in range(n):
            for j, (px, py) in enumerate(chips):
                q = 2 * px + py
                cp = pltpu.make_async_remote_copy(src_ref=ins[k].at[q, _gather_rows(bufs[k], False, c), :],
                                                  dst_ref=ins[k].at[q, _gather_rows(bufs[k], False, 1 - c), :],
                                                  send_sem=send_ref.at[3 * k + j], recv_sem=recv_ref.at[3 * k + j],
                                                  device_id=(x, y, 1 - c), device_id_type=MESH)
                cp.wait_send()
                cp.wait_recv()

    res = pl.pallas_call(
        body,
        name=name,
        in_specs=[HBM] * n + [SEM, SEM] + [ANY] * len(after),
        out_specs=[HBM] * n,
        out_shape=[pltpu.HBM(a.shape, a.dtype) for a in bufs],
        input_output_aliases={k: k for k in range(n)},
        compiler_params=pltpu.CompilerParams(has_side_effects=EFFECT),
    )(*bufs, send, recv, *after)
    return list(res)


def _pair_exchange_start(parts, name):
    n = len(parts)
    lands = [(N_CHIPS, a.shape[1] // 2, a.shape[2]) for a in parts]

    def body(*refs):
        send, recv = refs[2 * n], refs[2 * n + 1]
        src, dst = refs[2 * n + 2:3 * n + 2], refs[3 * n + 2:4 * n + 2]
        refs[4 * n + 2][...] = jnp.zeros(TOKEN.shape, TOKEN.dtype)
        x, y, c, _ = _place()
        for k in range(n):
            rh = lands[k][1]
            pltpu.make_async_remote_copy(src_ref=src[k].at[:, pl.ds((1 - c) * rh, rh), :], dst_ref=dst[k],
                                         send_sem=send.at[k], recv_sem=recv.at[k],
                                         device_id=(x, y, 1 - c), device_id_type=MESH).start()

    res = pl.pallas_call(
        body,
        name=name,
        in_specs=[HBM] * (2 * n),
        out_specs=[SEM, SEM] + [HBM] * (2 * n) + [VMEM_SPEC],
        out_shape=[pltpu.SemaphoreType.DMA((n,))] * 2 + [pltpu.HBM(a.shape, a.dtype) for a in parts]
        + [pltpu.HBM(s, a.dtype) for s, a in zip(lands, parts)] + [TOKEN],
        input_output_aliases={k: 2 + k for k in range(2 * n)},
        compiler_params=pltpu.CompilerParams(has_side_effects=EFFECT),
    )(*[_in_hbm(a) for a in parts], *[_in_hbm(lax.empty(s, a.dtype)) for s, a in zip(lands, parts)])
    return res[0], res[1], list(res[2:2 + n]), list(res[2 + n:2 + 2 * n]), res[2 + 2 * n]


def _pair_exchange_wait(send, recv, parts, lands, after, name):
    n = len(parts)

    def body(*refs):
        src, dst = refs[:n], refs[n:2 * n]
        send_ref, recv_ref = refs[2 * n], refs[2 * n + 1]
        x, y, c, _ = _place()
        for k in range(n):
            rh = parts[k].shape[1] // 2
            cp = pltpu.make_async_remote_copy(src_ref=src[k].at[:, pl.ds((1 - c) * rh, rh), :], dst_ref=dst[k],
                                              send_sem=send_ref.at[k], recv_sem=recv_ref.at[k],
                                              device_id=(x, y, 1 - c), device_id_type=MESH)
            cp.wait_send()
            cp.wait_recv()

    res = pl.pallas_call(
        body,
        name=name,
        in_specs=[HBM] * (2 * n) + [SEM, SEM] + [ANY] * len(after),
        out_specs=[HBM] * (2 * n),
        out_shape=[pltpu.HBM(a.shape, a.dtype) for a in parts] + [pltpu.HBM(a.shape, a.dtype) for a in lands],
        input_output_aliases={k: k for k in range(2 * n)},
        compiler_params=pltpu.CompilerParams(has_side_effects=EFFECT),
    )(*parts, *lands, send, recv, *after)
    return list(res[:n]), list(res[n:])


def _chip_exchange_start(sums, name):
    n = len(sums)

    def body(*refs):
        send, recv = refs[2 * n], refs[2 * n + 1]
        src = refs[2 * n + 2:3 * n + 2]
        dst = refs[3 * n + 2:4 * n + 2]
        refs[4 * n + 2][...] = jnp.zeros(TOKEN.shape, TOKEN.dtype)
        x, y, c, chips = _place()
        p = 2 * x + y
        for k in range(n):
            for j, (px, py) in enumerate(chips):
                pltpu.make_async_remote_copy(src_ref=src[k].at[2 * px + py], dst_ref=dst[k].at[p],
                                             send_sem=send.at[3 * k + j], recv_sem=recv.at[3 * k + j],
                                             device_id=(px, py, c), device_id_type=MESH).start()

    res = pl.pallas_call(
        body,
        name=name,
        in_specs=[HBM] * (2 * n),
        out_specs=[SEM, SEM] + [HBM] * (2 * n) + [VMEM_SPEC],
        out_shape=[pltpu.SemaphoreType.DMA((3 * n,))] * 2 + [pltpu.HBM(a.shape, a.dtype) for a in sums] * 2 + [TOKEN],
        input_output_aliases={k: 2 + k for k in range(2 * n)},
        compiler_params=pltpu.CompilerParams(has_side_effects=EFFECT),
    )(*[_in_hbm(a) for a in sums], *[_in_hbm(lax.empty(a.shape, a.dtype)) for a in sums])
    return res[0], res[1], list(res[2:2 + n]), list(res[2 + n:2 + 2 * n]), res[2 + 2 * n]


def _chip_exchange_wait(sums, slots, send, recv, after, name):
    n = len(sums)

    def body(*refs):
        src, dst = refs[:n], refs[n:2 * n]
        send_ref, recv_ref = refs[2 * n], refs[2 * n + 1]
        x, y, c, chips = _place()
        for k in range(n):
            for j, (px, py) in enumerate(chips):
                cp = pltpu.make_async_remote_copy(src_ref=src[k].at[2 * px + py], dst_ref=dst[k].at[2 * px + py],
                                                  send_sem=send_ref.at[3 * k + j], recv_sem=recv_ref.at[3 * k + j],
                                                  device_id=(px, py, c), device_id_type=MESH)
                cp.wait_send()
                cp.wait_recv()

    res = pl.pallas_call(
        body,
        name=name,
        in_specs=[HBM] * (2 * n) + [SEM, SEM] + [ANY] * len(after),
        out_specs=[HBM] * (2 * n),
        out_shape=[pltpu.HBM(a.shape, a.dtype) for a in sums] * 2,
        input_output_aliases={k: k for k in range(2 * n)},
        compiler_params=pltpu.CompilerParams(has_side_effects=EFFECT),
    )(*sums, *slots, send, recv, *after)
    return list(res[:n]), list(res[n:])


def _pair_sum(core, part, landed, name):
    _, r, cdim = part.shape
    rh = r // 2
    tr = _row_tile(rh, cdim, 2)
    nb = rh // tr

    def body(c_ref, a_ref, b_ref, o_ref):
        o_ref[...] = (a_ref[...].astype(F32) + b_ref[...].astype(F32)).astype(o_ref.dtype)

    return pl.pallas_call(
        body,
        name=name,
        grid_spec=pltpu.PrefetchScalarGridSpec(
            num_scalar_prefetch=1,
            grid=(N_CHIPS, nb),
            in_specs=[pl.BlockSpec((None, tr, cdim), lambda q, i, c_ref: (q, c_ref[0] * nb + i, 0)),
                      pl.BlockSpec((None, tr, cdim), lambda q, i, c_ref: (q, i, 0))],
            out_specs=pl.BlockSpec((None, tr, cdim), lambda q, i, c_ref: (q, i, 0))),
        out_shape=jax.ShapeDtypeStruct((N_CHIPS, rh, cdim), BF16),
        compiler_params=_params(("parallel", "parallel")),
    )(core, part, landed)


def _chip_sum(place, own, slots, name, after=()):
    _, rh, cdim = slots.shape
    tr = _row_tile(rh, cdim, 2)
    nb = rh // tr

    def body(place_ref, own_ref, s1_ref, s2_ref, s3_ref, *rest):
        o_ref = rest[len(after)]
        acc = own_ref[...].astype(F32)
        for s_ref in (s1_ref, s2_ref, s3_ref):
            acc = acc + s_ref[...].astype(F32)
        o_ref[...] = acc

    def other(j):
        return lambda i, place_ref: ((place_ref[0] + j) % N_CHIPS, i, 0)

    return pl.pallas_call(
        body,
        name=name,
        grid_spec=pltpu.PrefetchScalarGridSpec(
            num_scalar_prefetch=1,
            grid=(nb,),
            in_specs=[pl.BlockSpec((None, tr, cdim), other(j)) for j in (0, 1, 2, 3)] + [ANY] * len(after),
            out_specs=pl.BlockSpec((tr, cdim), lambda i, place_ref: (place_ref[1] * nb + i, 0))),
        out_shape=jax.ShapeDtypeStruct((2 * rh, cdim), F32),
        compiler_params=_params(("parallel",)),
    )(place, own, slots, slots, slots, *after)


def _half_exchange_start(buf, name):
    rh = buf.shape[0] // 2

    def body(buf_ref, send, recv, out_ref, token):
        token[...] = jnp.zeros_like(token)
        x, y, c, _ = _place()
        mine = out_ref.at[pl.ds(c * rh, rh), :]
        pltpu.make_async_remote_copy(src_ref=mine, dst_ref=mine, send_sem=send, recv_sem=recv,
                                     device_id=(x, y, 1 - c), device_id_type=MESH).start()

    return pl.pallas_call(
        body,
        name=name,
        in_specs=[HBM],
        out_specs=[SEM, SEM, HBM, VMEM_SPEC],
        out_shape=[pltpu.SemaphoreType.DMA(()), pltpu.SemaphoreType.DMA(()), pltpu.HBM(buf.shape, buf.dtype), TOKEN],
        input_output_aliases={0: 2},
        compiler_params=pltpu.CompilerParams(has_side_effects=EFFECT),
    )(_in_hbm(buf))


def _half_exchange_wait(send, recv, buf, after, name):
    rh = buf.shape[0] // 2

    def body(buf_ref, send_ref, recv_ref, *rest):
        x, y, c, _ = _place()
        cp = pltpu.make_async_remote_copy(src_ref=buf_ref.at[pl.ds(c * rh, rh), :], dst_ref=buf_ref.at[pl.ds((1 - c) * rh, rh), :],
                                          send_sem=send_ref, recv_sem=recv_ref, device_id=(x, y, 1 - c), device_id_type=MESH)
        cp.wait_send()
        cp.wait_recv()

    return pl.pallas_call(
        body,
        name=name,
        in_specs=[HBM, SEM, SEM] + [ANY] * len(after),
        out_specs=HBM,
        out_shape=pltpu.HBM(buf.shape, buf.dtype),
        input_output_aliases={0: 0},
        compiler_params=pltpu.CompilerParams(has_side_effects=EFFECT),
    )(buf, send, recv, *after)


def _half_exchange(bufs, name):
    n = len(bufs)

    def body(*refs):
        outs = refs[n:2 * n]
        send, recv = refs[2 * n:]
        x, y, c, _ = _place()
        copies = []
        for k in range(n):
            rh = bufs[k].shape[0] // 2
            mine = outs[k].at[pl.ds(c * rh, rh), :]
            cp = pltpu.make_async_remote_copy(src_ref=mine, dst_ref=mine, send_sem=send.at[k], recv_sem=recv.at[k],
                                              device_id=(x, y, 1 - c), device_id_type=MESH)
            cp.start()
            copies.append(cp)
        for k in range(n):
            rh = bufs[k].shape[0] // 2
            theirs = outs[k].at[pl.ds((1 - c) * rh, rh), :]
            copies[k].wait_send()
            pltpu.make_async_remote_copy(src_ref=theirs, dst_ref=theirs, send_sem=send.at[k], recv_sem=recv.at[k],
                                         device_id=(x, y, 1 - c), device_id_type=MESH).wait_recv()

    return pl.pallas_call(
        body,
        name=name,
        in_specs=[ANY] * n,
        out_specs=[ANY] * n,
        out_shape=[jax.ShapeDtypeStruct(a.shape, a.dtype) for a in bufs],
        input_output_aliases={k: k for k in range(n)},
        scratch_shapes=[pltpu.SemaphoreType.DMA((n,)), pltpu.SemaphoreType.DMA((n,))],
        compiler_params=pltpu.CompilerParams(has_side_effects=True),
    )(*bufs)


SMALL_ROWS = 32


def _small_peers():
    x, y, c, _ = _place()
    peers = []
    for k in range(1, N_DEV):
        px, py, pc = x ^ ((k >> 2) & 1), y ^ ((k >> 1) & 1), c ^ (k & 1)
        peers.append((k, (px, py, pc), 4 * px + 2 * py + pc))
    return 4 * x + 2 * y + c, peers


def _all_gather_small_start(gath):
    def body(in_ref, send, recv, out_ref, token):
        token[...] = jnp.zeros_like(token)
        me, peers = _small_peers()
        for k, peer, _ in peers:
            pltpu.make_async_remote_copy(src_ref=out_ref.at[me], dst_ref=out_ref.at[me], send_sem=send.at[k], recv_sem=recv.at[k],
                                         device_id=peer, device_id_type=MESH).start()

    return pl.pallas_call(
        body,
        name="small_gather_start",
        in_specs=[HBM],
        out_specs=[SEM, SEM, HBM, VMEM_SPEC],
        out_shape=[pltpu.SemaphoreType.DMA((N_DEV,)), pltpu.SemaphoreType.DMA((N_DEV,)), pltpu.HBM(gath.shape, gath.dtype), TOKEN],
        input_output_aliases={0: 2},
        compiler_params=pltpu.CompilerParams(has_side_effects=EFFECT),
    )(_in_hbm(gath))


def _all_gather_small_wait(send, recv, gath, after):
    def body(in_ref, send_ref, recv_ref, *rest):
        me, peers = _small_peers()
        for k, peer, peer_id in peers:
            cp = pltpu.make_async_remote_copy(src_ref=in_ref.at[me], dst_ref=in_ref.at[peer_id], send_sem=send_ref.at[k],
                                              recv_sem=recv_ref.at[k], device_id=peer, device_id_type=MESH)
            cp.wait_send()
            cp.wait_recv()

    return pl.pallas_call(
        body,
        name="small_gather_wait",
        in_specs=[HBM, SEM, SEM] + [ANY] * len(after),
        out_specs=HBM,
        out_shape=pltpu.HBM(gath.shape, gath.dtype),
        input_output_aliases={0: 0},
        compiler_params=pltpu.CompilerParams(has_side_effects=EFFECT),
    )(gath, send, recv, *after)


def _sum_small(gath):
    def body(g_ref, o_ref):
        acc = g_ref[0]
        for dev in range(1, N_DEV):
            acc = acc + g_ref[dev]
        o_ref[...] = acc

    return pl.pallas_call(
        body,
        name="small_sum",
        in_specs=[VMEM_SPEC],
        out_specs=VMEM_SPEC,
        out_shape=jax.ShapeDtypeStruct(gath.shape[1:], F32),
        compiler_params=pltpu.CompilerParams(vmem_limit_bytes=VMEM_LIMIT),
    )(gath)


def _adamw_update(g_ref, w_ref, m_ref, v_ref, go_ref, d_ref, mo_ref, vo_ref):
    bc1 = 1.0 - ADAM_B1 ** ADAM_STEP
    bc2 = 1.0 - ADAM_B2 ** ADAM_STEP
    gv = g_ref[...]
    mn = ADAM_B1 * m_ref[...] + (1.0 - ADAM_B1) * gv
    vn = ADAM_B2 * v_ref[...] + (1.0 - ADAM_B2) * (gv * gv)
    go_ref[...] = gv
    mo_ref[...] = mn
    vo_ref[...] = vn
    d_ref[...] = -ADAM_LR * ((mn / bc1) / (jnp.sqrt(vn / bc2) + ADAM_EPS) + ADAM_WD * w_ref[...])


def _adamw_small(gs, ws, ms, vs):
    n = len(gs)

    def body(*refs):
        ins, outs = refs[:4 * n], refs[4 * n:]
        for k in range(n):
            _adamw_update(*[ins[i * n + k] for i in range(4)], *outs[4 * k:4 * k + 4])

    vmem = pl.BlockSpec(memory_space=pltpu.VMEM)
    res = pl.pallas_call(
        body,
        name="adamw_small",
        in_specs=[vmem] * (4 * n),
        out_specs=[vmem] * (4 * n),
        out_shape=[jax.ShapeDtypeStruct(w.shape, F32) for w in ws for _ in range(4)],
        compiler_params=pltpu.CompilerParams(vmem_limit_bytes=VMEM_LIMIT),
    )(*gs, *ws, *ms, *vs)
    return [tuple(res[4 * k:4 * k + 4]) for k in range(n)]


def _adamw(g, w, m, v, name, g_block=0):
    r, cdim = w.shape
    tr = _row_tile(r, cdim, 4)

    def body(*refs):
        _adamw_update(*refs)

    spec = pl.BlockSpec((tr, cdim), lambda i: (i, 0))
    return pl.pallas_call(
        body,
        name=name,
        grid=(r // tr,),
        in_specs=[pl.BlockSpec((tr, cdim), lambda i: (i, g_block))] + [spec] * 3,
        out_specs=[spec] * 4,
        out_shape=[jax.ShapeDtypeStruct((r, cdim), F32)] * 4,
        compiler_params=_params(("parallel",)),
    )(g, w, m, v)


VEC_NAMES = ["pre_mix_norm", "q_norm", "kv_norm", "conv_b", "conv_ln_g", "conv_ln_b", "conv_out_norm",
             "attn_out_norm", "post_mix_norm", "pre_ffn_norm", "post_ffn_norm"]
LOSS_ROW = len(VEC_NAMES)
CONV_W_ROW = 16


def _cols_to_full(parts):
    _, r, cdim = parts.shape
    return parts.transpose(1, 0, 2).reshape(r, N_CHIPS * cdim)


def _full_to_cols(full):
    r, n = full.shape
    return full.reshape(r, N_CHIPS, n // N_CHIPS).transpose(1, 0, 2)


W_IN_SHARD = (2 * CONV_CH + Q_LORA + KV_LORA + QK_ROPE) // N_CHIPS
W_IN_PART = 1024
W_IN_BLOCKS = (2 * CONV_CH + Z2_COLS) // LANES
W_IN_BASE = [p * W_IN_SHARD // LANES for p in range(N_CHIPS)]
W_IN_SPAN = [-(-(p * W_IN_SHARD % LANES + W_IN_SHARD) // LANES) for p in range(N_CHIPS)]


def _w_in_block_home(b):
    n = CONV_CH // LANES
    if b < n:
        return 0, 2 * b
    if b < 2 * n:
        return 0, 2 * (b - n) + 1
    return 1, b - 2 * n


def _to_parts_w_in(shift_chip, w_in_t):
    r = w_in_t.shape[1]
    tr = 512

    def body(s_ref, a_ref, o_ref, scr):
        scr[...] = jnp.zeros_like(scr)
        scr[pl.ds(pl.multiple_of(s_ref[0], 8), W_IN_SHARD), :] = a_ref[...]
        o_ref[...] = scr[...].T.astype(o_ref.dtype)

    return pl.pallas_call(
        body,
        name="to_parts_w_in",
        grid_spec=pltpu.PrefetchScalarGridSpec(
            num_scalar_prefetch=1,
            grid=(r // tr,),
            in_specs=[pl.BlockSpec((W_IN_SHARD, tr), lambda i, s_ref: (0, i))],
            out_specs=pl.BlockSpec((None, tr, W_IN_PART), lambda i, s_ref: (s_ref[1], i, 0)),
            scratch_shapes=[pltpu.VMEM((W_IN_PART, tr), F32)]),
        out_shape=jax.ShapeDtypeStruct((N_CHIPS, r, W_IN_PART), BF16),
        compiler_params=_params(("parallel",)),
    )(shift_chip, w_in_t)


def _adamw_w_in(shift, g_shifted, w_t, m_t, v_t):
    cols = ROW_TILE

    def body(s_ref, g_ref, w_ref, m_ref, v_ref, go_ref, d_ref, mo_ref, vo_ref, scr):
        scr[...] = g_ref[...].T
        _adamw_update(scr.at[pl.ds(pl.multiple_of(s_ref[0], 8), W_IN_SHARD), :], w_ref, m_ref, v_ref, go_ref, d_ref, mo_ref, vo_ref)

    spec = pl.BlockSpec((W_IN_SHARD, cols), lambda i, s_ref: (0, i))
    return pl.pallas_call(
        body,
        name="adamw_w_in",
        grid_spec=pltpu.PrefetchScalarGridSpec(
            num_scalar_prefetch=1,
            grid=(w_t.shape[1] // cols,),
            in_specs=[pl.BlockSpec((cols, W_IN_PART), lambda i, s_ref: (i, 0))] + [spec] * 3,
            out_specs=[spec] * 4,
            scratch_shapes=[pltpu.VMEM((W_IN_PART, cols), F32)]),
        out_shape=[jax.ShapeDtypeStruct(w_t.shape, F32)] * 4,
        compiler_params=_params(("parallel",)),
    )(shift, g_shifted, w_t, m_t, v_t)


def _assemble_w_in(parts):
    r = parts.shape[1]
    tr = ROW_TILE

    def body(p_ref, ag_ref, z2_ref):
        outs = (ag_ref, z2_ref)
        for b in range(W_IN_BLOCKS):
            blk = None
            for p in range(N_CHIPS):
                i = b - W_IN_BASE[p]
                if 0 <= i < W_IN_SPAN[p]:
                    piece = p_ref[p, :, i * LANES:(i + 1) * LANES]
                    blk = piece if blk is None else blk + piece
            which, at = _w_in_block_home(b)
            outs[which][:, at * LANES:(at + 1) * LANES] = blk

    w_ag, w_z2 = pl.pallas_call(
        body,
        name="assemble_w_in",
        grid=(r // tr,),
        in_specs=[pl.BlockSpec((N_CHIPS, tr, W_IN_PART), lambda i: (0, i, 0))],
        out_specs=[pl.BlockSpec((tr, 2 * CONV_CH), lambda i: (i, 0)), pl.BlockSpec((tr, Z2_COLS), lambda i: (i, 0))],
        out_shape=[jax.ShapeDtypeStruct((r, 2 * CONV_CH), parts.dtype), jax.ShapeDtypeStruct((r, Z2_COLS), parts.dtype)],
        compiler_params=_params(("parallel",)),
    )(parts)
    return dict(w_ag=w_ag, w_z2=w_z2)


def _w_in_grad_parts(dw_ag, dw_z2):
    r = dw_ag.shape[0]
    tr = ROW_TILE

    def body(ag_ref, z2_ref, o_ref):
        ins = (ag_ref, z2_ref)
        for p in range(N_CHIPS):
            for i in range(W_IN_PART // LANES):
                if i < W_IN_SPAN[p]:
                    which, at = _w_in_block_home(W_IN_BASE[p] + i)
                    o_ref[p, :, i * LANES:(i + 1) * LANES] = ins[which][:, at * LANES:(at + 1) * LANES]
                else:
                    o_ref[p, :, i * LANES:(i + 1) * LANES] = jnp.zeros((tr, LANES), o_ref.dtype)

    return pl.pallas_call(
        body,
        name="w_in_grad_parts",
        grid=(r // tr,),
        in_specs=[pl.BlockSpec((tr, 2 * CONV_CH), lambda i: (i, 0)), pl.BlockSpec((tr, Z2_COLS), lambda i: (i, 0))],
        out_specs=pl.BlockSpec((N_CHIPS, tr, W_IN_PART), lambda i: (0, i, 0)),
        out_shape=jax.ShapeDtypeStruct((N_CHIPS, r, W_IN_PART), dw_ag.dtype),
        compiler_params=_params(("parallel",)),
    )(dw_ag, dw_z2)


def _assemble_mixer_rest(g):
    uq = _cols_to_full(g["w_uq"]).reshape(Q_LORA, N_HEADS, QK_HEAD)
    w_uq = jnp.concatenate([uq[:, :, :QK_NOPE].reshape(Q_LORA, N_HEADS * QK_NOPE),
                            uq[:, :, QK_NOPE:].reshape(Q_LORA, N_HEADS * QK_ROPE)], axis=1)
    return dict(w_uq=w_uq, w_ukv=_cols_to_full(g["w_ukv"]), conv_w=_cols_to_full(g["conv_w"]),
                w_out=g["w_out"].reshape(-1, g["w_out"].shape[2]))


def _grads_to_parts(dw):
    uq = dw["w_uq"]
    d_uq = jnp.concatenate([uq[:, :N_HEADS * QK_NOPE].reshape(Q_LORA, N_HEADS, QK_NOPE),
                            uq[:, N_HEADS * QK_NOPE:].reshape(Q_LORA, N_HEADS, QK_ROPE)], axis=2).reshape(Q_LORA, N_HEADS * QK_HEAD)
    return dict(w_in=_w_in_grad_parts(dw["w_ag"], dw["w_z2"]), w_uq=_full_to_cols(d_uq), w_ukv=_full_to_cols(dw["w_ukv"]),
                w_out=dw["w_out"].reshape(N_CHIPS, -1, dw["w_out"].shape[1]))


MIXER = ["w_in", "w_uq", "w_ukv", "w_out"]
FFN = ["w_gu", "w_down"]


def _pad_lanes(v, n):
    return jnp.pad(v, ((0, 0), (0, n - v.shape[1])))


def kernel(x, positions, pre_mix_norm, w_in, q_norm, w_uq, kv_norm, w_ukv, conv_w, conv_b, conv_ln_g, conv_ln_b, conv_out_norm, attn_out_norm, w_out, post_mix_norm, pre_ffn_norm, w_gate, w_up, w_down, post_ffn_norm, loss_target, m_pre_mix_norm, m_w_in, m_q_norm, m_w_uq, m_kv_norm, m_w_ukv, m_conv_w, m_conv_b, m_conv_ln_g, m_conv_ln_b, m_conv_out_norm, m_attn_out_norm, m_w_out, m_post_mix_norm, m_pre_ffn_norm, m_w_gate, m_w_up, m_w_down, m_post_ffn_norm, v_pre_mix_norm, v_w_in, v_q_norm, v_w_uq, v_kv_norm, v_w_ukv, v_conv_w, v_conv_b, v_conv_ln_g, v_conv_ln_b, v_conv_out_norm, v_attn_out_norm, v_w_out, v_post_mix_norm, v_pre_ffn_norm, v_w_gate, v_w_up, v_w_down, v_post_ffn_norm):
    given = dict(locals())
    names = ["pre_mix_norm", "w_in", "q_norm", "w_uq", "kv_norm", "w_ukv", "conv_w", "conv_b", "conv_ln_g", "conv_ln_b",
             "conv_out_norm", "attn_out_norm", "w_out", "post_mix_norm", "pre_ffn_norm", "w_gate", "w_up", "w_down", "post_ffn_norm"]
    def as_2d(a):
        return a if a.ndim == 2 else a[0]

    weights = {n: as_2d(given[n]) for n in names}
    mom = {n: as_2d(given["m_" + n]) for n in names}
    var = {n: as_2d(given["v_" + n]) for n in names}
    d = D_MODEL

    inv_freq = ROPE_THETA ** (-jnp.arange(0, QK_ROPE, 2, dtype=F32) / QK_ROPE)
    ang = positions[0].astype(F32)[:, None] * inv_freq
    cos, sin = jnp.cos(ang), jnp.sin(ang)
    cos2 = jnp.concatenate([cos, cos, cos, cos], axis=1)
    sin2 = jnp.concatenate([-sin, sin, -sin, sin], axis=1)

    chip = 2 * lax.axis_index("x") + lax.axis_index("y")
    core = lax.axis_index("c")
    chip1 = chip.astype(jnp.int32).reshape(1)
    pieces = {n: [weights[n]] for n in ("w_uq", "w_ukv", "w_out", "w_down")}
    pieces["w_gu"] = [weights["w_gate"], weights["w_up"]]
    core1 = core.astype(jnp.int32).reshape(1)
    place = jnp.stack([chip, core]).astype(jnp.int32)
    rest = ["w_uq", "w_ukv", "w_out"]
    w_in_shift = (chip * W_IN_SHARD) % LANES
    w_in_t, m_in_t, v_in_t = (jnp.swapaxes(given[n][0], 0, 1) for n in ("w_in", "m_w_in", "v_w_in"))
    w_in_buf = _to_parts_w_in(jnp.stack([w_in_shift, chip]).astype(jnp.int32), w_in_t)
    (w_in_sems,), w_in_thru, _ = _gather_start([w_in_buf], [False], [[0]], "gather_start_w_in")
    rest_bufs = [_to_parts(chip1, pieces[n], BF16, "to_parts_" + n) for n in rest]
    rest_bufs.append(_to_parts(chip1, [jnp.pad(weights["conv_w"], ((0, CONV_K_PAD - CONV_K), (0, 0)))], F32, "to_parts_conv_w"))
    rest_whole = [False] * 3 + [True]
    ffn_bufs = [_to_parts(chip1, pieces[n], BF16, "to_parts_" + n) for n in FFN]
    got = _gather_wait(w_in_thru, [False], *w_in_sems, rest_bufs + ffn_bufs, "gather_wait_w_in")
    (rest_sems,), rest_thru, started = _gather_start(rest_bufs, rest_whole, [[0, 1, 2, 3]], "gather_start_mixer_rest", after=got)
    full = _assemble_w_in(_gather_hand_on(got, "gather_hand_on_w_in", after=[started])[0])
    vec = {n: weights[n] for n in VEC_NAMES}
    rs = {}

    def get_mixer_rest(after):
        got = _gather_wait(rest_thru, rest_whole, *rest_sems, [after], "gather_wait_mixer_rest")
        (rs["w_gu_sems"],), rs["w_gu_thru"], started = _gather_start(ffn_bufs[:1], [False], [[0]], "gather_start_w_gu", after=got[:1])
        got = list(_gather_hand_on(got[:3], "gather_hand_on_mixer_rest", after=[started])) + [got[3]]
        return _assemble_mixer_rest(dict(zip(rest + ["conv_w"], got)))

    def w_gu_landed(after):
        got = _gather_wait(rs["w_gu_thru"], [False], *rs["w_gu_sems"], [after], "gather_wait_w_gu")
        (rs["w_down_sems"],), rs["w_down_thru"], started = _gather_start(ffn_bufs[1:], [False], [[0]], "gather_start_w_down", after=got)
        rs["w_gu"] = _hand_on_start(got, "hand_on_start_w_gu", after=[started])
        return rs["w_gu"][3]

    def get_w_gu(after):
        send, recv, bufs, _ = rs["w_gu"]
        return _hand_on_wait(bufs, send, recv, [after], "hand_on_wait_w_gu")[0]

    def get_w_down(after):
        got = _gather_wait(rs["w_down_thru"], [False], *rs["w_down_sems"], [after], "gather_wait_w_down")
        got = _gather_hand_on(got, "gather_hand_on_w_down")[0]
        return got.reshape(-1, got.shape[2])

    def pair_start(key):
        def hook(dw):
            rs[key] = _pair_exchange_start([dw.reshape(N_CHIPS, -1, dw.shape[-1])], "grad_pair_start_" + key)
            return rs[key][4]
        return hook

    def reduce_start(group, plist, landed):
        sums = [_pair_sum(core1, a, b, "grad_pair_sum_%s_%d" % (group, k)) for k, (a, b) in enumerate(zip(plist, landed))]
        rs[group] = _chip_exchange_start(sums, "grad_chip_exchange_start_" + group)
        return rs[group][4]

    def reduce_finish(group, after):
        send, recv, sums, slots, _ = rs[group]
        sums, slots = _chip_exchange_wait(sums, slots, send, recv, after, "grad_chip_exchange_wait_" + group)
        halves = [_chip_sum(place, s, sl, "grad_chip_sum_%s_%d" % (group, k)) for k, (s, sl) in enumerate(zip(sums, slots))]
        return list(_half_exchange(halves, "grad_half_exchange_" + group))

    def ffn_grads_exchanged(after):
        pairs = [_pair_exchange_wait(*rs[key][:4], [after], "grad_pair_wait_" + key) for key in ("dw_gu", "dw_down")]
        return reduce_start("ffn", [p[0][0] for p in pairs], [p[1][0] for p in pairs])

    def mixer_grads(dw_mixer):
        parts = _grads_to_parts(dw_mixer)
        rs["mixer_pairs"] = _pair_exchange_start([parts[n] for n in MIXER], "grad_pair_start_mixer")
        return rs["mixer_pairs"][4]

    hooks = dict(mixer_rest=get_mixer_rest, w_gu_landed=w_gu_landed, w_gu=get_w_gu, w_down=get_w_down, mixer_grads=mixer_grads, dw_down=pair_start("dw_down"), dw_gu=pair_start("dw_gu"),
                 grads_exchanged=ffn_grads_exchanged)
    loss, grad_x, dw, dvec = _local_step(x[0], loss_target[0], cos2, sin2, vec, full, hooks)

    rows = [_pad_lanes(dvec[n], d) for n in VEC_NAMES] + [_pad_lanes(loss, d)]
    rows.append(jnp.zeros((CONV_W_ROW - len(rows), d), F32))
    rows.append(dw["conv_w"].reshape(SMALL_ROWS - CONV_W_ROW, d))
    device1 = (2 * chip + core).astype(jnp.int32).reshape(1)
    small_gather = _all_gather_small_start(_to_parts(device1, [jnp.concatenate(rows, axis=0)], F32, "small_to_slot", n_parts=N_DEV))

    plist, landed = _pair_exchange_wait(*rs["mixer_pairs"][:4], [grad_x], "grad_pair_wait_mixer")
    started = reduce_start("mixer", plist, landed)
    send, recv, sums, slots, _ = rs["ffn"]
    sums, slots = _chip_exchange_wait(sums, slots, send, recv, [started, small_gather[3]], "grad_chip_exchange_wait_ffn")
    down = _half_exchange_start(_chip_sum(place, sums[1], slots[1], "grad_chip_sum_ffn_1"), "grad_half_start_w_down")
    gu = _half_exchange_start(_chip_sum(place, sums[0], slots[0], "grad_chip_sum_ffn_0", after=[down[3]]), "grad_half_start_w_gu")
    res = {}
    g_down = _half_exchange_wait(*down[:3], [gu[3]], "grad_half_wait_w_down")
    res["w_down"] = _adamw(g_down, weights["w_down"], mom["w_down"], var["w_down"], "adamw_w_down")
    g_gu = _half_exchange_wait(*gu[:3], [res["w_down"][1]], "grad_half_wait_w_gu")
    for n, blk in (("w_gate", 0), ("w_up", 1)):
        res[n] = _adamw(g_gu, weights[n], mom[n], var[n], "adamw_" + n, g_block=blk)
    small = _sum_small(_all_gather_small_wait(*small_gather[:3], [res["w_up"][1]]))
    g_conv_w_full = small[CONV_W_ROW:].reshape(CONV_K_PAD, CONV_CH)
    g_small = {n: small[i:i + 1, :weights[n].shape[1]] for i, n in enumerate(VEC_NAMES)}
    g_small["conv_w"] = lax.dynamic_slice(g_conv_w_full, (0, chip * (CONV_CH // N_CHIPS)), (CONV_K_PAD, CONV_CH // N_CHIPS))[:CONV_K]
    loss_out = small[LOSS_ROW, 0]
    small_names = VEC_NAMES + ["conv_w"]
    res.update(zip(small_names, _adamw_small([g_small[n] for n in small_names], [weights[n] for n in small_names],
                                             [mom[n] for n in small_names], [var[n] for n in small_names])))
    done_meanwhile = [res["w_gate"][1], res["w_up"][1], res["w_down"][1], res["conv_w"][1], grad_x]
    g_mixer = reduce_finish("mixer", done_meanwhile)
    in_t = _adamw_w_in(w_in_shift.astype(jnp.int32).reshape(1), g_mixer[0], w_in_t, m_in_t, v_in_t)
    res["w_in"] = tuple(jnp.swapaxes(a, 0, 1) for a in in_t)
    for n, g in zip(MIXER[1:], g_mixer[1:]):
        res[n] = _adamw(g, weights[n], mom[n], var[n], "adamw_" + n)
    outs = [loss_out, grad_x[None]]
    for i in range(4):
        outs += [res[n][i].reshape(given[n].shape) for n in names]
    return tuple(outs)
```

```python
import functools

import jax
import jax.numpy as jnp
from jax import lax
from jax.experimental import pallas as pl
from jax.experimental.pallas import tpu as pltpu

F32 = jnp.float32
BF16 = jnp.bfloat16

D_MODEL = 2048
CONV_CH = 1024
CONV_K = 31
CONV_K_PAD = 32
N_HEADS = 8
QK_NOPE = 128
QK_ROPE = 64
V_HEAD = 128
QK_HEAD = QK_NOPE + QK_ROPE
Q_LORA = 768
KV_LORA = 512
ATTN_CH = N_HEADS * V_HEAD
Z2_COLS = Q_LORA + KV_LORA + 128
D_FF = 5632
ROPE_THETA = 10000.0
EPS = 1e-6
LANES = 128
N_CHIPS = 4
N_DEV = 8

ADAM_LR = 0.001
ADAM_B1 = 0.9
ADAM_B2 = 0.999
ADAM_EPS = 1e-08
ADAM_WD = 0.01
ADAM_STEP = 10

VMEM_LIMIT = 56 * 1024 * 1024
ROW_TILE = 256
MAX_TK = 2816
MESH = pl.DeviceIdType.MESH


def _params(sem=None):
    return pltpu.CompilerParams(dimension_semantics=sem, vmem_limit_bytes=VMEM_LIMIT)


def _first_divisor(n, cands):
    for c in cands:
        if n % c == 0:
            return c
    return n


STREAM_BLOCK_BYTES = 3 << 19


def _row_tile(rows, cols, itemsize):
    for tr in (1024, 704, 512, 384, 352, 256, 176, 128, 64, 32, 16):
        if rows % tr == 0 and tr * cols * itemsize <= STREAM_BLOCK_BYTES:
            return tr
    return rows


def _matmul(pairs, mode, out_dtype, name, b_parts=None, out_parts=False, tiles=(None, None, None), after=None):
    a0, b0 = pairs[0]
    part_c = b0.shape[2] if b_parts else None
    if mode == "nn":
        m, n = a0.shape[0], (N_CHIPS * part_c if b_parts else b0.shape[1])
        ks = [a.shape[1] for a, _ in pairs]
    elif mode == "nt":
        m, n = a0.shape[0], b0.shape[-2]
        ks = [a.shape[1] for a, _ in pairs]
    else:
        m, n = a0.shape[1], b0.shape[1]
        ks = [a.shape[0] for a, _ in pairs]
    tm = tiles[0] or _first_divisor(m, (1024, 768, 512, 256))
    tn = tiles[1] or (n if n <= 1536 else _first_divisor(n, (1024, 512, 256, 128)))
    tks = [tiles[2] or (k if k <= MAX_TK else MAX_TK) for k in ks]
    nks = [k // tk for k, tk in zip(ks, tks)]
    offs = [sum(nks[:p]) for p in range(len(pairs))]
    nk = sum(nks)
    n_pairs = len(pairs)
    assert not (b_parts or out_parts) or n_pairs == 1

    def kk(k, p):
        return jnp.clip(k - offs[p], 0, nks[p] - 1)

    in_specs = []
    for p in range(n_pairs):
        tk = tks[p]
        if mode == "nn":
            in_specs.append(pl.BlockSpec((tm, tk), lambda i, j, k, p=p: (i, kk(k, p))))
            if b_parts == "n":
                per = part_c // tn
                in_specs.append(pl.BlockSpec((None, tk, tn), lambda i, j, k: (j // per, k, j % per)))
            else:
                in_specs.append(pl.BlockSpec((tk, tn), lambda i, j, k, p=p: (kk(k, p), j)))
        elif mode == "nt":
            in_specs.append(pl.BlockSpec((tm, tk), lambda i, j, k, p=p: (i, kk(k, p))))
            if b_parts == "k":
                per = part_c // tk
                in_specs.append(pl.BlockSpec((None, tn, tk), lambda i, j, k: (k // per, j, k % per)))
            else:
                in_specs.append(pl.BlockSpec((tn, tk), lambda i, j, k, p=p: (j, kk(k, p))))
        else:
            in_specs.append(pl.BlockSpec((tk, tm), lambda i, j, k, p=p: (kk(k, p), i)))
            in_specs.append(pl.BlockSpec((tk, tn), lambda i, j, k, p=p: (kk(k, p), j)))
    if out_parts:
        out_per = (n // N_CHIPS) // tn
        out_spec = pl.BlockSpec((None, tm, tn), lambda i, j, k: (j // out_per, i, j % out_per))
        out_shape = jax.ShapeDtypeStruct((N_CHIPS, m, n // N_CHIPS), out_dtype)
    else:
        out_spec = pl.BlockSpec((tm, tn), lambda i, j, k: (i, j))
        out_shape = jax.ShapeDtypeStruct((m, n), out_dtype)
    dims = {"nn": (((1,), (0,)), ((), ())), "nt": (((1,), (1,)), ((), ())), "tn": (((0,), (0,)), ((), ()))}[mode]

    n_after = 0 if after is None else 1

    def body(*refs):
        o_ref = refs[2 * n_pairs + n_after]
        k = pl.program_id(2)

        def prod(p):
            return lax.dot_general(refs[2 * p][...], refs[2 * p + 1][...], dims, preferred_element_type=F32)

        if nk == 1:
            o_ref[...] = prod(0).astype(o_ref.dtype)
            return
        acc = refs[2 * n_pairs + n_after + 1]
        for p in range(n_pairs):
            first, last = offs[p], offs[p] + nks[p] - 1
            lo, hi = max(first, 1), min(last, nk - 2)
            if first == 0:
                @pl.when(k == 0)
                def _(p=p):
                    acc[...] = prod(p)

            if lo <= hi:
                @pl.when((k >= lo) & (k <= hi))
                def _(p=p):
                    acc[...] += prod(p)

            if last == nk - 1:
                @pl.when(k == nk - 1)
                def _(p=p):
                    o_ref[...] = (acc[...] + prod(p)).astype(o_ref.dtype)

    flat = [t for pr in pairs for t in pr] + ([] if after is None else [after])
    return pl.pallas_call(
        body,
        name=name,
        grid=(m // tm, n // tn, nk),
        in_specs=in_specs + [pl.BlockSpec(memory_space=pl.ANY)] * n_after,
        out_specs=out_spec,
        out_shape=out_shape,
        scratch_shapes=[pltpu.VMEM((tm, tn), F32)] if nk > 1 else [],
        compiler_params=_params(("parallel", "parallel", "arbitrary")),
    )(*flat)


F4 = D_FF // N_CHIPS
FFN_TM = 512
FFN_STRIP = 256


def _ffn_up(hf, w_gu):
    t, d = hf.shape

    def body(a_ref, b_ref, gu_ref, act_ref):
        for r in range(0, FFN_TM, FFN_STRIP):
            acc = jnp.dot(a_ref[r:r + FFN_STRIP, :], b_ref[...], preferred_element_type=F32)
            g = acc[:, :F4]
            gu_ref[r:r + FFN_STRIP, :] = acc.astype(gu_ref.dtype)
            act_ref[r:r + FFN_STRIP, :] = (g * _sigmoid(g) * acc[:, F4:]).astype(act_ref.dtype)

    return pl.pallas_call(
        body,
        name="ffn_up",
        grid=(N_CHIPS, t // FFN_TM),
        in_specs=[pl.BlockSpec((FFN_TM, d), lambda q, i: (i, 0)),
                  pl.BlockSpec((None, d, 2 * F4), lambda q, i: (q, 0, 0))],
        out_specs=[pl.BlockSpec((FFN_TM, 2 * F4), lambda q, i: (i, q)),
                   pl.BlockSpec((FFN_TM, F4), lambda q, i: (i, q))],
        out_shape=[jax.ShapeDtypeStruct((t, 2 * D_FF), BF16), jax.ShapeDtypeStruct((t, D_FF), BF16)],
        compiler_params=_params(("parallel", "parallel")),
    )(hf, w_gu)


def _ffn_down_dx(d_ff, w_down, gu):
    t, d = d_ff.shape

    def body(a_ref, b_ref, gu_ref, o_ref):
        for r in range(0, FFN_TM, FFN_STRIP):
            rows = slice(r, r + FFN_STRIP)
            d_act = lax.dot_general(a_ref[rows, :], b_ref[...], NT_DIMS, preferred_element_type=F32)
            g = gu_ref[rows, :F4].astype(F32)
            u = gu_ref[rows, F4:].astype(F32)
            sg = _sigmoid(g)
            dsg = d_act * sg
            o_ref[rows, :F4] = (dsg * u * (1.0 + g - g * sg)).astype(o_ref.dtype)
            o_ref[rows, F4:] = (dsg * g).astype(o_ref.dtype)

    return pl.pallas_call(
        body,
        name="ffn_down_dx",
        grid=(N_CHIPS, t // FFN_TM),
        in_specs=[pl.BlockSpec((FFN_TM, d), lambda q, i: (i, 0)),
                  pl.BlockSpec((F4, d), lambda q, i: (q, 0)),
                  pl.BlockSpec((FFN_TM, 2 * F4), lambda q, i: (i, q))],
        out_specs=pl.BlockSpec((FFN_TM, 2 * F4), lambda q, i: (i, q)),
        out_shape=jax.ShapeDtypeStruct((t, 2 * D_FF), BF16),
        compiler_params=_params(("parallel", "parallel")),
    )(d_ff, w_down, gu)


def _rowwise(fn, row_ins, vec_ins, row_outs, acc_outs, name, after=None):
    t = row_ins[0].shape[0]
    tm = ROW_TILE
    n_in = len(row_ins) + len(vec_ins)
    n_row = len(row_outs)
    extra = [] if after is None else [after]

    def body(*refs):
        ins = [r[...] for r in refs[:n_in]]
        outs = refs[n_in + len(extra):]
        vals = fn(*ins)
        for r, v in zip(outs[:n_row], vals[:n_row]):
            r[...] = v.astype(r.dtype)
        if acc_outs:
            @pl.when(pl.program_id(0) == 0)
            def _():
                for r in outs[n_row:]:
                    r[...] = jnp.zeros_like(r)

            for r, v in zip(outs[n_row:], vals[n_row:]):
                r[...] += v

    in_specs = [pl.BlockSpec((tm, a.shape[1]), lambda i: (i, 0)) for a in row_ins]
    in_specs += [pl.BlockSpec(a.shape, lambda i: (0, 0)) for a in vec_ins]
    out_specs = [pl.BlockSpec((tm, c), lambda i: (i, 0)) for c, _ in row_outs]
    out_specs += [pl.BlockSpec((1, c), lambda i: (0, 0)) for c in acc_outs]
    out_shape = [jax.ShapeDtypeStruct((t, c), dt) for c, dt in row_outs]
    out_shape += [jax.ShapeDtypeStruct((1, c), F32) for c in acc_outs]
    return pl.pallas_call(
        body,
        name=name,
        grid=(t // tm,),
        in_specs=in_specs + [pl.BlockSpec(memory_space=pl.ANY)] * len(extra),
        out_specs=out_specs,
        out_shape=out_shape,
        compiler_params=_params(("arbitrary",)),
    )(*row_ins, *vec_ins, *extra)


def _mean(v):
    return jnp.mean(v, axis=-1, keepdims=True)


def _colsum(v):
    return jnp.sum(v, axis=0, keepdims=True)


def _rms_fwd(v, g):
    r = lax.rsqrt(_mean(v * v) + EPS)
    vhat = v * r
    return vhat * g, vhat, r


def _rms_bwd(dn, vhat, r, g):
    dng = dn * g
    return r * (dng - vhat * _mean(dng * vhat)), _colsum(dn * vhat)


def _swap_rope_halves(v):
    n = v.shape[-1]
    lane = lax.broadcasted_iota(jnp.int32, v.shape, v.ndim - 1)
    return jnp.where(lane % QK_ROPE < QK_ROPE // 2, pltpu.roll(v, n - QK_ROPE // 2, v.ndim - 1),
                     pltpu.roll(v, QK_ROPE // 2, v.ndim - 1))


def _rope(v, cos2, sin2):
    return v * cos2 + _swap_rope_halves(v) * sin2


def _rope_transposed(dv, cos2, sin2):
    return dv * cos2 + _swap_rope_halves(dv * sin2)


def _sigmoid(v):
    return 1.0 / (1.0 + jnp.exp(-v))


CONV_ROWS = 256


def _conv_fwd(ag, conv_w, conv_b):
    t = ag.shape[0]
    cb = LANES

    def body(ag_ref, w_ref, b_ref, o_ref, scr):
        a = ag_ref[:, :cb].astype(F32)
        g = ag_ref[:, cb:].astype(F32)
        scr[pl.ds(0, CONV_K_PAD), :] = jnp.zeros((CONV_K_PAD, cb), F32)
        scr[pl.ds(CONV_K_PAD, t), :] = a * _sigmoid(g)
        for r0 in range(0, t, CONV_ROWS):
            acc = jnp.zeros((CONV_ROWS, cb), F32) + b_ref[...]
            for k in range(CONV_K):
                acc = acc + w_ref[k:k + 1, :] * scr[pl.ds(r0 + CONV_K_PAD - (CONV_K - 1) + k, CONV_ROWS), :]
            o_ref[pl.ds(r0, CONV_ROWS), :] = acc

    return pl.pallas_call(
        body,
        name="conv_fwd",
        grid=(CONV_CH // cb,),
        in_specs=[pl.BlockSpec((t, 2 * cb), lambda j: (0, j)),
                  pl.BlockSpec((CONV_K_PAD, cb), lambda j: (0, j)),
                  pl.BlockSpec((1, cb), lambda j: (0, j))],
        out_specs=pl.BlockSpec((t, cb), lambda j: (0, j)),
        out_shape=jax.ShapeDtypeStruct((t, CONV_CH), F32),
        scratch_shapes=[pltpu.VMEM((t + CONV_K_PAD, cb), F32)],
        compiler_params=_params(("parallel",)),
    )(ag, conv_w, conv_b)


def _conv_bwd(d_u1, ag, conv_w):
    t = ag.shape[0]
    cb = LANES

    def body(du_ref, ag_ref, w_ref, dag_ref, dw_ref, db_ref, su, sd):
        a = ag_ref[:, :cb].astype(F32)
        g = ag_ref[:, cb:].astype(F32)
        sg = _sigmoid(g)
        su[pl.ds(0, CONV_K_PAD), :] = jnp.zeros((CONV_K_PAD, cb), F32)
        su[pl.ds(CONV_K_PAD, t), :] = a * sg
        sd[pl.ds(0, t), :] = du_ref[...]
        sd[pl.ds(t, CONV_K_PAD), :] = jnp.zeros((CONV_K_PAD, cb), F32)
        db_ref[...] = _colsum(du_ref[...])
        dw_ref[...] = jnp.zeros_like(dw_ref)
        for r0 in range(0, t, CONV_ROWS):
            du = sd[pl.ds(r0, CONV_ROWS), :]
            acc = jnp.zeros((CONV_ROWS, cb), F32)
            for k in range(CONV_K):
                acc = acc + w_ref[k:k + 1, :] * sd[pl.ds(r0 + (CONV_K - 1) - k, CONV_ROWS), :]
                dw_ref[k:k + 1, :] += _colsum(du * su[pl.ds(r0 + CONV_K_PAD - (CONV_K - 1) + k, CONV_ROWS), :])
            sgc = sg[r0:r0 + CONV_ROWS]
            ac = a[r0:r0 + CONV_ROWS]
            dag_ref[pl.ds(r0, CONV_ROWS), :cb] = (acc * sgc).astype(dag_ref.dtype)
            dag_ref[pl.ds(r0, CONV_ROWS), cb:] = (acc * ac * sgc * (1.0 - sgc)).astype(dag_ref.dtype)

    return pl.pallas_call(
        body,
        name="conv_bwd",
        grid=(CONV_CH // cb,),
        in_specs=[pl.BlockSpec((t, cb), lambda j: (0, j)),
                  pl.BlockSpec((t, 2 * cb), lambda j: (0, j)),
                  pl.BlockSpec((CONV_K_PAD, cb), lambda j: (0, j))],
        out_specs=[pl.BlockSpec((t, 2 * cb), lambda j: (0, j)),
                   pl.BlockSpec((CONV_K_PAD, cb), lambda j: (0, j)),
                   pl.BlockSpec((1, cb), lambda j: (0, j))],
        out_shape=[jax.ShapeDtypeStruct((t, 2 * CONV_CH), BF16),
                   jax.ShapeDtypeStruct((CONV_K_PAD, CONV_CH), F32),
                   jax.ShapeDtypeStruct((1, CONV_CH), F32)],
        scratch_shapes=[pltpu.VMEM((t + CONV_K_PAD, cb), F32), pltpu.VMEM((t + CONV_K_PAD, cb), F32)],
        compiler_params=_params(("parallel",)),
    )(d_u1, ag, conv_w)


ATT_TQ = 256
NEG = float(jnp.finfo(jnp.float32).min)
SCALE = QK_HEAD ** -0.5
NT_DIMS = (((1,), (1,)), ((), ()))
TN_DIMS = (((0,), (0,)), ((), ()))


def _att_weights(qf, kf, row0):
    s = lax.dot_general(qf, kf, NT_DIMS, preferred_element_type=F32)
    tq, t = s.shape
    qpos = row0 + lax.broadcasted_iota(jnp.int32, (tq, t), 0)
    kpos = lax.broadcasted_iota(jnp.int32, (tq, t), 1)
    s = jnp.where(kpos <= qpos, s, NEG)
    p = jnp.exp(s - jnp.max(s, axis=-1, keepdims=True))
    return p, 1.0 / jnp.sum(p, axis=-1, keepdims=True)


def _scaled_query(qn, roped_half):
    return jnp.concatenate([(qn.astype(F32) * SCALE).astype(BF16), (roped_half * SCALE).astype(BF16)], axis=1)


def _half_mask(shape, which):
    lane = lax.broadcasted_iota(jnp.int32, shape, len(shape) - 1)
    return (lane // QK_ROPE == which).astype(F32)


def _attention_fwd(q, kv, kpe2, cos2, sin2):
    t = q.shape[0]
    tq = ATT_TQ

    def body(qn_ref, qp_ref, c_ref, s_ref, kv_ref, kpe_ref, o_ref):
        roped = _rope(qp_ref[...].astype(F32), c_ref[...], s_ref[...])

        def block(i):
            keys = slice(0, (i + 1) * tq)
            for e in range(2):
                qf = _scaled_query(qn_ref[:, e * QK_NOPE:(e + 1) * QK_NOPE], roped * _half_mask(roped.shape, e))
                kf = jnp.concatenate([kv_ref[keys, e * 256:e * 256 + QK_NOPE], kpe_ref[keys, :]], axis=1)
                p, inv_l = _att_weights(qf, kf, i * tq)
                v = kv_ref[keys, e * 256 + QK_NOPE:(e + 1) * 256]
                o = jnp.dot(p.astype(BF16), v, preferred_element_type=F32) * inv_l
                o_ref[:, e * V_HEAD:(e + 1) * V_HEAD] = o.astype(o_ref.dtype)

        for i in range(t // tq):
            pl.when(pl.program_id(1) == i)(functools.partial(block, i))

    return pl.pallas_call(
        body,
        name="attention_fwd",
        grid=(N_HEADS // 2, t // tq),
        in_specs=[pl.BlockSpec((tq, 2 * QK_NOPE), lambda h, i: (i, h)),
                  pl.BlockSpec((tq, LANES), lambda h, i: (i, N_HEADS + h)),
                  pl.BlockSpec((tq, LANES), lambda h, i: (i, 0)),
                  pl.BlockSpec((tq, LANES), lambda h, i: (i, 0)),
                  pl.BlockSpec((t, 512), lambda h, i: (0, h)),
                  pl.BlockSpec((t, LANES), lambda h, i: (0, 0))],
        out_specs=pl.BlockSpec((tq, 2 * V_HEAD), lambda h, i: (i, h)),
        out_shape=jax.ShapeDtypeStruct((t, ATTN_CH), BF16),
        compiler_params=_params(("parallel", "parallel")),
    )(q, q, cos2, sin2, kv, kpe2)


def _attention_bwd(q, kv, kpe2, cos2, sin2, d_attn):
    t = q.shape[0]
    tq = ATT_TQ
    n_q = t // tq

    def body(qn_ref, qp_ref, c_ref, s_ref, kv_ref, kpe_ref, do_ref, dqn_ref, dqp_ref, dkv_ref, dkpe_ref, dkv_acc):
        h, i = pl.program_id(0), pl.program_id(1)

        @pl.when(i == 0)
        def _():
            dkv_acc[...] = jnp.zeros_like(dkv_acc)

        @pl.when((i == 0) & (h == 0))
        def _():
            dkpe_ref[...] = jnp.zeros_like(dkpe_ref)

        roped = _rope(qp_ref[...].astype(F32), c_ref[...], s_ref[...])

        def block(ib):
            keys = slice(0, (ib + 1) * tq)
            d_roped = jnp.zeros((tq, LANES), F32)
            for e in range(2):
                mask = _half_mask(roped.shape, e)
                qf = _scaled_query(qn_ref[:, e * QK_NOPE:(e + 1) * QK_NOPE], roped * mask)
                kf = jnp.concatenate([kv_ref[keys, e * 256:e * 256 + QK_NOPE], kpe_ref[keys, :]], axis=1)
                v = kv_ref[keys, e * 256 + QK_NOPE:(e + 1) * 256]
                do = do_ref[:, e * V_HEAD:(e + 1) * V_HEAD]
                p, inv_l = _att_weights(qf, kf, ib * tq)
                p = p * inv_l
                dp = lax.dot_general(do, v, NT_DIMS, preferred_element_type=F32)
                ds = (p * (dp - jnp.sum(p * dp, axis=-1, keepdims=True))).astype(BF16)
                dqf = jnp.dot(ds, kf, preferred_element_type=F32) * SCALE
                dkf = lax.dot_general(ds, qf, TN_DIMS, preferred_element_type=F32)
                dv = lax.dot_general(p.astype(BF16), do, TN_DIMS, preferred_element_type=F32)
                dqn_ref[:, e * QK_NOPE:(e + 1) * QK_NOPE] = dqf[:, :QK_NOPE].astype(dqn_ref.dtype)
                d_roped = d_roped + dqf[:, QK_NOPE:] * mask
                dkv_acc[keys, e * 256:e * 256 + QK_NOPE] += dkf[:, :QK_NOPE]
                dkv_acc[keys, e * 256 + QK_NOPE:(e + 1) * 256] += dv
                dkpe_ref[keys, :] += dkf[:, QK_NOPE:]
            dqp_ref[...] = _rope_transposed(d_roped, c_ref[...], s_ref[...]).astype(dqp_ref.dtype)

        for ib in range(n_q):
            pl.when(i == ib)(functools.partial(block, ib))

        @pl.when(i == n_q - 1)
        def _():
            dkv_ref[...] = dkv_acc[...].astype(dkv_ref.dtype)

    return pl.pallas_call(
        body,
        name="attention_bwd",
        grid=(N_HEADS // 2, n_q),
        in_specs=[pl.BlockSpec((tq, 2 * QK_NOPE), lambda h, i: (i, h)),
                  pl.BlockSpec((tq, LANES), lambda h, i: (i, N_HEADS + h)),
                  pl.BlockSpec((tq, LANES), lambda h, i: (i, 0)),
                  pl.BlockSpec((tq, LANES), lambda h, i: (i, 0)),
                  pl.BlockSpec((t, 512), lambda h, i: (0, h)),
                  pl.BlockSpec((t, LANES), lambda h, i: (0, 0)),
                  pl.BlockSpec((tq, 2 * V_HEAD), lambda h, i: (i, h))],
        out_specs=[pl.BlockSpec((tq, 2 * QK_NOPE), lambda h, i: (i, h)),
                   pl.BlockSpec((tq, LANES), lambda h, i: (i, h)),
                   pl.BlockSpec((t, 512), lambda h, i: (0, h)),
                   pl.BlockSpec((t, LANES), lambda h, i: (0, 0))],
        out_shape=[jax.ShapeDtypeStruct((t, N_HEADS * QK_NOPE), BF16),
                   jax.ShapeDtypeStruct((t, N_HEADS * QK_ROPE), BF16),
                   jax.ShapeDtypeStruct((t, N_HEADS * 256), BF16),
                   jax.ShapeDtypeStruct((t, LANES), F32)],
        scratch_shapes=[pltpu.VMEM((t, 512), F32)],
        compiler_params=_params(("arbitrary", "arbitrary")),
    )(q, q, cos2, sin2, kv, kpe2, d_attn)


def _local_step(x, target, cos2, sin2, vec, w, ffn):
    d = D_MODEL

    (h,) = _rowwise(lambda xv, g: (_rms_fwd(xv, g)[0],), [x], [vec["pre_mix_norm"]], [(d, BF16)], [], "pre_mix_norm_fwd")
    ag = _matmul([(h, w["w_ag"])], "nn", BF16, "in_proj_ag")
    z2 = _matmul([(h, w["w_z2"])], "nn", BF16, "in_proj_z2")
    w = {**w, **ffn["mixer_rest"](z2)}
    u1 = _conv_fwd(ag, w["conv_w"], vec["conv_b"])

    def latents_fwd(z, c2, s2, qg, kvg):
        z = z.astype(F32)
        qn = _rms_fwd(z[:, :Q_LORA], qg)[0]
        kvn = _rms_fwd(z[:, Q_LORA:Q_LORA + KV_LORA], kvg)[0]
        kr = z[:, Q_LORA + KV_LORA:]
        kr2 = kr + pltpu.roll(kr, QK_ROPE, 1)
        return qn, kvn, _rope(kr2, c2, s2)

    qn, kvn, kpe2 = _rowwise(latents_fwd, [z2, cos2, sin2], [vec["q_norm"], vec["kv_norm"]],
                             [(Q_LORA, BF16), (KV_LORA, BF16), (LANES, BF16)], [], "latents_fwd")
    q = _matmul([(qn, w["w_uq"])], "nn", BF16, "q_up")
    kv = _matmul([(kvn, w["w_ukv"])], "nn", BF16, "kv_up")
    attn = _attention_fwd(q, kv, kpe2, cos2, sin2)

    def conv_post(u, lg, lb):
        mu = _mean(u)
        uc = u - mu
        rstd = lax.rsqrt(_mean(uc * uc) + EPS)
        uhat = uc * rstd
        u2 = uhat * lg + lb
        sg = _sigmoid(u2)
        return uhat, rstd, u2, sg, u2 * sg

    def mix_in_fwd(u, at, lg, lb, cg, ag_):
        u3 = conv_post(u, lg, lb)[4]
        cn = _rms_fwd(u3, cg)[0]
        an = _rms_fwd(at.astype(F32), ag_)[0]
        return (jnp.concatenate([cn, an], axis=1),)

    (cat,) = _rowwise(mix_in_fwd, [u1, attn], [vec["conv_ln_g"], vec["conv_ln_b"], vec["conv_out_norm"], vec["attn_out_norm"]],
                      [(2 * CONV_CH, BF16)], [], "mix_in_fwd")
    mix = _matmul([(cat, w["w_out"])], "nn", F32, "out_proj")
    landed = ffn["w_gu_landed"](mix)

    def residual1(xv, mv, gpm, gpf):
        x1 = xv + _rms_fwd(mv, gpm)[0]
        return x1, _rms_fwd(x1, gpf)[0]

    x1, hf = _rowwise(residual1, [x, mix], [vec["post_mix_norm"], vec["pre_ffn_norm"]], [(d, F32), (d, BF16)], [],
                      "residual1_fwd", after=landed)
    w_gu = ffn["w_gu"](hf)
    gu, act = _ffn_up(hf, w_gu)
    w_down = ffn["w_down"](act)
    ff = _matmul([(act, w_down)], "nn", F32, "ffn_down")

    def loss_head(x1v, ffv, tg, g):
        n, fhat, r = _rms_fwd(ffv, g)
        err = x1v + n - tg
        loss = 0.5 * jnp.sum(_mean(err * err), axis=0, keepdims=True)
        dy = err * (1.0 / d)
        d_ff, dg = _rms_bwd(dy, fhat, r, g)
        return dy, d_ff, dg, jnp.broadcast_to(loss, (1, LANES))

    dy, d_ff, g_post_ffn, loss = _rowwise(loss_head, [x1, ff, target], [vec["post_ffn_norm"]],
                                          [(d, F32), (d, BF16)], [d, LANES], "loss_head")
    d_gu = _ffn_down_dx(d_ff, w_down, gu)
    dw_down = _matmul([(act, d_ff)], "tn", BF16, "ffn_down_dw", tiles=(F4, None, None))
    started = ffn["dw_down"](dw_down)
    dw_gu = _matmul([(hf, d_gu)], "tn", BF16, "ffn_gate_up_dw", out_parts=True, tiles=(None, F4, None), after=started)
    started = ffn["dw_gu"](dw_gu)
    d_hf = _matmul([(d_gu, w_gu)], "nt", F32, "ffn_gate_up_dx", b_parts="k", after=started)
    started = ffn["grads_exchanged"](d_hf)

    def residual1_bwd(dyv, dhf, x1v, mv, gpf, gpm):
        _, x1hat, r1 = _rms_fwd(x1v, gpf)
        dn, dgpf = _rms_bwd(dhf, x1hat, r1, gpf)
        d_x1 = dyv + dn
        _, mhat, rm = _rms_fwd(mv, gpm)
        d_mix, dgpm = _rms_bwd(d_x1, mhat, rm, gpm)
        return d_x1, d_mix, dgpf, dgpm

    d_x1, d_mix, g_pre_ffn, g_post_mix = _rowwise(residual1_bwd, [dy, d_hf, x1, mix], [vec["pre_ffn_norm"], vec["post_mix_norm"]],
                                                  [(d, F32), (d, BF16)], [d, d], "residual1_bwd", after=started)
    d_cat = _matmul([(d_mix, w["w_out"])], "nt", BF16, "out_proj_dx")
    dw_out = _matmul([(cat, d_mix)], "tn", BF16, "out_proj_dw")

    def mix_in_bwd(dc, u, at, lg, lb, cg, ag_):
        dc = dc.astype(F32)
        uhat, rstd, u2, sg, u3 = conv_post(u, lg, lb)
        _, u3hat, rc = _rms_fwd(u3, cg)
        d_u3, dcg = _rms_bwd(dc[:, :CONV_CH], u3hat, rc, cg)
        d_u2 = d_u3 * sg * (1.0 + u2 * (1.0 - sg))
        dgl = d_u2 * lg
        d_u1 = rstd * (dgl - _mean(dgl) - uhat * _mean(dgl * uhat))
        _, ahat, ra = _rms_fwd(at.astype(F32), ag_)
        d_at, dag = _rms_bwd(dc[:, CONV_CH:], ahat, ra, ag_)
        return d_u1, d_at, dcg, _colsum(d_u2 * uhat), _colsum(d_u2), dag

    d_u1, d_attn, g_conv_out, g_ln_g, g_ln_b, g_attn_out = _rowwise(
        mix_in_bwd, [d_cat, u1, attn], [vec["conv_ln_g"], vec["conv_ln_b"], vec["conv_out_norm"], vec["attn_out_norm"]],
        [(CONV_CH, F32), (ATTN_CH, BF16)], [CONV_CH] * 4, "mix_in_bwd")
    d_ag, d_conv_w, g_conv_b = _conv_bwd(d_u1, ag, w["conv_w"])
    d_qn_, d_qp_, d_kv, d_kpe2 = _attention_bwd(q, kv, kpe2, cos2, sin2, d_attn)
    d_q = jnp.concatenate([d_qn_, d_qp_], axis=1)
    d_qn = _matmul([(d_q, w["w_uq"])], "nt", BF16, "q_up_dx")
    dw_uq = _matmul([(qn, d_q)], "tn", BF16, "q_up_dw")
    d_kvn = _matmul([(d_kv, w["w_ukv"])], "nt", BF16, "kv_up_dx")
    dw_ukv = _matmul([(kvn, d_kv)], "tn", BF16, "kv_up_dw")

    def latents_bwd(z, dq, dk, dkp, c2, s2, qg, kvg):
        z = z.astype(F32)
        _, qhat, rq = _rms_fwd(z[:, :Q_LORA], qg)
        d_ql, dqg = _rms_bwd(dq.astype(F32), qhat, rq, qg)
        _, khat, rk = _rms_fwd(z[:, Q_LORA:Q_LORA + KV_LORA], kvg)
        d_kl, dkg = _rms_bwd(dk.astype(F32), khat, rk, kvg)
        both = dkp + pltpu.roll(dkp, QK_ROPE, 1)
        d_kr = _rope_transposed(both, c2, s2) * _half_mask(both.shape, 0)
        return jnp.concatenate([d_ql, d_kl, d_kr], axis=1), dqg, dkg

    d_z2, g_q_norm, g_kv_norm = _rowwise(latents_bwd, [z2, d_qn, d_kvn, d_kpe2, cos2, sin2], [vec["q_norm"], vec["kv_norm"]],
                                         [(Z2_COLS, BF16)], [Q_LORA, KV_LORA], "latents_bwd")
    dw_ag = _matmul([(h, d_ag)], "tn", BF16, "in_proj_ag_dw")
    dw_z2 = _matmul([(h, d_z2)], "tn", BF16, "in_proj_z2_dw")
    started = ffn["mixer_grads"](dict(w_ag=dw_ag, w_z2=dw_z2, w_uq=dw_uq, w_ukv=dw_ukv, w_out=dw_out))
    d_h = _matmul([(d_ag, w["w_ag"]), (d_z2, w["w_z2"])], "nt", F32, "in_proj_dx", after=started)

    def pre_mix_bwd(dx1, dh, xv, g):
        _, xhat, r = _rms_fwd(xv, g)
        dn, dg = _rms_bwd(dh, xhat, r, g)
        return dx1 + dn, dg

    grad_x, g_pre_mix = _rowwise(pre_mix_bwd, [d_x1, d_h, x], [vec["pre_mix_norm"]], [(d, F32)], [d], "pre_mix_norm_bwd")

    dw = dict(w_ag=dw_ag, w_z2=dw_z2, w_uq=dw_uq, w_ukv=dw_ukv, conv_w=d_conv_w, w_out=dw_out, w_gu=dw_gu, w_down=dw_down)
    dvec = dict(pre_mix_norm=g_pre_mix, q_norm=g_q_norm, kv_norm=g_kv_norm, conv_b=g_conv_b, conv_ln_g=g_ln_g,
                conv_ln_b=g_ln_b, conv_out_norm=g_conv_out, attn_out_norm=g_attn_out, post_mix_norm=g_post_mix,
                pre_ffn_norm=g_pre_ffn, post_ffn_norm=g_post_ffn)
    return loss, grad_x, dw, dvec


ANY = pl.BlockSpec(memory_space=pl.ANY)


def _place():
    x, y, c = lax.axis_index("x"), lax.axis_index("y"), lax.axis_index("c")
    chips = [(1 - x, y), (x, 1 - y), (1 - x, 1 - y)]
    return x, y, c, chips


def _to_parts(chip, pieces, dtype, name, n_parts=N_CHIPS):
    r = pieces[0].shape[0]
    widths = [a.shape[1] for a in pieces]
    tr = r if r <= 512 else _first_divisor(r, (512, 256, 128))

    def body(p_ref, *refs):
        o_ref = refs[len(pieces)]
        off = 0
        for a_ref, wdt in zip(refs, widths):
            o_ref[:, off:off + wdt] = a_ref[...].astype(o_ref.dtype)
            off += wdt

    return pl.pallas_call(
        body,
        name=name,
        grid_spec=pltpu.PrefetchScalarGridSpec(
            num_scalar_prefetch=1,
            grid=(r // tr,),
            in_specs=[pl.BlockSpec((tr, wdt), lambda i, p_ref: (i, 0)) for wdt in widths],
            out_specs=pl.BlockSpec((None, tr, sum(widths)), lambda i, p_ref: (p_ref[0], i, 0))),
        out_shape=jax.ShapeDtypeStruct((n_parts, r, sum(widths)), dtype),
        compiler_params=_params(("parallel",)),
    )(chip, *pieces)


HBM = pl.BlockSpec(memory_space=pltpu.HBM)
SEM = pl.BlockSpec(memory_space=pltpu.SEMAPHORE)
EFFECT = pltpu.SideEffectType.DATAFLOW_SIDE_EFFECTING
VMEM_SPEC = pl.BlockSpec(memory_space=pltpu.VMEM)
TOKEN = jax.ShapeDtypeStruct((8, LANES), F32)


def _in_hbm(a):
    return pltpu.with_memory_space_constraint(a, pltpu.HBM)


def _gather_rows(buf, whole, half):
    r = buf.shape[1]
    return pl.ds(0, r) if whole else pl.ds(half * (r // 2), r // 2)


def _gather_start(bufs, whole, groups, name, after=()):
    n = len(bufs)
    n_g = len(groups)

    def body(*refs):
        refs = refs[:n] + refs[n + len(after):]
        sems = refs[n:n + 2 * n_g]
        outs = refs[n + 2 * n_g:2 * n + 2 * n_g]
        token = refs[2 * n + 2 * n_g]
        token[...] = jnp.zeros_like(token)
        x, y, c, chips = _place()
        p = 2 * x + y
        for gi, group in enumerate(groups):
            for ki, k in enumerate(group):
                blk = outs[k].at[p, _gather_rows(bufs[k], whole[k], c), :]
                for j, (px, py) in enumerate(chips):
                    pltpu.make_async_remote_copy(src_ref=blk, dst_ref=blk, send_sem=sems[2 * gi].at[3 * ki + j],
                                                 recv_sem=sems[2 * gi + 1].at[3 * ki + j],
                                                 device_id=(px, py, c), device_id_type=MESH).start()

    sem_shapes = []
    for group in groups:
        sem_shapes += [pltpu.SemaphoreType.DMA((3 * len(group),))] * 2
    res = pl.pallas_call(
        body,
        name=name,
        in_specs=[HBM] * n + [ANY] * len(after),
        out_specs=[SEM] * (2 * n_g) + [HBM] * n + [VMEM_SPEC],
        out_shape=sem_shapes + [pltpu.HBM(a.shape, a.dtype) for a in bufs] + [TOKEN],
        input_output_aliases={k: 2 * n_g + k for k in range(n)},
        compiler_params=pltpu.CompilerParams(has_side_effects=EFFECT),
    )(*[_in_hbm(a) for a in bufs], *after)
    sems = [(res[2 * gi], res[2 * gi + 1]) for gi in range(n_g)]
    return sems, list(res[2 * n_g:2 * n_g + n]), res[2 * n_g + n]


def _gather_wait(bufs, whole, send, recv, after, name):
    n = len(bufs)

    def body(*refs):
        ins = refs[:n]
        send_ref, recv_ref = refs[n], refs[n + 1]
        x, y, c, chips = _place()
        p = 2 * x + y
        for ki in range(n):
            rows = _gather_rows(bufs[ki], whole[ki], c)
            for j, (px, py) in enumerate(chips):
                cp = pltpu.make_async_remote_copy(src_ref=ins[ki].at[p, rows, :], dst_ref=ins[ki].at[2 * px + py, rows, :],
                                                  send_sem=send_ref.at[3 * ki + j], recv_sem=recv_ref.at[3 * ki + j],
                                                  device_id=(px, py, c), device_id_type=MESH)
                cp.wait_send()
                cp.wait_recv()

    res = pl.pallas_call(
        body,
        name=name,
        in_specs=[HBM] * n + [SEM, SEM] + [ANY] * len(after),
        out_specs=[HBM] * n,
        out_shape=[pltpu.HBM(a.shape, a.dtype) for a in bufs],
        input_output_aliases={k: k for k in range(n)},
        compiler_params=pltpu.CompilerParams(has_side_effects=EFFECT),
    )(*bufs, send, recv, *after)
    return list(res)


def _gather_hand_on(bufs, name, after=()):
    n = len(bufs)

    def body(*refs):
        refs = refs[n + len(after):]
        outs = refs[:n]
        send, recv = refs[n:]
        x, y, c, chips = _place()

        def d2d(k, j, half):
            px, py = chips[j]
            blk = outs[k].at[2 * px + py, _gather_rows(bufs[k], False, half), :]
            return pltpu.make_async_remote_copy(src_ref=blk, dst_ref=blk, send_sem=send.at[3 * k + j], recv_sem=recv.at[3 * k + j],
                                                device_id=(x, y, 1 - c), device_id_type=MESH)

        sent = [d2d(k, j, c) for k in range(n) for j in range(3)]
        for cp in sent:
            cp.start()
        for k in range(n):
            for j in range(3):
                d2d(k, j, 1 - c).wait_recv()
        for cp in sent:
            cp.wait_send()

    return pl.pallas_call(
        body,
        name=name,
        in_specs=[ANY] * (n + len(after)),
        out_specs=[ANY] * n,
        out_shape=[jax.ShapeDtypeStruct(a.shape, a.dtype) for a in bufs],
        input_output_aliases={k: k for k in range(n)},
        scratch_shapes=[pltpu.SemaphoreType.DMA((3 * n,)), pltpu.SemaphoreType.DMA((3 * n,))],
        compiler_params=pltpu.CompilerParams(has_side_effects=True),
    )(*bufs, *after)


def _hand_on_start(bufs, name, after=()):
    n = len(bufs)

    def body(*refs):
        refs = refs[:n] + refs[n + len(after):]
        send, recv = refs[n], refs[n + 1]
        outs = refs[n + 2:2 * n + 2]
        refs[2 * n + 2][...] = jnp.zeros(TOKEN.shape, TOKEN.dtype)
        x, y, c, chips = _place()
        for k in range(n):
            for j, (px, py) in enumerate(chips):
                blk = outs[k].at[2 * px + py, _gather_rows(bufs[k], False, c), :]
                pltpu.make_async_remote_copy(src_ref=blk, dst_ref=blk, send_sem=send.at[3 * k + j], recv_sem=recv.at[3 * k + j],
                                             device_id=(x, y, 1 - c), device_id_type=MESH).start()

    res = pl.pallas_call(
        body,
        name=name,
        in_specs=[HBM] * n + [ANY] * len(after),
        out_specs=[SEM, SEM] + [HBM] * n + [VMEM_SPEC],
        out_shape=[pltpu.SemaphoreType.DMA((3 * n,))] * 2 + [pltpu.HBM(a.shape, a.dtype) for a in bufs] + [TOKEN],
        input_output_aliases={k: 2 + k for k in range(n)},
        compiler_params=pltpu.CompilerParams(has_side_effects=EFFECT),
    )(*[_in_hbm(a) for a in bufs], *after)
    return res[0], res[1], list(res[2:2 + n]), res[2 + n]


def _hand_on_wait(bufs, send, recv, after, name):
    n = len(bufs)

    def body(*refs):
        ins = refs[:n]
        send_ref, recv_ref = refs[n], refs[n + 1]
        x, y, c, chips = _place()
        for k in range(n):
            for j, (px, py) in enumerate(chips):
                q = 2 * px + py
                cp = pltpu.make_async_remote_copy(src_ref=ins[k].at[q, _gather_rows(bufs[k], False, c), :],
                                                  dst_ref=ins[k].at[q, _gather_rows(bufs[k], False, 1 - c), :],
                                                  send_sem=send_ref.at[3 * k + j], recv_sem=recv_ref.at[3 * k + j],
                                                  device_id=(x, y, 1 - c), device_id_type=MESH)
                cp.wait_send()
                cp.wait_recv()

    res = pl.pallas_call(
        body,
        name=name,
        in_specs=[HBM] * n + [SEM, SEM] + [ANY] * len(after),
        out_specs=[HBM] * n,
        out_shape=[pltpu.HBM(a.shape, a.dtype) for a in bufs],
        input_output_aliases={k: k for k in range(n)},
        compiler_params=pltpu.CompilerParams(has_side_effects=EFFECT),
    )(*bufs, send, recv, *after)
    return list(res)


def _pair_exchange_start(parts, name):
    n = len(parts)
    lands = [(N_CHIPS, a.shape[1] // 2, a.shape[2]) for a in parts]

    def body(*refs):
        send, recv = refs[2 * n], refs[2 * n + 1]
        src, dst = refs[2 * n + 2:3 * n + 2], refs[3 * n + 2:4 * n + 2]
        refs[4 * n + 2][...] = jnp.zeros(TOKEN.shape, TOKEN.dtype)
        x, y, c, _ = _place()
        for k in range(n):
            rh = lands[k][1]
            pltpu.make_async_remote_copy(src_ref=src[k].at[:, pl.ds((1 - c) * rh, rh), :], dst_ref=dst[k],
                                         send_sem=send.at[k], recv_sem=recv.at[k],
                                         device_id=(x, y, 1 - c), device_id_type=MESH).start()

    res = pl.pallas_call(
        body,
        name=name,
        in_specs=[HBM] * (2 * n),
        out_specs=[SEM, SEM] + [HBM] * (2 * n) + [VMEM_SPEC],
        out_shape=[pltpu.SemaphoreType.DMA((n,))] * 2 + [pltpu.HBM(a.shape, a.dtype) for a in parts]
        + [pltpu.HBM(s, a.dtype) for s, a in zip(lands, parts)] + [TOKEN],
        input_output_aliases={k: 2 + k for k in range(2 * n)},
        compiler_params=pltpu.CompilerParams(has_side_effects=EFFECT),
    )(*[_in_hbm(a) for a in parts], *[_in_hbm(lax.empty(s, a.dtype)) for s, a in zip(lands, parts)])
    return res[0], res[1], list(res[2:2 + n]), list(res[2 + n:2 + 2 * n]), res[2 + 2 * n]


def _pair_exchange_wait(send, recv, parts, lands, after, name):
    n = len(parts)

    def body(*refs):
        src, dst = refs[:n], refs[n:2 * n]
        send_ref, recv_ref = refs[2 * n], refs[2 * n + 1]
        x, y, c, _ = _place()
        for k in range(n):
            rh = parts[k].shape[1] // 2
            cp = pltpu.make_async_remote_copy(src_ref=src[k].at[:, pl.ds((1 - c) * rh, rh), :], dst_ref=dst[k],
                                              send_sem=send_ref.at[k], recv_sem=recv_ref.at[k],
                                              device_id=(x, y, 1 - c), device_id_type=MESH)
            cp.wait_send()
            cp.wait_recv()

    res = pl.pallas_call(
        body,
        name=name,
        in_specs=[HBM] * (2 * n) + [SEM, SEM] + [ANY] * len(after),
        out_specs=[HBM] * (2 * n),
        out_shape=[pltpu.HBM(a.shape, a.dtype) for a in parts] + [pltpu.HBM(a.shape, a.dtype) for a in lands],
        input_output_aliases={k: k for k in range(2 * n)},
        compiler_params=pltpu.CompilerParams(has_side_effects=EFFECT),
    )(*parts, *lands, send, recv, *after)
    return list(res[:n]), list(res[n:])


def _chip_exchange_start(sums, name):
    n = len(sums)

    def body(*refs):
        send, recv = refs[2 * n], refs[2 * n + 1]
        src = refs[2 * n + 2:3 * n + 2]
        dst = refs[3 * n + 2:4 * n + 2]
        refs[4 * n + 2][...] = jnp.zeros(TOKEN.shape, TOKEN.dtype)
        x, y, c, chips = _place()
        p = 2 * x + y
        for k in range(n):
            for j, (px, py) in enumerate(chips):
                pltpu.make_async_remote_copy(src_ref=src[k].at[2 * px + py], dst_ref=dst[k].at[p],
                                             send_sem=send.at[3 * k + j], recv_sem=recv.at[3 * k + j],
                                             device_id=(px, py, c), device_id_type=MESH).start()

    res = pl.pallas_call(
        body,
        name=name,
        in_specs=[HBM] * (2 * n),
        out_specs=[SEM, SEM] + [HBM] * (2 * n) + [VMEM_SPEC],
        out_shape=[pltpu.SemaphoreType.DMA((3 * n,))] * 2 + [pltpu.HBM(a.shape, a.dtype) for a in sums] * 2 + [TOKEN],
        input_output_aliases={k: 2 + k for k in range(2 * n)},
        compiler_params=pltpu.CompilerParams(has_side_effects=EFFECT),
    )(*[_in_hbm(a) for a in sums], *[_in_hbm(lax.empty(a.shape, a.dtype)) for a in sums])
    return res[0], res[1], list(res[2:2 + n]), list(res[2 + n:2 + 2 * n]), res[2 + 2 * n]


def _chip_exchange_wait(sums, slots, send, recv, after, name):
    n = len(sums)

    def body(*refs):
        src, dst = refs[:n], refs[n:2 * n]
        send_ref, recv_ref = refs[2 * n], refs[2 * n + 1]
        x, y, c, chips = _place()
        for k in range(n):
            for j, (px, py) in enumerate(chips):
                cp = pltpu.make_async_remote_copy(src_ref=src[k].at[2 * px + py], dst_ref=dst[k].at[2 * px + py],
                                                  send_sem=send_ref.at[3 * k + j], recv_sem=recv_ref.at[3 * k + j],
                                                  device_id=(px, py, c), device_id_type=MESH)
                cp.wait_send()
                cp.wait_recv()

    res = pl.pallas_call(
        body,
        name=name,
        in_specs=[HBM] * (2 * n) + [SEM, SEM] + [ANY] * len(after),
        out_specs=[HBM] * (2 * n),
        out_shape=[pltpu.HBM(a.shape, a.dtype) for a in sums] * 2,
        input_output_aliases={k: k for k in range(2 * n)},
        compiler_params=pltpu.CompilerParams(has_side_effects=EFFECT),
    )(*sums, *slots, send, recv, *after)
    return list(res[:n]), list(res[n:])


def _pair_sum(core, part, landed, name):
    _, r, cdim = part.shape
    rh = r // 2
    tr = _row_tile(rh, cdim, 2)
    nb = rh // tr

    def body(c_ref, a_ref, b_ref, o_ref):
        o_ref[...] = (a_ref[...].astype(F32) + b_ref[...].astype(F32)).astype(o_ref.dtype)

    return pl.pallas_call(
        body,
        name=name,
        grid_spec=pltpu.PrefetchScalarGridSpec(
            num_scalar_prefetch=1,
            grid=(N_CHIPS, nb),
            in_specs=[pl.BlockSpec((None, tr, cdim), lambda q, i, c_ref: (q, c_ref[0] * nb + i, 0)),
                      pl.BlockSpec((None, tr, cdim), lambda q, i, c_ref: (q, i, 0))],
            out_specs=pl.BlockSpec((None, tr, cdim), lambda q, i, c_ref: (q, i, 0))),
        out_shape=jax.ShapeDtypeStruct((N_CHIPS, rh, cdim), BF16),
        compiler_params=_params(("parallel", "parallel")),
    )(core, part, landed)


def _chip_sum(place, own, slots, name, after=()):
    _, rh, cdim = slots.shape
    tr = _row_tile(rh, cdim, 2)
    nb = rh // tr

    def body(place_ref, own_ref, s1_ref, s2_ref, s3_ref, *rest):
        o_ref = rest[len(after)]
        acc = own_ref[...].astype(F32)
        for s_ref in (s1_ref, s2_ref, s3_ref):
            acc = acc + s_ref[...].astype(F32)
        o_ref[...] = acc

    def other(j):
        return lambda i, place_ref: ((place_ref[0] + j) % N_CHIPS, i, 0)

    return pl.pallas_call(
        body,
        name=name,
        grid_spec=pltpu.PrefetchScalarGridSpec(
            num_scalar_prefetch=1,
            grid=(nb,),
            in_specs=[pl.BlockSpec((None, tr, cdim), other(j)) for j in (0, 1, 2, 3)] + [ANY] * len(after),
            out_specs=pl.BlockSpec((tr, cdim), lambda i, place_ref: (place_ref[1] * nb + i, 0))),
        out_shape=jax.ShapeDtypeStruct((2 * rh, cdim), F32),
        compiler_params=_params(("parallel",)),
    )(place, own, slots, slots, slots, *after)


def _half_exchange_start(buf, name):
    rh = buf.shape[0] // 2

    def body(buf_ref, send, recv, out_ref, token):
        token[...] = jnp.zeros_like(token)
        x, y, c, _ = _place()
        mine = out_ref.at[pl.ds(c * rh, rh), :]
        pltpu.make_async_remote_copy(src_ref=mine, dst_ref=mine, send_sem=send, recv_sem=recv,
                                     device_id=(x, y, 1 - c), device_id_type=MESH).start()

    return pl.pallas_call(
        body,
        name=name,
        in_specs=[HBM],
        out_specs=[SEM, SEM, HBM, VMEM_SPEC],
        out_shape=[pltpu.SemaphoreType.DMA(()), pltpu.SemaphoreType.DMA(()), pltpu.HBM(buf.shape, buf.dtype), TOKEN],
        input_output_aliases={0: 2},
        compiler_params=pltpu.CompilerParams(has_side_effects=EFFECT),
    )(_in_hbm(buf))


def _half_exchange_wait(send, recv, buf, after, name):
    rh = buf.shape[0] // 2

    def body(buf_ref, send_ref, recv_ref, *rest):
        x, y, c, _ = _place()
        cp = pltpu.make_async_remote_copy(src_ref=buf_ref.at[pl.ds(c * rh, rh), :], dst_ref=buf_ref.at[pl.ds((1 - c) * rh, rh), :],
                                          send_sem=send_ref, recv_sem=recv_ref, device_id=(x, y, 1 - c), device_id_type=MESH)
        cp.wait_send()
        cp.wait_recv()

    return pl.pallas_call(
        body,
        name=name,
        in_specs=[HBM, SEM, SEM] + [ANY] * len(after),
        out_specs=HBM,
        out_shape=pltpu.HBM(buf.shape, buf.dtype),
        input_output_aliases={0: 0},
        compiler_params=pltpu.CompilerParams(has_side_effects=EFFECT),
    )(buf, send, recv, *after)


def _half_exchange(bufs, name):
    n = len(bufs)

    def body(*refs):
        outs = refs[n:2 * n]
        send, recv = refs[2 * n:]
        x, y, c, _ = _place()
        copies = []
        for k in range(n):
            rh = bufs[k].shape[0] // 2
            mine = outs[k].at[pl.ds(c * rh, rh), :]
            cp = pltpu.make_async_remote_copy(src_ref=mine, dst_ref=mine, send_sem=send.at[k], recv_sem=recv.at[k],
                                              device_id=(x, y, 1 - c), device_id_type=MESH)
            cp.start()
            copies.append(cp)
        for k in range(n):
            rh = bufs[k].shape[0] // 2
            theirs = outs[k].at[pl.ds((1 - c) * rh, rh), :]
            copies[k].wait_send()
            pltpu.make_async_remote_copy(src_ref=theirs, dst_ref=theirs, send_sem=send.at[k], recv_sem=recv.at[k],
                                         device_id=(x, y, 1 - c), device_id_type=MESH).wait_recv()

    return pl.pallas_call(
        body,
        name=name,
        in_specs=[ANY] * n,
        out_specs=[ANY] * n,
        out_shape=[jax.ShapeDtypeStruct(a.shape, a.dtype) for a in bufs],
        input_output_aliases={k: k for k in range(n)},
        scratch_shapes=[pltpu.SemaphoreType.DMA((n,)), pltpu.SemaphoreType.DMA((n,))],
        compiler_params=pltpu.CompilerParams(has_side_effects=True),
    )(*bufs)


SMALL_ROWS = 32


def _small_peers():
    x, y, c, _ = _place()
    peers = []
    for k in range(1, N_DEV):
        px, py, pc = x ^ ((k >> 2) & 1), y ^ ((k >> 1) & 1), c ^ (k & 1)
        peers.append((k, (px, py, pc), 4 * px + 2 * py + pc))
    return 4 * x + 2 * y + c, peers


def _all_gather_small_start(gath):
    def body(in_ref, send, recv, out_ref, token):
        token[...] = jnp.zeros_like(token)
        me, peers = _small_peers()
        for k, peer, _ in peers:
            pltpu.make_async_remote_copy(src_ref=out_ref.at[me], dst_ref=out_ref.at[me], send_sem=send.at[k], recv_sem=recv.at[k],
                                         device_id=peer, device_id_type=MESH).start()

    return pl.pallas_call(
        body,
        name="small_gather_start",
        in_specs=[HBM],
        out_specs=[SEM, SEM, HBM, VMEM_SPEC],
        out_shape=[pltpu.SemaphoreType.DMA((N_DEV,)), pltpu.SemaphoreType.DMA((N_DEV,)), pltpu.HBM(gath.shape, gath.dtype), TOKEN],
        input_output_aliases={0: 2},
        compiler_params=pltpu.CompilerParams(has_side_effects=EFFECT),
    )(_in_hbm(gath))


def _all_gather_small_wait(send, recv, gath, after):
    def body(in_ref, send_ref, recv_ref, *rest):
        me, peers = _small_peers()
        for k, peer, peer_id in peers:
            cp = pltpu.make_async_remote_copy(src_ref=in_ref.at[me], dst_ref=in_ref.at[peer_id], send_sem=send_ref.at[k],
                                              recv_sem=recv_ref.at[k], device_id=peer, device_id_type=MESH)
            cp.wait_send()
            cp.wait_recv()

    return pl.pallas_call(
        body,
        name="small_gather_wait",
        in_specs=[HBM, SEM, SEM] + [ANY] * len(after),
        out_specs=HBM,
        out_shape=pltpu.HBM(gath.shape, gath.dtype),
        input_output_aliases={0: 0},
        compiler_params=pltpu.CompilerParams(has_side_effects=EFFECT),
    )(gath, send, recv, *after)


def _sum_small(gath):
    def body(g_ref, o_ref):
        acc = g_ref[0]
        for dev in range(1, N_DEV):
            acc = acc + g_ref[dev]
        o_ref[...] = acc

    return pl.pallas_call(
        body,
        name="small_sum",
        in_specs=[VMEM_SPEC],
        out_specs=VMEM_SPEC,
        out_shape=jax.ShapeDtypeStruct(gath.shape[1:], F32),
        compiler_params=pltpu.CompilerParams(vmem_limit_bytes=VMEM_LIMIT),
    )(gath)


def _adamw_update(g_ref, w_ref, m_ref, v_ref, go_ref, d_ref, mo_ref, vo_ref):
    bc1 = 1.0 - ADAM_B1 ** ADAM_STEP
    bc2 = 1.0 - ADAM_B2 ** ADAM_STEP
    gv = g_ref[...]
    mn = ADAM_B1 * m_ref[...] + (1.0 - ADAM_B1) * gv
    vn = ADAM_B2 * v_ref[...] + (1.0 - ADAM_B2) * (gv * gv)
    go_ref[...] = gv
    mo_ref[...] = mn
    vo_ref[...] = vn
    d_ref[...] = -ADAM_LR * ((mn / bc1) / (jnp.sqrt(vn / bc2) + ADAM_EPS) + ADAM_WD * w_ref[...])


def _adamw_small(gs, ws, ms, vs):
    n = len(gs)

    def body(*refs):
        ins, outs = refs[:4 * n], refs[4 * n:]
        for k in range(n):
            _adamw_update(*[ins[i * n + k] for i in range(4)], *outs[4 * k:4 * k + 4])

    vmem = pl.BlockSpec(memory_space=pltpu.VMEM)
    res = pl.pallas_call(
        body,
        name="adamw_small",
        in_specs=[vmem] * (4 * n),
        out_specs=[vmem] * (4 * n),
        out_shape=[jax.ShapeDtypeStruct(w.shape, F32) for w in ws for _ in range(4)],
        compiler_params=pltpu.CompilerParams(vmem_limit_bytes=VMEM_LIMIT),
    )(*gs, *ws, *ms, *vs)
    return [tuple(res[4 * k:4 * k + 4]) for k in range(n)]


def _adamw(g, w, m, v, name, g_block=0):
    r, cdim = w.shape
    tr = _row_tile(r, cdim, 4)

    def body(*refs):
        _adamw_update(*refs)

    spec = pl.BlockSpec((tr, cdim), lambda i: (i, 0))
    return pl.pallas_call(
        body,
        name=name,
        grid=(r // tr,),
        in_specs=[pl.BlockSpec((tr, cdim), lambda i: (i, g_block))] + [spec] * 3,
        out_specs=[spec] * 4,
        out_shape=[jax.ShapeDtypeStruct((r, cdim), F32)] * 4,
        compiler_params=_params(("parallel",)),
    )(g, w, m, v)


VEC_NAMES = ["pre_mix_norm", "q_norm", "kv_norm", "conv_b", "conv_ln_g", "conv_ln_b", "conv_out_norm",
             "attn_out_norm", "post_mix_norm", "pre_ffn_norm", "post_ffn_norm"]
LOSS_ROW = len(VEC_NAMES)
CONV_W_ROW = 16


def _cols_to_full(parts):
    _, r, cdim = parts.shape
    return parts.transpose(1, 0, 2).reshape(r, N_CHIPS * cdim)


def _full_to_cols(full):
    r, n = full.shape
    return full.reshape(r, N_CHIPS, n // N_CHIPS).transpose(1, 0, 2)


W_IN_SHARD = (2 * CONV_CH + Q_LORA + KV_LORA + QK_ROPE) // N_CHIPS
W_IN_PART = 1024
W_IN_BLOCKS = (2 * CONV_CH + Z2_COLS) // LANES
W_IN_BASE = [p * W_IN_SHARD // LANES for p in range(N_CHIPS)]
W_IN_SPAN = [-(-(p * W_IN_SHARD % LANES + W_IN_SHARD) // LANES) for p in range(N_CHIPS)]


def _w_in_block_home(b):
    n = CONV_CH // LANES
    if b < n:
        return 0, 2 * b
    if b < 2 * n:
        return 0, 2 * (b - n) + 1
    return 1, b - 2 * n


def _to_parts_w_in(shift_chip, w_in_t):
    r = w_in_t.shape[1]
    tr = 512

    def body(s_ref, a_ref, o_ref, scr):
        scr[...] = jnp.zeros_like(scr)
        scr[pl.ds(pl.multiple_of(s_ref[0], 8), W_IN_SHARD), :] = a_ref[...]
        o_ref[...] = scr[...].T.astype(o_ref.dtype)

    return pl.pallas_call(
        body,
        name="to_parts_w_in",
        grid_spec=pltpu.PrefetchScalarGridSpec(
            num_scalar_prefetch=1,
            grid=(r // tr,),
            in_specs=[pl.BlockSpec((W_IN_SHARD, tr), lambda i, s_ref: (0, i))],
            out_specs=pl.BlockSpec((None, tr, W_IN_PART), lambda i, s_ref: (s_ref[1], i, 0)),
            scratch_shapes=[pltpu.VMEM((W_IN_PART, tr), F32)]),
        out_shape=jax.ShapeDtypeStruct((N_CHIPS, r, W_IN_PART), BF16),
        compiler_params=_params(("parallel",)),
    )(shift_chip, w_in_t)


def _adamw_w_in(shift, g_shifted, w_t, m_t, v_t):
    cols = ROW_TILE

    def body(s_ref, g_ref, w_ref, m_ref, v_ref, go_ref, d_ref, mo_ref, vo_ref, scr):
        scr[...] = g_ref[...].T
        _adamw_update(scr.at[pl.ds(pl.multiple_of(s_ref[0], 8), W_IN_SHARD), :], w_ref, m_ref, v_ref, go_ref, d_ref, mo_ref, vo_ref)

    spec = pl.BlockSpec((W_IN_SHARD, cols), lambda i, s_ref: (0, i))
    return pl.pallas_call(
        body,
        name="adamw_w_in",
        grid_spec=pltpu.PrefetchScalarGridSpec(
            num_scalar_prefetch=1,
            grid=(w_t.shape[1] // cols,),
            in_specs=[pl.BlockSpec((cols, W_IN_PART), lambda i, s_ref: (i, 0))] + [spec] * 3,
            out_specs=[spec] * 4,
            scratch_shapes=[pltpu.VMEM((W_IN_PART, cols), F32)]),
        out_shape=[jax.ShapeDtypeStruct(w_t.shape, F32)] * 4,
        compiler_params=_params(("parallel",)),
    )(shift, g_shifted, w_t, m_t, v_t)


def _assemble_w_in(parts):
    r = parts.shape[1]
    tr = ROW_TILE

    def body(p_ref, ag_ref, z2_ref):
        outs = (ag_ref, z2_ref)
        for b in range(W_IN_BLOCKS):
            blk = None
            for p in range(N_CHIPS):
                i = b - W_IN_BASE[p]
                if 0 <= i < W_IN_SPAN[p]:
                    piece = p_ref[p, :, i * LANES:(i + 1) * LANES]
                    blk = piece if blk is None else blk + piece
            which, at = _w_in_block_home(b)
            outs[which][:, at * LANES:(at + 1) * LANES] = blk

    w_ag, w_z2 = pl.pallas_call(
        body,
        name="assemble_w_in",
        grid=(r // tr,),
        in_specs=[pl.BlockSpec((N_CHIPS, tr, W_IN_PART), lambda i: (0, i, 0))],
        out_specs=[pl.BlockSpec((tr, 2 * CONV_CH), lambda i: (i, 0)), pl.BlockSpec((tr, Z2_COLS), lambda i: (i, 0))],
        out_shape=[jax.ShapeDtypeStruct((r, 2 * CONV_CH), parts.dtype), jax.ShapeDtypeStruct((r, Z2_COLS), parts.dtype)],
        compiler_params=_params(("parallel",)),
    )(parts)
    return dict(w_ag=w_ag, w_z2=w_z2)


def _w_in_grad_parts(dw_ag, dw_z2):
    r = dw_ag.shape[0]
    tr = ROW_TILE

    def body(ag_ref, z2_ref, o_ref):
        ins = (ag_ref, z2_ref)
        for p in range(N_CHIPS):
            for i in range(W_IN_PART // LANES):
                if i < W_IN_SPAN[p]:
                    which, at = _w_in_block_home(W_IN_BASE[p] + i)
                    o_ref[p, :, i * LANES:(i + 1) * LANES] = ins[which][:, at * LANES:(at + 1) * LANES]
                else:
                    o_ref[p, :, i * LANES:(i + 1) * LANES] = jnp.zeros((tr, LANES), o_ref.dtype)

    return pl.pallas_call(
        body,
        name="w_in_grad_parts",
        grid=(r // tr,),
        in_specs=[pl.BlockSpec((tr, 2 * CONV_CH), lambda i: (i, 0)), pl.BlockSpec((tr, Z2_COLS), lambda i: (i, 0))],
        out_specs=pl.BlockSpec((N_CHIPS, tr, W_IN_PART), lambda i: (0, i, 0)),
        out_shape=jax.ShapeDtypeStruct((N_CHIPS, r, W_IN_PART), dw_ag.dtype),
        compiler_params=_params(("parallel",)),
    )(dw_ag, dw_z2)


def _assemble_mixer_rest(g):
    uq = _cols_to_full(g["w_uq"]).reshape(Q_LORA, N_HEADS, QK_HEAD)
    w_uq = jnp.concatenate([uq[:, :, :QK_NOPE].reshape(Q_LORA, N_HEADS * QK_NOPE),
                            uq[:, :, QK_NOPE:].reshape(Q_LORA, N_HEADS * QK_ROPE)], axis=1)
    return dict(w_uq=w_uq, w_ukv=_cols_to_full(g["w_ukv"]), conv_w=_cols_to_full(g["conv_w"]),
                w_out=g["w_out"].reshape(-1, g["w_out"].shape[2]))


def _grads_to_parts(dw):
    uq = dw["w_uq"]
    d_uq = jnp.concatenate([uq[:, :N_HEADS * QK_NOPE].reshape(Q_LORA, N_HEADS, QK_NOPE),
                            uq[:, N_HEADS * QK_NOPE:].reshape(Q_LORA, N_HEADS, QK_ROPE)], axis=2).reshape(Q_LORA, N_HEADS * QK_HEAD)
    return dict(w_in=_w_in_grad_parts(dw["w_ag"], dw["w_z2"]), w_uq=_full_to_cols(d_uq), w_ukv=_full_to_cols(dw["w_ukv"]),
                w_out=dw["w_out"].reshape(N_CHIPS, -1, dw["w_out"].shape[1]))


MIXER = ["w_in", "w_uq", "w_ukv", "w_out"]
FFN = ["w_gu", "w_down"]


def _pad_lanes(v, n):
    return jnp.pad(v, ((0, 0), (0, n - v.shape[1])))


def kernel(x, positions, pre_mix_norm, w_in, q_norm, w_uq, kv_norm, w_ukv, conv_w, conv_b, conv_ln_g, conv_ln_b, conv_out_norm, attn_out_norm, w_out, post_mix_norm, pre_ffn_norm, w_gate, w_up, w_down, post_ffn_norm, loss_target, m_pre_mix_norm, m_w_in, m_q_norm, m_w_uq, m_kv_norm, m_w_ukv, m_conv_w, m_conv_b, m_conv_ln_g, m_conv_ln_b, m_conv_out_norm, m_attn_out_norm, m_w_out, m_post_mix_norm, m_pre_ffn_norm, m_w_gate, m_w_up, m_w_down, m_post_ffn_norm, v_pre_mix_norm, v_w_in, v_q_norm, v_w_uq, v_kv_norm, v_w_ukv, v_conv_w, v_conv_b, v_conv_ln_g, v_conv_ln_b, v_conv_out_norm, v_attn_out_norm, v_w_out, v_post_mix_norm, v_pre_ffn_norm, v_w_gate, v_w_up, v_w_down, v_post_ffn_norm):
    given = dict(locals())
    names = ["pre_mix_norm", "w_in", "q_norm", "w_uq", "kv_norm", "w_ukv", "conv_w", "conv_b", "conv_ln_g", "conv_ln_b",
             "conv_out_norm", "attn_out_norm", "w_out", "post_mix_norm", "pre_ffn_norm", "w_gate", "w_up", "w_down", "post_ffn_norm"]
    def as_2d(a):
        return a if a.ndim == 2 else a[0]

    weights = {n: as_2d(given[n]) for n in names}
    mom = {n: as_2d(given["m_" + n]) for n in names}
    var = {n: as_2d(given["v_" + n]) for n in names}
    d = D_MODEL

    inv_freq = ROPE_THETA ** (-jnp.arange(0, QK_ROPE, 2, dtype=F32) / QK_ROPE)
    ang = positions[0].astype(F32)[:, None] * inv_freq
    cos, sin = jnp.cos(ang), jnp.sin(ang)
    cos2 = jnp.concatenate([cos, cos, cos, cos], axis=1)
    sin2 = jnp.concatenate([-sin, sin, -sin, sin], axis=1)

    chip = 2 * lax.axis_index("x") + lax.axis_index("y")
    core = lax.axis_index("c")
    chip1 = chip.astype(jnp.int32).reshape(1)
    pieces = {n: [weights[n]] for n in ("w_uq", "w_ukv", "w_out", "w_down")}
    pieces["w_gu"] = [weights["w_gate"], weights["w_up"]]
    core1 = core.astype(jnp.int32).reshape(1)
    place = jnp.stack([chip, core]).astype(jnp.int32)
    rest = ["w_uq", "w_ukv", "w_out"]
    w_in_shift = (chip * W_IN_SHARD) % LANES
    w_in_t, m_in_t, v_in_t = (jnp.swapaxes(given[n][0], 0, 1) for n in ("w_in", "m_w_in", "v_w_in"))
    w_in_buf = _to_parts_w_in(jnp.stack([w_in_shift, chip]).astype(jnp.int32), w_in_t)
    (w_in_sems,), w_in_thru, _ = _gather_start([w_in_buf], [False], [[0]], "gather_start_w_in")
    rest_bufs = [_to_parts(chip1, pieces[n], BF16, "to_parts_" + n) for n in rest]
    rest_bufs.append(_to_parts(chip1, [jnp.pad(weights["conv_w"], ((0, CONV_K_PAD - CONV_K), (0, 0)))], F32, "to_parts_conv_w"))
    rest_whole = [False] * 3 + [True]
    ffn_bufs = [_to_parts(chip1, pieces[n], BF16, "to_parts_" + n) for n in FFN]
    got = _gather_wait(w_in_thru, [False], *w_in_sems, rest_bufs + ffn_bufs, "gather_wait_w_in")
    (rest_sems,), rest_thru, started = _gather_start(rest_bufs, rest_whole, [[0, 1, 2, 3]], "gather_start_mixer_rest", after=got)
    full = _assemble_w_in(_gather_hand_on(got, "gather_hand_on_w_in", after=[started])[0])
    vec = {n: weights[n] for n in VEC_NAMES}
    rs = {}

    def get_mixer_rest(after):
        got = _gather_wait(rest_thru, rest_whole, *rest_sems, [after], "gather_wait_mixer_rest")
        (rs["w_gu_sems"],), rs["w_gu_thru"], started = _gather_start(ffn_bufs[:1], [False], [[0]], "gather_start_w_gu", after=got[:1])
        got = list(_gather_hand_on(got[:3], "gather_hand_on_mixer_rest", after=[started])) + [got[3]]
        return _assemble_mixer_rest(dict(zip(rest + ["conv_w"], got)))

    def w_gu_landed(after):
        got = _gather_wait(rs["w_gu_thru"], [False], *rs["w_gu_sems"], [after], "gather_wait_w_gu")
        (rs["w_down_sems"],), rs["w_down_thru"], started = _gather_start(ffn_bufs[1:], [False], [[0]], "gather_start_w_down", after=got)
        rs["w_gu"] = _hand_on_start(got, "hand_on_start_w_gu", after=[started])
        return rs["w_gu"][3]

    def get_w_gu(after):
        send, recv, bufs, _ = rs["w_gu"]
        return _hand_on_wait(bufs, send, recv, [after], "hand_on_wait_w_gu")[0]

    def get_w_down(after):
        got = _gather_wait(rs["w_down_thru"], [False], *rs["w_down_sems"], [after], "gather_wait_w_down")
        got = _gather_hand_on(got, "gather_hand_on_w_down")[0]
        return got.reshape(-1, got.shape[2])

    def pair_start(key):
        def hook(dw):
            rs[key] = _pair_exchange_start([dw.reshape(N_CHIPS, -1, dw.shape[-1])], "grad_pair_start_" + key)
            return rs[key][4]
        return hook

    def reduce_start(group, plist, landed):
        sums = [_pair_sum(core1, a, b, "grad_pair_sum_%s_%d" % (group, k)) for k, (a, b) in enumerate(zip(plist, landed))]
        rs[group] = _chip_exchange_start(sums, "grad_chip_exchange_start_" + group)
        return rs[group][4]

    def reduce_finish(group, after):
        send, recv, sums, slots, _ = rs[group]
        sums, slots = _chip_exchange_wait(sums, slots, send, recv, after, "grad_chip_exchange_wait_" + group)
        halves = [_chip_sum(place, s, sl, "grad_chip_sum_%s_%d" % (group, k)) for k, (s, sl) in enumerate(zip(sums, slots))]
        return list(_half_exchange(halves, "grad_half_exchange_" + group))

    def ffn_grads_exchanged(after):
        pairs = [_pair_exchange_wait(*rs[key][:4], [after], "grad_pair_wait_" + key) for key in ("dw_gu", "dw_down")]
        return reduce_start("ffn", [p[0][0] for p in pairs], [p[1][0] for p in pairs])

    def mixer_grads(dw_mixer):
        parts = _grads_to_parts(dw_mixer)
        rs["mixer_pairs"] = _pair_exchange_start([parts[n] for n in MIXER], "grad_pair_start_mixer")
        return rs["mixer_pairs"][4]

    hooks = dict(mixer_rest=get_mixer_rest, w_gu_landed=w_gu_landed, w_gu=get_w_gu, w_down=get_w_down, mixer_grads=mixer_grads, dw_down=pair_start("dw_down"), dw_gu=pair_start("dw_gu"),
                 grads_exchanged=ffn_grads_exchanged)
    loss, grad_x, dw, dvec = _local_step(x[0], loss_target[0], cos2, sin2, vec, full, hooks)

    rows = [_pad_lanes(dvec[n], d) for n in VEC_NAMES] + [_pad_lanes(loss, d)]
    rows.append(jnp.zeros((CONV_W_ROW - len(rows), d), F32))
    rows.append(dw["conv_w"].reshape(SMALL_ROWS - CONV_W_ROW, d))
    device1 = (2 * chip + core).astype(jnp.int32).reshape(1)
    small_gather = _all_gather_small_start(_to_parts(device1, [jnp.concatenate(rows, axis=0)], F32, "small_to_slot", n_parts=N_DEV))

    plist, landed = _pair_exchange_wait(*rs["mixer_pairs"][:4], [grad_x, small_gather[3]], "grad_pair_wait_mixer")
    started = reduce_start("mixer", plist, landed)
    send, recv, sums, slots, _ = rs["ffn"]
    sums, slots = _chip_exchange_wait(sums, slots, send, recv, [started, small_gather[3]], "grad_chip_exchange_wait_ffn")
    down = _half_exchange_start(_chip_sum(place, sums[1], slots[1], "grad_chip_sum_ffn_1"), "grad_half_start_w_down")
    gu = _half_exchange_start(_chip_sum(place, sums[0], slots[0], "grad_chip_sum_ffn_0", after=[down[3]]), "grad_half_start_w_gu")
    res = {}
    g_down = _half_exchange_wait(*down[:3], [gu[3]], "grad_half_wait_w_down")
    res["w_down"] = _adamw(g_down, weights["w_down"], mom["w_down"], var["w_down"], "adamw_w_down")
    g_gu = _half_exchange_wait(*gu[:3], [res["w_down"][1]], "grad_half_wait_w_gu")
    for n, blk in (("w_gate", 0), ("w_up", 1)):
        res[n] = _adamw(g_gu, weights[n], mom[n], var[n], "adamw_" + n, g_block=blk)
    small = _sum_small(_all_gather_small_wait(*small_gather[:3], [res["w_up"][1]]))
    g_conv_w_full = small[CONV_W_ROW:].reshape(CONV_K_PAD, CONV_CH)
    g_small = {n: small[i:i + 1, :weights[n].shape[1]] for i, n in enumerate(VEC_NAMES)}
    g_small["conv_w"] = lax.dynamic_slice(g_conv_w_full, (0, chip * (CONV_CH // N_CHIPS)), (CONV_K_PAD, CONV_CH // N_CHIPS))[:CONV_K]
    loss_out = small[LOSS_ROW, 0]
    small_names = VEC_NAMES + ["conv_w"]
    res.update(zip(small_names, _adamw_small([g_small[n] for n in small_names], [weights[n] for n in small_names],
                                             [mom[n] for n in small_names], [var[n] for n in small_names])))
    done_meanwhile = [res["w_gate"][1], res["w_up"][1], res["w_down"][1], res["conv_w"][1], grad_x]
    g_mixer = reduce_finish("mixer", done_meanwhile)
    in_t = _adamw_w_in(w_in_shift.astype(jnp.int32).reshape(1), g_mixer[0], w_in_t, m_in_t, v_in_t)
    res["w_in"] = tuple(jnp.swapaxes(a, 0, 1) for a in in_t)
    for n, g in zip(MIXER[1:], g_mixer[1:]):
        res[n] = _adamw(g, weights[n], mom[n], var[n], "adamw_" + n)
    outs = [loss_out, grad_x[None]]
    for i in range(4):
        outs += [res[n][i].reshape(given[n].shape) for n in names]
    return tuple(outs)
```

```python
import functools

import jax
import jax.numpy as jnp
from jax import lax
from jax.experimental import pallas as pl
from jax.experimental.pallas import tpu as pltpu

F32 = jnp.float32
BF16 = jnp.bfloat16

D_MODEL = 2048
CONV_CH = 1024
CONV_K = 31
CONV_K_PAD = 32
N_HEADS = 8
QK_NOPE = 128
QK_ROPE = 64
V_HEAD = 128
QK_HEAD = QK_NOPE + QK_ROPE
Q_LORA = 768
KV_LORA = 512
ATTN_CH = N_HEADS * V_HEAD
Z2_COLS = Q_LORA + KV_LORA + 128
D_FF = 5632
ROPE_THETA = 10000.0
EPS = 1e-6
LANES = 128
N_CHIPS = 4
N_DEV = 8

ADAM_LR = 0.001
ADAM_B1 = 0.9
ADAM_B2 = 0.999
ADAM_EPS = 1e-08
ADAM_WD = 0.01
ADAM_STEP = 10

VMEM_LIMIT = 56 * 1024 * 1024
ROW_TILE = 256
MAX_TK = 2816
MESH = pl.DeviceIdType.MESH


def _params(sem=None):
    return pltpu.CompilerParams(dimension_semantics=sem, vmem_limit_bytes=VMEM_LIMIT)


def _first_divisor(n, cands):
    for c in cands:
        if n % c == 0:
            return c
    return n


STREAM_BLOCK_BYTES = 3 << 19


def _row_tile(rows, cols, itemsize):
    for tr in (1024, 704, 512, 384, 352, 256, 176, 128, 64, 32, 16):
        if rows % tr == 0 and tr * cols * itemsize <= STREAM_BLOCK_BYTES:
            return tr
    return rows


def _matmul(pairs, mode, out_dtype, name, b_parts=None, out_parts=False, tiles=(None, None, None), after=None):
    a0, b0 = pairs[0]
    part_c = b0.shape[2] if b_parts else None
    if mode == "nn":
        m, n = a0.shape[0], (N_CHIPS * part_c if b_parts else b0.shape[1])
        ks = [a.shape[1] for a, _ in pairs]
    elif mode == "nt":
        m, n = a0.shape[0], b0.shape[-2]
        ks = [a.shape[1] for a, _ in pairs]
    else:
        m, n = a0.shape[1], b0.shape[1]
        ks = [a.shape[0] for a, _ in pairs]
    tm = tiles[0] or _first_divisor(m, (1024, 768, 512, 256))
    tn = tiles[1] or (n if n <= 1536 else _first_divisor(n, (1024, 512, 256, 128)))
    tks = [tiles[2] or (k if k <= MAX_TK else MAX_TK) for k in ks]
    nks = [k // tk for k, tk in zip(ks, tks)]
    offs = [sum(nks[:p]) for p in range(len(pairs))]
    nk = sum(nks)
    n_pairs = len(pairs)
    assert not (b_parts or out_parts) or n_pairs == 1

    def kk(k, p):
        return jnp.clip(k - offs[p], 0, nks[p] - 1)

    in_specs = []
    for p in range(n_pairs):
        tk = tks[p]
        if mode == "nn":
            in_specs.append(pl.BlockSpec((tm, tk), lambda i, j, k, p=p: (i, kk(k, p))))
            if b_parts == "n":
                per = part_c // tn
                in_specs.append(pl.BlockSpec((None, tk, tn), lambda i, j, k: (j // per, k, j % per)))
            else:
                in_specs.append(pl.BlockSpec((tk, tn), lambda i, j, k, p=p: (kk(k, p), j)))
        elif mode == "nt":
            in_specs.append(pl.BlockSpec((tm, tk), lambda i, j, k, p=p: (i, kk(k, p))))
            if b_parts == "k":
                per = part_c // tk
                in_specs.append(pl.BlockSpec((None, tn, tk), lambda i, j, k: (k // per, j, k % per)))
            else:
                in_specs.append(pl.BlockSpec((tn, tk), lambda i, j, k, p=p: (j, kk(k, p))))
        else:
            in_specs.append(pl.BlockSpec((tk, tm), lambda i, j, k, p=p: (kk(k, p), i)))
            in_specs.append(pl.BlockSpec((tk, tn), lambda i, j, k, p=p: (kk(k, p), j)))
    if out_parts:
        out_per = (n // N_CHIPS) // tn
        out_spec = pl.BlockSpec((None, tm, tn), lambda i, j, k: (j // out_per, i, j % out_per))
        out_shape = jax.ShapeDtypeStruct((N_CHIPS, m, n // N_CHIPS), out_dtype)
    else:
        out_spec = pl.BlockSpec((tm, tn), lambda i, j, k: (i, j))
        out_shape = jax.ShapeDtypeStruct((m, n), out_dtype)
    dims = {"nn": (((1,), (0,)), ((), ())), "nt": (((1,), (1,)), ((), ())), "tn": (((0,), (0,)), ((), ()))}[mode]

    n_after = 0 if after is None else 1

    def body(*refs):
        o_ref = refs[2 * n_pairs + n_after]
        k = pl.program_id(2)

        def prod(p):
            return lax.dot_general(refs[2 * p][...], refs[2 * p + 1][...], dims, preferred_element_type=F32)

        if nk == 1:
            o_ref[...] = prod(0).astype(o_ref.dtype)
            return
        acc = refs[2 * n_pairs + n_after + 1]
        for p in range(n_pairs):
            first, last = offs[p], offs[p] + nks[p] - 1
            lo, hi = max(first, 1), min(last, nk - 2)
            if first == 0:
                @pl.when(k == 0)
                def _(p=p):
                    acc[...] = prod(p)

            if lo <= hi:
                @pl.when((k >= lo) & (k <= hi))
                def _(p=p):
                    acc[...] += prod(p)

            if last == nk - 1:
                @pl.when(k == nk - 1)
                def _(p=p):
                    o_ref[...] = (acc[...] + prod(p)).astype(o_ref.dtype)

    flat = [t for pr in pairs for t in pr] + ([] if after is None else [after])
    return pl.pallas_call(
        body,
        name=name,
        grid=(m // tm, n // tn, nk),
        in_specs=in_specs + [pl.BlockSpec(memory_space=pl.ANY)] * n_after,
        out_specs=out_spec,
        out_shape=out_shape,
        scratch_shapes=[pltpu.VMEM((tm, tn), F32)] if nk > 1 else [],
        compiler_params=_params(("parallel", "parallel", "arbitrary")),
    )(*flat)


F4 = D_FF // N_CHIPS
FFN_TM = 512
FFN_STRIP = 256


def _ffn_up(hf, w_gu):
    t, d = hf.shape

    def body(a_ref, b_ref, gu_ref, act_ref):
        for r in range(0, FFN_TM, FFN_STRIP):
            acc = jnp.dot(a_ref[r:r + FFN_STRIP, :], b_ref[...], preferred_element_type=F32)
            g = acc[:, :F4]
            gu_ref[r:r + FFN_STRIP, :] = acc.astype(gu_ref.dtype)
            act_ref[r:r + FFN_STRIP, :] = (g * _sigmoid(g) * acc[:, F4:]).astype(act_ref.dtype)

    return pl.pallas_call(
        body,
        name="ffn_up",
        grid=(N_CHIPS, t // FFN_TM),
        in_specs=[pl.BlockSpec((FFN_TM, d), lambda q, i: (i, 0)),
                  pl.BlockSpec((None, d, 2 * F4), lambda q, i: (q, 0, 0))],
        out_specs=[pl.BlockSpec((FFN_TM, 2 * F4), lambda q, i: (i, q)),
                   pl.BlockSpec((FFN_TM, F4), lambda q, i: (i, q))],
        out_shape=[jax.ShapeDtypeStruct((t, 2 * D_FF), BF16), jax.ShapeDtypeStruct((t, D_FF), BF16)],
        compiler_params=_params(("parallel", "parallel")),
    )(hf, w_gu)


def _ffn_down_dx(d_ff, w_down, gu):
    t, d = d_ff.shape

    def body(a_ref, b_ref, gu_ref, o_ref):
        for r in range(0, FFN_TM, FFN_STRIP):
            rows = slice(r, r + FFN_STRIP)
            d_act = lax.dot_general(a_ref[rows, :], b_ref[...], NT_DIMS, preferred_element_type=F32)
            g = gu_ref[rows, :F4].astype(F32)
            u = gu_ref[rows, F4:].astype(F32)
            sg = _sigmoid(g)
            dsg = d_act * sg
            o_ref[rows, :F4] = (dsg * u * (1.0 + g - g * sg)).astype(o_ref.dtype)
            o_ref[rows, F4:] = (dsg * g).astype(o_ref.dtype)

    return pl.pallas_call(
        body,
        name="ffn_down_dx",
        grid=(N_CHIPS, t // FFN_TM),
        in_specs=[pl.BlockSpec((FFN_TM, d), lambda q, i: (i, 0)),
                  pl.BlockSpec((F4, d), lambda q, i: (q, 0)),
                  pl.BlockSpec((FFN_TM, 2 * F4), lambda q, i: (i, q))],
        out_specs=pl.BlockSpec((FFN_TM, 2 * F4), lambda q, i: (i, q)),
        out_shape=jax.ShapeDtypeStruct((t, 2 * D_FF), BF16),
        compiler_params=_params(("parallel", "parallel")),
    )(d_ff, w_down, gu)


def _rowwise(fn, row_ins, vec_ins, row_outs, acc_outs, name, after=None):
    t = row_ins[0].shape[0]
    tm = ROW_TILE
    n_in = len(row_ins) + len(vec_ins)
    n_row = len(row_outs)
    extra = [] if after is None else [after]

    def body(*refs):
        ins = [r[...] for r in refs[:n_in]]
        outs = refs[n_in + len(extra):]
        vals = fn(*ins)
        for r, v in zip(outs[:n_row], vals[:n_row]):
            r[...] = v.astype(r.dtype)
        if acc_outs:
            @pl.when(pl.program_id(0) == 0)
            def _():
                for r in outs[n_row:]:
                    r[...] = jnp.zeros_like(r)

            for r, v in zip(outs[n_row:], vals[n_row:]):
                r[...] += v

    in_specs = [pl.BlockSpec((tm, a.shape[1]), lambda i: (i, 0)) for a in row_ins]
    in_specs += [pl.BlockSpec(a.shape, lambda i: (0, 0)) for a in vec_ins]
    out_specs = [pl.BlockSpec((tm, c), lambda i: (i, 0)) for c, _ in row_outs]
    out_specs += [pl.BlockSpec((1, c), lambda i: (0, 0)) for c in acc_outs]
    out_shape = [jax.ShapeDtypeStruct((t, c), dt) for c, dt in row_outs]
    out_shape += [jax.ShapeDtypeStruct((1, c), F32) for c in acc_outs]
    return pl.pallas_call(
        body,
        name=name,
        grid=(t // tm,),
        in_specs=in_specs + [pl.BlockSpec(memory_space=pl.ANY)] * len(extra),
        out_specs=out_specs,
        out_shape=out_shape,
        compiler_params=_params(("arbitrary",)),
    )(*row_ins, *vec_ins, *extra)


def _mean(v):
    return jnp.mean(v, axis=-1, keepdims=True)


def _colsum(v):
    return jnp.sum(v, axis=0, keepdims=True)


def _rms_fwd(v, g):
    r = lax.rsqrt(_mean(v * v) + EPS)
    vhat = v * r
    return vhat * g, vhat, r


def _rms_bwd(dn, vhat, r, g):
    dng = dn * g
    return r * (dng - vhat * _mean(dng * vhat)), _colsum(dn * vhat)


def _swap_rope_halves(v):
    n = v.shape[-1]
    lane = lax.broadcasted_iota(jnp.int32, v.shape, v.ndim - 1)
    return jnp.where(lane % QK_ROPE < QK_ROPE // 2, pltpu.roll(v, n - QK_ROPE // 2, v.ndim - 1),
                     pltpu.roll(v, QK_ROPE // 2, v.ndim - 1))


def _rope(v, cos2, sin2):
    return v * cos2 + _swap_rope_halves(v) * sin2


def _rope_transposed(dv, cos2, sin2):
    return dv * cos2 + _swap_rope_halves(dv * sin2)


def _sigmoid(v):
    return 1.0 / (1.0 + jnp.exp(-v))


CONV_ROWS = 256


def _conv_fwd(ag, conv_w, conv_b):
    t = ag.shape[0]
    cb = LANES

    def body(ag_ref, w_ref, b_ref, o_ref, scr):
        a = ag_ref[:, :cb].astype(F32)
        g = ag_ref[:, cb:].astype(F32)
        scr[pl.ds(0, CONV_K_PAD), :] = jnp.zeros((CONV_K_PAD, cb), F32)
        scr[pl.ds(CONV_K_PAD, t), :] = a * _sigmoid(g)
        for r0 in range(0, t, CONV_ROWS):
            acc = jnp.zeros((CONV_ROWS, cb), F32) + b_ref[...]
            for k in range(CONV_K):
                acc = acc + w_ref[k:k + 1, :] * scr[pl.ds(r0 + CONV_K_PAD - (CONV_K - 1) + k, CONV_ROWS), :]
            o_ref[pl.ds(r0, CONV_ROWS), :] = acc

    return pl.pallas_call(
        body,
        name="conv_fwd",
        grid=(CONV_CH // cb,),
        in_specs=[pl.BlockSpec((t, 2 * cb), lambda j: (0, j)),
                  pl.BlockSpec((CONV_K_PAD, cb), lambda j: (0, j)),
                  pl.BlockSpec((1, cb), lambda j: (0, j))],
        out_specs=pl.BlockSpec((t, cb), lambda j: (0, j)),
        out_shape=jax.ShapeDtypeStruct((t, CONV_CH), F32),
        scratch_shapes=[pltpu.VMEM((t + CONV_K_PAD, cb), F32)],
        compiler_params=_params(("parallel",)),
    )(ag, conv_w, conv_b)


def _conv_bwd(d_u1, ag, conv_w):
    t = ag.shape[0]
    cb = LANES

    def body(du_ref, ag_ref, w_ref, dag_ref, dw_ref, db_ref, su, sd):
        a = ag_ref[:, :cb].astype(F32)
        g = ag_ref[:, cb:].astype(F32)
        sg = _sigmoid(g)
        su[pl.ds(0, CONV_K_PAD), :] = jnp.zeros((CONV_K_PAD, cb), F32)
        su[pl.ds(CONV_K_PAD, t), :] = a * sg
        sd[pl.ds(0, t), :] = du_ref[...]
        sd[pl.ds(t, CONV_K_PAD), :] = jnp.zeros((CONV_K_PAD, cb), F32)
        db_ref[...] = _colsum(du_ref[...])
        dw_ref[...] = jnp.zeros_like(dw_ref)
        for r0 in range(0, t, CONV_ROWS):
            du = sd[pl.ds(r0, CONV_ROWS), :]
            acc = jnp.zeros((CONV_ROWS, cb), F32)
            for k in range(CONV_K):
                acc = acc + w_ref[k:k + 1, :] * sd[pl.ds(r0 + (CONV_K - 1) - k, CONV_ROWS), :]
                dw_ref[k:k + 1, :] += _colsum(du * su[pl.ds(r0 + CONV_K_PAD - (CONV_K - 1) + k, CONV_ROWS), :])
            sgc = sg[r0:r0 + CONV_ROWS]
            ac = a[r0:r0 + CONV_ROWS]
            dag_ref[pl.ds(r0, CONV_ROWS), :cb] = (acc * sgc).astype(dag_ref.dtype)
            dag_ref[pl.ds(r0, CONV_ROWS), cb:] = (acc * ac * sgc * (1.0 - sgc)).astype(dag_ref.dtype)

    return pl.pallas_call(
        body,
        name="conv_bwd",
        grid=(CONV_CH // cb,),
        in_specs=[pl.BlockSpec((t, cb), lambda j: (0, j)),
                  pl.BlockSpec((t, 2 * cb), lambda j: (0, j)),
                  pl.BlockSpec((CONV_K_PAD, cb), lambda j: (0, j))],
        out_specs=[pl.BlockSpec((t, 2 * cb), lambda j: (0, j)),
                   pl.BlockSpec((CONV_K_PAD, cb), lambda j: (0, j)),
                   pl.BlockSpec((1, cb), lambda j: (0, j))],
        out_shape=[jax.ShapeDtypeStruct((t, 2 * CONV_CH), BF16),
                   jax.ShapeDtypeStruct((CONV_K_PAD, CONV_CH), F32),
                   jax.ShapeDtypeStruct((1, CONV_CH), F32)],
        scratch_shapes=[pltpu.VMEM((t + CONV_K_PAD, cb), F32), pltpu.VMEM((t + CONV_K_PAD, cb), F32)],
        compiler_params=_params(("parallel",)),
    )(d_u1, ag, conv_w)


ATT_TQ = 256
ATT_TQ_BWD = 512
NEG = float(jnp.finfo(jnp.float32).min)
SCALE = QK_HEAD ** -0.5
NT_DIMS = (((1,), (1,)), ((), ()))
TN_DIMS = (((0,), (0,)), ((), ()))


def _att_weights(qf, kf, row0):
    s = lax.dot_general(qf, kf, NT_DIMS, preferred_element_type=F32)
    tq, t = s.shape
    qpos = row0 + lax.broadcasted_iota(jnp.int32, (tq, t), 0)
    kpos = lax.broadcasted_iota(jnp.int32, (tq, t), 1)
    s = jnp.where(kpos <= qpos, s, NEG)
    p = jnp.exp(s - jnp.max(s, axis=-1, keepdims=True))
    return p, 1.0 / jnp.sum(p, axis=-1, keepdims=True)


def _scaled_query(qn, roped_half):
    return jnp.concatenate([(qn.astype(F32) * SCALE).astype(BF16), (roped_half * SCALE).astype(BF16)], axis=1)


def _half_mask(shape, which):
    lane = lax.broadcasted_iota(jnp.int32, shape, len(shape) - 1)
    return (lane // QK_ROPE == which).astype(F32)


def _attention_fwd(q, kv, kpe2, cos2, sin2):
    t = q.shape[0]
    tq = ATT_TQ

    def body(qn_ref, qp_ref, c_ref, s_ref, kv_ref, kpe_ref, o_ref):
        roped = _rope(qp_ref[...].astype(F32), c_ref[...], s_ref[...])

        def block(i):
            keys = slice(0, (i + 1) * tq)
            for e in range(2):
                qf = _scaled_query(qn_ref[:, e * QK_NOPE:(e + 1) * QK_NOPE], roped * _half_mask(roped.shape, e))
                kf = jnp.concatenate([kv_ref[keys, e * 256:e * 256 + QK_NOPE], kpe_ref[keys, :]], axis=1)
                p, inv_l = _att_weights(qf, kf, i * tq)
                v = kv_ref[keys, e * 256 + QK_NOPE:(e + 1) * 256]
                o = jnp.dot(p.astype(BF16), v, preferred_element_type=F32) * inv_l
                o_ref[:, e * V_HEAD:(e + 1) * V_HEAD] = o.astype(o_ref.dtype)

        for i in range(t // tq):
            pl.when(pl.program_id(1) == i)(functools.partial(block, i))

    return pl.pallas_call(
        body,
        name="attention_fwd",
        grid=(N_HEADS // 2, t // tq),
        in_specs=[pl.BlockSpec((tq, 2 * QK_NOPE), lambda h, i: (i, h)),
                  pl.BlockSpec((tq, LANES), lambda h, i: (i, N_HEADS + h)),
                  pl.BlockSpec((tq, LANES), lambda h, i: (i, 0)),
                  pl.BlockSpec((tq, LANES), lambda h, i: (i, 0)),
                  pl.BlockSpec((t, 512), lambda h, i: (0, h)),
                  pl.BlockSpec((t, LANES), lambda h, i: (0, 0))],
        out_specs=pl.BlockSpec((tq, 2 * V_HEAD), lambda h, i: (i, h)),
        out_shape=jax.ShapeDtypeStruct((t, ATTN_CH), BF16),
        compiler_params=_params(("parallel", "parallel")),
    )(q, q, cos2, sin2, kv, kpe2)


def _attention_bwd(q, kv, kpe2, cos2, sin2, d_attn):
    t = q.shape[0]
    tq = ATT_TQ_BWD
    n_q = t // tq

    def body(qn_ref, qp_ref, c_ref, s_ref, kv_ref, kpe_ref, do_ref, dqn_ref, dqp_ref, dkv_ref, dkpe_ref, dkv_acc):
        h, i = pl.program_id(0), pl.program_id(1)

        @pl.when(i == 0)
        def _():
            dkv_acc[...] = jnp.zeros_like(dkv_acc)

        @pl.when((i == 0) & (h == 0))
        def _():
            dkpe_ref[...] = jnp.zeros_like(dkpe_ref)

        roped = _rope(qp_ref[...].astype(F32), c_ref[...], s_ref[...])

        def block(ib):
            keys = slice(0, (ib + 1) * tq)
            d_roped = jnp.zeros((tq, LANES), F32)
            for e in range(2):
                mask = _half_mask(roped.shape, e)
                qf = _scaled_query(qn_ref[:, e * QK_NOPE:(e + 1) * QK_NOPE], roped * mask)
                kf = jnp.concatenate([kv_ref[keys, e * 256:e * 256 + QK_NOPE], kpe_ref[keys, :]], axis=1)
                v = kv_ref[keys, e * 256 + QK_NOPE:(e + 1) * 256]
                do = do_ref[:, e * V_HEAD:(e + 1) * V_HEAD]
                p, inv_l = _att_weights(qf, kf, ib * tq)
                p = p * inv_l
                dp = lax.dot_general(do, v, NT_DIMS, preferred_element_type=F32)
                ds = (p * (dp - jnp.sum(p * dp, axis=-1, keepdims=True))).astype(BF16)
                dqf = jnp.dot(ds, kf, preferred_element_type=F32) * SCALE
                dkf = lax.dot_general(ds, qf, TN_DIMS, preferred_element_type=F32)
                dv = lax.dot_general(p.astype(BF16), do, TN_DIMS, preferred_element_type=F32)
                dqn_ref[:, e * QK_NOPE:(e + 1) * QK_NOPE] = dqf[:, :QK_NOPE].astype(dqn_ref.dtype)
                d_roped = d_roped + dqf[:, QK_NOPE:] * mask
                dkv_acc[keys, e * 256:e * 256 + QK_NOPE] += dkf[:, :QK_NOPE]
                dkv_acc[keys, e * 256 + QK_NOPE:(e + 1) * 256] += dv
                dkpe_ref[keys, :] += dkf[:, QK_NOPE:]
            dqp_ref[...] = _rope_transposed(d_roped, c_ref[...], s_ref[...]).astype(dqp_ref.dtype)

        for ib in range(n_q):
            pl.when(i == ib)(functools.partial(block, ib))

        @pl.when(i == n_q - 1)
        def _():
            dkv_ref[...] = dkv_acc[...].astype(dkv_ref.dtype)

    return pl.pallas_call(
        body,
        name="attention_bwd",
        grid=(N_HEADS // 2, n_q),
        in_specs=[pl.BlockSpec((tq, 2 * QK_NOPE), lambda h, i: (i, h)),
                  pl.BlockSpec((tq, LANES), lambda h, i: (i, N_HEADS + h)),
                  pl.BlockSpec((tq, LANES), lambda h, i: (i, 0)),
                  pl.BlockSpec((tq, LANES), lambda h, i: (i, 0)),
                  pl.BlockSpec((t, 512), lambda h, i: (0, h)),
                  pl.BlockSpec((t, LANES), lambda h, i: (0, 0)),
                  pl.BlockSpec((tq, 2 * V_HEAD), lambda h, i: (i, h))],
        out_specs=[pl.BlockSpec((tq, 2 * QK_NOPE), lambda h, i: (i, h)),
                   pl.BlockSpec((tq, LANES), lambda h, i: (i, h)),
                   pl.BlockSpec((t, 512), lambda h, i: (0, h)),
                   pl.BlockSpec((t, LANES), lambda h, i: (0, 0))],
        out_shape=[jax.ShapeDtypeStruct((t, N_HEADS * QK_NOPE), BF16),
                   jax.ShapeDtypeStruct((t, N_HEADS * QK_ROPE), BF16),
                   jax.ShapeDtypeStruct((t, N_HEADS * 256), BF16),
                   jax.ShapeDtypeStruct((t, LANES), F32)],
        scratch_shapes=[pltpu.VMEM((t, 512), F32)],
        compiler_params=_params(("arbitrary", "arbitrary")),
    )(q, q, cos2, sin2, kv, kpe2, d_attn)


def _local_step(x, target, cos2, sin2, vec, w, ffn):
    d = D_MODEL

    (h,) = _rowwise(lambda xv, g: (_rms_fwd(xv, g)[0],), [x], [vec["pre_mix_norm"]], [(d, BF16)], [], "pre_mix_norm_fwd")
    ag = _matmul([(h, w["w_ag"])], "nn", BF16, "in_proj_ag")
    z2 = _matmul([(h, w["w_z2"])], "nn", BF16, "in_proj_z2")
    w = {**w, **ffn["mixer_rest"](z2)}
    u1 = _conv_fwd(ag, w["conv_w"], vec["conv_b"])

    def latents_fwd(z, c2, s2, qg, kvg):
        z = z.astype(F32)
        qn = _rms_fwd(z[:, :Q_LORA], qg)[0]
        kvn = _rms_fwd(z[:, Q_LORA:Q_LORA + KV_LORA], kvg)[0]
        kr = z[:, Q_LORA + KV_LORA:]
        kr2 = kr + pltpu.roll(kr, QK_ROPE, 1)
        return qn, kvn, _rope(kr2, c2, s2)

    qn, kvn, kpe2 = _rowwise(latents_fwd, [z2, cos2, sin2], [vec["q_norm"], vec["kv_norm"]],
                             [(Q_LORA, BF16), (KV_LORA, BF16), (LANES, BF16)], [], "latents_fwd")
    q = _matmul([(qn, w["w_uq"])], "nn", BF16, "q_up")
    kv = _matmul([(kvn, w["w_ukv"])], "nn", BF16, "kv_up")
    attn = _attention_fwd(q, kv, kpe2, cos2, sin2)

    def conv_post(u, lg, lb):
        mu = _mean(u)
        uc = u - mu
        rstd = lax.rsqrt(_mean(uc * uc) + EPS)
        uhat = uc * rstd
        u2 = uhat * lg + lb
        sg = _sigmoid(u2)
        return uhat, rstd, u2, sg, u2 * sg

    def mix_in_fwd(u, at, lg, lb, cg, ag_):
        u3 = conv_post(u, lg, lb)[4]
        cn = _rms_fwd(u3, cg)[0]
        an = _rms_fwd(at.astype(F32), ag_)[0]
        return (jnp.concatenate([cn, an], axis=1),)

    (cat,) = _rowwise(mix_in_fwd, [u1, attn], [vec["conv_ln_g"], vec["conv_ln_b"], vec["conv_out_norm"], vec["attn_out_norm"]],
                      [(2 * CONV_CH, BF16)], [], "mix_in_fwd")
    mix = _matmul([(cat, w["w_out"])], "nn", F32, "out_proj")
    landed = ffn["w_gu_landed"](mix)

    def residual1(xv, mv, gpm, gpf):
        x1 = xv + _rms_fwd(mv, gpm)[0]
        return x1, _rms_fwd(x1, gpf)[0]

    x1, hf = _rowwise(residual1, [x, mix], [vec["post_mix_norm"], vec["pre_ffn_norm"]], [(d, F32), (d, BF16)], [],
                      "residual1_fwd", after=landed)
    w_gu = ffn["w_gu"](hf)
    gu, act = _ffn_up(hf, w_gu)
    w_down = ffn["w_down"](act)
    ff = _matmul([(act, w_down)], "nn", F32, "ffn_down")

    def loss_head(x1v, ffv, tg, g):
        n, fhat, r = _rms_fwd(ffv, g)
        err = x1v + n - tg
        loss = 0.5 * jnp.sum(_mean(err * err), axis=0, keepdims=True)
        dy = err * (1.0 / d)
        d_ff, dg = _rms_bwd(dy, fhat, r, g)
        return dy, d_ff, dg, jnp.broadcast_to(loss, (1, LANES))

    dy, d_ff, g_post_ffn, loss = _rowwise(loss_head, [x1, ff, target], [vec["post_ffn_norm"]],
                                          [(d, F32), (d, BF16)], [d, LANES], "loss_head")
    d_gu = _ffn_down_dx(d_ff, w_down, gu)
    dw_down = _matmul([(act, d_ff)], "tn", BF16, "ffn_down_dw", tiles=(F4, None, None))
    started = ffn["dw_down"](dw_down)
    dw_gu = _matmul([(hf, d_gu)], "tn", BF16, "ffn_gate_up_dw", out_parts=True, tiles=(None, F4, None), after=started)
    started = ffn["dw_gu"](dw_gu)
    d_hf = _matmul([(d_gu, w_gu)], "nt", F32, "ffn_gate_up_dx", b_parts="k", after=started)
    started = ffn["grads_exchanged"](d_hf)

    def residual1_bwd(dyv, dhf, x1v, mv, gpf, gpm):
        _, x1hat, r1 = _rms_fwd(x1v, gpf)
        dn, dgpf = _rms_bwd(dhf, x1hat, r1, gpf)
        d_x1 = dyv + dn
        _, mhat, rm = _rms_fwd(mv, gpm)
        d_mix, dgpm = _rms_bwd(d_x1, mhat, rm, gpm)
        return d_x1, d_mix, dgpf, dgpm

    d_x1, d_mix, g_pre_ffn, g_post_mix = _rowwise(residual1_bwd, [dy, d_hf, x1, mix], [vec["pre_ffn_norm"], vec["post_mix_norm"]],
                                                  [(d, F32), (d, BF16)], [d, d], "residual1_bwd", after=started)
    d_cat = _matmul([(d_mix, w["w_out"])], "nt", BF16, "out_proj_dx")
    dw_out = _matmul([(cat, d_mix)], "tn", BF16, "out_proj_dw")

    def mix_in_bwd(dc, u, at, lg, lb, cg, ag_):
        dc = dc.astype(F32)
        uhat, rstd, u2, sg, u3 = conv_post(u, lg, lb)
        _, u3hat, rc = _rms_fwd(u3, cg)
        d_u3, dcg = _rms_bwd(dc[:, :CONV_CH], u3hat, rc, cg)
        d_u2 = d_u3 * sg * (1.0 + u2 * (1.0 - sg))
        dgl = d_u2 * lg
        d_u1 = rstd * (dgl - _mean(dgl) - uhat * _mean(dgl * uhat))
        _, ahat, ra = _rms_fwd(at.astype(F32), ag_)
        d_at, dag = _rms_bwd(dc[:, CONV_CH:], ahat, ra, ag_)
        return d_u1, d_at, dcg, _colsum(d_u2 * uhat), _colsum(d_u2), dag

    d_u1, d_attn, g_conv_out, g_ln_g, g_ln_b, g_attn_out = _rowwise(
        mix_in_bwd, [d_cat, u1, attn], [vec["conv_ln_g"], vec["conv_ln_b"], vec["conv_out_norm"], vec["attn_out_norm"]],
        [(CONV_CH, F32), (ATTN_CH, BF16)], [CONV_CH] * 4, "mix_in_bwd")
    d_ag, d_conv_w, g_conv_b = _conv_bwd(d_u1, ag, w["conv_w"])
    d_qn_, d_qp_, d_kv, d_kpe2 = _attention_bwd(q, kv, kpe2, cos2, sin2, d_attn)
    d_q = jnp.concatenate([d_qn_, d_qp_], axis=1)
    d_qn = _matmul([(d_q, w["w_uq"])], "nt", BF16, "q_up_dx")
    dw_uq = _matmul([(qn, d_q)], "tn", BF16, "q_up_dw")
    d_kvn = _matmul([(d_kv, w["w_ukv"])], "nt", BF16, "kv_up_dx")
    dw_ukv = _matmul([(kvn, d_kv)], "tn", BF16, "kv_up_dw")

    def latents_bwd(z, dq, dk, dkp, c2, s2, qg, kvg):
        z = z.astype(F32)
        _, qhat, rq = _rms_fwd(z[:, :Q_LORA], qg)
        d_ql, dqg = _rms_bwd(dq.astype(F32), qhat, rq, qg)
        _, khat, rk = _rms_fwd(z[:, Q_LORA:Q_LORA + KV_LORA], kvg)
        d_kl, dkg = _rms_bwd(dk.astype(F32), khat, rk, kvg)
        both = dkp + pltpu.roll(dkp, QK_ROPE, 1)
        d_kr = _rope_transposed(both, c2, s2) * _half_mask(both.shape, 0)
        return jnp.concatenate([d_ql, d_kl, d_kr], axis=1), dqg, dkg

    d_z2, g_q_norm, g_kv_norm = _rowwise(latents_bwd, [z2, d_qn, d_kvn, d_kpe2, cos2, sin2], [vec["q_norm"], vec["kv_norm"]],
                                         [(Z2_COLS, BF16)], [Q_LORA, KV_LORA], "latents_bwd")
    dw_ag = _matmul([(h, d_ag)], "tn", BF16, "in_proj_ag_dw")
    dw_z2 = _matmul([(h, d_z2)], "tn", BF16, "in_proj_z2_dw")
    started = ffn["mixer_grads"](dict(w_ag=dw_ag, w_z2=dw_z2, w_uq=dw_uq, w_ukv=dw_ukv, w_out=dw_out))
    d_h = _matmul([(d_ag, w["w_ag"]), (d_z2, w["w_z2"])], "nt", F32, "in_proj_dx", after=started)

    def pre_mix_bwd(dx1, dh, xv, g):
        _, xhat, r = _rms_fwd(xv, g)
        dn, dg = _rms_bwd(dh, xhat, r, g)
        return dx1 + dn, dg

    grad_x, g_pre_mix = _rowwise(pre_mix_bwd, [d_x1, d_h, x], [vec["pre_mix_norm"]], [(d, F32)], [d], "pre_mix_norm_bwd")

    dw = dict(w_ag=dw_ag, w_z2=dw_z2, w_uq=dw_uq, w_ukv=dw_ukv, conv_w=d_conv_w, w_out=dw_out, w_gu=dw_gu, w_down=dw_down)
    dvec = dict(pre_mix_norm=g_pre_mix, q_norm=g_q_norm, kv_norm=g_kv_norm, conv_b=g_conv_b, conv_ln_g=g_ln_g,
                conv_ln_b=g_ln_b, conv_out_norm=g_conv_out, attn_out_norm=g_attn_out, post_mix_norm=g_post_mix,
                pre_ffn_norm=g_pre_ffn, post_ffn_norm=g_post_ffn)
    return loss, grad_x, dw, dvec


ANY = pl.BlockSpec(memory_space=pl.ANY)


def _place():
    x, y, c = lax.axis_index("x"), lax.axis_index("y"), lax.axis_index("c")
    chips = [(1 - x, y), (x, 1 - y), (1 - x, 1 - y)]
    return x, y, c, chips


def _to_parts(chip, pieces, dtype, name, n_parts=N_CHIPS):
    r = pieces[0].shape[0]
    widths = [a.shape[1] for a in pieces]
    tr = r if r <= 512 else _first_divisor(r, (512, 256, 128))

    def body(p_ref, *refs):
        o_ref = refs[len(pieces)]
        off = 0
        for a_ref, wdt in zip(refs, widths):
            o_ref[:, off:off + wdt] = a_ref[...].astype(o_ref.dtype)
            off += wdt

    return pl.pallas_call(
        body,
        name=name,
        grid_spec=pltpu.PrefetchScalarGridSpec(
            num_scalar_prefetch=1,
            grid=(r // tr,),
            in_specs=[pl.BlockSpec((tr, wdt), lambda i, p_ref: (i, 0)) for wdt in widths],
            out_specs=pl.BlockSpec((None, tr, sum(widths)), lambda i, p_ref: (p_ref[0], i, 0))),
        out_shape=jax.ShapeDtypeStruct((n_parts, r, sum(widths)), dtype),
        compiler_params=_params(("parallel",)),
    )(chip, *pieces)


HBM = pl.BlockSpec(memory_space=pltpu.HBM)
SEM = pl.BlockSpec(memory_space=pltpu.SEMAPHORE)
EFFECT = pltpu.SideEffectType.DATAFLOW_SIDE_EFFECTING
VMEM_SPEC = pl.BlockSpec(memory_space=pltpu.VMEM)
TOKEN = jax.ShapeDtypeStruct((8, LANES), F32)


def _in_hbm(a):
    return pltpu.with_memory_space_constraint(a, pltpu.HBM)


def _gather_rows(buf, whole, half):
    r = buf.shape[1]
    return pl.ds(0, r) if whole else pl.ds(half * (r // 2), r // 2)


def _gather_start(bufs, whole, groups, name, after=()):
    n = len(bufs)
    n_g = len(groups)

    def body(*refs):
        refs = refs[:n] + refs[n + len(after):]
        sems = refs[n:n + 2 * n_g]
        outs = refs[n + 2 * n_g:2 * n + 2 * n_g]
        token = refs[2 * n + 2 * n_g]
        token[...] = jnp.zeros_like(token)
        x, y, c, chips = _place()
        p = 2 * x + y
        for gi, group in enumerate(groups):
            for ki, k in enumerate(group):
                blk = outs[k].at[p, _gather_rows(bufs[k], whole[k], c), :]
                for j, (px, py) in enumerate(chips):
                    pltpu.make_async_remote_copy(src_ref=blk, dst_ref=blk, send_sem=sems[2 * gi].at[3 * ki + j],
                                                 recv_sem=sems[2 * gi + 1].at[3 * ki + j],
                                                 device_id=(px, py, c), device_id_type=MESH).start()

    sem_shapes = []
    for group in groups:
        sem_shapes += [pltpu.SemaphoreType.DMA((3 * len(group),))] * 2
    res = pl.pallas_call(
        body,
        name=name,
        in_specs=[HBM] * n + [ANY] * len(after),
        out_specs=[SEM] * (2 * n_g) + [HBM] * n + [VMEM_SPEC],
        out_shape=sem_shapes + [pltpu.HBM(a.shape, a.dtype) for a in bufs] + [TOKEN],
        input_output_aliases={k: 2 * n_g + k for k in range(n)},
        compiler_params=pltpu.CompilerParams(has_side_effects=EFFECT),
    )(*[_in_hbm(a) for a in bufs], *after)
    sems = [(res[2 * gi], res[2 * gi + 1]) for gi in range(n_g)]
    return sems, list(res[2 * n_g:2 * n_g + n]), res[2 * n_g + n]


def _gather_wait(bufs, whole, send, recv, after, name):
    n = len(bufs)

    def body(*refs):
        ins = refs[:n]
        send_ref, recv_ref = refs[n], refs[n + 1]
        x, y, c, chips = _place()
        p = 2 * x + y
        for ki in range(n):
            rows = _gather_rows(bufs[ki], whole[ki], c)
            for j, (px, py) in enumerate(chips):
                cp = pltpu.make_async_remote_copy(src_ref=ins[ki].at[p, rows, :], dst_ref=ins[ki].at[2 * px + py, rows, :],
                                                  send_sem=send_ref.at[3 * ki + j], recv_sem=recv_ref.at[3 * ki + j],
                                                  device_id=(px, py, c), device_id_type=MESH)
                cp.wait_send()
                cp.wait_recv()

    res = pl.pallas_call(
        body,
        name=name,
        in_specs=[HBM] * n + [SEM, SEM] + [ANY] * len(after),
        out_specs=[HBM] * n,
        out_shape=[pltpu.HBM(a.shape, a.dtype) for a in bufs],
        input_output_aliases={k: k for k in range(n)},
        compiler_params=pltpu.CompilerParams(has_side_effects=EFFECT),
    )(*bufs, send, recv, *after)
    return list(res)


def _gather_hand_on(bufs, name, after=()):
    n = len(bufs)

    def body(*refs):
        refs = refs[n + len(after):]
        outs = refs[:n]
        send, recv = refs[n:]
        x, y, c, chips = _place()

        def d2d(k, j, half):
            px, py = chips[j]
            blk = outs[k].at[2 * px + py, _gather_rows(bufs[k], False, half), :]
            return pltpu.make_async_remote_copy(src_ref=blk, dst_ref=blk, send_sem=send.at[3 * k + j], recv_sem=recv.at[3 * k + j],
                                                device_id=(x, y, 1 - c), device_id_type=MESH)

        sent = [d2d(k, j, c) for k in range(n) for j in range(3)]
        for cp in sent:
            cp.start()
        for k in range(n):
            for j in range(3):
                d2d(k, j, 1 - c).wait_recv()
        for cp in sent:
            cp.wait_send()

    return pl.pallas_call(
        body,
        name=name,
        in_specs=[ANY] * (n + len(after)),
        out_specs=[ANY] * n,
        out_shape=[jax.ShapeDtypeStruct(a.shape, a.dtype) for a in bufs],
        input_output_aliases={k: k for k in range(n)},
        scratch_shapes=[pltpu.SemaphoreType.DMA((3 * n,)), pltpu.SemaphoreType.DMA((3 * n,))],
        compiler_params=pltpu.CompilerParams(has_side_effects=True),
    )(*bufs, *after)


def _hand_on_start(bufs, name, after=()):
    n = len(bufs)

    def body(*refs):
        refs = refs[:n] + refs[n + len(after):]
        send, recv = refs[n], refs[n + 1]
        outs = refs[n + 2:2 * n + 2]
        refs[2 * n + 2][...] = jnp.zeros(TOKEN.shape, TOKEN.dtype)
        x, y, c, chips = _place()
        for k in range(n):
            for j, (px, py) in enumerate(chips):
                blk = outs[k].at[2 * px + py, _gather_rows(bufs[k], False, c), :]
                pltpu.make_async_remote_copy(src_ref=blk, dst_ref=blk, send_sem=send.at[3 * k + j], recv_sem=recv.at[3 * k + j],
                                             device_id=(x, y, 1 - c), device_id_type=MESH).start()

    res = pl.pallas_call(
        body,
        name=name,
        in_specs=[HBM] * n + [ANY] * len(after),
        out_specs=[SEM, SEM] + [HBM] * n + [VMEM_SPEC],
        out_shape=[pltpu.SemaphoreType.DMA((3 * n,))] * 2 + [pltpu.HBM(a.shape, a.dtype) for a in bufs] + [TOKEN],
        input_output_aliases={k: 2 + k for k in range(n)},
        compiler_params=pltpu.CompilerParams(has_side_effects=EFFECT),
    )(*[_in_hbm(a) for a in bufs], *after)
    return res[0], res[1], list(res[2:2 + n]), res[2 + n]


def _hand_on_wait(bufs, send, recv, after, name):
    n = len(bufs)

    def body(*refs):
        ins = refs[:n]
        send_ref, recv_ref = refs[n], refs[n + 1]
        x, y, c, chips = _place()
        for k in range(n):
            for j, (px, py) in enumerate(chips):
                q = 2 * px + py
                cp = pltpu.make_async_remote_copy(src_ref=ins[k].at[q, _gather_rows(bufs[k], False, c), :],
                                                  dst_ref=ins[k].at[q, _gather_rows(bufs[k], False, 1 - c), :],
                                                  send_sem=send_ref.at[3 * k + j], recv_sem=recv_ref.at[3 * k + j],
                                                  device_id=(x, y, 1 - c), device_id_type=MESH)
                cp.wait_send()
                cp.wait_recv()

    res = pl.pallas_call(
        body,
        name=name,
        in_specs=[HBM] * n + [SEM, SEM] + [ANY] * len(after),
        out_specs=[HBM] * n,
        out_shape=[pltpu.HBM(a.shape, a.dtype) for a in bufs],
        input_output_aliases={k: k for k in range(n)},
        compiler_params=pltpu.CompilerParams(has_side_effects=EFFECT),
    )(*bufs, send, recv, *after)
    return list(res)


def _pair_exchange_start(parts, name):
    n = len(parts)
    lands = [(N_CHIPS, a.shape[1] // 2, a.shape[2]) for a in parts]

    def body(*refs):
        send, recv = refs[2 * n], refs[2 * n + 1]
        src, dst = refs[2 * n + 2:3 * n + 2], refs[3 * n + 2:4 * n + 2]
        refs[4 * n + 2][...] = jnp.zeros(TOKEN.shape, TOKEN.dtype)
        x, y, c, _ = _place()
        for k in range(n):
            rh = lands[k][1]
            pltpu.make_async_remote_copy(src_ref=src[k].at[:, pl.ds((1 - c) * rh, rh), :], dst_ref=dst[k],
                                         send_sem=send.at[k], recv_sem=recv.at[k],
                                         device_id=(x, y, 1 - c), device_id_type=MESH).start()

    res = pl.pallas_call(
        body,
        name=name,
        in_specs=[HBM] * (2 * n),
        out_specs=[SEM, SEM] + [HBM] * (2 * n) + [VMEM_SPEC],
        out_shape=[pltpu.SemaphoreType.DMA((n,))] * 2 + [pltpu.HBM(a.shape, a.dtype) for a in parts]
        + [pltpu.HBM(s, a.dtype) for s, a in zip(lands, parts)] + [TOKEN],
        input_output_aliases={k: 2 + k for k in range(2 * n)},
        compiler_params=pltpu.CompilerParams(has_side_effects=EFFECT),
    )(*[_in_hbm(a) for a in parts], *[_in_hbm(lax.empty(s, a.dtype)) for s, a in zip(lands, parts)])
    return res[0], res[1], list(res[2:2 + n]), list(res[2 + n:2 + 2 * n]), res[2 + 2 * n]


def _pair_exchange_wait(send, recv, parts, lands, after, name):
    n = len(parts)

    def body(*refs):
        src, dst = refs[:n], refs[n:2 * n]
        send_ref, recv_ref = refs[2 * n], refs[2 * n + 1]
        x, y, c, _ = _place()
        for k in range(n):
            rh = parts[k].shape[1] // 2
            cp = pltpu.make_async_remote_copy(src_ref=src[k].at[:, pl.ds((1 - c) * rh, rh), :], dst_ref=dst[k],
                                              send_sem=send_ref.at[k], recv_sem=recv_ref.at[k],
                                              device_id=(x, y, 1 - c), device_id_type=MESH)
            cp.wait_send()
            cp.wait_recv()

    res = pl.pallas_call(
        body,
        name=name,
        in_specs=[HBM] * (2 * n) + [SEM, SEM] + [ANY] * len(after),
        out_specs=[HBM] * (2 * n),
        out_shape=[pltpu.HBM(a.shape, a.dtype) for a in parts] + [pltpu.HBM(a.shape, a.dtype) for a in lands],
        input_output_aliases={k: k for k in range(2 * n)},
        compiler_params=pltpu.CompilerParams(has_side_effects=EFFECT),
    )(*parts, *lands, send, recv, *after)
    return list(res[:n]), list(res[n:])


def _chip_exchange_start(sums, name):
    n = len(sums)

    def body(*refs):
        send, recv = refs[2 * n], refs[2 * n + 1]
        src = refs[2 * n + 2:3 * n + 2]
        dst = refs[3 * n + 2:4 * n + 2]
        refs[4 * n + 2][...] = jnp.zeros(TOKEN.shape, TOKEN.dtype)
        x, y, c, chips = _place()
        p = 2 * x + y
        for k in range(n):
            for j, (px, py) in enumerate(chips):
                pltpu.make_async_remote_copy(src_ref=src[k].at[2 * px + py], dst_ref=dst[k].at[p],
                                             send_sem=send.at[3 * k + j], recv_sem=recv.at[3 * k + j],
                                             device_id=(px, py, c), device_id_type=MESH).start()

    res = pl.pallas_call(
        body,
        name=name,
        in_specs=[HBM] * (2 * n),
        out_specs=[SEM, SEM] + [HBM] * (2 * n) + [VMEM_SPEC],
        out_shape=[pltpu.SemaphoreType.DMA((3 * n,))] * 2 + [pltpu.HBM(a.shape, a.dtype) for a in sums] * 2 + [TOKEN],
        input_output_aliases={k: 2 + k for k in range(2 * n)},
        compiler_params=pltpu.CompilerParams(has_side_effects=EFFECT),
    )(*[_in_hbm(a) for a in sums], *[_in_hbm(lax.empty(a.shape, a.dtype)) for a in sums])
    return res[0], res[1], list(res[2:2 + n]), list(res[2 + n:2 + 2 * n]), res[2 + 2 * n]


def _chip_exchange_wait(sums, slots, send, recv, after, name):
    n = len(sums)

    def body(*refs):
        src, dst = refs[:n], refs[n:2 * n]
        send_ref, recv_ref = refs[2 * n], refs[2 * n + 1]
        x, y, c, chips = _place()
        for k in range(n):
            for j, (px, py) in enumerate(chips):
                cp = pltpu.make_async_remote_copy(src_ref=src[k].at[2 * px + py], dst_ref=dst[k].at[2 * px + py],
                                                  send_sem=send_ref.at[3 * k + j], recv_sem=recv_ref.at[3 * k + j],
                                                  device_id=(px, py, c), device_id_type=MESH)
                cp.wait_send()
                cp.wait_recv()

    res = pl.pallas_call(
        body,
        name=name,
        in_specs=[HBM] * (2 * n) + [SEM, SEM] + [ANY] * len(after),
        out_specs=[HBM] * (2 * n),
        out_shape=[pltpu.HBM(a.shape, a.dtype) for a in sums] * 2,
        input_output_aliases={k: k for k in range(2 * n)},
        compiler_params=pltpu.CompilerParams(has_side_effects=EFFECT),
    )(*sums, *slots, send, recv, *after)
    return list(res[:n]), list(res[n:])


def _pair_sum(core, part, landed, name):
    _, r, cdim = part.shape
    rh = r // 2
    tr = _row_tile(rh, cdim, 2)
    nb = rh // tr

    def body(c_ref, a_ref, b_ref, o_ref):
        o_ref[...] = (a_ref[...].astype(F32) + b_ref[...].astype(F32)).astype(o_ref.dtype)

    return pl.pallas_call(
        body,
        name=name,
        grid_spec=pltpu.PrefetchScalarGridSpec(
            num_scalar_prefetch=1,
            grid=(N_CHIPS, nb),
            in_specs=[pl.BlockSpec((None, tr, cdim), lambda q, i, c_ref: (q, c_ref[0] * nb + i, 0)),
                      pl.BlockSpec((None, tr, cdim), lambda q, i, c_ref: (q, i, 0))],
            out_specs=pl.BlockSpec((None, tr, cdim), lambda q, i, c_ref: (q, i, 0))),
        out_shape=jax.ShapeDtypeStruct((N_CHIPS, rh, cdim), BF16),
        compiler_params=_params(("parallel", "parallel")),
    )(core, part, landed)


def _chip_sum(place, own, slots, name, after=()):
    _, rh, cdim = slots.shape
    tr = _row_tile(rh, cdim, 2)
    nb = rh // tr

    def body(place_ref, own_ref, s1_ref, s2_ref, s3_ref, *rest):
        o_ref = rest[len(after)]
        acc = own_ref[...].astype(F32)
        for s_ref in (s1_ref, s2_ref, s3_ref):
            acc = acc + s_ref[...].astype(F32)
        o_ref[...] = acc

    def other(j):
        return lambda i, place_ref: ((place_ref[0] + j) % N_CHIPS, i, 0)

    return pl.pallas_call(
        body,
        name=name,
        grid_spec=pltpu.PrefetchScalarGridSpec(
            num_scalar_prefetch=1,
            grid=(nb,),
            in_specs=[pl.BlockSpec((None, tr, cdim), other(j)) for j in (0, 1, 2, 3)] + [ANY] * len(after),
            out_specs=pl.BlockSpec((tr, cdim), lambda i, place_ref: (place_ref[1] * nb + i, 0))),
        out_shape=jax.ShapeDtypeStruct((2 * rh, cdim), F32),
        compiler_params=_params(("parallel",)),
    )(place, own, slots, slots, slots, *after)


def _half_exchange_start(buf, name):
    rh = buf.shape[0] // 2

    def body(buf_ref, send, recv, out_ref, token):
        token[...] = jnp.zeros_like(token)
        x, y, c, _ = _place()
        mine = out_ref.at[pl.ds(c * rh, rh), :]
        pltpu.make_async_remote_copy(src_ref=mine, dst_ref=mine, send_sem=send, recv_sem=recv,
                                     device_id=(x, y, 1 - c), device_id_type=MESH).start()

    return pl.pallas_call(
        body,
        name=name,
        in_specs=[HBM],
        out_specs=[SEM, SEM, HBM, VMEM_SPEC],
        out_shape=[pltpu.SemaphoreType.DMA(()), pltpu.SemaphoreType.DMA(()), pltpu.HBM(buf.shape, buf.dtype), TOKEN],
        input_output_aliases={0: 2},
        compiler_params=pltpu.CompilerParams(has_side_effects=EFFECT),
    )(_in_hbm(buf))


def _half_exchange_wait(send, recv, buf, after, name):
    rh = buf.shape[0] // 2

    def body(buf_ref, send_ref, recv_ref, *rest):
        x, y, c, _ = _place()
        cp = pltpu.make_async_remote_copy(src_ref=buf_ref.at[pl.ds(c * rh, rh), :], dst_ref=buf_ref.at[pl.ds((1 - c) * rh, rh), :],
                                          send_sem=send_ref, recv_sem=recv_ref, device_id=(x, y, 1 - c), device_id_type=MESH)
        cp.wait_send()
        cp.wait_recv()

    return pl.pallas_call(
        body,
        name=name,
        in_specs=[HBM, SEM, SEM] + [ANY] * len(after),
        out_specs=HBM,
        out_shape=pltpu.HBM(buf.shape, buf.dtype),
        input_output_aliases={0: 0},
        compiler_params=pltpu.CompilerParams(has_side_effects=EFFECT),
    )(buf, send, recv, *after)


def _half_exchange(bufs, name):
    n = len(bufs)

    def body(*refs):
        outs = refs[n:2 * n]
        send, recv = refs[2 * n:]
        x, y, c, _ = _place()
        copies = []
        for k in range(n):
            rh = bufs[k].shape[0] // 2
            mine = outs[k].at[pl.ds(c * rh, rh), :]
            cp = pltpu.make_async_remote_copy(src_ref=mine, dst_ref=mine, send_sem=send.at[k], recv_sem=recv.at[k],
                                              device_id=(x, y, 1 - c), device_id_type=MESH)
            cp.start()
            copies.append(cp)
        for k in range(n):
            rh = bufs[k].shape[0] // 2
            theirs = outs[k].at[pl.ds((1 - c) * rh, rh), :]
            copies[k].wait_send()
            pltpu.make_async_remote_copy(src_ref=theirs, dst_ref=theirs, send_sem=send.at[k], recv_sem=recv.at[k],
                                         device_id=(x, y, 1 - c), device_id_type=MESH).wait_recv()

    return pl.pallas_call(
        body,
        name=name,
        in_specs=[ANY] * n,
        out_specs=[ANY] * n,
        out_shape=[jax.ShapeDtypeStruct(a.shape, a.dtype) for a in bufs],
        input_output_aliases={k: k for k in range(n)},
        scratch_shapes=[pltpu.SemaphoreType.DMA((n,)), pltpu.SemaphoreType.DMA((n,))],
        compiler_params=pltpu.CompilerParams(has_side_effects=True),
    )(*bufs)


SMALL_ROWS = 32


def _small_peers():
    x, y, c, _ = _place()
    peers = []
    for k in range(1, N_DEV):
        px, py, pc = x ^ ((k >> 2) & 1), y ^ ((k >> 1) & 1), c ^ (k & 1)
        peers.append((k, (px, py, pc), 4 * px + 2 * py + pc))
    return 4 * x + 2 * y + c, peers


def _all_gather_small_start(gath):
    def body(in_ref, send, recv, out_ref, token):
        token[...] = jnp.zeros_like(token)
        me, peers = _small_peers()
        for k, peer, _ in peers:
            pltpu.make_async_remote_copy(src_ref=out_ref.at[me], dst_ref=out_ref.at[me], send_sem=send.at[k], recv_sem=recv.at[k],
                                         device_id=peer, device_id_type=MESH).start()

    return pl.pallas_call(
        body,
        name="small_gather_start",
        in_specs=[HBM],
        out_specs=[SEM, SEM, HBM, VMEM_SPEC],
        out_shape=[pltpu.SemaphoreType.DMA((N_DEV,)), pltpu.SemaphoreType.DMA((N_DEV,)), pltpu.HBM(gath.shape, gath.dtype), TOKEN],
        input_output_aliases={0: 2},
        compiler_params=pltpu.CompilerParams(has_side_effects=EFFECT),
    )(_in_hbm(gath))


def _all_gather_small_wait(send, recv, gath, after):
    def body(in_ref, send_ref, recv_ref, *rest):
        me, peers = _small_peers()
        for k, peer, peer_id in peers:
            cp = pltpu.make_async_remote_copy(src_ref=in_ref.at[me], dst_ref=in_ref.at[peer_id], send_sem=send_ref.at[k],
                                              recv_sem=recv_ref.at[k], device_id=peer, device_id_type=MESH)
            cp.wait_send()
            cp.wait_recv()

    return pl.pallas_call(
        body,
        name="small_gather_wait",
        in_specs=[HBM, SEM, SEM] + [ANY] * len(after),
        out_specs=HBM,
        out_shape=pltpu.HBM(gath.shape, gath.dtype),
        input_output_aliases={0: 0},
        compiler_params=pltpu.CompilerParams(has_side_effects=EFFECT),
    )(gath, send, recv, *after)


def _sum_small(gath):
    def body(g_ref, o_ref):
        acc = g_ref[0]
        for dev in range(1, N_DEV):
            acc = acc + g_ref[dev]
        o_ref[...] = acc

    return pl.pallas_call(
        body,
        name="small_sum",
        in_specs=[VMEM_SPEC],
        out_specs=VMEM_SPEC,
        out_shape=jax.ShapeDtypeStruct(gath.shape[1:], F32),
        compiler_params=pltpu.CompilerParams(vmem_limit_bytes=VMEM_LIMIT),
    )(gath)


def _adamw_update(g_ref, w_ref, m_ref, v_ref, go_ref, d_ref, mo_ref, vo_ref):
    bc1 = 1.0 - ADAM_B1 ** ADAM_STEP
    bc2 = 1.0 - ADAM_B2 ** ADAM_STEP
    gv = g_ref[...]
    mn = ADAM_B1 * m_ref[...] + (1.0 - ADAM_B1) * gv
    vn = ADAM_B2 * v_ref[...] + (1.0 - ADAM_B2) * (gv * gv)
    go_ref[...] = gv
    mo_ref[...] = mn
    vo_ref[...] = vn
    d_ref[...] = -ADAM_LR * ((mn / bc1) / (jnp.sqrt(vn / bc2) + ADAM_EPS) + ADAM_WD * w_ref[...])


def _adamw_small(gs, ws, ms, vs):
    n = len(gs)

    def body(*refs):
        ins, outs = refs[:4 * n], refs[4 * n:]
        for k in range(n):
            _adamw_update(*[ins[i * n + k] for i in range(4)], *outs[4 * k:4 * k + 4])

    vmem = pl.BlockSpec(memory_space=pltpu.VMEM)
    res = pl.pallas_call(
        body,
        name="adamw_small",
        in_specs=[vmem] * (4 * n),
        out_specs=[vmem] * (4 * n),
        out_shape=[jax.ShapeDtypeStruct(w.shape, F32) for w in ws for _ in range(4)],
        compiler_params=pltpu.CompilerParams(vmem_limit_bytes=VMEM_LIMIT),
    )(*gs, *ws, *ms, *vs)
    return [tuple(res[4 * k:4 * k + 4]) for k in range(n)]


def _adamw(g, w, m, v, name, g_block=0):
    r, cdim = w.shape
    tr = _row_tile(r, cdim, 4)

    def body(*refs):
        _adamw_update(*refs)

    spec = pl.BlockSpec((tr, cdim), lambda i: (i, 0))
    return pl.pallas_call(
        body,
        name=name,
        grid=(r // tr,),
        in_specs=[pl.BlockSpec((tr, cdim), lambda i: (i, g_block))] + [spec] * 3,
        out_specs=[spec] * 4,
        out_shape=[jax.ShapeDtypeStruct((r, cdim), F32)] * 4,
        compiler_params=_params(("parallel",)),
    )(g, w, m, v)


VEC_NAMES = ["pre_mix_norm", "q_norm", "kv_norm", "conv_b", "conv_ln_g", "conv_ln_b", "conv_out_norm",
             "attn_out_norm", "post_mix_norm", "pre_ffn_norm", "post_ffn_norm"]
LOSS_ROW = len(VEC_NAMES)
CONV_W_ROW = 16


def _cols_to_full(parts):
    _, r, cdim = parts.shape
    return parts.transpose(1, 0, 2).reshape(r, N_CHIPS * cdim)


def _full_to_cols(full):
    r, n = full.shape
    return full.reshape(r, N_CHIPS, n // N_CHIPS).transpose(1, 0, 2)


W_IN_SHARD = (2 * CONV_CH + Q_LORA + KV_LORA + QK_ROPE) // N_CHIPS
W_IN_PART = 1024
W_IN_BLOCKS = (2 * CONV_CH + Z2_COLS) // LANES
W_IN_BASE = [p * W_IN_SHARD // LANES for p in range(N_CHIPS)]
W_IN_SPAN = [-(-(p * W_IN_SHARD % LANES + W_IN_SHARD) // LANES) for p in range(N_CHIPS)]


def _w_in_block_home(b):
    n = CONV_CH // LANES
    if b < n:
        return 0, 2 * b
    if b < 2 * n:
        return 0, 2 * (b - n) + 1
    return 1, b - 2 * n


def _to_parts_w_in(shift_chip, w_in_t):
    r = w_in_t.shape[1]
    tr = 512

    def body(s_ref, a_ref, o_ref, scr):
        scr[...] = jnp.zeros_like(scr)
        scr[pl.ds(pl.multiple_of(s_ref[0], 8), W_IN_SHARD), :] = a_ref[...]
        o_ref[...] = scr[...].T.astype(o_ref.dtype)

    return pl.pallas_call(
        body,
        name="to_parts_w_in",
        grid_spec=pltpu.PrefetchScalarGridSpec(
            num_scalar_prefetch=1,
            grid=(r // tr,),
            in_specs=[pl.BlockSpec((W_IN_SHARD, tr), lambda i, s_ref: (0, i))],
            out_specs=pl.BlockSpec((None, tr, W_IN_PART), lambda i, s_ref: (s_ref[1], i, 0)),
            scratch_shapes=[pltpu.VMEM((W_IN_PART, tr), F32)]),
        out_shape=jax.ShapeDtypeStruct((N_CHIPS, r, W_IN_PART), BF16),
        compiler_params=_params(("parallel",)),
    )(shift_chip, w_in_t)


def _adamw_w_in(shift, g_shifted, w_t, m_t, v_t):
    cols = ROW_TILE

    def body(s_ref, g_ref, w_ref, m_ref, v_ref, go_ref, d_ref, mo_ref, vo_ref, scr):
        scr[...] = g_ref[...].T
        _adamw_update(scr.at[pl.ds(pl.multiple_of(s_ref[0], 8), W_IN_SHARD), :], w_ref, m_ref, v_ref, go_ref, d_ref, mo_ref, vo_ref)

    spec = pl.BlockSpec((W_IN_SHARD, cols), lambda i, s_ref: (0, i))
    return pl.pallas_call(
        body,
        name="adamw_w_in",
        grid_spec=pltpu.PrefetchScalarGridSpec(
            num_scalar_prefetch=1,
            grid=(w_t.shape[1] // cols,),
            in_specs=[pl.BlockSpec((cols, W_IN_PART), lambda i, s_ref: (i, 0))] + [spec] * 3,
            out_specs=[spec] * 4,
            scratch_shapes=[pltpu.VMEM((W_IN_PART, cols), F32)]),
        out_shape=[jax.ShapeDtypeStruct(w_t.shape, F32)] * 4,
        compiler_params=_params(("parallel",)),
    )(shift, g_shifted, w_t, m_t, v_t)


def _assemble_w_in(parts):
    r = parts.shape[1]
    tr = ROW_TILE

    def body(p_ref, ag_ref, z2_ref):
        outs = (ag_ref, z2_ref)
        for b in range(W_IN_BLOCKS):
            blk = None
            for p in range(N_CHIPS):
                i = b - W_IN_BASE[p]
                if 0 <= i < W_IN_SPAN[p]:
                    piece = p_ref[p, :, i * LANES:(i + 1) * LANES]
                    blk = piece if blk is None else blk + piece
            which, at = _w_in_block_home(b)
            outs[which][:, at * LANES:(at + 1) * LANES] = blk

    w_ag, w_z2 = pl.pallas_call(
        body,
        name="assemble_w_in",
        grid=(r // tr,),
        in_specs=[pl.BlockSpec((N_CHIPS, tr, W_IN_PART), lambda i: (0, i, 0))],
        out_specs=[pl.BlockSpec((tr, 2 * CONV_CH), lambda i: (i, 0)), pl.BlockSpec((tr, Z2_COLS), lambda i: (i, 0))],
        out_shape=[jax.ShapeDtypeStruct((r, 2 * CONV_CH), parts.dtype), jax.ShapeDtypeStruct((r, Z2_COLS), parts.dtype)],
        compiler_params=_params(("parallel",)),
    )(parts)
    return dict(w_ag=w_ag, w_z2=w_z2)


def _w_in_grad_parts(dw_ag, dw_z2):
    r = dw_ag.shape[0]
    tr = ROW_TILE

    def body(ag_ref, z2_ref, o_ref):
        ins = (ag_ref, z2_ref)
        for p in range(N_CHIPS):
            for i in range(W_IN_PART // LANES):
                if i < W_IN_SPAN[p]:
                    which, at = _w_in_block_home(W_IN_BASE[p] + i)
                    o_ref[p, :, i * LANES:(i + 1) * LANES] = ins[which][:, at * LANES:(at + 1) * LANES]
                else:
                    o_ref[p, :, i * LANES:(i + 1) * LANES] = jnp.zeros((tr, LANES), o_ref.dtype)

    return pl.pallas_call(
        body,
        name="w_in_grad_parts",
        grid=(r // tr,),
        in_specs=[pl.BlockSpec((tr, 2 * CONV_CH), lambda i: (i, 0)), pl.BlockSpec((tr, Z2_COLS), lambda i: (i, 0))],
        out_specs=pl.BlockSpec((N_CHIPS, tr, W_IN_PART), lambda i: (0, i, 0)),
        out_shape=jax.ShapeDtypeStruct((N_CHIPS, r, W_IN_PART), dw_ag.dtype),
        compiler_params=_params(("parallel",)),
    )(dw_ag, dw_z2)


def _assemble_mixer_rest(g):
    uq = _cols_to_full(g["w_uq"]).reshape(Q_LORA, N_HEADS, QK_HEAD)
    w_uq = jnp.concatenate([uq[:, :, :QK_NOPE].reshape(Q_LORA, N_HEADS * QK_NOPE),
                            uq[:, :, QK_NOPE:].reshape(Q_LORA, N_HEADS * QK_ROPE)], axis=1)
    return dict(w_uq=w_uq, w_ukv=_cols_to_full(g["w_ukv"]), conv_w=_cols_to_full(g["conv_w"]),
                w_out=g["w_out"].reshape(-1, g["w_out"].shape[2]))


def _grads_to_parts(dw):
    uq = dw["w_uq"]
    d_uq = jnp.concatenate([uq[:, :N_HEADS * QK_NOPE].reshape(Q_LORA, N_HEADS, QK_NOPE),
                            uq[:, N_HEADS * QK_NOPE:].reshape(Q_LORA, N_HEADS, QK_ROPE)], axis=2).reshape(Q_LORA, N_HEADS * QK_HEAD)
    return dict(w_in=_w_in_grad_parts(dw["w_ag"], dw["w_z2"]), w_uq=_full_to_cols(d_uq), w_ukv=_full_to_cols(dw["w_ukv"]),
                w_out=dw["w_out"].reshape(N_CHIPS, -1, dw["w_out"].shape[1]))


MIXER = ["w_in", "w_uq", "w_ukv", "w_out"]
FFN = ["w_gu", "w_down"]


def _pad_lanes(v, n):
    return jnp.pad(v, ((0, 0), (0, n - v.shape[1])))


def kernel(x, positions, pre_mix_norm, w_in, q_norm, w_uq, kv_norm, w_ukv, conv_w, conv_b, conv_ln_g, conv_ln_b, conv_out_norm, attn_out_norm, w_out, post_mix_norm, pre_ffn_norm, w_gate, w_up, w_down, post_ffn_norm, loss_target, m_pre_mix_norm, m_w_in, m_q_norm, m_w_uq, m_kv_norm, m_w_ukv, m_conv_w, m_conv_b, m_conv_ln_g, m_conv_ln_b, m_conv_out_norm, m_attn_out_norm, m_w_out, m_post_mix_norm, m_pre_ffn_norm, m_w_gate, m_w_up, m_w_down, m_post_ffn_norm, v_pre_mix_norm, v_w_in, v_q_norm, v_w_uq, v_kv_norm, v_w_ukv, v_conv_w, v_conv_b, v_conv_ln_g, v_conv_ln_b, v_conv_out_norm, v_attn_out_norm, v_w_out, v_post_mix_norm, v_pre_ffn_norm, v_w_gate, v_w_up, v_w_down, v_post_ffn_norm):
    given = dict(locals())
    names = ["pre_mix_norm", "w_in", "q_norm", "w_uq", "kv_norm", "w_ukv", "conv_w", "conv_b", "conv_ln_g", "conv_ln_b",
             "conv_out_norm", "attn_out_norm", "w_out", "post_mix_norm", "pre_ffn_norm", "w_gate", "w_up", "w_down", "post_ffn_norm"]
    def as_2d(a):
        return a if a.ndim == 2 else a[0]

    weights = {n: as_2d(given[n]) for n in names}
    mom = {n: as_2d(given["m_" + n]) for n in names}
    var = {n: as_2d(given["v_" + n]) for n in names}
    d = D_MODEL

    inv_freq = ROPE_THETA ** (-jnp.arange(0, QK_ROPE, 2, dtype=F32) / QK_ROPE)
    ang = positions[0].astype(F32)[:, None] * inv_freq
    cos, sin = jnp.cos(ang), jnp.sin(ang)
    cos2 = jnp.concatenate([cos, cos, cos, cos], axis=1)
    sin2 = jnp.concatenate([-sin, sin, -sin, sin], axis=1)

    chip = 2 * lax.axis_index("x") + lax.axis_index("y")
    core = lax.axis_index("c")
    chip1 = chip.astype(jnp.int32).reshape(1)
    pieces = {n: [weights[n]] for n in ("w_uq", "w_ukv", "w_out", "w_down")}
    pieces["w_gu"] = [weights["w_gate"], weights["w_up"]]
    core1 = core.astype(jnp.int32).reshape(1)
    place = jnp.stack([chip, core]).astype(jnp.int32)
    rest = ["w_uq", "w_ukv", "w_out"]
    w_in_shift = (chip * W_IN_SHARD) % LANES
    w_in_t, m_in_t, v_in_t = (jnp.swapaxes(given[n][0], 0, 1) for n in ("w_in", "m_w_in", "v_w_in"))
    w_in_buf = _to_parts_w_in(jnp.stack([w_in_shift, chip]).astype(jnp.int32), w_in_t)
    (w_in_sems,), w_in_thru, _ = _gather_start([w_in_buf], [False], [[0]], "gather_start_w_in")
    rest_bufs = [_to_parts(chip1, pieces[n], BF16, "to_parts_" + n) for n in rest]
    rest_bufs.append(_to_parts(chip1, [jnp.pad(weights["conv_w"], ((0, CONV_K_PAD - CONV_K), (0, 0)))], F32, "to_parts_conv_w"))
    rest_whole = [False] * 3 + [True]
    ffn_bufs = [_to_parts(chip1, pieces[n], BF16, "to_parts_" + n) for n in FFN]
    got = _gather_wait(w_in_thru, [False], *w_in_sems, rest_bufs + ffn_bufs, "gather_wait_w_in")
    (rest_sems,), rest_thru, started = _gather_start(rest_bufs, rest_whole, [[0, 1, 2, 3]], "gather_start_mixer_rest", after=got)
    full = _assemble_w_in(_gather_hand_on(got, "gather_hand_on_w_in", after=[started])[0])
    vec = {n: weights[n] for n in VEC_NAMES}
    rs = {}

    def get_mixer_rest(after):
        got = _gather_wait(rest_thru, rest_whole, *rest_sems, [after], "gather_wait_mixer_rest")
        (rs["w_gu_sems"],), rs["w_gu_thru"], started = _gather_start(ffn_bufs[:1], [False], [[0]], "gather_start_w_gu", after=got[:1])
        got = list(_gather_hand_on(got[:3], "gather_hand_on_mixer_rest", after=[started])) + [got[3]]
        return _assemble_mixer_rest(dict(zip(rest + ["conv_w"], got)))

    def w_gu_landed(after):
        got = _gather_wait(rs["w_gu_thru"], [False], *rs["w_gu_sems"], [after], "gather_wait_w_gu")
        (rs["w_down_sems"],), rs["w_down_thru"], started = _gather_start(ffn_bufs[1:], [False], [[0]], "gather_start_w_down", after=got)
        rs["w_gu"] = _hand_on_start(got, "hand_on_start_w_gu", after=[started])
        return rs["w_gu"][3]

    def get_w_gu(after):
        send, recv, bufs, _ = rs["w_gu"]
        return _hand_on_wait(bufs, send, recv, [after], "hand_on_wait_w_gu")[0]

    def get_w_down(after):
        got = _gather_wait(rs["w_down_thru"], [False], *rs["w_down_sems"], [after], "gather_wait_w_down")
        got = _gather_hand_on(got, "gather_hand_on_w_down")[0]
        return got.reshape(-1, got.shape[2])

    def pair_start(key):
        def hook(dw):
            rs[key] = _pair_exchange_start([dw.reshape(N_CHIPS, -1, dw.shape[-1])], "grad_pair_start_" + key)
            return rs[key][4]
        return hook

    def reduce_start(group, plist, landed):
        sums = [_pair_sum(core1, a, b, "grad_pair_sum_%s_%d" % (group, k)) for k, (a, b) in enumerate(zip(plist, landed))]
        rs[group] = _chip_exchange_start(sums, "grad_chip_exchange_start_" + group)
        return rs[group][4]

    def reduce_finish(group, after):
        send, recv, sums, slots, _ = rs[group]
        sums, slots = _chip_exchange_wait(sums, slots, send, recv, after, "grad_chip_exchange_wait_" + group)
        halves = [_chip_sum(place, s, sl, "grad_chip_sum_%s_%d" % (group, k)) for k, (s, sl) in enumerate(zip(sums, slots))]
        return list(_half_exchange(halves, "grad_half_exchange_" + group))

    def ffn_grads_exchanged(after):
        pairs = [_pair_exchange_wait(*rs[key][:4], [after], "grad_pair_wait_" + key) for key in ("dw_gu", "dw_down")]
        return reduce_start("ffn", [p[0][0] for p in pairs], [p[1][0] for p in pairs])

    def mixer_grads(dw_mixer):
        parts = _grads_to_parts(dw_mixer)
        rs["mixer_pairs"] = _pair_exchange_start([parts[n] for n in MIXER], "grad_pair_start_mixer")
        return rs["mixer_pairs"][4]

    hooks = dict(mixer_rest=get_mixer_rest, w_gu_landed=w_gu_landed, w_gu=get_w_gu, w_down=get_w_down, mixer_grads=mixer_grads, dw_down=pair_start("dw_down"), dw_gu=pair_start("dw_gu"),
                 grads_exchanged=ffn_grads_exchanged)
    loss, grad_x, dw, dvec = _local_step(x[0], loss_target[0], cos2, sin2, vec, full, hooks)

    rows = [_pad_lanes(dvec[n], d) for n in VEC_NAMES] + [_pad_lanes(loss, d)]
    rows.append(jnp.zeros((CONV_W_ROW - len(rows), d), F32))
    rows.append(dw["conv_w"].reshape(SMALL_ROWS - CONV_W_ROW, d))
    device1 = (2 * chip + core).astype(jnp.int32).reshape(1)
    small_gather = _all_gather_small_start(_to_parts(device1, [jnp.concatenate(rows, axis=0)], F32, "small_to_slot", n_parts=N_DEV))

    plist, landed = _pair_exchange_wait(*rs["mixer_pairs"][:4], [grad_x, small_gather[3]], "grad_pair_wait_mixer")
    started = reduce_start("mixer", plist, landed)
    send, recv, sums, slots, _ = rs["ffn"]
    sums, slots = _chip_exchange_wait(sums, slots, send, recv, [started, small_gather[3]], "grad_chip_exchange_wait_ffn")
    down = _half_exchange_start(_chip_sum(place, sums[1], slots[1], "grad_chip_sum_ffn_1"), "grad_half_start_w_down")
    gu = _half_exchange_start(_chip_sum(place, sums[0], slots[0], "grad_chip_sum_ffn_0", after=[down[3]]), "grad_half_start_w_gu")
    res = {}
    g_down = _half_exchange_wait(*down[:3], [gu[3]], "grad_half_wait_w_down")
    res["w_down"] = _adamw(g_down, weights["w_down"], mom["w_down"], var["w_down"], "adamw_w_down")
    g_gu = _half_exchange_wait(*gu[:3], [res["w_down"][1]], "grad_half_wait_w_gu")
    for n, blk in (("w_gate", 0), ("w_up", 1)):
        res[n] = _adamw(g_gu, weights[n], mom[n], var[n], "adamw_" + n, g_block=blk)
    small = _sum_small(_all_gather_small_wait(*small_gather[:3], [res["w_up"][1]]))
    g_conv_w_full = small[CONV_W_ROW:].reshape(CONV_K_PAD, CONV_CH)
    g_small = {n: small[i:i + 1, :weights[n].shape[1]] for i, n in enumerate(VEC_NAMES)}
    g_small["conv_w"] = lax.dynamic_slice(g_conv_w_full, (0, chip * (CONV_CH // N_CHIPS)), (CONV_K_PAD, CONV_CH // N_CHIPS))[:CONV_K]
    loss_out = small[LOSS_ROW, 0]
    small_names = VEC_NAMES + ["conv_w"]
    res.update(zip(small_names, _adamw_small([g_small[n] for n in small_names], [weights[n] for n in small_names],
                                             [mom[n] for n in small_names], [var[n] for n in small_names])))
    done_meanwhile = [res["w_gate"][1], res["w_up"][1], res["w_down"][1], res["conv_w"][1], grad_x]
    g_mixer = reduce_finish("mixer", done_meanwhile)
    in_t = _adamw_w_in(w_in_shift.astype(jnp.int32).reshape(1), g_mixer[0], w_in_t, m_in_t, v_in_t)
    res["w_in"] = tuple(jnp.swapaxes(a, 0, 1) for a in in_t)
    for n, g in zip(MIXER[1:], g_mixer[1:]):
        res[n] = _adamw(g, weights[n], mom[n], var[n], "adamw_" + n)
    outs = [loss_out, grad_x[None]]
    for i in range(4):
        outs += [res[n][i].reshape(given[n].shape) for n in names]
    return tuple(outs)
```
